```python
import jax, jax.numpy as jnp
from jax import lax
import numpy as np

D_MODEL = 1024
BATCH = 8
SEQ = 8192
DEPTH = 1

CHUNK = 64
D_MIX = D_MODEL
HEAD_DIM = 64
N_HEADS_A = 8
N_HEADS_B = 8
D_A = N_HEADS_A * HEAD_DIM
D_B = N_HEADS_B * HEAD_DIM
SG_BLOCK = 128
Q_BLOCK = 128
D_FF = 2816
CONV_W = 3
EPS = 1e-6
D_IN = 2 * D_A + 3 * D_B + N_HEADS_B

kernel_name = "hymba_gmlp_fox_convffn"


def rms_norm(x, g):
    xf = x.astype(jnp.float32)
    y = xf * lax.rsqrt(jnp.mean(xf * xf, axis=-1, keepdims=True) + EPS)
    return (y * g.astype(jnp.float32)).astype(x.dtype)


def spatial_gating(u, v, ln_g, w_s, b_s):
    B, S, _ = u.shape
    n = S // SG_BLOCK
    v = v.reshape(B, n, SG_BLOCK, N_HEADS_A, HEAD_DIM)
    vf = v.astype(jnp.float32)
    mu = jnp.mean(vf, axis=-1, keepdims=True)
    var = jnp.mean(jnp.square(vf - mu), axis=-1, keepdims=True)
    vn = ((vf - mu) * lax.rsqrt(var + EPS) * ln_g.astype(jnp.float32)).astype(u.dtype)
    pos_chunk = jnp.arange(SG_BLOCK) // CHUNK
    mask = pos_chunk[:, None] >= pos_chunk[None, :]
    w = jnp.where(mask[None], w_s, 0).astype(u.dtype)
    mixed = jnp.einsum('hts,bnshd->bnthd', w, vn) + b_s.T.astype(u.dtype)[None, None, :, :, None]
    out = u.reshape(B, n, SG_BLOCK, N_HEADS_A, HEAD_DIM) * mixed
    return out.reshape(B, S, D_A)


def forgetting_attention(q, k, v, f_logit):
    B, S, H, Dh = q.shape
    n = S // Q_BLOCK
    scale = Dh ** -0.5
    c = jnp.cumsum(jax.nn.log_sigmoid(f_logit.astype(jnp.float32)), axis=1)
    c = c.transpose(0, 2, 1)
    qb = q.reshape(B, n, Q_BLOCK, H, Dh).transpose(1, 0, 3, 2, 4)
    cb = c.reshape(B, H, n, Q_BLOCK).transpose(2, 0, 1, 3)
    kpos = jnp.arange(S)

    def block(args):
        i, qi, ci = args
        s = jnp.einsum('bhtd,bshd->bhts', qi, k, preferred_element_type=jnp.float32) * scale
        s = s + ci[..., :, None] - c[:, :, None, :]
        qpos = i * Q_BLOCK + jnp.arange(Q_BLOCK)
        s = jnp.where(kpos[None, :] <= qpos[:, None], s, -jnp.inf)
        p = jax.nn.softmax(s, axis=-1)
        return jnp.einsum('bhts,bshd->bthd', p.astype(v.dtype), v)

    out = lax.map(block, (jnp.arange(n), qb, cb))
    return out.transpose(1, 0, 2, 3, 4).reshape(B, S, H * Dh)


def conv_ffn(h, w_up, w_conv, b_conv, w_down):
    a = h @ w_up
    C = a.shape[-1]
    a = lax.conv_general_dilated(
        a, w_conv[:, None, :].astype(a.dtype), window_strides=(1,),
        padding=[(CONV_W - 1, 0)], dimension_numbers=('NWC', 'WIO', 'NWC'),
        feature_group_count=C) + b_conv.astype(a.dtype)
    g, val = jnp.split(a, 2, axis=-1)
    return (jax.nn.silu(g) * val) @ w_down


def _fwd_setup_inputs(seed: int = 0) -> dict:
    key = jax.random.key(seed)
    ks = jax.random.split(key, 14)
    L = DEPTH
    nrm = jax.random.normal
    return {
        "x": nrm(ks[0], (BATCH, SEQ, D_MODEL), jnp.float32),
        "norm_mix_g": 1.0 + 0.02 * nrm(ks[1], (L, D_MODEL), jnp.float32),
        "w_in": nrm(ks[2], (L, D_MODEL, D_IN), jnp.float32) * D_MODEL ** -0.5,
        "f_bias": 3.0 + 0.5 * nrm(ks[3], (L, N_HEADS_B), jnp.float32),
        "sg_ln_g": 1.0 + 0.02 * nrm(ks[4], (L, N_HEADS_A, HEAD_DIM), jnp.float32),
        "sg_w": nrm(ks[5], (L, N_HEADS_A, SG_BLOCK, SG_BLOCK), jnp.float32) * SG_BLOCK ** -0.5,
        "sg_b": 1.0 + 0.1 * nrm(ks[6], (L, N_HEADS_A, SG_BLOCK), jnp.float32),
        "w_out": nrm(ks[7], (L, D_MIX, D_MODEL), jnp.float32) * D_MIX ** -0.5,
        "norm_ffn_g": 1.0 + 0.02 * nrm(ks[8], (L, D_MODEL), jnp.float32),
        "w_up": nrm(ks[9], (L, D_MODEL, 2 * D_FF), jnp.float32) * D_MODEL ** -0.5,
        "w_conv": nrm(ks[10], (L, CONV_W, 2 * D_FF), jnp.float32) * CONV_W ** -0.5,
        "b_conv": 0.02 * nrm(ks[11], (L, 2 * D_FF), jnp.float32),
        "w_down": nrm(ks[12], (L, D_FF, D_MODEL), jnp.float32) * D_FF ** -0.5,
        "norm_final_g": 1.0 + 0.02 * nrm(ks[13], (D_MODEL,), jnp.float32),
    }


def _fwd_reference(x, norm_mix_g, w_in, f_bias, sg_ln_g, sg_w, sg_b, w_out,
              norm_ffn_g, w_up, w_conv, b_conv, w_down, norm_final_g):
    B, S, _ = x.shape
    for l in range(DEPTH):
        h = rms_norm(x, norm_mix_g[l])
        z = h @ w_in[l]
        o = 0
        u_a = jax.nn.gelu(z[..., o:o + D_A], approximate=False); o += D_A
        v_a = jax.nn.gelu(z[..., o:o + D_A], approximate=False); o += D_A
        q_b = z[..., o:o + D_B].reshape(B, S, N_HEADS_B, HEAD_DIM); o += D_B
        k_b = z[..., o:o + D_B].reshape(B, S, N_HEADS_B, HEAD_DIM); o += D_B
        v_b = z[..., o:o + D_B].reshape(B, S, N_HEADS_B, HEAD_DIM); o += D_B
        f_logit = z[..., o:o + N_HEADS_B] + f_bias[l].astype(z.dtype)
        out_a = spatial_gating(u_a, v_a, sg_ln_g[l], sg_w[l], sg_b[l])
        out_b = forgetting_attention(q_b, k_b, v_b, f_logit)
        x = x + jnp.concatenate([out_a, out_b], axis=-1) @ w_out[l]
        x = x + conv_ffn(rms_norm(x, norm_ffn_g[l]), w_up[l], w_conv[l], b_conv[l], w_down[l])
    return rms_norm(x, norm_final_g)


import jax as _jax
import jax.numpy as _jnp

TWIN_FORMAT = 'train_step'
FWD_PARAMS = ['x', 'norm_mix_g', 'w_in', 'f_bias', 'sg_ln_g', 'sg_w', 'sg_b', 'w_out', 'norm_ffn_g', 'w_up', 'w_conv', 'b_conv', 'w_down', 'norm_final_g']
TWIN_WEIGHTS = ['norm_mix_g', 'w_in', 'f_bias', 'sg_ln_g', 'sg_w', 'sg_b', 'w_out', 'norm_ffn_g', 'w_up', 'w_conv', 'b_conv', 'w_down', 'norm_final_g']
TWIN_DIFF_INPUT = 'x'
TWIN_INPUTS = ['x', 'norm_mix_g', 'w_in', 'f_bias', 'sg_ln_g', 'sg_w', 'sg_b', 'w_out', 'norm_ffn_g', 'w_up', 'w_conv', 'b_conv', 'w_down', 'norm_final_g', 'loss_target', 'm_norm_mix_g', 'm_w_in', 'm_f_bias', 'm_sg_ln_g', 'm_sg_w', 'm_sg_b', 'm_w_out', 'm_norm_ffn_g', 'm_w_up', 'm_w_conv', 'm_b_conv', 'm_w_down', 'm_norm_final_g', 'v_norm_mix_g', 'v_w_in', 'v_f_bias', 'v_sg_ln_g', 'v_sg_w', 'v_sg_b', 'v_w_out', 'v_norm_ffn_g', 'v_w_up', 'v_w_conv', 'v_b_conv', 'v_w_down', 'v_norm_final_g']
TWIN_OUTPUTS = ['loss', 'grad_x', 'grad_norm_mix_g', 'grad_w_in', 'grad_f_bias', 'grad_sg_ln_g', 'grad_sg_w', 'grad_sg_b', 'grad_w_out', 'grad_norm_ffn_g', 'grad_w_up', 'grad_w_conv', 'grad_b_conv', 'grad_w_down', 'grad_norm_final_g', 'delta_norm_mix_g', 'delta_w_in', 'delta_f_bias', 'delta_sg_ln_g', 'delta_sg_w', 'delta_sg_b', 'delta_w_out', 'delta_norm_ffn_g', 'delta_w_up', 'delta_w_conv', 'delta_b_conv', 'delta_w_down', 'delta_norm_final_g', 'new_m_norm_mix_g', 'new_m_w_in', 'new_m_f_bias', 'new_m_sg_ln_g', 'new_m_sg_w', 'new_m_sg_b', 'new_m_w_out', 'new_m_norm_ffn_g', 'new_m_w_up', 'new_m_w_conv', 'new_m_b_conv', 'new_m_w_down', 'new_m_norm_final_g', 'new_v_norm_mix_g', 'new_v_w_in', 'new_v_f_bias', 'new_v_sg_ln_g', 'new_v_sg_w', 'new_v_sg_b', 'new_v_w_out', 'new_v_norm_ffn_g', 'new_v_w_up', 'new_v_w_conv', 'new_v_b_conv', 'new_v_w_down', 'new_v_norm_final_g']
TWIN_LEAF_KINDS = {'loss': 'loss', 'grad_x': 'grad_x', 'grad_norm_mix_g': 'grad_w', 'grad_w_in': 'grad_w', 'grad_f_bias': 'grad_w', 'grad_sg_ln_g': 'grad_w', 'grad_sg_w': 'grad_w', 'grad_sg_b': 'grad_w', 'grad_w_out': 'grad_w', 'grad_norm_ffn_g': 'grad_w', 'grad_w_up': 'grad_w', 'grad_w_conv': 'grad_w', 'grad_b_conv': 'grad_w', 'grad_w_down': 'grad_w', 'grad_norm_final_g': 'grad_w', 'delta_norm_mix_g': 'delta_w', 'delta_w_in': 'delta_w', 'delta_f_bias': 'delta_w', 'delta_sg_ln_g': 'delta_w', 'delta_sg_w': 'delta_w', 'delta_sg_b': 'delta_w', 'delta_w_out': 'delta_w', 'delta_norm_ffn_g': 'delta_w', 'delta_w_up': 'delta_w', 'delta_w_conv': 'delta_w', 'delta_b_conv': 'delta_w', 'delta_w_down': 'delta_w', 'delta_norm_final_g': 'delta_w', 'new_m_norm_mix_g': 'new_m', 'new_m_w_in': 'new_m', 'new_m_f_bias': 'new_m', 'new_m_sg_ln_g': 'new_m', 'new_m_sg_w': 'new_m', 'new_m_sg_b': 'new_m', 'new_m_w_out': 'new_m', 'new_m_norm_ffn_g': 'new_m', 'new_m_w_up': 'new_m', 'new_m_w_conv': 'new_m', 'new_m_b_conv': 'new_m', 'new_m_w_down': 'new_m', 'new_m_norm_final_g': 'new_m', 'new_v_norm_mix_g': 'new_v', 'new_v_w_in': 'new_v', 'new_v_f_bias': 'new_v', 'new_v_sg_ln_g': 'new_v', 'new_v_sg_w': 'new_v', 'new_v_sg_b': 'new_v', 'new_v_w_out': 'new_v', 'new_v_norm_ffn_g': 'new_v', 'new_v_w_up': 'new_v', 'new_v_w_conv': 'new_v', 'new_v_b_conv': 'new_v', 'new_v_w_down': 'new_v', 'new_v_norm_final_g': 'new_v'}


def _forward(args):
    return _fwd_reference(*[args[k] for k in FWD_PARAMS])


def _output_shape():
    def fwd():
        inp = _fwd_setup_inputs(0)
        return _fwd_reference(*[inp[k] for k in FWD_PARAMS])
    out = _jax.eval_shape(fwd)
    return out.shape, out.dtype

N_MICROBATCH = 1
ADAM_LR = 0.001
ADAM_B1 = 0.9
ADAM_B2 = 0.999
ADAM_EPS = 1e-08
ADAM_WD = 0.01
ADAM_STEP = 10
PER_EXAMPLE_BATCH_AXIS = {'x': 0, 'loss_target': 0}
SHARED_INPUTS = []
_WEIGHT_DTYPES = {'norm_mix_g': _jnp.float32, 'w_in': _jnp.float32, 'f_bias': _jnp.float32, 'sg_ln_g': _jnp.float32, 'sg_w': _jnp.float32, 'sg_b': _jnp.float32, 'w_out': _jnp.float32, 'norm_ffn_g': _jnp.float32, 'w_up': _jnp.float32, 'w_conv': _jnp.float32, 'b_conv': _jnp.float32, 'w_down': _jnp.float32, 'norm_final_g': _jnp.float32}
MOMENT_SCALE = {'norm_mix_g': 2.059088e-01, 'w_in': 1.253334e-01, 'f_bias': 3.613044e-01, 'sg_ln_g': 1.265416e-01, 'sg_w': 9.195336e-02, 'sg_b': 1.111766e-01, 'w_out': 1.550333e-01, 'norm_ffn_g': 1.787076e-01, 'w_up': 7.076837e-02, 'w_conv': 6.980763e-02, 'b_conv': 7.478864e-02, 'w_down': 1.157204e-01, 'norm_final_g': 6.408136e+01}


def _to_microbatches(a, axis):
    t = _jnp.moveaxis(a, axis, 0)
    t = t.reshape((N_MICROBATCH, t.shape[0] // N_MICROBATCH) + t.shape[1:])
    return _jnp.moveaxis(t, 1, axis + 1)


def setup_inputs(seed: int = 0) -> dict:
    inp = _fwd_setup_inputs(seed)
    key = _jax.random.fold_in(_jax.random.key(seed), 7919)
    shape, _ = _output_shape()
    out = dict(inp)
    out["loss_target"] = _jax.random.normal(_jax.random.fold_in(key, 0), shape, _jnp.float32)
    for i, name in enumerate(TWIN_WEIGHTS):
        w = inp[name].astype(_jnp.float32)
        if MOMENT_SCALE is None:
            s = _jnp.sqrt(_jnp.mean(_jnp.square(w)) + 1e-30)
        else:
            s = MOMENT_SCALE[name]
        km, kv = _jax.random.split(_jax.random.fold_in(key, i + 1))
        out[name] = w
        out["m_" + name] = s * _jax.random.normal(km, w.shape, _jnp.float32)
        out["v_" + name] = (s * s) * _jax.random.uniform(kv, w.shape, _jnp.float32, 0.5, 1.5)
    if N_MICROBATCH > 1:
        for name, axis in PER_EXAMPLE_BATCH_AXIS.items():
            out[name] = _to_microbatches(out[name], axis)
    return {'x': out['x'], 'norm_mix_g': out['norm_mix_g'], 'w_in': out['w_in'], 'f_bias': out['f_bias'], 'sg_ln_g': out['sg_ln_g'], 'sg_w': out['sg_w'], 'sg_b': out['sg_b'], 'w_out': out['w_out'], 'norm_ffn_g': out['norm_ffn_g'], 'w_up': out['w_up'], 'w_conv': out['w_conv'], 'b_conv': out['b_conv'], 'w_down': out['w_down'], 'norm_final_g': out['norm_final_g'], 'loss_target': out['loss_target'], 'm_norm_mix_g': out['m_norm_mix_g'], 'm_w_in': out['m_w_in'], 'm_f_bias': out['m_f_bias'], 'm_sg_ln_g': out['m_sg_ln_g'], 'm_sg_w': out['m_sg_w'], 'm_sg_b': out['m_sg_b'], 'm_w_out': out['m_w_out'], 'm_norm_ffn_g': out['m_norm_ffn_g'], 'm_w_up': out['m_w_up'], 'm_w_conv': out['m_w_conv'], 'm_b_conv': out['m_b_conv'], 'm_w_down': out['m_w_down'], 'm_norm_final_g': out['m_norm_final_g'], 'v_norm_mix_g': out['v_norm_mix_g'], 'v_w_in': out['v_w_in'], 'v_f_bias': out['v_f_bias'], 'v_sg_ln_g': out['v_sg_ln_g'], 'v_sg_w': out['v_sg_w'], 'v_sg_b': out['v_sg_b'], 'v_w_out': out['v_w_out'], 'v_norm_ffn_g': out['v_norm_ffn_g'], 'v_w_up': out['v_w_up'], 'v_w_conv': out['v_w_conv'], 'v_b_conv': out['v_b_conv'], 'v_w_down': out['v_w_down'], 'v_norm_final_g': out['v_norm_final_g']}


def _loss(weights, diff, rest, loss_target):
    with _jax.named_scope("forward"):
        args = {**rest, TWIN_DIFF_INPUT: diff, **{k: w.astype(_WEIGHT_DTYPES[k]) for k, w in weights.items()}}
        y = _forward(args)
    with _jax.named_scope("loss_head"):
        err = _jnp.square(y.astype(_jnp.float32) - loss_target)
        return 0.5 * _jnp.sum(_jnp.mean(err, axis=-1)) if err.ndim else 0.5 * err


def _adamw(w, g, m, v):
    m = ADAM_B1 * m + (1.0 - ADAM_B1) * g
    v = ADAM_B2 * v + (1.0 - ADAM_B2) * _jnp.square(g)
    m_hat = m / (1.0 - ADAM_B1 ** ADAM_STEP)
    v_hat = v / (1.0 - ADAM_B2 ** ADAM_STEP)
    delta = -ADAM_LR * (m_hat / (_jnp.sqrt(v_hat) + ADAM_EPS) + ADAM_WD * w)
    return delta, m, v


def reference(x, norm_mix_g, w_in, f_bias, sg_ln_g, sg_w, sg_b, w_out, norm_ffn_g, w_up, w_conv, b_conv, w_down, norm_final_g, loss_target, m_norm_mix_g, m_w_in, m_f_bias, m_sg_ln_g, m_sg_w, m_sg_b, m_w_out, m_norm_ffn_g, m_w_up, m_w_conv, m_b_conv, m_w_down, m_norm_final_g, v_norm_mix_g, v_w_in, v_f_bias, v_sg_ln_g, v_sg_w, v_sg_b, v_w_out, v_norm_ffn_g, v_w_up, v_w_conv, v_b_conv, v_w_down, v_norm_final_g):
    given = dict(x=x, norm_mix_g=norm_mix_g, w_in=w_in, f_bias=f_bias, sg_ln_g=sg_ln_g, sg_w=sg_w, sg_b=sg_b, w_out=w_out, norm_ffn_g=norm_ffn_g, w_up=w_up, w_conv=w_conv, b_conv=b_conv, w_down=w_down, norm_final_g=norm_final_g, loss_target=loss_target, m_norm_mix_g=m_norm_mix_g, m_w_in=m_w_in, m_f_bias=m_f_bias, m_sg_ln_g=m_sg_ln_g, m_sg_w=m_sg_w, m_sg_b=m_sg_b, m_w_out=m_w_out, m_norm_ffn_g=m_norm_ffn_g, m_w_up=m_w_up, m_w_conv=m_w_conv, m_b_conv=m_b_conv, m_w_down=m_w_down, m_norm_final_g=m_norm_final_g, v_norm_mix_g=v_norm_mix_g, v_w_in=v_w_in, v_f_bias=v_f_bias, v_sg_ln_g=v_sg_ln_g, v_sg_w=v_sg_w, v_sg_b=v_sg_b, v_w_out=v_w_out, v_norm_ffn_g=v_norm_ffn_g, v_w_up=v_w_up, v_w_conv=v_w_conv, v_b_conv=v_b_conv, v_w_down=v_w_down, v_norm_final_g=v_norm_final_g)
    weights = {n: given[n] for n in TWIN_WEIGHTS}
    shared = {n: given[n] for n in SHARED_INPUTS}
    per_example = {n: given[n] for n in ['x']}
    grad_fn = _jax.value_and_grad(_loss, argnums=(0, 1))

    def one_microbatch(ex, loss_target):
        ex = dict(ex)
        diff = ex.pop(TWIN_DIFF_INPUT)
        return grad_fn(weights, diff, {**shared, **ex}, loss_target)

    if N_MICROBATCH == 1:
        loss, (grad_w, grad_x) = one_microbatch(per_example, given["loss_target"])
    else:
        def body(carry, xs):
            loss_sum, grad_sum = carry
            l_k, (gw_k, gx_k) = one_microbatch(xs[0], xs[1])
            with _jax.named_scope("update"):
                return (loss_sum + l_k, _jax.tree.map(_jnp.add, grad_sum, gw_k)), gx_k

        init = (_jnp.zeros((), _jnp.float32), _jax.tree.map(_jnp.zeros_like, weights))
        (loss, grad_w), grad_x = _jax.lax.scan(body, init, (per_example, given["loss_target"]))
    with _jax.named_scope("update"):
        delta_w, new_m, new_v = {}, {}, {}
        for n in TWIN_WEIGHTS:
            delta_w[n], new_m[n], new_v[n] = _adamw(weights[n], grad_w[n], given["m_" + n], given["v_" + n])
    return (loss, grad_x, *[grad_w[n] for n in TWIN_WEIGHTS], *[delta_w[n] for n in TWIN_WEIGHTS],
            *[new_m[n] for n in TWIN_WEIGHTS], *[new_v[n] for n in TWIN_WEIGHTS])
```

```python
import functools
import math

import jax
import jax.numpy as jnp
from jax import lax
from jax.experimental import pallas as pl
from jax.experimental.pallas import tpu as pltpu

F32 = jnp.float32
BF16 = jnp.bfloat16

D_MODEL = 1024
HEAD_DIM = 64
N_HEADS = 8
D_A = 512
D_B = 512
D_IN = 2 * D_A + 3 * D_B + N_HEADS
D_IN_PAD = 2688
D_FF = 2816
D_FF2 = 2 * D_FF
SG_BLOCK = 128
SG_CHUNK = 64
EPS = 1e-6
N_DEV = 8
LANES = 128
NEG = -1e30
VMEM_LIMIT = 56 * 1024 * 1024

ADAM_LR = 0.001
ADAM_B1 = 0.9
ADAM_B2 = 0.999
ADAM_EPS = 1e-08
ADAM_WD = 0.01
ADAM_STEP = 10

TM = 256
TQ = 512
CW = 256

MESH_AXES = ("x", "y", "c")
ANY = pl.BlockSpec(memory_space=pl.ANY)


def _cp(*sem):
    return pltpu.CompilerParams(dimension_semantics=sem, vmem_limit_bytes=VMEM_LIMIT)


def _dot(a, b):
    return jnp.dot(a, b, preferred_element_type=F32)


def _dot_nt(a, b):
    return lax.dot_general(a, b, (((1,), (1,)), ((), ())), preferred_element_type=F32)


def _dot_tn(a, b):
    return lax.dot_general(a, b, (((0,), (0,)), ((), ())), preferred_element_type=F32)


def _dot_f32(a, b):
    return jnp.dot(a, b, precision=lax.Precision.HIGHEST, preferred_element_type=F32)


def _gelu(z):
    return 0.5 * z * (1.0 + lax.erf(z * (1.0 / math.sqrt(2.0))))


def _gelu_grad(z):
    return 0.5 * (1.0 + lax.erf(z * (1.0 / math.sqrt(2.0)))) + z * jnp.exp(-0.5 * z * z) * (1.0 / math.sqrt(2.0 * math.pi))


def _log_sigmoid(x):
    return jnp.minimum(x, 0.0) - jnp.log1p(jnp.exp(-jnp.abs(x)))


def _rms(x):
    r = lax.rsqrt(jnp.mean(x * x, axis=-1, keepdims=True) + EPS)
    return r, x * r


def _rms_bwd(dy, n, r, g):
    dn = dy * g
    return r * (dn - n * jnp.mean(dn * n, axis=-1, keepdims=True))


def _full(shape, single=False):
    nd = len(shape)
    if single:
        return pl.BlockSpec(shape, lambda *_: (0,) * nd, pipeline_mode=pl.Buffered(1))
    return pl.BlockSpec(shape, lambda *_: (0,) * nd)


def _rows(tm, cols, rev_nt=None):
    if rev_nt is None:
        return pl.BlockSpec((tm, cols), lambda i: (i, 0))
    return pl.BlockSpec((tm, cols), lambda i: (rev_nt - 1 - i, 0))


def _head_sl(h):
    return slice(HEAD_DIM * h, HEAD_DIM * (h + 1))


def _fwd_in(x, g1, w_in_p, fb):
    T = x.shape[0]
    tm = min(TM, T)

    def body(x_ref, g_ref, w_ref, fb_ref, zuv_ref, q_ref, k_ref, v_ref, fl_ref, c_ref, h1_ref, carry):
        @pl.when(pl.program_id(0) == 0)
        def _():
            carry[...] = jnp.zeros_like(carry)

        r, n = _rms(x_ref[...])
        h = (n * g_ref[...]).astype(BF16)
        h1_ref[...] = h
        z = _dot(h, w_ref[...])
        zuv_ref[...] = z[:, :2 * D_A]
        o = 2 * D_A
        q_ref[...] = (z[:, o:o + D_B] * (HEAD_DIM ** -0.5)).astype(BF16)
        k_ref[...] = z[:, o + D_B:o + 2 * D_B].astype(BF16)
        v_ref[...] = z[:, o + 2 * D_B:o + 3 * D_B].astype(BF16)
        fl = z[:, o + 3 * D_B:] + fb_ref[...]
        fl_ref[...] = fl
        tri = (lax.broadcasted_iota(jnp.int32, (tm, tm), 0) >= lax.broadcasted_iota(jnp.int32, (tm, tm), 1)).astype(F32)
        c = _dot_f32(tri, _log_sigmoid(fl)) + carry[...]
        c_ref[...] = c
        carry[...] = c[tm - 1:tm, :]

    return pl.pallas_call(
        body, name="fwd_in", grid=(T // tm,),
        in_specs=[_rows(tm, D_MODEL), _full((1, D_MODEL)), _full((D_MODEL, D_IN_PAD), True), _full((1, LANES))],
        out_specs=[_rows(tm, 2 * D_A), _rows(tm, D_B), _rows(tm, D_B), _rows(tm, D_B), _rows(tm, LANES), _rows(tm, LANES),
                   _rows(tm, D_MODEL)],
        out_shape=[jax.ShapeDtypeStruct((T, 2 * D_A), F32), jax.ShapeDtypeStruct((T, D_B), BF16),
                   jax.ShapeDtypeStruct((T, D_B), BF16), jax.ShapeDtypeStruct((T, D_B), BF16),
                   jax.ShapeDtypeStruct((T, LANES), F32), jax.ShapeDtypeStruct((T, LANES), F32),
                   jax.ShapeDtypeStruct((T, D_MODEL), BF16)],
        scratch_shapes=[pltpu.VMEM((1, LANES), F32)],
        compiler_params=_cp("arbitrary"),
    )(x, g1, w_in_p, fb)


def _ln_head(vh, lgh):
    mu = jnp.mean(vh, axis=-1, keepdims=True)
    d = vh - mu
    rstd = lax.rsqrt(jnp.mean(d * d, axis=-1, keepdims=True) + EPS)
    vhat = d * rstd
    return vhat, rstd, vhat * lgh


def _gmlp_fwd(zuv, lg, wm, bfull):
    T = zuv.shape[0]
    tg = min(TM, T)
    nb = tg // SG_BLOCK

    def body(zuv_ref, lg_ref, wm_ref, b_ref, oa_ref):
        for h in range(N_HEADS):
            sl = _head_sl(h)
            u = _gelu(zuv_ref[:, sl])
            v = _gelu(zuv_ref[:, D_A + HEAD_DIM * h:D_A + HEAD_DIM * (h + 1)])
            _, _, vn = _ln_head(v, lg_ref[:, sl])
            vn = vn.astype(BF16)
            for n in range(nb):
                rs = slice(n * SG_BLOCK, (n + 1) * SG_BLOCK)
                mixed = _dot(wm_ref[h], vn[rs]) + b_ref[:, sl]
                oa_ref[rs, sl] = (u[rs] * mixed).astype(BF16)

    return pl.pallas_call(
        body, name="gmlp_fwd", grid=(T // tg,),
        in_specs=[_rows(tg, 2 * D_A), _full((1, D_A)), _full((N_HEADS, SG_BLOCK, SG_BLOCK)), _full((SG_BLOCK, D_A))],
        out_specs=_rows(tg, D_A),
        out_shape=jax.ShapeDtypeStruct((T, D_A), BF16),
        compiler_params=_cp("arbitrary"),
    )(zuv, lg, wm, bfull)


def _gmlp_bwd(zuv, doa, lg, wm, wmt, bfull, maskf, hsel):
    T = zuv.shape[0]
    tg = min(TM, T)
    nb = tg // SG_BLOCK
    nt = T // tg

    def body(zuv_ref, doa_ref, lg_ref, wm_ref, wmt_ref, b_ref, mask_ref, hsel_ref,
             dzuv_ref, dwm_ref, dsgb_ref, dlg_ref, dbacc):
        i = pl.program_id(0)

        @pl.when(i == 0)
        def _():
            dwm_ref[...] = jnp.zeros_like(dwm_ref)
            dlg_ref[...] = jnp.zeros_like(dlg_ref)
            dbacc[...] = jnp.zeros_like(dbacc)

        for h in range(N_HEADS):
            sl = _head_sl(h)
            zu = zuv_ref[:, sl]
            zv = zuv_ref[:, D_A + HEAD_DIM * h:D_A + HEAD_DIM * (h + 1)]
            u = _gelu(zu)
            lgh = lg_ref[:, sl]
            vhat, rstd, vn = _ln_head(_gelu(zv), lgh)
            vn = vn.astype(BF16)
            doa_h = doa_ref[:, sl]
            du_parts, dvn_parts = [], []
            for n in range(nb):
                rs = slice(n * SG_BLOCK, (n + 1) * SG_BLOCK)
                mixed = _dot(wm_ref[h], vn[rs]) + b_ref[:, sl]
                du_parts.append(doa_h[rs] * mixed)
                dmixed = doa_h[rs] * u[rs]
                dbacc[:, sl] += dmixed
                dmb = dmixed.astype(BF16)
                dwm_ref[h] += _dot_nt(dmb, vn[rs]) * mask_ref[...]
                dvn_parts.append(_dot(wmt_ref[h], dmb))
            du = jnp.concatenate(du_parts, axis=0) if nb > 1 else du_parts[0]
            dvn = jnp.concatenate(dvn_parts, axis=0) if nb > 1 else dvn_parts[0]
            dlg_ref[:, sl] += jnp.sum(dvn * vhat, axis=0, keepdims=True)
            dvhat = dvn * lgh
            dv = rstd * (dvhat - jnp.mean(dvhat, axis=-1, keepdims=True)
                         - vhat * jnp.mean(dvhat * vhat, axis=-1, keepdims=True))
            dzuv_ref[:, sl] = (du * _gelu_grad(zu)).astype(BF16)
            dzuv_ref[:, D_A + HEAD_DIM * h:D_A + HEAD_DIM * (h + 1)] = (dv * _gelu_grad(zv)).astype(BF16)

        @pl.when(i == nt - 1)
        def _():
            dsgb_ref[...] = lax.dot_general(hsel_ref[...], dbacc[...], (((1,), (1,)), ((), ())),
                                            precision=lax.Precision.HIGHEST, preferred_element_type=F32)

    return pl.pallas_call(
        body, name="gmlp_bwd", grid=(nt,),
        in_specs=[_rows(tg, 2 * D_A), _rows(tg, D_A), _full((1, D_A)), _full((N_HEADS, SG_BLOCK, SG_BLOCK)),
                  _full((N_HEADS, SG_BLOCK, SG_BLOCK)), _full((SG_BLOCK, D_A)), _full((SG_BLOCK, SG_BLOCK)),
                  _full((N_HEADS, D_A))],
        out_specs=[_rows(tg, 2 * D_A), _full((N_HEADS, SG_BLOCK, SG_BLOCK)), _full((N_HEADS, SG_BLOCK)), _full((1, D_A))],
        out_shape=[jax.ShapeDtypeStruct((T, 2 * D_A), BF16), jax.ShapeDtypeStruct((N_HEADS, SG_BLOCK, SG_BLOCK), F32),
                   jax.ShapeDtypeStruct((N_HEADS, SG_BLOCK), F32), jax.ShapeDtypeStruct((1, D_A), F32)],
        scratch_shapes=[pltpu.VMEM((SG_BLOCK, D_A), F32)],
        compiler_params=_cp("arbitrary"),
    )(zuv, doa, lg, wm, wmt, bfull, maskf, hsel)


def _lane_pick(blk, h):
    lane = lax.broadcasted_iota(jnp.int32, blk.shape, 1)
    return jnp.sum(jnp.where(lane == h, blk, 0.0), axis=-1, keepdims=True)


def _fox_fwd(q, k, v, c, ct):
    T = q.shape[0]
    tq = min(TQ, T)
    nq = T // tq

    def body(q_ref, k_ref, v_ref, c_ref, ct_ref, o_ref, lse_ref):
        p = pl.program_id(0)
        i = pl.program_id(1)
        row = lax.broadcasted_iota(jnp.int32, (tq, tq), 0)
        col = lax.broadcasted_iota(jnp.int32, (tq, tq), 1)
        for hh in range(2):
            h = 2 * p + hh
            sl = _head_sl(hh)
            cq = _lane_pick(c_ref[...], h)
            qh = q_ref[:, sl]

            def step(j, carry, diag):
                m, l, acc = carry
                off = pl.multiple_of(j * tq, tq)
                kj = k_ref[pl.ds(off, tq), sl]
                vj = v_ref[pl.ds(off, tq), sl]
                ck = ct_ref[pl.ds(h, 1), pl.ds(off, tq)]
                s = _dot_nt(qh, kj) + cq - ck
                if diag:
                    s = jnp.where(col <= row, s, NEG)
                m_new = jnp.maximum(m, jnp.max(s, axis=-1, keepdims=True))
                alpha = jnp.exp(m - m_new)
                pr = jnp.exp(s - m_new)
                l = alpha * l + jnp.sum(pr, axis=-1, keepdims=True)
                acc = alpha * acc + _dot(pr.astype(BF16), vj)
                return m_new, l, acc

            init = (jnp.full((tq, 1), NEG, F32), jnp.zeros((tq, 1), F32), jnp.zeros((tq, HEAD_DIM), F32))
            carry = lax.fori_loop(0, i, functools.partial(step, diag=False), init)
            m, l, acc = step(i, carry, True)
            o_ref[:, sl] = acc / l
            lse_ref[:, sl] = jnp.broadcast_to(m + jnp.log(l), (tq, HEAD_DIM))

    pair = pl.BlockSpec((tq, LANES), lambda p, i: (i, p))
    seq = pl.BlockSpec((T, LANES), lambda p, i: (0, p))
    return pl.pallas_call(
        body, name="fox_fwd", grid=(N_HEADS // 2, nq),
        in_specs=[pair, seq, seq, pl.BlockSpec((tq, LANES), lambda p, i: (i, 0)), pl.BlockSpec((N_HEADS, T), lambda p, i: (0, 0))],
        out_specs=[pair, pair],
        out_shape=[jax.ShapeDtypeStruct((T, D_B), F32), jax.ShapeDtypeStruct((T, D_B), F32)],
        compiler_params=_cp("arbitrary", "arbitrary"),
    )(q, k, v, c, ct)


def _fox_bwd(q, k, v, do, c, ct, lset, delt):
    T = q.shape[0]
    tq = min(TQ, T)
    nq = T // tq

    def body(q_ref, do_ref, k_ref, v_ref, c_ref, ct_ref, lset_ref, delt_ref, dq_ref, dk_ref, dv_ref, dck_ref, dcq_ref):
        p = pl.program_id(0)
        j = pl.program_id(1)

        @pl.when(j == 0)
        def _():
            dq_ref[...] = jnp.zeros_like(dq_ref)

        @pl.when((j == 0) & (p == 0))
        def _():
            dcq_ref[...] = jnp.zeros_like(dcq_ref)

        row = lax.broadcasted_iota(jnp.int32, (tq, tq), 0)
        col = lax.broadcasted_iota(jnp.int32, (tq, tq), 1)
        for hh in range(2):
            h = 2 * p + hh
            sl = _head_sl(hh)
            ck = _lane_pick(c_ref[...], h)
            kh = k_ref[:, sl]
            vh = v_ref[:, sl]

            def step(i, carry, diag):
                dk, dv, dcs = carry
                off = pl.multiple_of(i * tq, tq)
                qi = q_ref[pl.ds(off, tq), sl]
                doi = do_ref[pl.ds(off, tq), sl]
                cq = ct_ref[pl.ds(h, 1), pl.ds(off, tq)]
                lse = lset_ref[pl.ds(h, 1), pl.ds(off, tq)]
                dl = delt_ref[pl.ds(h, 1), pl.ds(off, tq)]
                st = _dot_nt(kh, qi) + cq - ck - lse
                if diag:
                    st = jnp.where(row <= col, st, NEG)
                pt = jnp.exp(st)
                dv = dv + _dot(pt.astype(BF16), doi)
                dst = pt * (_dot_nt(vh, doi) - dl)
                dcs = dcs + jnp.sum(dst, axis=-1, keepdims=True)
                dcq_ref[pl.ds(h, 1), pl.ds(off, tq)] += jnp.sum(dst, axis=0, keepdims=True)
                dsb = dst.astype(BF16)
                dk = dk + _dot(dsb, qi)
                dq_ref[pl.ds(off, tq), sl] += _dot_tn(dsb, kh)
                return dk, dv, dcs

            init = (jnp.zeros((tq, HEAD_DIM), F32), jnp.zeros((tq, HEAD_DIM), F32), jnp.zeros((tq, 1), F32))
            carry = step(j, init, True)
            dk, dv, dcs = lax.fori_loop(j + 1, nq, functools.partial(step, diag=False), carry)
            dk_ref[:, sl] = dk.astype(BF16)
            dv_ref[:, sl] = dv.astype(BF16)
            dck_ref[:, sl] = jnp.broadcast_to(dcs, (tq, HEAD_DIM))

    pair = pl.BlockSpec((tq, LANES), lambda p, j: (j, p))
    seq = pl.BlockSpec((T, LANES), lambda p, j: (0, p))
    rowsT = pl.BlockSpec((N_HEADS, T), lambda p, j: (0, 0))
    return pl.pallas_call(
        body, name="fox_bwd", grid=(N_HEADS // 2, nq),
        in_specs=[seq, seq, pair, pair, pl.BlockSpec((tq, LANES), lambda p, j: (j, 0)), rowsT, rowsT, rowsT],
        out_specs=[seq, pair, pair, pair, rowsT],
        out_shape=[jax.ShapeDtypeStruct((T, D_B), F32), jax.ShapeDtypeStruct((T, D_B), BF16),
                   jax.ShapeDtypeStruct((T, D_B), BF16), jax.ShapeDtypeStruct((T, D_B), F32),
                   jax.ShapeDtypeStruct((N_HEADS, T), F32)],
        compiler_params=_cp("arbitrary", "arbitrary"),
    )(q, do, k, v, c, ct, lset, delt)


def _fwd_mid(x, oa, ob, w_out, g2, w_up):
    T = x.shape[0]
    tm = min(TM, T)

    def body(x_ref, oa_ref, ob_ref, wo_ref, g_ref, wu_ref, x2_ref, h2_ref, a_ref):
        oab = jnp.concatenate([oa_ref[...], ob_ref[...].astype(BF16)], axis=-1)
        x2 = x_ref[...] + _dot(oab, wo_ref[...])
        x2_ref[...] = x2
        _, n = _rms(x2)
        h2 = (n * g_ref[...]).astype(BF16)
        h2_ref[...] = h2
        a_ref[...] = _dot(h2, wu_ref[...])

    return pl.pallas_call(
        body, name="fwd_mid", grid=(T // tm,),
        in_specs=[_rows(tm, D_MODEL), _rows(tm, D_A), _rows(tm, D_B), _full((D_MODEL, D_MODEL), True), _full((1, D_MODEL)),
                  _full((D_MODEL, D_FF2), True)],
        out_specs=[_rows(tm, D_MODEL), _rows(tm, D_MODEL), _rows(tm, D_FF2)],
        out_shape=[jax.ShapeDtypeStruct((T, D_MODEL), F32), jax.ShapeDtypeStruct((T, D_MODEL), BF16),
                   jax.ShapeDtypeStruct((T, D_FF2), F32)],
        compiler_params=_cp("arbitrary"),
    )(x, oa, ob, w_out, g2, w_up)


def _conv_taps(a0, prev):
    rowi = lax.broadcasted_iota(jnp.int32, a0.shape, 0)
    am1 = jnp.where(rowi == 0, prev[7:8], pltpu.roll(a0, 1, axis=0))
    am2 = jnp.where(rowi == 0, prev[6:7], jnp.where(rowi == 1, prev[7:8], pltpu.roll(a0, 2, axis=0)))
    return am1, am2


def _fwd_ffn(a, x2, wc, bc, w_down, g3, tgt):
    T = x2.shape[0]
    tm = min(TM, T)

    def body(a_ref, x2_ref, wc_ref, bc_ref, wd_ref, g_ref, tgt_ref, yff_ref, dx3_ref, loss_ref, dg3_ref, carry):
        @pl.when(pl.program_id(0) == 0)
        def _():
            carry[...] = jnp.zeros_like(carry)
            loss_ref[...] = jnp.zeros_like(loss_ref)
            dg3_ref[...] = jnp.zeros_like(dg3_ref)

        def conv(cs):
            a0 = a_ref[:, cs]
            am1, am2 = _conv_taps(a0, carry[:, cs])
            return wc_ref[0:1, cs] * am2 + wc_ref[1:2, cs] * am1 + wc_ref[2:3, cs] * a0 + bc_ref[:, cs]

        x3 = x2_ref[...]
        for ci in range(D_FF // CW):
            gs = slice(ci * CW, (ci + 1) * CW)
            ag = conv(gs)
            av = conv(slice(D_FF + ci * CW, D_FF + (ci + 1) * CW))
            yb = (ag * jax.nn.sigmoid(ag) * av).astype(BF16)
            yff_ref[:, gs] = yb
            x3 = x3 + _dot(yb, wd_ref[gs, :])
        carry[...] = a_ref[tm - 8:tm, :]
        r, n = _rms(x3)
        g = g_ref[...]
        diff = n * g - tgt_ref[...]
        loss_ref[...] += (0.5 / D_MODEL) * jnp.sum(diff * diff)
        dout = diff * (1.0 / D_MODEL)
        dg3_ref[...] += jnp.sum(dout * n, axis=0, keepdims=True)
        dx3_ref[...] = _rms_bwd(dout, n, r, g)

    return pl.pallas_call(
        body, name="fwd_ffn", grid=(T // tm,),
        in_specs=[_rows(tm, D_FF2), _rows(tm, D_MODEL), _full((3, D_FF2)), _full((1, D_FF2)), _full((D_FF, D_MODEL), True),
                  _full((1, D_MODEL)), _rows(tm, D_MODEL)],
        out_specs=[_rows(tm, D_FF), _rows(tm, D_MODEL), _full((8, LANES)), _full((1, D_MODEL))],
        out_shape=[jax.ShapeDtypeStruct((T, D_FF), BF16), jax.ShapeDtypeStruct((T, D_MODEL), F32),
                   jax.ShapeDtypeStruct((8, LANES), F32), jax.ShapeDtypeStruct((1, D_MODEL), F32)],
        scratch_shapes=[pltpu.VMEM((8, D_FF2), F32)],
        compiler_params=_cp("arbitrary"),
    )(a, x2, wc, bc, w_down, g3, tgt)


def _bwd_ffn(dx3, a, w_down, wc, bc):
    T = dx3.shape[0]
    tm = min(TM, T)
    nt = T // tm

    def body(dx3_ref, a_ref, halo_ref, wd_ref, wc_ref, bc_ref, da_ref, dwc_ref, dbc_ref, nxt):
        r = pl.program_id(0)

        @pl.when(r == 0)
        def _():
            nxt[...] = jnp.zeros_like(nxt)
            dwc_ref[...] = jnp.zeros_like(dwc_ref)
            dbc_ref[...] = jnp.zeros_like(dbc_ref)

        first = r == nt - 1
        dxb = dx3_ref[...].astype(BF16)
        rowi = lax.broadcasted_iota(jnp.int32, (tm, CW), 0)

        def taps(cs):
            a0 = a_ref[:, cs]
            prev = jnp.where(first, 0.0, halo_ref[:, cs])
            am1, am2 = _conv_taps(a0, prev)
            ac = wc_ref[0:1, cs] * am2 + wc_ref[1:2, cs] * am1 + wc_ref[2:3, cs] * a0 + bc_ref[:, cs]
            return ac, a0, am1, am2

        def back(cs, dac, a0, am1, am2):
            dbc_ref[:, cs] += jnp.sum(dac, axis=0, keepdims=True)
            dwc_ref[0:1, cs] += jnp.sum(dac * am2, axis=0, keepdims=True)
            dwc_ref[1:2, cs] += jnp.sum(dac * am1, axis=0, keepdims=True)
            dwc_ref[2:3, cs] += jnp.sum(dac * a0, axis=0, keepdims=True)
            nx = nxt[:, cs]
            dp1 = jnp.where(rowi == tm - 1, nx[0:1], pltpu.roll(dac, tm - 1, axis=0))
            dp2 = jnp.where(rowi == tm - 2, nx[0:1], jnp.where(rowi == tm - 1, nx[1:2], pltpu.roll(dac, tm - 2, axis=0)))
            da_ref[:, cs] = (wc_ref[2:3, cs] * dac + wc_ref[1:2, cs] * dp1 + wc_ref[0:1, cs] * dp2).astype(BF16)
            nxt[:, cs] = dac[0:8]

        for ci in range(D_FF // CW):
            gs = slice(ci * CW, (ci + 1) * CW)
            vs = slice(D_FF + ci * CW, D_FF + (ci + 1) * CW)
            dy = _dot_nt(dxb, wd_ref[gs, :])
            ag, g0, g1, g2 = taps(gs)
            av, v0, v1, v2 = taps(vs)
            sg = jax.nn.sigmoid(ag)
            back(vs, dy * (ag * sg), v0, v1, v2)
            back(gs, dy * av * (sg * (1.0 + ag * (1.0 - sg))), g0, g1, g2)

    halo = pl.BlockSpec((8, D_FF2), lambda i: (jnp.maximum((nt - 1 - i) * (tm // 8) - 1, 0), 0))
    return pl.pallas_call(
        body, name="bwd_ffn", grid=(nt,),
        in_specs=[_rows(tm, D_MODEL, nt), _rows(tm, D_FF2, nt), halo, _full((D_FF, D_MODEL), True), _full((3, D_FF2)),
                  _full((1, D_FF2))],
        out_specs=[_rows(tm, D_FF2, nt), _full((3, D_FF2)), _full((1, D_FF2))],
        out_shape=[jax.ShapeDtypeStruct((T, D_FF2), BF16), jax.ShapeDtypeStruct((3, D_FF2), F32),
                   jax.ShapeDtypeStruct((1, D_FF2), F32)],
        scratch_shapes=[pltpu.VMEM((8, D_FF2), F32)],
        compiler_params=_cp("arbitrary"),
    )(dx3, a, a, w_down, wc, bc)


def _bwd_mid(da, w_up, x2, g2, dx3, w_out, ob):
    T = x2.shape[0]
    tm = min(TM, T)

    def body(da_ref, wu_ref, x2_ref, g_ref, dx3_ref, wo_ref, ob_ref, dx2_ref, doa_ref, dob_ref, del_ref, dg2_ref):
        @pl.when(pl.program_id(0) == 0)
        def _():
            dg2_ref[...] = jnp.zeros_like(dg2_ref)

        dh2 = _dot_nt(da_ref[...], wu_ref[...])
        r, n = _rms(x2_ref[...])
        dg2_ref[...] += jnp.sum(dh2 * n, axis=0, keepdims=True)
        dx2 = dx3_ref[...] + _rms_bwd(dh2, n, r, g_ref[...])
        dx2_ref[...] = dx2
        doab = _dot_nt(dx2.astype(BF16), wo_ref[...])
        doa_ref[...] = doab[:, :D_A]
        dob = doab[:, D_A:]
        dob_ref[...] = dob.astype(BF16)
        prod = dob * ob_ref[...]
        lane = lax.broadcasted_iota(jnp.int32, (tm, LANES), 1)
        delta = jnp.zeros((tm, LANES), F32)
        for h in range(N_HEADS):
            delta = jnp.where(lane == h, jnp.sum(prod[:, _head_sl(h)], axis=-1, keepdims=True), delta)
        del_ref[...] = delta

    return pl.pallas_call(
        body, name="bwd_mid", grid=(T // tm,),
        in_specs=[_rows(tm, D_FF2), _full((D_MODEL, D_FF2), True), _rows(tm, D_MODEL), _full((1, D_MODEL)), _rows(tm, D_MODEL),
                  _full((D_MODEL, D_MODEL), True), _rows(tm, D_B)],
        out_specs=[_rows(tm, D_MODEL), _rows(tm, D_A), _rows(tm, D_B), _rows(tm, LANES), _full((1, D_MODEL))],
        out_shape=[jax.ShapeDtypeStruct((T, D_MODEL), F32), jax.ShapeDtypeStruct((T, D_A), F32),
                   jax.ShapeDtypeStruct((T, D_B), BF16), jax.ShapeDtypeStruct((T, LANES), F32),
                   jax.ShapeDtypeStruct((1, D_MODEL), F32)],
        compiler_params=_cp("arbitrary"),
    )(da, w_up, x2, g2, dx3, w_out, ob)


def _bwd_in(dzuv, dq, dk, dv, dc, fl, x, dx2, w_in_p, g1):
    T = x.shape[0]
    tm = min(TM, T)
    nt = T // tm

    def body(dzuv_ref, dq_ref, dk_ref, dv_ref, dc_ref, fl_ref, x_ref, dx2_ref, w_ref, g_ref,
             gx_ref, dz_ref, dg1_ref, dfb_ref, carry):
        @pl.when(pl.program_id(0) == 0)
        def _():
            carry[...] = jnp.zeros_like(carry)
            dg1_ref[...] = jnp.zeros_like(dg1_ref)
            dfb_ref[...] = jnp.zeros_like(dfb_ref)

        later = (lax.broadcasted_iota(jnp.int32, (tm, tm), 1) >= lax.broadcasted_iota(jnp.int32, (tm, tm), 0)).astype(F32)
        dls = _dot_f32(later, dc_ref[...]) + carry[...]
        carry[...] = dls[0:1, :]
        dzf = dls * jax.nn.sigmoid(-fl_ref[...])
        dfb_ref[...] += jnp.sum(dzf, axis=0, keepdims=True)
        dz = jnp.concatenate([dzuv_ref[...], (dq_ref[...] * (HEAD_DIM ** -0.5)).astype(BF16), dk_ref[...], dv_ref[...],
                              dzf.astype(BF16)], axis=-1)
        dz_ref[...] = dz
        dh1 = _dot_nt(dz, w_ref[...])
        r, n = _rms(x_ref[...])
        dg1_ref[...] += jnp.sum(dh1 * n, axis=0, keepdims=True)
        gx_ref[...] = dx2_ref[...] + _rms_bwd(dh1, n, r, g_ref[...])

    rv = functools.partial(_rows, tm, rev_nt=nt)
    return pl.pallas_call(
        body, name="bwd_in", grid=(nt,),
        in_specs=[rv(2 * D_A), rv(D_B), rv(D_B), rv(D_B), rv(LANES), rv(LANES), rv(D_MODEL), rv(D_MODEL),
                  _full((D_MODEL, D_IN_PAD), True), _full((1, D_MODEL))],
        out_specs=[rv(D_MODEL), rv(D_IN_PAD), _full((1, D_MODEL)), _full((1, LANES))],
        out_shape=[jax.ShapeDtypeStruct((T, D_MODEL), F32), jax.ShapeDtypeStruct((T, D_IN_PAD), BF16),
                   jax.ShapeDtypeStruct((1, D_MODEL), F32), jax.ShapeDtypeStruct((1, LANES), F32)],
        scratch_shapes=[pltpu.VMEM((1, LANES), F32)],
        compiler_params=_cp("arbitrary"),
    )(dzuv, dq, dk, dv, dc, fl, x, dx2, w_in_p, g1)


def _matmul_tn(a, b, tmm, tn, name):
    T, M = a.shape
    N = b.shape[1]
    tk = min(512, T)

    def body(a_ref, b_ref, o_ref):
        @pl.when(pl.program_id(2) == 0)
        def _():
            o_ref[...] = jnp.zeros_like(o_ref)

        o_ref[...] += _dot_tn(a_ref[...].astype(BF16), b_ref[...].astype(BF16))

    return pl.pallas_call(
        body, name=name, grid=(M // tmm, N // tn, T // tk),
        in_specs=[pl.BlockSpec((tk, tmm), lambda i, j, k: (k, i)), pl.BlockSpec((tk, tn), lambda i, j, k: (k, j))],
        out_specs=pl.BlockSpec((tmm, tn), lambda i, j, k: (i, j)),
        out_shape=jax.ShapeDtypeStruct((M, N), F32),
        compiler_params=_cp("arbitrary", "arbitrary", "arbitrary"),
    )(a, b)


def _exchange(gather, scatter, name):
    n_g, n_s = len(gather), len(scatter)
    n = n_g + n_s

    def body(*refs):
        ins, outs = refs[:n], refs[n:2 * n]
        send_sems, recv_sems, local_sems = refs[2 * n:]
        x, y, c = (lax.axis_index(ax) for ax in MESH_AXES)
        me = 4 * x + 2 * y + c
        sibling = (x, y, 1 - c)
        chips = [(1 - x, y), (x, 1 - y), (1 - x, 1 - y)]
        peers = [sibling] + [(*chip, c) for chip in chips] + [(*chip, 1 - c) for chip in chips]

        def index(dev):
            return 4 * dev[0] + 2 * dev[1] + dev[2]

        def remote(k, src, dst, to):
            return pltpu.make_async_remote_copy(src_ref=src, dst_ref=dst, send_sem=send_sems.at[k], recv_sem=recv_sems.at[k],
                                                device_id=to, device_id_type=pl.DeviceIdType.MESH)

        local, sends, forwards = [], [], []
        for a in range(n_g):
            src, out, base = ins[a], outs[a], 7 * a
            local.append(pltpu.make_async_copy(src, out.at[me], local_sems.at[a]))
            sends.append(remote(base, src, out.at[me], sibling))
            for j, chip in enumerate(chips):
                sends.append(remote(base + 1 + j, src, out.at[me], (*chip, c)))
        for a in range(n_g, n):
            src, out, base = ins[a], outs[a], 7 * a
            local.append(pltpu.make_async_copy(src.at[me], out.at[me], local_sems.at[a]))
            for k, peer in enumerate(peers):
                sends.append(remote(base + k, src.at[index(peer)], out.at[me], peer))
        for cp in local + sends:
            cp.start()
        for a in range(n_g):
            out, base = outs[a], 7 * a
            for j, chip in enumerate(chips):
                slot = out.at[index((*chip, c))]
                remote(base + 1 + j, slot, slot, (*chip, c)).wait_recv()
                fwd = remote(base + 4 + j, slot, slot, sibling)
                fwd.start()
                forwards.append(fwd)
        for a in range(n_g):
            out, base = outs[a], 7 * a
            slot = out.at[index(sibling)]
            remote(base, slot, slot, sibling).wait_recv()
            for j, chip in enumerate(chips):
                slot = out.at[index((*chip, 1 - c))]
                remote(base + 4 + j, slot, slot, sibling).wait_recv()
        for a in range(n_g, n):
            out, base = outs[a], 7 * a
            for k, peer in enumerate(peers):
                slot = out.at[index(peer)]
                remote(base + k, slot, slot, peer).wait_recv()
        for cp in sends + forwards:
            cp.wait_send()
        for cp in local:
            cp.wait()

    out_shape = [jax.ShapeDtypeStruct((N_DEV, *g.shape), g.dtype) for g in gather]
    out_shape += [jax.ShapeDtypeStruct(s.shape, s.dtype) for s in scatter]
    return pl.pallas_call(
        body, name=name,
        in_specs=[ANY] * n, out_specs=[ANY] * n, out_shape=out_shape,
        scratch_shapes=[pltpu.SemaphoreType.DMA((7 * n,)), pltpu.SemaphoreType.DMA((7 * n,)), pltpu.SemaphoreType.DMA((n,))],
    )(*gather, *scatter)


def _adamw(w, g, m, v):
    m = ADAM_B1 * m + (1.0 - ADAM_B1) * g
    v = ADAM_B2 * v + (1.0 - ADAM_B2) * jnp.square(g)
    m_hat = m / (1.0 - ADAM_B1 ** ADAM_STEP)
    v_hat = v / (1.0 - ADAM_B2 ** ADAM_STEP)
    delta = -ADAM_LR * (m_hat / (jnp.sqrt(v_hat) + ADAM_EPS) + ADAM_WD * w)
    return delta, m, v


def _adamw_shard(w, m, v, own, recv, tr, name):
    R, C = w.shape

    def body(w_ref, m_ref, v_ref, own_ref, recv_ref, g_ref, d_ref, nm_ref, nv_ref):
        x, y, c = (lax.axis_index(ax) for ax in MESH_AXES)
        me = 4 * x + 2 * y + c
        g = own_ref[...]
        for d in range(N_DEV):
            g = g + jnp.where(d == me, 0.0, recv_ref[d].astype(F32))
        g_ref[...] = g
        d_ref[...], nm_ref[...], nv_ref[...] = _adamw(w_ref[...], g, m_ref[...], v_ref[...])

    blk = pl.BlockSpec((tr, C), lambda i: (i, 0))
    return pl.pallas_call(
        body, name=name, grid=(R // tr,),
        in_specs=[blk, blk, blk, blk, pl.BlockSpec((N_DEV, tr, C), lambda i: (0, i, 0))],
        out_specs=[blk] * 4, out_shape=[jax.ShapeDtypeStruct((R, C), F32)] * 4,
        compiler_params=_cp("arbitrary"),
    )(w, m, v, own, recv)


def _adamw_small(w, m, v, gall):
    R = w.shape[0]
    tr = R // 4

    def body(w_ref, m_ref, v_ref, gall_ref, g_ref, d_ref, nm_ref, nv_ref):
        g = gall_ref[0]
        for d in range(1, N_DEV):
            g = g + gall_ref[d]
        g_ref[...] = g
        d_ref[...], nm_ref[...], nv_ref[...] = _adamw(w_ref[...], g, m_ref[...], v_ref[...])

    blk = pl.BlockSpec((tr, LANES), lambda i: (i, 0))
    return pl.pallas_call(
        body, name="adamw_small", grid=(R // tr,),
        in_specs=[blk, blk, blk, pl.BlockSpec((N_DEV, tr, LANES), lambda i: (0, i, 0))],
        out_specs=[blk] * 4, out_shape=[jax.ShapeDtypeStruct((R, LANES), F32)] * 4,
        compiler_params=_cp("arbitrary"),
    )(w, m, v, gall)


def _local_step(x, tgt, g1, w_in_p, fb, lg, sg_w, sg_b, w_out, g2, w_up, wc, bc, w_down, g3):
    pos_chunk = jnp.arange(SG_BLOCK) // SG_CHUNK
    maskf = (pos_chunk[:, None] >= pos_chunk[None, :]).astype(F32)
    wm = (sg_w * maskf[None]).astype(BF16)
    wmt = jnp.swapaxes(wm, 1, 2)
    bfull = jnp.repeat(sg_b.T, HEAD_DIM, axis=1)
    hsel = jnp.repeat(jnp.eye(N_HEADS, dtype=F32), HEAD_DIM, axis=1)

    zuv, q, k, v, fl, c, h1 = _fwd_in(x, g1, w_in_p, fb)
    ct = c[:, :N_HEADS].T
    oa = _gmlp_fwd(zuv, lg, wm, bfull)
    ob, lse = _fox_fwd(q, k, v, c, ct)
    x2, h2, a = _fwd_mid(x, oa, ob, w_out, g2, w_up)
    yff, dx3, loss, dg3 = _fwd_ffn(a, x2, wc, bc, w_down, g3, tgt)
    da, dwc, dbc = _bwd_ffn(dx3, a, w_down, wc, bc)
    dx2, doa, dob, delta, dg2 = _bwd_mid(da, w_up, x2, g2, dx3, w_out, ob)
    lset = lse[:, ::HEAD_DIM].T
    delt = delta[:, :N_HEADS].T
    dq, dk, dv, dck, dcq = _fox_bwd(q, k, v, dob, c, ct, lset, delt)
    dc = jnp.pad(dcq.T - dck[:, ::HEAD_DIM], ((0, 0), (0, LANES - N_HEADS)))
    dzuv, dwm, dsgb, dlg = _gmlp_bwd(zuv, doa, lg, wm, wmt, bfull, maskf, hsel)
    gx, dz, dg1, dfb = _bwd_in(dzuv, dq, dk, dv, dc, fl, x, dx2, w_in_p, g1)
    dwin = _matmul_tn(h1, dz, D_MODEL, D_IN_PAD // 3, "dw_in")
    dwout = jnp.concatenate([_matmul_tn(oa, dx2, D_A, D_MODEL, "dw_out_a"), _matmul_tn(ob, dx2, D_B, D_MODEL, "dw_out_b")],
                            axis=0)
    dwup = _matmul_tn(h2, da, D_MODEL, D_FF2 // 4, "dw_up")
    dwdown = _matmul_tn(yff, dx3, D_FF // 2, D_MODEL, "dw_down")
    small = dict(g1=dg1, fb=dfb, lg=dlg, sg_w=dwm, sg_b=dsgb, g2=dg2, bc=dbc, g3=dg3)
    big = dict(w_in=dwin[:, :D_IN], w_out=dwout, w_up=dwup, wc=dwc, w_down=dwdown)
    return loss[0, 0], gx, small, big


_SMALL = (("g1", 8), ("fb", 8), ("lg", 8), ("sg_w", 1024), ("sg_b", 8), ("g2", 8), ("bc", 48), ("g3", 8))


def _pack_small(parts):
    rows = []
    for name, nrow in _SMALL:
        flat = parts[name].astype(F32).reshape(-1)
        flat = jnp.pad(flat, (0, nrow * LANES - flat.shape[0]))
        rows.append(flat.reshape(nrow, LANES))
    return jnp.concatenate(rows, axis=0)


def _unpack_small(packed, shapes):
    out, r = {}, 0
    for name, nrow in _SMALL:
        size = math.prod(shapes[name])
        out[name] = packed[r:r + nrow].reshape(-1)[:size].reshape(shapes[name])
        r += nrow
    return out


def kernel(x, norm_mix_g, w_in, f_bias, sg_ln_g, sg_w, sg_b, w_out, norm_ffn_g, w_up, w_conv, b_conv, w_down, norm_final_g, loss_target, m_norm_mix_g, m_w_in, m_f_bias, m_sg_ln_g, m_sg_w, m_sg_b, m_w_out, m_norm_ffn_g, m_w_up, m_w_conv, m_b_conv, m_w_down, m_norm_final_g, v_norm_mix_g, v_w_in, v_f_bias, v_sg_ln_g, v_sg_w, v_sg_b, v_w_out, v_norm_ffn_g, v_w_up, v_w_conv, v_b_conv, v_w_down, v_norm_final_g):
    me = 4 * lax.axis_index("x") + 2 * lax.axis_index("y") + lax.axis_index("c")

    win_g, wout_g, wup_g, wdown_g, wc_g = _exchange(
        [w_in[0].astype(BF16), w_out[0].astype(BF16), w_up[0].astype(BF16), w_down[0].astype(BF16), w_conv[0]], [], "gather_weights")
    w_in_full = jnp.transpose(win_g, (1, 0, 2)).reshape(D_MODEL, D_IN)
    w_in_p = jnp.pad(w_in_full, ((0, 0), (0, D_IN_PAD - D_IN)))
    w_out_full = wout_g.reshape(D_MODEL, D_MODEL)
    w_up_full = jnp.transpose(wup_g, (1, 0, 2)).reshape(D_MODEL, D_FF2)
    w_down_full = wdown_g.reshape(D_FF, D_MODEL)
    wc_full = jnp.transpose(wc_g, (1, 0, 2)).reshape(3, D_FF2)

    fb = jnp.pad(f_bias, ((0, 0), (0, LANES - N_HEADS)))
    loss, gx, small, big = _local_step(
        x[0], loss_target[0], norm_mix_g, w_in_p, fb, sg_ln_g.reshape(1, D_A), sg_w[0], sg_b[0], w_out_full, norm_ffn_g,
        w_up_full, wc_full, b_conv, w_down_full, norm_final_g.reshape(1, D_MODEL))

    def col_shards(g):
        return jnp.transpose(g.reshape(g.shape[0], N_DEV, -1), (1, 0, 2))

    def row_shards(g):
        return g.reshape(N_DEV, -1, g.shape[1])

    shards = dict(w_in=col_shards(big["w_in"]), w_out=row_shards(big["w_out"]), w_up=col_shards(big["w_up"]),
                  wc=col_shards(big["wc"]), w_down=row_shards(big["w_down"]))
    order = ("w_in", "w_out", "w_up", "wc", "w_down")
    got = _exchange([_pack_small(small)], [shards[n].astype(BF16) for n in order], "exchange_grads")
    small_all, recv = got[0], dict(zip(order, got[1:]))

    weights = dict(w_in=(w_in, m_w_in, v_w_in, 256), w_out=(w_out, m_w_out, v_w_out, 128), w_up=(w_up, m_w_up, v_w_up, 256),
                   wc=(w_conv, m_w_conv, v_w_conv, 3), w_down=(w_down, m_w_down, v_w_down, 176))
    res = {}
    for n in order:
        w, m, v, tr = weights[n]
        own = lax.dynamic_index_in_dim(shards[n], me, axis=0, keepdims=False)
        res[n] = [r[None] for r in _adamw_shard(w[0], m[0], v[0], own, recv[n], tr, "adamw_" + n)]

    reps = dict(g1=(norm_mix_g, m_norm_mix_g, v_norm_mix_g), fb=(f_bias, m_f_bias, v_f_bias), lg=(sg_ln_g, m_sg_ln_g, v_sg_ln_g),
                sg_w=(sg_w, m_sg_w, v_sg_w), sg_b=(sg_b, m_sg_b, v_sg_b), g2=(norm_ffn_g, m_norm_ffn_g, v_norm_ffn_g),
                bc=(b_conv, m_b_conv, v_b_conv), g3=(norm_final_g, m_norm_final_g, v_norm_final_g))
    shapes = {n: t[0].shape for n, t in reps.items()}
    packed = _adamw_small(*[_pack_small({n: t[i] for n, t in reps.items()}) for i in range(3)], small_all)
    unpacked = [_unpack_small(p, shapes) for p in packed]
    for n in reps:
        res[n] = [u[n] for u in unpacked]

    loss = lax.psum(loss, MESH_AXES)
    names = ("g1", "w_in", "fb", "lg", "sg_w", "sg_b", "w_out", "g2", "w_up", "wc", "bc", "w_down", "g3")
    return (loss, gx[None], *[res[n][0] for n in names], *[res[n][1] for n in names], *[res[n][2] for n in names],
            *[res[n][3] for n in names])
```

```python
import functools
import math

import jax
import jax.numpy as jnp
from jax import lax
from jax.experimental import pallas as pl
from jax.experimental.pallas import tpu as pltpu

F32 = jnp.float32
BF16 = jnp.bfloat16

D_MODEL = 1024
HEAD_DIM = 64
N_HEADS = 8
D_A = 512
D_B = 512
D_IN = 2 * D_A + 3 * D_B + N_HEADS
D_IN_PAD = 2688
D_FF = 2816
D_FF2 = 2 * D_FF
SG_BLOCK = 128
SG_CHUNK = 64
EPS = 1e-6
N_DEV = 8
LANES = 128
NEG = -1e30
VMEM_LIMIT = 56 * 1024 * 1024

ADAM_LR = 0.001
ADAM_B1 = 0.9
ADAM_B2 = 0.999
ADAM_EPS = 1e-08
ADAM_WD = 0.01
ADAM_STEP = 10

TM = 256
TQ = 512
CW = 256

MESH_AXES = ("x", "y", "c")
ANY = pl.BlockSpec(memory_space=pl.ANY)


def _cp(*sem):
    return pltpu.CompilerParams(dimension_semantics=sem, vmem_limit_bytes=VMEM_LIMIT)


def _dot(a, b):
    return jnp.dot(a, b, preferred_element_type=F32)


def _dot_nt(a, b):
    return lax.dot_general(a, b, (((1,), (1,)), ((), ())), preferred_element_type=F32)


def _dot_tn(a, b):
    return lax.dot_general(a, b, (((0,), (0,)), ((), ())), preferred_element_type=F32)


def _dot_f32(a, b):
    return jnp.dot(a, b, precision=lax.Precision.HIGHEST, preferred_element_type=F32)


def _gelu(z):
    return 0.5 * z * (1.0 + lax.erf(z * (1.0 / math.sqrt(2.0))))


def _gelu_grad(z):
    return 0.5 * (1.0 + lax.erf(z * (1.0 / math.sqrt(2.0)))) + z * jnp.exp(-0.5 * z * z) * (1.0 / math.sqrt(2.0 * math.pi))


def _log_sigmoid(x):
    return jnp.minimum(x, 0.0) - jnp.log1p(jnp.exp(-jnp.abs(x)))


def _rms(x):
    r = lax.rsqrt(jnp.mean(x * x, axis=-1, keepdims=True) + EPS)
    return r, x * r


def _rms_bwd(dy, n, r, g):
    dn = dy * g
    return r * (dn - n * jnp.mean(dn * n, axis=-1, keepdims=True))


def _full(shape, single=False):
    nd = len(shape)
    if single:
        return pl.BlockSpec(shape, lambda *_: (0,) * nd, pipeline_mode=pl.Buffered(1))
    return pl.BlockSpec(shape, lambda *_: (0,) * nd)


def _rows(tm, cols, rev_nt=None):
    if rev_nt is None:
        return pl.BlockSpec((tm, cols), lambda i: (i, 0))
    return pl.BlockSpec((tm, cols), lambda i: (rev_nt - 1 - i, 0))


def _head_sl(h):
    return slice(HEAD_DIM * h, HEAD_DIM * (h + 1))


L_ROW = HEAD_DIM
L_COL = HEAD_DIM + 3
L_LSE = HEAD_DIM + 6


def _split3(x):
    hi = x.astype(BF16).astype(F32)
    mid = (x - hi).astype(BF16).astype(F32)
    lo = (x - hi - mid).astype(BF16).astype(F32)
    return hi, mid, lo


def _lanes(rows, width, parts):
    lane = lax.broadcasted_iota(jnp.int32, (rows, width), 1)
    out = jnp.zeros((rows, width), F32)
    for at, val in parts.items():
        out = jnp.where(lane == at, val, out)
    return out


def _fwd_in(x, g1, w_in_p, fb):
    T = x.shape[0]
    tm = min(TM, T)

    def body(x_ref, g_ref, w_ref, fb_ref, zuv_ref, qa_ref, ka_ref, va_ref, fl_ref, h1_ref, carry):
        @pl.when(pl.program_id(0) == 0)
        def _():
            carry[...] = jnp.zeros_like(carry)

        r, n = _rms(x_ref[...])
        h = (n * g_ref[...]).astype(BF16)
        h1_ref[...] = h
        z = _dot(h, w_ref[...])
        zuv_ref[...] = z[:, :2 * D_A]
        o = 2 * D_A
        fl = z[:, o + 3 * D_B:] + fb_ref[...]
        fl_ref[...] = fl
        tri = (lax.broadcasted_iota(jnp.int32, (tm, tm), 0) >= lax.broadcasted_iota(jnp.int32, (tm, tm), 1)).astype(F32)
        c = _dot_f32(tri, _log_sigmoid(fl)) + carry[...]
        carry[...] = c[tm - 1:tm, :]
        ext_v = _lanes(tm, HEAD_DIM, {0: 1.0, 1: 1.0, 2: 1.0}).astype(BF16)
        for hd in range(N_HEADS):
            hi, mid, lo = _split3(c[:, hd:hd + 1])
            ext_q = _lanes(tm, HEAD_DIM, {0: hi, 1: mid, 2: lo, 3: 1.0, 4: 1.0, 5: 1.0})
            ext_k = _lanes(tm, HEAD_DIM, {0: 1.0, 1: 1.0, 2: 1.0, 3: -hi, 4: -mid, 5: -lo, 6: 1.0, 7: 1.0, 8: 1.0})
            qh = z[:, o + HEAD_DIM * hd:o + HEAD_DIM * (hd + 1)] * (HEAD_DIM ** -0.5)
            kh = z[:, o + D_B + HEAD_DIM * hd:o + D_B + HEAD_DIM * (hd + 1)]
            vh = z[:, o + 2 * D_B + HEAD_DIM * hd:o + 2 * D_B + HEAD_DIM * (hd + 1)]
            qa_ref[hd] = jnp.concatenate([qh.astype(BF16), ext_q.astype(BF16)], axis=-1)
            ka_ref[hd] = jnp.concatenate([kh.astype(BF16), ext_k.astype(BF16)], axis=-1)
            va_ref[hd] = jnp.concatenate([vh.astype(BF16), ext_v], axis=-1)

    heads = pl.BlockSpec((N_HEADS, tm, LANES), lambda i: (0, i, 0))
    aug = jax.ShapeDtypeStruct((N_HEADS, T, LANES), BF16)
    return pl.pallas_call(
        body, name="fwd_in", grid=(T // tm,),
        in_specs=[_rows(tm, D_MODEL), _full((1, D_MODEL)), _full((D_MODEL, D_IN_PAD), True), _full((1, LANES))],
        out_specs=[_rows(tm, 2 * D_A), heads, heads, heads, _rows(tm, LANES), _rows(tm, D_MODEL)],
        out_shape=[jax.ShapeDtypeStruct((T, 2 * D_A), F32), aug, aug, aug, jax.ShapeDtypeStruct((T, LANES), F32),
                   jax.ShapeDtypeStruct((T, D_MODEL), BF16)],
        scratch_shapes=[pltpu.VMEM((1, LANES), F32)],
        compiler_params=_cp("arbitrary"),
    )(x, g1, w_in_p, fb)


def _ln_head(vh, lgh):
    mu = jnp.mean(vh, axis=-1, keepdims=True)
    d = vh - mu
    rstd = lax.rsqrt(jnp.mean(d * d, axis=-1, keepdims=True) + EPS)
    vhat = d * rstd
    return vhat, rstd, vhat * lgh


def _gmlp_fwd(zuv, lg, wm, bfull):
    T = zuv.shape[0]
    tg = min(TM, T)
    nb = tg // SG_BLOCK

    def body(zuv_ref, lg_ref, wm_ref, b_ref, oa_ref):
        for h in range(N_HEADS):
            sl = _head_sl(h)
            u = _gelu(zuv_ref[:, sl])
            v = _gelu(zuv_ref[:, D_A + HEAD_DIM * h:D_A + HEAD_DIM * (h + 1)])
            _, _, vn = _ln_head(v, lg_ref[:, sl])
            vn = vn.astype(BF16)
            for n in range(nb):
                rs = slice(n * SG_BLOCK, (n + 1) * SG_BLOCK)
                mixed = _dot(wm_ref[h], vn[rs]) + b_ref[:, sl]
                oa_ref[rs, sl] = (u[rs] * mixed).astype(BF16)

    return pl.pallas_call(
        body, name="gmlp_fwd", grid=(T // tg,),
        in_specs=[_rows(tg, 2 * D_A), _full((1, D_A)), _full((N_HEADS, SG_BLOCK, SG_BLOCK)), _full((SG_BLOCK, D_A))],
        out_specs=_rows(tg, D_A),
        out_shape=jax.ShapeDtypeStruct((T, D_A), BF16),
        compiler_params=_cp("arbitrary"),
    )(zuv, lg, wm, bfull)


def _gmlp_bwd(zuv, doa, lg, wm, wmt, bfull, maskf, hsel):
    T = zuv.shape[0]
    tg = min(TM, T)
    nb = tg // SG_BLOCK
    nt = T // tg

    def body(zuv_ref, doa_ref, lg_ref, wm_ref, wmt_ref, b_ref, mask_ref, hsel_ref,
             dzuv_ref, dwm_ref, dsgb_ref, dlg_ref, dbacc):
        i = pl.program_id(0)

        @pl.when(i == 0)
        def _():
            dwm_ref[...] = jnp.zeros_like(dwm_ref)
            dlg_ref[...] = jnp.zeros_like(dlg_ref)
            dbacc[...] = jnp.zeros_like(dbacc)

        for h in range(N_HEADS):
            sl = _head_sl(h)
            zu = zuv_ref[:, sl]
            zv = zuv_ref[:, D_A + HEAD_DIM * h:D_A + HEAD_DIM * (h + 1)]
            u = _gelu(zu)
            lgh = lg_ref[:, sl]
            vhat, rstd, vn = _ln_head(_gelu(zv), lgh)
            vn = vn.astype(BF16)
            doa_h = doa_ref[:, sl]
            du_parts, dvn_parts = [], []
            for n in range(nb):
                rs = slice(n * SG_BLOCK, (n + 1) * SG_BLOCK)
                mixed = _dot(wm_ref[h], vn[rs]) + b_ref[:, sl]
                du_parts.append(doa_h[rs] * mixed)
                dmixed = doa_h[rs] * u[rs]
                dbacc[:, sl] += dmixed
                dmb = dmixed.astype(BF16)
                dwm_ref[h] += _dot_nt(dmb, vn[rs]) * mask_ref[...]
                dvn_parts.append(_dot(wmt_ref[h], dmb))
            du = jnp.concatenate(du_parts, axis=0) if nb > 1 else du_parts[0]
            dvn = jnp.concatenate(dvn_parts, axis=0) if nb > 1 else dvn_parts[0]
            dlg_ref[:, sl] += jnp.sum(dvn * vhat, axis=0, keepdims=True)
            dvhat = dvn * lgh
            dv = rstd * (dvhat - jnp.mean(dvhat, axis=-1, keepdims=True)
                         - vhat * jnp.mean(dvhat * vhat, axis=-1, keepdims=True))
            dzuv_ref[:, sl] = (du * _gelu_grad(zu)).astype(BF16)
            dzuv_ref[:, D_A + HEAD_DIM * h:D_A + HEAD_DIM * (h + 1)] = (dv * _gelu_grad(zv)).astype(BF16)

        @pl.when(i == nt - 1)
        def _():
            dsgb_ref[...] = lax.dot_general(hsel_ref[...], dbacc[...], (((1,), (1,)), ((), ())),
                                            precision=lax.Precision.HIGHEST, preferred_element_type=F32)

    return pl.pallas_call(
        body, name="gmlp_bwd", grid=(nt,),
        in_specs=[_rows(tg, 2 * D_A), _rows(tg, D_A), _full((1, D_A)), _full((N_HEADS, SG_BLOCK, SG_BLOCK)),
                  _full((N_HEADS, SG_BLOCK, SG_BLOCK)), _full((SG_BLOCK, D_A)), _full((SG_BLOCK, SG_BLOCK)),
                  _full((N_HEADS, D_A))],
        out_specs=[_rows(tg, 2 * D_A), _full((N_HEADS, SG_BLOCK, SG_BLOCK)), _full((N_HEADS, SG_BLOCK)), _full((1, D_A))],
        out_shape=[jax.ShapeDtypeStruct((T, 2 * D_A), BF16), jax.ShapeDtypeStruct((N_HEADS, SG_BLOCK, SG_BLOCK), F32),
                   jax.ShapeDtypeStruct((N_HEADS, SG_BLOCK), F32), jax.ShapeDtypeStruct((1, D_A), F32)],
        scratch_shapes=[pltpu.VMEM((SG_BLOCK, D_A), F32)],
        compiler_params=_cp("arbitrary"),
    )(zuv, doa, lg, wm, wmt, bfull, maskf, hsel)


def _fox_fwd(qa, ka, va):
    T = qa.shape[1]
    tq = min(TQ, T)
    nq = T // tq

    def body(qa_ref, ka_ref, va_ref, o_ref, lse_ref):
        i = pl.program_id(1)
        row = lax.broadcasted_iota(jnp.int32, (tq, tq), 0)
        col = lax.broadcasted_iota(jnp.int32, (tq, tq), 1)
        qs = [qa_ref[0], qa_ref[1]]

        def step(j, carry, diag):
            off = pl.multiple_of(j * tq, tq)
            out = []
            for hh in range(2):
                m, acc = carry[hh]
                s = _dot_nt(qs[hh], ka_ref[hh, pl.ds(off, tq), :])
                if diag:
                    s = jnp.where(col <= row, s, NEG)
                m_new = jnp.maximum(m, jnp.max(s, axis=-1, keepdims=True))
                pr = jnp.exp(s - m_new)
                acc = jnp.exp(m - m_new) * acc + _dot(pr.astype(BF16), va_ref[hh, pl.ds(off, tq), :])
                out.append((m_new, acc))
            return tuple(out)

        init = ((jnp.full((tq, 1), NEG, F32), jnp.zeros((tq, LANES), F32)),) * 2
        carry = lax.fori_loop(0, i, functools.partial(step, diag=False), init)
        carry = step(i, carry, True)
        for hh in range(2):
            m, acc = carry[hh]
            l = acc[:, L_ROW:L_ROW + 1]
            o_ref[:, _head_sl(hh)] = acc[:, :HEAD_DIM] / l
            hi, mid, lo = _split3(-(m + jnp.log(l)))
            lse_ref[hh] = _lanes(tq, LANES, {L_LSE: hi, L_LSE + 1: mid, L_LSE + 2: lo}).astype(BF16)

    tile = pl.BlockSpec((2, tq, LANES), lambda p, i: (p, i, 0))
    seq = pl.BlockSpec((2, T, LANES), lambda p, i: (p, 0, 0))
    return pl.pallas_call(
        body, name="fox_fwd", grid=(N_HEADS // 2, nq),
        in_specs=[tile, seq, seq],
        out_specs=[pl.BlockSpec((tq, LANES), lambda p, i: (i, p)), tile],
        out_shape=[jax.ShapeDtypeStruct((T, D_B), F32), jax.ShapeDtypeStruct((N_HEADS, T, LANES), BF16)],
        compiler_params=_cp("arbitrary", "arbitrary"),
    )(qa, ka, va)


def _fox_bwd(qa, lse, doa, ka, va):
    T = qa.shape[1]
    tq = min(TQ, T)
    nq = T // tq

    def body(qa_ref, lse_ref, doa_ref, ka_ref, va_ref, dqa_ref, dka_ref, dva_ref):
        j = pl.program_id(1)

        @pl.when(j == 0)
        def _():
            dqa_ref[...] = jnp.zeros_like(dqa_ref)

        row = lax.broadcasted_iota(jnp.int32, (tq, tq), 0)
        col = lax.broadcasted_iota(jnp.int32, (tq, tq), 1)
        ks = [ka_ref[0], ka_ref[1]]
        vs = [va_ref[0], va_ref[1]]

        def step(i, carry, diag):
            off = pl.multiple_of(i * tq, tq)
            out = []
            for hh in range(2):
                dk, dv = carry[hh]
                qi = qa_ref[hh, pl.ds(off, tq), :] + lse_ref[hh, pl.ds(off, tq), :]
                doi = doa_ref[hh, pl.ds(off, tq), :]
                st = _dot_nt(ks[hh], qi)
                if diag:
                    st = jnp.where(row <= col, st, NEG)
                pt = jnp.exp(st)
                dv = dv + _dot(pt.astype(BF16), doi)
                dsb = (pt * _dot_nt(vs[hh], doi)).astype(BF16)
                dk = dk + _dot(dsb, qi)
                dqa_ref[hh, pl.ds(off, tq), :] += _dot_tn(dsb, ks[hh])
                out.append((dk, dv))
            return tuple(out)

        init = ((jnp.zeros((tq, LANES), F32), jnp.zeros((tq, LANES), F32)),) * 2
        carry = step(j, init, True)
        carry = lax.fori_loop(j + 1, nq, functools.partial(step, diag=False), carry)
        for hh in range(2):
            dka_ref[hh] = carry[hh][0]
            dva_ref[hh] = carry[hh][1].astype(BF16)

    tile = pl.BlockSpec((2, tq, LANES), lambda p, j: (p, j, 0))
    seq = pl.BlockSpec((2, T, LANES), lambda p, j: (p, 0, 0))
    return pl.pallas_call(
        body, name="fox_bwd", grid=(N_HEADS // 2, nq),
        in_specs=[seq, seq, seq, tile, tile],
        out_specs=[seq, tile, tile],
        out_shape=[jax.ShapeDtypeStruct((N_HEADS, T, LANES), F32), jax.ShapeDtypeStruct((N_HEADS, T, LANES), F32),
                   jax.ShapeDtypeStruct((N_HEADS, T, LANES), BF16)],
        compiler_params=_cp("arbitrary", "arbitrary"),
    )(qa, lse, doa, ka, va)


def _fwd_mid(x, oa, ob, w_out, g2, w_up):
    T = x.shape[0]
    tm = min(TM, T)

    def body(x_ref, oa_ref, ob_ref, wo_ref, g_ref, wu_ref, x2_ref, h2_ref, a_ref):
        oab = jnp.concatenate([oa_ref[...], ob_ref[...].astype(BF16)], axis=-1)
        x2 = x_ref[...] + _dot(oab, wo_ref[...])
        x2_ref[...] = x2
        _, n = _rms(x2)
        h2 = (n * g_ref[...]).astype(BF16)
        h2_ref[...] = h2
        a_ref[...] = _dot(h2, wu_ref[...])

    return pl.pallas_call(
        body, name="fwd_mid", grid=(T // tm,),
        in_specs=[_rows(tm, D_MODEL), _rows(tm, D_A), _rows(tm, D_B), _full((D_MODEL, D_MODEL), True), _full((1, D_MODEL)),
                  _full((D_MODEL, D_FF2), True)],
        out_specs=[_rows(tm, D_MODEL), _rows(tm, D_MODEL), _rows(tm, D_FF2)],
        out_shape=[jax.ShapeDtypeStruct((T, D_MODEL), F32), jax.ShapeDtypeStruct((T, D_MODEL), BF16),
                   jax.ShapeDtypeStruct((T, D_FF2), F32)],
        compiler_params=_cp("arbitrary"),
    )(x, oa, ob, w_out, g2, w_up)


def _conv_taps(a0, prev):
    rowi = lax.broadcasted_iota(jnp.int32, a0.shape, 0)
    am1 = jnp.where(rowi == 0, prev[7:8], pltpu.roll(a0, 1, axis=0))
    am2 = jnp.where(rowi == 0, prev[6:7], jnp.where(rowi == 1, prev[7:8], pltpu.roll(a0, 2, axis=0)))
    return am1, am2


def _fwd_ffn(a, x2, wc, bc, w_down, g3, tgt):
    T = x2.shape[0]
    tm = min(TM, T)

    def body(a_ref, x2_ref, wc_ref, bc_ref, wd_ref, g_ref, tgt_ref, yff_ref, dx3_ref, loss_ref, dg3_ref, carry):
        @pl.when(pl.program_id(0) == 0)
        def _():
            carry[...] = jnp.zeros_like(carry)
            loss_ref[...] = jnp.zeros_like(loss_ref)
            dg3_ref[...] = jnp.zeros_like(dg3_ref)

        def conv(cs):
            a0 = a_ref[:, cs]
            am1, am2 = _conv_taps(a0, carry[:, cs])
            return wc_ref[0:1, cs] * am2 + wc_ref[1:2, cs] * am1 + wc_ref[2:3, cs] * a0 + bc_ref[:, cs]

        x3 = x2_ref[...]
        for ci in range(D_FF // CW):
            gs = slice(ci * CW, (ci + 1) * CW)
            ag = conv(gs)
            av = conv(slice(D_FF + ci * CW, D_FF + (ci + 1) * CW))
            yb = (ag * jax.nn.sigmoid(ag) * av).astype(BF16)
            yff_ref[:, gs] = yb
            x3 = x3 + _dot(yb, wd_ref[gs, :])
        carry[...] = a_ref[tm - 8:tm, :]
        r, n = _rms(x3)
        g = g_ref[...]
        diff = n * g - tgt_ref[...]
        loss_ref[...] += (0.5 / D_MODEL) * jnp.sum(diff * diff)
        dout = diff * (1.0 / D_MODEL)
        dg3_ref[...] += jnp.sum(dout * n, axis=0, keepdims=True)
        dx3_ref[...] = _rms_bwd(dout, n, r, g)

    return pl.pallas_call(
        body, name="fwd_ffn", grid=(T // tm,),
        in_specs=[_rows(tm, D_FF2), _rows(tm, D_MODEL), _full((3, D_FF2)), _full((1, D_FF2)), _full((D_FF, D_MODEL), True),
                  _full((1, D_MODEL)), _rows(tm, D_MODEL)],
        out_specs=[_rows(tm, D_FF), _rows(tm, D_MODEL), _full((8, LANES)), _full((1, D_MODEL))],
        out_shape=[jax.ShapeDtypeStruct((T, D_FF), BF16), jax.ShapeDtypeStruct((T, D_MODEL), F32),
                   jax.ShapeDtypeStruct((8, LANES), F32), jax.ShapeDtypeStruct((1, D_MODEL), F32)],
        scratch_shapes=[pltpu.VMEM((8, D_FF2), F32)],
        compiler_params=_cp("arbitrary"),
    )(a, x2, wc, bc, w_down, g3, tgt)


def _bwd_ffn(dx3, a, w_down, wc, bc):
    T = dx3.shape[0]
    tm = min(TM, T)
    nt = T // tm

    def body(dx3_ref, a_ref, halo_ref, wd_ref, wc_ref, bc_ref, da_ref, dwc_ref, dbc_ref, nxt):
        r = pl.program_id(0)

        @pl.when(r == 0)
        def _():
            nxt[...] = jnp.zeros_like(nxt)
            dwc_ref[...] = jnp.zeros_like(dwc_ref)
            dbc_ref[...] = jnp.zeros_like(dbc_ref)

        first = r == nt - 1
        dxb = dx3_ref[...].astype(BF16)
        rowi = lax.broadcasted_iota(jnp.int32, (tm, CW), 0)

        def taps(cs):
            a0 = a_ref[:, cs]
            prev = jnp.where(first, 0.0, halo_ref[:, cs])
            am1, am2 = _conv_taps(a0, prev)
            ac = wc_ref[0:1, cs] * am2 + wc_ref[1:2, cs] * am1 + wc_ref[2:3, cs] * a0 + bc_ref[:, cs]
            return ac, a0, am1, am2

        def back(cs, dac, a0, am1, am2):
            dbc_ref[:, cs] += jnp.sum(dac, axis=0, keepdims=True)
            dwc_ref[0:1, cs] += jnp.sum(dac * am2, axis=0, keepdims=True)
            dwc_ref[1:2, cs] += jnp.sum(dac * am1, axis=0, keepdims=True)
            dwc_ref[2:3, cs] += jnp.sum(dac * a0, axis=0, keepdims=True)
            nx = nxt[:, cs]
            dp1 = jnp.where(rowi == tm - 1, nx[0:1], pltpu.roll(dac, tm - 1, axis=0))
            dp2 = jnp.where(rowi == tm - 2, nx[0:1], jnp.where(rowi == tm - 1, nx[1:2], pltpu.roll(dac, tm - 2, axis=0)))
            da_ref[:, cs] = (wc_ref[2:3, cs] * dac + wc_ref[1:2, cs] * dp1 + wc_ref[0:1, cs] * dp2).astype(BF16)
            nxt[:, cs] = dac[0:8]

        for ci in range(D_FF // CW):
            gs = slice(ci * CW, (ci + 1) * CW)
            vs = slice(D_FF + ci * CW, D_FF + (ci + 1) * CW)
            dy = _dot_nt(dxb, wd_ref[gs, :])
            ag, g0, g1, g2 = taps(gs)
            av, v0, v1, v2 = taps(vs)
            sg = jax.nn.sigmoid(ag)
            back(vs, dy * (ag * sg), v0, v1, v2)
            back(gs, dy * av * (sg * (1.0 + ag * (1.0 - sg))), g0, g1, g2)

    halo = pl.BlockSpec((8, D_FF2), lambda i: (jnp.maximum((nt - 1 - i) * (tm // 8) - 1, 0), 0))
    return pl.pallas_call(
        body, name="bwd_ffn", grid=(nt,),
        in_specs=[_rows(tm, D_MODEL, nt), _rows(tm, D_FF2, nt), halo, _full((D_FF, D_MODEL), True), _full((3, D_FF2)),
                  _full((1, D_FF2))],
        out_specs=[_rows(tm, D_FF2, nt), _full((3, D_FF2)), _full((1, D_FF2))],
        out_shape=[jax.ShapeDtypeStruct((T, D_FF2), BF16), jax.ShapeDtypeStruct((3, D_FF2), F32),
                   jax.ShapeDtypeStruct((1, D_FF2), F32)],
        scratch_shapes=[pltpu.VMEM((8, D_FF2), F32)],
        compiler_params=_cp("arbitrary"),
    )(dx3, a, a, w_down, wc, bc)


def _bwd_mid(da, w_up, x2, g2, dx3, w_out, ob):
    T = x2.shape[0]
    tm = min(TM, T)

    def body(da_ref, wu_ref, x2_ref, g_ref, dx3_ref, wo_ref, ob_ref, dx2_ref, doa_ref, dob_ref, dg2_ref):
        @pl.when(pl.program_id(0) == 0)
        def _():
            dg2_ref[...] = jnp.zeros_like(dg2_ref)

        dh2 = _dot_nt(da_ref[...], wu_ref[...])
        r, n = _rms(x2_ref[...])
        dg2_ref[...] += jnp.sum(dh2 * n, axis=0, keepdims=True)
        dx2 = dx3_ref[...] + _rms_bwd(dh2, n, r, g_ref[...])
        dx2_ref[...] = dx2
        doab = _dot_nt(dx2.astype(BF16), wo_ref[...])
        doa_ref[...] = doab[:, :D_A]
        dob = doab[:, D_A:]
        prod = dob * ob_ref[...]
        for hd in range(N_HEADS):
            sl = _head_sl(hd)
            hi, mid, lo = _split3(-jnp.sum(prod[:, sl], axis=-1, keepdims=True))
            ext = _lanes(tm, HEAD_DIM, {0: hi, 1: mid, 2: lo})
            dob_ref[hd] = jnp.concatenate([dob[:, sl].astype(BF16), ext.astype(BF16)], axis=-1)

    return pl.pallas_call(
        body, name="bwd_mid", grid=(T // tm,),
        in_specs=[_rows(tm, D_FF2), _full((D_MODEL, D_FF2), True), _rows(tm, D_MODEL), _full((1, D_MODEL)), _rows(tm, D_MODEL),
                  _full((D_MODEL, D_MODEL), True), _rows(tm, D_B)],
        out_specs=[_rows(tm, D_MODEL), _rows(tm, D_A), pl.BlockSpec((N_HEADS, tm, LANES), lambda i: (0, i, 0)),
                   _full((1, D_MODEL))],
        out_shape=[jax.ShapeDtypeStruct((T, D_MODEL), F32), jax.ShapeDtypeStruct((T, D_A), F32),
                   jax.ShapeDtypeStruct((N_HEADS, T, LANES), BF16), jax.ShapeDtypeStruct((1, D_MODEL), F32)],
        compiler_params=_cp("arbitrary"),
    )(da, w_up, x2, g2, dx3, w_out, ob)


def _bwd_in(dzuv, dqa, dka, dva, fl, x, dx2, w_in_p, g1):
    T = x.shape[0]
    tm = min(TM, T)
    nt = T // tm

    def body(dzuv_ref, dqa_ref, dka_ref, dva_ref, fl_ref, x_ref, dx2_ref, w_ref, g_ref,
             gx_ref, dz_ref, dg1_ref, dfb_ref, carry):
        @pl.when(pl.program_id(0) == 0)
        def _():
            carry[...] = jnp.zeros_like(carry)
            dg1_ref[...] = jnp.zeros_like(dg1_ref)
            dfb_ref[...] = jnp.zeros_like(dfb_ref)

        dc = _lanes(tm, LANES, {hd: dqa_ref[hd][:, L_ROW:L_ROW + 1] - dka_ref[hd][:, L_COL:L_COL + 1] for hd in range(N_HEADS)})
        later = (lax.broadcasted_iota(jnp.int32, (tm, tm), 1) >= lax.broadcasted_iota(jnp.int32, (tm, tm), 0)).astype(F32)
        dls = _dot_f32(later, dc) + carry[...]
        carry[...] = dls[0:1, :]
        dzf = dls * jax.nn.sigmoid(-fl_ref[...])
        dfb_ref[...] += jnp.sum(dzf, axis=0, keepdims=True)
        dzq = jnp.concatenate([dqa_ref[hd][:, :HEAD_DIM] for hd in range(N_HEADS)], axis=-1) * (HEAD_DIM ** -0.5)
        dzk = jnp.concatenate([dka_ref[hd][:, :HEAD_DIM] for hd in range(N_HEADS)], axis=-1)
        dzv = jnp.concatenate([dva_ref[hd][:, :HEAD_DIM] for hd in range(N_HEADS)], axis=-1)
        dz = jnp.concatenate([dzuv_ref[...], dzq.astype(BF16), dzk.astype(BF16), dzv, dzf.astype(BF16)], axis=-1)
        dz_ref[...] = dz
        dh1 = _dot_nt(dz, w_ref[...])
        r, n = _rms(x_ref[...])
        dg1_ref[...] += jnp.sum(dh1 * n, axis=0, keepdims=True)
        gx_ref[...] = dx2_ref[...] + _rms_bwd(dh1, n, r, g_ref[...])

    rv = functools.partial(_rows, tm, rev_nt=nt)
    heads = pl.BlockSpec((N_HEADS, tm, LANES), lambda i: (0, nt - 1 - i, 0))
    return pl.pallas_call(
        body, name="bwd_in", grid=(nt,),
        in_specs=[rv(2 * D_A), heads, heads, heads, rv(LANES), rv(D_MODEL), rv(D_MODEL),
                  _full((D_MODEL, D_IN_PAD), True), _full((1, D_MODEL))],
        out_specs=[rv(D_MODEL), rv(D_IN_PAD), _full((1, D_MODEL)), _full((1, LANES))],
        out_shape=[jax.ShapeDtypeStruct((T, D_MODEL), F32), jax.ShapeDtypeStruct((T, D_IN_PAD), BF16),
                   jax.ShapeDtypeStruct((1, D_MODEL), F32), jax.ShapeDtypeStruct((1, LANES), F32)],
        scratch_shapes=[pltpu.VMEM((1, LANES), F32)],
        compiler_params=_cp("arbitrary"),
    )(dzuv, dqa, dka, dva, fl, x, dx2, w_in_p, g1)


def _matmul_tn(a, b, tmm, tn, name):
    T, M = a.shape
    N = b.shape[1]
    tk = min(512, T)

    def body(a_ref, b_ref, o_ref):
        @pl.when(pl.program_id(2) == 0)
        def _():
            o_ref[...] = jnp.zeros_like(o_ref)

        o_ref[...] += _dot_tn(a_ref[...].astype(BF16), b_ref[...].astype(BF16))

    return pl.pallas_call(
        body, name=name, grid=(M // tmm, N // tn, T // tk),
        in_specs=[pl.BlockSpec((tk, tmm), lambda i, j, k: (k, i)), pl.BlockSpec((tk, tn), lambda i, j, k: (k, j))],
        out_specs=pl.BlockSpec((tmm, tn), lambda i, j, k: (i, j)),
        out_shape=jax.ShapeDtypeStruct((M, N), F32),
        compiler_params=_cp("arbitrary", "arbitrary", "arbitrary"),
    )(a, b)


def _exchange(gather, scatter, name):
    n_g, n_s = len(gather), len(scatter)
    n = n_g + n_s

    def body(*refs):
        ins, outs = refs[:n], refs[n:2 * n]
        send_sems, recv_sems, local_sems = refs[2 * n:]
        x, y, c = (lax.axis_index(ax) for ax in MESH_AXES)
        me = 4 * x + 2 * y + c
        sibling = (x, y, 1 - c)
        chips = [(1 - x, y), (x, 1 - y), (1 - x, 1 - y)]
        peers = [sibling] + [(*chip, c) for chip in chips] + [(*chip, 1 - c) for chip in chips]

        def index(dev):
            return 4 * dev[0] + 2 * dev[1] + dev[2]

        def remote(k, src, dst, to):
            return pltpu.make_async_remote_copy(src_ref=src, dst_ref=dst, send_sem=send_sems.at[k], recv_sem=recv_sems.at[k],
                                                device_id=to, device_id_type=pl.DeviceIdType.MESH)

        local, sends, forwards = [], [], []
        for a in range(n_g):
            src, out, base = ins[a], outs[a], 7 * a
            local.append(pltpu.make_async_copy(src, out.at[me], local_sems.at[a]))
            sends.append(remote(base, src, out.at[me], sibling))
            for j, chip in enumerate(chips):
                sends.append(remote(base + 1 + j, src, out.at[me], (*chip, c)))
        for a in range(n_g, n):
            src, out, base = ins[a], outs[a], 7 * a
            local.append(pltpu.make_async_copy(src.at[me], out.at[me], local_sems.at[a]))
            for k, peer in enumerate(peers):
                sends.append(remote(base + k, src.at[index(peer)], out.at[me], peer))
        for cp in local + sends:
            cp.start()
        for a in range(n_g):
            out, base = outs[a], 7 * a
            for j, chip in enumerate(chips):
                slot = out.at[index((*chip, c))]
                remote(base + 1 + j, slot, slot, (*chip, c)).wait_recv()
                fwd = remote(base + 4 + j, slot, slot, sibling)
                fwd.start()
                forwards.append(fwd)
        for a in range(n_g):
            out, base = outs[a], 7 * a
            slot = out.at[index(sibling)]
            remote(base, slot, slot, sibling).wait_recv()
            for j, chip in enumerate(chips):
                slot = out.at[index((*chip, 1 - c))]
                remote(base + 4 + j, slot, slot, sibling).wait_recv()
        for a in range(n_g, n):
            out, base = outs[a], 7 * a
            for k, peer in enumerate(peers):
                slot = out.at[index(peer)]
                remote(base + k, slot, slot, peer).wait_recv()
        for cp in sends + forwards:
            cp.wait_send()
        for cp in local:
            cp.wait()

    out_shape = [jax.ShapeDtypeStruct((N_DEV, *g.shape), g.dtype) for g in gather]
    out_shape += [jax.ShapeDtypeStruct(s.shape, s.dtype) for s in scatter]
    return pl.pallas_call(
        body, name=name,
        in_specs=[ANY] * n, out_specs=[ANY] * n, out_shape=out_shape,
        scratch_shapes=[pltpu.SemaphoreType.DMA((7 * n,)), pltpu.SemaphoreType.DMA((7 * n,)), pltpu.SemaphoreType.DMA((n,))],
    )(*gather, *scatter)


def _adamw(w, g, m, v):
    m = ADAM_B1 * m + (1.0 - ADAM_B1) * g
    v = ADAM_B2 * v + (1.0 - ADAM_B2) * jnp.square(g)
    m_hat = m / (1.0 - ADAM_B1 ** ADAM_STEP)
    v_hat = v / (1.0 - ADAM_B2 ** ADAM_STEP)
    delta = -ADAM_LR * (m_hat / (jnp.sqrt(v_hat) + ADAM_EPS) + ADAM_WD * w)
    return delta, m, v


def _adamw_shard(w, m, v, own, recv, tr, name):
    R, C = w.shape

    def body(w_ref, m_ref, v_ref, own_ref, recv_ref, g_ref, d_ref, nm_ref, nv_ref):
        x, y, c = (lax.axis_index(ax) for ax in MESH_AXES)
        me = 4 * x + 2 * y + c
        g = own_ref[...]
        for d in range(N_DEV):
            g = g + jnp.where(d == me, 0.0, recv_ref[d].astype(F32))
        g_ref[...] = g
        d_ref[...], nm_ref[...], nv_ref[...] = _adamw(w_ref[...], g, m_ref[...], v_ref[...])

    blk = pl.BlockSpec((tr, C), lambda i: (i, 0))
    return pl.pallas_call(
        body, name=name, grid=(R // tr,),
        in_specs=[blk, blk, blk, blk, pl.BlockSpec((N_DEV, tr, C), lambda i: (0, i, 0))],
        out_specs=[blk] * 4, out_shape=[jax.ShapeDtypeStruct((R, C), F32)] * 4,
        compiler_params=_cp("arbitrary"),
    )(w, m, v, own, recv)


def _adamw_small(w, m, v, gall):
    R = w.shape[0]
    tr = R // 4

    def body(w_ref, m_ref, v_ref, gall_ref, g_ref, d_ref, nm_ref, nv_ref):
        g = gall_ref[0]
        for d in range(1, N_DEV):
            g = g + gall_ref[d]
        g_ref[...] = g
        d_ref[...], nm_ref[...], nv_ref[...] = _adamw(w_ref[...], g, m_ref[...], v_ref[...])

    blk = pl.BlockSpec((tr, LANES), lambda i: (i, 0))
    return pl.pallas_call(
        body, name="adamw_small", grid=(R // tr,),
        in_specs=[blk, blk, blk, pl.BlockSpec((N_DEV, tr, LANES), lambda i: (0, i, 0))],
        out_specs=[blk] * 4, out_shape=[jax.ShapeDtypeStruct((R, LANES), F32)] * 4,
        compiler_params=_cp("arbitrary"),
    )(w, m, v, gall)


def _local_step(x, tgt, g1, w_in_p, fb, lg, sg_w, sg_b, w_out, g2, w_up, wc, bc, w_down, g3):
    pos_chunk = jnp.arange(SG_BLOCK) // SG_CHUNK
    maskf = (pos_chunk[:, None] >= pos_chunk[None, :]).astype(F32)
    wm = (sg_w * maskf[None]).astype(BF16)
    wmt = jnp.swapaxes(wm, 1, 2)
    bfull = jnp.repeat(sg_b.T, HEAD_DIM, axis=1)
    hsel = jnp.repeat(jnp.eye(N_HEADS, dtype=F32), HEAD_DIM, axis=1)

    zuv, qa, ka, va, fl, h1 = _fwd_in(x, g1, w_in_p, fb)
    oa = _gmlp_fwd(zuv, lg, wm, bfull)
    ob, lse = _fox_fwd(qa, ka, va)
    x2, h2, a = _fwd_mid(x, oa, ob, w_out, g2, w_up)
    yff, dx3, loss, dg3 = _fwd_ffn(a, x2, wc, bc, w_down, g3, tgt)
    da, dwc, dbc = _bwd_ffn(dx3, a, w_down, wc, bc)
    dx2, doa, dob, dg2 = _bwd_mid(da, w_up, x2, g2, dx3, w_out, ob)
    dqa, dka, dva = _fox_bwd(qa, lse, dob, ka, va)
    dzuv, dwm, dsgb, dlg = _gmlp_bwd(zuv, doa, lg, wm, wmt, bfull, maskf, hsel)
    gx, dz, dg1, dfb = _bwd_in(dzuv, dqa, dka, dva, fl, x, dx2, w_in_p, g1)
    dwin = _matmul_tn(h1, dz, D_MODEL, D_IN_PAD // 3, "dw_in")
    dwout = jnp.concatenate([_matmul_tn(oa, dx2, D_A, D_MODEL, "dw_out_a"), _matmul_tn(ob, dx2, D_B, D_MODEL, "dw_out_b")],
                            axis=0)
    dwup = _matmul_tn(h2, da, D_MODEL, D_FF2 // 4, "dw_up")
    dwdown = _matmul_tn(yff, dx3, D_FF // 2, D_MODEL, "dw_down")
    small = dict(g1=dg1, fb=dfb, lg=dlg, sg_w=dwm, sg_b=dsgb, g2=dg2, bc=dbc, g3=dg3)
    big = dict(w_in=dwin[:, :D_IN], w_out=dwout, w_up=dwup, wc=dwc, w_down=dwdown)
    return loss[0, 0], gx, small, big


_SMALL = (("g1", 8), ("fb", 8), ("lg", 8), ("sg_w", 1024), ("sg_b", 8), ("g2", 8), ("bc", 48), ("g3", 8))


def _pack_small(parts):
    rows = []
    for name, nrow in _SMALL:
        flat = parts[name].astype(F32).reshape(-1)
        flat = jnp.pad(flat, (0, nrow * LANES - flat.shape[0]))
        rows.append(flat.reshape(nrow, LANES))
    return jnp.concatenate(rows, axis=0)


def _unpack_small(packed, shapes):
    out, r = {}, 0
    for name, nrow in _SMALL:
        size = math.prod(shapes[name])
        out[name] = packed[r:r + nrow].reshape(-1)[:size].reshape(shapes[name])
        r += nrow
    return out


def kernel(x, norm_mix_g, w_in, f_bias, sg_ln_g, sg_w, sg_b, w_out, norm_ffn_g, w_up, w_conv, b_conv, w_down, norm_final_g, loss_target, m_norm_mix_g, m_w_in, m_f_bias, m_sg_ln_g, m_sg_w, m_sg_b, m_w_out, m_norm_ffn_g, m_w_up, m_w_conv, m_b_conv, m_w_down, m_norm_final_g, v_norm_mix_g, v_w_in, v_f_bias, v_sg_ln_g, v_sg_w, v_sg_b, v_w_out, v_norm_ffn_g, v_w_up, v_w_conv, v_b_conv, v_w_down, v_norm_final_g):
    me = 4 * lax.axis_index("x") + 2 * lax.axis_index("y") + lax.axis_index("c")

    win_g, wout_g, wup_g, wdown_g, wc_g = _exchange(
        [w_in[0].astype(BF16), w_out[0].astype(BF16), w_up[0].astype(BF16), w_down[0].astype(BF16), w_conv[0]], [], "gather_weights")
    w_in_full = jnp.transpose(win_g, (1, 0, 2)).reshape(D_MODEL, D_IN)
    w_in_p = jnp.pad(w_in_full, ((0, 0), (0, D_IN_PAD - D_IN)))
    w_out_full = wout_g.reshape(D_MODEL, D_MODEL)
    w_up_full = jnp.transpose(wup_g, (1, 0, 2)).reshape(D_MODEL, D_FF2)
    w_down_full = wdown_g.reshape(D_FF, D_MODEL)
    wc_full = jnp.transpose(wc_g, (1, 0, 2)).reshape(3, D_FF2)

    fb = jnp.pad(f_bias, ((0, 0), (0, LANES - N_HEADS)))
    loss, gx, small, big = _local_step(
        x[0], loss_target[0], norm_mix_g, w_in_p, fb, sg_ln_g.reshape(1, D_A), sg_w[0], sg_b[0], w_out_full, norm_ffn_g,
        w_up_full, wc_full, b_conv, w_down_full, norm_final_g.reshape(1, D_MODEL))

    def col_shards(g):
        return jnp.transpose(g.reshape(g.shape[0], N_DEV, -1), (1, 0, 2))

    def row_shards(g):
        return g.reshape(N_DEV, -1, g.shape[1])

    shards = dict(w_in=col_shards(big["w_in"]), w_out=row_shards(big["w_out"]), w_up=col_shards(big["w_up"]),
                  wc=col_shards(big["wc"]), w_down=row_shards(big["w_down"]))
    order = ("w_in", "w_out", "w_up", "wc", "w_down")
    got = _exchange([_pack_small(small)], [shards[n].astype(BF16) for n in order], "exchange_grads")
    small_all, recv = got[0], dict(zip(order, got[1:]))

    weights = dict(w_in=(w_in, m_w_in, v_w_in, 256), w_out=(w_out, m_w_out, v_w_out, 128), w_up=(w_up, m_w_up, v_w_up, 256),
                   wc=(w_conv, m_w_conv, v_w_conv, 3), w_down=(w_down, m_w_down, v_w_down, 176))
    res = {}
    for n in order:
        w, m, v, tr = weights[n]
        own = lax.dynamic_index_in_dim(shards[n], me, axis=0, keepdims=False)
        res[n] = [r[None] for r in _adamw_shard(w[0], m[0], v[0], own, recv[n], tr, "adamw_" + n)]

    reps = dict(g1=(norm_mix_g, m_norm_mix_g, v_norm_mix_g), fb=(f_bias, m_f_bias, v_f_bias), lg=(sg_ln_g, m_sg_ln_g, v_sg_ln_g),
                sg_w=(sg_w, m_sg_w, v_sg_w), sg_b=(sg_b, m_sg_b, v_sg_b), g2=(norm_ffn_g, m_norm_ffn_g, v_norm_ffn_g),
                bc=(b_conv, m_b_conv, v_b_conv), g3=(norm_final_g, m_norm_final_g, v_norm_final_g))
    shapes = {n: t[0].shape for n, t in reps.items()}
    packed = _adamw_small(*[_pack_small({n: t[i] for n, t in reps.items()}) for i in range(3)], small_all)
    unpacked = [_unpack_small(p, shapes) for p in packed]
    for n in reps:
        res[n] = [u[n] for u in unpacked]

    loss = lax.psum(loss, MESH_AXES)
    names = ("g1", "w_in", "fb", "lg", "sg_w", "sg_b", "w_out", "g2", "w_up", "wc", "bc", "w_down", "g3")
    return (loss, gx[None], *[res[n][0] for n in names], *[res[n][1] for n in names], *[res[n][2] for n in names],
            *[res[n][3] for n in names])
```

```python
import functools
import math

import jax
import jax.numpy as jnp
from jax import lax
from jax.experimental import pallas as pl
from jax.experimental.pallas import tpu as pltpu

F32 = jnp.float32
BF16 = jnp.bfloat16

D_MODEL = 1024
HEAD_DIM = 64
N_HEADS = 8
D_A = 512
D_B = 512
D_IN = 2 * D_A + 3 * D_B + N_HEADS
D_IN_PAD = 2688
D_FF = 2816
D_FF2 = 2 * D_FF
SG_BLOCK = 128
SG_CHUNK = 64
EPS = 1e-6
N_DEV = 8
LANES = 128
NEG = -1e30
VMEM_LIMIT = 56 * 1024 * 1024

ADAM_LR = 0.001
ADAM_B1 = 0.9
ADAM_B2 = 0.999
ADAM_EPS = 1e-08
ADAM_WD = 0.01
ADAM_STEP = 10

TM = 256
TQ = 512
CW = 256

MESH_AXES = ("x", "y", "c")
ANY = pl.BlockSpec(memory_space=pl.ANY)


def _cp(*sem):
    return pltpu.CompilerParams(dimension_semantics=sem, vmem_limit_bytes=VMEM_LIMIT)


def _dot(a, b):
    return jnp.dot(a, b, preferred_element_type=F32)


def _dot_nt(a, b):
    return lax.dot_general(a, b, (((1,), (1,)), ((), ())), preferred_element_type=F32)


def _dot_tn(a, b):
    return lax.dot_general(a, b, (((0,), (0,)), ((), ())), preferred_element_type=F32)


def _dot_f32(a, b):
    return jnp.dot(a, b, precision=lax.Precision.HIGHEST, preferred_element_type=F32)


def _gelu(z):
    return 0.5 * z * (1.0 + lax.erf(z * (1.0 / math.sqrt(2.0))))


def _gelu_grad(z):
    return 0.5 * (1.0 + lax.erf(z * (1.0 / math.sqrt(2.0)))) + z * jnp.exp(-0.5 * z * z) * (1.0 / math.sqrt(2.0 * math.pi))


def _log_sigmoid(x):
    return jnp.minimum(x, 0.0) - jnp.log1p(jnp.exp(-jnp.abs(x)))


def _rms(x):
    r = lax.rsqrt(jnp.mean(x * x, axis=-1, keepdims=True) + EPS)
    return r, x * r


def _rms_bwd(dy, n, r, g):
    dn = dy * g
    return r * (dn - n * jnp.mean(dn * n, axis=-1, keepdims=True))


def _full(shape, single=False):
    nd = len(shape)
    if single:
        return pl.BlockSpec(shape, lambda *_: (0,) * nd, pipeline_mode=pl.Buffered(1))
    return pl.BlockSpec(shape, lambda *_: (0,) * nd)


def _rows(tm, cols, rev_nt=None):
    if rev_nt is None:
        return pl.BlockSpec((tm, cols), lambda i: (i, 0))
    return pl.BlockSpec((tm, cols), lambda i: (rev_nt - 1 - i, 0))


def _head_sl(h):
    return slice(HEAD_DIM * h, HEAD_DIM * (h + 1))


L_ROW = HEAD_DIM
L_COL = HEAD_DIM + 3
L_LSE = HEAD_DIM + 6


def _split3(x):
    hi = x.astype(BF16).astype(F32)
    mid = (x - hi).astype(BF16).astype(F32)
    lo = (x - hi - mid).astype(BF16).astype(F32)
    return hi, mid, lo


def _lanes(rows, width, parts):
    lane = lax.broadcasted_iota(jnp.int32, (rows, width), 1)
    out = jnp.zeros((rows, width), F32)
    for at, val in parts.items():
        out = jnp.where(lane == at, val, out)
    return out


def _fwd_in(x, g1, w_in_p, fb):
    T = x.shape[0]
    tm = min(TM, T)

    def body(x_ref, g_ref, w_ref, fb_ref, zuv_ref, qa_ref, ka_ref, va_ref, fl_ref, h1_ref, carry):
        @pl.when(pl.program_id(0) == 0)
        def _():
            carry[...] = jnp.zeros_like(carry)

        r, n = _rms(x_ref[...])
        h = (n * g_ref[...]).astype(BF16)
        h1_ref[...] = h
        z = _dot(h, w_ref[...])
        zuv_ref[...] = z[:, :2 * D_A]
        o = 2 * D_A
        fl = z[:, o + 3 * D_B:] + fb_ref[...]
        fl_ref[...] = fl
        tri = (lax.broadcasted_iota(jnp.int32, (tm, tm), 0) >= lax.broadcasted_iota(jnp.int32, (tm, tm), 1)).astype(F32)
        c = _dot_f32(tri, _log_sigmoid(fl)) + carry[...]
        carry[...] = c[tm - 1:tm, :]
        ext_v = _lanes(tm, HEAD_DIM, {0: 1.0, 1: 1.0, 2: 1.0}).astype(BF16)
        for hd in range(N_HEADS):
            hi, mid, lo = _split3(c[:, hd:hd + 1])
            ext_q = _lanes(tm, HEAD_DIM, {0: hi, 1: mid, 2: lo, 3: 1.0, 4: 1.0, 5: 1.0})
            ext_k = _lanes(tm, HEAD_DIM, {0: 1.0, 1: 1.0, 2: 1.0, 3: -hi, 4: -mid, 5: -lo, 6: 1.0, 7: 1.0, 8: 1.0})
            qh = z[:, o + HEAD_DIM * hd:o + HEAD_DIM * (hd + 1)] * (HEAD_DIM ** -0.5)
            kh = z[:, o + D_B + HEAD_DIM * hd:o + D_B + HEAD_DIM * (hd + 1)]
            vh = z[:, o + 2 * D_B + HEAD_DIM * hd:o + 2 * D_B + HEAD_DIM * (hd + 1)]
            qa_ref[hd] = jnp.concatenate([qh.astype(BF16), ext_q.astype(BF16)], axis=-1)
            ka_ref[hd] = jnp.concatenate([kh.astype(BF16), ext_k.astype(BF16)], axis=-1)
            va_ref[hd] = jnp.concatenate([vh.astype(BF16), ext_v], axis=-1)

    heads = pl.BlockSpec((N_HEADS, tm, LANES), lambda i: (0, i, 0))
    aug = jax.ShapeDtypeStruct((N_HEADS, T, LANES), BF16)
    return pl.pallas_call(
        body, name="fwd_in", grid=(T // tm,),
        in_specs=[_rows(tm, D_MODEL), _full((1, D_MODEL)), _full((D_MODEL, D_IN_PAD), True), _full((1, LANES))],
        out_specs=[_rows(tm, 2 * D_A), heads, heads, heads, _rows(tm, LANES), _rows(tm, D_MODEL)],
        out_shape=[jax.ShapeDtypeStruct((T, 2 * D_A), F32), aug, aug, aug, jax.ShapeDtypeStruct((T, LANES), F32),
                   jax.ShapeDtypeStruct((T, D_MODEL), BF16)],
        scratch_shapes=[pltpu.VMEM((1, LANES), F32)],
        compiler_params=_cp("arbitrary"),
    )(x, g1, w_in_p, fb)


def _ln_head(vh, lgh):
    mu = jnp.mean(vh, axis=-1, keepdims=True)
    d = vh - mu
    rstd = lax.rsqrt(jnp.mean(d * d, axis=-1, keepdims=True) + EPS)
    vhat = d * rstd
    return vhat, rstd, vhat * lgh


def _gmlp_fwd(zuv, lg, wm, bfull):
    T = zuv.shape[0]
    tg = min(TM, T)
    nb = tg // SG_BLOCK

    def body(zuv_ref, lg_ref, wm_ref, b_ref, oa_ref):
        for h in range(N_HEADS):
            sl = _head_sl(h)
            u = _gelu(zuv_ref[:, sl])
            v = _gelu(zuv_ref[:, D_A + HEAD_DIM * h:D_A + HEAD_DIM * (h + 1)])
            _, _, vn = _ln_head(v, lg_ref[:, sl])
            vn = vn.astype(BF16)
            for n in range(nb):
                rs = slice(n * SG_BLOCK, (n + 1) * SG_BLOCK)
                mixed = _dot(wm_ref[h], vn[rs]) + b_ref[:, sl]
                oa_ref[rs, sl] = (u[rs] * mixed).astype(BF16)

    return pl.pallas_call(
        body, name="gmlp_fwd", grid=(T // tg,),
        in_specs=[_rows(tg, 2 * D_A), _full((1, D_A)), _full((N_HEADS, SG_BLOCK, SG_BLOCK)), _full((SG_BLOCK, D_A))],
        out_specs=_rows(tg, D_A),
        out_shape=jax.ShapeDtypeStruct((T, D_A), BF16),
        compiler_params=_cp("arbitrary"),
    )(zuv, lg, wm, bfull)


def _gmlp_bwd(zuv, doa, lg, wm, wmt, bfull, maskf, hsel):
    T = zuv.shape[0]
    tg = min(TM, T)
    nb = tg // SG_BLOCK
    nt = T // tg

    def body(zuv_ref, doa_ref, lg_ref, wm_ref, wmt_ref, b_ref, mask_ref, hsel_ref,
             dzuv_ref, dwm_ref, dsgb_ref, dlg_ref, dbacc):
        i = pl.program_id(0)

        @pl.when(i == 0)
        def _():
            dwm_ref[...] = jnp.zeros_like(dwm_ref)
            dlg_ref[...] = jnp.zeros_like(dlg_ref)
            dbacc[...] = jnp.zeros_like(dbacc)

        for h in range(N_HEADS):
            sl = _head_sl(h)
            zu = zuv_ref[:, sl]
            zv = zuv_ref[:, D_A + HEAD_DIM * h:D_A + HEAD_DIM * (h + 1)]
            u = _gelu(zu)
            lgh = lg_ref[:, sl]
            vhat, rstd, vn = _ln_head(_gelu(zv), lgh)
            vn = vn.astype(BF16)
            doa_h = doa_ref[:, sl]
            du_parts, dvn_parts = [], []
            for n in range(nb):
                rs = slice(n * SG_BLOCK, (n + 1) * SG_BLOCK)
                mixed = _dot(wm_ref[h], vn[rs]) + b_ref[:, sl]
                du_parts.append(doa_h[rs] * mixed)
                dmixed = doa_h[rs] * u[rs]
                dbacc[:, sl] += dmixed
                dmb = dmixed.astype(BF16)
                dwm_ref[h] += _dot_nt(dmb, vn[rs]) * mask_ref[...]
                dvn_parts.append(_dot(wmt_ref[h], dmb))
            du = jnp.concatenate(du_parts, axis=0) if nb > 1 else du_parts[0]
            dvn = jnp.concatenate(dvn_parts, axis=0) if nb > 1 else dvn_parts[0]
            dlg_ref[:, sl] += jnp.sum(dvn * vhat, axis=0, keepdims=True)
            dvhat = dvn * lgh
            dv = rstd * (dvhat - jnp.mean(dvhat, axis=-1, keepdims=True)
                         - vhat * jnp.mean(dvhat * vhat, axis=-1, keepdims=True))
            dzuv_ref[:, sl] = (du * _gelu_grad(zu)).astype(BF16)
            dzuv_ref[:, D_A + HEAD_DIM * h:D_A + HEAD_DIM * (h + 1)] = (dv * _gelu_grad(zv)).astype(BF16)

        @pl.when(i == nt - 1)
        def _():
            dsgb_ref[...] = lax.dot_general(hsel_ref[...], dbacc[...], (((1,), (1,)), ((), ())),
                                            precision=lax.Precision.HIGHEST, preferred_element_type=F32)

    return pl.pallas_call(
        body, name="gmlp_bwd", grid=(nt,),
        in_specs=[_rows(tg, 2 * D_A), _rows(tg, D_A), _full((1, D_A)), _full((N_HEADS, SG_BLOCK, SG_BLOCK)),
                  _full((N_HEADS, SG_BLOCK, SG_BLOCK)), _full((SG_BLOCK, D_A)), _full((SG_BLOCK, SG_BLOCK)),
                  _full((N_HEADS, D_A))],
        out_specs=[_rows(tg, 2 * D_A), _full((N_HEADS, SG_BLOCK, SG_BLOCK)), _full((N_HEADS, SG_BLOCK)), _full((1, D_A))],
        out_shape=[jax.ShapeDtypeStruct((T, 2 * D_A), BF16), jax.ShapeDtypeStruct((N_HEADS, SG_BLOCK, SG_BLOCK), F32),
                   jax.ShapeDtypeStruct((N_HEADS, SG_BLOCK), F32), jax.ShapeDtypeStruct((1, D_A), F32)],
        scratch_shapes=[pltpu.VMEM((SG_BLOCK, D_A), F32)],
        compiler_params=_cp("arbitrary"),
    )(zuv, doa, lg, wm, wmt, bfull, maskf, hsel)


def _fox_fwd(qa, ka, va, side):
    T = qa.shape[1]
    tq = min(TQ, T)
    nq = T // tq
    n = side.n

    def body(qa_ref, ka_ref, va_ref, *rest):
        ins, (o_ref, lse_ref), outs, sems = rest[:n], rest[n:n + 2], rest[n + 2:2 * n + 2], rest[2 * n + 2:]
        i = pl.program_id(1)
        if n:
            @pl.when((pl.program_id(0) == 0) & (i == 0))
            def _():
                side.start(ins, outs, sems)
        row = lax.broadcasted_iota(jnp.int32, (tq, tq), 0)
        col = lax.broadcasted_iota(jnp.int32, (tq, tq), 1)
        qs = [qa_ref[0], qa_ref[1]]

        def step(j, carry, diag):
            off = pl.multiple_of(j * tq, tq)
            out = []
            for hh in range(2):
                m, acc = carry[hh]
                s = _dot_nt(qs[hh], ka_ref[hh, pl.ds(off, tq), :])
                if diag:
                    s = jnp.where(col <= row, s, NEG)
                m_new = jnp.maximum(m, jnp.max(s, axis=-1, keepdims=True))
                pr = jnp.exp(s - m_new)
                acc = jnp.exp(m - m_new) * acc + _dot(pr.astype(BF16), va_ref[hh, pl.ds(off, tq), :])
                out.append((m_new, acc))
            return tuple(out)

        init = ((jnp.full((tq, 1), NEG, F32), jnp.zeros((tq, LANES), F32)),) * 2
        carry = lax.fori_loop(0, i, functools.partial(step, diag=False), init)
        carry = step(i, carry, True)
        for hh in range(2):
            m, acc = carry[hh]
            l = acc[:, L_ROW:L_ROW + 1]
            o_ref[:, _head_sl(hh)] = acc[:, :HEAD_DIM] / l
            hi, mid, lo = _split3(-(m + jnp.log(l)))
            lse_ref[hh] = _lanes(tq, LANES, {L_LSE: hi, L_LSE + 1: mid, L_LSE + 2: lo}).astype(BF16)
        if n:
            @pl.when((pl.program_id(0) == N_HEADS // 2 - 1) & (i == nq - 1))
            def _():
                side.finish(ins, outs, sems)

    tile = pl.BlockSpec((2, tq, LANES), lambda p, i: (p, i, 0))
    seq = pl.BlockSpec((2, T, LANES), lambda p, i: (p, 0, 0))
    res = pl.pallas_call(
        body, name="fox_fwd", grid=(N_HEADS // 2, nq),
        in_specs=[tile, seq, seq, *side.specs],
        out_specs=[pl.BlockSpec((tq, LANES), lambda p, i: (i, p)), tile, *side.specs],
        out_shape=[jax.ShapeDtypeStruct((T, D_B), F32), jax.ShapeDtypeStruct((N_HEADS, T, LANES), BF16), *side.out_shape],
        scratch_shapes=side.scratch,
        compiler_params=_cp("arbitrary", "arbitrary"),
    )(qa, ka, va, *side.operands)
    return res[0], res[1], res[2:]


def _fox_bwd(qa, lse, doa, ka, va, side):
    T = qa.shape[1]
    tq = min(TQ, T)
    nq = T // tq
    n = side.n

    def body(qa_ref, lse_ref, doa_ref, ka_ref, va_ref, *rest):
        ins, (dqa_ref, dka_ref, dva_ref), outs, sems = rest[:n], rest[n:n + 3], rest[n + 3:2 * n + 3], rest[2 * n + 3:]
        j = pl.program_id(1)
        if n:
            @pl.when((pl.program_id(0) == 0) & (j == 0))
            def _():
                side.start(ins, outs, sems)

        @pl.when(j == 0)
        def _():
            dqa_ref[...] = jnp.zeros_like(dqa_ref)

        row = lax.broadcasted_iota(jnp.int32, (tq, tq), 0)
        col = lax.broadcasted_iota(jnp.int32, (tq, tq), 1)
        ks = [ka_ref[0], ka_ref[1]]
        vs = [va_ref[0], va_ref[1]]

        def step(i, carry, diag):
            off = pl.multiple_of(i * tq, tq)
            out = []
            for hh in range(2):
                dk, dv = carry[hh]
                qi = qa_ref[hh, pl.ds(off, tq), :] + lse_ref[hh, pl.ds(off, tq), :]
                doi = doa_ref[hh, pl.ds(off, tq), :]
                st = _dot_nt(ks[hh], qi)
                if diag:
                    st = jnp.where(row <= col, st, NEG)
                pt = jnp.exp(st)
                dv = dv + _dot(pt.astype(BF16), doi)
                dsb = (pt * _dot_nt(vs[hh], doi)).astype(BF16)
                dk = dk + _dot(dsb, qi)
                dqa_ref[hh, pl.ds(off, tq), :] += _dot_tn(dsb, ks[hh])
                out.append((dk, dv))
            return tuple(out)

        init = ((jnp.zeros((tq, LANES), F32), jnp.zeros((tq, LANES), F32)),) * 2
        carry = step(j, init, True)
        carry = lax.fori_loop(j + 1, nq, functools.partial(step, diag=False), carry)
        for hh in range(2):
            dka_ref[hh] = carry[hh][0]
            dva_ref[hh] = carry[hh][1].astype(BF16)
        if n:
            @pl.when((pl.program_id(0) == N_HEADS // 2 - 1) & (j == nq - 1))
            def _():
                side.finish(ins, outs, sems)

    tile = pl.BlockSpec((2, tq, LANES), lambda p, j: (p, j, 0))
    seq = pl.BlockSpec((2, T, LANES), lambda p, j: (p, 0, 0))
    res = pl.pallas_call(
        body, name="fox_bwd", grid=(N_HEADS // 2, nq),
        in_specs=[seq, seq, seq, tile, tile, *side.specs],
        out_specs=[seq, tile, tile, *side.specs],
        out_shape=[jax.ShapeDtypeStruct((N_HEADS, T, LANES), F32), jax.ShapeDtypeStruct((N_HEADS, T, LANES), F32),
                   jax.ShapeDtypeStruct((N_HEADS, T, LANES), BF16), *side.out_shape],
        scratch_shapes=side.scratch,
        compiler_params=_cp("arbitrary", "arbitrary"),
    )(qa, lse, doa, ka, va, *side.operands)
    return res[0], res[1], res[2], res[3:]


def _fwd_mid(x, oa, ob, w_out, g2, w_up):
    T = x.shape[0]
    tm = min(TM, T)

    def body(x_ref, oa_ref, ob_ref, wo_ref, g_ref, wu_ref, x2_ref, h2_ref, a_ref):
        oab = jnp.concatenate([oa_ref[...], ob_ref[...].astype(BF16)], axis=-1)
        x2 = x_ref[...] + _dot(oab, wo_ref[...])
        x2_ref[...] = x2
        _, n = _rms(x2)
        h2 = (n * g_ref[...]).astype(BF16)
        h2_ref[...] = h2
        a_ref[...] = _dot(h2, wu_ref[...])

    return pl.pallas_call(
        body, name="fwd_mid", grid=(T // tm,),
        in_specs=[_rows(tm, D_MODEL), _rows(tm, D_A), _rows(tm, D_B), _full((D_MODEL, D_MODEL), True), _full((1, D_MODEL)),
                  _full((D_MODEL, D_FF2), True)],
        out_specs=[_rows(tm, D_MODEL), _rows(tm, D_MODEL), _rows(tm, D_FF2)],
        out_shape=[jax.ShapeDtypeStruct((T, D_MODEL), F32), jax.ShapeDtypeStruct((T, D_MODEL), BF16),
                   jax.ShapeDtypeStruct((T, D_FF2), F32)],
        compiler_params=_cp("arbitrary"),
    )(x, oa, ob, w_out, g2, w_up)


def _conv_taps(a0, prev):
    rowi = lax.broadcasted_iota(jnp.int32, a0.shape, 0)
    am1 = jnp.where(rowi == 0, prev[7:8], pltpu.roll(a0, 1, axis=0))
    am2 = jnp.where(rowi == 0, prev[6:7], jnp.where(rowi == 1, prev[7:8], pltpu.roll(a0, 2, axis=0)))
    return am1, am2


def _fwd_ffn(a, x2, wc, bc, w_down, g3, tgt):
    T = x2.shape[0]
    tm = min(TM, T)

    def body(a_ref, x2_ref, wc_ref, bc_ref, wd_ref, g_ref, tgt_ref, yff_ref, dx3_ref, loss_ref, dg3_ref, carry):
        @pl.when(pl.program_id(0) == 0)
        def _():
            carry[...] = jnp.zeros_like(carry)
            loss_ref[...] = jnp.zeros_like(loss_ref)
            dg3_ref[...] = jnp.zeros_like(dg3_ref)

        def conv(cs):
            a0 = a_ref[:, cs]
            am1, am2 = _conv_taps(a0, carry[:, cs])
            return wc_ref[0:1, cs] * am2 + wc_ref[1:2, cs] * am1 + wc_ref[2:3, cs] * a0 + bc_ref[:, cs]

        x3 = x2_ref[...]
        for ci in range(D_FF // CW):
            gs = slice(ci * CW, (ci + 1) * CW)
            ag = conv(gs)
            av = conv(slice(D_FF + ci * CW, D_FF + (ci + 1) * CW))
            yb = (ag * jax.nn.sigmoid(ag) * av).astype(BF16)
            yff_ref[:, gs] = yb
            x3 = x3 + _dot(yb, wd_ref[gs, :])
        carry[...] = a_ref[tm - 8:tm, :]
        r, n = _rms(x3)
        g = g_ref[...]
        diff = n * g - tgt_ref[...]
        loss_ref[...] += (0.5 / D_MODEL) * jnp.sum(diff * diff)
        dout = diff * (1.0 / D_MODEL)
        dg3_ref[...] += jnp.sum(dout * n, axis=0, keepdims=True)
        dx3_ref[...] = _rms_bwd(dout, n, r, g)

    return pl.pallas_call(
        body, name="fwd_ffn", grid=(T // tm,),
        in_specs=[_rows(tm, D_FF2), _rows(tm, D_MODEL), _full((3, D_FF2)), _full((1, D_FF2)), _full((D_FF, D_MODEL), True),
                  _full((1, D_MODEL)), _rows(tm, D_MODEL)],
        out_specs=[_rows(tm, D_FF), _rows(tm, D_MODEL), _full((8, LANES)), _full((1, D_MODEL))],
        out_shape=[jax.ShapeDtypeStruct((T, D_FF), BF16), jax.ShapeDtypeStruct((T, D_MODEL), F32),
                   jax.ShapeDtypeStruct((8, LANES), F32), jax.ShapeDtypeStruct((1, D_MODEL), F32)],
        scratch_shapes=[pltpu.VMEM((8, D_FF2), F32)],
        compiler_params=_cp("arbitrary"),
    )(a, x2, wc, bc, w_down, g3, tgt)


def _bwd_ffn(dx3, a, w_down, wc, bc):
    T = dx3.shape[0]
    tm = min(TM, T)
    nt = T // tm

    def body(dx3_ref, a_ref, halo_ref, wd_ref, wc_ref, bc_ref, da_ref, dwc_ref, dbc_ref, nxt):
        r = pl.program_id(0)

        @pl.when(r == 0)
        def _():
            nxt[...] = jnp.zeros_like(nxt)
            dwc_ref[...] = jnp.zeros_like(dwc_ref)
            dbc_ref[...] = jnp.zeros_like(dbc_ref)

        first = r == nt - 1
        dxb = dx3_ref[...].astype(BF16)
        rowi = lax.broadcasted_iota(jnp.int32, (tm, CW), 0)

        def taps(cs):
            a0 = a_ref[:, cs]
            prev = jnp.where(first, 0.0, halo_ref[:, cs])
            am1, am2 = _conv_taps(a0, prev)
            ac = wc_ref[0:1, cs] * am2 + wc_ref[1:2, cs] * am1 + wc_ref[2:3, cs] * a0 + bc_ref[:, cs]
            return ac, a0, am1, am2

        def back(cs, dac, a0, am1, am2):
            dbc_ref[:, cs] += jnp.sum(dac, axis=0, keepdims=True)
            dwc_ref[0:1, cs] += jnp.sum(dac * am2, axis=0, keepdims=True)
            dwc_ref[1:2, cs] += jnp.sum(dac * am1, axis=0, keepdims=True)
            dwc_ref[2:3, cs] += jnp.sum(dac * a0, axis=0, keepdims=True)
            nx = nxt[:, cs]
            dp1 = jnp.where(rowi == tm - 1, nx[0:1], pltpu.roll(dac, tm - 1, axis=0))
            dp2 = jnp.where(rowi == tm - 2, nx[0:1], jnp.where(rowi == tm - 1, nx[1:2], pltpu.roll(dac, tm - 2, axis=0)))
            da_ref[:, cs] = (wc_ref[2:3, cs] * dac + wc_ref[1:2, cs] * dp1 + wc_ref[0:1, cs] * dp2).astype(BF16)
            nxt[:, cs] = dac[0:8]

        for ci in range(D_FF // CW):
            gs = slice(ci * CW, (ci + 1) * CW)
            vs = slice(D_FF + ci * CW, D_FF + (ci + 1) * CW)
            dy = _dot_nt(dxb, wd_ref[gs, :])
            ag, g0, g1, g2 = taps(gs)
            av, v0, v1, v2 = taps(vs)
            sg = jax.nn.sigmoid(ag)
            back(vs, dy * (ag * sg), v0, v1, v2)
            back(gs, dy * av * (sg * (1.0 + ag * (1.0 - sg))), g0, g1, g2)

    halo = pl.BlockSpec((8, D_FF2), lambda i: (jnp.maximum((nt - 1 - i) * (tm // 8) - 1, 0), 0))
    return pl.pallas_call(
        body, name="bwd_ffn", grid=(nt,),
        in_specs=[_rows(tm, D_MODEL, nt), _rows(tm, D_FF2, nt), halo, _full((D_FF, D_MODEL), True), _full((3, D_FF2)),
                  _full((1, D_FF2))],
        out_specs=[_rows(tm, D_FF2, nt), _full((3, D_FF2)), _full((1, D_FF2))],
        out_shape=[jax.ShapeDtypeStruct((T, D_FF2), BF16), jax.ShapeDtypeStruct((3, D_FF2), F32),
                   jax.ShapeDtypeStruct((1, D_FF2), F32)],
        scratch_shapes=[pltpu.VMEM((8, D_FF2), F32)],
        compiler_params=_cp("arbitrary"),
    )(dx3, a, a, w_down, wc, bc)


def _bwd_mid(da, w_up, x2, g2, dx3, w_out, ob):
    T = x2.shape[0]
    tm = min(TM, T)

    def body(da_ref, wu_ref, x2_ref, g_ref, dx3_ref, wo_ref, ob_ref, dx2_ref, doa_ref, dob_ref, dg2_ref):
        @pl.when(pl.program_id(0) == 0)
        def _():
            dg2_ref[...] = jnp.zeros_like(dg2_ref)

        dh2 = _dot_nt(da_ref[...], wu_ref[...])
        r, n = _rms(x2_ref[...])
        dg2_ref[...] += jnp.sum(dh2 * n, axis=0, keepdims=True)
        dx2 = dx3_ref[...] + _rms_bwd(dh2, n, r, g_ref[...])
        dx2_ref[...] = dx2
        doab = _dot_nt(dx2.astype(BF16), wo_ref[...])
        doa_ref[...] = doab[:, :D_A]
        dob = doab[:, D_A:]
        prod = dob * ob_ref[...]
        for hd in range(N_HEADS):
            sl = _head_sl(hd)
            hi, mid, lo = _split3(-jnp.sum(prod[:, sl], axis=-1, keepdims=True))
            ext = _lanes(tm, HEAD_DIM, {0: hi, 1: mid, 2: lo})
            dob_ref[hd] = jnp.concatenate([dob[:, sl].astype(BF16), ext.astype(BF16)], axis=-1)

    return pl.pallas_call(
        body, name="bwd_mid", grid=(T // tm,),
        in_specs=[_rows(tm, D_FF2), _full((D_MODEL, D_FF2), True), _rows(tm, D_MODEL), _full((1, D_MODEL)), _rows(tm, D_MODEL),
                  _full((D_MODEL, D_MODEL), True), _rows(tm, D_B)],
        out_specs=[_rows(tm, D_MODEL), _rows(tm, D_A), pl.BlockSpec((N_HEADS, tm, LANES), lambda i: (0, i, 0)),
                   _full((1, D_MODEL))],
        out_shape=[jax.ShapeDtypeStruct((T, D_MODEL), F32), jax.ShapeDtypeStruct((T, D_A), F32),
                   jax.ShapeDtypeStruct((N_HEADS, T, LANES), BF16), jax.ShapeDtypeStruct((1, D_MODEL), F32)],
        compiler_params=_cp("arbitrary"),
    )(da, w_up, x2, g2, dx3, w_out, ob)


def _bwd_in(dzuv, dqa, dka, dva, fl, x, dx2, w_in_p, g1):
    T = x.shape[0]
    tm = min(TM, T)
    nt = T // tm

    def body(dzuv_ref, dqa_ref, dka_ref, dva_ref, fl_ref, x_ref, dx2_ref, w_ref, g_ref,
             gx_ref, dz_ref, dg1_ref, dfb_ref, carry):
        @pl.when(pl.program_id(0) == 0)
        def _():
            carry[...] = jnp.zeros_like(carry)
            dg1_ref[...] = jnp.zeros_like(dg1_ref)
            dfb_ref[...] = jnp.zeros_like(dfb_ref)

        dc = _lanes(tm, LANES, {hd: dqa_ref[hd][:, L_ROW:L_ROW + 1] - dka_ref[hd][:, L_COL:L_COL + 1] for hd in range(N_HEADS)})
        later = (lax.broadcasted_iota(jnp.int32, (tm, tm), 1) >= lax.broadcasted_iota(jnp.int32, (tm, tm), 0)).astype(F32)
        dls = _dot_f32(later, dc) + carry[...]
        carry[...] = dls[0:1, :]
        dzf = dls * jax.nn.sigmoid(-fl_ref[...])
        dfb_ref[...] += jnp.sum(dzf, axis=0, keepdims=True)
        dzq = jnp.concatenate([dqa_ref[hd][:, :HEAD_DIM] for hd in range(N_HEADS)], axis=-1) * (HEAD_DIM ** -0.5)
        dzk = jnp.concatenate([dka_ref[hd][:, :HEAD_DIM] for hd in range(N_HEADS)], axis=-1)
        dzv = jnp.concatenate([dva_ref[hd][:, :HEAD_DIM] for hd in range(N_HEADS)], axis=-1)
        dz = jnp.concatenate([dzuv_ref[...], dzq.astype(BF16), dzk.astype(BF16), dzv, dzf.astype(BF16)], axis=-1)
        dz_ref[...] = dz
        dh1 = _dot_nt(dz, w_ref[...])
        r, n = _rms(x_ref[...])
        dg1_ref[...] += jnp.sum(dh1 * n, axis=0, keepdims=True)
        gx_ref[...] = dx2_ref[...] + _rms_bwd(dh1, n, r, g_ref[...])

    rv = functools.partial(_rows, tm, rev_nt=nt)
    heads = pl.BlockSpec((N_HEADS, tm, LANES), lambda i: (0, nt - 1 - i, 0))
    return pl.pallas_call(
        body, name="bwd_in", grid=(nt,),
        in_specs=[rv(2 * D_A), heads, heads, heads, rv(LANES), rv(D_MODEL), rv(D_MODEL),
                  _full((D_MODEL, D_IN_PAD), True), _full((1, D_MODEL))],
        out_specs=[rv(D_MODEL), rv(D_IN_PAD), _full((1, D_MODEL)), _full((1, LANES))],
        out_shape=[jax.ShapeDtypeStruct((T, D_MODEL), F32), jax.ShapeDtypeStruct((T, D_IN_PAD), BF16),
                   jax.ShapeDtypeStruct((1, D_MODEL), F32), jax.ShapeDtypeStruct((1, LANES), F32)],
        scratch_shapes=[pltpu.VMEM((1, LANES), F32)],
        compiler_params=_cp("arbitrary"),
    )(dzuv, dqa, dka, dva, fl, x, dx2, w_in_p, g1)


def _matmul_tn(a, b, tmm, tn, name):
    T, M = a.shape
    N = b.shape[1]
    tk = min(512, T)

    def body(a_ref, b_ref, o_ref):
        @pl.when(pl.program_id(2) == 0)
        def _():
            o_ref[...] = jnp.zeros_like(o_ref)

        o_ref[...] += _dot_tn(a_ref[...].astype(BF16), b_ref[...].astype(BF16))

    return pl.pallas_call(
        body, name=name, grid=(M // tmm, N // tn, T // tk),
        in_specs=[pl.BlockSpec((tk, tmm), lambda i, j, k: (k, i)), pl.BlockSpec((tk, tn), lambda i, j, k: (k, j))],
        out_specs=pl.BlockSpec((tmm, tn), lambda i, j, k: (i, j)),
        out_shape=jax.ShapeDtypeStruct((M, N), F32),
        compiler_params=_cp("arbitrary", "arbitrary", "arbitrary"),
    )(a, b)


class _Exchange:
    def __init__(self, gather, scatter, relay):
        self.n_g, self.n, self.relay = len(gather), len(gather) + len(scatter), relay
        self.operands = [*gather, *scatter]
        self.out_shape = [jax.ShapeDtypeStruct((N_DEV, *g.shape), g.dtype) for g in gather]
        self.out_shape += [jax.ShapeDtypeStruct(s.shape, s.dtype) for s in scatter]
        self.specs = [ANY] * self.n
        n = self.n
        self.scratch = [pltpu.SemaphoreType.DMA((7 * n,)), pltpu.SemaphoreType.DMA((7 * n,)),
                        pltpu.SemaphoreType.DMA((n,))] if n else []

    def _plan(self, ins, outs, sems):
        send_sems, recv_sems, local_sems = sems
        x, y, c = (lax.axis_index(ax) for ax in MESH_AXES)
        me = 4 * x + 2 * y + c
        sibling = (x, y, 1 - c)
        chips = [(1 - x, y), (x, 1 - y), (1 - x, 1 - y)]
        peers = [sibling] + [(*chip, c) for chip in chips] + [(*chip, 1 - c) for chip in chips]

        def index(dev):
            return 4 * dev[0] + 2 * dev[1] + dev[2]

        def remote(k, src, dst, to):
            return pltpu.make_async_remote_copy(src_ref=src, dst_ref=dst, send_sem=send_sems.at[k], recv_sem=recv_sems.at[k],
                                                device_id=to, device_id_type=pl.DeviceIdType.MESH)

        local, sends, relays, recvs = [], [], [], []
        for a in range(self.n):
            src, out, base = ins[a], outs[a], 7 * a
            if a >= self.n_g:
                local.append(pltpu.make_async_copy(src.at[me], out.at[me], local_sems.at[a]))
                sends += [remote(base + k, src.at[index(peer)], out.at[me], peer) for k, peer in enumerate(peers)]
            else:
                local.append(pltpu.make_async_copy(src, out.at[me], local_sems.at[a]))
                sends += [remote(base + k, src, out.at[me], peer) for k, peer in enumerate(peers[:4 if self.relay else 7])]
            for k, peer in enumerate(peers):
                slot = out.at[index(peer)]
                if a < self.n_g and self.relay and k >= 4:
                    continue
                recv = remote(base + k, slot, slot, peer)
                if a < self.n_g and self.relay and k >= 1:
                    relays.append((recv, remote(base + 3 + k, slot, slot, sibling)))
                else:
                    recvs.append(recv)
            if a < self.n_g and self.relay:
                for j, chip in enumerate(chips):
                    slot = out.at[index((*chip, 1 - c))]
                    recvs.append(remote(base + 4 + j, slot, slot, sibling))
        return local, sends, relays, recvs

    def start(self, ins, outs, sems):
        local, sends, _, _ = self._plan(ins, outs, sems)
        for cp in local + sends:
            cp.start()

    def finish(self, ins, outs, sems):
        local, sends, relays, recvs = self._plan(ins, outs, sems)
        for recv, fwd in relays:
            recv.wait_recv()
            fwd.start()
        for recv in recvs:
            recv.wait_recv()
        for cp in sends + [fwd for _, fwd in relays]:
            cp.wait_send()
        for cp in local:
            cp.wait()


def _exchange(gather, scatter, name):
    ex = _Exchange(gather, scatter, relay=True)
    n = ex.n

    def body(*refs):
        ins, outs, sems = refs[:n], refs[n:2 * n], refs[2 * n:]
        ex.start(ins, outs, sems)
        ex.finish(ins, outs, sems)

    return pl.pallas_call(body, name=name, in_specs=ex.specs, out_specs=ex.specs, out_shape=ex.out_shape,
                          scratch_shapes=ex.scratch)(*ex.operands)


def _adamw(w, g, m, v):
    m = ADAM_B1 * m + (1.0 - ADAM_B1) * g
    v = ADAM_B2 * v + (1.0 - ADAM_B2) * jnp.square(g)
    m_hat = m / (1.0 - ADAM_B1 ** ADAM_STEP)
    v_hat = v / (1.0 - ADAM_B2 ** ADAM_STEP)
    delta = -ADAM_LR * (m_hat / (jnp.sqrt(v_hat) + ADAM_EPS) + ADAM_WD * w)
    return delta, m, v


def _adamw_shard(w, m, v, own, recv, tr, name):
    R, C = w.shape

    def body(w_ref, m_ref, v_ref, own_ref, recv_ref, g_ref, d_ref, nm_ref, nv_ref):
        x, y, c = (lax.axis_index(ax) for ax in MESH_AXES)
        me = 4 * x + 2 * y + c
        g = own_ref[...]
        for d in range(N_DEV):
            g = g + jnp.where(d == me, 0.0, recv_ref[d].astype(F32))
        g_ref[...] = g
        d_ref[...], nm_ref[...], nv_ref[...] = _adamw(w_ref[...], g, m_ref[...], v_ref[...])

    blk = pl.BlockSpec((tr, C), lambda i: (i, 0))
    return pl.pallas_call(
        body, name=name, grid=(R // tr,),
        in_specs=[blk, blk, blk, blk, pl.BlockSpec((N_DEV, tr, C), lambda i: (0, i, 0))],
        out_specs=[blk] * 4, out_shape=[jax.ShapeDtypeStruct((R, C), F32)] * 4,
        compiler_params=_cp("arbitrary"),
    )(w, m, v, own, recv)


def _adamw_small(w, m, v, gall):
    R = w.shape[0]
    tr = R // 3

    def body(w_ref, m_ref, v_ref, gall_ref, g_ref, d_ref, nm_ref, nv_ref):
        g = gall_ref[0]
        for d in range(1, N_DEV):
            g = g + gall_ref[d]
        g_ref[...] = g
        d_ref[...], nm_ref[...], nv_ref[...] = _adamw(w_ref[...], g, m_ref[...], v_ref[...])

    blk = pl.BlockSpec((tr, LANES), lambda i: (i, 0))
    return pl.pallas_call(
        body, name="adamw_small", grid=(R // tr,),
        in_specs=[blk, blk, blk, pl.BlockSpec((N_DEV, tr, LANES), lambda i: (0, i, 0))],
        out_specs=[blk] * 4, out_shape=[jax.ShapeDtypeStruct((R, LANES), F32)] * 4,
        compiler_params=_cp("arbitrary"),
    )(w, m, v, gall)


_SMALL_LATE = (("g1", 8), ("fb", 8))
_SMALL_EARLY = (("lg", 8), ("sg_w", 1024), ("sg_b", 8), ("g2", 8), ("bc", 48), ("g3", 8), ("loss", 8))
_SMALL = _SMALL_LATE + _SMALL_EARLY


def _pack_small(parts, layout):
    rows = []
    for name, nrow in layout:
        flat = parts[name].astype(F32).reshape(-1)
        flat = jnp.pad(flat, (0, nrow * LANES - flat.shape[0]))
        rows.append(flat.reshape(nrow, LANES))
    return jnp.concatenate(rows, axis=0)


def _unpack_small(packed, shapes):
    out, r = {}, 0
    for name, nrow in _SMALL:
        size = math.prod(shapes[name])
        out[name] = packed[r:r + nrow].reshape(-1)[:size].reshape(shapes[name])
        r += nrow
    return out


def _col_shards(g):
    return jnp.transpose(g.reshape(g.shape[0], N_DEV, -1), (1, 0, 2))


def _row_shards(g):
    return g.reshape(N_DEV, -1, g.shape[1])


def _cols_whole(g):
    return jnp.transpose(g, (1, 0, 2)).reshape(g.shape[1], -1)


def kernel(x, norm_mix_g, w_in, f_bias, sg_ln_g, sg_w, sg_b, w_out, norm_ffn_g, w_up, w_conv, b_conv, w_down, norm_final_g, loss_target, m_norm_mix_g, m_w_in, m_f_bias, m_sg_ln_g, m_sg_w, m_sg_b, m_w_out, m_norm_ffn_g, m_w_up, m_w_conv, m_b_conv, m_w_down, m_norm_final_g, v_norm_mix_g, v_w_in, v_f_bias, v_sg_ln_g, v_sg_w, v_sg_b, v_w_out, v_norm_ffn_g, v_w_up, v_w_conv, v_b_conv, v_w_down, v_norm_final_g):
    me = 4 * lax.axis_index("x") + 2 * lax.axis_index("y") + lax.axis_index("c")
    xs, tgt = x[0], loss_target[0]
    g1, g2, g3 = norm_mix_g, norm_ffn_g, norm_final_g.reshape(1, D_MODEL)
    lg = sg_ln_g.reshape(1, D_A)
    fb = jnp.pad(f_bias, ((0, 0), (0, LANES - N_HEADS)))
    pos_chunk = jnp.arange(SG_BLOCK) // SG_CHUNK
    maskf = (pos_chunk[:, None] >= pos_chunk[None, :]).astype(F32)
    wm = (sg_w[0] * maskf[None]).astype(BF16)
    wmt = jnp.swapaxes(wm, 1, 2)
    bfull = jnp.repeat(sg_b[0].T, HEAD_DIM, axis=1)
    hsel = jnp.repeat(jnp.eye(N_HEADS, dtype=F32), HEAD_DIM, axis=1)

    (win_g,) = _exchange([w_in[0].astype(BF16)], [], "gather_w_in")
    w_in_p = jnp.pad(_cols_whole(win_g), ((0, 0), (0, D_IN_PAD - D_IN)))
    zuv, qa, ka, va, fl, h1 = _fwd_in(xs, g1, w_in_p, fb)
    oa = _gmlp_fwd(zuv, lg, wm, bfull)
    rest = _Exchange([w_out[0].astype(BF16), w_up[0].astype(BF16), w_down[0].astype(BF16), w_conv[0]], [], relay=False)
    ob, lse, (wout_g, wup_g, wdown_g, wc_g) = _fox_fwd(qa, ka, va, rest)
    w_out_f, w_up_f = wout_g.reshape(D_MODEL, D_MODEL), _cols_whole(wup_g)
    w_down_f, wc_f = wdown_g.reshape(D_FF, D_MODEL), _cols_whole(wc_g)

    x2, h2, a = _fwd_mid(xs, oa, ob, w_out_f, g2, w_up_f)
    yff, dx3, loss, dg3 = _fwd_ffn(a, x2, wc_f, b_conv, w_down_f, g3, tgt)
    da, dwc, dbc = _bwd_ffn(dx3, a, w_down_f, wc_f, b_conv)
    dx2, doa, dob, dg2 = _bwd_mid(da, w_up_f, x2, g2, dx3, w_out_f, ob)
    dzuv, dwm, dsgb, dlg = _gmlp_bwd(zuv, doa, lg, wm, wmt, bfull, maskf, hsel)
    dwout = jnp.concatenate([_matmul_tn(oa, dx2, D_A, D_MODEL, "dw_out_a"), _matmul_tn(ob, dx2, D_B, D_MODEL, "dw_out_b")],
                            axis=0)
    dwup = _matmul_tn(h2, da, D_MODEL, D_FF2 // 4, "dw_up")
    dwdown = _matmul_tn(yff, dx3, D_FF // 2, D_MODEL, "dw_down")

    shards = dict(w_out=_row_shards(dwout), w_up=_col_shards(dwup), wc=_col_shards(dwc), w_down=_row_shards(dwdown))
    early = ("w_out", "w_up", "wc", "w_down")
    small = dict(lg=dlg, sg_w=dwm, sg_b=dsgb, g2=dg2, bc=dbc, g3=dg3, loss=loss[0:1, 0:1])
    grads = _Exchange([_pack_small(small, _SMALL_EARLY)], [shards[n].astype(BF16) for n in early], relay=False)
    dqa, dka, dva, got = _fox_bwd(qa, lse, dob, ka, va, grads)
    small_early, recv = got[0], dict(zip(early, got[1:]))

    gx, dz, dg1, dfb = _bwd_in(dzuv, dqa, dka, dva, fl, xs, dx2, w_in_p, g1)
    shards["w_in"] = _col_shards(_matmul_tn(h1, dz, D_MODEL, D_IN_PAD // 3, "dw_in")[:, :D_IN])
    small_late, recv["w_in"] = _exchange([_pack_small(dict(g1=dg1, fb=dfb), _SMALL_LATE)], [shards["w_in"].astype(BF16)],
                                         "exchange_w_in")

    weights = dict(w_in=(w_in, m_w_in, v_w_in, 256), w_out=(w_out, m_w_out, v_w_out, 128), w_up=(w_up, m_w_up, v_w_up, 256),
                   wc=(w_conv, m_w_conv, v_w_conv, 3), w_down=(w_down, m_w_down, v_w_down, 176))
    res = {}
    for n, (w, m, v, tr) in weights.items():
        own = lax.dynamic_index_in_dim(shards[n], me, axis=0, keepdims=False)
        res[n] = [r[None] for r in _adamw_shard(w[0], m[0], v[0], own, recv[n], tr, "adamw_" + n)]

    zero = jnp.zeros((1, 1), F32)
    reps = dict(g1=(norm_mix_g, m_norm_mix_g, v_norm_mix_g), fb=(f_bias, m_f_bias, v_f_bias), lg=(sg_ln_g, m_sg_ln_g, v_sg_ln_g),
                sg_w=(sg_w, m_sg_w, v_sg_w), sg_b=(sg_b, m_sg_b, v_sg_b), g2=(norm_ffn_g, m_norm_ffn_g, v_norm_ffn_g),
                bc=(b_conv, m_b_conv, v_b_conv), g3=(norm_final_g, m_norm_final_g, v_norm_final_g), loss=(zero, zero, zero))
    shapes = {n: t[0].shape for n, t in reps.items()}
    packed = _adamw_small(*[_pack_small({n: t[i] for n, t in reps.items()}, _SMALL) for i in range(3)],
                          jnp.concatenate([small_late, small_early], axis=1))
    unpacked = [_unpack_small(p, shapes) for p in packed]
    for n in reps:
        res[n] = [u[n] for u in unpacked]

    names = ("g1", "w_in", "fb", "lg", "sg_w", "sg_b", "w_out", "g2", "w_up", "wc", "bc", "w_down", "g3")
    return (res["loss"][0][0, 0], gx[None], *[res[n][0] for n in names], *[res[n][1] for n in names],
            *[res[n][2] for n in names], *[res[n][3] for n in names])
```

```python
import functools
import math

import jax
import jax.numpy as jnp
from jax import lax
from jax.experimental import pallas as pl
from jax.experimental.pallas import tpu as pltpu

F32 = jnp.float32
BF16 = jnp.bfloat16

D_MODEL = 1024
HEAD_DIM = 64
N_HEADS = 8
D_A = 512
D_B = 512
D_IN = 2 * D_A + 3 * D_B + N_HEADS
D_IN_PAD = 2688
D_FF = 2816
D_FF2 = 2 * D_FF
SG_BLOCK = 128
SG_CHUNK = 64
EPS = 1e-6
N_DEV = 8
LANES = 128
NEG = -1e30
VMEM_LIMIT = 56 * 1024 * 1024

ADAM_LR = 0.001
ADAM_B1 = 0.9
ADAM_B2 = 0.999
ADAM_EPS = 1e-08
ADAM_WD = 0.01
ADAM_STEP = 10

TM = 256
TQ = 512
FWD_UNROLL = 4
BWD_UNROLL = 2
CW = 256

MESH_AXES = ("x", "y", "c")
ANY = pl.BlockSpec(memory_space=pl.ANY)


def _cp(*sem):
    return pltpu.CompilerParams(dimension_semantics=sem, vmem_limit_bytes=VMEM_LIMIT)


def _dot(a, b):
    return jnp.dot(a, b, preferred_element_type=F32)


def _dot_nt(a, b):
    return lax.dot_general(a, b, (((1,), (1,)), ((), ())), preferred_element_type=F32)


def _dot_tn(a, b):
    return lax.dot_general(a, b, (((0,), (0,)), ((), ())), preferred_element_type=F32)


def _dot_f32(a, b):
    return jnp.dot(a, b, precision=lax.Precision.HIGHEST, preferred_element_type=F32)


def _gelu(z):
    return 0.5 * z * (1.0 + lax.erf(z * (1.0 / math.sqrt(2.0))))


def _gelu_grad(z):
    return 0.5 * (1.0 + lax.erf(z * (1.0 / math.sqrt(2.0)))) + z * jnp.exp(-0.5 * z * z) * (1.0 / math.sqrt(2.0 * math.pi))


def _log_sigmoid(x):
    return jnp.minimum(x, 0.0) - jnp.log1p(jnp.exp(-jnp.abs(x)))


def _rms(x):
    r = lax.rsqrt(jnp.mean(x * x, axis=-1, keepdims=True) + EPS)
    return r, x * r


def _rms_bwd(dy, n, r, g):
    dn = dy * g
    return r * (dn - n * jnp.mean(dn * n, axis=-1, keepdims=True))


def _full(shape, single=False):
    nd = len(shape)
    if single:
        return pl.BlockSpec(shape, lambda *_: (0,) * nd, pipeline_mode=pl.Buffered(1))
    return pl.BlockSpec(shape, lambda *_: (0,) * nd)


def _rows(tm, cols, rev_nt=None):
    if rev_nt is None:
        return pl.BlockSpec((tm, cols), lambda i: (i, 0))
    return pl.BlockSpec((tm, cols), lambda i: (rev_nt - 1 - i, 0))


def _head_sl(h):
    return slice(HEAD_DIM * h, HEAD_DIM * (h + 1))


L_ROW = HEAD_DIM
L_COL = HEAD_DIM + 3
L_LSE = HEAD_DIM + 6


def _split3(x):
    hi = x.astype(BF16).astype(F32)
    mid = (x - hi).astype(BF16).astype(F32)
    lo = (x - hi - mid).astype(BF16).astype(F32)
    return hi, mid, lo


def _lanes(rows, width, parts):
    lane = lax.broadcasted_iota(jnp.int32, (rows, width), 1)
    out = jnp.zeros((rows, width), F32)
    for at, val in parts.items():
        out = jnp.where(lane == at, val, out)
    return out


def _fwd_in(x, g1, w_in_p, fb):
    T = x.shape[0]
    tm = min(TM, T)

    def body(x_ref, g_ref, w_ref, fb_ref, zuv_ref, qa_ref, ka_ref, va_ref, fl_ref, h1_ref, carry):
        @pl.when(pl.program_id(0) == 0)
        def _():
            carry[...] = jnp.zeros_like(carry)

        r, n = _rms(x_ref[...])
        h = (n * g_ref[...]).astype(BF16)
        h1_ref[...] = h
        z = _dot(h, w_ref[...])
        zuv_ref[...] = z[:, :2 * D_A]
        o = 2 * D_A
        fl = z[:, o + 3 * D_B:] + fb_ref[...]
        fl_ref[...] = fl
        tri = (lax.broadcasted_iota(jnp.int32, (tm, tm), 0) >= lax.broadcasted_iota(jnp.int32, (tm, tm), 1)).astype(F32)
        c = _dot_f32(tri, _log_sigmoid(fl)) + carry[...]
        carry[...] = c[tm - 1:tm, :]
        ext_v = _lanes(tm, HEAD_DIM, {0: 1.0, 1: 1.0, 2: 1.0}).astype(BF16)
        for hd in range(N_HEADS):
            hi, mid, lo = _split3(c[:, hd:hd + 1])
            ext_q = _lanes(tm, HEAD_DIM, {0: hi, 1: mid, 2: lo, 3: 1.0, 4: 1.0, 5: 1.0})
            ext_k = _lanes(tm, HEAD_DIM, {0: 1.0, 1: 1.0, 2: 1.0, 3: -hi, 4: -mid, 5: -lo, 6: 1.0, 7: 1.0, 8: 1.0})
            qh = z[:, o + HEAD_DIM * hd:o + HEAD_DIM * (hd + 1)] * (HEAD_DIM ** -0.5)
            kh = z[:, o + D_B + HEAD_DIM * hd:o + D_B + HEAD_DIM * (hd + 1)]
            vh = z[:, o + 2 * D_B + HEAD_DIM * hd:o + 2 * D_B + HEAD_DIM * (hd + 1)]
            qa_ref[hd] = jnp.concatenate([qh.astype(BF16), ext_q.astype(BF16)], axis=-1)
            ka_ref[hd] = jnp.concatenate([kh.astype(BF16), ext_k.astype(BF16)], axis=-1)
            va_ref[hd] = jnp.concatenate([vh.astype(BF16), ext_v], axis=-1)

    heads = pl.BlockSpec((N_HEADS, tm, LANES), lambda i: (0, i, 0))
    aug = jax.ShapeDtypeStruct((N_HEADS, T, LANES), BF16)
    return pl.pallas_call(
        body, name="fwd_in", grid=(T // tm,),
        in_specs=[_rows(tm, D_MODEL), _full((1, D_MODEL)), _full((D_MODEL, D_IN_PAD), True), _full((1, LANES))],
        out_specs=[_rows(tm, 2 * D_A), heads, heads, heads, _rows(tm, LANES), _rows(tm, D_MODEL)],
        out_shape=[jax.ShapeDtypeStruct((T, 2 * D_A), F32), aug, aug, aug, jax.ShapeDtypeStruct((T, LANES), F32),
                   jax.ShapeDtypeStruct((T, D_MODEL), BF16)],
        scratch_shapes=[pltpu.VMEM((1, LANES), F32)],
        compiler_params=_cp("arbitrary"),
    )(x, g1, w_in_p, fb)


def _ln_head(vh, lgh):
    mu = jnp.mean(vh, axis=-1, keepdims=True)
    d = vh - mu
    rstd = lax.rsqrt(jnp.mean(d * d, axis=-1, keepdims=True) + EPS)
    vhat = d * rstd
    return vhat, rstd, vhat * lgh


def _gmlp_fwd(zuv, lg, wm, bfull):
    T = zuv.shape[0]
    tg = min(TM, T)
    nb = tg // SG_BLOCK

    def body(zuv_ref, lg_ref, wm_ref, b_ref, oa_ref):
        for h in range(N_HEADS):
            sl = _head_sl(h)
            u = _gelu(zuv_ref[:, sl])
            v = _gelu(zuv_ref[:, D_A + HEAD_DIM * h:D_A + HEAD_DIM * (h + 1)])
            _, _, vn = _ln_head(v, lg_ref[:, sl])
            vn = vn.astype(BF16)
            for n in range(nb):
                rs = slice(n * SG_BLOCK, (n + 1) * SG_BLOCK)
                mixed = _dot(wm_ref[h], vn[rs]) + b_ref[:, sl]
                oa_ref[rs, sl] = (u[rs] * mixed).astype(BF16)

    return pl.pallas_call(
        body, name="gmlp_fwd", grid=(T // tg,),
        in_specs=[_rows(tg, 2 * D_A), _full((1, D_A)), _full((N_HEADS, SG_BLOCK, SG_BLOCK)), _full((SG_BLOCK, D_A))],
        out_specs=_rows(tg, D_A),
        out_shape=jax.ShapeDtypeStruct((T, D_A), BF16),
        compiler_params=_cp("arbitrary"),
    )(zuv, lg, wm, bfull)


def _gmlp_bwd(zuv, doa, lg, wm, wmt, bfull, maskf, hsel):
    T = zuv.shape[0]
    tg = min(TM, T)
    nb = tg // SG_BLOCK
    nt = T // tg

    def body(zuv_ref, doa_ref, lg_ref, wm_ref, wmt_ref, b_ref, mask_ref, hsel_ref,
             dzuv_ref, dwm_ref, dsgb_ref, dlg_ref, dbacc):
        i = pl.program_id(0)

        @pl.when(i == 0)
        def _():
            dwm_ref[...] = jnp.zeros_like(dwm_ref)
            dlg_ref[...] = jnp.zeros_like(dlg_ref)
            dbacc[...] = jnp.zeros_like(dbacc)

        for h in range(N_HEADS):
            sl = _head_sl(h)
            zu = zuv_ref[:, sl]
            zv = zuv_ref[:, D_A + HEAD_DIM * h:D_A + HEAD_DIM * (h + 1)]
            u = _gelu(zu)
            lgh = lg_ref[:, sl]
            vhat, rstd, vn = _ln_head(_gelu(zv), lgh)
            vn = vn.astype(BF16)
            doa_h = doa_ref[:, sl]
            du_parts, dvn_parts = [], []
            for n in range(nb):
                rs = slice(n * SG_BLOCK, (n + 1) * SG_BLOCK)
                mixed = _dot(wm_ref[h], vn[rs]) + b_ref[:, sl]
                du_parts.append(doa_h[rs] * mixed)
                dmixed = doa_h[rs] * u[rs]
                dbacc[:, sl] += dmixed
                dmb = dmixed.astype(BF16)
                dwm_ref[h] += _dot_nt(dmb, vn[rs]) * mask_ref[...]
                dvn_parts.append(_dot(wmt_ref[h], dmb))
            du = jnp.concatenate(du_parts, axis=0) if nb > 1 else du_parts[0]
            dvn = jnp.concatenate(dvn_parts, axis=0) if nb > 1 else dvn_parts[0]
            dlg_ref[:, sl] += jnp.sum(dvn * vhat, axis=0, keepdims=True)
            dvhat = dvn * lgh
            dv = rstd * (dvhat - jnp.mean(dvhat, axis=-1, keepdims=True)
                         - vhat * jnp.mean(dvhat * vhat, axis=-1, keepdims=True))
            dzuv_ref[:, sl] = (du * _gelu_grad(zu)).astype(BF16)
            dzuv_ref[:, D_A + HEAD_DIM * h:D_A + HEAD_DIM * (h + 1)] = (dv * _gelu_grad(zv)).astype(BF16)

        @pl.when(i == nt - 1)
        def _():
            dsgb_ref[...] = lax.dot_general(hsel_ref[...], dbacc[...], (((1,), (1,)), ((), ())),
                                            precision=lax.Precision.HIGHEST, preferred_element_type=F32)

    return pl.pallas_call(
        body, name="gmlp_bwd", grid=(nt,),
        in_specs=[_rows(tg, 2 * D_A), _rows(tg, D_A), _full((1, D_A)), _full((N_HEADS, SG_BLOCK, SG_BLOCK)),
                  _full((N_HEADS, SG_BLOCK, SG_BLOCK)), _full((SG_BLOCK, D_A)), _full((SG_BLOCK, SG_BLOCK)),
                  _full((N_HEADS, D_A))],
        out_specs=[_rows(tg, 2 * D_A), _full((N_HEADS, SG_BLOCK, SG_BLOCK)), _full((N_HEADS, SG_BLOCK)), _full((1, D_A))],
        out_shape=[jax.ShapeDtypeStruct((T, 2 * D_A), BF16), jax.ShapeDtypeStruct((N_HEADS, SG_BLOCK, SG_BLOCK), F32),
                   jax.ShapeDtypeStruct((N_HEADS, SG_BLOCK), F32), jax.ShapeDtypeStruct((1, D_A), F32)],
        scratch_shapes=[pltpu.VMEM((SG_BLOCK, D_A), F32)],
        compiler_params=_cp("arbitrary"),
    )(zuv, doa, lg, wm, wmt, bfull, maskf, hsel)


def _fox_fwd(qa, ka, va, side):
    T = qa.shape[1]
    tq = min(TQ, T)
    nq = T // tq
    n = side.n

    def body(qa_ref, ka_ref, va_ref, *rest):
        ins, (o_ref, lse_ref), outs, sems = rest[:n], rest[n:n + 2], rest[n + 2:2 * n + 2], rest[2 * n + 2:]
        i = pl.program_id(1)
        if n:
            @pl.when((pl.program_id(0) == 0) & (i == 0))
            def _():
                side.start(ins, outs, sems)
        row = lax.broadcasted_iota(jnp.int32, (tq, tq), 0)
        col = lax.broadcasted_iota(jnp.int32, (tq, tq), 1)
        qs = [qa_ref[0], qa_ref[1]]

        def tiles(js, carry, diag):
            offs = [pl.multiple_of(j * tq, tq) for j in js]
            logits = [[_dot_nt(qs[hh], ka_ref[hh, pl.ds(off, tq), :]) for hh in range(2)] for off in offs]
            carry = list(carry)
            for off, per_head in zip(offs, logits):
                for hh, s in enumerate(per_head):
                    if diag:
                        s = jnp.where(col <= row, s, NEG)
                    m, acc = carry[hh]
                    m_new = jnp.maximum(m, jnp.max(s, axis=-1, keepdims=True))
                    pr = jnp.exp(s - m_new)
                    acc = jnp.exp(m - m_new) * acc + _dot(pr.astype(BF16), va_ref[hh, pl.ds(off, tq), :])
                    carry[hh] = (m_new, acc)
            return tuple(carry)

        init = ((jnp.full((tq, 1), NEG, F32), jnp.zeros((tq, LANES), F32)),) * 2
        carry = lax.fori_loop(0, i // FWD_UNROLL, lambda t, cr: tiles([FWD_UNROLL * t + u for u in range(FWD_UNROLL)], cr, False), init)
        carry = lax.fori_loop(i - i % FWD_UNROLL, i, lambda j, cr: tiles([j], cr, False), carry)
        carry = tiles([i], carry, True)
        for hh in range(2):
            m, acc = carry[hh]
            l = acc[:, L_ROW:L_ROW + 1]
            o_ref[:, _head_sl(hh)] = acc[:, :HEAD_DIM] / l
            hi, mid, lo = _split3(-(m + jnp.log(l)))
            lse_ref[hh] = _lanes(tq, LANES, {L_LSE: hi, L_LSE + 1: mid, L_LSE + 2: lo}).astype(BF16)
        if n:
            @pl.when((pl.program_id(0) == N_HEADS // 2 - 1) & (i == nq - 1))
            def _():
                side.finish(ins, outs, sems)

    tile = pl.BlockSpec((2, tq, LANES), lambda p, i: (p, i, 0))
    seq = pl.BlockSpec((2, T, LANES), lambda p, i: (p, 0, 0))
    res = pl.pallas_call(
        body, name="fox_fwd", grid=(N_HEADS // 2, nq),
        in_specs=[tile, seq, seq, *side.specs],
        out_specs=[pl.BlockSpec((tq, LANES), lambda p, i: (i, p)), tile, *side.specs],
        out_shape=[jax.ShapeDtypeStruct((T, D_B), F32), jax.ShapeDtypeStruct((N_HEADS, T, LANES), BF16), *side.out_shape],
        scratch_shapes=side.scratch,
        compiler_params=_cp("arbitrary", "arbitrary"),
    )(qa, ka, va, *side.operands)
    return res[0], res[1], res[2:]


def _fox_bwd(qa, lse, doa, ka, va, side):
    T = qa.shape[1]
    tq = min(TQ, T)
    nq = T // tq
    n = side.n

    def body(qa_ref, lse_ref, doa_ref, ka_ref, va_ref, *rest):
        ins, (dqa_ref, dka_ref, dva_ref), outs, sems = rest[:n], rest[n:n + 3], rest[n + 3:2 * n + 3], rest[2 * n + 3:]
        j = pl.program_id(1)
        if n:
            @pl.when((pl.program_id(0) == 0) & (j == 0))
            def _():
                side.start(ins, outs, sems)

        @pl.when(j == 0)
        def _():
            dqa_ref[...] = jnp.zeros_like(dqa_ref)

        row = lax.broadcasted_iota(jnp.int32, (tq, tq), 0)
        col = lax.broadcasted_iota(jnp.int32, (tq, tq), 1)
        ks = [ka_ref[0], ka_ref[1]]
        vs = [va_ref[0], va_ref[1]]

        def tiles(ids, carry, diag):
            work = []
            for i in ids:
                off = pl.multiple_of(i * tq, tq)
                for hh in range(2):
                    qi = qa_ref[hh, pl.ds(off, tq), :] + lse_ref[hh, pl.ds(off, tq), :]
                    doi = doa_ref[hh, pl.ds(off, tq), :]
                    work.append((off, hh, qi, doi, _dot_nt(ks[hh], qi), _dot_nt(vs[hh], doi)))
            carry = list(carry)
            for off, hh, qi, doi, st, dpt in work:
                dk, dv = carry[hh]
                if diag:
                    st = jnp.where(row <= col, st, NEG)
                pt = jnp.exp(st)
                dv = dv + _dot(pt.astype(BF16), doi)
                dsb = (pt * dpt).astype(BF16)
                dk = dk + _dot(dsb, qi)
                dqa_ref[hh, pl.ds(off, tq), :] += _dot_tn(dsb, ks[hh])
                carry[hh] = (dk, dv)
            return tuple(carry)

        init = ((jnp.zeros((tq, LANES), F32), jnp.zeros((tq, LANES), F32)),) * 2
        carry = tiles([j], init, True)
        todo = nq - 1 - j
        carry = lax.fori_loop(0, todo // BWD_UNROLL,
                              lambda t, cr: tiles([j + 1 + BWD_UNROLL * t + u for u in range(BWD_UNROLL)], cr, False), carry)
        carry = lax.fori_loop(nq - todo % BWD_UNROLL, nq, lambda i, cr: tiles([i], cr, False), carry)
        for hh in range(2):
            dka_ref[hh] = carry[hh][0]
            dva_ref[hh] = carry[hh][1].astype(BF16)
        if n:
            @pl.when((pl.program_id(0) == N_HEADS // 2 - 1) & (j == nq - 1))
            def _():
                side.finish(ins, outs, sems)

    tile = pl.BlockSpec((2, tq, LANES), lambda p, j: (p, j, 0))
    seq = pl.BlockSpec((2, T, LANES), lambda p, j: (p, 0, 0))
    res = pl.pallas_call(
        body, name="fox_bwd", grid=(N_HEADS // 2, nq),
        in_specs=[seq, seq, seq, tile, tile, *side.specs],
        out_specs=[seq, tile, tile, *side.specs],
        out_shape=[jax.ShapeDtypeStruct((N_HEADS, T, LANES), F32), jax.ShapeDtypeStruct((N_HEADS, T, LANES), F32),
                   jax.ShapeDtypeStruct((N_HEADS, T, LANES), BF16), *side.out_shape],
        scratch_shapes=side.scratch,
        compiler_params=_cp("arbitrary", "arbitrary"),
    )(qa, lse, doa, ka, va, *side.operands)
    return res[0], res[1], res[2], res[3:]


def _fwd_mid(x, oa, ob, w_out, g2, w_up):
    T = x.shape[0]
    tm = min(TM, T)

    def body(x_ref, oa_ref, ob_ref, wo_ref, g_ref, wu_ref, x2_ref, h2_ref, a_ref):
        oab = jnp.concatenate([oa_ref[...], ob_ref[...].astype(BF16)], axis=-1)
        x2 = x_ref[...] + _dot(oab, wo_ref[...])
        x2_ref[...] = x2
        _, n = _rms(x2)
        h2 = (n * g_ref[...]).astype(BF16)
        h2_ref[...] = h2
        a_ref[...] = _dot(h2, wu_ref[...])

    return pl.pallas_call(
        body, name="fwd_mid", grid=(T // tm,),
        in_specs=[_rows(tm, D_MODEL), _rows(tm, D_A), _rows(tm, D_B), _full((D_MODEL, D_MODEL), True), _full((1, D_MODEL)),
                  _full((D_MODEL, D_FF2), True)],
        out_specs=[_rows(tm, D_MODEL), _rows(tm, D_MODEL), _rows(tm, D_FF2)],
        out_shape=[jax.ShapeDtypeStruct((T, D_MODEL), F32), jax.ShapeDtypeStruct((T, D_MODEL), BF16),
                   jax.ShapeDtypeStruct((T, D_FF2), F32)],
        compiler_params=_cp("arbitrary"),
    )(x, oa, ob, w_out, g2, w_up)


def _conv_taps(a0, prev):
    rowi = lax.broadcasted_iota(jnp.int32, a0.shape, 0)
    am1 = jnp.where(rowi == 0, prev[7:8], pltpu.roll(a0, 1, axis=0))
    am2 = jnp.where(rowi == 0, prev[6:7], jnp.where(rowi == 1, prev[7:8], pltpu.roll(a0, 2, axis=0)))
    return am1, am2


def _fwd_ffn(a, x2, wc, bc, w_down, g3, tgt):
    T = x2.shape[0]
    tm = min(TM, T)

    def body(a_ref, x2_ref, wc_ref, bc_ref, wd_ref, g_ref, tgt_ref, yff_ref, dx3_ref, loss_ref, dg3_ref, carry):
        @pl.when(pl.program_id(0) == 0)
        def _():
            carry[...] = jnp.zeros_like(carry)
            loss_ref[...] = jnp.zeros_like(loss_ref)
            dg3_ref[...] = jnp.zeros_like(dg3_ref)

        def conv(cs):
            a0 = a_ref[:, cs]
            am1, am2 = _conv_taps(a0, carry[:, cs])
            return wc_ref[0:1, cs] * am2 + wc_ref[1:2, cs] * am1 + wc_ref[2:3, cs] * a0 + bc_ref[:, cs]

        x3 = x2_ref[...]
        for ci in range(D_FF // CW):
            gs = slice(ci * CW, (ci + 1) * CW)
            ag = conv(gs)
            av = conv(slice(D_FF + ci * CW, D_FF + (ci + 1) * CW))
            yb = (ag * jax.nn.sigmoid(ag) * av).astype(BF16)
            yff_ref[:, gs] = yb
            x3 = x3 + _dot(yb, wd_ref[gs, :])
        carry[...] = a_ref[tm - 8:tm, :]
        r, n = _rms(x3)
        g = g_ref[...]
        diff = n * g - tgt_ref[...]
        loss_ref[...] += (0.5 / D_MODEL) * jnp.sum(diff * diff)
        dout = diff * (1.0 / D_MODEL)
        dg3_ref[...] += jnp.sum(dout * n, axis=0, keepdims=True)
        dx3_ref[...] = _rms_bwd(dout, n, r, g)

    return pl.pallas_call(
        body, name="fwd_ffn", grid=(T // tm,),
        in_specs=[_rows(tm, D_FF2), _rows(tm, D_MODEL), _full((3, D_FF2)), _full((1, D_FF2)), _full((D_FF, D_MODEL), True),
                  _full((1, D_MODEL)), _rows(tm, D_MODEL)],
        out_specs=[_rows(tm, D_FF), _rows(tm, D_MODEL), _full((8, LANES)), _full((1, D_MODEL))],
        out_shape=[jax.ShapeDtypeStruct((T, D_FF), BF16), jax.ShapeDtypeStruct((T, D_MODEL), F32),
                   jax.ShapeDtypeStruct((8, LANES), F32), jax.ShapeDtypeStruct((1, D_MODEL), F32)],
        scratch_shapes=[pltpu.VMEM((8, D_FF2), F32)],
        compiler_params=_cp("arbitrary"),
    )(a, x2, wc, bc, w_down, g3, tgt)


def _bwd_ffn(dx3, a, w_down, wc, bc):
    T = dx3.shape[0]
    tm = min(TM, T)
    nt = T // tm

    def body(dx3_ref, a_ref, halo_ref, wd_ref, wc_ref, bc_ref, da_ref, dwc_ref, dbc_ref, nxt):
        r = pl.program_id(0)

        @pl.when(r == 0)
        def _():
            nxt[...] = jnp.zeros_like(nxt)
            dwc_ref[...] = jnp.zeros_like(dwc_ref)
            dbc_ref[...] = jnp.zeros_like(dbc_ref)

        first = r == nt - 1
        dxb = dx3_ref[...].astype(BF16)
        rowi = lax.broadcasted_iota(jnp.int32, (tm, CW), 0)

        def taps(cs):
            a0 = a_ref[:, cs]
            prev = jnp.where(first, 0.0, halo_ref[:, cs])
            am1, am2 = _conv_taps(a0, prev)
            ac = wc_ref[0:1, cs] * am2 + wc_ref[1:2, cs] * am1 + wc_ref[2:3, cs] * a0 + bc_ref[:, cs]
            return ac, a0, am1, am2

        def back(cs, dac, a0, am1, am2):
            dbc_ref[:, cs] += jnp.sum(dac, axis=0, keepdims=True)
            dwc_ref[0:1, cs] += jnp.sum(dac * am2, axis=0, keepdims=True)
            dwc_ref[1:2, cs] += jnp.sum(dac * am1, axis=0, keepdims=True)
            dwc_ref[2:3, cs] += jnp.sum(dac * a0, axis=0, keepdims=True)
            nx = nxt[:, cs]
            dp1 = jnp.where(rowi == tm - 1, nx[0:1], pltpu.roll(dac, tm - 1, axis=0))
            dp2 = jnp.where(rowi == tm - 2, nx[0:1], jnp.where(rowi == tm - 1, nx[1:2], pltpu.roll(dac, tm - 2, axis=0)))
            da_ref[:, cs] = (wc_ref[2:3, cs] * dac + wc_ref[1:2, cs] * dp1 + wc_ref[0:1, cs] * dp2).astype(BF16)
            nxt[:, cs] = dac[0:8]

        for ci in range(D_FF // CW):
            gs = slice(ci * CW, (ci + 1) * CW)
            vs = slice(D_FF + ci * CW, D_FF + (ci + 1) * CW)
            dy = _dot_nt(dxb, wd_ref[gs, :])
            ag, g0, g1, g2 = taps(gs)
            av, v0, v1, v2 = taps(vs)
            sg = jax.nn.sigmoid(ag)
            back(vs, dy * (ag * sg), v0, v1, v2)
            back(gs, dy * av * (sg * (1.0 + ag * (1.0 - sg))), g0, g1, g2)

    halo = pl.BlockSpec((8, D_FF2), lambda i: (jnp.maximum((nt - 1 - i) * (tm // 8) - 1, 0), 0))
    return pl.pallas_call(
        body, name="bwd_ffn", grid=(nt,),
        in_specs=[_rows(tm, D_MODEL, nt), _rows(tm, D_FF2, nt), halo, _full((D_FF, D_MODEL), True), _full((3, D_FF2)),
                  _full((1, D_FF2))],
        out_specs=[_rows(tm, D_FF2, nt), _full((3, D_FF2)), _full((1, D_FF2))],
        out_shape=[jax.ShapeDtypeStruct((T, D_FF2), BF16), jax.ShapeDtypeStruct((3, D_FF2), F32),
                   jax.ShapeDtypeStruct((1, D_FF2), F32)],
        scratch_shapes=[pltpu.VMEM((8, D_FF2), F32)],
        compiler_params=_cp("arbitrary"),
    )(dx3, a, a, w_down, wc, bc)


def _bwd_mid(da, w_up, x2, g2, dx3, w_out, ob):
    T = x2.shape[0]
    tm = min(TM, T)

    def body(da_ref, wu_ref, x2_ref, g_ref, dx3_ref, wo_ref, ob_ref, dx2_ref, doa_ref, dob_ref, dg2_ref):
        @pl.when(pl.program_id(0) == 0)
        def _():
            dg2_ref[...] = jnp.zeros_like(dg2_ref)

        dh2 = _dot_nt(da_ref[...], wu_ref[...])
        r, n = _rms(x2_ref[...])
        dg2_ref[...] += jnp.sum(dh2 * n, axis=0, keepdims=True)
        dx2 = dx3_ref[...] + _rms_bwd(dh2, n, r, g_ref[...])
        dx2_ref[...] = dx2
        doab = _dot_nt(dx2.astype(BF16), wo_ref[...])
        doa_ref[...] = doab[:, :D_A]
        dob = doab[:, D_A:]
        prod = dob * ob_ref[...]
        for hd in range(N_HEADS):
            sl = _head_sl(hd)
            hi, mid, lo = _split3(-jnp.sum(prod[:, sl], axis=-1, keepdims=True))
            ext = _lanes(tm, HEAD_DIM, {0: hi, 1: mid, 2: lo})
            dob_ref[hd] = jnp.concatenate([dob[:, sl].astype(BF16), ext.astype(BF16)], axis=-1)

    return pl.pallas_call(
        body, name="bwd_mid", grid=(T // tm,),
        in_specs=[_rows(tm, D_FF2), _full((D_MODEL, D_FF2), True), _rows(tm, D_MODEL), _full((1, D_MODEL)), _rows(tm, D_MODEL),
                  _full((D_MODEL, D_MODEL), True), _rows(tm, D_B)],
        out_specs=[_rows(tm, D_MODEL), _rows(tm, D_A), pl.BlockSpec((N_HEADS, tm, LANES), lambda i: (0, i, 0)),
                   _full((1, D_MODEL))],
        out_shape=[jax.ShapeDtypeStruct((T, D_MODEL), F32), jax.ShapeDtypeStruct((T, D_A), F32),
                   jax.ShapeDtypeStruct((N_HEADS, T, LANES), BF16), jax.ShapeDtypeStruct((1, D_MODEL), F32)],
        compiler_params=_cp("arbitrary"),
    )(da, w_up, x2, g2, dx3, w_out, ob)


def _bwd_in(dzuv, dqa, dka, dva, fl, x, dx2, w_in_p, g1):
    T = x.shape[0]
    tm = min(TM, T)
    nt = T // tm

    def body(dzuv_ref, dqa_ref, dka_ref, dva_ref, fl_ref, x_ref, dx2_ref, w_ref, g_ref,
             gx_ref, dz_ref, dg1_ref, dfb_ref, carry):
        @pl.when(pl.program_id(0) == 0)
        def _():
            carry[...] = jnp.zeros_like(carry)
            dg1_ref[...] = jnp.zeros_like(dg1_ref)
            dfb_ref[...] = jnp.zeros_like(dfb_ref)

        dc = _lanes(tm, LANES, {hd: dqa_ref[hd][:, L_ROW:L_ROW + 1] - dka_ref[hd][:, L_COL:L_COL + 1] for hd in range(N_HEADS)})
        later = (lax.broadcasted_iota(jnp.int32, (tm, tm), 1) >= lax.broadcasted_iota(jnp.int32, (tm, tm), 0)).astype(F32)
        dls = _dot_f32(later, dc) + carry[...]
        carry[...] = dls[0:1, :]
        dzf = dls * jax.nn.sigmoid(-fl_ref[...])
        dfb_ref[...] += jnp.sum(dzf, axis=0, keepdims=True)
        dzq = jnp.concatenate([dqa_ref[hd][:, :HEAD_DIM] for hd in range(N_HEADS)], axis=-1) * (HEAD_DIM ** -0.5)
        dzk = jnp.concatenate([dka_ref[hd][:, :HEAD_DIM] for hd in range(N_HEADS)], axis=-1)
        dzv = jnp.concatenate([dva_ref[hd][:, :HEAD_DIM] for hd in range(N_HEADS)], axis=-1)
        dz = jnp.concatenate([dzuv_ref[...], dzq.astype(BF16), dzk.astype(BF16), dzv, dzf.astype(BF16)], axis=-1)
        dz_ref[...] = dz
        dh1 = _dot_nt(dz, w_ref[...])
        r, n = _rms(x_ref[...])
        dg1_ref[...] += jnp.sum(dh1 * n, axis=0, keepdims=True)
        gx_ref[...] = dx2_ref[...] + _rms_bwd(dh1, n, r, g_ref[...])

    rv = functools.partial(_rows, tm, rev_nt=nt)
    heads = pl.BlockSpec((N_HEADS, tm, LANES), lambda i: (0, nt - 1 - i, 0))
    return pl.pallas_call(
        body, name="bwd_in", grid=(nt,),
        in_specs=[rv(2 * D_A), heads, heads, heads, rv(LANES), rv(D_MODEL), rv(D_MODEL),
                  _full((D_MODEL, D_IN_PAD), True), _full((1, D_MODEL))],
        out_specs=[rv(D_MODEL), rv(D_IN_PAD), _full((1, D_MODEL)), _full((1, LANES))],
        out_shape=[jax.ShapeDtypeStruct((T, D_MODEL), F32), jax.ShapeDtypeStruct((T, D_IN_PAD), BF16),
                   jax.ShapeDtypeStruct((1, D_MODEL), F32), jax.ShapeDtypeStruct((1, LANES), F32)],
        scratch_shapes=[pltpu.VMEM((1, LANES), F32)],
        compiler_params=_cp("arbitrary"),
    )(dzuv, dqa, dka, dva, fl, x, dx2, w_in_p, g1)


def _matmul_tn(a, b, tmm, tn, name):
    T, M = a.shape
    N = b.shape[1]
    tk = min(512, T)

    def body(a_ref, b_ref, o_ref):
        @pl.when(pl.program_id(2) == 0)
        def _():
            o_ref[...] = jnp.zeros_like(o_ref)

        o_ref[...] += _dot_tn(a_ref[...].astype(BF16), b_ref[...].astype(BF16))

    return pl.pallas_call(
        body, name=name, grid=(M // tmm, N // tn, T // tk),
        in_specs=[pl.BlockSpec((tk, tmm), lambda i, j, k: (k, i)), pl.BlockSpec((tk, tn), lambda i, j, k: (k, j))],
        out_specs=pl.BlockSpec((tmm, tn), lambda i, j, k: (i, j)),
        out_shape=jax.ShapeDtypeStruct((M, N), F32),
        compiler_params=_cp("arbitrary", "arbitrary", "arbitrary"),
    )(a, b)


class _Exchange:
    def __init__(self, gather, scatter, relay):
        self.n_g, self.n, self.relay = len(gather), len(gather) + len(scatter), relay
        self.operands = [*gather, *scatter]
        self.out_shape = [jax.ShapeDtypeStruct((N_DEV, *g.shape), g.dtype) for g in gather]
        self.out_shape += [jax.ShapeDtypeStruct(s.shape, s.dtype) for s in scatter]
        self.specs = [ANY] * self.n
        n = self.n
        self.scratch = [pltpu.SemaphoreType.DMA((7 * n,)), pltpu.SemaphoreType.DMA((7 * n,)),
                        pltpu.SemaphoreType.DMA((n,))] if n else []

    def _plan(self, ins, outs, sems):
        send_sems, recv_sems, local_sems = sems
        x, y, c = (lax.axis_index(ax) for ax in MESH_AXES)
        me = 4 * x + 2 * y + c
        sibling = (x, y, 1 - c)
        chips = [(1 - x, y), (x, 1 - y), (1 - x, 1 - y)]
        peers = [sibling] + [(*chip, c) for chip in chips] + [(*chip, 1 - c) for chip in chips]

        def index(dev):
            return 4 * dev[0] + 2 * dev[1] + dev[2]

        def remote(k, src, dst, to):
            return pltpu.make_async_remote_copy(src_ref=src, dst_ref=dst, send_sem=send_sems.at[k], recv_sem=recv_sems.at[k],
                                                device_id=to, device_id_type=pl.DeviceIdType.MESH)

        local, sends, relays, recvs = [], [], [], []
        for a in range(self.n):
            src, out, base = ins[a], outs[a], 7 * a
            if a >= self.n_g:
                local.append(pltpu.make_async_copy(src.at[me], out.at[me], local_sems.at[a]))
                sends += [remote(base + k, src.at[index(peer)], out.at[me], peer) for k, peer in enumerate(peers)]
            else:
                local.append(pltpu.make_async_copy(src, out.at[me], local_sems.at[a]))
                sends += [remote(base + k, src, out.at[me], peer) for k, peer in enumerate(peers[:4 if self.relay else 7])]
            for k, peer in enumerate(peers):
                slot = out.at[index(peer)]
                if a < self.n_g and self.relay and k >= 4:
                    continue
                recv = remote(base + k, slot, slot, peer)
                if a < self.n_g and self.relay and k >= 1:
                    relays.append((recv, remote(base + 3 + k, slot, slot, sibling)))
                else:
                    recvs.append(recv)
            if a < self.n_g and self.relay:
                for j, chip in enumerate(chips):
                    slot = out.at[index((*chip, 1 - c))]
                    recvs.append(remote(base + 4 + j, slot, slot, sibling))
        return local, sends, relays, recvs

    def start(self, ins, outs, sems):
        local, sends, _, _ = self._plan(ins, outs, sems)
        for cp in local + sends:
            cp.start()

    def finish(self, ins, outs, sems):
        local, sends, relays, recvs = self._plan(ins, outs, sems)
        for recv, fwd in relays:
            recv.wait_recv()
            fwd.start()
        for recv in recvs:
            recv.wait_recv()
        for cp in sends + [fwd for _, fwd in relays]:
            cp.wait_send()
        for cp in local:
            cp.wait()


def _exchange(gather, scatter, name):
    ex = _Exchange(gather, scatter, relay=True)
    n = ex.n

    def body(*refs):
        ins, outs, sems = refs[:n], refs[n:2 * n], refs[2 * n:]
        ex.start(ins, outs, sems)
        ex.finish(ins, outs, sems)

    return pl.pallas_call(body, name=name, in_specs=ex.specs, out_specs=ex.specs, out_shape=ex.out_shape,
                          scratch_shapes=ex.scratch)(*ex.operands)


def _adamw(w, g, m, v):
    m = ADAM_B1 * m + (1.0 - ADAM_B1) * g
    v = ADAM_B2 * v + (1.0 - ADAM_B2) * jnp.square(g)
    m_hat = m / (1.0 - ADAM_B1 ** ADAM_STEP)
    v_hat = v / (1.0 - ADAM_B2 ** ADAM_STEP)
    delta = -ADAM_LR * (m_hat / (jnp.sqrt(v_hat) + ADAM_EPS) + ADAM_WD * w)
    return delta, m, v


def _adamw_shard(w, m, v, own, recv, tr, name):
    R, C = w.shape

    def body(w_ref, m_ref, v_ref, own_ref, recv_ref, g_ref, d_ref, nm_ref, nv_ref):
        x, y, c = (lax.axis_index(ax) for ax in MESH_AXES)
        me = 4 * x + 2 * y + c
        g = own_ref[...]
        for d in range(N_DEV):
            g = g + jnp.where(d == me, 0.0, recv_ref[d].astype(F32))
        g_ref[...] = g
        d_ref[...], nm_ref[...], nv_ref[...] = _adamw(w_ref[...], g, m_ref[...], v_ref[...])

    blk = pl.BlockSpec((tr, C), lambda i: (i, 0))
    return pl.pallas_call(
        body, name=name, grid=(R // tr,),
        in_specs=[blk, blk, blk, blk, pl.BlockSpec((N_DEV, tr, C), lambda i: (0, i, 0))],
        out_specs=[blk] * 4, out_shape=[jax.ShapeDtypeStruct((R, C), F32)] * 4,
        compiler_params=_cp("arbitrary"),
    )(w, m, v, own, recv)


def _adamw_small(w, m, v, gall):
    R = w.shape[0]
    tr = R // 3

    def body(w_ref, m_ref, v_ref, gall_ref, g_ref, d_ref, nm_ref, nv_ref):
        g = gall_ref[0]
        for d in range(1, N_DEV):
            g = g + gall_ref[d]
        g_ref[...] = g
        d_ref[...], nm_ref[...], nv_ref[...] = _adamw(w_ref[...], g, m_ref[...], v_ref[...])

    blk = pl.BlockSpec((tr, LANES), lambda i: (i, 0))
    return pl.pallas_call(
        body, name="adamw_small", grid=(R // tr,),
        in_specs=[blk, blk, blk, pl.BlockSpec((N_DEV, tr, LANES), lambda i: (0, i, 0))],
        out_specs=[blk] * 4, out_shape=[jax.ShapeDtypeStruct((R, LANES), F32)] * 4,
        compiler_params=_cp("arbitrary"),
    )(w, m, v, gall)


_SMALL_LATE = (("g1", 8), ("fb", 8))
_SMALL_EARLY = (("lg", 8), ("sg_w", 1024), ("sg_b", 8), ("g2", 8), ("bc", 48), ("g3", 8), ("loss", 8))
_SMALL = _SMALL_LATE + _SMALL_EARLY


def _pack_small(parts, layout):
    rows = []
    for name, nrow in layout:
        flat = parts[name].astype(F32).reshape(-1)
        flat = jnp.pad(flat, (0, nrow * LANES - flat.shape[0]))
        rows.append(flat.reshape(nrow, LANES))
    return jnp.concatenate(rows, axis=0)


def _unpack_small(packed, shapes):
    out, r = {}, 0
    for name, nrow in _SMALL:
        size = math.prod(shapes[name])
        out[name] = packed[r:r + nrow].reshape(-1)[:size].reshape(shapes[name])
        r += nrow
    return out


def _col_shards(g):
    return jnp.transpose(g.reshape(g.shape[0], N_DEV, -1), (1, 0, 2))


def _row_shards(g):
    return g.reshape(N_DEV, -1, g.shape[1])


def _cols_whole(g):
    return jnp.transpose(g, (1, 0, 2)).reshape(g.shape[1], -1)


def kernel(x, norm_mix_g, w_in, f_bias, sg_ln_g, sg_w, sg_b, w_out, norm_ffn_g, w_up, w_conv, b_conv, w_down, norm_final_g, loss_target, m_norm_mix_g, m_w_in, m_f_bias, m_sg_ln_g, m_sg_w, m_sg_b, m_w_out, m_norm_ffn_g, m_w_up, m_w_conv, m_b_conv, m_w_down, m_norm_final_g, v_norm_mix_g, v_w_in, v_f_bias, v_sg_ln_g, v_sg_w, v_sg_b, v_w_out, v_norm_ffn_g, v_w_up, v_w_conv, v_b_conv, v_w_down, v_norm_final_g):
    me = 4 * lax.axis_index("x") + 2 * lax.axis_index("y") + lax.axis_index("c")
    xs, tgt = x[0], loss_target[0]
    g1, g2, g3 = norm_mix_g, norm_ffn_g, norm_final_g.reshape(1, D_MODEL)
    lg = sg_ln_g.reshape(1, D_A)
    fb = jnp.pad(f_bias, ((0, 0), (0, LANES - N_HEADS)))
    pos_chunk = jnp.arange(SG_BLOCK) // SG_CHUNK
    maskf = (pos_chunk[:, None] >= pos_chunk[None, :]).astype(F32)
    wm = (sg_w[0] * maskf[None]).astype(BF16)
    wmt = jnp.swapaxes(wm, 1, 2)
    bfull = jnp.repeat(sg_b[0].T, HEAD_DIM, axis=1)
    hsel = jnp.repeat(jnp.eye(N_HEADS, dtype=F32), HEAD_DIM, axis=1)

    (win_g,) = _exchange([w_in[0].astype(BF16)], [], "gather_w_in")
    w_in_p = jnp.pad(_cols_whole(win_g), ((0, 0), (0, D_IN_PAD - D_IN)))
    zuv, qa, ka, va, fl, h1 = _fwd_in(xs, g1, w_in_p, fb)
    oa = _gmlp_fwd(zuv, lg, wm, bfull)
    rest = _Exchange([w_out[0].astype(BF16), w_up[0].astype(BF16), w_down[0].astype(BF16), w_conv[0]], [], relay=False)
    ob, lse, (wout_g, wup_g, wdown_g, wc_g) = _fox_fwd(qa, ka, va, rest)
    w_out_f, w_up_f = wout_g.reshape(D_MODEL, D_MODEL), _cols_whole(wup_g)
    w_down_f, wc_f = wdown_g.reshape(D_FF, D_MODEL), _cols_whole(wc_g)

    x2, h2, a = _fwd_mid(xs, oa, ob, w_out_f, g2, w_up_f)
    yff, dx3, loss, dg3 = _fwd_ffn(a, x2, wc_f, b_conv, w_down_f, g3, tgt)
    da, dwc, dbc = _bwd_ffn(dx3, a, w_down_f, wc_f, b_conv)
    dx2, doa, dob, dg2 = _bwd_mid(da, w_up_f, x2, g2, dx3, w_out_f, ob)
    dzuv, dwm, dsgb, dlg = _gmlp_bwd(zuv, doa, lg, wm, wmt, bfull, maskf, hsel)
    dwout = jnp.concatenate([_matmul_tn(oa, dx2, D_A, D_MODEL, "dw_out_a"), _matmul_tn(ob, dx2, D_B, D_MODEL, "dw_out_b")],
                            axis=0)
    dwup = _matmul_tn(h2, da, D_MODEL, D_FF2 // 4, "dw_up")
    dwdown = _matmul_tn(yff, dx3, D_FF // 2, D_MODEL, "dw_down")

    shards = dict(w_out=_row_shards(dwout), w_up=_col_shards(dwup), wc=_col_shards(dwc), w_down=_row_shards(dwdown))
    early = ("w_out", "w_up", "wc", "w_down")
    small = dict(lg=dlg, sg_w=dwm, sg_b=dsgb, g2=dg2, bc=dbc, g3=dg3, loss=loss[0:1, 0:1])
    grads = _Exchange([_pack_small(small, _SMALL_EARLY)], [shards[n].astype(BF16) for n in early], relay=False)
    dqa, dka, dva, got = _fox_bwd(qa, lse, dob, ka, va, grads)
    small_early, recv = got[0], dict(zip(early, got[1:]))

    gx, dz, dg1, dfb = _bwd_in(dzuv, dqa, dka, dva, fl, xs, dx2, w_in_p, g1)
    shards["w_in"] = _col_shards(_matmul_tn(h1, dz, D_MODEL, D_IN_PAD // 3, "dw_in")[:, :D_IN])
    small_late, recv["w_in"] = _exchange([_pack_small(dict(g1=dg1, fb=dfb), _SMALL_LATE)], [shards["w_in"].astype(BF16)],
                                         "exchange_w_in")

    weights = dict(w_in=(w_in, m_w_in, v_w_in, 256), w_out=(w_out, m_w_out, v_w_out, 128), w_up=(w_up, m_w_up, v_w_up, 256),
                   wc=(w_conv, m_w_conv, v_w_conv, 3), w_down=(w_down, m_w_down, v_w_down, 176))
    res = {}
    for n, (w, m, v, tr) in weights.items():
        own = lax.dynamic_index_in_dim(shards[n], me, axis=0, keepdims=False)
        res[n] = [r[None] for r in _adamw_shard(w[0], m[0], v[0], own, recv[n], tr, "adamw_" + n)]

    zero = jnp.zeros((1, 1), F32)
    reps = dict(g1=(norm_mix_g, m_norm_mix_g, v_norm_mix_g), fb=(f_bias, m_f_bias, v_f_bias), lg=(sg_ln_g, m_sg_ln_g, v_sg_ln_g),
                sg_w=(sg_w, m_sg_w, v_sg_w), sg_b=(sg_b, m_sg_b, v_sg_b), g2=(norm_ffn_g, m_norm_ffn_g, v_norm_ffn_g),
                bc=(b_conv, m_b_conv, v_b_conv), g3=(norm_final_g, m_norm_final_g, v_norm_final_g), loss=(zero, zero, zero))
    shapes = {n: t[0].shape for n, t in reps.items()}
    packed = _adamw_small(*[_pack_small({n: t[i] for n, t in reps.items()}, _SMALL) for i in range(3)],
                          jnp.concatenate([small_late, small_early], axis=1))
    unpacked = [_unpack_small(p, shapes) for p in packed]
    for n in reps:
        res[n] = [u[n] for u in unpacked]

    names = ("g1", "w_in", "fb", "lg", "sg_w", "sg_b", "w_out", "g2", "w_up", "wc", "bc", "w_down", "g3")
    return (res["loss"][0][0, 0], gx[None], *[res[n][0] for n in names], *[res[n][1] for n in names],
            *[res[n][2] for n in names], *[res[n][3] for n in names])
```

```python
import functools
import math

import jax
import jax.numpy as jnp
from jax import lax
from jax.experimental import pallas as pl
from jax.experimental.pallas import tpu as pltpu

F32 = jnp.float32
BF16 = jnp.bfloat16

D_MODEL = 1024
HEAD_DIM = 64
N_HEADS = 8
D_A = 512
D_B = 512
D_IN = 2 * D_A + 3 * D_B + N_HEADS
D_IN_PAD = 2688
D_FF = 2816
D_FF2 = 2 * D_FF
SG_BLOCK = 128
SG_CHUNK = 64
EPS = 1e-6
N_DEV = 8
LANES = 128
NEG = -1e30
VMEM_LIMIT = 56 * 1024 * 1024

ADAM_LR = 0.001
ADAM_B1 = 0.9
ADAM_B2 = 0.999
ADAM_EPS = 1e-08
ADAM_WD = 0.01
ADAM_STEP = 10

TM = 256
TQ = 512
FWD_UNROLL = 4
BWD_UNROLL = 2
CW = 256

MESH_AXES = ("x", "y", "c")
ANY = pl.BlockSpec(memory_space=pl.ANY)


def _cp(*sem):
    return pltpu.CompilerParams(dimension_semantics=sem, vmem_limit_bytes=VMEM_LIMIT)


def _dot(a, b):
    return jnp.dot(a, b, preferred_element_type=F32)


def _dot_nt(a, b):
    return lax.dot_general(a, b, (((1,), (1,)), ((), ())), preferred_element_type=F32)


def _dot_tn(a, b):
    return lax.dot_general(a, b, (((0,), (0,)), ((), ())), preferred_element_type=F32)


def _dot_f32(a, b):
    return jnp.dot(a, b, precision=lax.Precision.HIGHEST, preferred_element_type=F32)


def _gelu(z):
    return 0.5 * z * (1.0 + lax.erf(z * (1.0 / math.sqrt(2.0))))


def _gelu_grad(z):
    return 0.5 * (1.0 + lax.erf(z * (1.0 / math.sqrt(2.0)))) + z * jnp.exp(-0.5 * z * z) * (1.0 / math.sqrt(2.0 * math.pi))


def _log_sigmoid(x):
    return jnp.minimum(x, 0.0) - jnp.log1p(jnp.exp(-jnp.abs(x)))


def _rms(x):
    r = lax.rsqrt(jnp.mean(x * x, axis=-1, keepdims=True) + EPS)
    return r, x * r


def _rms_bwd(dy, n, r, g):
    dn = dy * g
    return r * (dn - n * jnp.mean(dn * n, axis=-1, keepdims=True))


def _full(shape, single=False):
    nd = len(shape)
    if single:
        return pl.BlockSpec(shape, lambda *_: (0,) * nd, pipeline_mode=pl.Buffered(1))
    return pl.BlockSpec(shape, lambda *_: (0,) * nd)


def _rows(tm, cols, rev_nt=None):
    if rev_nt is None:
        return pl.BlockSpec((tm, cols), lambda i: (i, 0))
    return pl.BlockSpec((tm, cols), lambda i: (rev_nt - 1 - i, 0))


def _head_sl(h):
    return slice(HEAD_DIM * h, HEAD_DIM * (h + 1))


L_ROW = HEAD_DIM
L_COL = HEAD_DIM + 3
L_LSE = HEAD_DIM + 6


def _split3(x):
    hi = x.astype(BF16).astype(F32)
    mid = (x - hi).astype(BF16).astype(F32)
    lo = (x - hi - mid).astype(BF16).astype(F32)
    return hi, mid, lo


def _lanes(rows, width, parts):
    lane = lax.broadcasted_iota(jnp.int32, (rows, width), 1)
    out = jnp.zeros((rows, width), F32)
    for at, val in parts.items():
        out = jnp.where(lane == at, val, out)
    return out


def _fwd_in(x, g1, w_in_p, fb):
    T = x.shape[0]
    tm = min(TM, T)

    def body(x_ref, g_ref, w_ref, fb_ref, place_ref, zuv_ref, qa_ref, ka_ref, va_ref, fl_ref, h1_ref, carry):
        @pl.when(pl.program_id(0) == 0)
        def _():
            carry[...] = jnp.zeros_like(carry)

        r, n = _rms(x_ref[...])
        h = (n * g_ref[...]).astype(BF16)
        h1_ref[...] = h
        z = _dot(h, w_ref[...])
        zuv_ref[...] = z[:, :2 * D_A]
        o = 2 * D_A
        fl = z[:, o + 3 * D_B:] + fb_ref[...]
        fl_ref[...] = fl
        tri = (lax.broadcasted_iota(jnp.int32, (tm, tm), 0) >= lax.broadcasted_iota(jnp.int32, (tm, tm), 1)).astype(F32)
        c = _dot_f32(tri, _log_sigmoid(fl)) + carry[...]
        carry[...] = c[tm - 1:tm, :]
        hi, mid, lo = _split3(c)
        parts = jnp.concatenate([hi.astype(BF16), mid.astype(BF16), lo.astype(BF16)], axis=-1)
        placed = _dot(parts, place_ref[...])
        lane = lax.broadcasted_iota(jnp.int32, (tm, LANES), 1)
        data = lane < HEAD_DIM
        ones_q = ((lane >= L_COL) & (lane < L_COL + 3)).astype(F32)
        ones_k = (((lane >= L_ROW) & (lane < L_ROW + 3)) | ((lane >= L_LSE) & (lane < L_LSE + 3))).astype(F32)
        ones_v = ((lane >= L_ROW) & (lane < L_ROW + 3)).astype(F32)
        for hd in range(N_HEADS):
            def rows_of(first_col):
                tile = z[:, first_col + LANES * (hd // 2):first_col + LANES * (hd // 2 + 1)]
                return tile if hd % 2 == 0 else pltpu.roll(tile, HEAD_DIM, axis=1)

            hs = slice(LANES * hd, LANES * (hd + 1))
            qa_ref[hd] = jnp.where(data, rows_of(o) * (HEAD_DIM ** -0.5), placed[:, hs] + ones_q).astype(BF16)
            ka_ref[hd] = jnp.where(data, rows_of(o + D_B), ones_k - placed[:, N_HEADS * LANES:][:, hs]).astype(BF16)
            va_ref[hd] = jnp.where(data, rows_of(o + 2 * D_B), ones_v).astype(BF16)

    heads = pl.BlockSpec((N_HEADS, tm, LANES), lambda i: (0, i, 0))
    aug = jax.ShapeDtypeStruct((N_HEADS, T, LANES), BF16)
    j, h = jnp.arange(3 * LANES) // LANES, jnp.arange(3 * LANES) % LANES
    cols = jnp.arange(2 * N_HEADS * LANES)
    place = ((h[:, None] < N_HEADS) & ((cols[None, :] == LANES * h[:, None] + L_ROW + j[:, None])
                                       | (cols[None, :] == N_HEADS * LANES + LANES * h[:, None] + L_COL + j[:, None]))).astype(BF16)
    return pl.pallas_call(
        body, name="fwd_in", grid=(T // tm,),
        in_specs=[_rows(tm, D_MODEL), _full((1, D_MODEL)), _full((D_MODEL, D_IN_PAD), True), _full((1, LANES)),
                  _full((3 * LANES, 2 * N_HEADS * LANES))],
        out_specs=[_rows(tm, 2 * D_A), heads, heads, heads, _rows(tm, LANES), _rows(tm, D_MODEL)],
        out_shape=[jax.ShapeDtypeStruct((T, 2 * D_A), F32), aug, aug, aug, jax.ShapeDtypeStruct((T, LANES), F32),
                   jax.ShapeDtypeStruct((T, D_MODEL), BF16)],
        scratch_shapes=[pltpu.VMEM((1, LANES), F32)],
        compiler_params=_cp("arbitrary"),
    )(x, g1, w_in_p, fb, place)


def _segmean(x, avg_ref, parts):
    out, rest = None, x
    for _ in range(parts):
        piece = rest.astype(BF16)
        term = _dot(piece, avg_ref[...])
        out = term if out is None else out + term
        rest = rest - piece.astype(F32)
    return out


def _layer_norm(v, lg, avg_ref):
    d = v - _segmean(v, avg_ref, 3)
    rstd = lax.rsqrt(_segmean(d * d, avg_ref, 2) + EPS)
    vhat = d * rstd
    return vhat, rstd, (vhat * lg).astype(BF16)


def _mix_heads(w_ref, x, row_slice):
    low = lax.broadcasted_iota(jnp.int32, (SG_BLOCK, LANES), 1) < HEAD_DIM
    tiles = []
    for p in range(N_HEADS // 2):
        xt = x[row_slice, LANES * p:LANES * (p + 1)]
        zero = jnp.zeros_like(xt)
        tiles.append(_dot(w_ref[2 * p], jnp.where(low, xt, zero)) + _dot(w_ref[2 * p + 1], jnp.where(low, zero, xt)))
    return jnp.concatenate(tiles, axis=-1)


def _gmlp_fwd(zuv, lg, wm, bfull, avg):
    T = zuv.shape[0]
    tg = min(TM, T)
    nb = tg // SG_BLOCK

    def body(zuv_ref, lg_ref, wm_ref, b_ref, avg_ref, oa_ref):
        u = _gelu(zuv_ref[:, :D_A])
        _, _, vn = _layer_norm(_gelu(zuv_ref[:, D_A:]), lg_ref[...], avg_ref)
        for n in range(nb):
            rs = slice(n * SG_BLOCK, (n + 1) * SG_BLOCK)
            oa_ref[rs, :] = (u[rs] * (_mix_heads(wm_ref, vn, rs) + b_ref[...])).astype(BF16)

    return pl.pallas_call(
        body, name="gmlp_fwd", grid=(T // tg,),
        in_specs=[_rows(tg, 2 * D_A), _full((1, D_A)), _full((N_HEADS, SG_BLOCK, SG_BLOCK)), _full((SG_BLOCK, D_A)),
                  _full((D_A, D_A))],
        out_specs=_rows(tg, D_A),
        out_shape=jax.ShapeDtypeStruct((T, D_A), BF16),
        compiler_params=_cp("arbitrary"),
    )(zuv, lg, wm, bfull, avg)


def _gmlp_bwd(zuv, doa, lg, wm, wmt, bfull, maskf, hsel, avg):
    T = zuv.shape[0]
    tg = min(TM, T)
    nb = tg // SG_BLOCK
    nt = T // tg

    def body(zuv_ref, doa_ref, lg_ref, wm_ref, wmt_ref, b_ref, mask_ref, hsel_ref, avg_ref,
             dzuv_ref, dwm_ref, dsgb_ref, dlg_ref, dbacc):
        i = pl.program_id(0)

        @pl.when(i == 0)
        def _():
            dwm_ref[...] = jnp.zeros_like(dwm_ref)
            dlg_ref[...] = jnp.zeros_like(dlg_ref)
            dbacc[...] = jnp.zeros_like(dbacc)

        zu, zv = zuv_ref[:, :D_A], zuv_ref[:, D_A:]
        lgv = lg_ref[...]
        vhat, rstd, vn = _layer_norm(_gelu(zv), lgv, avg_ref)
        dmixed = doa_ref[...] * _gelu(zu)
        dmb = dmixed.astype(BF16)
        low = lax.broadcasted_iota(jnp.int32, (SG_BLOCK, LANES), 1) < HEAD_DIM
        mixed, dvn = [], []
        for n in range(nb):
            rs = slice(n * SG_BLOCK, (n + 1) * SG_BLOCK)
            mixed.append(_mix_heads(wm_ref, vn, rs) + b_ref[...])
            dvn.append(_mix_heads(wmt_ref, dmb, rs))
            dbacc[...] += dmixed[rs]
            for p in range(N_HEADS // 2):
                ls = slice(LANES * p, LANES * (p + 1))
                dmt, vnt = dmb[rs, ls], vn[rs, ls]
                zero = jnp.zeros_like(dmt)
                dwm_ref[2 * p] += _dot_nt(jnp.where(low, dmt, zero), vnt) * mask_ref[...]
                dwm_ref[2 * p + 1] += _dot_nt(jnp.where(low, zero, dmt), vnt) * mask_ref[...]
        mixed = jnp.concatenate(mixed, axis=0) if nb > 1 else mixed[0]
        dvn = jnp.concatenate(dvn, axis=0) if nb > 1 else dvn[0]
        dlg_ref[...] += jnp.sum(dvn * vhat, axis=0, keepdims=True)
        dvhat = dvn * lgv
        dv = rstd * (dvhat - _segmean(dvhat, avg_ref, 3) - vhat * _segmean(dvhat * vhat, avg_ref, 3))
        dzuv_ref[:, :D_A] = (doa_ref[...] * mixed * _gelu_grad(zu)).astype(BF16)
        dzuv_ref[:, D_A:] = (dv * _gelu_grad(zv)).astype(BF16)

        @pl.when(i == nt - 1)
        def _():
            dsgb_ref[...] = lax.dot_general(hsel_ref[...], dbacc[...], (((1,), (1,)), ((), ())),
                                            precision=lax.Precision.HIGHEST, preferred_element_type=F32)

    return pl.pallas_call(
        body, name="gmlp_bwd", grid=(nt,),
        in_specs=[_rows(tg, 2 * D_A), _rows(tg, D_A), _full((1, D_A)), _full((N_HEADS, SG_BLOCK, SG_BLOCK)),
                  _full((N_HEADS, SG_BLOCK, SG_BLOCK)), _full((SG_BLOCK, D_A)), _full((SG_BLOCK, SG_BLOCK)),
                  _full((N_HEADS, D_A)), _full((D_A, D_A))],
        out_specs=[_rows(tg, 2 * D_A), _full((N_HEADS, SG_BLOCK, SG_BLOCK)), _full((N_HEADS, SG_BLOCK)), _full((1, D_A))],
        out_shape=[jax.ShapeDtypeStruct((T, 2 * D_A), BF16), jax.ShapeDtypeStruct((N_HEADS, SG_BLOCK, SG_BLOCK), F32),
                   jax.ShapeDtypeStruct((N_HEADS, SG_BLOCK), F32), jax.ShapeDtypeStruct((1, D_A), F32)],
        scratch_shapes=[pltpu.VMEM((SG_BLOCK, D_A), F32)],
        compiler_params=_cp("arbitrary"),
    )(zuv, doa, lg, wm, wmt, bfull, maskf, hsel, avg)


def _fox_fwd(qa, ka, va, side):
    T = qa.shape[1]
    tq = min(TQ, T)
    nq = T // tq
    n = side.n

    def body(qa_ref, ka_ref, va_ref, *rest):
        ins, (o_ref, lse_ref), outs, sems = rest[:n], rest[n:n + 2], rest[n + 2:2 * n + 2], rest[2 * n + 2:]
        i = pl.program_id(1)
        if n:
            @pl.when((pl.program_id(0) == 0) & (i == 0))
            def _():
                side.start(ins, outs, sems)
        row = lax.broadcasted_iota(jnp.int32, (tq, tq), 0)
        col = lax.broadcasted_iota(jnp.int32, (tq, tq), 1)
        qs = [qa_ref[0], qa_ref[1]]

        def tiles(js, carry, diag):
            offs = [pl.multiple_of(j * tq, tq) for j in js]
            logits = [[_dot_nt(qs[hh], ka_ref[hh, pl.ds(off, tq), :]) for hh in range(2)] for off in offs]
            carry = list(carry)
            for off, per_head in zip(offs, logits):
                for hh, s in enumerate(per_head):
                    if diag:
                        s = jnp.where(col <= row, s, NEG)
                    m, acc = carry[hh]
                    m_new = jnp.maximum(m, jnp.max(s, axis=-1, keepdims=True))
                    pr = jnp.exp(s - m_new)
                    acc = jnp.exp(m - m_new) * acc + _dot(pr.astype(BF16), va_ref[hh, pl.ds(off, tq), :])
                    carry[hh] = (m_new, acc)
            return tuple(carry)

        init = ((jnp.full((tq, 1), NEG, F32), jnp.zeros((tq, LANES), F32)),) * 2
        carry = lax.fori_loop(0, i // FWD_UNROLL, lambda t, cr: tiles([FWD_UNROLL * t + u for u in range(FWD_UNROLL)], cr, False), init)
        carry = lax.fori_loop(i - i % FWD_UNROLL, i, lambda j, cr: tiles([j], cr, False), carry)
        carry = tiles([i], carry, True)
        for hh in range(2):
            m, acc = carry[hh]
            l = acc[:, L_ROW:L_ROW + 1]
            o_ref[:, _head_sl(hh)] = acc[:, :HEAD_DIM] / l
            hi, mid, lo = _split3(-(m + jnp.log(l)))
            lse_ref[hh] = _lanes(tq, LANES, {L_LSE: hi, L_LSE + 1: mid, L_LSE + 2: lo}).astype(BF16)
        if n:
            @pl.when((pl.program_id(0) == N_HEADS // 2 - 1) & (i == nq - 1))
            def _():
                side.finish(ins, outs, sems)

    tile = pl.BlockSpec((2, tq, LANES), lambda p, i: (p, i, 0))
    seq = pl.BlockSpec((2, T, LANES), lambda p, i: (p, 0, 0))
    res = pl.pallas_call(
        body, name="fox_fwd", grid=(N_HEADS // 2, nq),
        in_specs=[tile, seq, seq, *side.specs],
        out_specs=[pl.BlockSpec((tq, LANES), lambda p, i: (i, p)), tile, *side.specs],
        out_shape=[jax.ShapeDtypeStruct((T, D_B), F32), jax.ShapeDtypeStruct((N_HEADS, T, LANES), BF16), *side.out_shape],
        scratch_shapes=side.scratch,
        compiler_params=_cp("arbitrary", "arbitrary"),
    )(qa, ka, va, *side.operands)
    return res[0], res[1], res[2:]


def _fox_bwd(qa, lse, doa, ka, va, side):
    T = qa.shape[1]
    tq = min(TQ, T)
    nq = T // tq
    n = side.n

    def body(qa_ref, lse_ref, doa_ref, ka_ref, va_ref, *rest):
        ins, (dqa_ref, dka_ref, dva_ref), outs, sems = rest[:n], rest[n:n + 3], rest[n + 3:2 * n + 3], rest[2 * n + 3:]
        j = pl.program_id(1)
        if n:
            @pl.when((pl.program_id(0) == 0) & (j == 0))
            def _():
                side.start(ins, outs, sems)

        @pl.when(j == 0)
        def _():
            dqa_ref[...] = jnp.zeros_like(dqa_ref)

        row = lax.broadcasted_iota(jnp.int32, (tq, tq), 0)
        col = lax.broadcasted_iota(jnp.int32, (tq, tq), 1)
        ks = [ka_ref[0], ka_ref[1]]
        vs = [va_ref[0], va_ref[1]]

        def tiles(ids, carry, diag):
            work = []
            for i in ids:
                off = pl.multiple_of(i * tq, tq)
                for hh in range(2):
                    qi = qa_ref[hh, pl.ds(off, tq), :] + lse_ref[hh, pl.ds(off, tq), :]
                    doi = doa_ref[hh, pl.ds(off, tq), :]
                    work.append((off, hh, qi, doi, _dot_nt(ks[hh], qi), _dot_nt(vs[hh], doi)))
            carry = list(carry)
            for off, hh, qi, doi, st, dpt in work:
                dk, dv = carry[hh]
                if diag:
                    st = jnp.where(row <= col, st, NEG)
                pt = jnp.exp(st)
                dv = dv + _dot(pt.astype(BF16), doi)
                dsb = (pt * dpt).astype(BF16)
                dk = dk + _dot(dsb, qi)
                dqa_ref[hh, pl.ds(off, tq), :] += _dot_tn(dsb, ks[hh])
                carry[hh] = (dk, dv)
            return tuple(carry)

        init = ((jnp.zeros((tq, LANES), F32), jnp.zeros((tq, LANES), F32)),) * 2
        carry = tiles([j], init, True)
        todo = nq - 1 - j
        carry = lax.fori_loop(0, todo // BWD_UNROLL,
                              lambda t, cr: tiles([j + 1 + BWD_UNROLL * t + u for u in range(BWD_UNROLL)], cr, False), carry)
        carry = lax.fori_loop(nq - todo % BWD_UNROLL, nq, lambda i, cr: tiles([i], cr, False), carry)
        for hh in range(2):
            dka_ref[hh] = carry[hh][0]
            dva_ref[hh] = carry[hh][1].astype(BF16)
        if n:
            @pl.when((pl.program_id(0) == N_HEADS // 2 - 1) & (j == nq - 1))
            def _():
                side.finish(ins, outs, sems)

    tile = pl.BlockSpec((2, tq, LANES), lambda p, j: (p, j, 0))
    seq = pl.BlockSpec((2, T, LANES), lambda p, j: (p, 0, 0))
    res = pl.pallas_call(
        body, name="fox_bwd", grid=(N_HEADS // 2, nq),
        in_specs=[seq, seq, seq, tile, tile, *side.specs],
        out_specs=[seq, tile, tile, *side.specs],
        out_shape=[jax.ShapeDtypeStruct((N_HEADS, T, LANES), F32), jax.ShapeDtypeStruct((N_HEADS, T, LANES), F32),
                   jax.ShapeDtypeStruct((N_HEADS, T, LANES), BF16), *side.out_shape],
        scratch_shapes=side.scratch,
        compiler_params=_cp("arbitrary", "arbitrary"),
    )(qa, lse, doa, ka, va, *side.operands)
    return res[0], res[1], res[2], res[3:]


def _fwd_mid(x, oa, ob, w_out, g2, w_up):
    T = x.shape[0]
    tm = min(TM, T)

    def body(x_ref, oa_ref, ob_ref, wo_ref, g_ref, wu_ref, x2_ref, h2_ref, a_ref):
        oab = jnp.concatenate([oa_ref[...], ob_ref[...].astype(BF16)], axis=-1)
        x2 = x_ref[...] + _dot(oab, wo_ref[...])
        x2_ref[...] = x2
        _, n = _rms(x2)
        h2 = (n * g_ref[...]).astype(BF16)
        h2_ref[...] = h2
        a_ref[...] = _dot(h2, wu_ref[...])

    return pl.pallas_call(
        body, name="fwd_mid", grid=(T // tm,),
        in_specs=[_rows(tm, D_MODEL), _rows(tm, D_A), _rows(tm, D_B), _full((D_MODEL, D_MODEL), True), _full((1, D_MODEL)),
                  _full((D_MODEL, D_FF2), True)],
        out_specs=[_rows(tm, D_MODEL), _rows(tm, D_MODEL), _rows(tm, D_FF2)],
        out_shape=[jax.ShapeDtypeStruct((T, D_MODEL), F32), jax.ShapeDtypeStruct((T, D_MODEL), BF16),
                   jax.ShapeDtypeStruct((T, D_FF2), F32)],
        compiler_params=_cp("arbitrary"),
    )(x, oa, ob, w_out, g2, w_up)


def _row_before(x, prev, k):
    rolled = pltpu.roll(x, k, axis=0)
    row = lax.broadcasted_iota(jnp.int32, (8, x.shape[1]), 0)
    head = rolled[0:8]
    for r in range(k):
        head = jnp.where(row == r, prev[8 - k + r:9 - k + r], head)
    return jnp.concatenate([head, rolled[8:]], axis=0)


def _row_after(x, nxt, k):
    tm = x.shape[0]
    rolled = pltpu.roll(x, tm - k, axis=0)
    row = lax.broadcasted_iota(jnp.int32, (8, x.shape[1]), 0)
    tail = rolled[tm - 8:tm]
    for r in range(k):
        tail = jnp.where(row == 8 - k + r, nxt[r:r + 1], tail)
    return jnp.concatenate([rolled[:tm - 8], tail], axis=0)


def _fwd_ffn(a, x2, wc, bc, w_down, g3, tgt):
    T = x2.shape[0]
    tm = min(TM, T)

    def body(a_ref, x2_ref, wc_ref, bc_ref, wd_ref, g_ref, tgt_ref, ac_ref, yff_ref, dx3_ref, loss_ref, dg3_ref, carry):
        @pl.when(pl.program_id(0) == 0)
        def _():
            carry[...] = jnp.zeros_like(carry)
            loss_ref[...] = jnp.zeros_like(loss_ref)
            dg3_ref[...] = jnp.zeros_like(dg3_ref)

        def conv(cs):
            a0 = a_ref[:, cs]
            prev = carry[:, cs]
            ac = (wc_ref[0:1, cs] * _row_before(a0, prev, 2) + wc_ref[1:2, cs] * _row_before(a0, prev, 1)
                  + wc_ref[2:3, cs] * a0 + bc_ref[:, cs])
            ac_ref[:, cs] = ac
            return ac

        x3 = x2_ref[...]
        for ci in range(D_FF // CW):
            gs = slice(ci * CW, (ci + 1) * CW)
            ag = conv(gs)
            av = conv(slice(D_FF + ci * CW, D_FF + (ci + 1) * CW))
            yb = (ag * jax.nn.sigmoid(ag) * av).astype(BF16)
            yff_ref[:, gs] = yb
            x3 = x3 + _dot(yb, wd_ref[gs, :])
        carry[...] = a_ref[tm - 8:tm, :]
        r, n = _rms(x3)
        g = g_ref[...]
        diff = n * g - tgt_ref[...]
        loss_ref[...] += (0.5 / D_MODEL) * jnp.sum(diff * diff)
        dout = diff * (1.0 / D_MODEL)
        dg3_ref[...] += jnp.sum(dout * n, axis=0, keepdims=True)
        dx3_ref[...] = _rms_bwd(dout, n, r, g)

    return pl.pallas_call(
        body, name="fwd_ffn", grid=(T // tm,),
        in_specs=[_rows(tm, D_FF2), _rows(tm, D_MODEL), _full((3, D_FF2)), _full((1, D_FF2)), _full((D_FF, D_MODEL), True),
                  _full((1, D_MODEL)), _rows(tm, D_MODEL)],
        out_specs=[_rows(tm, D_FF2), _rows(tm, D_FF), _rows(tm, D_MODEL), _full((8, LANES)), _full((1, D_MODEL))],
        out_shape=[jax.ShapeDtypeStruct((T, D_FF2), F32), jax.ShapeDtypeStruct((T, D_FF), BF16),
                   jax.ShapeDtypeStruct((T, D_MODEL), F32), jax.ShapeDtypeStruct((8, LANES), F32),
                   jax.ShapeDtypeStruct((1, D_MODEL), F32)],
        scratch_shapes=[pltpu.VMEM((8, D_FF2), F32)],
        compiler_params=_cp("arbitrary"),
    )(a, x2, wc, bc, w_down, g3, tgt)


def _bwd_ffn(dx3, a, ac, w_down, wc):
    T = dx3.shape[0]
    tm = min(TM, T)
    nt = T // tm

    def body(dx3_ref, a_ref, ac_ref, wd_ref, wc_ref, da_ref, dwc_ref, dbc_ref, nxt, shifted):
        @pl.when(pl.program_id(0) == 0)
        def _():
            nxt[...] = jnp.zeros_like(nxt)
            dwc_ref[...] = jnp.zeros_like(dwc_ref)
            dbc_ref[...] = jnp.zeros_like(dbc_ref)

        dxb = dx3_ref[...].astype(BF16)

        def back(cs, dac):
            a0 = a_ref[:, cs]
            shifted[0] = _row_after(dac, nxt[:, cs], 1)
            shifted[1] = _row_after(dac, nxt[:, cs], 2)
            dp1, dp2 = shifted[0], shifted[1]
            dbc_ref[:, cs] += jnp.sum(dac, axis=0, keepdims=True)
            dwc_ref[0:1, cs] += jnp.sum(dp2 * a0, axis=0, keepdims=True)
            dwc_ref[1:2, cs] += jnp.sum(dp1 * a0, axis=0, keepdims=True)
            dwc_ref[2:3, cs] += jnp.sum(dac * a0, axis=0, keepdims=True)
            da_ref[:, cs] = (wc_ref[2:3, cs] * dac + wc_ref[1:2, cs] * dp1 + wc_ref[0:1, cs] * dp2).astype(BF16)
            nxt[:, cs] = dac[0:8]

        for ci in range(D_FF // CW):
            gs = slice(ci * CW, (ci + 1) * CW)
            vs = slice(D_FF + ci * CW, D_FF + (ci + 1) * CW)
            dy = _dot_nt(dxb, wd_ref[gs, :])
            ag, av = ac_ref[:, gs], ac_ref[:, vs]
            sg = jax.nn.sigmoid(ag)
            back(vs, dy * (ag * sg))
            back(gs, dy * av * (sg * (1.0 + ag * (1.0 - sg))))

    return pl.pallas_call(
        body, name="bwd_ffn", grid=(nt,),
        in_specs=[_rows(tm, D_MODEL, nt), _rows(tm, D_FF2, nt), _rows(tm, D_FF2, nt), _full((D_FF, D_MODEL), True),
                  _full((3, D_FF2))],
        out_specs=[_rows(tm, D_FF2, nt), _full((3, D_FF2)), _full((1, D_FF2))],
        out_shape=[jax.ShapeDtypeStruct((T, D_FF2), BF16), jax.ShapeDtypeStruct((3, D_FF2), F32),
                   jax.ShapeDtypeStruct((1, D_FF2), F32)],
        scratch_shapes=[pltpu.VMEM((8, D_FF2), F32), pltpu.VMEM((2, tm, CW), F32)],
        compiler_params=_cp("arbitrary"),
    )(dx3, a, ac, w_down, wc)


def _bwd_mid(da, w_up, x2, g2, dx3, w_out, ob):
    T = x2.shape[0]
    tm = min(TM, T)

    def body(da_ref, wu_ref, x2_ref, g_ref, dx3_ref, wo_ref, ob_ref, hsum_ref, place_ref, dx2_ref, doa_ref, dob_ref, dg2_ref):
        @pl.when(pl.program_id(0) == 0)
        def _():
            dg2_ref[...] = jnp.zeros_like(dg2_ref)

        dh2 = _dot_nt(da_ref[...], wu_ref[...])
        r, n = _rms(x2_ref[...])
        dg2_ref[...] += jnp.sum(dh2 * n, axis=0, keepdims=True)
        dx2 = dx3_ref[...] + _rms_bwd(dh2, n, r, g_ref[...])
        dx2_ref[...] = dx2
        doab = _dot_nt(dx2.astype(BF16), wo_ref[...])
        doa_ref[...] = doab[:, :D_A]
        dob = doab[:, D_A:]
        rest, delta = dob * ob_ref[...], None
        for _ in range(3):
            piece = rest.astype(BF16)
            term = _dot(piece, hsum_ref[...])
            delta = term if delta is None else delta + term
            rest = rest - piece.astype(F32)
        hi, mid, lo = _split3(-delta)
        parts = jnp.concatenate([hi.astype(BF16), mid.astype(BF16), lo.astype(BF16)], axis=-1)
        placed = _dot(parts, place_ref[...])
        data = lax.broadcasted_iota(jnp.int32, (tm, LANES), 1) < HEAD_DIM
        for hd in range(N_HEADS):
            tile = dob[:, LANES * (hd // 2):LANES * (hd // 2 + 1)]
            rows = tile if hd % 2 == 0 else pltpu.roll(tile, HEAD_DIM, axis=1)
            dob_ref[hd] = jnp.where(data, rows, placed[:, LANES * hd:LANES * (hd + 1)]).astype(BF16)

    hsum = (jnp.arange(D_B)[:, None] // HEAD_DIM == jnp.arange(LANES)[None, :]).astype(BF16)
    j, h = jnp.arange(3 * LANES) // LANES, jnp.arange(3 * LANES) % LANES
    place = ((h[:, None] < N_HEADS)
             & (jnp.arange(N_HEADS * LANES)[None, :] == LANES * h[:, None] + L_ROW + j[:, None])).astype(BF16)
    return pl.pallas_call(
        body, name="bwd_mid", grid=(T // tm,),
        in_specs=[_rows(tm, D_FF2), _full((D_MODEL, D_FF2), True), _rows(tm, D_MODEL), _full((1, D_MODEL)), _rows(tm, D_MODEL),
                  _full((D_MODEL, D_MODEL), True), _rows(tm, D_B), _full((D_B, LANES)), _full((3 * LANES, N_HEADS * LANES))],
        out_specs=[_rows(tm, D_MODEL), _rows(tm, D_A), pl.BlockSpec((N_HEADS, tm, LANES), lambda i: (0, i, 0)),
                   _full((1, D_MODEL))],
        out_shape=[jax.ShapeDtypeStruct((T, D_MODEL), F32), jax.ShapeDtypeStruct((T, D_A), F32),
                   jax.ShapeDtypeStruct((N_HEADS, T, LANES), BF16), jax.ShapeDtypeStruct((1, D_MODEL), F32)],
        compiler_params=_cp("arbitrary"),
    )(da, w_up, x2, g2, dx3, w_out, ob, hsum, place)


def _bwd_in(dzuv, dqa, dka, dva, fl, x, dx2, w_in_p, g1):
    T = x.shape[0]
    tm = min(TM, T)
    nt = T // tm

    def body(dzuv_ref, dqa_ref, dka_ref, dva_ref, fl_ref, x_ref, dx2_ref, w_ref, g_ref,
             gx_ref, dz_ref, dg1_ref, dfb_ref, carry):
        @pl.when(pl.program_id(0) == 0)
        def _():
            carry[...] = jnp.zeros_like(carry)
            dg1_ref[...] = jnp.zeros_like(dg1_ref)
            dfb_ref[...] = jnp.zeros_like(dfb_ref)

        dc = _lanes(tm, LANES, {hd: dqa_ref[hd][:, L_ROW:L_ROW + 1] - dka_ref[hd][:, L_COL:L_COL + 1] for hd in range(N_HEADS)})
        later = (lax.broadcasted_iota(jnp.int32, (tm, tm), 1) >= lax.broadcasted_iota(jnp.int32, (tm, tm), 0)).astype(F32)
        dls = _dot_f32(later, dc) + carry[...]
        carry[...] = dls[0:1, :]
        dzf = dls * jax.nn.sigmoid(-fl_ref[...])
        dfb_ref[...] += jnp.sum(dzf, axis=0, keepdims=True)
        data = lax.broadcasted_iota(jnp.int32, (tm, LANES), 1) < HEAD_DIM

        def compact(ref):
            return [jnp.where(data, ref[2 * p], pltpu.roll(ref[2 * p + 1], HEAD_DIM, axis=1)) for p in range(N_HEADS // 2)]

        dz = jnp.concatenate([dzuv_ref[...], *[(t * (HEAD_DIM ** -0.5)).astype(BF16) for t in compact(dqa_ref)],
                              *[t.astype(BF16) for t in compact(dka_ref)], *compact(dva_ref), dzf.astype(BF16)], axis=-1)
        dz_ref[...] = dz
        dh1 = _dot_nt(dz, w_ref[...])
        r, n = _rms(x_ref[...])
        dg1_ref[...] += jnp.sum(dh1 * n, axis=0, keepdims=True)
        gx_ref[...] = dx2_ref[...] + _rms_bwd(dh1, n, r, g_ref[...])

    rv = functools.partial(_rows, tm, rev_nt=nt)
    heads = pl.BlockSpec((N_HEADS, tm, LANES), lambda i: (0, nt - 1 - i, 0))
    return pl.pallas_call(
        body, name="bwd_in", grid=(nt,),
        in_specs=[rv(2 * D_A), heads, heads, heads, rv(LANES), rv(D_MODEL), rv(D_MODEL),
                  _full((D_MODEL, D_IN_PAD), True), _full((1, D_MODEL))],
        out_specs=[rv(D_MODEL), rv(D_IN_PAD), _full((1, D_MODEL)), _full((1, LANES))],
        out_shape=[jax.ShapeDtypeStruct((T, D_MODEL), F32), jax.ShapeDtypeStruct((T, D_IN_PAD), BF16),
                   jax.ShapeDtypeStruct((1, D_MODEL), F32), jax.ShapeDtypeStruct((1, LANES), F32)],
        scratch_shapes=[pltpu.VMEM((1, LANES), F32)],
        compiler_params=_cp("arbitrary"),
    )(dzuv, dqa, dka, dva, fl, x, dx2, w_in_p, g1)


def _matmul_tn(a, b, tmm, tn, name):
    T, M = a.shape
    N = b.shape[1]
    tk = min(512, T)

    def body(a_ref, b_ref, o_ref):
        @pl.when(pl.program_id(2) == 0)
        def _():
            o_ref[...] = jnp.zeros_like(o_ref)

        o_ref[...] += _dot_tn(a_ref[...].astype(BF16), b_ref[...].astype(BF16))

    return pl.pallas_call(
        body, name=name, grid=(M // tmm, N // tn, T // tk),
        in_specs=[pl.BlockSpec((tk, tmm), lambda i, j, k: (k, i)), pl.BlockSpec((tk, tn), lambda i, j, k: (k, j))],
        out_specs=pl.BlockSpec((tmm, tn), lambda i, j, k: (i, j)),
        out_shape=jax.ShapeDtypeStruct((M, N), F32),
        compiler_params=_cp("arbitrary", "arbitrary", "arbitrary"),
    )(a, b)


class _Exchange:
    def __init__(self, gather, scatter, relay):
        self.n_g, self.n, self.relay = len(gather), len(gather) + len(scatter), relay
        self.operands = [*gather, *scatter]
        self.out_shape = [jax.ShapeDtypeStruct((N_DEV, *g.shape), g.dtype) for g in gather]
        self.out_shape += [jax.ShapeDtypeStruct(s.shape, s.dtype) for s in scatter]
        self.specs = [ANY] * self.n
        n = self.n
        self.scratch = [pltpu.SemaphoreType.DMA((7 * n,)), pltpu.SemaphoreType.DMA((7 * n,)),
                        pltpu.SemaphoreType.DMA((n,))] if n else []

    def _plan(self, ins, outs, sems):
        send_sems, recv_sems, local_sems = sems
        x, y, c = (lax.axis_index(ax) for ax in MESH_AXES)
        me = 4 * x + 2 * y + c
        sibling = (x, y, 1 - c)
        chips = [(1 - x, y), (x, 1 - y), (1 - x, 1 - y)]
        peers = [sibling] + [(*chip, c) for chip in chips] + [(*chip, 1 - c) for chip in chips]

        def index(dev):
            return 4 * dev[0] + 2 * dev[1] + dev[2]

        def remote(k, src, dst, to):
            return pltpu.make_async_remote_copy(src_ref=src, dst_ref=dst, send_sem=send_sems.at[k], recv_sem=recv_sems.at[k],
                                                device_id=to, device_id_type=pl.DeviceIdType.MESH)

        local, sends, relays, recvs = [], [], [], []
        for a in range(self.n):
            src, out, base = ins[a], outs[a], 7 * a
            if a >= self.n_g:
                local.append(pltpu.make_async_copy(src.at[me], out.at[me], local_sems.at[a]))
                sends += [remote(base + k, src.at[index(peer)], out.at[me], peer) for k, peer in enumerate(peers)]
            else:
                local.append(pltpu.make_async_copy(src, out.at[me], local_sems.at[a]))
                sends += [remote(base + k, src, out.at[me], peer) for k, peer in enumerate(peers[:4 if self.relay else 7])]
            for k, peer in enumerate(peers):
                slot = out.at[index(peer)]
                if a < self.n_g and self.relay and k >= 4:
                    continue
                recv = remote(base + k, slot, slot, peer)
                if a < self.n_g and self.relay and k >= 1:
                    relays.append((recv, remote(base + 3 + k, slot, slot, sibling)))
                else:
                    recvs.append(recv)
            if a < self.n_g and self.relay:
                for j, chip in enumerate(chips):
                    slot = out.at[index((*chip, 1 - c))]
                    recvs.append(remote(base + 4 + j, slot, slot, sibling))
        return local, sends, relays, recvs

    def start(self, ins, outs, sems):
        local, sends, _, _ = self._plan(ins, outs, sems)
        for cp in local + sends:
            cp.start()

    def finish(self, ins, outs, sems):
        local, sends, relays, recvs = self._plan(ins, outs, sems)
        for recv, fwd in relays:
            recv.wait_recv()
            fwd.start()
        for recv in recvs:
            recv.wait_recv()
        for cp in sends + [fwd for _, fwd in relays]:
            cp.wait_send()
        for cp in local:
            cp.wait()


def _exchange(gather, scatter, name):
    ex = _Exchange(gather, scatter, relay=True)
    n = ex.n

    def body(*refs):
        ins, outs, sems = refs[:n], refs[n:2 * n], refs[2 * n:]
        ex.start(ins, outs, sems)
        ex.finish(ins, outs, sems)

    return pl.pallas_call(body, name=name, in_specs=ex.specs, out_specs=ex.specs, out_shape=ex.out_shape,
                          scratch_shapes=ex.scratch)(*ex.operands)


def _adamw(w, g, m, v):
    m = ADAM_B1 * m + (1.0 - ADAM_B1) * g
    v = ADAM_B2 * v + (1.0 - ADAM_B2) * jnp.square(g)
    m_hat = m / (1.0 - ADAM_B1 ** ADAM_STEP)
    v_hat = v / (1.0 - ADAM_B2 ** ADAM_STEP)
    delta = -ADAM_LR * (m_hat / (jnp.sqrt(v_hat) + ADAM_EPS) + ADAM_WD * w)
    return delta, m, v


def _adamw_shard(w, m, v, own, recv, tr, name):
    R, C = w.shape

    def body(w_ref, m_ref, v_ref, own_ref, recv_ref, g_ref, d_ref, nm_ref, nv_ref):
        x, y, c = (lax.axis_index(ax) for ax in MESH_AXES)
        me = 4 * x + 2 * y + c
        g = own_ref[...]
        for d in range(N_DEV):
            g = g + jnp.where(d == me, 0.0, recv_ref[d].astype(F32))
        g_ref[...] = g
        d_ref[...], nm_ref[...], nv_ref[...] = _adamw(w_ref[...], g, m_ref[...], v_ref[...])

    blk = pl.BlockSpec((tr, C), lambda i: (i, 0))
    return pl.pallas_call(
        body, name=name, grid=(R // tr,),
        in_specs=[blk, blk, blk, blk, pl.BlockSpec((N_DEV, tr, C), lambda i: (0, i, 0))],
        out_specs=[blk] * 4, out_shape=[jax.ShapeDtypeStruct((R, C), F32)] * 4,
        compiler_params=_cp("arbitrary"),
    )(w, m, v, own, recv)


def _adamw_small(w, m, v, gall):
    R = w.shape[0]
    tr = R // 3

    def body(w_ref, m_ref, v_ref, gall_ref, g_ref, d_ref, nm_ref, nv_ref):
        g = gall_ref[0]
        for d in range(1, N_DEV):
            g = g + gall_ref[d]
        g_ref[...] = g
        d_ref[...], nm_ref[...], nv_ref[...] = _adamw(w_ref[...], g, m_ref[...], v_ref[...])

    blk = pl.BlockSpec((tr, LANES), lambda i: (i, 0))
    return pl.pallas_call(
        body, name="adamw_small", grid=(R // tr,),
        in_specs=[blk, blk, blk, pl.BlockSpec((N_DEV, tr, LANES), lambda i: (0, i, 0))],
        out_specs=[blk] * 4, out_shape=[jax.ShapeDtypeStruct((R, LANES), F32)] * 4,
        compiler_params=_cp("arbitrary"),
    )(w, m, v, gall)


_SMALL_LATE = (("g1", 8), ("fb", 8))
_SMALL_EARLY = (("lg", 8), ("sg_w", 1024), ("sg_b", 8), ("g2", 8), ("bc", 48), ("g3", 8), ("loss", 8))
_SMALL = _SMALL_LATE + _SMALL_EARLY


def _pack_small(parts, layout):
    rows = []
    for name, nrow in layout:
        flat = parts[name].astype(F32).reshape(-1)
        flat = jnp.pad(flat, (0, nrow * LANES - flat.shape[0]))
        rows.append(flat.reshape(nrow, LANES))
    return jnp.concatenate(rows, axis=0)


def _unpack_small(packed, shapes):
    out, r = {}, 0
    for name, nrow in _SMALL:
        size = math.prod(shapes[name])
        out[name] = packed[r:r + nrow].reshape(-1)[:size].reshape(shapes[name])
        r += nrow
    return out


def _col_shards(g):
    return jnp.transpose(g.reshape(g.shape[0], N_DEV, -1), (1, 0, 2))


def _row_shards(g):
    return g.reshape(N_DEV, -1, g.shape[1])


def _cols_whole(g):
    return jnp.transpose(g, (1, 0, 2)).reshape(g.shape[1], -1)


def kernel(x, norm_mix_g, w_in, f_bias, sg_ln_g, sg_w, sg_b, w_out, norm_ffn_g, w_up, w_conv, b_conv, w_down, norm_final_g, loss_target, m_norm_mix_g, m_w_in, m_f_bias, m_sg_ln_g, m_sg_w, m_sg_b, m_w_out, m_norm_ffn_g, m_w_up, m_w_conv, m_b_conv, m_w_down, m_norm_final_g, v_norm_mix_g, v_w_in, v_f_bias, v_sg_ln_g, v_sg_w, v_sg_b, v_w_out, v_norm_ffn_g, v_w_up, v_w_conv, v_b_conv, v_w_down, v_norm_final_g):
    me = 4 * lax.axis_index("x") + 2 * lax.axis_index("y") + lax.axis_index("c")
    xs, tgt = x[0], loss_target[0]
    g1, g2, g3 = norm_mix_g, norm_ffn_g, norm_final_g.reshape(1, D_MODEL)
    lg = sg_ln_g.reshape(1, D_A)
    fb = jnp.pad(f_bias, ((0, 0), (0, LANES - N_HEADS)))
    pos_chunk = jnp.arange(SG_BLOCK) // SG_CHUNK
    maskf = (pos_chunk[:, None] >= pos_chunk[None, :]).astype(F32)
    wm = (sg_w[0] * maskf[None]).astype(BF16)
    wmt = jnp.swapaxes(wm, 1, 2)
    bfull = jnp.repeat(sg_b[0].T, HEAD_DIM, axis=1)
    hsel = jnp.repeat(jnp.eye(N_HEADS, dtype=F32), HEAD_DIM, axis=1)
    avg = (jnp.repeat(hsel, HEAD_DIM, axis=0) * (1.0 / HEAD_DIM)).astype(BF16)

    (win_g,) = _exchange([w_in[0].astype(BF16)], [], "gather_w_in")
    w_in_p = jnp.pad(_cols_whole(win_g), ((0, 0), (0, D_IN_PAD - D_IN)))
    zuv, qa, ka, va, fl, h1 = _fwd_in(xs, g1, w_in_p, fb)
    oa = _gmlp_fwd(zuv, lg, wm, bfull, avg)
    rest = _Exchange([w_out[0].astype(BF16), w_up[0].astype(BF16), w_down[0].astype(BF16), w_conv[0]], [], relay=False)
    ob, lse, (wout_g, wup_g, wdown_g, wc_g) = _fox_fwd(qa, ka, va, rest)
    w_out_f, w_up_f = wout_g.reshape(D_MODEL, D_MODEL), _cols_whole(wup_g)
    w_down_f, wc_f = wdown_g.reshape(D_FF, D_MODEL), _cols_whole(wc_g)

    x2, h2, a = _fwd_mid(xs, oa, ob, w_out_f, g2, w_up_f)
    ac, yff, dx3, loss, dg3 = _fwd_ffn(a, x2, wc_f, b_conv, w_down_f, g3, tgt)
    da, dwc, dbc = _bwd_ffn(dx3, a, ac, w_down_f, wc_f)
    dx2, doa, dob, dg2 = _bwd_mid(da, w_up_f, x2, g2, dx3, w_out_f, ob)
    dzuv, dwm, dsgb, dlg = _gmlp_bwd(zuv, doa, lg, wm, wmt, bfull, maskf, hsel, avg)
    dwout = jnp.concatenate([_matmul_tn(oa, dx2, D_A, D_MODEL, "dw_out_a"), _matmul_tn(ob, dx2, D_B, D_MODEL, "dw_out_b")],
                            axis=0)
    dwup = _matmul_tn(h2, da, D_MODEL, D_FF2 // 4, "dw_up")
    dwdown = _matmul_tn(yff, dx3, D_FF // 2, D_MODEL, "dw_down")

    shards = dict(w_out=_row_shards(dwout), w_up=_col_shards(dwup), wc=_col_shards(dwc), w_down=_row_shards(dwdown))
    early = ("w_out", "w_up", "wc", "w_down")
    small = dict(lg=dlg, sg_w=dwm, sg_b=dsgb, g2=dg2, bc=dbc, g3=dg3, loss=loss[0:1, 0:1])
    grads = _Exchange([_pack_small(small, _SMALL_EARLY)], [shards[n].astype(BF16) for n in early], relay=False)
    dqa, dka, dva, got = _fox_bwd(qa, lse, dob, ka, va, grads)
    small_early, recv = got[0], dict(zip(early, got[1:]))

    gx, dz, dg1, dfb = _bwd_in(dzuv, dqa, dka, dva, fl, xs, dx2, w_in_p, g1)
    shards["w_in"] = _col_shards(_matmul_tn(h1, dz, D_MODEL, D_IN_PAD // 3, "dw_in")[:, :D_IN])
    small_late, recv["w_in"] = _exchange([_pack_small(dict(g1=dg1, fb=dfb), _SMALL_LATE)], [shards["w_in"].astype(BF16)],
                                         "exchange_w_in")

    weights = dict(w_in=(w_in, m_w_in, v_w_in, 256), w_out=(w_out, m_w_out, v_w_out, 128), w_up=(w_up, m_w_up, v_w_up, 256),
                   wc=(w_conv, m_w_conv, v_w_conv, 3), w_down=(w_down, m_w_down, v_w_down, 176))
    res = {}
    for n, (w, m, v, tr) in weights.items():
        own = lax.dynamic_index_in_dim(shards[n], me, axis=0, keepdims=False)
        res[n] = [r[None] for r in _adamw_shard(w[0], m[0], v[0], own, recv[n], tr, "adamw_" + n)]

    zero = jnp.zeros((1, 1), F32)
    reps = dict(g1=(norm_mix_g, m_norm_mix_g, v_norm_mix_g), fb=(f_bias, m_f_bias, v_f_bias), lg=(sg_ln_g, m_sg_ln_g, v_sg_ln_g),
                sg_w=(sg_w, m_sg_w, v_sg_w), sg_b=(sg_b, m_sg_b, v_sg_b), g2=(norm_ffn_g, m_norm_ffn_g, v_norm_ffn_g),
                bc=(b_conv, m_b_conv, v_b_conv), g3=(norm_final_g, m_norm_final_g, v_norm_final_g), loss=(zero, zero, zero))
    shapes = {n: t[0].shape for n, t in reps.items()}
    packed = _adamw_small(*[_pack_small({n: t[i] for n, t in reps.items()}, _SMALL) for i in range(3)],
                          jnp.concatenate([small_late, small_early], axis=1))
    unpacked = [_unpack_small(p, shapes) for p in packed]
    for n in reps:
        res[n] = [u[n] for u in unpacked]

    names = ("g1", "w_in", "fb", "lg", "sg_w", "sg_b", "w_out", "g2", "w_up", "wc", "bc", "w_down", "g3")
    return (res["loss"][0][0, 0], gx[None], *[res[n][0] for n in names], *[res[n][1] for n in names],
            *[res[n][2] for n in names], *[res[n][3] for n in names])
```

```python
import functools
import math

import jax
import jax.numpy as jnp
from jax import lax
from jax.experimental import pallas as pl
from jax.experimental.pallas import tpu as pltpu

F32 = jnp.float32
BF16 = jnp.bfloat16

D_MODEL = 1024
HEAD_DIM = 64
N_HEADS = 8
D_A = 512
D_B = 512
D_IN = 2 * D_A + 3 * D_B + N_HEADS
D_IN_PAD = 2688
D_FF = 2816
D_FF2 = 2 * D_FF
SG_BLOCK = 128
SG_CHUNK = 64
EPS = 1e-6
N_DEV = 8
LANES = 128
NEG = -1e30
VMEM_LIMIT = 56 * 1024 * 1024

ADAM_LR = 0.001
ADAM_B1 = 0.9
ADAM_B2 = 0.999
ADAM_EPS = 1e-08
ADAM_WD = 0.01
ADAM_STEP = 10

TM = 256
TQ = 512
FWD_UNROLL = 4
BWD_UNROLL = 2
CW = 256

MESH_AXES = ("x", "y", "c")
ANY = pl.BlockSpec(memory_space=pl.ANY)


def _cp(*sem):
    return pltpu.CompilerParams(dimension_semantics=sem, vmem_limit_bytes=VMEM_LIMIT)


def _dot(a, b):
    return jnp.dot(a, b, preferred_element_type=F32)


def _dot_nt(a, b):
    return lax.dot_general(a, b, (((1,), (1,)), ((), ())), preferred_element_type=F32)


def _dot_tn(a, b):
    return lax.dot_general(a, b, (((0,), (0,)), ((), ())), preferred_element_type=F32)


def _dot_f32(a, b):
    return jnp.dot(a, b, precision=lax.Precision.HIGHEST, preferred_element_type=F32)


def _gelu(z):
    return 0.5 * z * (1.0 + lax.erf(z * (1.0 / math.sqrt(2.0))))


def _gelu_grad(z):
    return 0.5 * (1.0 + lax.erf(z * (1.0 / math.sqrt(2.0)))) + z * jnp.exp(-0.5 * z * z) * (1.0 / math.sqrt(2.0 * math.pi))


def _log_sigmoid(x):
    return jnp.minimum(x, 0.0) - jnp.log1p(jnp.exp(-jnp.abs(x)))


def _rms(x):
    r = lax.rsqrt(jnp.mean(x * x, axis=-1, keepdims=True) + EPS)
    return r, x * r


def _rms_bwd(dy, n, r, g):
    dn = dy * g
    return r * (dn - n * jnp.mean(dn * n, axis=-1, keepdims=True))


def _full(shape, single=False):
    nd = len(shape)
    if single:
        return pl.BlockSpec(shape, lambda *_: (0,) * nd, pipeline_mode=pl.Buffered(1))
    return pl.BlockSpec(shape, lambda *_: (0,) * nd)


def _rows(tm, cols, rev_nt=None):
    if rev_nt is None:
        return pl.BlockSpec((tm, cols), lambda i: (i, 0))
    return pl.BlockSpec((tm, cols), lambda i: (rev_nt - 1 - i, 0))


def _head_sl(h):
    return slice(HEAD_DIM * h, HEAD_DIM * (h + 1))


L_ROW = HEAD_DIM
L_COL = HEAD_DIM + 3
L_LSE = HEAD_DIM + 6


def _split3(x):
    hi = x.astype(BF16).astype(F32)
    mid = (x - hi).astype(BF16).astype(F32)
    lo = (x - hi - mid).astype(BF16).astype(F32)
    return hi, mid, lo


def _lanes(rows, width, parts):
    lane = lax.broadcasted_iota(jnp.int32, (rows, width), 1)
    out = jnp.zeros((rows, width), F32)
    for at, val in parts.items():
        out = jnp.where(lane == at, val, out)
    return out


def _fwd_in(x, g1, w_in_p, fb):
    T = x.shape[0]
    tm = min(TM, T)

    def body(x_ref, g_ref, w_ref, fb_ref, place_ref, zuv_ref, qa_ref, ka_ref, va_ref, fl_ref, h1_ref, carry):
        @pl.when(pl.program_id(0) == 0)
        def _():
            carry[...] = jnp.zeros_like(carry)

        r, n = _rms(x_ref[...])
        h = (n * g_ref[...]).astype(BF16)
        h1_ref[...] = h
        z = _dot(h, w_ref[...])
        zuv_ref[...] = z[:, :2 * D_A]
        o = 2 * D_A
        fl = z[:, o + 3 * D_B:] + fb_ref[...]
        fl_ref[...] = fl
        tri = (lax.broadcasted_iota(jnp.int32, (tm, tm), 0) >= lax.broadcasted_iota(jnp.int32, (tm, tm), 1)).astype(F32)
        c = _dot_f32(tri, _log_sigmoid(fl)) + carry[...]
        carry[...] = c[tm - 1:tm, :]
        hi, mid, lo = _split3(c)
        parts = jnp.concatenate([hi.astype(BF16), mid.astype(BF16), lo.astype(BF16)], axis=-1)
        placed = _dot(parts, place_ref[...])
        lane = lax.broadcasted_iota(jnp.int32, (tm, LANES), 1)
        data = lane < HEAD_DIM
        ones_q = ((lane >= L_COL) & (lane < L_COL + 3)).astype(F32)
        ones_k = (((lane >= L_ROW) & (lane < L_ROW + 3)) | ((lane >= L_LSE) & (lane < L_LSE + 3))).astype(F32)
        ones_v = ((lane >= L_ROW) & (lane < L_ROW + 3)).astype(F32)
        for hd in range(N_HEADS):
            def rows_of(first_col):
                tile = z[:, first_col + LANES * (hd // 2):first_col + LANES * (hd // 2 + 1)]
                return tile if hd % 2 == 0 else pltpu.roll(tile, HEAD_DIM, axis=1)

            hs = slice(LANES * hd, LANES * (hd + 1))
            qa_ref[hd] = jnp.where(data, rows_of(o) * (HEAD_DIM ** -0.5), placed[:, hs] + ones_q).astype(BF16)
            ka_ref[hd] = jnp.where(data, rows_of(o + D_B), ones_k - placed[:, N_HEADS * LANES:][:, hs]).astype(BF16)
            va_ref[hd] = jnp.where(data, rows_of(o + 2 * D_B), ones_v).astype(BF16)

    heads = pl.BlockSpec((N_HEADS, tm, LANES), lambda i: (0, i, 0))
    aug = jax.ShapeDtypeStruct((N_HEADS, T, LANES), BF16)
    j, h = jnp.arange(3 * LANES) // LANES, jnp.arange(3 * LANES) % LANES
    cols = jnp.arange(2 * N_HEADS * LANES)
    place = ((h[:, None] < N_HEADS) & ((cols[None, :] == LANES * h[:, None] + L_ROW + j[:, None])
                                       | (cols[None, :] == N_HEADS * LANES + LANES * h[:, None] + L_COL + j[:, None]))).astype(BF16)
    return pl.pallas_call(
        body, name="fwd_in", grid=(T // tm,),
        in_specs=[_rows(tm, D_MODEL), _full((1, D_MODEL)), _full((D_MODEL, D_IN_PAD), True), _full((1, LANES)),
                  _full((3 * LANES, 2 * N_HEADS * LANES))],
        out_specs=[_rows(tm, 2 * D_A), heads, heads, heads, _rows(tm, LANES), _rows(tm, D_MODEL)],
        out_shape=[jax.ShapeDtypeStruct((T, 2 * D_A), F32), aug, aug, aug, jax.ShapeDtypeStruct((T, LANES), F32),
                   jax.ShapeDtypeStruct((T, D_MODEL), BF16)],
        scratch_shapes=[pltpu.VMEM((1, LANES), F32)],
        compiler_params=_cp("arbitrary"),
    )(x, g1, w_in_p, fb, place)


def _segmean(x, avg_ref, parts):
    out, rest = None, x
    for _ in range(parts):
        piece = rest.astype(BF16)
        term = _dot(piece, avg_ref[...])
        out = term if out is None else out + term
        rest = rest - piece.astype(F32)
    return out


def _layer_norm(v, lg, avg_ref):
    d = v - _segmean(v, avg_ref, 3)
    rstd = lax.rsqrt(_segmean(d * d, avg_ref, 2) + EPS)
    vhat = d * rstd
    return vhat, rstd, (vhat * lg).astype(BF16)


def _mix_heads(w_ref, x, row_slice):
    low = lax.broadcasted_iota(jnp.int32, (SG_BLOCK, LANES), 1) < HEAD_DIM
    tiles = []
    for p in range(N_HEADS // 2):
        xt = x[row_slice, LANES * p:LANES * (p + 1)]
        zero = jnp.zeros_like(xt)
        tiles.append(_dot(w_ref[2 * p], jnp.where(low, xt, zero)) + _dot(w_ref[2 * p + 1], jnp.where(low, zero, xt)))
    return jnp.concatenate(tiles, axis=-1)


def _gmlp_fwd(zuv, lg, wm, bfull, avg):
    T = zuv.shape[0]
    tg = min(TM, T)
    nb = tg // SG_BLOCK

    def body(zuv_ref, lg_ref, wm_ref, b_ref, avg_ref, oa_ref):
        u = _gelu(zuv_ref[:, :D_A])
        _, _, vn = _layer_norm(_gelu(zuv_ref[:, D_A:]), lg_ref[...], avg_ref)
        for n in range(nb):
            rs = slice(n * SG_BLOCK, (n + 1) * SG_BLOCK)
            oa_ref[rs, :] = (u[rs] * (_mix_heads(wm_ref, vn, rs) + b_ref[...])).astype(BF16)

    return pl.pallas_call(
        body, name="gmlp_fwd", grid=(T // tg,),
        in_specs=[_rows(tg, 2 * D_A), _full((1, D_A)), _full((N_HEADS, SG_BLOCK, SG_BLOCK)), _full((SG_BLOCK, D_A)),
                  _full((D_A, D_A))],
        out_specs=_rows(tg, D_A),
        out_shape=jax.ShapeDtypeStruct((T, D_A), BF16),
        compiler_params=_cp("arbitrary"),
    )(zuv, lg, wm, bfull, avg)


def _gmlp_bwd(zuv, doa, lg, wm, wmt, bfull, maskf, hsel, avg):
    T = zuv.shape[0]
    tg = min(TM, T)
    nb = tg // SG_BLOCK
    nt = T // tg

    def body(zuv_ref, doa_ref, lg_ref, wm_ref, wmt_ref, b_ref, mask_ref, hsel_ref, avg_ref,
             dzuv_ref, dwm_ref, dsgb_ref, dlg_ref, dbacc):
        i = pl.program_id(0)

        @pl.when(i == 0)
        def _():
            dwm_ref[...] = jnp.zeros_like(dwm_ref)
            dlg_ref[...] = jnp.zeros_like(dlg_ref)
            dbacc[...] = jnp.zeros_like(dbacc)

        zu, zv = zuv_ref[:, :D_A], zuv_ref[:, D_A:]
        lgv = lg_ref[...]
        vhat, rstd, vn = _layer_norm(_gelu(zv), lgv, avg_ref)
        dmixed = doa_ref[...] * _gelu(zu)
        dmb = dmixed.astype(BF16)
        low = lax.broadcasted_iota(jnp.int32, (SG_BLOCK, LANES), 1) < HEAD_DIM
        mixed, dvn = [], []
        for n in range(nb):
            rs = slice(n * SG_BLOCK, (n + 1) * SG_BLOCK)
            mixed.append(_mix_heads(wm_ref, vn, rs) + b_ref[...])
            dvn.append(_mix_heads(wmt_ref, dmb, rs))
            dbacc[...] += dmixed[rs]
            for p in range(N_HEADS // 2):
                ls = slice(LANES * p, LANES * (p + 1))
                dmt, vnt = dmb[rs, ls], vn[rs, ls]
                zero = jnp.zeros_like(dmt)
                dwm_ref[2 * p] += _dot_nt(jnp.where(low, dmt, zero), vnt) * mask_ref[...]
                dwm_ref[2 * p + 1] += _dot_nt(jnp.where(low, zero, dmt), vnt) * mask_ref[...]
        mixed = jnp.concatenate(mixed, axis=0) if nb > 1 else mixed[0]
        dvn = jnp.concatenate(dvn, axis=0) if nb > 1 else dvn[0]
        dlg_ref[...] += jnp.sum(dvn * vhat, axis=0, keepdims=True)
        dvhat = dvn * lgv
        dv = rstd * (dvhat - _segmean(dvhat, avg_ref, 3) - vhat * _segmean(dvhat * vhat, avg_ref, 3))
        dzuv_ref[:, :D_A] = (doa_ref[...] * mixed * _gelu_grad(zu)).astype(BF16)
        dzuv_ref[:, D_A:] = (dv * _gelu_grad(zv)).astype(BF16)

        @pl.when(i == nt - 1)
        def _():
            dsgb_ref[...] = lax.dot_general(hsel_ref[...], dbacc[...], (((1,), (1,)), ((), ())),
                                            precision=lax.Precision.HIGHEST, preferred_element_type=F32)

    return pl.pallas_call(
        body, name="gmlp_bwd", grid=(nt,),
        in_specs=[_rows(tg, 2 * D_A), _rows(tg, D_A), _full((1, D_A)), _full((N_HEADS, SG_BLOCK, SG_BLOCK)),
                  _full((N_HEADS, SG_BLOCK, SG_BLOCK)), _full((SG_BLOCK, D_A)), _full((SG_BLOCK, SG_BLOCK)),
                  _full((N_HEADS, D_A)), _full((D_A, D_A))],
        out_specs=[_rows(tg, 2 * D_A), _full((N_HEADS, SG_BLOCK, SG_BLOCK)), _full((N_HEADS, SG_BLOCK)), _full((1, D_A))],
        out_shape=[jax.ShapeDtypeStruct((T, 2 * D_A), BF16), jax.ShapeDtypeStruct((N_HEADS, SG_BLOCK, SG_BLOCK), F32),
                   jax.ShapeDtypeStruct((N_HEADS, SG_BLOCK), F32), jax.ShapeDtypeStruct((1, D_A), F32)],
        scratch_shapes=[pltpu.VMEM((SG_BLOCK, D_A), F32)],
        compiler_params=_cp("arbitrary"),
    )(zuv, doa, lg, wm, wmt, bfull, maskf, hsel, avg)


def _fox_fwd(qa, ka, va, side):
    T = qa.shape[1]
    tq = min(TQ, T)
    nq = T // tq
    n = side.n

    def body(qa_ref, ka_ref, va_ref, *rest):
        ins, (o_ref, lse_ref), outs, sems = rest[:n], rest[n:n + 2], rest[n + 2:2 * n + 2], rest[2 * n + 2:]
        i = pl.program_id(1)
        if n:
            @pl.when((pl.program_id(0) == 0) & (i == 0))
            def _():
                side.start(ins, outs, sems)
        row = lax.broadcasted_iota(jnp.int32, (tq, tq), 0)
        col = lax.broadcasted_iota(jnp.int32, (tq, tq), 1)
        qs = [qa_ref[0], qa_ref[1]]

        def tiles(js, carry, diag):
            offs = [pl.multiple_of(j * tq, tq) for j in js]
            logits = [[_dot_nt(qs[hh], ka_ref[hh, pl.ds(off, tq), :]) for hh in range(2)] for off in offs]
            carry = list(carry)
            for off, per_head in zip(offs, logits):
                for hh, s in enumerate(per_head):
                    if diag:
                        s = jnp.where(col <= row, s, NEG)
                    m, acc = carry[hh]
                    m_new = jnp.maximum(m, jnp.max(s, axis=-1, keepdims=True))
                    pr = jnp.exp(s - m_new)
                    acc = jnp.exp(m - m_new) * acc + _dot(pr.astype(BF16), va_ref[hh, pl.ds(off, tq), :])
                    carry[hh] = (m_new, acc)
            return tuple(carry)

        init = ((jnp.full((tq, 1), NEG, F32), jnp.zeros((tq, LANES), F32)),) * 2
        carry = lax.fori_loop(0, i // FWD_UNROLL, lambda t, cr: tiles([FWD_UNROLL * t + u for u in range(FWD_UNROLL)], cr, False), init)
        carry = lax.fori_loop(i - i % FWD_UNROLL, i, lambda j, cr: tiles([j], cr, False), carry)
        carry = tiles([i], carry, True)
        for hh in range(2):
            m, acc = carry[hh]
            l = acc[:, L_ROW:L_ROW + 1]
            o_ref[:, _head_sl(hh)] = acc[:, :HEAD_DIM] / l
            hi, mid, lo = _split3(-(m + jnp.log(l)))
            lse_ref[hh] = _lanes(tq, LANES, {L_LSE: hi, L_LSE + 1: mid, L_LSE + 2: lo}).astype(BF16)
        if n:
            @pl.when((pl.program_id(0) == N_HEADS // 2 - 1) & (i == nq - 1))
            def _():
                side.finish(ins, outs, sems)

    tile = pl.BlockSpec((2, tq, LANES), lambda p, i: (p, i, 0))
    seq = pl.BlockSpec((2, T, LANES), lambda p, i: (p, 0, 0))
    res = pl.pallas_call(
        body, name="fox_fwd", grid=(N_HEADS // 2, nq),
        in_specs=[tile, seq, seq, *side.specs],
        out_specs=[pl.BlockSpec((tq, LANES), lambda p, i: (i, p)), tile, *side.specs],
        out_shape=[jax.ShapeDtypeStruct((T, D_B), F32), jax.ShapeDtypeStruct((N_HEADS, T, LANES), BF16), *side.out_shape],
        scratch_shapes=side.scratch,
        compiler_params=_cp("arbitrary", "arbitrary"),
    )(qa, ka, va, *side.operands)
    return res[0], res[1], res[2:]


def _fox_bwd(qa, lse, doa, ka, va, side):
    T = qa.shape[1]
    tq = min(TQ, T)
    nq = T // tq
    n = side.n

    def body(qa_ref, lse_ref, doa_ref, ka_ref, va_ref, *rest):
        ins, (dqa_ref, dka_ref, dva_ref), outs, sems = rest[:n], rest[n:n + 3], rest[n + 3:2 * n + 3], rest[2 * n + 3:]
        j = pl.program_id(1)
        if n:
            @pl.when((pl.program_id(0) == 0) & (j == 0))
            def _():
                side.start(ins, outs, sems)

        @pl.when(j == 0)
        def _():
            dqa_ref[...] = jnp.zeros_like(dqa_ref)

        row = lax.broadcasted_iota(jnp.int32, (tq, tq), 0)
        col = lax.broadcasted_iota(jnp.int32, (tq, tq), 1)
        ks = [ka_ref[0], ka_ref[1]]
        vs = [va_ref[0], va_ref[1]]

        def tiles(ids, carry, diag):
            work = []
            for i in ids:
                off = pl.multiple_of(i * tq, tq)
                for hh in range(2):
                    qi = qa_ref[hh, pl.ds(off, tq), :] + lse_ref[hh, pl.ds(off, tq), :]
                    doi = doa_ref[hh, pl.ds(off, tq), :]
                    work.append((off, hh, qi, doi, _dot_nt(ks[hh], qi), _dot_nt(vs[hh], doi)))
            carry = list(carry)
            for off, hh, qi, doi, st, dpt in work:
                dk, dv = carry[hh]
                if diag:
                    st = jnp.where(row <= col, st, NEG)
                pt = jnp.exp(st)
                dv = dv + _dot(pt.astype(BF16), doi)
                dsb = (pt * dpt).astype(BF16)
                dk = dk + _dot(dsb, qi)
                dqa_ref[hh, pl.ds(off, tq), :] += _dot_tn(dsb, ks[hh])
                carry[hh] = (dk, dv)
            return tuple(carry)

        init = ((jnp.zeros((tq, LANES), F32), jnp.zeros((tq, LANES), F32)),) * 2
        carry = tiles([j], init, True)
        todo = nq - 1 - j
        carry = lax.fori_loop(0, todo // BWD_UNROLL,
                              lambda t, cr: tiles([j + 1 + BWD_UNROLL * t + u for u in range(BWD_UNROLL)], cr, False), carry)
        carry = lax.fori_loop(nq - todo % BWD_UNROLL, nq, lambda i, cr: tiles([i], cr, False), carry)
        for hh in range(2):
            dka_ref[hh] = carry[hh][0]
            dva_ref[hh] = carry[hh][1].astype(BF16)
        if n:
            @pl.when((pl.program_id(0) == N_HEADS // 2 - 1) & (j == nq - 1))
            def _():
                side.finish(ins, outs, sems)

    tile = pl.BlockSpec((2, tq, LANES), lambda p, j: (p, j, 0))
    seq = pl.BlockSpec((2, T, LANES), lambda p, j: (p, 0, 0))
    res = pl.pallas_call(
        body, name="fox_bwd", grid=(N_HEADS // 2, nq),
        in_specs=[seq, seq, seq, tile, tile, *side.specs],
        out_specs=[seq, tile, tile, *side.specs],
        out_shape=[jax.ShapeDtypeStruct((N_HEADS, T, LANES), F32), jax.ShapeDtypeStruct((N_HEADS, T, LANES), F32),
                   jax.ShapeDtypeStruct((N_HEADS, T, LANES), BF16), *side.out_shape],
        scratch_shapes=side.scratch,
        compiler_params=_cp("arbitrary", "arbitrary"),
    )(qa, lse, doa, ka, va, *side.operands)
    return res[0], res[1], res[2], res[3:]


def _fwd_mid(x, oa, ob, w_out, g2, w_up):
    T = x.shape[0]
    tm = min(TM, T)

    def body(x_ref, oa_ref, ob_ref, wo_ref, g_ref, wu_ref, x2_ref, h2_ref, a_ref):
        oab = jnp.concatenate([oa_ref[...], ob_ref[...].astype(BF16)], axis=-1)
        x2 = x_ref[...] + _dot(oab, wo_ref[...])
        x2_ref[...] = x2
        _, n = _rms(x2)
        h2 = (n * g_ref[...]).astype(BF16)
        h2_ref[...] = h2
        a_ref[...] = _dot(h2, wu_ref[...])

    return pl.pallas_call(
        body, name="fwd_mid", grid=(T // tm,),
        in_specs=[_rows(tm, D_MODEL), _rows(tm, D_A), _rows(tm, D_B), _full((D_MODEL, D_MODEL), True), _full((1, D_MODEL)),
                  _full((D_MODEL, D_FF2), True)],
        out_specs=[_rows(tm, D_MODEL), _rows(tm, D_MODEL), _rows(tm, D_FF2)],
        out_shape=[jax.ShapeDtypeStruct((T, D_MODEL), F32), jax.ShapeDtypeStruct((T, D_MODEL), BF16),
                   jax.ShapeDtypeStruct((T, D_FF2), F32)],
        compiler_params=_cp("arbitrary"),
    )(x, oa, ob, w_out, g2, w_up)


def _row_before(x, prev, k):
    rolled = pltpu.roll(x, k, axis=0)
    row = lax.broadcasted_iota(jnp.int32, (8, x.shape[1]), 0)
    head = rolled[0:8]
    for r in range(k):
        head = jnp.where(row == r, prev[8 - k + r:9 - k + r], head)
    return jnp.concatenate([head, rolled[8:]], axis=0)


def _row_after(x, nxt, k):
    tm = x.shape[0]
    rolled = pltpu.roll(x, tm - k, axis=0)
    row = lax.broadcasted_iota(jnp.int32, (8, x.shape[1]), 0)
    tail = rolled[tm - 8:tm]
    for r in range(k):
        tail = jnp.where(row == 8 - k + r, nxt[r:r + 1], tail)
    return jnp.concatenate([rolled[:tm - 8], tail], axis=0)


def _fwd_ffn(a, x2, wc, bc, w_down, g3, tgt):
    T = x2.shape[0]
    tm = min(TM, T)

    def body(a_ref, x2_ref, wc_ref, bc_ref, wd_ref, g_ref, tgt_ref, ac_ref, yff_ref, dx3_ref, loss_ref, dg3_ref, carry):
        @pl.when(pl.program_id(0) == 0)
        def _():
            carry[...] = jnp.zeros_like(carry)
            loss_ref[...] = jnp.zeros_like(loss_ref)
            dg3_ref[...] = jnp.zeros_like(dg3_ref)

        def conv(cs):
            a0 = a_ref[:, cs]
            prev = carry[:, cs]
            ac = (wc_ref[0:1, cs] * _row_before(a0, prev, 2) + wc_ref[1:2, cs] * _row_before(a0, prev, 1)
                  + wc_ref[2:3, cs] * a0 + bc_ref[:, cs])
            ac_ref[:, cs] = ac.astype(BF16)
            return ac

        x3 = x2_ref[...]
        for ci in range(D_FF // CW):
            gs = slice(ci * CW, (ci + 1) * CW)
            ag = conv(gs)
            av = conv(slice(D_FF + ci * CW, D_FF + (ci + 1) * CW))
            yb = (ag * jax.nn.sigmoid(ag) * av).astype(BF16)
            yff_ref[:, gs] = yb
            x3 = x3 + _dot(yb, wd_ref[gs, :])
        carry[...] = a_ref[tm - 8:tm, :]
        r, n = _rms(x3)
        g = g_ref[...]
        diff = n * g - tgt_ref[...]
        loss_ref[...] += (0.5 / D_MODEL) * jnp.sum(diff * diff)
        dout = diff * (1.0 / D_MODEL)
        dg3_ref[...] += jnp.sum(dout * n, axis=0, keepdims=True)
        dx3_ref[...] = _rms_bwd(dout, n, r, g)

    return pl.pallas_call(
        body, name="fwd_ffn", grid=(T // tm,),
        in_specs=[_rows(tm, D_FF2), _rows(tm, D_MODEL), _full((3, D_FF2)), _full((1, D_FF2)), _full((D_FF, D_MODEL), True),
                  _full((1, D_MODEL)), _rows(tm, D_MODEL)],
        out_specs=[_rows(tm, D_FF2), _rows(tm, D_FF), _rows(tm, D_MODEL), _full((8, LANES)), _full((1, D_MODEL))],
        out_shape=[jax.ShapeDtypeStruct((T, D_FF2), BF16), jax.ShapeDtypeStruct((T, D_FF), BF16),
                   jax.ShapeDtypeStruct((T, D_MODEL), F32), jax.ShapeDtypeStruct((8, LANES), F32),
                   jax.ShapeDtypeStruct((1, D_MODEL), F32)],
        scratch_shapes=[pltpu.VMEM((8, D_FF2), F32)],
        compiler_params=_cp("arbitrary"),
    )(a, x2, wc, bc, w_down, g3, tgt)


def _bwd_ffn(dx3, a, ac, w_down, wc):
    T = dx3.shape[0]
    tm = min(TM, T)
    nt = T // tm

    def body(dx3_ref, a_ref, ac_ref, wd_ref, wc_ref, da_ref, dwc_ref, dbc_ref, nxt, shifted):
        @pl.when(pl.program_id(0) == 0)
        def _():
            nxt[...] = jnp.zeros_like(nxt)
            dwc_ref[...] = jnp.zeros_like(dwc_ref)
            dbc_ref[...] = jnp.zeros_like(dbc_ref)

        dxb = dx3_ref[...].astype(BF16)

        def back(cs, dac):
            a0 = a_ref[:, cs]
            shifted[0] = _row_after(dac, nxt[:, cs], 1)
            shifted[1] = _row_after(dac, nxt[:, cs], 2)
            dp1, dp2 = shifted[0], shifted[1]
            dbc_ref[:, cs] += jnp.sum(dac, axis=0, keepdims=True)
            dwc_ref[0:1, cs] += jnp.sum(dp2 * a0, axis=0, keepdims=True)
            dwc_ref[1:2, cs] += jnp.sum(dp1 * a0, axis=0, keepdims=True)
            dwc_ref[2:3, cs] += jnp.sum(dac * a0, axis=0, keepdims=True)
            da_ref[:, cs] = (wc_ref[2:3, cs] * dac + wc_ref[1:2, cs] * dp1 + wc_ref[0:1, cs] * dp2).astype(BF16)
            nxt[:, cs] = dac[0:8]

        for ci in range(D_FF // CW):
            gs = slice(ci * CW, (ci + 1) * CW)
            vs = slice(D_FF + ci * CW, D_FF + (ci + 1) * CW)
            dy = _dot_nt(dxb, wd_ref[gs, :])
            ag, av = ac_ref[:, gs].astype(F32), ac_ref[:, vs].astype(F32)
            sg = jax.nn.sigmoid(ag)
            back(vs, dy * (ag * sg))
            back(gs, dy * av * (sg * (1.0 + ag * (1.0 - sg))))

    return pl.pallas_call(
        body, name="bwd_ffn", grid=(nt,),
        in_specs=[_rows(tm, D_MODEL, nt), _rows(tm, D_FF2, nt), _rows(tm, D_FF2, nt), _full((D_FF, D_MODEL), True),
                  _full((3, D_FF2))],
        out_specs=[_rows(tm, D_FF2, nt), _full((3, D_FF2)), _full((1, D_FF2))],
        out_shape=[jax.ShapeDtypeStruct((T, D_FF2), BF16), jax.ShapeDtypeStruct((3, D_FF2), F32),
                   jax.ShapeDtypeStruct((1, D_FF2), F32)],
        scratch_shapes=[pltpu.VMEM((8, D_FF2), F32), pltpu.VMEM((2, tm, CW), F32)],
        compiler_params=_cp("arbitrary"),
    )(dx3, a, ac, w_down, wc)


def _bwd_mid(da, w_up, x2, g2, dx3, w_out, ob):
    T = x2.shape[0]
    tm = min(TM, T)

    def body(da_ref, wu_ref, x2_ref, g_ref, dx3_ref, wo_ref, ob_ref, hsum_ref, place_ref, dx2_ref, doa_ref, dob_ref, dg2_ref):
        @pl.when(pl.program_id(0) == 0)
        def _():
            dg2_ref[...] = jnp.zeros_like(dg2_ref)

        dh2 = _dot_nt(da_ref[...], wu_ref[...])
        r, n = _rms(x2_ref[...])
        dg2_ref[...] += jnp.sum(dh2 * n, axis=0, keepdims=True)
        dx2 = dx3_ref[...] + _rms_bwd(dh2, n, r, g_ref[...])
        dx2_ref[...] = dx2
        doab = _dot_nt(dx2.astype(BF16), wo_ref[...])
        doa_ref[...] = doab[:, :D_A]
        dob = doab[:, D_A:]
        rest, delta = dob * ob_ref[...], None
        for _ in range(3):
            piece = rest.astype(BF16)
            term = _dot(piece, hsum_ref[...])
            delta = term if delta is None else delta + term
            rest = rest - piece.astype(F32)
        hi, mid, lo = _split3(-delta)
        parts = jnp.concatenate([hi.astype(BF16), mid.astype(BF16), lo.astype(BF16)], axis=-1)
        placed = _dot(parts, place_ref[...])
        data = lax.broadcasted_iota(jnp.int32, (tm, LANES), 1) < HEAD_DIM
        for hd in range(N_HEADS):
            tile = dob[:, LANES * (hd // 2):LANES * (hd // 2 + 1)]
            rows = tile if hd % 2 == 0 else pltpu.roll(tile, HEAD_DIM, axis=1)
            dob_ref[hd] = jnp.where(data, rows, placed[:, LANES * hd:LANES * (hd + 1)]).astype(BF16)

    hsum = (jnp.arange(D_B)[:, None] // HEAD_DIM == jnp.arange(LANES)[None, :]).astype(BF16)
    j, h = jnp.arange(3 * LANES) // LANES, jnp.arange(3 * LANES) % LANES
    place = ((h[:, None] < N_HEADS)
             & (jnp.arange(N_HEADS * LANES)[None, :] == LANES * h[:, None] + L_ROW + j[:, None])).astype(BF16)
    return pl.pallas_call(
        body, name="bwd_mid", grid=(T // tm,),
        in_specs=[_rows(tm, D_FF2), _full((D_MODEL, D_FF2), True), _rows(tm, D_MODEL), _full((1, D_MODEL)), _rows(tm, D_MODEL),
                  _full((D_MODEL, D_MODEL), True), _rows(tm, D_B), _full((D_B, LANES)), _full((3 * LANES, N_HEADS * LANES))],
        out_specs=[_rows(tm, D_MODEL), _rows(tm, D_A), pl.BlockSpec((N_HEADS, tm, LANES), lambda i: (0, i, 0)),
                   _full((1, D_MODEL))],
        out_shape=[jax.ShapeDtypeStruct((T, D_MODEL), F32), jax.ShapeDtypeStruct((T, D_A), F32),
                   jax.ShapeDtypeStruct((N_HEADS, T, LANES), BF16), jax.ShapeDtypeStruct((1, D_MODEL), F32)],
        compiler_params=_cp("arbitrary"),
    )(da, w_up, x2, g2, dx3, w_out, ob, hsum, place)


def _bwd_in(dzuv, dqa, dka, dva, fl, x, dx2, w_in_p, g1):
    T = x.shape[0]
    tm = min(TM, T)
    nt = T // tm

    def body(dzuv_ref, dqa_ref, dka_ref, dva_ref, fl_ref, x_ref, dx2_ref, w_ref, g_ref,
             gx_ref, dz_ref, dg1_ref, dfb_ref, carry):
        @pl.when(pl.program_id(0) == 0)
        def _():
            carry[...] = jnp.zeros_like(carry)
            dg1_ref[...] = jnp.zeros_like(dg1_ref)
            dfb_ref[...] = jnp.zeros_like(dfb_ref)

        dc = _lanes(tm, LANES, {hd: dqa_ref[hd][:, L_ROW:L_ROW + 1] - dka_ref[hd][:, L_COL:L_COL + 1] for hd in range(N_HEADS)})
        later = (lax.broadcasted_iota(jnp.int32, (tm, tm), 1) >= lax.broadcasted_iota(jnp.int32, (tm, tm), 0)).astype(F32)
        dls = _dot_f32(later, dc) + carry[...]
        carry[...] = dls[0:1, :]
        dzf = dls * jax.nn.sigmoid(-fl_ref[...])
        dfb_ref[...] += jnp.sum(dzf, axis=0, keepdims=True)
        data = lax.broadcasted_iota(jnp.int32, (tm, LANES), 1) < HEAD_DIM

        def compact(ref):
            return [jnp.where(data, ref[2 * p], pltpu.roll(ref[2 * p + 1], HEAD_DIM, axis=1)) for p in range(N_HEADS // 2)]

        dz = jnp.concatenate([dzuv_ref[...], *[(t * (HEAD_DIM ** -0.5)).astype(BF16) for t in compact(dqa_ref)],
                              *[t.astype(BF16) for t in compact(dka_ref)], *compact(dva_ref), dzf.astype(BF16)], axis=-1)
        dz_ref[...] = dz
        dh1 = _dot_nt(dz, w_ref[...])
        r, n = _rms(x_ref[...])
        dg1_ref[...] += jnp.sum(dh1 * n, axis=0, keepdims=True)
        gx_ref[...] = dx2_ref[...] + _rms_bwd(dh1, n, r, g_ref[...])

    rv = functools.partial(_rows, tm, rev_nt=nt)
    heads = pl.BlockSpec((N_HEADS, tm, LANES), lambda i: (0, nt - 1 - i, 0))
    return pl.pallas_call(
        body, name="bwd_in", grid=(nt,),
        in_specs=[rv(2 * D_A), heads, heads, heads, rv(LANES), rv(D_MODEL), rv(D_MODEL),
                  _full((D_MODEL, D_IN_PAD), True), _full((1, D_MODEL))],
        out_specs=[rv(D_MODEL), rv(D_IN_PAD), _full((1, D_MODEL)), _full((1, LANES))],
        out_shape=[jax.ShapeDtypeStruct((T, D_MODEL), F32), jax.ShapeDtypeStruct((T, D_IN_PAD), BF16),
                   jax.ShapeDtypeStruct((1, D_MODEL), F32), jax.ShapeDtypeStruct((1, LANES), F32)],
        scratch_shapes=[pltpu.VMEM((1, LANES), F32)],
        compiler_params=_cp("arbitrary"),
    )(dzuv, dqa, dka, dva, fl, x, dx2, w_in_p, g1)


def _matmul_tn(a_parts, b, tmm, tn, tk, name, shard_cols=None, n_valid=None):
    T = b.shape[0]
    widths = [a.shape[1] for a in a_parts]
    M, N = sum(widths), b.shape[1]
    tk = min(tk, T)
    nk = T // tk
    part_w = tmm // len(a_parts)
    n_valid = N if n_valid is None else n_valid

    def body(*refs):
        a_refs, b_ref, o_ref, obf_ref = refs[:len(a_parts)], refs[-3], refs[-2], refs[-1]
        k = pl.program_id(2)

        @pl.when(k == 0)
        def _():
            o_ref[...] = jnp.zeros_like(o_ref)

        a = [r[...].astype(BF16) for r in a_refs]
        o_ref[...] += _dot_tn(a[0] if len(a) == 1 else jnp.concatenate(a, axis=-1), b_ref[...].astype(BF16))

        @pl.when(k == nk - 1)
        def _():
            if shard_cols is None:
                obf_ref[...] = o_ref[...].astype(BF16)
            else:
                for d in range(min(tn, n_valid) // shard_cols):
                    obf_ref[d] = o_ref[:, d * shard_cols:(d + 1) * shard_cols].astype(BF16)

    if shard_cols is None:
        bf_spec, bf_shape = pl.BlockSpec((tmm, tn), lambda i, j, k: (i, j)), (M, N)
    else:
        per_tile = min(tn, n_valid) // shard_cols
        bf_spec, bf_shape = pl.BlockSpec((per_tile, tmm, shard_cols), lambda i, j, k: (j, i, 0)), (N_DEV, M, shard_cols)
    a_specs = [pl.BlockSpec((tk, part_w), lambda i, j, k: (k, i)) for _ in a_parts]
    return pl.pallas_call(
        body, name=name, grid=(M // tmm, N // tn, nk),
        in_specs=[*a_specs, pl.BlockSpec((tk, tn), lambda i, j, k: (k, j))],
        out_specs=[pl.BlockSpec((tmm, tn), lambda i, j, k: (i, j)), bf_spec],
        out_shape=[jax.ShapeDtypeStruct((M, N), F32), jax.ShapeDtypeStruct(bf_shape, BF16)],
        compiler_params=_cp("arbitrary", "arbitrary", "arbitrary"),
    )(*a_parts, b)


class _Exchange:
    def __init__(self, gather, scatter, relay):
        self.n_g, self.n, self.relay = len(gather), len(gather) + len(scatter), relay
        self.operands = [*gather, *scatter]
        self.out_shape = [jax.ShapeDtypeStruct((N_DEV, *g.shape), g.dtype) for g in gather]
        self.out_shape += [jax.ShapeDtypeStruct(s.shape, s.dtype) for s in scatter]
        self.specs = [ANY] * self.n
        n = self.n
        self.scratch = [pltpu.SemaphoreType.DMA((7 * n,)), pltpu.SemaphoreType.DMA((7 * n,)),
                        pltpu.SemaphoreType.DMA((n,))] if n else []

    def _plan(self, ins, outs, sems):
        send_sems, recv_sems, local_sems = sems
        x, y, c = (lax.axis_index(ax) for ax in MESH_AXES)
        me = 4 * x + 2 * y + c
        sibling = (x, y, 1 - c)
        chips = [(1 - x, y), (x, 1 - y), (1 - x, 1 - y)]
        peers = [sibling] + [(*chip, c) for chip in chips] + [(*chip, 1 - c) for chip in chips]

        def index(dev):
            return 4 * dev[0] + 2 * dev[1] + dev[2]

        def remote(k, src, dst, to):
            return pltpu.make_async_remote_copy(src_ref=src, dst_ref=dst, send_sem=send_sems.at[k], recv_sem=recv_sems.at[k],
                                                device_id=to, device_id_type=pl.DeviceIdType.MESH)

        local, sends, relays, recvs = [], [], [], []
        for a in range(self.n):
            src, out, base = ins[a], outs[a], 7 * a
            if a >= self.n_g:
                local.append(pltpu.make_async_copy(src.at[me], out.at[me], local_sems.at[a]))
                sends += [remote(base + k, src.at[index(peer)], out.at[me], peer) for k, peer in enumerate(peers)]
            else:
                local.append(pltpu.make_async_copy(src, out.at[me], local_sems.at[a]))
                sends += [remote(base + k, src, out.at[me], peer) for k, peer in enumerate(peers[:4 if self.relay else 7])]
            for k, peer in enumerate(peers):
                slot = out.at[index(peer)]
                if a < self.n_g and self.relay and k >= 4:
                    continue
                recv = remote(base + k, slot, slot, peer)
                if a < self.n_g and self.relay and k >= 1:
                    relays.append((recv, remote(base + 3 + k, slot, slot, sibling)))
                else:
                    recvs.append(recv)
            if a < self.n_g and self.relay:
                for j, chip in enumerate(chips):
                    slot = out.at[index((*chip, 1 - c))]
                    recvs.append(remote(base + 4 + j, slot, slot, sibling))
        return local, sends, relays, recvs

    def start(self, ins, outs, sems):
        local, sends, _, _ = self._plan(ins, outs, sems)
        for cp in local + sends:
            cp.start()

    def finish(self, ins, outs, sems):
        local, sends, relays, recvs = self._plan(ins, outs, sems)
        for recv, fwd in relays:
            recv.wait_recv()
            fwd.start()
        for recv in recvs:
            recv.wait_recv()
        for cp in sends + [fwd for _, fwd in relays]:
            cp.wait_send()
        for cp in local:
            cp.wait()


def _exchange(gather, scatter, name):
    ex = _Exchange(gather, scatter, relay=True)
    n = ex.n

    def body(*refs):
        ins, outs, sems = refs[:n], refs[n:2 * n], refs[2 * n:]
        ex.start(ins, outs, sems)
        ex.finish(ins, outs, sems)

    return pl.pallas_call(body, name=name, in_specs=ex.specs, out_specs=ex.specs, out_shape=ex.out_shape,
                          scratch_shapes=ex.scratch)(*ex.operands)


def _adamw(w, g, m, v):
    m = ADAM_B1 * m + (1.0 - ADAM_B1) * g
    v = ADAM_B2 * v + (1.0 - ADAM_B2) * jnp.square(g)
    m_hat = m / (1.0 - ADAM_B1 ** ADAM_STEP)
    v_hat = v / (1.0 - ADAM_B2 ** ADAM_STEP)
    delta = -ADAM_LR * (m_hat / (jnp.sqrt(v_hat) + ADAM_EPS) + ADAM_WD * w)
    return delta, m, v


def _adamw_shard(w, m, v, own, recv, tr, name):
    R, C = w.shape

    def body(w_ref, m_ref, v_ref, own_ref, recv_ref, g_ref, d_ref, nm_ref, nv_ref):
        x, y, c = (lax.axis_index(ax) for ax in MESH_AXES)
        me = 4 * x + 2 * y + c
        g = own_ref[...]
        for d in range(N_DEV):
            g = g + jnp.where(d == me, 0.0, recv_ref[d].astype(F32))
        g_ref[...] = g
        d_ref[...], nm_ref[...], nv_ref[...] = _adamw(w_ref[...], g, m_ref[...], v_ref[...])

    blk = pl.BlockSpec((tr, C), lambda i: (i, 0))
    return pl.pallas_call(
        body, name=name, grid=(R // tr,),
        in_specs=[blk, blk, blk, blk, pl.BlockSpec((N_DEV, tr, C), lambda i: (0, i, 0))],
        out_specs=[blk] * 4, out_shape=[jax.ShapeDtypeStruct((R, C), F32)] * 4,
        compiler_params=_cp("arbitrary"),
    )(w, m, v, own, recv)


def _adamw_small(w, m, v, gall):
    R = w.shape[0]
    tr = R // 3

    def body(w_ref, m_ref, v_ref, gall_ref, g_ref, d_ref, nm_ref, nv_ref):
        g = gall_ref[0]
        for d in range(1, N_DEV):
            g = g + gall_ref[d]
        g_ref[...] = g
        d_ref[...], nm_ref[...], nv_ref[...] = _adamw(w_ref[...], g, m_ref[...], v_ref[...])

    blk = pl.BlockSpec((tr, LANES), lambda i: (i, 0))
    return pl.pallas_call(
        body, name="adamw_small", grid=(R // tr,),
        in_specs=[blk, blk, blk, pl.BlockSpec((N_DEV, tr, LANES), lambda i: (0, i, 0))],
        out_specs=[blk] * 4, out_shape=[jax.ShapeDtypeStruct((R, LANES), F32)] * 4,
        compiler_params=_cp("arbitrary"),
    )(w, m, v, gall)


_SMALL_LATE = (("g1", 8), ("fb", 8))
_SMALL_EARLY = (("lg", 8), ("sg_w", 1024), ("sg_b", 8), ("g2", 8), ("bc", 48), ("g3", 8), ("loss", 8))
_SMALL = _SMALL_LATE + _SMALL_EARLY


def _pack_small(parts, layout):
    rows = []
    for name, nrow in layout:
        flat = parts[name].astype(F32).reshape(-1)
        flat = jnp.pad(flat, (0, nrow * LANES - flat.shape[0]))
        rows.append(flat.reshape(nrow, LANES))
    return jnp.concatenate(rows, axis=0)


def _unpack_small(packed, shapes):
    out, r = {}, 0
    for name, nrow in _SMALL:
        size = math.prod(shapes[name])
        out[name] = packed[r:r + nrow].reshape(-1)[:size].reshape(shapes[name])
        r += nrow
    return out


def _col_shards(g):
    return jnp.transpose(g.reshape(g.shape[0], N_DEV, -1), (1, 0, 2))


def _row_shards(g):
    return g.reshape(N_DEV, -1, g.shape[1])


def _cols_whole(g):
    return jnp.transpose(g, (1, 0, 2)).reshape(g.shape[1], -1)


def kernel(x, norm_mix_g, w_in, f_bias, sg_ln_g, sg_w, sg_b, w_out, norm_ffn_g, w_up, w_conv, b_conv, w_down, norm_final_g, loss_target, m_norm_mix_g, m_w_in, m_f_bias, m_sg_ln_g, m_sg_w, m_sg_b, m_w_out, m_norm_ffn_g, m_w_up, m_w_conv, m_b_conv, m_w_down, m_norm_final_g, v_norm_mix_g, v_w_in, v_f_bias, v_sg_ln_g, v_sg_w, v_sg_b, v_w_out, v_norm_ffn_g, v_w_up, v_w_conv, v_b_conv, v_w_down, v_norm_final_g):
    me = 4 * lax.axis_index("x") + 2 * lax.axis_index("y") + lax.axis_index("c")
    xs, tgt = x[0], loss_target[0]
    g1, g2, g3 = norm_mix_g, norm_ffn_g, norm_final_g.reshape(1, D_MODEL)
    lg = sg_ln_g.reshape(1, D_A)
    fb = jnp.pad(f_bias, ((0, 0), (0, LANES - N_HEADS)))
    pos_chunk = jnp.arange(SG_BLOCK) // SG_CHUNK
    maskf = (pos_chunk[:, None] >= pos_chunk[None, :]).astype(F32)
    wm = (sg_w[0] * maskf[None]).astype(BF16)
    wmt = jnp.swapaxes(wm, 1, 2)
    bfull = jnp.repeat(sg_b[0].T, HEAD_DIM, axis=1)
    hsel = jnp.repeat(jnp.eye(N_HEADS, dtype=F32), HEAD_DIM, axis=1)
    avg = (jnp.repeat(hsel, HEAD_DIM, axis=0) * (1.0 / HEAD_DIM)).astype(BF16)

    (win_g,) = _exchange([w_in[0].astype(BF16)], [], "gather_w_in")
    w_in_p = jnp.pad(_cols_whole(win_g), ((0, 0), (0, D_IN_PAD - D_IN)))
    zuv, qa, ka, va, fl, h1 = _fwd_in(xs, g1, w_in_p, fb)
    oa = _gmlp_fwd(zuv, lg, wm, bfull, avg)
    rest = _Exchange([w_out[0].astype(BF16), w_up[0].astype(BF16), w_down[0].astype(BF16), w_conv[0]], [], relay=False)
    ob, lse, (wout_g, wup_g, wdown_g, wc_g) = _fox_fwd(qa, ka, va, rest)
    w_out_f, w_up_f = wout_g.reshape(D_MODEL, D_MODEL), _cols_whole(wup_g)
    w_down_f, wc_f = wdown_g.reshape(D_FF, D_MODEL), _cols_whole(wc_g)

    x2, h2, a = _fwd_mid(xs, oa, ob, w_out_f, g2, w_up_f)
    ac, yff, dx3, loss, dg3 = _fwd_ffn(a, x2, wc_f, b_conv, w_down_f, g3, tgt)
    da, dwc, dbc = _bwd_ffn(dx3, a, ac, w_down_f, wc_f)
    dx2, doa, dob, dg2 = _bwd_mid(da, w_up_f, x2, g2, dx3, w_out_f, ob)
    dzuv, dwm, dsgb, dlg = _gmlp_bwd(zuv, doa, lg, wm, wmt, bfull, maskf, hsel, avg)
    dwout, dwout_bf = _matmul_tn([oa, ob], dx2, D_MODEL, D_MODEL, 1024, "dw_out")
    dwup, dwup_bf = _matmul_tn([h2], da, D_MODEL, D_FF2 // 4, 2048, "dw_up", shard_cols=D_FF2 // N_DEV)
    dwdown, dwdown_bf = _matmul_tn([yff], dx3, D_FF // 2, D_MODEL, 1024, "dw_down")
    full = dict(w_out=(dwout, 0), w_up=(dwup, 1), wc=(dwc, 1), w_down=(dwdown, 0))

    early = ("w_out", "w_up", "wc", "w_down")
    wire = [_row_shards(dwout_bf), dwup_bf, _col_shards(dwc).astype(BF16), _row_shards(dwdown_bf)]
    small = dict(lg=dlg, sg_w=dwm, sg_b=dsgb, g2=dg2, bc=dbc, g3=dg3, loss=loss[0:1, 0:1])
    grads = _Exchange([_pack_small(small, _SMALL_EARLY)], wire, relay=False)
    dqa, dka, dva, got = _fox_bwd(qa, lse, dob, ka, va, grads)
    small_early, recv = got[0], dict(zip(early, got[1:]))

    gx, dz, dg1, dfb = _bwd_in(dzuv, dqa, dka, dva, fl, xs, dx2, w_in_p, g1)
    dwin, dwin_bf = _matmul_tn([h1], dz, D_MODEL // 2, D_IN_PAD, 1024, "dw_in", shard_cols=D_IN // N_DEV, n_valid=D_IN)
    full["w_in"] = (dwin, 1)
    small_late, recv["w_in"] = _exchange([_pack_small(dict(g1=dg1, fb=dfb), _SMALL_LATE)], [dwin_bf], "exchange_w_in")

    weights = dict(w_in=(w_in, m_w_in, v_w_in, 256), w_out=(w_out, m_w_out, v_w_out, 128), w_up=(w_up, m_w_up, v_w_up, 256),
                   wc=(w_conv, m_w_conv, v_w_conv, 3), w_down=(w_down, m_w_down, v_w_down, 176))
    res = {}
    for n, (w, m, v, tr) in weights.items():
        g, axis = full[n]
        size = w.shape[1 + axis]
        own = lax.dynamic_slice_in_dim(g, me * size, size, axis=axis)
        res[n] = [r[None] for r in _adamw_shard(w[0], m[0], v[0], own, recv[n], tr, "adamw_" + n)]

    zero = jnp.zeros((1, 1), F32)
    reps = dict(g1=(norm_mix_g, m_norm_mix_g, v_norm_mix_g), fb=(f_bias, m_f_bias, v_f_bias), lg=(sg_ln_g, m_sg_ln_g, v_sg_ln_g),
                sg_w=(sg_w, m_sg_w, v_sg_w), sg_b=(sg_b, m_sg_b, v_sg_b), g2=(norm_ffn_g, m_norm_ffn_g, v_norm_ffn_g),
                bc=(b_conv, m_b_conv, v_b_conv), g3=(norm_final_g, m_norm_final_g, v_norm_final_g), loss=(zero, zero, zero))
    shapes = {n: t[0].shape for n, t in reps.items()}
    packed = _adamw_small(*[_pack_small({n: t[i] for n, t in reps.items()}, _SMALL) for i in range(3)],
                          jnp.concatenate([small_late, small_early], axis=1))
    unpacked = [_unpack_small(p, shapes) for p in packed]
    for n in reps:
        res[n] = [u[n] for u in unpacked]

    names = ("g1", "w_in", "fb", "lg", "sg_w", "sg_b", "w_out", "g2", "w_up", "wc", "bc", "w_down", "g3")
    return (res["loss"][0][0, 0], gx[None], *[res[n][0] for n in names], *[res[n][1] for n in names],
            *[res[n][2] for n in names], *[res[n][3] for n in names])
```

```python
import functools
import math

import jax
import jax.numpy as jnp
from jax import lax
from jax.experimental import pallas as pl
from jax.experimental.pallas import tpu as pltpu

F32 = jnp.float32
BF16 = jnp.bfloat16

D_MODEL = 1024
HEAD_DIM = 64
N_HEADS = 8
D_A = 512
D_B = 512
D_IN = 2 * D_A + 3 * D_B + N_HEADS
D_IN_PAD = 2688
D_FF = 2816
D_FF2 = 2 * D_FF
SG_BLOCK = 128
SG_CHUNK = 64
EPS = 1e-6
N_DEV = 8
LANES = 128
NEG = -1e30
VMEM_LIMIT = 56 * 1024 * 1024

ADAM_LR = 0.001
ADAM_B1 = 0.9
ADAM_B2 = 0.999
ADAM_EPS = 1e-08
ADAM_WD = 0.01
ADAM_STEP = 10

TM = 256
TQ = 512
FWD_UNROLL = 4
BWD_UNROLL = 2
CW = 256

MESH_AXES = ("x", "y", "c")
ANY = pl.BlockSpec(memory_space=pl.ANY)


def _cp(*sem):
    return pltpu.CompilerParams(dimension_semantics=sem, vmem_limit_bytes=VMEM_LIMIT)


def _dot(a, b):
    return jnp.dot(a, b, preferred_element_type=F32)


def _dot_nt(a, b):
    return lax.dot_general(a, b, (((1,), (1,)), ((), ())), preferred_element_type=F32)


def _dot_tn(a, b):
    return lax.dot_general(a, b, (((0,), (0,)), ((), ())), preferred_element_type=F32)


def _dot_f32(a, b):
    return jnp.dot(a, b, precision=lax.Precision.HIGHEST, preferred_element_type=F32)


def _gelu(z):
    return 0.5 * z * (1.0 + lax.erf(z * (1.0 / math.sqrt(2.0))))


def _gelu_grad(z):
    return 0.5 * (1.0 + lax.erf(z * (1.0 / math.sqrt(2.0)))) + z * jnp.exp(-0.5 * z * z) * (1.0 / math.sqrt(2.0 * math.pi))


def _log_sigmoid(x):
    return jnp.minimum(x, 0.0) - jnp.log1p(jnp.exp(-jnp.abs(x)))


def _rms(x):
    r = lax.rsqrt(jnp.mean(x * x, axis=-1, keepdims=True) + EPS)
    return r, x * r


def _rms_bwd(dy, n, r, g):
    dn = dy * g
    return r * (dn - n * jnp.mean(dn * n, axis=-1, keepdims=True))


def _full(shape, single=False):
    nd = len(shape)
    if single:
        return pl.BlockSpec(shape, lambda *_: (0,) * nd, pipeline_mode=pl.Buffered(1))
    return pl.BlockSpec(shape, lambda *_: (0,) * nd)


def _rows(tm, cols, rev_nt=None):
    if rev_nt is None:
        return pl.BlockSpec((tm, cols), lambda i: (i, 0))
    return pl.BlockSpec((tm, cols), lambda i: (rev_nt - 1 - i, 0))


def _head_sl(h):
    return slice(HEAD_DIM * h, HEAD_DIM * (h + 1))


L_ROW = HEAD_DIM
L_COL = HEAD_DIM + 3
L_LSE = HEAD_DIM + 6


def _split3(x):
    hi = x.astype(BF16).astype(F32)
    mid = (x - hi).astype(BF16).astype(F32)
    lo = (x - hi - mid).astype(BF16).astype(F32)
    return hi, mid, lo


def _lanes(rows, width, parts):
    lane = lax.broadcasted_iota(jnp.int32, (rows, width), 1)
    out = jnp.zeros((rows, width), F32)
    for at, val in parts.items():
        out = jnp.where(lane == at, val, out)
    return out


def _fwd_in(x, g1, w_in_p, fb):
    T = x.shape[0]
    tm = min(TM, T)

    def body(x_ref, g_ref, w_ref, fb_ref, place_ref, zuv_ref, qa_ref, ka_ref, va_ref, fl_ref, h1_ref, carry):
        @pl.when(pl.program_id(0) == 0)
        def _():
            carry[...] = jnp.zeros_like(carry)

        r, n = _rms(x_ref[...])
        h = (n * g_ref[...]).astype(BF16)
        h1_ref[...] = h
        z = _dot(h, w_ref[...])
        zuv_ref[...] = z[:, :2 * D_A]
        o = 2 * D_A
        fl = z[:, o + 3 * D_B:] + fb_ref[...]
        fl_ref[...] = fl
        tri = (lax.broadcasted_iota(jnp.int32, (tm, tm), 0) >= lax.broadcasted_iota(jnp.int32, (tm, tm), 1)).astype(F32)
        c = _dot_f32(tri, _log_sigmoid(fl)) + carry[...]
        carry[...] = c[tm - 1:tm, :]
        hi, mid, lo = _split3(c)
        parts = jnp.concatenate([hi.astype(BF16), mid.astype(BF16), lo.astype(BF16)], axis=-1)
        placed = _dot(parts, place_ref[...])
        lane = lax.broadcasted_iota(jnp.int32, (tm, LANES), 1)
        data = lane < HEAD_DIM
        ones_q = ((lane >= L_COL) & (lane < L_COL + 3)).astype(F32)
        ones_k = (((lane >= L_ROW) & (lane < L_ROW + 3)) | ((lane >= L_LSE) & (lane < L_LSE + 3))).astype(F32)
        ones_v = ((lane >= L_ROW) & (lane < L_ROW + 3)).astype(F32)
        for hd in range(N_HEADS):
            def rows_of(first_col):
                tile = z[:, first_col + LANES * (hd // 2):first_col + LANES * (hd // 2 + 1)]
                return tile if hd % 2 == 0 else pltpu.roll(tile, HEAD_DIM, axis=1)

            hs = slice(LANES * hd, LANES * (hd + 1))
            qa_ref[hd] = jnp.where(data, rows_of(o) * (HEAD_DIM ** -0.5), placed[:, hs] + ones_q).astype(BF16)
            ka_ref[hd] = jnp.where(data, rows_of(o + D_B), ones_k - placed[:, N_HEADS * LANES:][:, hs]).astype(BF16)
            va_ref[hd] = jnp.where(data, rows_of(o + 2 * D_B), ones_v).astype(BF16)

    heads = pl.BlockSpec((N_HEADS, tm, LANES), lambda i: (0, i, 0))
    aug = jax.ShapeDtypeStruct((N_HEADS, T, LANES), BF16)
    j, h = jnp.arange(3 * LANES) // LANES, jnp.arange(3 * LANES) % LANES
    cols = jnp.arange(2 * N_HEADS * LANES)
    place = ((h[:, None] < N_HEADS) & ((cols[None, :] == LANES * h[:, None] + L_ROW + j[:, None])
                                       | (cols[None, :] == N_HEADS * LANES + LANES * h[:, None] + L_COL + j[:, None]))).astype(BF16)
    return pl.pallas_call(
        body, name="fwd_in", grid=(T // tm,),
        in_specs=[_rows(tm, D_MODEL), _full((1, D_MODEL)), _full((D_MODEL, D_IN_PAD), True), _full((1, LANES)),
                  _full((3 * LANES, 2 * N_HEADS * LANES))],
        out_specs=[_rows(tm, 2 * D_A), heads, heads, heads, _rows(tm, LANES), _rows(tm, D_MODEL)],
        out_shape=[jax.ShapeDtypeStruct((T, 2 * D_A), F32), aug, aug, aug, jax.ShapeDtypeStruct((T, LANES), F32),
                   jax.ShapeDtypeStruct((T, D_MODEL), BF16)],
        scratch_shapes=[pltpu.VMEM((1, LANES), F32)],
        compiler_params=_cp("arbitrary"),
    )(x, g1, w_in_p, fb, place)


def _segmean(x, avg_ref, parts):
    out, rest = None, x
    for _ in range(parts):
        piece = rest.astype(BF16)
        term = _dot(piece, avg_ref[...])
        out = term if out is None else out + term
        rest = rest - piece.astype(F32)
    return out


def _layer_norm(v, lg, avg_ref):
    d = v - _segmean(v, avg_ref, 3)
    rstd = lax.rsqrt(_segmean(d * d, avg_ref, 2) + EPS)
    vhat = d * rstd
    return vhat, rstd, (vhat * lg).astype(BF16)


def _mix_heads(w_ref, x, row_slice):
    low = lax.broadcasted_iota(jnp.int32, (SG_BLOCK, LANES), 1) < HEAD_DIM
    tiles = []
    for p in range(N_HEADS // 2):
        xt = x[row_slice, LANES * p:LANES * (p + 1)]
        zero = jnp.zeros_like(xt)
        tiles.append(_dot(w_ref[2 * p], jnp.where(low, xt, zero)) + _dot(w_ref[2 * p + 1], jnp.where(low, zero, xt)))
    return jnp.concatenate(tiles, axis=-1)


def _gmlp_fwd(zuv, lg, wm, bfull, avg):
    T = zuv.shape[0]
    tg = min(TM, T)
    nb = tg // SG_BLOCK

    def body(zuv_ref, lg_ref, wm_ref, b_ref, avg_ref, oa_ref):
        u = _gelu(zuv_ref[:, :D_A])
        _, _, vn = _layer_norm(_gelu(zuv_ref[:, D_A:]), lg_ref[...], avg_ref)
        for n in range(nb):
            rs = slice(n * SG_BLOCK, (n + 1) * SG_BLOCK)
            oa_ref[rs, :] = (u[rs] * (_mix_heads(wm_ref, vn, rs) + b_ref[...])).astype(BF16)

    return pl.pallas_call(
        body, name="gmlp_fwd", grid=(T // tg,),
        in_specs=[_rows(tg, 2 * D_A), _full((1, D_A)), _full((N_HEADS, SG_BLOCK, SG_BLOCK)), _full((SG_BLOCK, D_A)),
                  _full((D_A, D_A))],
        out_specs=_rows(tg, D_A),
        out_shape=jax.ShapeDtypeStruct((T, D_A), BF16),
        compiler_params=_cp("arbitrary"),
    )(zuv, lg, wm, bfull, avg)


def _gmlp_bwd(zuv, doa, lg, wm, wmt, bfull, maskf, hsel, avg):
    T = zuv.shape[0]
    tg = min(TM, T)
    nb = tg // SG_BLOCK
    nt = T // tg

    def body(zuv_ref, doa_ref, lg_ref, wm_ref, wmt_ref, b_ref, mask_ref, hsel_ref, avg_ref,
             dzuv_ref, dwm_ref, dsgb_ref, dlg_ref, dbacc):
        i = pl.program_id(0)

        @pl.when(i == 0)
        def _():
            dwm_ref[...] = jnp.zeros_like(dwm_ref)
            dlg_ref[...] = jnp.zeros_like(dlg_ref)
            dbacc[...] = jnp.zeros_like(dbacc)

        zu, zv = zuv_ref[:, :D_A], zuv_ref[:, D_A:]
        lgv = lg_ref[...]
        vhat, rstd, vn = _layer_norm(_gelu(zv), lgv, avg_ref)
        dmixed = doa_ref[...] * _gelu(zu)
        dmb = dmixed.astype(BF16)
        low = lax.broadcasted_iota(jnp.int32, (SG_BLOCK, LANES), 1) < HEAD_DIM
        mixed, dvn = [], []
        for n in range(nb):
            rs = slice(n * SG_BLOCK, (n + 1) * SG_BLOCK)
            mixed.append(_mix_heads(wm_ref, vn, rs) + b_ref[...])
            dvn.append(_mix_heads(wmt_ref, dmb, rs))
            dbacc[...] += dmixed[rs]
            for p in range(N_HEADS // 2):
                ls = slice(LANES * p, LANES * (p + 1))
                dmt, vnt = dmb[rs, ls], vn[rs, ls]
                zero = jnp.zeros_like(dmt)
                dwm_ref[2 * p] += _dot_nt(jnp.where(low, dmt, zero), vnt) * mask_ref[...]
                dwm_ref[2 * p + 1] += _dot_nt(jnp.where(low, zero, dmt), vnt) * mask_ref[...]
        mixed = jnp.concatenate(mixed, axis=0) if nb > 1 else mixed[0]
        dvn = jnp.concatenate(dvn, axis=0) if nb > 1 else dvn[0]
        dlg_ref[...] += jnp.sum(dvn * vhat, axis=0, keepdims=True)
        dvhat = dvn * lgv
        dv = rstd * (dvhat - _segmean(dvhat, avg_ref, 3) - vhat * _segmean(dvhat * vhat, avg_ref, 3))
        dzuv_ref[:, :D_A] = (doa_ref[...] * mixed * _gelu_grad(zu)).astype(BF16)
        dzuv_ref[:, D_A:] = (dv * _gelu_grad(zv)).astype(BF16)

        @pl.when(i == nt - 1)
        def _():
            dsgb_ref[...] = lax.dot_general(hsel_ref[...], dbacc[...], (((1,), (1,)), ((), ())),
                                            precision=lax.Precision.HIGHEST, preferred_element_type=F32)

    return pl.pallas_call(
        body, name="gmlp_bwd", grid=(nt,),
        in_specs=[_rows(tg, 2 * D_A), _rows(tg, D_A), _full((1, D_A)), _full((N_HEADS, SG_BLOCK, SG_BLOCK)),
                  _full((N_HEADS, SG_BLOCK, SG_BLOCK)), _full((SG_BLOCK, D_A)), _full((SG_BLOCK, SG_BLOCK)),
                  _full((N_HEADS, D_A)), _full((D_A, D_A))],
        out_specs=[_rows(tg, 2 * D_A), _full((N_HEADS, SG_BLOCK, SG_BLOCK)), _full((N_HEADS, SG_BLOCK)), _full((1, D_A))],
        out_shape=[jax.ShapeDtypeStruct((T, 2 * D_A), BF16), jax.ShapeDtypeStruct((N_HEADS, SG_BLOCK, SG_BLOCK), F32),
                   jax.ShapeDtypeStruct((N_HEADS, SG_BLOCK), F32), jax.ShapeDtypeStruct((1, D_A), F32)],
        scratch_shapes=[pltpu.VMEM((SG_BLOCK, D_A), F32)],
        compiler_params=_cp("arbitrary"),
    )(zuv, doa, lg, wm, wmt, bfull, maskf, hsel, avg)


def _fox_fwd(qa, ka, va, side):
    T = qa.shape[1]
    tq = min(TQ, T)
    nq = T // tq
    n = side.n

    def body(qa_ref, ka_ref, va_ref, *rest):
        ins, (o_ref, lse_ref), outs, sems = rest[:n], rest[n:n + 2], rest[n + 2:2 * n + 2], rest[2 * n + 2:]
        i = pl.program_id(1)
        if n:
            @pl.when((pl.program_id(0) == 0) & (i == 0))
            def _():
                side.start(ins, outs, sems)
        row = lax.broadcasted_iota(jnp.int32, (tq, tq), 0)
        col = lax.broadcasted_iota(jnp.int32, (tq, tq), 1)
        qs = [qa_ref[0], qa_ref[1]]

        def tiles(js, carry, diag):
            offs = [pl.multiple_of(j * tq, tq) for j in js]
            logits = [[_dot_nt(qs[hh], ka_ref[hh, pl.ds(off, tq), :]) for hh in range(2)] for off in offs]
            carry = list(carry)
            for off, per_head in zip(offs, logits):
                for hh, s in enumerate(per_head):
                    if diag:
                        s = jnp.where(col <= row, s, NEG)
                    m, acc = carry[hh]
                    m_new = jnp.maximum(m, jnp.max(s, axis=-1, keepdims=True))
                    pr = jnp.exp(s - m_new)
                    acc = jnp.exp(m - m_new) * acc + _dot(pr.astype(BF16), va_ref[hh, pl.ds(off, tq), :])
                    carry[hh] = (m_new, acc)
            return tuple(carry)

        init = ((jnp.full((tq, 1), NEG, F32), jnp.zeros((tq, LANES), F32)),) * 2
        carry = lax.fori_loop(0, i // FWD_UNROLL, lambda t, cr: tiles([FWD_UNROLL * t + u for u in range(FWD_UNROLL)], cr, False), init)
        carry = lax.fori_loop(i - i % FWD_UNROLL, i, lambda j, cr: tiles([j], cr, False), carry)
        carry = tiles([i], carry, True)
        for hh in range(2):
            m, acc = carry[hh]
            l = acc[:, L_ROW:L_ROW + 1]
            o_ref[:, _head_sl(hh)] = acc[:, :HEAD_DIM] / l
            hi, mid, lo = _split3(-(m + jnp.log(l)))
            lse_ref[hh] = _lanes(tq, LANES, {L_LSE: hi, L_LSE + 1: mid, L_LSE + 2: lo}).astype(BF16)
        if n:
            @pl.when((pl.program_id(0) == N_HEADS // 2 - 1) & (i == nq - 1))
            def _():
                side.finish(ins, outs, sems)

    tile = pl.BlockSpec((2, tq, LANES), lambda p, i: (p, i, 0))
    seq = pl.BlockSpec((2, T, LANES), lambda p, i: (p, 0, 0))
    res = pl.pallas_call(
        body, name="fox_fwd", grid=(N_HEADS // 2, nq),
        in_specs=[tile, seq, seq, *side.specs],
        out_specs=[pl.BlockSpec((tq, LANES), lambda p, i: (i, p)), tile, *side.specs],
        out_shape=[jax.ShapeDtypeStruct((T, D_B), F32), jax.ShapeDtypeStruct((N_HEADS, T, LANES), BF16), *side.out_shape],
        scratch_shapes=side.scratch,
        compiler_params=_cp("arbitrary", "arbitrary"),
    )(qa, ka, va, *side.operands)
    return res[0], res[1], res[2:]


def _fox_bwd(qa, lse, doa, ka, va, side):
    T = qa.shape[1]
    tq = min(TQ, T)
    nq = T // tq
    n = side.n

    def body(qa_ref, lse_ref, doa_ref, ka_ref, va_ref, *rest):
        ins, (dqa_ref, dka_ref, dva_ref), outs, sems = rest[:n], rest[n:n + 3], rest[n + 3:2 * n + 3], rest[2 * n + 3:]
        j = pl.program_id(1)
        if n:
            @pl.when((pl.program_id(0) == 0) & (j == 0))
            def _():
                side.start(ins, outs, sems)

        @pl.when(j == 0)
        def _():
            dqa_ref[...] = jnp.zeros_like(dqa_ref)

        row = lax.broadcasted_iota(jnp.int32, (tq, tq), 0)
        col = lax.broadcasted_iota(jnp.int32, (tq, tq), 1)
        ks = [ka_ref[0], ka_ref[1]]
        vs = [va_ref[0], va_ref[1]]

        def tiles(ids, carry, diag):
            work = []
            for i in ids:
                off = pl.multiple_of(i * tq, tq)
                for hh in range(2):
                    qi = qa_ref[hh, pl.ds(off, tq), :] + lse_ref[hh, pl.ds(off, tq), :]
                    doi = doa_ref[hh, pl.ds(off, tq), :]
                    work.append((off, hh, qi, doi, _dot_nt(ks[hh], qi), _dot_nt(vs[hh], doi)))
            carry = list(carry)
            for off, hh, qi, doi, st, dpt in work:
                dk, dv = carry[hh]
                if diag:
                    st = jnp.where(row <= col, st, NEG)
                pt = jnp.exp(st)
                dv = dv + _dot(pt.astype(BF16), doi)
                dsb = (pt * dpt).astype(BF16)
                dk = dk + _dot(dsb, qi)
                dqa_ref[hh, pl.ds(off, tq), :] += _dot_tn(dsb, ks[hh])
                carry[hh] = (dk, dv)
            return tuple(carry)

        init = ((jnp.zeros((tq, LANES), F32), jnp.zeros((tq, LANES), F32)),) * 2
        carry = tiles([j], init, True)
        todo = nq - 1 - j
        carry = lax.fori_loop(0, todo // BWD_UNROLL,
                              lambda t, cr: tiles([j + 1 + BWD_UNROLL * t + u for u in range(BWD_UNROLL)], cr, False), carry)
        carry = lax.fori_loop(nq - todo % BWD_UNROLL, nq, lambda i, cr: tiles([i], cr, False), carry)
        for hh in range(2):
            dka_ref[hh] = carry[hh][0]
            dva_ref[hh] = carry[hh][1].astype(BF16)
        if n:
            @pl.when((pl.program_id(0) == N_HEADS // 2 - 1) & (j == nq - 1))
            def _():
                side.finish(ins, outs, sems)

    tile = pl.BlockSpec((2, tq, LANES), lambda p, j: (p, j, 0))
    seq = pl.BlockSpec((2, T, LANES), lambda p, j: (p, 0, 0))
    res = pl.pallas_call(
        body, name="fox_bwd", grid=(N_HEADS // 2, nq),
        in_specs=[seq, seq, seq, tile, tile, *side.specs],
        out_specs=[seq, tile, tile, *side.specs],
        out_shape=[jax.ShapeDtypeStruct((N_HEADS, T, LANES), F32), jax.ShapeDtypeStruct((N_HEADS, T, LANES), F32),
                   jax.ShapeDtypeStruct((N_HEADS, T, LANES), BF16), *side.out_shape],
        scratch_shapes=side.scratch,
        compiler_params=_cp("arbitrary", "arbitrary"),
    )(qa, lse, doa, ka, va, *side.operands)
    return res[0], res[1], res[2], res[3:]


def _fwd_mid(x, oa, ob, w_out, g2, w_up):
    T = x.shape[0]
    tm = min(TM, T)

    def body(x_ref, oa_ref, ob_ref, wo_ref, g_ref, wu_ref, x2_ref, h2_ref, a_ref):
        oab = jnp.concatenate([oa_ref[...], ob_ref[...].astype(BF16)], axis=-1)
        x2 = x_ref[...] + _dot(oab, wo_ref[...])
        x2_ref[...] = x2
        _, n = _rms(x2)
        h2 = (n * g_ref[...]).astype(BF16)
        h2_ref[...] = h2
        a_ref[...] = _dot(h2, wu_ref[...])

    return pl.pallas_call(
        body, name="fwd_mid", grid=(T // tm,),
        in_specs=[_rows(tm, D_MODEL), _rows(tm, D_A), _rows(tm, D_B), _full((D_MODEL, D_MODEL), True), _full((1, D_MODEL)),
                  _full((D_MODEL, D_FF2), True)],
        out_specs=[_rows(tm, D_MODEL), _rows(tm, D_MODEL), _rows(tm, D_FF2)],
        out_shape=[jax.ShapeDtypeStruct((T, D_MODEL), F32), jax.ShapeDtypeStruct((T, D_MODEL), BF16),
                   jax.ShapeDtypeStruct((T, D_FF2), F32)],
        compiler_params=_cp("arbitrary"),
    )(x, oa, ob, w_out, g2, w_up)


def _row_before(x, prev, k):
    rolled = pltpu.roll(x, k, axis=0)
    row = lax.broadcasted_iota(jnp.int32, (8, x.shape[1]), 0)
    head = rolled[0:8]
    for r in range(k):
        head = jnp.where(row == r, prev[8 - k + r:9 - k + r], head)
    return jnp.concatenate([head, rolled[8:]], axis=0)


def _row_after(x, nxt, k):
    tm = x.shape[0]
    rolled = pltpu.roll(x, tm - k, axis=0)
    row = lax.broadcasted_iota(jnp.int32, (8, x.shape[1]), 0)
    tail = rolled[tm - 8:tm]
    for r in range(k):
        tail = jnp.where(row == 8 - k + r, nxt[r:r + 1], tail)
    return jnp.concatenate([rolled[:tm - 8], tail], axis=0)


def _fwd_ffn(a, x2, wc, bc, w_down, g3, tgt):
    T = x2.shape[0]
    tm = min(TM, T)

    def body(a_ref, x2_ref, wc_ref, bc_ref, wd_ref, g_ref, tgt_ref, ac_ref, yff_ref, dx3_ref, loss_ref, dg3_ref, carry):
        @pl.when(pl.program_id(0) == 0)
        def _():
            carry[...] = jnp.zeros_like(carry)
            loss_ref[...] = jnp.zeros_like(loss_ref)
            dg3_ref[...] = jnp.zeros_like(dg3_ref)

        def conv(cs):
            a0 = a_ref[:, cs]
            prev = carry[:, cs]
            ac = (wc_ref[0:1, cs] * _row_before(a0, prev, 2) + wc_ref[1:2, cs] * _row_before(a0, prev, 1)
                  + wc_ref[2:3, cs] * a0 + bc_ref[:, cs])
            ac_ref[:, cs] = ac.astype(BF16)
            return ac

        x3 = x2_ref[...]
        for ci in range(D_FF // CW):
            gs = slice(ci * CW, (ci + 1) * CW)
            ag = conv(gs)
            av = conv(slice(D_FF + ci * CW, D_FF + (ci + 1) * CW))
            yb = (ag * jax.nn.sigmoid(ag) * av).astype(BF16)
            yff_ref[:, gs] = yb
            x3 = x3 + _dot(yb, wd_ref[gs, :])
        carry[...] = a_ref[tm - 8:tm, :]
        r, n = _rms(x3)
        g = g_ref[...]
        diff = n * g - tgt_ref[...]
        loss_ref[...] += (0.5 / D_MODEL) * jnp.sum(diff * diff)
        dout = diff * (1.0 / D_MODEL)
        dg3_ref[...] += jnp.sum(dout * n, axis=0, keepdims=True)
        dx3_ref[...] = _rms_bwd(dout, n, r, g)

    return pl.pallas_call(
        body, name="fwd_ffn", grid=(T // tm,),
        in_specs=[_rows(tm, D_FF2), _rows(tm, D_MODEL), _full((3, D_FF2)), _full((1, D_FF2)), _full((D_FF, D_MODEL), True),
                  _full((1, D_MODEL)), _rows(tm, D_MODEL)],
        out_specs=[_rows(tm, D_FF2), _rows(tm, D_FF), _rows(tm, D_MODEL), _full((8, LANES)), _full((1, D_MODEL))],
        out_shape=[jax.ShapeDtypeStruct((T, D_FF2), BF16), jax.ShapeDtypeStruct((T, D_FF), BF16),
                   jax.ShapeDtypeStruct((T, D_MODEL), F32), jax.ShapeDtypeStruct((8, LANES), F32),
                   jax.ShapeDtypeStruct((1, D_MODEL), F32)],
        scratch_shapes=[pltpu.VMEM((8, D_FF2), F32)],
        compiler_params=_cp("arbitrary"),
    )(a, x2, wc, bc, w_down, g3, tgt)


def _bwd_ffn(dx3, a, ac, yff, h2, w_down, wc):
    T = dx3.shape[0]
    tm = min(TM, T)
    nt = T // tm
    half = D_FF // 2
    shard_up, shard_down = D_FF2 // N_DEV, D_FF // N_DEV

    def body(dx3_ref, ag_ref, av_ref, acg_ref, acv_ref, yff_ref, h2_ref, wd_ref, wcg_ref, wcv_ref,
             dag_ref, dav_ref, dwcg_ref, dwcv_ref, dbcg_ref, dbcv_ref, dwd_ref, dwu_ref,
             nxt, shifted, acc_down, acc_g, acc_v, stage_up, stage_down, sem):
        c, r = pl.program_id(0), pl.program_id(1)

        @pl.when(r == 0)
        def _():
            for ref in (nxt, dwcg_ref, dwcv_ref, dbcg_ref, dbcv_ref, acc_down, acc_g, acc_v):
                ref[...] = jnp.zeros_like(ref)

        dxb = dx3_ref[...].astype(BF16)
        dy_all = _dot_nt(dxb, wd_ref[...])

        def back(a_ref, w_ref, da_ref, dwc_ref, dbc_ref, nx, cs, dac):
            a0 = a_ref[:, cs]
            shifted[0] = _row_after(dac, nxt[:, nx], 1)
            shifted[1] = _row_after(dac, nxt[:, nx], 2)
            dp1, dp2 = shifted[0], shifted[1]
            dbc_ref[:, cs] += jnp.sum(dac, axis=0, keepdims=True)
            dwc_ref[0:1, cs] += jnp.sum(dp2 * a0, axis=0, keepdims=True)
            dwc_ref[1:2, cs] += jnp.sum(dp1 * a0, axis=0, keepdims=True)
            dwc_ref[2:3, cs] += jnp.sum(dac * a0, axis=0, keepdims=True)
            da_ref[:, cs] = (w_ref[2:3, cs] * dac + w_ref[1:2, cs] * dp1 + w_ref[0:1, cs] * dp2).astype(BF16)
            nxt[:, nx] = dac[0:8]

        for ci in range(half // LANES):
            cs = slice(ci * LANES, (ci + 1) * LANES)
            dy = dy_all[:, cs]
            ag, av = acg_ref[:, cs].astype(F32), acv_ref[:, cs].astype(F32)
            sg = jax.nn.sigmoid(ag)
            back(av_ref, wcv_ref, dav_ref, dwcv_ref, dbcv_ref, slice(half + ci * LANES, half + (ci + 1) * LANES), cs,
                 dy * (ag * sg))
            back(ag_ref, wcg_ref, dag_ref, dwcg_ref, dbcg_ref, cs, cs, dy * av * (sg * (1.0 + ag * (1.0 - sg))))

        acc_down[...] += _dot_tn(yff_ref[...], dxb)
        h2 = h2_ref[...]
        acc_g[...] += _dot_tn(h2, dag_ref[...])
        acc_v[...] += _dot_tn(h2, dav_ref[...])

        @pl.when(r == nt - 1)
        def _():
            for s in range(half // shard_down):
                stage_down[...] = acc_down[s * shard_down:(s + 1) * shard_down, :].astype(BF16)
                out = pltpu.make_async_copy(stage_down, dwd_ref.at[(half // shard_down) * c + s], sem)
                out.start()
                out.wait()
            for acc, first in ((acc_g, 0), (acc_v, N_DEV // 2)):
                for s in range(half // shard_up):
                    stage_up[...] = acc[:, s * shard_up:(s + 1) * shard_up].astype(BF16)
                    out = pltpu.make_async_copy(stage_up, dwu_ref.at[first + (half // shard_up) * c + s], sem)
                    out.start()
                    out.wait()

    def cols(width, second_half):
        return pl.BlockSpec((tm, width), lambda c, r: (nt - 1 - r, c + (2 if second_half else 0)))

    def param(rows, second_half):
        return pl.BlockSpec((rows, half), lambda c, r: (0, c + (2 if second_half else 0)))

    tokens = pl.BlockSpec((tm, D_MODEL), lambda c, r: (nt - 1 - r, 0))
    return pl.pallas_call(
        body, name="bwd_ffn", grid=(2, nt),
        in_specs=[tokens, cols(half, False), cols(half, True), cols(half, False), cols(half, True), cols(half, False), tokens,
                  pl.BlockSpec((half, D_MODEL), lambda c, r: (c, 0), pipeline_mode=pl.Buffered(1)),
                  param(3, False), param(3, True)],
        out_specs=[cols(half, False), cols(half, False), param(3, False), param(3, False), param(1, False), param(1, False),
                   ANY, ANY],
        out_shape=[jax.ShapeDtypeStruct((T, D_FF), BF16), jax.ShapeDtypeStruct((T, D_FF), BF16),
                   jax.ShapeDtypeStruct((3, D_FF), F32), jax.ShapeDtypeStruct((3, D_FF), F32),
                   jax.ShapeDtypeStruct((1, D_FF), F32), jax.ShapeDtypeStruct((1, D_FF), F32),
                   jax.ShapeDtypeStruct((N_DEV, shard_down, D_MODEL), BF16), jax.ShapeDtypeStruct((N_DEV, D_MODEL, shard_up), BF16)],
        scratch_shapes=[pltpu.VMEM((8, D_FF), F32), pltpu.VMEM((2, tm, LANES), F32), pltpu.VMEM((half, D_MODEL), F32),
                        pltpu.VMEM((D_MODEL, half), F32), pltpu.VMEM((D_MODEL, half), F32),
                        pltpu.VMEM((D_MODEL, shard_up), BF16), pltpu.VMEM((shard_down, D_MODEL), BF16),
                        pltpu.SemaphoreType.DMA],
        compiler_params=_cp("arbitrary", "arbitrary"),
    )(dx3, a, a, ac, ac, yff, h2, w_down, wc, wc)


def _bwd_mid(da_g, da_v, w_up, x2, g2, dx3, w_out, ob):
    T = x2.shape[0]
    tm = min(TM, T)

    def body(dag_ref, dav_ref, wu_ref, x2_ref, g_ref, dx3_ref, wo_ref, ob_ref, hsum_ref, place_ref,
             dx2_ref, doa_ref, dob_ref, dg2_ref):
        @pl.when(pl.program_id(0) == 0)
        def _():
            dg2_ref[...] = jnp.zeros_like(dg2_ref)

        dh2 = _dot_nt(dag_ref[...], wu_ref[:, :D_FF]) + _dot_nt(dav_ref[...], wu_ref[:, D_FF:])
        r, n = _rms(x2_ref[...])
        dg2_ref[...] += jnp.sum(dh2 * n, axis=0, keepdims=True)
        dx2 = dx3_ref[...] + _rms_bwd(dh2, n, r, g_ref[...])
        dx2_ref[...] = dx2
        doab = _dot_nt(dx2.astype(BF16), wo_ref[...])
        doa_ref[...] = doab[:, :D_A]
        dob = doab[:, D_A:]
        rest, delta = dob * ob_ref[...], None
        for _ in range(3):
            piece = rest.astype(BF16)
            term = _dot(piece, hsum_ref[...])
            delta = term if delta is None else delta + term
            rest = rest - piece.astype(F32)
        hi, mid, lo = _split3(-delta)
        parts = jnp.concatenate([hi.astype(BF16), mid.astype(BF16), lo.astype(BF16)], axis=-1)
        placed = _dot(parts, place_ref[...])
        data = lax.broadcasted_iota(jnp.int32, (tm, LANES), 1) < HEAD_DIM
        for hd in range(N_HEADS):
            tile = dob[:, LANES * (hd // 2):LANES * (hd // 2 + 1)]
            rows = tile if hd % 2 == 0 else pltpu.roll(tile, HEAD_DIM, axis=1)
            dob_ref[hd] = jnp.where(data, rows, placed[:, LANES * hd:LANES * (hd + 1)]).astype(BF16)

    hsum = (jnp.arange(D_B)[:, None] // HEAD_DIM == jnp.arange(LANES)[None, :]).astype(BF16)
    j, h = jnp.arange(3 * LANES) // LANES, jnp.arange(3 * LANES) % LANES
    place = ((h[:, None] < N_HEADS)
             & (jnp.arange(N_HEADS * LANES)[None, :] == LANES * h[:, None] + L_ROW + j[:, None])).astype(BF16)
    return pl.pallas_call(
        body, name="bwd_mid", grid=(T // tm,),
        in_specs=[_rows(tm, D_FF), _rows(tm, D_FF), _full((D_MODEL, D_FF2), True), _rows(tm, D_MODEL), _full((1, D_MODEL)),
                  _rows(tm, D_MODEL), _full((D_MODEL, D_MODEL), True), _rows(tm, D_B), _full((D_B, LANES)),
                  _full((3 * LANES, N_HEADS * LANES))],
        out_specs=[_rows(tm, D_MODEL), _rows(tm, D_A), pl.BlockSpec((N_HEADS, tm, LANES), lambda i: (0, i, 0)),
                   _full((1, D_MODEL))],
        out_shape=[jax.ShapeDtypeStruct((T, D_MODEL), F32), jax.ShapeDtypeStruct((T, D_A), F32),
                   jax.ShapeDtypeStruct((N_HEADS, T, LANES), BF16), jax.ShapeDtypeStruct((1, D_MODEL), F32)],
        compiler_params=_cp("arbitrary"),
    )(da_g, da_v, w_up, x2, g2, dx3, w_out, ob, hsum, place)


def _bwd_in(dzuv, dqa, dka, dva, fl, x, dx2, w_in_p, g1):
    T = x.shape[0]
    tm = min(TM, T)
    nt = T // tm

    def body(dzuv_ref, dqa_ref, dka_ref, dva_ref, fl_ref, x_ref, dx2_ref, w_ref, g_ref,
             gx_ref, dz_ref, dg1_ref, dfb_ref, carry):
        @pl.when(pl.program_id(0) == 0)
        def _():
            carry[...] = jnp.zeros_like(carry)
            dg1_ref[...] = jnp.zeros_like(dg1_ref)
            dfb_ref[...] = jnp.zeros_like(dfb_ref)

        dc = _lanes(tm, LANES, {hd: dqa_ref[hd][:, L_ROW:L_ROW + 1] - dka_ref[hd][:, L_COL:L_COL + 1] for hd in range(N_HEADS)})
        later = (lax.broadcasted_iota(jnp.int32, (tm, tm), 1) >= lax.broadcasted_iota(jnp.int32, (tm, tm), 0)).astype(F32)
        dls = _dot_f32(later, dc) + carry[...]
        carry[...] = dls[0:1, :]
        dzf = dls * jax.nn.sigmoid(-fl_ref[...])
        dfb_ref[...] += jnp.sum(dzf, axis=0, keepdims=True)
        data = lax.broadcasted_iota(jnp.int32, (tm, LANES), 1) < HEAD_DIM

        def compact(ref):
            return [jnp.where(data, ref[2 * p], pltpu.roll(ref[2 * p + 1], HEAD_DIM, axis=1)) for p in range(N_HEADS // 2)]

        dz = jnp.concatenate([dzuv_ref[...], *[(t * (HEAD_DIM ** -0.5)).astype(BF16) for t in compact(dqa_ref)],
                              *[t.astype(BF16) for t in compact(dka_ref)], *compact(dva_ref), dzf.astype(BF16)], axis=-1)
        dz_ref[...] = dz
        dh1 = _dot_nt(dz, w_ref[...])
        r, n = _rms(x_ref[...])
        dg1_ref[...] += jnp.sum(dh1 * n, axis=0, keepdims=True)
        gx_ref[...] = dx2_ref[...] + _rms_bwd(dh1, n, r, g_ref[...])

    rv = functools.partial(_rows, tm, rev_nt=nt)
    heads = pl.BlockSpec((N_HEADS, tm, LANES), lambda i: (0, nt - 1 - i, 0))
    return pl.pallas_call(
        body, name="bwd_in", grid=(nt,),
        in_specs=[rv(2 * D_A), heads, heads, heads, rv(LANES), rv(D_MODEL), rv(D_MODEL),
                  _full((D_MODEL, D_IN_PAD), True), _full((1, D_MODEL))],
        out_specs=[rv(D_MODEL), rv(D_IN_PAD), _full((1, D_MODEL)), _full((1, LANES))],
        out_shape=[jax.ShapeDtypeStruct((T, D_MODEL), F32), jax.ShapeDtypeStruct((T, D_IN_PAD), BF16),
                   jax.ShapeDtypeStruct((1, D_MODEL), F32), jax.ShapeDtypeStruct((1, LANES), F32)],
        scratch_shapes=[pltpu.VMEM((1, LANES), F32)],
        compiler_params=_cp("arbitrary"),
    )(dzuv, dqa, dka, dva, fl, x, dx2, w_in_p, g1)


def _matmul_tn(a_parts, b, tmm, tn, tk, name, shard_cols=None, n_valid=None):
    T = b.shape[0]
    widths = [a.shape[1] for a in a_parts]
    M, N = sum(widths), b.shape[1]
    tk = min(tk, T)
    nk = T // tk
    part_w = tmm // len(a_parts)
    n_valid = N if n_valid is None else n_valid

    def body(*refs):
        a_refs, b_ref, o_ref, obf_ref = refs[:len(a_parts)], refs[-3], refs[-2], refs[-1]
        k = pl.program_id(2)

        @pl.when(k == 0)
        def _():
            o_ref[...] = jnp.zeros_like(o_ref)

        a = [r[...].astype(BF16) for r in a_refs]
        o_ref[...] += _dot_tn(a[0] if len(a) == 1 else jnp.concatenate(a, axis=-1), b_ref[...].astype(BF16))

        @pl.when(k == nk - 1)
        def _():
            if shard_cols is None:
                obf_ref[...] = o_ref[...].astype(BF16)
            else:
                for d in range(min(tn, n_valid) // shard_cols):
                    obf_ref[d] = o_ref[:, d * shard_cols:(d + 1) * shard_cols].astype(BF16)

    if shard_cols is None:
        bf_spec, bf_shape = pl.BlockSpec((tmm, tn), lambda i, j, k: (i, j)), (M, N)
    else:
        per_tile = min(tn, n_valid) // shard_cols
        bf_spec, bf_shape = pl.BlockSpec((per_tile, tmm, shard_cols), lambda i, j, k: (j, i, 0)), (N_DEV, M, shard_cols)
    a_specs = [pl.BlockSpec((tk, part_w), lambda i, j, k: (k, i)) for _ in a_parts]
    return pl.pallas_call(
        body, name=name, grid=(M // tmm, N // tn, nk),
        in_specs=[*a_specs, pl.BlockSpec((tk, tn), lambda i, j, k: (k, j))],
        out_specs=[pl.BlockSpec((tmm, tn), lambda i, j, k: (i, j)), bf_spec],
        out_shape=[jax.ShapeDtypeStruct((M, N), F32), jax.ShapeDtypeStruct(bf_shape, BF16)],
        compiler_params=_cp("arbitrary", "arbitrary", "arbitrary"),
    )(*a_parts, b)


class _Exchange:
    def __init__(self, gather, scatter, relay):
        self.n_g, self.n, self.relay = len(gather), len(gather) + len(scatter), relay
        self.operands = [*gather, *scatter]
        self.out_shape = [jax.ShapeDtypeStruct((N_DEV, *g.shape), g.dtype) for g in gather]
        self.out_shape += [jax.ShapeDtypeStruct(s.shape, s.dtype) for s in scatter]
        self.specs = [ANY] * self.n
        n = self.n
        self.scratch = [pltpu.SemaphoreType.DMA((7 * n,)), pltpu.SemaphoreType.DMA((7 * n,)),
                        pltpu.SemaphoreType.DMA((n,))] if n else []

    def _plan(self, ins, outs, sems):
        send_sems, recv_sems, local_sems = sems
        x, y, c = (lax.axis_index(ax) for ax in MESH_AXES)
        me = 4 * x + 2 * y + c
        sibling = (x, y, 1 - c)
        chips = [(1 - x, y), (x, 1 - y), (1 - x, 1 - y)]
        peers = [sibling] + [(*chip, c) for chip in chips] + [(*chip, 1 - c) for chip in chips]

        def index(dev):
            return 4 * dev[0] + 2 * dev[1] + dev[2]

        def remote(k, src, dst, to):
            return pltpu.make_async_remote_copy(src_ref=src, dst_ref=dst, send_sem=send_sems.at[k], recv_sem=recv_sems.at[k],
                                                device_id=to, device_id_type=pl.DeviceIdType.MESH)

        local, sends, relays, recvs = [], [], [], []
        for a in range(self.n):
            src, out, base = ins[a], outs[a], 7 * a
            if a >= self.n_g:
                local.append(pltpu.make_async_copy(src.at[me], out.at[me], local_sems.at[a]))
                sends += [remote(base + k, src.at[index(peer)], out.at[me], peer) for k, peer in enumerate(peers)]
            else:
                local.append(pltpu.make_async_copy(src, out.at[me], local_sems.at[a]))
                sends += [remote(base + k, src, out.at[me], peer) for k, peer in enumerate(peers[:4 if self.relay else 7])]
            for k, peer in enumerate(peers):
                slot = out.at[index(peer)]
                if a < self.n_g and self.relay and k >= 4:
                    continue
                recv = remote(base + k, slot, slot, peer)
                if a < self.n_g and self.relay and k >= 1:
                    relays.append((recv, remote(base + 3 + k, slot, slot, sibling)))
                else:
                    recvs.append(recv)
            if a < self.n_g and self.relay:
                for j, chip in enumerate(chips):
                    slot = out.at[index((*chip, 1 - c))]
                    recvs.append(remote(base + 4 + j, slot, slot, sibling))
        return local, sends, relays, recvs

    def start(self, ins, outs, sems):
        local, sends, _, _ = self._plan(ins, outs, sems)
        for cp in local + sends:
            cp.start()

    def finish(self, ins, outs, sems):
        local, sends, relays, recvs = self._plan(ins, outs, sems)
        for recv, fwd in relays:
            recv.wait_recv()
            fwd.start()
        for recv in recvs:
            recv.wait_recv()
        for cp in sends + [fwd for _, fwd in relays]:
            cp.wait_send()
        for cp in local:
            cp.wait()


def _exchange(gather, scatter, name):
    ex = _Exchange(gather, scatter, relay=True)
    n = ex.n

    def body(*refs):
        ins, outs, sems = refs[:n], refs[n:2 * n], refs[2 * n:]
        ex.start(ins, outs, sems)
        ex.finish(ins, outs, sems)

    return pl.pallas_call(body, name=name, in_specs=ex.specs, out_specs=ex.specs, out_shape=ex.out_shape,
                          scratch_shapes=ex.scratch)(*ex.operands)


def _adamw(w, g, m, v):
    m = ADAM_B1 * m + (1.0 - ADAM_B1) * g
    v = ADAM_B2 * v + (1.0 - ADAM_B2) * jnp.square(g)
    m_hat = m / (1.0 - ADAM_B1 ** ADAM_STEP)
    v_hat = v / (1.0 - ADAM_B2 ** ADAM_STEP)
    delta = -ADAM_LR * (m_hat / (jnp.sqrt(v_hat) + ADAM_EPS) + ADAM_WD * w)
    return delta, m, v


def _adamw_shard(w, m, v, recv, tr, name):
    _, R, C = w.shape

    def body(w_ref, m_ref, v_ref, recv_ref, g_ref, d_ref, nm_ref, nv_ref):
        g = recv_ref[0].astype(F32)
        for d in range(1, N_DEV):
            g = g + recv_ref[d].astype(F32)
        g_ref[...] = g
        d_ref[...], nm_ref[...], nv_ref[...] = _adamw(w_ref[...], g, m_ref[...], v_ref[...])

    blk = pl.BlockSpec((None, tr, C), lambda i: (0, i, 0))
    return pl.pallas_call(
        body, name=name, grid=(R // tr,),
        in_specs=[blk, blk, blk, pl.BlockSpec((N_DEV, tr, C), lambda i: (0, i, 0))],
        out_specs=[blk] * 4, out_shape=[jax.ShapeDtypeStruct((1, R, C), F32)] * 4,
        compiler_params=_cp("arbitrary"),
    )(w, m, v, recv)


def _adamw_small(w, m, v, gall):
    R = w.shape[0]
    tr = R // 3

    def body(w_ref, m_ref, v_ref, gall_ref, g_ref, d_ref, nm_ref, nv_ref):
        g = gall_ref[0]
        for d in range(1, N_DEV):
            g = g + gall_ref[d]
        g_ref[...] = g
        d_ref[...], nm_ref[...], nv_ref[...] = _adamw(w_ref[...], g, m_ref[...], v_ref[...])

    blk = pl.BlockSpec((tr, LANES), lambda i: (i, 0))
    return pl.pallas_call(
        body, name="adamw_small", grid=(R // tr,),
        in_specs=[blk, blk, blk, pl.BlockSpec((N_DEV, tr, LANES), lambda i: (0, i, 0))],
        out_specs=[blk] * 4, out_shape=[jax.ShapeDtypeStruct((R, LANES), F32)] * 4,
        compiler_params=_cp("arbitrary"),
    )(w, m, v, gall)


_SMALL_LATE = (("g1", 8), ("fb", 8))
_SMALL_EARLY = (("lg", 8), ("sg_w", 1024), ("sg_b", 8), ("g2", 8), ("bc", 48), ("g3", 8), ("loss", 8))
_SMALL = _SMALL_LATE + _SMALL_EARLY


def _pack_small(parts, layout):
    rows = []
    for name, nrow in layout:
        flat = parts[name].astype(F32).reshape(-1)
        flat = jnp.pad(flat, (0, nrow * LANES - flat.shape[0]))
        rows.append(flat.reshape(nrow, LANES))
    return jnp.concatenate(rows, axis=0)


def _unpack_small(packed, shapes):
    out, r = {}, 0
    for name, nrow in _SMALL:
        size = math.prod(shapes[name])
        out[name] = packed[r:r + nrow].reshape(-1)[:size].reshape(shapes[name])
        r += nrow
    return out


def _col_shards(g):
    return jnp.transpose(g.reshape(g.shape[0], N_DEV, -1), (1, 0, 2))


def _row_shards(g):
    return g.reshape(N_DEV, -1, g.shape[1])


def _cols_whole(g):
    return jnp.transpose(g, (1, 0, 2)).reshape(g.shape[1], -1)


def kernel(x, norm_mix_g, w_in, f_bias, sg_ln_g, sg_w, sg_b, w_out, norm_ffn_g, w_up, w_conv, b_conv, w_down, norm_final_g, loss_target, m_norm_mix_g, m_w_in, m_f_bias, m_sg_ln_g, m_sg_w, m_sg_b, m_w_out, m_norm_ffn_g, m_w_up, m_w_conv, m_b_conv, m_w_down, m_norm_final_g, v_norm_mix_g, v_w_in, v_f_bias, v_sg_ln_g, v_sg_w, v_sg_b, v_w_out, v_norm_ffn_g, v_w_up, v_w_conv, v_b_conv, v_w_down, v_norm_final_g):
    xs, tgt = x[0], loss_target[0]
    g1, g2, g3 = norm_mix_g, norm_ffn_g, norm_final_g.reshape(1, D_MODEL)
    lg = sg_ln_g.reshape(1, D_A)
    fb = jnp.pad(f_bias, ((0, 0), (0, LANES - N_HEADS)))
    pos_chunk = jnp.arange(SG_BLOCK) // SG_CHUNK
    maskf = (pos_chunk[:, None] >= pos_chunk[None, :]).astype(F32)
    wm = (sg_w[0] * maskf[None]).astype(BF16)
    wmt = jnp.swapaxes(wm, 1, 2)
    bfull = jnp.repeat(sg_b[0].T, HEAD_DIM, axis=1)
    hsel = jnp.repeat(jnp.eye(N_HEADS, dtype=F32), HEAD_DIM, axis=1)
    avg = (jnp.repeat(hsel, HEAD_DIM, axis=0) * (1.0 / HEAD_DIM)).astype(BF16)

    (win_g,) = _exchange([w_in[0].astype(BF16)], [], "gather_w_in")
    w_in_p = jnp.pad(_cols_whole(win_g), ((0, 0), (0, D_IN_PAD - D_IN)))
    zuv, qa, ka, va, fl, h1 = _fwd_in(xs, g1, w_in_p, fb)
    oa = _gmlp_fwd(zuv, lg, wm, bfull, avg)
    rest = _Exchange([w_out[0].astype(BF16), w_up[0].astype(BF16), w_down[0].astype(BF16), w_conv[0]], [], relay=False)
    ob, lse, (wout_g, wup_g, wdown_g, wc_g) = _fox_fwd(qa, ka, va, rest)
    w_out_f, w_up_f = wout_g.reshape(D_MODEL, D_MODEL), _cols_whole(wup_g)
    w_down_f, wc_f = wdown_g.reshape(D_FF, D_MODEL), _cols_whole(wc_g)

    x2, h2, a = _fwd_mid(xs, oa, ob, w_out_f, g2, w_up_f)
    ac, yff, dx3, loss, dg3 = _fwd_ffn(a, x2, wc_f, b_conv, w_down_f, g3, tgt)
    da_g, da_v, dwc_g, dwc_v, dbc_g, dbc_v, dwdown_bf, dwup_bf = _bwd_ffn(dx3, a, ac, yff, h2, w_down_f, wc_f)
    dwc, dbc = jnp.concatenate([dwc_g, dwc_v], axis=1), jnp.concatenate([dbc_g, dbc_v], axis=1)
    dx2, doa, dob, dg2 = _bwd_mid(da_g, da_v, w_up_f, x2, g2, dx3, w_out_f, ob)
    dzuv, dwm, dsgb, dlg = _gmlp_bwd(zuv, doa, lg, wm, wmt, bfull, maskf, hsel, avg)
    _, dwout_bf = _matmul_tn([oa, ob], dx2, D_MODEL, D_MODEL, 1024, "dw_out")

    early = ("w_out", "w_up", "wc", "w_down")
    wire = [_row_shards(dwout_bf), dwup_bf, _col_shards(dwc).astype(BF16), dwdown_bf]
    small = dict(lg=dlg, sg_w=dwm, sg_b=dsgb, g2=dg2, bc=dbc, g3=dg3, loss=loss[0:1, 0:1])
    grads = _Exchange([_pack_small(small, _SMALL_EARLY)], wire, relay=False)
    dqa, dka, dva, got = _fox_bwd(qa, lse, dob, ka, va, grads)
    small_early, recv = got[0], dict(zip(early, got[1:]))

    gx, dz, dg1, dfb = _bwd_in(dzuv, dqa, dka, dva, fl, xs, dx2, w_in_p, g1)
    _, dwin_bf = _matmul_tn([h1], dz, D_MODEL // 2, D_IN_PAD, 1024, "dw_in", shard_cols=D_IN // N_DEV, n_valid=D_IN)
    small_late, recv["w_in"] = _exchange([_pack_small(dict(g1=dg1, fb=dfb), _SMALL_LATE)], [dwin_bf], "exchange_w_in")

    weights = dict(w_in=(w_in, m_w_in, v_w_in, 256), w_out=(w_out, m_w_out, v_w_out, 128), w_up=(w_up, m_w_up, v_w_up, 256),
                   wc=(w_conv, m_w_conv, v_w_conv, 3), w_down=(w_down, m_w_down, v_w_down, 176))
    res = {n: _adamw_shard(w, m, v, recv[n], tr, "adamw_" + n) for n, (w, m, v, tr) in weights.items()}

    zero = jnp.zeros((1, 1), F32)
    reps = dict(g1=(norm_mix_g, m_norm_mix_g, v_norm_mix_g), fb=(f_bias, m_f_bias, v_f_bias), lg=(sg_ln_g, m_sg_ln_g, v_sg_ln_g),
                sg_w=(sg_w, m_sg_w, v_sg_w), sg_b=(sg_b, m_sg_b, v_sg_b), g2=(norm_ffn_g, m_norm_ffn_g, v_norm_ffn_g),
                bc=(b_conv, m_b_conv, v_b_conv), g3=(norm_final_g, m_norm_final_g, v_norm_final_g), loss=(zero, zero, zero))
    shapes = {n: t[0].shape for n, t in reps.items()}
    packed = _adamw_small(*[_pack_small({n: t[i] for n, t in reps.items()}, _SMALL) for i in range(3)],
                          jnp.concatenate([small_late, small_early], axis=1))
    unpacked = [_unpack_small(p, shapes) for p in packed]
    for n in reps:
        res[n] = [u[n] for u in unpacked]

    names = ("g1", "w_in", "fb", "lg", "sg_w", "sg_b", "w_out", "g2", "w_up", "wc", "bc", "w_down", "g3")
    return (res["loss"][0][0, 0], gx[None], *[res[n][0] for n in names], *[res[n][1] for n in names],
            *[res[n][2] for n in names], *[res[n][3] for n in names])
```

```python
import functools
import math

import jax
import jax.numpy as jnp
from jax import lax
from jax.experimental import pallas as pl
from jax.experimental.pallas import tpu as pltpu

F32 = jnp.float32
BF16 = jnp.bfloat16

D_MODEL = 1024
HEAD_DIM = 64
N_HEADS = 8
D_A = 512
D_B = 512
D_IN = 2 * D_A + 3 * D_B + N_HEADS
D_IN_PAD = 2688
D_FF = 2816
D_FF2 = 2 * D_FF
SG_BLOCK = 128
SG_CHUNK = 64
EPS = 1e-6
N_DEV = 8
LANES = 128
NEG = -1e30
VMEM_LIMIT = 56 * 1024 * 1024

ADAM_LR = 0.001
ADAM_B1 = 0.9
ADAM_B2 = 0.999
ADAM_EPS = 1e-08
ADAM_WD = 0.01
ADAM_STEP = 10

TM = 256
TQ = 512
FWD_UNROLL = 4
BWD_UNROLL = 2
CW = 256

MESH_AXES = ("x", "y", "c")
ANY = pl.BlockSpec(memory_space=pl.ANY)


def _cp(*sem):
    return pltpu.CompilerParams(dimension_semantics=sem, vmem_limit_bytes=VMEM_LIMIT)


def _dot(a, b):
    return jnp.dot(a, b, preferred_element_type=F32)


def _dot_nt(a, b):
    return lax.dot_general(a, b, (((1,), (1,)), ((), ())), preferred_element_type=F32)


def _dot_tn(a, b):
    return lax.dot_general(a, b, (((0,), (0,)), ((), ())), preferred_element_type=F32)


def _dot_f32(a, b):
    return jnp.dot(a, b, precision=lax.Precision.HIGHEST, preferred_element_type=F32)


def _gelu(z):
    return 0.5 * z * (1.0 + lax.erf(z * (1.0 / math.sqrt(2.0))))


def _gelu_grad(z):
    return 0.5 * (1.0 + lax.erf(z * (1.0 / math.sqrt(2.0)))) + z * jnp.exp(-0.5 * z * z) * (1.0 / math.sqrt(2.0 * math.pi))


def _log_sigmoid(x):
    return jnp.minimum(x, 0.0) - jnp.log1p(jnp.exp(-jnp.abs(x)))


def _rms(x):
    r = lax.rsqrt(jnp.mean(x * x, axis=-1, keepdims=True) + EPS)
    return r, x * r


def _rms_bwd(dy, n, r, g):
    dn = dy * g
    return r * (dn - n * jnp.mean(dn * n, axis=-1, keepdims=True))


def _full(shape, single=False):
    nd = len(shape)
    if single:
        return pl.BlockSpec(shape, lambda *_: (0,) * nd, pipeline_mode=pl.Buffered(1))
    return pl.BlockSpec(shape, lambda *_: (0,) * nd)


def _rows(tm, cols, rev_nt=None):
    if rev_nt is None:
        return pl.BlockSpec((tm, cols), lambda i: (i, 0))
    return pl.BlockSpec((tm, cols), lambda i: (rev_nt - 1 - i, 0))


def _head_sl(h):
    return slice(HEAD_DIM * h, HEAD_DIM * (h + 1))


L_ROW = HEAD_DIM
L_COL = HEAD_DIM + 3
L_LSE = HEAD_DIM + 6


def _split3(x):
    hi = x.astype(BF16).astype(F32)
    mid = (x - hi).astype(BF16).astype(F32)
    lo = (x - hi - mid).astype(BF16).astype(F32)
    return hi, mid, lo


def _lanes(rows, width, parts):
    lane = lax.broadcasted_iota(jnp.int32, (rows, width), 1)
    out = jnp.zeros((rows, width), F32)
    for at, val in parts.items():
        out = jnp.where(lane == at, val, out)
    return out


def _fwd_in(x, g1, w_in_p, fb):
    T = x.shape[0]
    tm = min(TM, T)

    def body(x_ref, g_ref, w_ref, fb_ref, place_ref, zuv_ref, qa_ref, ka_ref, va_ref, fl_ref, h1_ref, carry):
        @pl.when(pl.program_id(0) == 0)
        def _():
            carry[...] = jnp.zeros_like(carry)

        r, n = _rms(x_ref[...])
        h = (n * g_ref[...]).astype(BF16)
        h1_ref[...] = h
        z = _dot(h, w_ref[...])
        zuv_ref[...] = z[:, :2 * D_A]
        o = 2 * D_A
        fl = z[:, o + 3 * D_B:] + fb_ref[...]
        fl_ref[...] = fl
        tri = (lax.broadcasted_iota(jnp.int32, (tm, tm), 0) >= lax.broadcasted_iota(jnp.int32, (tm, tm), 1)).astype(F32)
        c = _dot_f32(tri, _log_sigmoid(fl)) + carry[...]
        carry[...] = c[tm - 1:tm, :]
        hi, mid, lo = _split3(c)
        parts = jnp.concatenate([hi.astype(BF16), mid.astype(BF16), lo.astype(BF16)], axis=-1)
        placed = _dot(parts, place_ref[...])
        lane = lax.broadcasted_iota(jnp.int32, (tm, LANES), 1)
        data = lane < HEAD_DIM
        ones_q = ((lane >= L_COL) & (lane < L_COL + 3)).astype(F32)
        ones_k = (((lane >= L_ROW) & (lane < L_ROW + 3)) | ((lane >= L_LSE) & (lane < L_LSE + 3))).astype(F32)
        ones_v = ((lane >= L_ROW) & (lane < L_ROW + 3)).astype(F32)
        for hd in range(N_HEADS):
            def rows_of(first_col):
                tile = z[:, first_col + LANES * (hd // 2):first_col + LANES * (hd // 2 + 1)]
                return tile if hd % 2 == 0 else pltpu.roll(tile, HEAD_DIM, axis=1)

            hs = slice(LANES * hd, LANES * (hd + 1))
            qa_ref[hd] = jnp.where(data, rows_of(o) * (HEAD_DIM ** -0.5), placed[:, hs] + ones_q).astype(BF16)
            ka_ref[hd] = jnp.where(data, rows_of(o + D_B), ones_k - placed[:, N_HEADS * LANES:][:, hs]).astype(BF16)
            va_ref[hd] = jnp.where(data, rows_of(o + 2 * D_B), ones_v).astype(BF16)

    heads = pl.BlockSpec((N_HEADS, tm, LANES), lambda i: (0, i, 0))
    aug = jax.ShapeDtypeStruct((N_HEADS, T, LANES), BF16)
    j, h = jnp.arange(3 * LANES) // LANES, jnp.arange(3 * LANES) % LANES
    cols = jnp.arange(2 * N_HEADS * LANES)
    place = ((h[:, None] < N_HEADS) & ((cols[None, :] == LANES * h[:, None] + L_ROW + j[:, None])
                                       | (cols[None, :] == N_HEADS * LANES + LANES * h[:, None] + L_COL + j[:, None]))).astype(BF16)
    return pl.pallas_call(
        body, name="fwd_in", grid=(T // tm,),
        in_specs=[_rows(tm, D_MODEL), _full((1, D_MODEL)), _full((D_MODEL, D_IN_PAD), True), _full((1, LANES)),
                  _full((3 * LANES, 2 * N_HEADS * LANES))],
        out_specs=[_rows(tm, 2 * D_A), heads, heads, heads, _rows(tm, LANES), _rows(tm, D_MODEL)],
        out_shape=[jax.ShapeDtypeStruct((T, 2 * D_A), F32), aug, aug, aug, jax.ShapeDtypeStruct((T, LANES), F32),
                   jax.ShapeDtypeStruct((T, D_MODEL), BF16)],
        scratch_shapes=[pltpu.VMEM((1, LANES), F32)],
        compiler_params=_cp("arbitrary"),
    )(x, g1, w_in_p, fb, place)


def _segmean(x, avg_ref, parts):
    out, rest = None, x
    for _ in range(parts):
        piece = rest.astype(BF16)
        term = _dot(piece, avg_ref[...])
        out = term if out is None else out + term
        rest = rest - piece.astype(F32)
    return out


def _layer_norm(v, lg, avg_ref):
    d = v - _segmean(v, avg_ref, 3)
    rstd = lax.rsqrt(_segmean(d * d, avg_ref, 2) + EPS)
    vhat = d * rstd
    return vhat, rstd, (vhat * lg).astype(BF16)


def _mix_heads(w_ref, x, row_slice):
    low = lax.broadcasted_iota(jnp.int32, (SG_BLOCK, LANES), 1) < HEAD_DIM
    tiles = []
    for p in range(N_HEADS // 2):
        xt = x[row_slice, LANES * p:LANES * (p + 1)]
        zero = jnp.zeros_like(xt)
        tiles.append(_dot(w_ref[2 * p], jnp.where(low, xt, zero)) + _dot(w_ref[2 * p + 1], jnp.where(low, zero, xt)))
    return jnp.concatenate(tiles, axis=-1)


def _gmlp_fwd(zuv, lg, wm, bfull, avg):
    T = zuv.shape[0]
    tg = min(TM, T)
    nb = tg // SG_BLOCK

    def body(zuv_ref, lg_ref, wm_ref, b_ref, avg_ref, oa_ref):
        u = _gelu(zuv_ref[:, :D_A])
        _, _, vn = _layer_norm(_gelu(zuv_ref[:, D_A:]), lg_ref[...], avg_ref)
        for n in range(nb):
            rs = slice(n * SG_BLOCK, (n + 1) * SG_BLOCK)
            oa_ref[rs, :] = (u[rs] * (_mix_heads(wm_ref, vn, rs) + b_ref[...])).astype(BF16)

    return pl.pallas_call(
        body, name="gmlp_fwd", grid=(T // tg,),
        in_specs=[_rows(tg, 2 * D_A), _full((1, D_A)), _full((N_HEADS, SG_BLOCK, SG_BLOCK)), _full((SG_BLOCK, D_A)),
                  _full((D_A, D_A))],
        out_specs=_rows(tg, D_A),
        out_shape=jax.ShapeDtypeStruct((T, D_A), BF16),
        compiler_params=_cp("arbitrary"),
    )(zuv, lg, wm, bfull, avg)


def _gmlp_bwd(zuv, doa, lg, wm, wmt, bfull, maskf, hsel, avg):
    T = zuv.shape[0]
    tg = min(TM, T)
    nb = tg // SG_BLOCK
    nt = T // tg

    def body(zuv_ref, doa_ref, lg_ref, wm_ref, wmt_ref, b_ref, mask_ref, hsel_ref, avg_ref,
             dzuv_ref, dwm_ref, dsgb_ref, dlg_ref, dbacc):
        i = pl.program_id(0)

        @pl.when(i == 0)
        def _():
            dwm_ref[...] = jnp.zeros_like(dwm_ref)
            dlg_ref[...] = jnp.zeros_like(dlg_ref)
            dbacc[...] = jnp.zeros_like(dbacc)

        zu, zv = zuv_ref[:, :D_A], zuv_ref[:, D_A:]
        lgv = lg_ref[...]
        vhat, rstd, vn = _layer_norm(_gelu(zv), lgv, avg_ref)
        dmixed = doa_ref[...] * _gelu(zu)
        dmb = dmixed.astype(BF16)
        low = lax.broadcasted_iota(jnp.int32, (SG_BLOCK, LANES), 1) < HEAD_DIM
        mixed, dvn = [], []
        for n in range(nb):
            rs = slice(n * SG_BLOCK, (n + 1) * SG_BLOCK)
            mixed.append(_mix_heads(wm_ref, vn, rs) + b_ref[...])
            dvn.append(_mix_heads(wmt_ref, dmb, rs))
            dbacc[...] += dmixed[rs]
            for p in range(N_HEADS // 2):
                ls = slice(LANES * p, LANES * (p + 1))
                dmt, vnt = dmb[rs, ls], vn[rs, ls]
                zero = jnp.zeros_like(dmt)
                dwm_ref[2 * p] += _dot_nt(jnp.where(low, dmt, zero), vnt) * mask_ref[...]
                dwm_ref[2 * p + 1] += _dot_nt(jnp.where(low, zero, dmt), vnt) * mask_ref[...]
        mixed = jnp.concatenate(mixed, axis=0) if nb > 1 else mixed[0]
        dvn = jnp.concatenate(dvn, axis=0) if nb > 1 else dvn[0]
        dlg_ref[...] += jnp.sum(dvn * vhat, axis=0, keepdims=True)
        dvhat = dvn * lgv
        dv = rstd * (dvhat - _segmean(dvhat, avg_ref, 3) - vhat * _segmean(dvhat * vhat, avg_ref, 3))
        dzuv_ref[:, :D_A] = (doa_ref[...] * mixed * _gelu_grad(zu)).astype(BF16)
        dzuv_ref[:, D_A:] = (dv * _gelu_grad(zv)).astype(BF16)

        @pl.when(i == nt - 1)
        def _():
            dsgb_ref[...] = lax.dot_general(hsel_ref[...], dbacc[...], (((1,), (1,)), ((), ())),
                                            precision=lax.Precision.HIGHEST, preferred_element_type=F32)

    return pl.pallas_call(
        body, name="gmlp_bwd", grid=(nt,),
        in_specs=[_rows(tg, 2 * D_A), _rows(tg, D_A), _full((1, D_A)), _full((N_HEADS, SG_BLOCK, SG_BLOCK)),
                  _full((N_HEADS, SG_BLOCK, SG_BLOCK)), _full((SG_BLOCK, D_A)), _full((SG_BLOCK, SG_BLOCK)),
                  _full((N_HEADS, D_A)), _full((D_A, D_A))],
        out_specs=[_rows(tg, 2 * D_A), _full((N_HEADS, SG_BLOCK, SG_BLOCK)), _full((N_HEADS, SG_BLOCK)), _full((1, D_A))],
        out_shape=[jax.ShapeDtypeStruct((T, 2 * D_A), BF16), jax.ShapeDtypeStruct((N_HEADS, SG_BLOCK, SG_BLOCK), F32),
                   jax.ShapeDtypeStruct((N_HEADS, SG_BLOCK), F32), jax.ShapeDtypeStruct((1, D_A), F32)],
        scratch_shapes=[pltpu.VMEM((SG_BLOCK, D_A), F32)],
        compiler_params=_cp("arbitrary"),
    )(zuv, doa, lg, wm, wmt, bfull, maskf, hsel, avg)


def _fox_fwd(qa, ka, va, side):
    T = qa.shape[1]
    tq = min(TQ, T)
    nq = T // tq
    n = side.n

    def body(qa_ref, ka_ref, va_ref, *rest):
        ins, (o_ref, lse_ref), outs, sems = rest[:n], rest[n:n + 2], rest[n + 2:2 * n + 2], rest[2 * n + 2:]
        i = pl.program_id(1)
        if n:
            @pl.when((pl.program_id(0) == 0) & (i == 0))
            def _():
                side.start(ins, outs, sems)
        row = lax.broadcasted_iota(jnp.int32, (tq, tq), 0)
        col = lax.broadcasted_iota(jnp.int32, (tq, tq), 1)
        qs = [qa_ref[0], qa_ref[1]]

        def tiles(js, carry, diag):
            offs = [pl.multiple_of(j * tq, tq) for j in js]
            logits = [[_dot_nt(qs[hh], ka_ref[hh, pl.ds(off, tq), :]) for hh in range(2)] for off in offs]
            carry = list(carry)
            for off, per_head in zip(offs, logits):
                for hh, s in enumerate(per_head):
                    if diag:
                        s = jnp.where(col <= row, s, NEG)
                    m, acc = carry[hh]
                    m_new = jnp.maximum(m, jnp.max(s, axis=-1, keepdims=True))
                    pr = jnp.exp(s - m_new)
                    acc = jnp.exp(m - m_new) * acc + _dot(pr.astype(BF16), va_ref[hh, pl.ds(off, tq), :])
                    carry[hh] = (m_new, acc)
            return tuple(carry)

        init = ((jnp.full((tq, 1), NEG, F32), jnp.zeros((tq, LANES), F32)),) * 2
        carry = lax.fori_loop(0, i // FWD_UNROLL, lambda t, cr: tiles([FWD_UNROLL * t + u for u in range(FWD_UNROLL)], cr, False), init)
        carry = lax.fori_loop(i - i % FWD_UNROLL, i, lambda j, cr: tiles([j], cr, False), carry)
        carry = tiles([i], carry, True)
        for hh in range(2):
            m, acc = carry[hh]
            l = acc[:, L_ROW:L_ROW + 1]
            o_ref[:, _head_sl(hh)] = acc[:, :HEAD_DIM] / l
            hi, mid, lo = _split3(-(m + jnp.log(l)))
            lse_ref[hh] = _lanes(tq, LANES, {L_LSE: hi, L_LSE + 1: mid, L_LSE + 2: lo}).astype(BF16)
        if n:
            @pl.when((pl.program_id(0) == N_HEADS // 2 - 1) & (i == nq - 1))
            def _():
                side.finish(ins, outs, sems)

    tile = pl.BlockSpec((2, tq, LANES), lambda p, i: (p, i, 0))
    seq = pl.BlockSpec((2, T, LANES), lambda p, i: (p, 0, 0))
    res = pl.pallas_call(
        body, name="fox_fwd", grid=(N_HEADS // 2, nq),
        in_specs=[tile, seq, seq, *side.specs],
        out_specs=[pl.BlockSpec((tq, LANES), lambda p, i: (i, p)), tile, *side.specs],
        out_shape=[jax.ShapeDtypeStruct((T, D_B), F32), jax.ShapeDtypeStruct((N_HEADS, T, LANES), BF16), *side.out_shape],
        scratch_shapes=side.scratch,
        compiler_params=_cp("arbitrary", "arbitrary"),
    )(qa, ka, va, *side.operands)
    return res[0], res[1], res[2:]


def _fox_bwd(qa, lse, doa, ka, va, side):
    T = qa.shape[1]
    tq = min(TQ, T)
    nq = T // tq
    n = side.n

    def body(qa_ref, lse_ref, doa_ref, ka_ref, va_ref, *rest):
        ins, (dqa_ref, dka_ref, dva_ref), outs, sems = rest[:n], rest[n:n + 3], rest[n + 3:2 * n + 3], rest[2 * n + 3:]
        j = pl.program_id(1)
        if n:
            @pl.when((pl.program_id(0) == 0) & (j == 0))
            def _():
                side.start(ins, outs, sems)

        @pl.when(j == 0)
        def _():
            dqa_ref[...] = jnp.zeros_like(dqa_ref)

        row = lax.broadcasted_iota(jnp.int32, (tq, tq), 0)
        col = lax.broadcasted_iota(jnp.int32, (tq, tq), 1)
        ks = [ka_ref[0], ka_ref[1]]
        vs = [va_ref[0], va_ref[1]]

        def tiles(ids, carry, diag):
            work = []
            for i in ids:
                off = pl.multiple_of(i * tq, tq)
                for hh in range(2):
                    qi = qa_ref[hh, pl.ds(off, tq), :] + lse_ref[hh, pl.ds(off, tq), :]
                    doi = doa_ref[hh, pl.ds(off, tq), :]
                    work.append((off, hh, qi, doi, _dot_nt(ks[hh], qi), _dot_nt(vs[hh], doi)))
            carry = list(carry)
            for off, hh, qi, doi, st, dpt in work:
                dk, dv = carry[hh]
                if diag:
                    st = jnp.where(row <= col, st, NEG)
                pt = jnp.exp(st)
                dv = dv + _dot(pt.astype(BF16), doi)
                dsb = (pt * dpt).astype(BF16)
                dk = dk + _dot(dsb, qi)
                dqa_ref[hh, pl.ds(off, tq), :] += _dot_tn(dsb, ks[hh])
                carry[hh] = (dk, dv)
            return tuple(carry)

        init = ((jnp.zeros((tq, LANES), F32), jnp.zeros((tq, LANES), F32)),) * 2
        carry = tiles([j], init, True)
        todo = nq - 1 - j
        carry = lax.fori_loop(0, todo // BWD_UNROLL,
                              lambda t, cr: tiles([j + 1 + BWD_UNROLL * t + u for u in range(BWD_UNROLL)], cr, False), carry)
        carry = lax.fori_loop(nq - todo % BWD_UNROLL, nq, lambda i, cr: tiles([i], cr, False), carry)
        for hh in range(2):
            dka_ref[hh] = carry[hh][0]
            dva_ref[hh] = carry[hh][1].astype(BF16)
        if n:
            @pl.when((pl.program_id(0) == N_HEADS // 2 - 1) & (j == nq - 1))
            def _():
                side.finish(ins, outs, sems)

    tile = pl.BlockSpec((2, tq, LANES), lambda p, j: (p, j, 0))
    seq = pl.BlockSpec((2, T, LANES), lambda p, j: (p, 0, 0))
    res = pl.pallas_call(
        body, name="fox_bwd", grid=(N_HEADS // 2, nq),
        in_specs=[seq, seq, seq, tile, tile, *side.specs],
        out_specs=[seq, tile, tile, *side.specs],
        out_shape=[jax.ShapeDtypeStruct((N_HEADS, T, LANES), F32), jax.ShapeDtypeStruct((N_HEADS, T, LANES), F32),
                   jax.ShapeDtypeStruct((N_HEADS, T, LANES), BF16), *side.out_shape],
        scratch_shapes=side.scratch,
        compiler_params=_cp("arbitrary", "arbitrary"),
    )(qa, lse, doa, ka, va, *side.operands)
    return res[0], res[1], res[2], res[3:]


def _fwd_mid(x, oa, ob, w_out, g2, w_up):
    T = x.shape[0]
    tm = min(TM, T)

    def body(x_ref, oa_ref, ob_ref, wo_ref, g_ref, wu_ref, x2_ref, h2_ref, a_ref):
        oab = jnp.concatenate([oa_ref[...], ob_ref[...].astype(BF16)], axis=-1)
        x2 = x_ref[...] + _dot(oab, wo_ref[...])
        x2_ref[...] = x2
        _, n = _rms(x2)
        h2 = (n * g_ref[...]).astype(BF16)
        h2_ref[...] = h2
        a_ref[...] = _dot(h2, wu_ref[...])

    return pl.pallas_call(
        body, name="fwd_mid", grid=(T // tm,),
        in_specs=[_rows(tm, D_MODEL), _rows(tm, D_A), _rows(tm, D_B), _full((D_MODEL, D_MODEL), True), _full((1, D_MODEL)),
                  _full((D_MODEL, D_FF2), True)],
        out_specs=[_rows(tm, D_MODEL), _rows(tm, D_MODEL), _rows(tm, D_FF2)],
        out_shape=[jax.ShapeDtypeStruct((T, D_MODEL), F32), jax.ShapeDtypeStruct((T, D_MODEL), BF16),
                   jax.ShapeDtypeStruct((T, D_FF2), F32)],
        compiler_params=_cp("arbitrary"),
    )(x, oa, ob, w_out, g2, w_up)


def _row_before(x, prev, k):
    rolled = pltpu.roll(x, k, axis=0)
    row = lax.broadcasted_iota(jnp.int32, (8, x.shape[1]), 0)
    head = rolled[0:8]
    for r in range(k):
        head = jnp.where(row == r, prev[8 - k + r:9 - k + r], head)
    return jnp.concatenate([head, rolled[8:]], axis=0)


def _row_after(x, nxt, k):
    tm = x.shape[0]
    rolled = pltpu.roll(x, tm - k, axis=0)
    row = lax.broadcasted_iota(jnp.int32, (8, x.shape[1]), 0)
    tail = rolled[tm - 8:tm]
    for r in range(k):
        tail = jnp.where(row == 8 - k + r, nxt[r:r + 1], tail)
    return jnp.concatenate([rolled[:tm - 8], tail], axis=0)


def _fwd_ffn(a, x2, wc, bc, w_down, g3, tgt):
    T = x2.shape[0]
    tm = min(TM, T)

    def body(a_ref, x2_ref, wc_ref, bc_ref, wd_ref, g_ref, tgt_ref, ac_ref, yff_ref, dx3_ref, loss_ref, dg3_ref, carry):
        @pl.when(pl.program_id(0) == 0)
        def _():
            carry[...] = jnp.zeros_like(carry)
            loss_ref[...] = jnp.zeros_like(loss_ref)
            dg3_ref[...] = jnp.zeros_like(dg3_ref)

        def conv(cs):
            a0 = a_ref[:, cs]
            prev = carry[:, cs]
            ac = (wc_ref[0:1, cs] * _row_before(a0, prev, 2) + wc_ref[1:2, cs] * _row_before(a0, prev, 1)
                  + wc_ref[2:3, cs] * a0 + bc_ref[:, cs])
            ac_ref[:, cs] = ac.astype(BF16)
            return ac

        x3 = x2_ref[...]
        for ci in range(D_FF // CW):
            gs = slice(ci * CW, (ci + 1) * CW)
            ag = conv(gs)
            av = conv(slice(D_FF + ci * CW, D_FF + (ci + 1) * CW))
            yb = (ag * jax.nn.sigmoid(ag) * av).astype(BF16)
            yff_ref[:, gs] = yb
            x3 = x3 + _dot(yb, wd_ref[gs, :])
        carry[...] = a_ref[tm - 8:tm, :]
        r, n = _rms(x3)
        g = g_ref[...]
        diff = n * g - tgt_ref[...]
        loss_ref[...] += (0.5 / D_MODEL) * jnp.sum(diff * diff)
        dout = diff * (1.0 / D_MODEL)
        dg3_ref[...] += jnp.sum(dout * n, axis=0, keepdims=True)
        dx3_ref[...] = _rms_bwd(dout, n, r, g)

    return pl.pallas_call(
        body, name="fwd_ffn", grid=(T // tm,),
        in_specs=[_rows(tm, D_FF2), _rows(tm, D_MODEL), _full((3, D_FF2)), _full((1, D_FF2)), _full((D_FF, D_MODEL), True),
                  _full((1, D_MODEL)), _rows(tm, D_MODEL)],
        out_specs=[_rows(tm, D_FF2), _rows(tm, D_FF), _rows(tm, D_MODEL), _full((8, LANES)), _full((1, D_MODEL))],
        out_shape=[jax.ShapeDtypeStruct((T, D_FF2), BF16), jax.ShapeDtypeStruct((T, D_FF), BF16),
                   jax.ShapeDtypeStruct((T, D_MODEL), F32), jax.ShapeDtypeStruct((8, LANES), F32),
                   jax.ShapeDtypeStruct((1, D_MODEL), F32)],
        scratch_shapes=[pltpu.VMEM((8, D_FF2), F32)],
        compiler_params=_cp("arbitrary"),
    )(a, x2, wc, bc, w_down, g3, tgt)


def _bwd_ffn(dx3, a, ac, yff, h2, w_down, wc):
    T = dx3.shape[0]
    tm = min(TM, T)
    nt = T // tm
    half = D_FF // 2
    shard_up, shard_down = D_FF2 // N_DEV, D_FF // N_DEV

    def body(dx3_ref, ag_ref, av_ref, acg_ref, acv_ref, yff_ref, h2_ref, wd_ref, wcg_ref, wcv_ref,
             dag_ref, dav_ref, dwcg_ref, dwcv_ref, dbcg_ref, dbcv_ref, dwd_ref, dwu_ref,
             nxt, shifted, acc_down, acc_g, acc_v, stage_up, stage_down, sem):
        c, r = pl.program_id(0), pl.program_id(1)

        @pl.when(r == 0)
        def _():
            for ref in (nxt, dwcg_ref, dwcv_ref, dbcg_ref, dbcv_ref, acc_down, acc_g, acc_v):
                ref[...] = jnp.zeros_like(ref)

        dxb = dx3_ref[...].astype(BF16)
        dy_all = _dot_nt(dxb, wd_ref[...])

        def back(a_ref, w_ref, da_ref, dwc_ref, dbc_ref, nx, cs, dac):
            a0 = a_ref[:, cs]
            shifted[0] = _row_after(dac, nxt[:, nx], 1)
            shifted[1] = _row_after(dac, nxt[:, nx], 2)
            dp1, dp2 = shifted[0], shifted[1]
            dbc_ref[:, cs] += jnp.sum(dac, axis=0, keepdims=True)
            dwc_ref[0:1, cs] += jnp.sum(dp2 * a0, axis=0, keepdims=True)
            dwc_ref[1:2, cs] += jnp.sum(dp1 * a0, axis=0, keepdims=True)
            dwc_ref[2:3, cs] += jnp.sum(dac * a0, axis=0, keepdims=True)
            da_ref[:, cs] = (w_ref[2:3, cs] * dac + w_ref[1:2, cs] * dp1 + w_ref[0:1, cs] * dp2).astype(BF16)
            nxt[:, nx] = dac[0:8]

        for ci in range(half // LANES):
            cs = slice(ci * LANES, (ci + 1) * LANES)
            dy = dy_all[:, cs]
            ag, av = acg_ref[:, cs].astype(F32), acv_ref[:, cs].astype(F32)
            sg = jax.nn.sigmoid(ag)
            back(av_ref, wcv_ref, dav_ref, dwcv_ref, dbcv_ref, slice(half + ci * LANES, half + (ci + 1) * LANES), cs,
                 dy * (ag * sg))
            back(ag_ref, wcg_ref, dag_ref, dwcg_ref, dbcg_ref, cs, cs, dy * av * (sg * (1.0 + ag * (1.0 - sg))))

        acc_down[...] += _dot_tn(yff_ref[...], dxb)
        h2 = h2_ref[...]
        acc_g[...] += _dot_tn(h2, dag_ref[...])
        acc_v[...] += _dot_tn(h2, dav_ref[...])

        @pl.when(r == nt - 1)
        def _():
            for s in range(half // shard_down):
                stage_down[...] = acc_down[s * shard_down:(s + 1) * shard_down, :].astype(BF16)
                out = pltpu.make_async_copy(stage_down, dwd_ref.at[(half // shard_down) * c + s], sem)
                out.start()
                out.wait()
            for acc, first in ((acc_g, 0), (acc_v, N_DEV // 2)):
                for s in range(half // shard_up):
                    stage_up[...] = acc[:, s * shard_up:(s + 1) * shard_up].astype(BF16)
                    out = pltpu.make_async_copy(stage_up, dwu_ref.at[first + (half // shard_up) * c + s], sem)
                    out.start()
                    out.wait()

    def cols(width, second_half):
        return pl.BlockSpec((tm, width), lambda c, r: (nt - 1 - r, c + (2 if second_half else 0)))

    def param(rows, second_half):
        return pl.BlockSpec((rows, half), lambda c, r: (0, c + (2 if second_half else 0)))

    tokens = pl.BlockSpec((tm, D_MODEL), lambda c, r: (nt - 1 - r, 0))
    return pl.pallas_call(
        body, name="bwd_ffn", grid=(2, nt),
        in_specs=[tokens, cols(half, False), cols(half, True), cols(half, False), cols(half, True), cols(half, False), tokens,
                  pl.BlockSpec((half, D_MODEL), lambda c, r: (c, 0), pipeline_mode=pl.Buffered(1)),
                  param(3, False), param(3, True)],
        out_specs=[cols(half, False), cols(half, False), param(3, False), param(3, False), param(1, False), param(1, False),
                   ANY, ANY],
        out_shape=[jax.ShapeDtypeStruct((T, D_FF), BF16), jax.ShapeDtypeStruct((T, D_FF), BF16),
                   jax.ShapeDtypeStruct((3, D_FF), F32), jax.ShapeDtypeStruct((3, D_FF), F32),
                   jax.ShapeDtypeStruct((1, D_FF), F32), jax.ShapeDtypeStruct((1, D_FF), F32),
                   jax.ShapeDtypeStruct((N_DEV, shard_down, D_MODEL), BF16), jax.ShapeDtypeStruct((N_DEV, D_MODEL, shard_up), BF16)],
        scratch_shapes=[pltpu.VMEM((8, D_FF), F32), pltpu.VMEM((2, tm, LANES), F32), pltpu.VMEM((half, D_MODEL), F32),
                        pltpu.VMEM((D_MODEL, half), F32), pltpu.VMEM((D_MODEL, half), F32),
                        pltpu.VMEM((D_MODEL, shard_up), BF16), pltpu.VMEM((shard_down, D_MODEL), BF16),
                        pltpu.SemaphoreType.DMA],
        compiler_params=_cp("arbitrary", "arbitrary"),
    )(dx3, a, a, ac, ac, yff, h2, w_down, wc, wc)


def _bwd_mid(da_g, da_v, w_up, x2, g2, dx3, w_out, ob):
    T = x2.shape[0]
    tm = min(TM, T)

    def body(dag_ref, dav_ref, wu_ref, x2_ref, g_ref, dx3_ref, wo_ref, ob_ref, hsum_ref, place_ref,
             dx2_ref, doa_ref, dob_ref, dg2_ref):
        @pl.when(pl.program_id(0) == 0)
        def _():
            dg2_ref[...] = jnp.zeros_like(dg2_ref)

        dh2 = _dot_nt(dag_ref[...], wu_ref[:, :D_FF]) + _dot_nt(dav_ref[...], wu_ref[:, D_FF:])
        r, n = _rms(x2_ref[...])
        dg2_ref[...] += jnp.sum(dh2 * n, axis=0, keepdims=True)
        dx2 = dx3_ref[...] + _rms_bwd(dh2, n, r, g_ref[...])
        dx2_ref[...] = dx2
        doab = _dot_nt(dx2.astype(BF16), wo_ref[...])
        doa_ref[...] = doab[:, :D_A]
        dob = doab[:, D_A:]
        rest, delta = dob * ob_ref[...], None
        for _ in range(3):
            piece = rest.astype(BF16)
            term = _dot(piece, hsum_ref[...])
            delta = term if delta is None else delta + term
            rest = rest - piece.astype(F32)
        hi, mid, lo = _split3(-delta)
        parts = jnp.concatenate([hi.astype(BF16), mid.astype(BF16), lo.astype(BF16)], axis=-1)
        placed = _dot(parts, place_ref[...])
        data = lax.broadcasted_iota(jnp.int32, (tm, LANES), 1) < HEAD_DIM
        for hd in range(N_HEADS):
            tile = dob[:, LANES * (hd // 2):LANES * (hd // 2 + 1)]
            rows = tile if hd % 2 == 0 else pltpu.roll(tile, HEAD_DIM, axis=1)
            dob_ref[hd] = jnp.where(data, rows, placed[:, LANES * hd:LANES * (hd + 1)]).astype(BF16)

    hsum = (jnp.arange(D_B)[:, None] // HEAD_DIM == jnp.arange(LANES)[None, :]).astype(BF16)
    j, h = jnp.arange(3 * LANES) // LANES, jnp.arange(3 * LANES) % LANES
    place = ((h[:, None] < N_HEADS)
             & (jnp.arange(N_HEADS * LANES)[None, :] == LANES * h[:, None] + L_ROW + j[:, None])).astype(BF16)
    return pl.pallas_call(
        body, name="bwd_mid", grid=(T // tm,),
        in_specs=[_rows(tm, D_FF), _rows(tm, D_FF), _full((D_MODEL, D_FF2), True), _rows(tm, D_MODEL), _full((1, D_MODEL)),
                  _rows(tm, D_MODEL), _full((D_MODEL, D_MODEL), True), _rows(tm, D_B), _full((D_B, LANES)),
                  _full((3 * LANES, N_HEADS * LANES))],
        out_specs=[_rows(tm, D_MODEL), _rows(tm, D_A), pl.BlockSpec((N_HEADS, tm, LANES), lambda i: (0, i, 0)),
                   _full((1, D_MODEL))],
        out_shape=[jax.ShapeDtypeStruct((T, D_MODEL), F32), jax.ShapeDtypeStruct((T, D_A), F32),
                   jax.ShapeDtypeStruct((N_HEADS, T, LANES), BF16), jax.ShapeDtypeStruct((1, D_MODEL), F32)],
        compiler_params=_cp("arbitrary"),
    )(da_g, da_v, w_up, x2, g2, dx3, w_out, ob, hsum, place)


def _bwd_in(dzuv, dqa, dka, dva, fl, x, dx2, w_in_p, g1):
    T = x.shape[0]
    tm = min(TM, T)
    nt = T // tm

    def body(dzuv_ref, dqa_ref, dka_ref, dva_ref, fl_ref, x_ref, dx2_ref, w_ref, g_ref,
             gx_ref, dz_ref, dg1_ref, dfb_ref, carry):
        @pl.when(pl.program_id(0) == 0)
        def _():
            carry[...] = jnp.zeros_like(carry)
            dg1_ref[...] = jnp.zeros_like(dg1_ref)
            dfb_ref[...] = jnp.zeros_like(dfb_ref)

        dc = _lanes(tm, LANES, {hd: dqa_ref[hd][:, L_ROW:L_ROW + 1] - dka_ref[hd][:, L_COL:L_COL + 1] for hd in range(N_HEADS)})
        later = (lax.broadcasted_iota(jnp.int32, (tm, tm), 1) >= lax.broadcasted_iota(jnp.int32, (tm, tm), 0)).astype(F32)
        dls = _dot_f32(later, dc) + carry[...]
        carry[...] = dls[0:1, :]
        dzf = dls * jax.nn.sigmoid(-fl_ref[...])
        dfb_ref[...] += jnp.sum(dzf, axis=0, keepdims=True)
        data = lax.broadcasted_iota(jnp.int32, (tm, LANES), 1) < HEAD_DIM

        def compact(ref):
            return [jnp.where(data, ref[2 * p], pltpu.roll(ref[2 * p + 1], HEAD_DIM, axis=1)) for p in range(N_HEADS // 2)]

        dz = jnp.concatenate([dzuv_ref[...], *[(t * (HEAD_DIM ** -0.5)).astype(BF16) for t in compact(dqa_ref)],
                              *[t.astype(BF16) for t in compact(dka_ref)], *compact(dva_ref), dzf.astype(BF16)], axis=-1)
        dz_ref[...] = dz
        dh1 = _dot_nt(dz, w_ref[...])
        r, n = _rms(x_ref[...])
        dg1_ref[...] += jnp.sum(dh1 * n, axis=0, keepdims=True)
        gx_ref[...] = dx2_ref[...] + _rms_bwd(dh1, n, r, g_ref[...])

    rv = functools.partial(_rows, tm, rev_nt=nt)
    heads = pl.BlockSpec((N_HEADS, tm, LANES), lambda i: (0, nt - 1 - i, 0))
    return pl.pallas_call(
        body, name="bwd_in", grid=(nt,),
        in_specs=[rv(2 * D_A), heads, heads, heads, rv(LANES), rv(D_MODEL), rv(D_MODEL),
                  _full((D_MODEL, D_IN_PAD), True), _full((1, D_MODEL))],
        out_specs=[rv(D_MODEL), rv(D_IN_PAD), _full((1, D_MODEL)), _full((1, LANES))],
        out_shape=[jax.ShapeDtypeStruct((T, D_MODEL), F32), jax.ShapeDtypeStruct((T, D_IN_PAD), BF16),
                   jax.ShapeDtypeStruct((1, D_MODEL), F32), jax.ShapeDtypeStruct((1, LANES), F32)],
        scratch_shapes=[pltpu.VMEM((1, LANES), F32)],
        compiler_params=_cp("arbitrary"),
    )(dzuv, dqa, dka, dva, fl, x, dx2, w_in_p, g1)


def _matmul_tn(a_parts, b, tmm, tn, tk, name, shard_cols=None, n_valid=None):
    T = b.shape[0]
    widths = [a.shape[1] for a in a_parts]
    M, N = sum(widths), b.shape[1]
    tk = min(tk, T)
    nk = T // tk
    part_w = tmm // len(a_parts)
    n_valid = N if n_valid is None else n_valid

    def body(*refs):
        a_refs, b_ref, o_ref, obf_ref = refs[:len(a_parts)], refs[-3], refs[-2], refs[-1]
        k = pl.program_id(2)

        @pl.when(k == 0)
        def _():
            o_ref[...] = jnp.zeros_like(o_ref)

        a = [r[...].astype(BF16) for r in a_refs]
        o_ref[...] += _dot_tn(a[0] if len(a) == 1 else jnp.concatenate(a, axis=-1), b_ref[...].astype(BF16))

        @pl.when(k == nk - 1)
        def _():
            if shard_cols is None:
                obf_ref[...] = o_ref[...].astype(BF16)
            else:
                for d in range(min(tn, n_valid) // shard_cols):
                    obf_ref[d] = o_ref[:, d * shard_cols:(d + 1) * shard_cols].astype(BF16)

    if shard_cols is None:
        bf_spec, bf_shape = pl.BlockSpec((tmm, tn), lambda i, j, k: (i, j)), (M, N)
    else:
        per_tile = min(tn, n_valid) // shard_cols
        bf_spec, bf_shape = pl.BlockSpec((per_tile, tmm, shard_cols), lambda i, j, k: (j, i, 0)), (N_DEV, M, shard_cols)
    a_specs = [pl.BlockSpec((tk, part_w), lambda i, j, k: (k, i)) for _ in a_parts]
    return pl.pallas_call(
        body, name=name, grid=(M // tmm, N // tn, nk),
        in_specs=[*a_specs, pl.BlockSpec((tk, tn), lambda i, j, k: (k, j))],
        out_specs=[pl.BlockSpec((tmm, tn), lambda i, j, k: (i, j)), bf_spec],
        out_shape=[jax.ShapeDtypeStruct((M, N), F32), jax.ShapeDtypeStruct(bf_shape, BF16)],
        compiler_params=_cp("arbitrary", "arbitrary", "arbitrary"),
    )(*a_parts, b)


class _Exchange:
    def __init__(self, gather, scatter, relay):
        self.n_g, self.n, self.relay = len(gather), len(gather) + len(scatter), relay
        self.operands = [*gather, *scatter]
        self.out_shape = [jax.ShapeDtypeStruct((N_DEV, *g.shape), g.dtype) for g in gather]
        self.out_shape += [jax.ShapeDtypeStruct(s.shape, s.dtype) for s in scatter]
        self.specs = [ANY] * self.n
        n = self.n
        self.scratch = [pltpu.SemaphoreType.DMA((7 * n,)), pltpu.SemaphoreType.DMA((7 * n,)),
                        pltpu.SemaphoreType.DMA((n,))] if n else []

    def _plan(self, ins, outs, sems):
        send_sems, recv_sems, local_sems = sems
        x, y, c = (lax.axis_index(ax) for ax in MESH_AXES)
        me = 4 * x + 2 * y + c
        sibling = (x, y, 1 - c)
        chips = [(1 - x, y), (x, 1 - y), (1 - x, 1 - y)]
        peers = [sibling] + [(*chip, c) for chip in chips] + [(*chip, 1 - c) for chip in chips]

        def index(dev):
            return 4 * dev[0] + 2 * dev[1] + dev[2]

        def remote(k, src, dst, to):
            return pltpu.make_async_remote_copy(src_ref=src, dst_ref=dst, send_sem=send_sems.at[k], recv_sem=recv_sems.at[k],
                                                device_id=to, device_id_type=pl.DeviceIdType.MESH)

        local, sends, relays, recvs = [], [], [], []
        for a in range(self.n):
            src, out, base = ins[a], outs[a], 7 * a
            if a >= self.n_g:
                local.append(pltpu.make_async_copy(src.at[me], out.at[me], local_sems.at[a]))
                sends += [remote(base + k, src.at[index(peer)], out.at[me], peer) for k, peer in enumerate(peers)]
            else:
                local.append(pltpu.make_async_copy(src, out.at[me], local_sems.at[a]))
                sends += [remote(base + k, src, out.at[me], peer) for k, peer in enumerate(peers[:4 if self.relay else 7])]
            for k, peer in enumerate(peers):
                slot = out.at[index(peer)]
                if a < self.n_g and self.relay and k >= 4:
                    continue
                recv = remote(base + k, slot, slot, peer)
                if a < self.n_g and self.relay and k >= 1:
                    relays.append((recv, remote(base + 3 + k, slot, slot, sibling)))
                else:
                    recvs.append(recv)
            if a < self.n_g and self.relay:
                for j, chip in enumerate(chips):
                    slot = out.at[index((*chip, 1 - c))]
                    recvs.append(remote(base + 4 + j, slot, slot, sibling))
        return local, sends, relays, recvs

    def start(self, ins, outs, sems):
        local, sends, _, _ = self._plan(ins, outs, sems)
        for cp in local + sends:
            cp.start()

    def finish(self, ins, outs, sems):
        local, sends, relays, recvs = self._plan(ins, outs, sems)
        for recv, fwd in relays:
            recv.wait_recv()
            fwd.start()
        for recv in recvs:
            recv.wait_recv()
        for cp in sends + [fwd for _, fwd in relays]:
            cp.wait_send()
        for cp in local:
            cp.wait()


def _exchange(gather, scatter, name):
    ex = _Exchange(gather, scatter, relay=True)
    n = ex.n

    def body(*refs):
        ins, outs, sems = refs[:n], refs[n:2 * n], refs[2 * n:]
        ex.start(ins, outs, sems)
        ex.finish(ins, outs, sems)

    return pl.pallas_call(body, name=name, in_specs=ex.specs, out_specs=ex.specs, out_shape=ex.out_shape,
                          scratch_shapes=ex.scratch)(*ex.operands)


def _adamw(w, g, m, v):
    m = ADAM_B1 * m + (1.0 - ADAM_B1) * g
    v = ADAM_B2 * v + (1.0 - ADAM_B2) * jnp.square(g)
    m_hat = m / (1.0 - ADAM_B1 ** ADAM_STEP)
    v_hat = v / (1.0 - ADAM_B2 ** ADAM_STEP)
    delta = -ADAM_LR * (m_hat / (jnp.sqrt(v_hat) + ADAM_EPS) + ADAM_WD * w)
    return delta, m, v


def _adamw_shard(w, m, v, recv, tr, name):
    _, R, C = w.shape

    def body(w_ref, m_ref, v_ref, recv_ref, g_ref, d_ref, nm_ref, nv_ref):
        g = recv_ref[0].astype(F32)
        for d in range(1, N_DEV):
            g = g + recv_ref[d].astype(F32)
        g_ref[...] = g
        d_ref[...], nm_ref[...], nv_ref[...] = _adamw(w_ref[...], g, m_ref[...], v_ref[...])

    blk = pl.BlockSpec((None, tr, C), lambda i: (0, i, 0))
    return pl.pallas_call(
        body, name=name, grid=(R // tr,),
        in_specs=[blk, blk, blk, pl.BlockSpec((N_DEV, tr, C), lambda i: (0, i, 0))],
        out_specs=[blk] * 4, out_shape=[jax.ShapeDtypeStruct((1, R, C), F32)] * 4,
        compiler_params=_cp("arbitrary"),
    )(w, m, v, recv)


def _adamw_small(params, gathered, loss_parts):
    n = len(params)

    def body(*refs):
        ins, gs, loss_ref, outs = refs[:3 * n], refs[3 * n:4 * n], refs[4 * n], refs[4 * n + 1:]
        for p in range(n):
            w_ref, m_ref, v_ref = ins[3 * p:3 * p + 3]
            g = gs[p][0]
            for d in range(1, N_DEV):
                g = g + gs[p][d]
            g = g[..., :w_ref.shape[-1]]
            g_ref, d_ref, nm_ref, nv_ref = outs[4 * p:4 * p + 4]
            g_ref[...] = g
            d_ref[...], nm_ref[...], nv_ref[...] = _adamw(w_ref[...], g, m_ref[...], v_ref[...])
        total = loss_ref[0]
        for d in range(1, N_DEV):
            total = total + loss_ref[d]
        outs[4 * n][...] = total

    out_shape = [jax.ShapeDtypeStruct(w.shape, F32) for w, _, _ in params for _ in range(4)]
    res = pl.pallas_call(body, name="adamw_small", out_shape=[*out_shape, jax.ShapeDtypeStruct((8, LANES), F32)],
                         compiler_params=pltpu.CompilerParams(vmem_limit_bytes=VMEM_LIMIT))(
        *[t for p in params for t in p], *gathered, loss_parts)
    return [res[4 * p:4 * p + 4] for p in range(n)], res[4 * n][0, 0]


def _col_shards(g):
    return jnp.transpose(g.reshape(g.shape[0], N_DEV, -1), (1, 0, 2))


def _row_shards(g):
    return g.reshape(N_DEV, -1, g.shape[1])


def _cols_whole(g):
    return jnp.transpose(g, (1, 0, 2)).reshape(g.shape[1], -1)


def kernel(x, norm_mix_g, w_in, f_bias, sg_ln_g, sg_w, sg_b, w_out, norm_ffn_g, w_up, w_conv, b_conv, w_down, norm_final_g, loss_target, m_norm_mix_g, m_w_in, m_f_bias, m_sg_ln_g, m_sg_w, m_sg_b, m_w_out, m_norm_ffn_g, m_w_up, m_w_conv, m_b_conv, m_w_down, m_norm_final_g, v_norm_mix_g, v_w_in, v_f_bias, v_sg_ln_g, v_sg_w, v_sg_b, v_w_out, v_norm_ffn_g, v_w_up, v_w_conv, v_b_conv, v_w_down, v_norm_final_g):
    xs, tgt = x[0], loss_target[0]
    g1, g2, g3 = norm_mix_g, norm_ffn_g, norm_final_g.reshape(1, D_MODEL)
    lg = sg_ln_g.reshape(1, D_A)
    fb = jnp.pad(f_bias, ((0, 0), (0, LANES - N_HEADS)))
    pos_chunk = jnp.arange(SG_BLOCK) // SG_CHUNK
    maskf = (pos_chunk[:, None] >= pos_chunk[None, :]).astype(F32)
    wm = (sg_w[0] * maskf[None]).astype(BF16)
    wmt = jnp.swapaxes(wm, 1, 2)
    bfull = jnp.repeat(sg_b[0].T, HEAD_DIM, axis=1)
    hsel = jnp.repeat(jnp.eye(N_HEADS, dtype=F32), HEAD_DIM, axis=1)
    avg = (jnp.repeat(hsel, HEAD_DIM, axis=0) * (1.0 / HEAD_DIM)).astype(BF16)

    (win_g,) = _exchange([w_in[0].astype(BF16)], [], "gather_w_in")
    w_in_p = jnp.pad(_cols_whole(win_g), ((0, 0), (0, D_IN_PAD - D_IN)))
    zuv, qa, ka, va, fl, h1 = _fwd_in(xs, g1, w_in_p, fb)
    oa = _gmlp_fwd(zuv, lg, wm, bfull, avg)
    rest = _Exchange([w_out[0].astype(BF16), w_up[0].astype(BF16), w_down[0].astype(BF16), w_conv[0]], [], relay=False)
    ob, lse, (wout_g, wup_g, wdown_g, wc_g) = _fox_fwd(qa, ka, va, rest)
    w_out_f, w_up_f = wout_g.reshape(D_MODEL, D_MODEL), _cols_whole(wup_g)
    w_down_f, wc_f = wdown_g.reshape(D_FF, D_MODEL), _cols_whole(wc_g)

    x2, h2, a = _fwd_mid(xs, oa, ob, w_out_f, g2, w_up_f)
    ac, yff, dx3, loss, dg3 = _fwd_ffn(a, x2, wc_f, b_conv, w_down_f, g3, tgt)
    da_g, da_v, dwc_g, dwc_v, dbc_g, dbc_v, dwdown_bf, dwup_bf = _bwd_ffn(dx3, a, ac, yff, h2, w_down_f, wc_f)
    dwc, dbc = jnp.concatenate([dwc_g, dwc_v], axis=1), jnp.concatenate([dbc_g, dbc_v], axis=1)
    dx2, doa, dob, dg2 = _bwd_mid(da_g, da_v, w_up_f, x2, g2, dx3, w_out_f, ob)
    dzuv, dwm, dsgb, dlg = _gmlp_bwd(zuv, doa, lg, wm, wmt, bfull, maskf, hsel, avg)
    _, dwout_bf = _matmul_tn([oa, ob], dx2, D_MODEL, D_MODEL, 1024, "dw_out")

    early = ("w_out", "w_up", "wc", "w_down")
    wire = [_row_shards(dwout_bf), dwup_bf, _col_shards(dwc).astype(BF16), dwdown_bf]
    small_early = dict(lg=dlg, sg_w=dwm, sg_b=dsgb, g2=dg2, bc=dbc, g3=dg3)
    grads = _Exchange([*small_early.values(), loss], wire, relay=False)
    dqa, dka, dva, got = _fox_bwd(qa, lse, dob, ka, va, grads)
    n_small = len(small_early)
    gathered, loss_parts = dict(zip(small_early, got[:n_small])), got[n_small]
    recv = dict(zip(early, got[n_small + 1:]))

    gx, dz, dg1, dfb = _bwd_in(dzuv, dqa, dka, dva, fl, xs, dx2, w_in_p, g1)
    _, dwin_bf = _matmul_tn([h1], dz, D_MODEL // 2, D_IN_PAD, 1024, "dw_in", shard_cols=D_IN // N_DEV, n_valid=D_IN)
    gathered["g1"], gathered["fb"], recv["w_in"] = _exchange([dg1, dfb], [dwin_bf], "exchange_w_in")

    weights = dict(w_in=(w_in, m_w_in, v_w_in, 256), w_out=(w_out, m_w_out, v_w_out, 128), w_up=(w_up, m_w_up, v_w_up, 256),
                   wc=(w_conv, m_w_conv, v_w_conv, 3), w_down=(w_down, m_w_down, v_w_down, 176))
    res = {n: _adamw_shard(w, m, v, recv[n], tr, "adamw_" + n) for n, (w, m, v, tr) in weights.items()}

    reps = dict(g1=((norm_mix_g, m_norm_mix_g, v_norm_mix_g), (1, D_MODEL)), fb=((f_bias, m_f_bias, v_f_bias), (1, N_HEADS)),
                lg=((sg_ln_g, m_sg_ln_g, v_sg_ln_g), (1, D_A)), sg_w=((sg_w, m_sg_w, v_sg_w), (N_HEADS, SG_BLOCK, SG_BLOCK)),
                sg_b=((sg_b, m_sg_b, v_sg_b), (N_HEADS, SG_BLOCK)), g2=((norm_ffn_g, m_norm_ffn_g, v_norm_ffn_g), (1, D_MODEL)),
                bc=((b_conv, m_b_conv, v_b_conv), (1, D_FF2)), g3=((norm_final_g, m_norm_final_g, v_norm_final_g), (1, D_MODEL)))
    outs, loss_sum = _adamw_small([tuple(t.reshape(shape) for t in wmv) for wmv, shape in reps.values()],
                                  [gathered[n] for n in reps], loss_parts)
    for (n, (wmv, _)), out in zip(reps.items(), outs):
        res[n] = [o.reshape(wmv[0].shape) for o in out]

    names = ("g1", "w_in", "fb", "lg", "sg_w", "sg_b", "w_out", "g2", "w_up", "wc", "bc", "w_down", "g3")
    return (loss_sum, gx[None], *[res[n][0] for n in names], *[res[n][1] for n in names],
            *[res[n][2] for n in names], *[res[n][3] for n in names])
```

```python
import functools
import math

import jax
import jax.numpy as jnp
from jax import lax
from jax.experimental import pallas as pl
from jax.experimental.pallas import tpu as pltpu

F32 = jnp.float32
BF16 = jnp.bfloat16

D_MODEL = 1024
HEAD_DIM = 64
N_HEADS = 8
D_A = 512
D_B = 512
D_IN = 2 * D_A + 3 * D_B + N_HEADS
D_IN_PAD = 2688
D_FF = 2816
D_FF2 = 2 * D_FF
SG_BLOCK = 128
SG_CHUNK = 64
EPS = 1e-6
N_DEV = 8
LANES = 128
NEG = -1e30
VMEM_LIMIT = 56 * 1024 * 1024

ADAM_LR = 0.001
ADAM_B1 = 0.9
ADAM_B2 = 0.999
ADAM_EPS = 1e-08
ADAM_WD = 0.01
ADAM_STEP = 10

TM = 256
TQ = 512
FWD_UNROLL = 4
BWD_UNROLL = 2
CW = 256

MESH_AXES = ("x", "y", "c")
ANY = pl.BlockSpec(memory_space=pl.ANY)


def _cp(*sem):
    return pltpu.CompilerParams(dimension_semantics=sem, vmem_limit_bytes=VMEM_LIMIT)


def _dot(a, b):
    return jnp.dot(a, b, preferred_element_type=F32)


def _dot_nt(a, b):
    return lax.dot_general(a, b, (((1,), (1,)), ((), ())), preferred_element_type=F32)


def _dot_tn(a, b):
    return lax.dot_general(a, b, (((0,), (0,)), ((), ())), preferred_element_type=F32)


def _dot_f32(a, b):
    return jnp.dot(a, b, precision=lax.Precision.HIGHEST, preferred_element_type=F32)


def _gelu(z):
    return 0.5 * z * (1.0 + lax.erf(z * (1.0 / math.sqrt(2.0))))


def _gelu_grad(z):
    return 0.5 * (1.0 + lax.erf(z * (1.0 / math.sqrt(2.0)))) + z * jnp.exp(-0.5 * z * z) * (1.0 / math.sqrt(2.0 * math.pi))


def _log_sigmoid(x):
    return jnp.minimum(x, 0.0) - jnp.log1p(jnp.exp(-jnp.abs(x)))


def _rms(x):
    r = lax.rsqrt(jnp.mean(x * x, axis=-1, keepdims=True) + EPS)
    return r, x * r


def _rms_bwd(dy, n, r, g):
    dn = dy * g
    return r * (dn - n * jnp.mean(dn * n, axis=-1, keepdims=True))


def _full(shape, single=False):
    nd = len(shape)
    if single:
        return pl.BlockSpec(shape, lambda *_: (0,) * nd, pipeline_mode=pl.Buffered(1))
    return pl.BlockSpec(shape, lambda *_: (0,) * nd)


def _rows(tm, cols, rev_nt=None):
    if rev_nt is None:
        return pl.BlockSpec((tm, cols), lambda i: (i, 0))
    return pl.BlockSpec((tm, cols), lambda i: (rev_nt - 1 - i, 0))


def _head_sl(h):
    return slice(HEAD_DIM * h, HEAD_DIM * (h + 1))


L_ROW = HEAD_DIM
L_COL = HEAD_DIM + 3
L_LSE = HEAD_DIM + 6


def _split3(x):
    hi = x.astype(BF16).astype(F32)
    mid = (x - hi).astype(BF16).astype(F32)
    lo = (x - hi - mid).astype(BF16).astype(F32)
    return hi, mid, lo


def _lanes(rows, width, parts):
    lane = lax.broadcasted_iota(jnp.int32, (rows, width), 1)
    out = jnp.zeros((rows, width), F32)
    for at, val in parts.items():
        out = jnp.where(lane == at, val, out)
    return out


def _fwd_in(x, g1, w_in_p, fb):
    T = x.shape[0]
    tm = min(TM, T)

    def body(x_ref, g_ref, w_ref, fb_ref, place_ref, zuv_ref, qa_ref, ka_ref, va_ref, fl_ref, h1_ref, carry):
        @pl.when(pl.program_id(0) == 0)
        def _():
            carry[...] = jnp.zeros_like(carry)

        r, n = _rms(x_ref[...])
        h = (n * g_ref[...]).astype(BF16)
        h1_ref[...] = h
        z = _dot(h, w_ref[...])
        zuv_ref[...] = z[:, :2 * D_A]
        o = 2 * D_A
        fl = z[:, o + 3 * D_B:] + fb_ref[...]
        fl_ref[...] = fl
        tri = (lax.broadcasted_iota(jnp.int32, (tm, tm), 0) >= lax.broadcasted_iota(jnp.int32, (tm, tm), 1)).astype(F32)
        c = _dot_f32(tri, _log_sigmoid(fl)) + carry[...]
        carry[...] = c[tm - 1:tm, :]
        hi, mid, lo = _split3(c)
        parts = jnp.concatenate([hi.astype(BF16), mid.astype(BF16), lo.astype(BF16)], axis=-1)
        placed = _dot(parts, place_ref[...])
        lane = lax.broadcasted_iota(jnp.int32, (tm, LANES), 1)
        data = lane < HEAD_DIM
        ones_q = ((lane >= L_COL) & (lane < L_COL + 3)).astype(F32)
        ones_k = (((lane >= L_ROW) & (lane < L_ROW + 3)) | ((lane >= L_LSE) & (lane < L_LSE + 3))).astype(F32)
        ones_v = ((lane >= L_ROW) & (lane < L_ROW + 3)).astype(F32)
        for hd in range(N_HEADS):
            def rows_of(first_col):
                tile = z[:, first_col + LANES * (hd // 2):first_col + LANES * (hd // 2 + 1)]
                return tile if hd % 2 == 0 else pltpu.roll(tile, HEAD_DIM, axis=1)

            hs = slice(LANES * hd, LANES * (hd + 1))
            qa_ref[hd] = jnp.where(data, rows_of(o) * (HEAD_DIM ** -0.5), placed[:, hs] + ones_q).astype(BF16)
            ka_ref[hd] = jnp.where(data, rows_of(o + D_B), ones_k - placed[:, N_HEADS * LANES:][:, hs]).astype(BF16)
            va_ref[hd] = jnp.where(data, rows_of(o + 2 * D_B), ones_v).astype(BF16)

    heads = pl.BlockSpec((N_HEADS, tm, LANES), lambda i: (0, i, 0))
    aug = jax.ShapeDtypeStruct((N_HEADS, T, LANES), BF16)
    j, h = jnp.arange(3 * LANES) // LANES, jnp.arange(3 * LANES) % LANES
    cols = jnp.arange(2 * N_HEADS * LANES)
    place = ((h[:, None] < N_HEADS) & ((cols[None, :] == LANES * h[:, None] + L_ROW + j[:, None])
                                       | (cols[None, :] == N_HEADS * LANES + LANES * h[:, None] + L_COL + j[:, None]))).astype(BF16)
    return pl.pallas_call(
        body, name="fwd_in", grid=(T // tm,),
        in_specs=[_rows(tm, D_MODEL), _full((1, D_MODEL)), _full((D_MODEL, D_IN_PAD), True), _full((1, LANES)),
                  _full((3 * LANES, 2 * N_HEADS * LANES))],
        out_specs=[_rows(tm, 2 * D_A), heads, heads, heads, _rows(tm, LANES), _rows(tm, D_MODEL)],
        out_shape=[jax.ShapeDtypeStruct((T, 2 * D_A), F32), aug, aug, aug, jax.ShapeDtypeStruct((T, LANES), F32),
                   jax.ShapeDtypeStruct((T, D_MODEL), BF16)],
        scratch_shapes=[pltpu.VMEM((1, LANES), F32)],
        compiler_params=_cp("arbitrary"),
    )(x, g1, w_in_p, fb, place)


def _segmean(x, avg_ref, parts):
    out, rest = None, x
    for _ in range(parts):
        piece = rest.astype(BF16)
        term = _dot(piece, avg_ref[...])
        out = term if out is None else out + term
        rest = rest - piece.astype(F32)
    return out


def _layer_norm(v, lg, avg_ref):
    d = v - _segmean(v, avg_ref, 3)
    rstd = lax.rsqrt(_segmean(d * d, avg_ref, 2) + EPS)
    vhat = d * rstd
    return vhat, rstd, (vhat * lg).astype(BF16)


def _mix_heads(w_ref, x, row_slice):
    low = lax.broadcasted_iota(jnp.int32, (SG_BLOCK, LANES), 1) < HEAD_DIM
    tiles = []
    for p in range(N_HEADS // 2):
        xt = x[row_slice, LANES * p:LANES * (p + 1)]
        zero = jnp.zeros_like(xt)
        tiles.append(_dot(w_ref[2 * p], jnp.where(low, xt, zero)) + _dot(w_ref[2 * p + 1], jnp.where(low, zero, xt)))
    return jnp.concatenate(tiles, axis=-1)


def _gmlp_fwd(zuv, lg, wm, bfull, avg):
    T = zuv.shape[0]
    tg = min(TM, T)
    nb = tg // SG_BLOCK

    def body(zuv_ref, lg_ref, wm_ref, b_ref, avg_ref, oa_ref):
        u = _gelu(zuv_ref[:, :D_A])
        _, _, vn = _layer_norm(_gelu(zuv_ref[:, D_A:]), lg_ref[...], avg_ref)
        for n in range(nb):
            rs = slice(n * SG_BLOCK, (n + 1) * SG_BLOCK)
            oa_ref[rs, :] = (u[rs] * (_mix_heads(wm_ref, vn, rs) + b_ref[...])).astype(BF16)

    return pl.pallas_call(
        body, name="gmlp_fwd", grid=(T // tg,),
        in_specs=[_rows(tg, 2 * D_A), _full((1, D_A)), _full((N_HEADS, SG_BLOCK, SG_BLOCK)), _full((SG_BLOCK, D_A)),
                  _full((D_A, D_A))],
        out_specs=_rows(tg, D_A),
        out_shape=jax.ShapeDtypeStruct((T, D_A), BF16),
        compiler_params=_cp("arbitrary"),
    )(zuv, lg, wm, bfull, avg)


def _gmlp_bwd(zuv, doa, lg, wm, wmt, bfull, maskf, hsel, avg):
    T = zuv.shape[0]
    tg = min(TM, T)
    nb = tg // SG_BLOCK
    nt = T // tg

    def body(zuv_ref, doa_ref, lg_ref, wm_ref, wmt_ref, b_ref, mask_ref, hsel_ref, avg_ref,
             dzuv_ref, dwm_ref, dsgb_ref, dlg_ref, dbacc):
        i = pl.program_id(0)

        @pl.when(i == 0)
        def _():
            dwm_ref[...] = jnp.zeros_like(dwm_ref)
            dlg_ref[...] = jnp.zeros_like(dlg_ref)
            dbacc[...] = jnp.zeros_like(dbacc)

        zu, zv = zuv_ref[:, :D_A], zuv_ref[:, D_A:]
        lgv = lg_ref[...]
        vhat, rstd, vn = _layer_norm(_gelu(zv), lgv, avg_ref)
        dmixed = doa_ref[...] * _gelu(zu)
        dmb = dmixed.astype(BF16)
        low = lax.broadcasted_iota(jnp.int32, (SG_BLOCK, LANES), 1) < HEAD_DIM
        mixed, dvn = [], []
        for n in range(nb):
            rs = slice(n * SG_BLOCK, (n + 1) * SG_BLOCK)
            mixed.append(_mix_heads(wm_ref, vn, rs) + b_ref[...])
            dvn.append(_mix_heads(wmt_ref, dmb, rs))
            dbacc[...] += dmixed[rs]
            for p in range(N_HEADS // 2):
                ls = slice(LANES * p, LANES * (p + 1))
                dmt, vnt = dmb[rs, ls], vn[rs, ls]
                zero = jnp.zeros_like(dmt)
                dwm_ref[2 * p] += _dot_nt(jnp.where(low, dmt, zero), vnt) * mask_ref[...]
                dwm_ref[2 * p + 1] += _dot_nt(jnp.where(low, zero, dmt), vnt) * mask_ref[...]
        mixed = jnp.concatenate(mixed, axis=0) if nb > 1 else mixed[0]
        dvn = jnp.concatenate(dvn, axis=0) if nb > 1 else dvn[0]
        dlg_ref[...] += jnp.sum(dvn * vhat, axis=0, keepdims=True)
        dvhat = dvn * lgv
        dv = rstd * (dvhat - _segmean(dvhat, avg_ref, 3) - vhat * _segmean(dvhat * vhat, avg_ref, 3))
        dzuv_ref[:, :D_A] = (doa_ref[...] * mixed * _gelu_grad(zu)).astype(BF16)
        dzuv_ref[:, D_A:] = (dv * _gelu_grad(zv)).astype(BF16)

        @pl.when(i == nt - 1)
        def _():
            dsgb_ref[...] = lax.dot_general(hsel_ref[...], dbacc[...], (((1,), (1,)), ((), ())),
                                            precision=lax.Precision.HIGHEST, preferred_element_type=F32)

    return pl.pallas_call(
        body, name="gmlp_bwd", grid=(nt,),
        in_specs=[_rows(tg, 2 * D_A), _rows(tg, D_A), _full((1, D_A)), _full((N_HEADS, SG_BLOCK, SG_BLOCK)),
                  _full((N_HEADS, SG_BLOCK, SG_BLOCK)), _full((SG_BLOCK, D_A)), _full((SG_BLOCK, SG_BLOCK)),
                  _full((N_HEADS, D_A)), _full((D_A, D_A))],
        out_specs=[_rows(tg, 2 * D_A), _full((N_HEADS, SG_BLOCK, SG_BLOCK)), _full((N_HEADS, SG_BLOCK)), _full((1, D_A))],
        out_shape=[jax.ShapeDtypeStruct((T, 2 * D_A), BF16), jax.ShapeDtypeStruct((N_HEADS, SG_BLOCK, SG_BLOCK), F32),
                   jax.ShapeDtypeStruct((N_HEADS, SG_BLOCK), F32), jax.ShapeDtypeStruct((1, D_A), F32)],
        scratch_shapes=[pltpu.VMEM((SG_BLOCK, D_A), F32)],
        compiler_params=_cp("arbitrary"),
    )(zuv, doa, lg, wm, wmt, bfull, maskf, hsel, avg)


def _fox_fwd(qa, ka, va, side):
    T = qa.shape[1]
    tq = min(TQ, T)
    nq = T // tq
    n = side.n

    def body(qa_ref, ka_ref, va_ref, *rest):
        ins, (o_ref, lse_ref), outs = rest[:n], rest[n:n + 2], rest[n + 2:2 * n + 2]
        acc_ref, *sems = rest[2 * n + 2:]
        i = pl.program_id(1)
        if n:
            @pl.when((pl.program_id(0) == 0) & (i == 0))
            def _():
                side.start(ins, outs, sems)
        row = lax.broadcasted_iota(jnp.int32, (tq, tq), 0)
        col = lax.broadcasted_iota(jnp.int32, (tq, tq), 1)
        qs = [qa_ref[0], qa_ref[1]]

        def tiles(js, carry, diag):
            offs = [pl.multiple_of(j * tq, tq) for j in js]
            logits = [[_dot_nt(qs[hh], ka_ref[hh, pl.ds(off, tq), :]) for hh in range(2)] for off in offs]
            carry = list(carry)
            for off, per_head in zip(offs, logits):
                for hh, s in enumerate(per_head):
                    if diag:
                        s = jnp.where(col <= row, s, NEG)
                    m = carry[hh]
                    m_new = jnp.maximum(m, jnp.max(s, axis=-1, keepdims=True))
                    pr = jnp.exp(s - m_new)
                    acc_ref[hh] = jnp.exp(m - m_new) * acc_ref[hh] + _dot(pr.astype(BF16), va_ref[hh, pl.ds(off, tq), :])
                    carry[hh] = m_new
            return tuple(carry)

        acc_ref[...] = jnp.zeros_like(acc_ref)
        init = (jnp.full((tq, 1), NEG, F32),) * 2
        carry = lax.fori_loop(0, i // FWD_UNROLL, lambda t, cr: tiles([FWD_UNROLL * t + u for u in range(FWD_UNROLL)], cr, False), init)
        carry = lax.fori_loop(i - i % FWD_UNROLL, i, lambda j, cr: tiles([j], cr, False), carry)
        carry = tiles([i], carry, True)
        for hh in range(2):
            m, acc = carry[hh], acc_ref[hh]
            l = acc[:, L_ROW:L_ROW + 1]
            o_ref[:, _head_sl(hh)] = acc[:, :HEAD_DIM] / l
            hi, mid, lo = _split3(-(m + jnp.log(l)))
            lse_ref[hh] = _lanes(tq, LANES, {L_LSE: hi, L_LSE + 1: mid, L_LSE + 2: lo}).astype(BF16)
        if n:
            @pl.when((pl.program_id(0) == N_HEADS // 2 - 1) & (i == nq - 1))
            def _():
                side.finish(ins, outs, sems)

    tile = pl.BlockSpec((2, tq, LANES), lambda p, i: (p, i, 0))
    seq = pl.BlockSpec((2, T, LANES), lambda p, i: (p, 0, 0))
    res = pl.pallas_call(
        body, name="fox_fwd", grid=(N_HEADS // 2, nq),
        in_specs=[tile, seq, seq, *side.specs],
        out_specs=[pl.BlockSpec((tq, LANES), lambda p, i: (i, p)), tile, *side.specs],
        out_shape=[jax.ShapeDtypeStruct((T, D_B), F32), jax.ShapeDtypeStruct((N_HEADS, T, LANES), BF16), *side.out_shape],
        scratch_shapes=[pltpu.VMEM((2, tq, LANES), F32), *side.scratch],
        compiler_params=_cp("arbitrary", "arbitrary"),
    )(qa, ka, va, *side.operands)
    return res[0], res[1], res[2:]


def _fox_bwd(qa, lse, doa, ka, va, side):
    T = qa.shape[1]
    tq = min(TQ, T)
    nq = T // tq
    n = side.n

    def body(qa_ref, lse_ref, doa_ref, ka_ref, va_ref, *rest):
        ins, (dqa_ref, dka_ref, dva_ref), outs = rest[:n], rest[n:n + 3], rest[n + 3:2 * n + 3]
        dv_acc, *sems = rest[2 * n + 3:]
        j = pl.program_id(1)
        if n:
            @pl.when((pl.program_id(0) == 0) & (j == 0))
            def _():
                side.start(ins, outs, sems)

        @pl.when(j == 0)
        def _():
            dqa_ref[...] = jnp.zeros_like(dqa_ref)

        row = lax.broadcasted_iota(jnp.int32, (tq, tq), 0)
        col = lax.broadcasted_iota(jnp.int32, (tq, tq), 1)
        ks = [ka_ref[0], ka_ref[1]]
        vs = [va_ref[0], va_ref[1]]

        def tiles(ids, diag):
            work = []
            for i in ids:
                off = pl.multiple_of(i * tq, tq)
                for hh in range(2):
                    qi = qa_ref[hh, pl.ds(off, tq), :] + lse_ref[hh, pl.ds(off, tq), :]
                    doi = doa_ref[hh, pl.ds(off, tq), :]
                    work.append((off, hh, qi, doi, _dot_nt(ks[hh], qi), _dot_nt(vs[hh], doi)))
            for off, hh, qi, doi, st, dpt in work:
                if diag:
                    st = jnp.where(row <= col, st, NEG)
                pt = jnp.exp(st)
                dv_acc[hh] += _dot(pt.astype(BF16), doi)
                dsb = (pt * dpt).astype(BF16)
                dka_ref[hh] += _dot(dsb, qi)
                dqa_ref[hh, pl.ds(off, tq), :] += _dot_tn(dsb, ks[hh])

        dka_ref[...] = jnp.zeros_like(dka_ref)
        dv_acc[...] = jnp.zeros_like(dv_acc)
        tiles([j], True)
        todo = nq - 1 - j

        @pl.loop(0, todo // BWD_UNROLL)
        def _(t):
            tiles([j + 1 + BWD_UNROLL * t + u for u in range(BWD_UNROLL)], False)

        @pl.loop(nq - todo % BWD_UNROLL, nq)
        def _(i):
            tiles([i], False)

        dva_ref[...] = dv_acc[...].astype(BF16)
        if n:
            @pl.when((pl.program_id(0) == N_HEADS // 2 - 1) & (j == nq - 1))
            def _():
                side.finish(ins, outs, sems)

    tile = pl.BlockSpec((2, tq, LANES), lambda p, j: (p, j, 0))
    seq = pl.BlockSpec((2, T, LANES), lambda p, j: (p, 0, 0))
    res = pl.pallas_call(
        body, name="fox_bwd", grid=(N_HEADS // 2, nq),
        in_specs=[seq, seq, seq, tile, tile, *side.specs],
        out_specs=[seq, tile, tile, *side.specs],
        out_shape=[jax.ShapeDtypeStruct((N_HEADS, T, LANES), F32), jax.ShapeDtypeStruct((N_HEADS, T, LANES), F32),
                   jax.ShapeDtypeStruct((N_HEADS, T, LANES), BF16), *side.out_shape],
        scratch_shapes=[pltpu.VMEM((2, tq, LANES), F32), *side.scratch],
        compiler_params=_cp("arbitrary", "arbitrary"),
    )(qa, lse, doa, ka, va, *side.operands)
    return res[0], res[1], res[2], res[3:]


def _fwd_mid(x, oa, ob, w_out, g2, w_up):
    T = x.shape[0]
    tm = min(TM, T)

    def body(x_ref, oa_ref, ob_ref, wo_ref, g_ref, wu_ref, x2_ref, h2_ref, a_ref):
        oab = jnp.concatenate([oa_ref[...], ob_ref[...].astype(BF16)], axis=-1)
        x2 = x_ref[...] + _dot(oab, wo_ref[...])
        x2_ref[...] = x2
        _, n = _rms(x2)
        h2 = (n * g_ref[...]).astype(BF16)
        h2_ref[...] = h2
        a_ref[...] = _dot(h2, wu_ref[...])

    return pl.pallas_call(
        body, name="fwd_mid", grid=(T // tm,),
        in_specs=[_rows(tm, D_MODEL), _rows(tm, D_A), _rows(tm, D_B), _full((D_MODEL, D_MODEL), True), _full((1, D_MODEL)),
                  _full((D_MODEL, D_FF2), True)],
        out_specs=[_rows(tm, D_MODEL), _rows(tm, D_MODEL), _rows(tm, D_FF2)],
        out_shape=[jax.ShapeDtypeStruct((T, D_MODEL), F32), jax.ShapeDtypeStruct((T, D_MODEL), BF16),
                   jax.ShapeDtypeStruct((T, D_FF2), F32)],
        compiler_params=_cp("arbitrary"),
    )(x, oa, ob, w_out, g2, w_up)


def _row_before(x, prev, k):
    rolled = pltpu.roll(x, k, axis=0)
    row = lax.broadcasted_iota(jnp.int32, (8, x.shape[1]), 0)
    head = rolled[0:8]
    for r in range(k):
        head = jnp.where(row == r, prev[8 - k + r:9 - k + r], head)
    return jnp.concatenate([head, rolled[8:]], axis=0)


def _row_after(x, nxt, k):
    tm = x.shape[0]
    rolled = pltpu.roll(x, tm - k, axis=0)
    row = lax.broadcasted_iota(jnp.int32, (8, x.shape[1]), 0)
    tail = rolled[tm - 8:tm]
    for r in range(k):
        tail = jnp.where(row == 8 - k + r, nxt[r:r + 1], tail)
    return jnp.concatenate([rolled[:tm - 8], tail], axis=0)


def _fwd_ffn(a, x2, wc, bc, w_down, g3, tgt):
    T = x2.shape[0]
    tm = min(TM, T)

    def body(a_ref, x2_ref, wc_ref, bc_ref, wd_ref, g_ref, tgt_ref, ac_ref, yff_ref, dx3_ref, loss_ref, dg3_ref, carry):
        @pl.when(pl.program_id(0) == 0)
        def _():
            carry[...] = jnp.zeros_like(carry)
            loss_ref[...] = jnp.zeros_like(loss_ref)
            dg3_ref[...] = jnp.zeros_like(dg3_ref)

        def conv(cs):
            a0 = a_ref[:, cs]
            prev = carry[:, cs]
            ac = (wc_ref[0:1, cs] * _row_before(a0, prev, 2) + wc_ref[1:2, cs] * _row_before(a0, prev, 1)
                  + wc_ref[2:3, cs] * a0 + bc_ref[:, cs])
            ac_ref[:, cs] = ac.astype(BF16)
            return ac

        x3 = x2_ref[...]
        for ci in range(D_FF // CW):
            gs = slice(ci * CW, (ci + 1) * CW)
            ag = conv(gs)
            av = conv(slice(D_FF + ci * CW, D_FF + (ci + 1) * CW))
            yb = (ag * jax.nn.sigmoid(ag) * av).astype(BF16)
            yff_ref[:, gs] = yb
            x3 = x3 + _dot(yb, wd_ref[gs, :])
        carry[...] = a_ref[tm - 8:tm, :]
        r, n = _rms(x3)
        g = g_ref[...]
        diff = n * g - tgt_ref[...]
        loss_ref[...] += (0.5 / D_MODEL) * jnp.sum(diff * diff)
        dout = diff * (1.0 / D_MODEL)
        dg3_ref[...] += jnp.sum(dout * n, axis=0, keepdims=True)
        dx3_ref[...] = _rms_bwd(dout, n, r, g)

    return pl.pallas_call(
        body, name="fwd_ffn", grid=(T // tm,),
        in_specs=[_rows(tm, D_FF2), _rows(tm, D_MODEL), _full((3, D_FF2)), _full((1, D_FF2)), _full((D_FF, D_MODEL), True),
                  _full((1, D_MODEL)), _rows(tm, D_MODEL)],
        out_specs=[_rows(tm, D_FF2), _rows(tm, D_FF), _rows(tm, D_MODEL), _full((8, LANES)), _full((1, D_MODEL))],
        out_shape=[jax.ShapeDtypeStruct((T, D_FF2), BF16), jax.ShapeDtypeStruct((T, D_FF), BF16),
                   jax.ShapeDtypeStruct((T, D_MODEL), F32), jax.ShapeDtypeStruct((8, LANES), F32),
                   jax.ShapeDtypeStruct((1, D_MODEL), F32)],
        scratch_shapes=[pltpu.VMEM((8, D_FF2), F32)],
        compiler_params=_cp("arbitrary"),
    )(a, x2, wc, bc, w_down, g3, tgt)


def _bwd_ffn(dx3, a, ac, yff, h2, w_down, wc):
    T = dx3.shape[0]
    tm = min(TM, T)
    nt = T // tm
    half = D_FF // 2
    shard_up, shard_down = D_FF2 // N_DEV, D_FF // N_DEV

    def body(dx3_ref, ag_ref, av_ref, acg_ref, acv_ref, yff_ref, h2_ref, wd_ref, wcg_ref, wcv_ref,
             dag_ref, dav_ref, dwcg_ref, dwcv_ref, dbcg_ref, dbcv_ref, dwd_ref, dwu_ref,
             nxt, shifted, acc_down, acc_g, acc_v, stage_up, stage_down, sem):
        c, r = pl.program_id(0), pl.program_id(1)

        @pl.when(r == 0)
        def _():
            for ref in (nxt, dwcg_ref, dwcv_ref, dbcg_ref, dbcv_ref, acc_down, acc_g, acc_v):
                ref[...] = jnp.zeros_like(ref)

        dxb = dx3_ref[...].astype(BF16)
        dy_all = _dot_nt(dxb, wd_ref[...])

        def back(a_ref, w_ref, da_ref, dwc_ref, dbc_ref, nx, cs, dac):
            a0 = a_ref[:, cs]
            shifted[0] = _row_after(dac, nxt[:, nx], 1)
            shifted[1] = _row_after(dac, nxt[:, nx], 2)
            dp1, dp2 = shifted[0], shifted[1]
            dbc_ref[:, cs] += jnp.sum(dac, axis=0, keepdims=True)
            dwc_ref[0:1, cs] += jnp.sum(dp2 * a0, axis=0, keepdims=True)
            dwc_ref[1:2, cs] += jnp.sum(dp1 * a0, axis=0, keepdims=True)
            dwc_ref[2:3, cs] += jnp.sum(dac * a0, axis=0, keepdims=True)
            da_ref[:, cs] = (w_ref[2:3, cs] * dac + w_ref[1:2, cs] * dp1 + w_ref[0:1, cs] * dp2).astype(BF16)
            nxt[:, nx] = dac[0:8]

        for ci in range(half // LANES):
            cs = slice(ci * LANES, (ci + 1) * LANES)
            dy = dy_all[:, cs]
            ag, av = acg_ref[:, cs].astype(F32), acv_ref[:, cs].astype(F32)
            sg = jax.nn.sigmoid(ag)
            back(av_ref, wcv_ref, dav_ref, dwcv_ref, dbcv_ref, slice(half + ci * LANES, half + (ci + 1) * LANES), cs,
                 dy * (ag * sg))
            back(ag_ref, wcg_ref, dag_ref, dwcg_ref, dbcg_ref, cs, cs, dy * av * (sg * (1.0 + ag * (1.0 - sg))))

        acc_down[...] += _dot_tn(yff_ref[...], dxb)
        h2 = h2_ref[...]
        acc_g[...] += _dot_tn(h2, dag_ref[...])
        acc_v[...] += _dot_tn(h2, dav_ref[...])

        @pl.when(r == nt - 1)
        def _():
            for s in range(half // shard_down):
                stage_down[...] = acc_down[s * shard_down:(s + 1) * shard_down, :].astype(BF16)
                out = pltpu.make_async_copy(stage_down, dwd_ref.at[(half // shard_down) * c + s], sem)
                out.start()
                out.wait()
            for acc, first in ((acc_g, 0), (acc_v, N_DEV // 2)):
                for s in range(half // shard_up):
                    stage_up[...] = acc[:, s * shard_up:(s + 1) * shard_up].astype(BF16)
                    out = pltpu.make_async_copy(stage_up, dwu_ref.at[first + (half // shard_up) * c + s], sem)
                    out.start()
                    out.wait()

    def cols(width, second_half):
        return pl.BlockSpec((tm, width), lambda c, r: (nt - 1 - r, c + (2 if second_half else 0)))

    def param(rows, second_half):
        return pl.BlockSpec((rows, half), lambda c, r: (0, c + (2 if second_half else 0)))

    tokens = pl.BlockSpec((tm, D_MODEL), lambda c, r: (nt - 1 - r, 0))
    return pl.pallas_call(
        body, name="bwd_ffn", grid=(2, nt),
        in_specs=[tokens, cols(half, False), cols(half, True), cols(half, False), cols(half, True), cols(half, False), tokens,
                  pl.BlockSpec((half, D_MODEL), lambda c, r: (c, 0), pipeline_mode=pl.Buffered(1)),
                  param(3, False), param(3, True)],
        out_specs=[cols(half, False), cols(half, False), param(3, False), param(3, False), param(1, False), param(1, False),
                   ANY, ANY],
        out_shape=[jax.ShapeDtypeStruct((T, D_FF), BF16), jax.ShapeDtypeStruct((T, D_FF), BF16),
                   jax.ShapeDtypeStruct((3, D_FF), F32), jax.ShapeDtypeStruct((3, D_FF), F32),
                   jax.ShapeDtypeStruct((1, D_FF), F32), jax.ShapeDtypeStruct((1, D_FF), F32),
                   jax.ShapeDtypeStruct((N_DEV, shard_down, D_MODEL), BF16), jax.ShapeDtypeStruct((N_DEV, D_MODEL, shard_up), BF16)],
        scratch_shapes=[pltpu.VMEM((8, D_FF), F32), pltpu.VMEM((2, tm, LANES), F32), pltpu.VMEM((half, D_MODEL), F32),
                        pltpu.VMEM((D_MODEL, half), F32), pltpu.VMEM((D_MODEL, half), F32),
                        pltpu.VMEM((D_MODEL, shard_up), BF16), pltpu.VMEM((shard_down, D_MODEL), BF16),
                        pltpu.SemaphoreType.DMA],
        compiler_params=_cp("arbitrary", "arbitrary"),
    )(dx3, a, a, ac, ac, yff, h2, w_down, wc, wc)


def _bwd_mid(da_g, da_v, w_up, x2, g2, dx3, w_out, ob):
    T = x2.shape[0]
    tm = min(TM, T)

    def body(dag_ref, dav_ref, wu_ref, x2_ref, g_ref, dx3_ref, wo_ref, ob_ref, hsum_ref, place_ref,
             dx2_ref, doa_ref, dob_ref, dg2_ref):
        @pl.when(pl.program_id(0) == 0)
        def _():
            dg2_ref[...] = jnp.zeros_like(dg2_ref)

        dh2 = _dot_nt(dag_ref[...], wu_ref[:, :D_FF]) + _dot_nt(dav_ref[...], wu_ref[:, D_FF:])
        r, n = _rms(x2_ref[...])
        dg2_ref[...] += jnp.sum(dh2 * n, axis=0, keepdims=True)
        dx2 = dx3_ref[...] + _rms_bwd(dh2, n, r, g_ref[...])
        dx2_ref[...] = dx2
        doab = _dot_nt(dx2.astype(BF16), wo_ref[...])
        doa_ref[...] = doab[:, :D_A]
        dob = doab[:, D_A:]
        rest, delta = dob * ob_ref[...], None
        for _ in range(3):
            piece = rest.astype(BF16)
            term = _dot(piece, hsum_ref[...])
            delta = term if delta is None else delta + term
            rest = rest - piece.astype(F32)
        hi, mid, lo = _split3(-delta)
        parts = jnp.concatenate([hi.astype(BF16), mid.astype(BF16), lo.astype(BF16)], axis=-1)
        placed = _dot(parts, place_ref[...])
        data = lax.broadcasted_iota(jnp.int32, (tm, LANES), 1) < HEAD_DIM
        for hd in range(N_HEADS):
            tile = dob[:, LANES * (hd // 2):LANES * (hd // 2 + 1)]
            rows = tile if hd % 2 == 0 else pltpu.roll(tile, HEAD_DIM, axis=1)
            dob_ref[hd] = jnp.where(data, rows, placed[:, LANES * hd:LANES * (hd + 1)]).astype(BF16)

    hsum = (jnp.arange(D_B)[:, None] // HEAD_DIM == jnp.arange(LANES)[None, :]).astype(BF16)
    j, h = jnp.arange(3 * LANES) // LANES, jnp.arange(3 * LANES) % LANES
    place = ((h[:, None] < N_HEADS)
             & (jnp.arange(N_HEADS * LANES)[None, :] == LANES * h[:, None] + L_ROW + j[:, None])).astype(BF16)
    return pl.pallas_call(
        body, name="bwd_mid", grid=(T // tm,),
        in_specs=[_rows(tm, D_FF), _rows(tm, D_FF), _full((D_MODEL, D_FF2), True), _rows(tm, D_MODEL), _full((1, D_MODEL)),
                  _rows(tm, D_MODEL), _full((D_MODEL, D_MODEL), True), _rows(tm, D_B), _full((D_B, LANES)),
                  _full((3 * LANES, N_HEADS * LANES))],
        out_specs=[_rows(tm, D_MODEL), _rows(tm, D_A), pl.BlockSpec((N_HEADS, tm, LANES), lambda i: (0, i, 0)),
                   _full((1, D_MODEL))],
        out_shape=[jax.ShapeDtypeStruct((T, D_MODEL), F32), jax.ShapeDtypeStruct((T, D_A), F32),
                   jax.ShapeDtypeStruct((N_HEADS, T, LANES), BF16), jax.ShapeDtypeStruct((1, D_MODEL), F32)],
        compiler_params=_cp("arbitrary"),
    )(da_g, da_v, w_up, x2, g2, dx3, w_out, ob, hsum, place)


def _bwd_in(dzuv, dqa, dka, dva, fl, x, dx2, w_in_p, g1):
    T = x.shape[0]
    tm = min(TM, T)
    nt = T // tm

    def body(dzuv_ref, dqa_ref, dka_ref, dva_ref, fl_ref, x_ref, dx2_ref, w_ref, g_ref,
             gx_ref, dz_ref, dg1_ref, dfb_ref, carry):
        @pl.when(pl.program_id(0) == 0)
        def _():
            carry[...] = jnp.zeros_like(carry)
            dg1_ref[...] = jnp.zeros_like(dg1_ref)
            dfb_ref[...] = jnp.zeros_like(dfb_ref)

        dc = _lanes(tm, LANES, {hd: dqa_ref[hd][:, L_ROW:L_ROW + 1] - dka_ref[hd][:, L_COL:L_COL + 1] for hd in range(N_HEADS)})
        later = (lax.broadcasted_iota(jnp.int32, (tm, tm), 1) >= lax.broadcasted_iota(jnp.int32, (tm, tm), 0)).astype(F32)
        dls = _dot_f32(later, dc) + carry[...]
        carry[...] = dls[0:1, :]
        dzf = dls * jax.nn.sigmoid(-fl_ref[...])
        dfb_ref[...] += jnp.sum(dzf, axis=0, keepdims=True)
        data = lax.broadcasted_iota(jnp.int32, (tm, LANES), 1) < HEAD_DIM

        def compact(ref):
            return [jnp.where(data, ref[2 * p], pltpu.roll(ref[2 * p + 1], HEAD_DIM, axis=1)) for p in range(N_HEADS // 2)]

        dz = jnp.concatenate([dzuv_ref[...], *[(t * (HEAD_DIM ** -0.5)).astype(BF16) for t in compact(dqa_ref)],
                              *[t.astype(BF16) for t in compact(dka_ref)], *compact(dva_ref), dzf.astype(BF16)], axis=-1)
        dz_ref[...] = dz
        dh1 = _dot_nt(dz, w_ref[...])
        r, n = _rms(x_ref[...])
        dg1_ref[...] += jnp.sum(dh1 * n, axis=0, keepdims=True)
        gx_ref[...] = dx2_ref[...] + _rms_bwd(dh1, n, r, g_ref[...])

    rv = functools.partial(_rows, tm, rev_nt=nt)
    heads = pl.BlockSpec((N_HEADS, tm, LANES), lambda i: (0, nt - 1 - i, 0))
    return pl.pallas_call(
        body, name="bwd_in", grid=(nt,),
        in_specs=[rv(2 * D_A), heads, heads, heads, rv(LANES), rv(D_MODEL), rv(D_MODEL),
                  _full((D_MODEL, D_IN_PAD), True), _full((1, D_MODEL))],
        out_specs=[rv(D_MODEL), rv(D_IN_PAD), _full((1, D_MODEL)), _full((1, LANES))],
        out_shape=[jax.ShapeDtypeStruct((T, D_MODEL), F32), jax.ShapeDtypeStruct((T, D_IN_PAD), BF16),
                   jax.ShapeDtypeStruct((1, D_MODEL), F32), jax.ShapeDtypeStruct((1, LANES), F32)],
        scratch_shapes=[pltpu.VMEM((1, LANES), F32)],
        compiler_params=_cp("arbitrary"),
    )(dzuv, dqa, dka, dva, fl, x, dx2, w_in_p, g1)


def _matmul_tn(a_parts, b, tmm, tn, tk, name, shard_cols=None, n_valid=None):
    T = b.shape[0]
    widths = [a.shape[1] for a in a_parts]
    M, N = sum(widths), b.shape[1]
    tk = min(tk, T)
    nk = T // tk
    part_w = tmm // len(a_parts)
    n_valid = N if n_valid is None else n_valid

    def body(*refs):
        a_refs, b_ref, o_ref, obf_ref = refs[:len(a_parts)], refs[-3], refs[-2], refs[-1]
        k = pl.program_id(2)

        @pl.when(k == 0)
        def _():
            o_ref[...] = jnp.zeros_like(o_ref)

        a = [r[...].astype(BF16) for r in a_refs]
        o_ref[...] += _dot_tn(a[0] if len(a) == 1 else jnp.concatenate(a, axis=-1), b_ref[...].astype(BF16))

        @pl.when(k == nk - 1)
        def _():
            if shard_cols is None:
                obf_ref[...] = o_ref[...].astype(BF16)
            else:
                for d in range(min(tn, n_valid) // shard_cols):
                    obf_ref[d] = o_ref[:, d * shard_cols:(d + 1) * shard_cols].astype(BF16)

    if shard_cols is None:
        bf_spec, bf_shape = pl.BlockSpec((tmm, tn), lambda i, j, k: (i, j)), (M, N)
    else:
        per_tile = min(tn, n_valid) // shard_cols
        bf_spec, bf_shape = pl.BlockSpec((per_tile, tmm, shard_cols), lambda i, j, k: (j, i, 0)), (N_DEV, M, shard_cols)
    a_specs = [pl.BlockSpec((tk, part_w), lambda i, j, k: (k, i)) for _ in a_parts]
    return pl.pallas_call(
        body, name=name, grid=(M // tmm, N // tn, nk),
        in_specs=[*a_specs, pl.BlockSpec((tk, tn), lambda i, j, k: (k, j))],
        out_specs=[pl.BlockSpec((tmm, tn), lambda i, j, k: (i, j)), bf_spec],
        out_shape=[jax.ShapeDtypeStruct((M, N), F32), jax.ShapeDtypeStruct(bf_shape, BF16)],
        compiler_params=_cp("arbitrary", "arbitrary", "arbitrary"),
    )(*a_parts, b)


class _Exchange:
    def __init__(self, gather, scatter, relay):
        self.n_g, self.n, self.relay = len(gather), len(gather) + len(scatter), relay
        self.operands = [*gather, *scatter]
        self.out_shape = [jax.ShapeDtypeStruct((N_DEV, *g.shape), g.dtype) for g in gather]
        self.out_shape += [jax.ShapeDtypeStruct(s.shape, s.dtype) for s in scatter]
        self.specs = [ANY] * self.n
        n = self.n
        self.scratch = [pltpu.SemaphoreType.DMA((7 * n,)), pltpu.SemaphoreType.DMA((7 * n,)),
                        pltpu.SemaphoreType.DMA((n,))] if n else []

    def _plan(self, ins, outs, sems):
        send_sems, recv_sems, local_sems = sems
        x, y, c = (lax.axis_index(ax) for ax in MESH_AXES)
        me = 4 * x + 2 * y + c
        sibling = (x, y, 1 - c)
        chips = [(1 - x, y), (x, 1 - y), (1 - x, 1 - y)]
        peers = [sibling] + [(*chip, c) for chip in chips] + [(*chip, 1 - c) for chip in chips]

        def index(dev):
            return 4 * dev[0] + 2 * dev[1] + dev[2]

        def remote(k, src, dst, to):
            return pltpu.make_async_remote_copy(src_ref=src, dst_ref=dst, send_sem=send_sems.at[k], recv_sem=recv_sems.at[k],
                                                device_id=to, device_id_type=pl.DeviceIdType.MESH)

        local, sends, relays, recvs = [], [], [], []
        for a in range(self.n):
            src, out, base = ins[a], outs[a], 7 * a
            if a >= self.n_g:
                local.append(pltpu.make_async_copy(src.at[me], out.at[me], local_sems.at[a]))
                sends += [remote(base + k, src.at[index(peer)], out.at[me], peer) for k, peer in enumerate(peers)]
            else:
                local.append(pltpu.make_async_copy(src, out.at[me], local_sems.at[a]))
                sends += [remote(base + k, src, out.at[me], peer) for k, peer in enumerate(peers[:4 if self.relay else 7])]
            for k, peer in enumerate(peers):
                slot = out.at[index(peer)]
                if a < self.n_g and self.relay and k >= 4:
                    continue
                recv = remote(base + k, slot, slot, peer)
                if a < self.n_g and self.relay and k >= 1:
                    relays.append((recv, remote(base + 3 + k, slot, slot, sibling)))
                else:
                    recvs.append(recv)
            if a < self.n_g and self.relay:
                for j, chip in enumerate(chips):
                    slot = out.at[index((*chip, 1 - c))]
                    recvs.append(remote(base + 4 + j, slot, slot, sibling))
        return local, sends, relays, recvs

    def start(self, ins, outs, sems):
        local, sends, _, _ = self._plan(ins, outs, sems)
        for cp in local + sends:
            cp.start()

    def finish(self, ins, outs, sems):
        local, sends, relays, recvs = self._plan(ins, outs, sems)
        for recv, fwd in relays:
            recv.wait_recv()
            fwd.start()
        for recv in recvs:
            recv.wait_recv()
        for cp in sends + [fwd for _, fwd in relays]:
            cp.wait_send()
        for cp in local:
            cp.wait()


def _exchange(gather, scatter, name):
    ex = _Exchange(gather, scatter, relay=True)
    n = ex.n

    def body(*refs):
        ins, outs, sems = refs[:n], refs[n:2 * n], refs[2 * n:]
        ex.start(ins, outs, sems)
        ex.finish(ins, outs, sems)

    return pl.pallas_call(body, name=name, in_specs=ex.specs, out_specs=ex.specs, out_shape=ex.out_shape,
                          scratch_shapes=ex.scratch)(*ex.operands)


def _adamw(w, g, m, v):
    m = ADAM_B1 * m + (1.0 - ADAM_B1) * g
    v = ADAM_B2 * v + (1.0 - ADAM_B2) * jnp.square(g)
    m_hat = m / (1.0 - ADAM_B1 ** ADAM_STEP)
    v_hat = v / (1.0 - ADAM_B2 ** ADAM_STEP)
    delta = -ADAM_LR * (m_hat / (jnp.sqrt(v_hat) + ADAM_EPS) + ADAM_WD * w)
    return delta, m, v


def _adamw_shard(w, m, v, recv, tr, name):
    _, R, C = w.shape

    def body(w_ref, m_ref, v_ref, recv_ref, g_ref, d_ref, nm_ref, nv_ref):
        g = recv_ref[0].astype(F32)
        for d in range(1, N_DEV):
            g = g + recv_ref[d].astype(F32)
        g_ref[...] = g
        d_ref[...], nm_ref[...], nv_ref[...] = _adamw(w_ref[...], g, m_ref[...], v_ref[...])

    blk = pl.BlockSpec((None, tr, C), lambda i: (0, i, 0))
    return pl.pallas_call(
        body, name=name, grid=(R // tr,),
        in_specs=[blk, blk, blk, pl.BlockSpec((N_DEV, tr, C), lambda i: (0, i, 0))],
        out_specs=[blk] * 4, out_shape=[jax.ShapeDtypeStruct((1, R, C), F32)] * 4,
        compiler_params=_cp("arbitrary"),
    )(w, m, v, recv)


def _adamw_small(params, gathered, loss_parts):
    n = len(params)

    def body(*refs):
        ins, gs, loss_ref, outs = refs[:3 * n], refs[3 * n:4 * n], refs[4 * n], refs[4 * n + 1:]
        for p in range(n):
            w_ref, m_ref, v_ref = ins[3 * p:3 * p + 3]
            g = gs[p][0]
            for d in range(1, N_DEV):
                g = g + gs[p][d]
            g = g[..., :w_ref.shape[-1]]
            g_ref, d_ref, nm_ref, nv_ref = outs[4 * p:4 * p + 4]
            g_ref[...] = g
            d_ref[...], nm_ref[...], nv_ref[...] = _adamw(w_ref[...], g, m_ref[...], v_ref[...])
        total = loss_ref[0]
        for d in range(1, N_DEV):
            total = total + loss_ref[d]
        outs[4 * n][...] = total

    out_shape = [jax.ShapeDtypeStruct(w.shape, F32) for w, _, _ in params for _ in range(4)]
    res = pl.pallas_call(body, name="adamw_small", out_shape=[*out_shape, jax.ShapeDtypeStruct((8, LANES), F32)],
                         compiler_params=pltpu.CompilerParams(vmem_limit_bytes=VMEM_LIMIT))(
        *[t for p in params for t in p], *gathered, loss_parts)
    return [res[4 * p:4 * p + 4] for p in range(n)], res[4 * n][0, 0]


def _col_shards(g):
    return jnp.transpose(g.reshape(g.shape[0], N_DEV, -1), (1, 0, 2))


def _row_shards(g):
    return g.reshape(N_DEV, -1, g.shape[1])


def _cols_whole(g):
    return jnp.transpose(g, (1, 0, 2)).reshape(g.shape[1], -1)


def kernel(x, norm_mix_g, w_in, f_bias, sg_ln_g, sg_w, sg_b, w_out, norm_ffn_g, w_up, w_conv, b_conv, w_down, norm_final_g, loss_target, m_norm_mix_g, m_w_in, m_f_bias, m_sg_ln_g, m_sg_w, m_sg_b, m_w_out, m_norm_ffn_g, m_w_up, m_w_conv, m_b_conv, m_w_down, m_norm_final_g, v_norm_mix_g, v_w_in, v_f_bias, v_sg_ln_g, v_sg_w, v_sg_b, v_w_out, v_norm_ffn_g, v_w_up, v_w_conv, v_b_conv, v_w_down, v_norm_final_g):
    xs, tgt = x[0], loss_target[0]
    g1, g2, g3 = norm_mix_g, norm_ffn_g, norm_final_g.reshape(1, D_MODEL)
    lg = sg_ln_g.reshape(1, D_A)
    fb = jnp.pad(f_bias, ((0, 0), (0, LANES - N_HEADS)))
    pos_chunk = jnp.arange(SG_BLOCK) // SG_CHUNK
    maskf = (pos_chunk[:, None] >= pos_chunk[None, :]).astype(F32)
    wm = (sg_w[0] * maskf[None]).astype(BF16)
    wmt = jnp.swapaxes(wm, 1, 2)
    bfull = jnp.repeat(sg_b[0].T, HEAD_DIM, axis=1)
    hsel = jnp.repeat(jnp.eye(N_HEADS, dtype=F32), HEAD_DIM, axis=1)
    avg = (jnp.repeat(hsel, HEAD_DIM, axis=0) * (1.0 / HEAD_DIM)).astype(BF16)

    (win_g,) = _exchange([w_in[0].astype(BF16)], [], "gather_w_in")
    w_in_p = jnp.pad(_cols_whole(win_g), ((0, 0), (0, D_IN_PAD - D_IN)))
    zuv, qa, ka, va, fl, h1 = _fwd_in(xs, g1, w_in_p, fb)
    oa = _gmlp_fwd(zuv, lg, wm, bfull, avg)
    rest = _Exchange([w_out[0].astype(BF16), w_up[0].astype(BF16), w_down[0].astype(BF16), w_conv[0]], [], relay=False)
    ob, lse, (wout_g, wup_g, wdown_g, wc_g) = _fox_fwd(qa, ka, va, rest)
    w_out_f, w_up_f = wout_g.reshape(D_MODEL, D_MODEL), _cols_whole(wup_g)
    w_down_f, wc_f = wdown_g.reshape(D_FF, D_MODEL), _cols_whole(wc_g)

    x2, h2, a = _fwd_mid(xs, oa, ob, w_out_f, g2, w_up_f)
    ac, yff, dx3, loss, dg3 = _fwd_ffn(a, x2, wc_f, b_conv, w_down_f, g3, tgt)
    da_g, da_v, dwc_g, dwc_v, dbc_g, dbc_v, dwdown_bf, dwup_bf = _bwd_ffn(dx3, a, ac, yff, h2, w_down_f, wc_f)
    dwc, dbc = jnp.concatenate([dwc_g, dwc_v], axis=1), jnp.concatenate([dbc_g, dbc_v], axis=1)
    dx2, doa, dob, dg2 = _bwd_mid(da_g, da_v, w_up_f, x2, g2, dx3, w_out_f, ob)
    dzuv, dwm, dsgb, dlg = _gmlp_bwd(zuv, doa, lg, wm, wmt, bfull, maskf, hsel, avg)
    _, dwout_bf = _matmul_tn([oa, ob], dx2, D_MODEL, D_MODEL, 1024, "dw_out")

    early = ("w_out", "w_up", "wc", "w_down")
    wire = [_row_shards(dwout_bf), dwup_bf, _col_shards(dwc).astype(BF16), dwdown_bf]
    small_early = dict(lg=dlg, sg_w=dwm, sg_b=dsgb, g2=dg2, bc=dbc, g3=dg3)
    grads = _Exchange([*small_early.values(), loss], wire, relay=False)
    dqa, dka, dva, got = _fox_bwd(qa, lse, dob, ka, va, grads)
    n_small = len(small_early)
    gathered, loss_parts = dict(zip(small_early, got[:n_small])), got[n_small]
    recv = dict(zip(early, got[n_small + 1:]))

    gx, dz, dg1, dfb = _bwd_in(dzuv, dqa, dka, dva, fl, xs, dx2, w_in_p, g1)
    _, dwin_bf = _matmul_tn([h1], dz, D_MODEL // 2, D_IN_PAD, 1024, "dw_in", shard_cols=D_IN // N_DEV, n_valid=D_IN)
    gathered["g1"], gathered["fb"], recv["w_in"] = _exchange([dg1, dfb], [dwin_bf], "exchange_w_in")

    weights = dict(w_in=(w_in, m_w_in, v_w_in, 256), w_out=(w_out, m_w_out, v_w_out, 128), w_up=(w_up, m_w_up, v_w_up, 256),
                   wc=(w_conv, m_w_conv, v_w_conv, 3), w_down=(w_down, m_w_down, v_w_down, 176))
    res = {n: _adamw_shard(w, m, v, recv[n], tr, "adamw_" + n) for n, (w, m, v, tr) in weights.items()}

    reps = dict(g1=((norm_mix_g, m_norm_mix_g, v_norm_mix_g), (1, D_MODEL)), fb=((f_bias, m_f_bias, v_f_bias), (1, N_HEADS)),
                lg=((sg_ln_g, m_sg_ln_g, v_sg_ln_g), (1, D_A)), sg_w=((sg_w, m_sg_w, v_sg_w), (N_HEADS, SG_BLOCK, SG_BLOCK)),
                sg_b=((sg_b, m_sg_b, v_sg_b), (N_HEADS, SG_BLOCK)), g2=((norm_ffn_g, m_norm_ffn_g, v_norm_ffn_g), (1, D_MODEL)),
                bc=((b_conv, m_b_conv, v_b_conv), (1, D_FF2)), g3=((norm_final_g, m_norm_final_g, v_norm_final_g), (1, D_MODEL)))
    outs, loss_sum = _adamw_small([tuple(t.reshape(shape) for t in wmv) for wmv, shape in reps.values()],
                                  [gathered[n] for n in reps], loss_parts)
    for (n, (wmv, _)), out in zip(reps.items(), outs):
        res[n] = [o.reshape(wmv[0].shape) for o in out]

    names = ("g1", "w_in", "fb", "lg", "sg_w", "sg_b", "w_out", "g2", "w_up", "wc", "bc", "w_down", "g3")
    return (loss_sum, gx[None], *[res[n][0] for n in names], *[res[n][1] for n in names],
            *[res[n][2] for n in names], *[res[n][3] for n in names])
```

```python
import functools
import math

import jax
import jax.numpy as jnp
from jax import lax
from jax.experimental import pallas as pl
from jax.experimental.pallas import tpu as pltpu

F32 = jnp.float32
BF16 = jnp.bfloat16

D_MODEL = 1024
HEAD_DIM = 64
N_HEADS = 8
D_A = 512
D_B = 512
D_IN = 2 * D_A + 3 * D_B + N_HEADS
D_IN_PAD = 2688
D_FF = 2816
D_FF2 = 2 * D_FF
SG_BLOCK = 128
SG_CHUNK = 64
EPS = 1e-6
N_DEV = 8
LANES = 128
NEG = -1e30
VMEM_LIMIT = 56 * 1024 * 1024

ADAM_LR = 0.001
ADAM_B1 = 0.9
ADAM_B2 = 0.999
ADAM_EPS = 1e-08
ADAM_WD = 0.01
ADAM_STEP = 10

TM = 256
TM_WIDE = 512
TQ = 512
FWD_UNROLL = 4
BWD_UNROLL = 2
CW = 256

MESH_AXES = ("x", "y", "c")
ANY = pl.BlockSpec(memory_space=pl.ANY)


def _cp(*sem):
    return pltpu.CompilerParams(dimension_semantics=sem, vmem_limit_bytes=VMEM_LIMIT)


def _dot(a, b):
    return jnp.dot(a, b, preferred_element_type=F32)


def _dot_nt(a, b):
    return lax.dot_general(a, b, (((1,), (1,)), ((), ())), preferred_element_type=F32)


def _dot_tn(a, b):
    return lax.dot_general(a, b, (((0,), (0,)), ((), ())), preferred_element_type=F32)


def _dot_f32(a, b):
    return jnp.dot(a, b, precision=lax.Precision.HIGHEST, preferred_element_type=F32)


def _gelu(z):
    return 0.5 * z * (1.0 + lax.erf(z * (1.0 / math.sqrt(2.0))))


def _gelu_grad(z):
    return 0.5 * (1.0 + lax.erf(z * (1.0 / math.sqrt(2.0)))) + z * jnp.exp(-0.5 * z * z) * (1.0 / math.sqrt(2.0 * math.pi))


def _log_sigmoid(x):
    return jnp.minimum(x, 0.0) - jnp.log1p(jnp.exp(-jnp.abs(x)))


def _rms(x):
    r = lax.rsqrt(jnp.mean(x * x, axis=-1, keepdims=True) + EPS)
    return r, x * r


def _rms_bwd(dy, n, r, g):
    dn = dy * g
    return r * (dn - n * jnp.mean(dn * n, axis=-1, keepdims=True))


def _full(shape, single=False):
    nd = len(shape)
    if single:
        return pl.BlockSpec(shape, lambda *_: (0,) * nd, pipeline_mode=pl.Buffered(1))
    return pl.BlockSpec(shape, lambda *_: (0,) * nd)


def _rows(tm, cols, rev_nt=None):
    if rev_nt is None:
        return pl.BlockSpec((tm, cols), lambda i: (i, 0))
    return pl.BlockSpec((tm, cols), lambda i: (rev_nt - 1 - i, 0))


def _head_sl(h):
    return slice(HEAD_DIM * h, HEAD_DIM * (h + 1))


L_ROW = HEAD_DIM
L_COL = HEAD_DIM + 3
L_LSE = HEAD_DIM + 6


def _split3(x):
    hi = x.astype(BF16).astype(F32)
    mid = (x - hi).astype(BF16).astype(F32)
    lo = (x - hi - mid).astype(BF16).astype(F32)
    return hi, mid, lo


def _lanes(rows, width, parts):
    lane = lax.broadcasted_iota(jnp.int32, (rows, width), 1)
    out = jnp.zeros((rows, width), F32)
    for at, val in parts.items():
        out = jnp.where(lane == at, val, out)
    return out


def _fwd_in(x, g1, w_in_p, fb):
    T = x.shape[0]
    tm = min(TM, T)

    def body(x_ref, g_ref, w_ref, fb_ref, place_ref, zuv_ref, qa_ref, ka_ref, va_ref, fl_ref, h1_ref, carry):
        @pl.when(pl.program_id(0) == 0)
        def _():
            carry[...] = jnp.zeros_like(carry)

        r, n = _rms(x_ref[...])
        h = (n * g_ref[...]).astype(BF16)
        h1_ref[...] = h
        z = _dot(h, w_ref[...])
        zuv_ref[...] = z[:, :2 * D_A]
        o = 2 * D_A
        fl = z[:, o + 3 * D_B:] + fb_ref[...]
        fl_ref[...] = fl
        tri = (lax.broadcasted_iota(jnp.int32, (tm, tm), 0) >= lax.broadcasted_iota(jnp.int32, (tm, tm), 1)).astype(F32)
        c = _dot_f32(tri, _log_sigmoid(fl)) + carry[...]
        carry[...] = c[tm - 1:tm, :]
        hi, mid, lo = _split3(c)
        parts = jnp.concatenate([hi.astype(BF16), mid.astype(BF16), lo.astype(BF16)], axis=-1)
        placed = _dot(parts, place_ref[...])
        lane = lax.broadcasted_iota(jnp.int32, (tm, LANES), 1)
        data = lane < HEAD_DIM
        ones_q = ((lane >= L_COL) & (lane < L_COL + 3)).astype(F32)
        ones_k = (((lane >= L_ROW) & (lane < L_ROW + 3)) | ((lane >= L_LSE) & (lane < L_LSE + 3))).astype(F32)
        ones_v = ((lane >= L_ROW) & (lane < L_ROW + 3)).astype(F32)
        for hd in range(N_HEADS):
            def rows_of(first_col):
                tile = z[:, first_col + LANES * (hd // 2):first_col + LANES * (hd // 2 + 1)]
                return tile if hd % 2 == 0 else pltpu.roll(tile, HEAD_DIM, axis=1)

            hs = slice(LANES * hd, LANES * (hd + 1))
            qa_ref[hd] = jnp.where(data, rows_of(o) * (HEAD_DIM ** -0.5), placed[:, hs] + ones_q).astype(BF16)
            ka_ref[hd] = jnp.where(data, rows_of(o + D_B), ones_k - placed[:, N_HEADS * LANES:][:, hs]).astype(BF16)
            va_ref[hd] = jnp.where(data, rows_of(o + 2 * D_B), ones_v).astype(BF16)

    heads = pl.BlockSpec((N_HEADS, tm, LANES), lambda i: (0, i, 0))
    aug = jax.ShapeDtypeStruct((N_HEADS, T, LANES), BF16)
    j, h = jnp.arange(3 * LANES) // LANES, jnp.arange(3 * LANES) % LANES
    cols = jnp.arange(2 * N_HEADS * LANES)
    place = ((h[:, None] < N_HEADS) & ((cols[None, :] == LANES * h[:, None] + L_ROW + j[:, None])
                                       | (cols[None, :] == N_HEADS * LANES + LANES * h[:, None] + L_COL + j[:, None]))).astype(BF16)
    return pl.pallas_call(
        body, name="fwd_in", grid=(T // tm,),
        in_specs=[_rows(tm, D_MODEL), _full((1, D_MODEL)), _full((D_MODEL, D_IN_PAD), True), _full((1, LANES)),
                  _full((3 * LANES, 2 * N_HEADS * LANES))],
        out_specs=[_rows(tm, 2 * D_A), heads, heads, heads, _rows(tm, LANES), _rows(tm, D_MODEL)],
        out_shape=[jax.ShapeDtypeStruct((T, 2 * D_A), F32), aug, aug, aug, jax.ShapeDtypeStruct((T, LANES), F32),
                   jax.ShapeDtypeStruct((T, D_MODEL), BF16)],
        scratch_shapes=[pltpu.VMEM((1, LANES), F32)],
        compiler_params=_cp("arbitrary"),
    )(x, g1, w_in_p, fb, place)


def _segmean(x, avg_ref, parts):
    out, rest = None, x
    for _ in range(parts):
        piece = rest.astype(BF16)
        term = _dot(piece, avg_ref[...])
        out = term if out is None else out + term
        rest = rest - piece.astype(F32)
    return out


def _layer_norm(v, lg, avg_ref):
    d = v - _segmean(v, avg_ref, 3)
    rstd = lax.rsqrt(_segmean(d * d, avg_ref, 2) + EPS)
    vhat = d * rstd
    return vhat, rstd, (vhat * lg).astype(BF16)


def _mix_heads(w_ref, x, row_slice):
    low = lax.broadcasted_iota(jnp.int32, (SG_BLOCK, LANES), 1) < HEAD_DIM
    tiles = []
    for p in range(N_HEADS // 2):
        xt = x[row_slice, LANES * p:LANES * (p + 1)]
        zero = jnp.zeros_like(xt)
        tiles.append(_dot(w_ref[2 * p], jnp.where(low, xt, zero)) + _dot(w_ref[2 * p + 1], jnp.where(low, zero, xt)))
    return jnp.concatenate(tiles, axis=-1)


def _gmlp_fwd(zuv, lg, wm, bfull, avg):
    T = zuv.shape[0]
    tg = min(TM, T)
    nb = tg // SG_BLOCK

    def body(zuv_ref, lg_ref, wm_ref, b_ref, avg_ref, oa_ref):
        u = _gelu(zuv_ref[:, :D_A])
        _, _, vn = _layer_norm(_gelu(zuv_ref[:, D_A:]), lg_ref[...], avg_ref)
        for n in range(nb):
            rs = slice(n * SG_BLOCK, (n + 1) * SG_BLOCK)
            oa_ref[rs, :] = (u[rs] * (_mix_heads(wm_ref, vn, rs) + b_ref[...])).astype(BF16)

    return pl.pallas_call(
        body, name="gmlp_fwd", grid=(T // tg,),
        in_specs=[_rows(tg, 2 * D_A), _full((1, D_A)), _full((N_HEADS, SG_BLOCK, SG_BLOCK)), _full((SG_BLOCK, D_A)),
                  _full((D_A, D_A))],
        out_specs=_rows(tg, D_A),
        out_shape=jax.ShapeDtypeStruct((T, D_A), BF16),
        compiler_params=_cp("arbitrary"),
    )(zuv, lg, wm, bfull, avg)


def _gmlp_bwd(zuv, doa, lg, wm, wmt, bfull, maskf, hsel, avg):
    T = zuv.shape[0]
    tg = min(TM, T)
    nb = tg // SG_BLOCK
    nt = T // tg

    def body(zuv_ref, doa_ref, lg_ref, wm_ref, wmt_ref, b_ref, mask_ref, hsel_ref, avg_ref,
             dzuv_ref, dwm_ref, dsgb_ref, dlg_ref, dbacc):
        i = pl.program_id(0)

        @pl.when(i == 0)
        def _():
            dwm_ref[...] = jnp.zeros_like(dwm_ref)
            dlg_ref[...] = jnp.zeros_like(dlg_ref)
            dbacc[...] = jnp.zeros_like(dbacc)

        zu, zv = zuv_ref[:, :D_A], zuv_ref[:, D_A:]
        lgv = lg_ref[...]
        vhat, rstd, vn = _layer_norm(_gelu(zv), lgv, avg_ref)
        dmixed = doa_ref[...] * _gelu(zu)
        dmb = dmixed.astype(BF16)
        low = lax.broadcasted_iota(jnp.int32, (SG_BLOCK, LANES), 1) < HEAD_DIM
        mixed, dvn = [], []
        for n in range(nb):
            rs = slice(n * SG_BLOCK, (n + 1) * SG_BLOCK)
            mixed.append(_mix_heads(wm_ref, vn, rs) + b_ref[...])
            dvn.append(_mix_heads(wmt_ref, dmb, rs))
            dbacc[...] += dmixed[rs]
            for p in range(N_HEADS // 2):
                ls = slice(LANES * p, LANES * (p + 1))
                dmt, vnt = dmb[rs, ls], vn[rs, ls]
                zero = jnp.zeros_like(dmt)
                dwm_ref[2 * p] += _dot_nt(jnp.where(low, dmt, zero), vnt) * mask_ref[...]
                dwm_ref[2 * p + 1] += _dot_nt(jnp.where(low, zero, dmt), vnt) * mask_ref[...]
        mixed = jnp.concatenate(mixed, axis=0) if nb > 1 else mixed[0]
        dvn = jnp.concatenate(dvn, axis=0) if nb > 1 else dvn[0]
        dlg_ref[...] += jnp.sum(dvn * vhat, axis=0, keepdims=True)
        dvhat = dvn * lgv
        dv = rstd * (dvhat - _segmean(dvhat, avg_ref, 3) - vhat * _segmean(dvhat * vhat, avg_ref, 3))
        dzuv_ref[:, :D_A] = (doa_ref[...] * mixed * _gelu_grad(zu)).astype(BF16)
        dzuv_ref[:, D_A:] = (dv * _gelu_grad(zv)).astype(BF16)

        @pl.when(i == nt - 1)
        def _():
            dsgb_ref[...] = lax.dot_general(hsel_ref[...], dbacc[...], (((1,), (1,)), ((), ())),
                                            precision=lax.Precision.HIGHEST, preferred_element_type=F32)

    return pl.pallas_call(
        body, name="gmlp_bwd", grid=(nt,),
        in_specs=[_rows(tg, 2 * D_A), _rows(tg, D_A), _full((1, D_A)), _full((N_HEADS, SG_BLOCK, SG_BLOCK)),
                  _full((N_HEADS, SG_BLOCK, SG_BLOCK)), _full((SG_BLOCK, D_A)), _full((SG_BLOCK, SG_BLOCK)),
                  _full((N_HEADS, D_A)), _full((D_A, D_A))],
        out_specs=[_rows(tg, 2 * D_A), _full((N_HEADS, SG_BLOCK, SG_BLOCK)), _full((N_HEADS, SG_BLOCK)), _full((1, D_A))],
        out_shape=[jax.ShapeDtypeStruct((T, 2 * D_A), BF16), jax.ShapeDtypeStruct((N_HEADS, SG_BLOCK, SG_BLOCK), F32),
                   jax.ShapeDtypeStruct((N_HEADS, SG_BLOCK), F32), jax.ShapeDtypeStruct((1, D_A), F32)],
        scratch_shapes=[pltpu.VMEM((SG_BLOCK, D_A), F32)],
        compiler_params=_cp("arbitrary"),
    )(zuv, doa, lg, wm, wmt, bfull, maskf, hsel, avg)


def _fox_fwd(qa, ka, va, side):
    T = qa.shape[1]
    tq = min(TQ, T)
    nq = T // tq
    n = side.n

    def body(qa_ref, ka_ref, va_ref, *rest):
        ins, (o_ref, lse_ref), outs = rest[:n], rest[n:n + 2], rest[n + 2:2 * n + 2]
        acc_ref, *sems = rest[2 * n + 2:]
        i = pl.program_id(1)
        if n:
            @pl.when((pl.program_id(0) == 0) & (i == 0))
            def _():
                side.start(ins, outs, sems)
        row = lax.broadcasted_iota(jnp.int32, (tq, tq), 0)
        col = lax.broadcasted_iota(jnp.int32, (tq, tq), 1)
        qs = [qa_ref[0], qa_ref[1]]

        def tiles(js, carry, diag):
            offs = [pl.multiple_of(j * tq, tq) for j in js]
            logits = [[_dot_nt(qs[hh], ka_ref[hh, pl.ds(off, tq), :]) for hh in range(2)] for off in offs]
            carry = list(carry)
            for off, per_head in zip(offs, logits):
                for hh, s in enumerate(per_head):
                    if diag:
                        s = jnp.where(col <= row, s, NEG)
                    m = carry[hh]
                    m_new = jnp.maximum(m, jnp.max(s, axis=-1, keepdims=True))
                    pr = jnp.exp(s - m_new)
                    acc_ref[hh] = jnp.exp(m - m_new) * acc_ref[hh] + _dot(pr.astype(BF16), va_ref[hh, pl.ds(off, tq), :])
                    carry[hh] = m_new
            return tuple(carry)

        acc_ref[...] = jnp.zeros_like(acc_ref)
        init = (jnp.full((tq, 1), NEG, F32),) * 2
        carry = lax.fori_loop(0, i // FWD_UNROLL, lambda t, cr: tiles([FWD_UNROLL * t + u for u in range(FWD_UNROLL)], cr, False), init)
        carry = lax.fori_loop(i - i % FWD_UNROLL, i, lambda j, cr: tiles([j], cr, False), carry)
        carry = tiles([i], carry, True)
        for hh in range(2):
            m, acc = carry[hh], acc_ref[hh]
            l = acc[:, L_ROW:L_ROW + 1]
            o_ref[:, _head_sl(hh)] = acc[:, :HEAD_DIM] / l
            hi, mid, lo = _split3(-(m + jnp.log(l)))
            lse_ref[hh] = _lanes(tq, LANES, {L_LSE: hi, L_LSE + 1: mid, L_LSE + 2: lo}).astype(BF16)
        if n:
            @pl.when((pl.program_id(0) == N_HEADS // 2 - 1) & (i == nq - 1))
            def _():
                side.finish(ins, outs, sems)

    tile = pl.BlockSpec((2, tq, LANES), lambda p, i: (p, i, 0))
    seq = pl.BlockSpec((2, T, LANES), lambda p, i: (p, 0, 0))
    res = pl.pallas_call(
        body, name="fox_fwd", grid=(N_HEADS // 2, nq),
        in_specs=[tile, seq, seq, *side.specs],
        out_specs=[pl.BlockSpec((tq, LANES), lambda p, i: (i, p)), tile, *side.specs],
        out_shape=[jax.ShapeDtypeStruct((T, D_B), F32), jax.ShapeDtypeStruct((N_HEADS, T, LANES), BF16), *side.out_shape],
        scratch_shapes=[pltpu.VMEM((2, tq, LANES), F32), *side.scratch],
        compiler_params=_cp("arbitrary", "arbitrary"),
    )(qa, ka, va, *side.operands)
    return res[0], res[1], res[2:]


def _fox_bwd(qa, lse, doa, ka, va, side):
    T = qa.shape[1]
    tq = min(TQ, T)
    nq = T // tq
    n = side.n

    def body(qa_ref, lse_ref, doa_ref, ka_ref, va_ref, *rest):
        ins, (dqa_ref, dka_ref, dva_ref), outs = rest[:n], rest[n:n + 3], rest[n + 3:2 * n + 3]
        dv_acc, *sems = rest[2 * n + 3:]
        j = pl.program_id(1)
        if n:
            @pl.when((pl.program_id(0) == 0) & (j == 0))
            def _():
                side.start(ins, outs, sems)

        @pl.when(j == 0)
        def _():
            dqa_ref[...] = jnp.zeros_like(dqa_ref)

        row = lax.broadcasted_iota(jnp.int32, (tq, tq), 0)
        col = lax.broadcasted_iota(jnp.int32, (tq, tq), 1)
        ks = [ka_ref[0], ka_ref[1]]
        vs = [va_ref[0], va_ref[1]]

        def tiles(ids, diag):
            work = []
            for i in ids:
                off = pl.multiple_of(i * tq, tq)
                for hh in range(2):
                    qi = qa_ref[hh, pl.ds(off, tq), :] + lse_ref[hh, pl.ds(off, tq), :]
                    doi = doa_ref[hh, pl.ds(off, tq), :]
                    work.append((off, hh, qi, doi, _dot_nt(ks[hh], qi), _dot_nt(vs[hh], doi)))
            for off, hh, qi, doi, st, dpt in work:
                if diag:
                    st = jnp.where(row <= col, st, NEG)
                pt = jnp.exp(st)
                dv_acc[hh] += _dot(pt.astype(BF16), doi)
                dsb = (pt * dpt).astype(BF16)
                dka_ref[hh] += _dot(dsb, qi)
                dqa_ref[hh, pl.ds(off, tq), :] += _dot_tn(dsb, ks[hh])

        dka_ref[...] = jnp.zeros_like(dka_ref)
        dv_acc[...] = jnp.zeros_like(dv_acc)
        tiles([j], True)
        todo = nq - 1 - j

        @pl.loop(0, todo // BWD_UNROLL)
        def _(t):
            tiles([j + 1 + BWD_UNROLL * t + u for u in range(BWD_UNROLL)], False)

        @pl.loop(nq - todo % BWD_UNROLL, nq)
        def _(i):
            tiles([i], False)

        dva_ref[...] = dv_acc[...].astype(BF16)
        if n:
            @pl.when((pl.program_id(0) == N_HEADS // 2 - 1) & (j == nq - 1))
            def _():
                side.finish(ins, outs, sems)

    tile = pl.BlockSpec((2, tq, LANES), lambda p, j: (p, j, 0))
    seq = pl.BlockSpec((2, T, LANES), lambda p, j: (p, 0, 0))
    res = pl.pallas_call(
        body, name="fox_bwd", grid=(N_HEADS // 2, nq),
        in_specs=[seq, seq, seq, tile, tile, *side.specs],
        out_specs=[seq, tile, tile, *side.specs],
        out_shape=[jax.ShapeDtypeStruct((N_HEADS, T, LANES), F32), jax.ShapeDtypeStruct((N_HEADS, T, LANES), F32),
                   jax.ShapeDtypeStruct((N_HEADS, T, LANES), BF16), *side.out_shape],
        scratch_shapes=[pltpu.VMEM((2, tq, LANES), F32), *side.scratch],
        compiler_params=_cp("arbitrary", "arbitrary"),
    )(qa, lse, doa, ka, va, *side.operands)
    return res[0], res[1], res[2], res[3:]


def _fwd_mid(x, oa, ob, w_out, g2, w_up):
    T = x.shape[0]
    tm = min(TM_WIDE, T)

    def body(x_ref, oa_ref, ob_ref, wo_ref, g_ref, wu_ref, x2_ref, h2_ref, a_ref):
        oab = jnp.concatenate([oa_ref[...], ob_ref[...].astype(BF16)], axis=-1)
        x2 = x_ref[...] + _dot(oab, wo_ref[...])
        x2_ref[...] = x2
        _, n = _rms(x2)
        h2 = (n * g_ref[...]).astype(BF16)
        h2_ref[...] = h2
        a_ref[...] = _dot(h2, wu_ref[...])

    return pl.pallas_call(
        body, name="fwd_mid", grid=(T // tm,),
        in_specs=[_rows(tm, D_MODEL), _rows(tm, D_A), _rows(tm, D_B), _full((D_MODEL, D_MODEL), True), _full((1, D_MODEL)),
                  _full((D_MODEL, D_FF2), True)],
        out_specs=[_rows(tm, D_MODEL), _rows(tm, D_MODEL), _rows(tm, D_FF2)],
        out_shape=[jax.ShapeDtypeStruct((T, D_MODEL), F32), jax.ShapeDtypeStruct((T, D_MODEL), BF16),
                   jax.ShapeDtypeStruct((T, D_FF2), F32)],
        compiler_params=_cp("arbitrary"),
    )(x, oa, ob, w_out, g2, w_up)


def _row_before(x, prev, k):
    rolled = pltpu.roll(x, k, axis=0)
    row = lax.broadcasted_iota(jnp.int32, (8, x.shape[1]), 0)
    head = rolled[0:8]
    for r in range(k):
        head = jnp.where(row == r, prev[8 - k + r:9 - k + r], head)
    return jnp.concatenate([head, rolled[8:]], axis=0)


def _row_after(x, nxt, k):
    tm = x.shape[0]
    rolled = pltpu.roll(x, tm - k, axis=0)
    row = lax.broadcasted_iota(jnp.int32, (8, x.shape[1]), 0)
    tail = rolled[tm - 8:tm]
    for r in range(k):
        tail = jnp.where(row == 8 - k + r, nxt[r:r + 1], tail)
    return jnp.concatenate([rolled[:tm - 8], tail], axis=0)


def _fwd_ffn(a, x2, wc, bc, w_down, g3, tgt):
    T = x2.shape[0]
    tm = min(TM, T)

    def body(a_ref, x2_ref, wc_ref, bc_ref, wd_ref, g_ref, tgt_ref, ac_ref, yff_ref, dx3_ref, loss_ref, dg3_ref, carry):
        @pl.when(pl.program_id(0) == 0)
        def _():
            carry[...] = jnp.zeros_like(carry)
            loss_ref[...] = jnp.zeros_like(loss_ref)
            dg3_ref[...] = jnp.zeros_like(dg3_ref)

        def conv(cs):
            a0 = a_ref[:, cs]
            prev = carry[:, cs]
            ac = (wc_ref[0:1, cs] * _row_before(a0, prev, 2) + wc_ref[1:2, cs] * _row_before(a0, prev, 1)
                  + wc_ref[2:3, cs] * a0 + bc_ref[:, cs])
            ac_ref[:, cs] = ac.astype(BF16)
            return ac

        x3 = x2_ref[...]
        for ci in range(D_FF // CW):
            gs = slice(ci * CW, (ci + 1) * CW)
            ag = conv(gs)
            av = conv(slice(D_FF + ci * CW, D_FF + (ci + 1) * CW))
            yb = (ag * jax.nn.sigmoid(ag) * av).astype(BF16)
            yff_ref[:, gs] = yb
            x3 = x3 + _dot(yb, wd_ref[gs, :])
        carry[...] = a_ref[tm - 8:tm, :]
        r, n = _rms(x3)
        g = g_ref[...]
        diff = n * g - tgt_ref[...]
        loss_ref[...] += (0.5 / D_MODEL) * jnp.sum(diff * diff)
        dout = diff * (1.0 / D_MODEL)
        dg3_ref[...] += jnp.sum(dout * n, axis=0, keepdims=True)
        dx3_ref[...] = _rms_bwd(dout, n, r, g)

    return pl.pallas_call(
        body, name="fwd_ffn", grid=(T // tm,),
        in_specs=[_rows(tm, D_FF2), _rows(tm, D_MODEL), _full((3, D_FF2)), _full((1, D_FF2)), _full((D_FF, D_MODEL), True),
                  _full((1, D_MODEL)), _rows(tm, D_MODEL)],
        out_specs=[_rows(tm, D_FF2), _rows(tm, D_FF), _rows(tm, D_MODEL), _full((8, LANES)), _full((1, D_MODEL))],
        out_shape=[jax.ShapeDtypeStruct((T, D_FF2), BF16), jax.ShapeDtypeStruct((T, D_FF), BF16),
                   jax.ShapeDtypeStruct((T, D_MODEL), F32), jax.ShapeDtypeStruct((8, LANES), F32),
                   jax.ShapeDtypeStruct((1, D_MODEL), F32)],
        scratch_shapes=[pltpu.VMEM((8, D_FF2), F32)],
        compiler_params=_cp("arbitrary"),
    )(a, x2, wc, bc, w_down, g3, tgt)


def _bwd_ffn(dx3, a, ac, yff, h2, w_down, wc):
    T = dx3.shape[0]
    tm = min(TM, T)
    nt = T // tm
    half = D_FF // 2
    shard_up, shard_down = D_FF2 // N_DEV, D_FF // N_DEV

    def body(dx3_ref, ag_ref, av_ref, acg_ref, acv_ref, yff_ref, h2_ref, wd_ref, wcg_ref, wcv_ref,
             dag_ref, dav_ref, dwcg_ref, dwcv_ref, dbcg_ref, dbcv_ref, dwd_ref, dwu_ref,
             nxt, shifted, acc_down, acc_g, acc_v, stage_up, stage_down, sem):
        c, r = pl.program_id(0), pl.program_id(1)

        @pl.when(r == 0)
        def _():
            for ref in (nxt, dwcg_ref, dwcv_ref, dbcg_ref, dbcv_ref, acc_down, acc_g, acc_v):
                ref[...] = jnp.zeros_like(ref)

        dxb = dx3_ref[...].astype(BF16)
        dy_all = _dot_nt(dxb, wd_ref[...])

        def back(a_ref, w_ref, da_ref, dwc_ref, dbc_ref, nx, cs, dac):
            a0 = a_ref[:, cs]
            shifted[0] = _row_after(dac, nxt[:, nx], 1)
            shifted[1] = _row_after(dac, nxt[:, nx], 2)
            dp1, dp2 = shifted[0], shifted[1]
            dbc_ref[:, cs] += jnp.sum(dac, axis=0, keepdims=True)
            dwc_ref[0:1, cs] += jnp.sum(dp2 * a0, axis=0, keepdims=True)
            dwc_ref[1:2, cs] += jnp.sum(dp1 * a0, axis=0, keepdims=True)
            dwc_ref[2:3, cs] += jnp.sum(dac * a0, axis=0, keepdims=True)
            da_ref[:, cs] = (w_ref[2:3, cs] * dac + w_ref[1:2, cs] * dp1 + w_ref[0:1, cs] * dp2).astype(BF16)
            nxt[:, nx] = dac[0:8]

        for ci in range(half // LANES):
            cs = slice(ci * LANES, (ci + 1) * LANES)
            dy = dy_all[:, cs]
            ag, av = acg_ref[:, cs].astype(F32), acv_ref[:, cs].astype(F32)
            sg = jax.nn.sigmoid(ag)
            back(av_ref, wcv_ref, dav_ref, dwcv_ref, dbcv_ref, slice(half + ci * LANES, half + (ci + 1) * LANES), cs,
                 dy * (ag * sg))
            back(ag_ref, wcg_ref, dag_ref, dwcg_ref, dbcg_ref, cs, cs, dy * av * (sg * (1.0 + ag * (1.0 - sg))))

        acc_down[...] += _dot_tn(yff_ref[...], dxb)
        h2 = h2_ref[...]
        acc_g[...] += _dot_tn(h2, dag_ref[...])
        acc_v[...] += _dot_tn(h2, dav_ref[...])

        @pl.when(r == nt - 1)
        def _():
            for s in range(half // shard_down):
                stage_down[...] = acc_down[s * shard_down:(s + 1) * shard_down, :].astype(BF16)
                out = pltpu.make_async_copy(stage_down, dwd_ref.at[(half // shard_down) * c + s], sem)
                out.start()
                out.wait()
            for acc, first in ((acc_g, 0), (acc_v, N_DEV // 2)):
                for s in range(half // shard_up):
                    stage_up[...] = acc[:, s * shard_up:(s + 1) * shard_up].astype(BF16)
                    out = pltpu.make_async_copy(stage_up, dwu_ref.at[first + (half // shard_up) * c + s], sem)
                    out.start()
                    out.wait()

    def cols(width, second_half):
        return pl.BlockSpec((tm, width), lambda c, r: (nt - 1 - r, c + (2 if second_half else 0)))

    def param(rows, second_half):
        return pl.BlockSpec((rows, half), lambda c, r: (0, c + (2 if second_half else 0)))

    tokens = pl.BlockSpec((tm, D_MODEL), lambda c, r: (nt - 1 - r, 0))
    return pl.pallas_call(
        body, name="bwd_ffn", grid=(2, nt),
        in_specs=[tokens, cols(half, False), cols(half, True), cols(half, False), cols(half, True), cols(half, False), tokens,
                  pl.BlockSpec((half, D_MODEL), lambda c, r: (c, 0), pipeline_mode=pl.Buffered(1)),
                  param(3, False), param(3, True)],
        out_specs=[cols(half, False), cols(half, False), param(3, False), param(3, False), param(1, False), param(1, False),
                   ANY, ANY],
        out_shape=[jax.ShapeDtypeStruct((T, D_FF), BF16), jax.ShapeDtypeStruct((T, D_FF), BF16),
                   jax.ShapeDtypeStruct((3, D_FF), F32), jax.ShapeDtypeStruct((3, D_FF), F32),
                   jax.ShapeDtypeStruct((1, D_FF), F32), jax.ShapeDtypeStruct((1, D_FF), F32),
                   jax.ShapeDtypeStruct((N_DEV, shard_down, D_MODEL), BF16), jax.ShapeDtypeStruct((N_DEV, D_MODEL, shard_up), BF16)],
        scratch_shapes=[pltpu.VMEM((8, D_FF), F32), pltpu.VMEM((2, tm, LANES), F32), pltpu.VMEM((half, D_MODEL), F32),
                        pltpu.VMEM((D_MODEL, half), F32), pltpu.VMEM((D_MODEL, half), F32),
                        pltpu.VMEM((D_MODEL, shard_up), BF16), pltpu.VMEM((shard_down, D_MODEL), BF16),
                        pltpu.SemaphoreType.DMA],
        compiler_params=_cp("arbitrary", "arbitrary"),
    )(dx3, a, a, ac, ac, yff, h2, w_down, wc, wc)


def _bwd_mid(da_g, da_v, w_up, x2, g2, dx3, w_out, ob):
    T = x2.shape[0]
    tm = min(TM, T)

    def body(dag_ref, dav_ref, wu_ref, x2_ref, g_ref, dx3_ref, wo_ref, ob_ref, hsum_ref, place_ref,
             dx2_ref, doa_ref, dob_ref, dg2_ref):
        @pl.when(pl.program_id(0) == 0)
        def _():
            dg2_ref[...] = jnp.zeros_like(dg2_ref)

        dh2 = _dot_nt(dag_ref[...], wu_ref[:, :D_FF]) + _dot_nt(dav_ref[...], wu_ref[:, D_FF:])
        r, n = _rms(x2_ref[...])
        dg2_ref[...] += jnp.sum(dh2 * n, axis=0, keepdims=True)
        dx2 = dx3_ref[...] + _rms_bwd(dh2, n, r, g_ref[...])
        dx2_ref[...] = dx2
        doab = _dot_nt(dx2.astype(BF16), wo_ref[...])
        doa_ref[...] = doab[:, :D_A]
        dob = doab[:, D_A:]
        rest, delta = dob.astype(BF16).astype(F32) * ob_ref[...], None
        for _ in range(3):
            piece = rest.astype(BF16)
            term = _dot(piece, hsum_ref[...])
            delta = term if delta is None else delta + term
            rest = rest - piece.astype(F32)
        hi, mid, lo = _split3(-delta)
        parts = jnp.concatenate([hi.astype(BF16), mid.astype(BF16), lo.astype(BF16)], axis=-1)
        placed = _dot(parts, place_ref[...])
        data = lax.broadcasted_iota(jnp.int32, (tm, LANES), 1) < HEAD_DIM
        for hd in range(N_HEADS):
            tile = dob[:, LANES * (hd // 2):LANES * (hd // 2 + 1)]
            rows = tile if hd % 2 == 0 else pltpu.roll(tile, HEAD_DIM, axis=1)
            dob_ref[hd] = jnp.where(data, rows, placed[:, LANES * hd:LANES * (hd + 1)]).astype(BF16)

    hsum = (jnp.arange(D_B)[:, None] // HEAD_DIM == jnp.arange(LANES)[None, :]).astype(BF16)
    j, h = jnp.arange(3 * LANES) // LANES, jnp.arange(3 * LANES) % LANES
    place = ((h[:, None] < N_HEADS)
             & (jnp.arange(N_HEADS * LANES)[None, :] == LANES * h[:, None] + L_ROW + j[:, None])).astype(BF16)
    return pl.pallas_call(
        body, name="bwd_mid", grid=(T // tm,),
        in_specs=[_rows(tm, D_FF), _rows(tm, D_FF), _full((D_MODEL, D_FF2), True), _rows(tm, D_MODEL), _full((1, D_MODEL)),
                  _rows(tm, D_MODEL), _full((D_MODEL, D_MODEL), True), _rows(tm, D_B), _full((D_B, LANES)),
                  _full((3 * LANES, N_HEADS * LANES))],
        out_specs=[_rows(tm, D_MODEL), _rows(tm, D_A), pl.BlockSpec((N_HEADS, tm, LANES), lambda i: (0, i, 0)),
                   _full((1, D_MODEL))],
        out_shape=[jax.ShapeDtypeStruct((T, D_MODEL), F32), jax.ShapeDtypeStruct((T, D_A), F32),
                   jax.ShapeDtypeStruct((N_HEADS, T, LANES), BF16), jax.ShapeDtypeStruct((1, D_MODEL), F32)],
        compiler_params=_cp("arbitrary"),
    )(da_g, da_v, w_up, x2, g2, dx3, w_out, ob, hsum, place)


def _bwd_in(dzuv, dqa, dka, dva, fl, x, dx2, w_in_p, g1):
    T = x.shape[0]
    tm = min(TM, T)
    nt = T // tm

    def body(dzuv_ref, dqa_ref, dka_ref, dva_ref, fl_ref, x_ref, dx2_ref, w_ref, g_ref,
             gx_ref, dz_ref, dg1_ref, dfb_ref, carry):
        @pl.when(pl.program_id(0) == 0)
        def _():
            carry[...] = jnp.zeros_like(carry)
            dg1_ref[...] = jnp.zeros_like(dg1_ref)
            dfb_ref[...] = jnp.zeros_like(dfb_ref)

        dc = _lanes(tm, LANES, {hd: dqa_ref[hd][:, L_ROW:L_ROW + 1] - dka_ref[hd][:, L_COL:L_COL + 1] for hd in range(N_HEADS)})
        later = (lax.broadcasted_iota(jnp.int32, (tm, tm), 1) >= lax.broadcasted_iota(jnp.int32, (tm, tm), 0)).astype(F32)
        dls = _dot_f32(later, dc) + carry[...]
        carry[...] = dls[0:1, :]
        dzf = dls * jax.nn.sigmoid(-fl_ref[...])
        dfb_ref[...] += jnp.sum(dzf, axis=0, keepdims=True)
        data = lax.broadcasted_iota(jnp.int32, (tm, LANES), 1) < HEAD_DIM

        def compact(ref):
            return [jnp.where(data, ref[2 * p], pltpu.roll(ref[2 * p + 1], HEAD_DIM, axis=1)) for p in range(N_HEADS // 2)]

        dz = jnp.concatenate([dzuv_ref[...], *[(t * (HEAD_DIM ** -0.5)).astype(BF16) for t in compact(dqa_ref)],
                              *[t.astype(BF16) for t in compact(dka_ref)], *compact(dva_ref), dzf.astype(BF16)], axis=-1)
        dz_ref[...] = dz
        dh1 = _dot_nt(dz, w_ref[...])
        r, n = _rms(x_ref[...])
        dg1_ref[...] += jnp.sum(dh1 * n, axis=0, keepdims=True)
        gx_ref[...] = dx2_ref[...] + _rms_bwd(dh1, n, r, g_ref[...])

    rv = functools.partial(_rows, tm, rev_nt=nt)
    heads = pl.BlockSpec((N_HEADS, tm, LANES), lambda i: (0, nt - 1 - i, 0))
    return pl.pallas_call(
        body, name="bwd_in", grid=(nt,),
        in_specs=[rv(2 * D_A), heads, heads, heads, rv(LANES), rv(D_MODEL), rv(D_MODEL),
                  _full((D_MODEL, D_IN_PAD), True), _full((1, D_MODEL))],
        out_specs=[rv(D_MODEL), rv(D_IN_PAD), _full((1, D_MODEL)), _full((1, LANES))],
        out_shape=[jax.ShapeDtypeStruct((T, D_MODEL), F32), jax.ShapeDtypeStruct((T, D_IN_PAD), BF16),
                   jax.ShapeDtypeStruct((1, D_MODEL), F32), jax.ShapeDtypeStruct((1, LANES), F32)],
        scratch_shapes=[pltpu.VMEM((1, LANES), F32)],
        compiler_params=_cp("arbitrary"),
    )(dzuv, dqa, dka, dva, fl, x, dx2, w_in_p, g1)


def _matmul_tn(a_parts, b, tmm, tn, tk, name, shard_cols=None, n_valid=None):
    T = b.shape[0]
    widths = [a.shape[1] for a in a_parts]
    M, N = sum(widths), b.shape[1]
    tk = min(tk, T)
    nk = T // tk
    part_w = tmm // len(a_parts)
    n_valid = N if n_valid is None else n_valid

    def body(*refs):
        a_refs, b_ref, o_ref, obf_ref = refs[:len(a_parts)], refs[-3], refs[-2], refs[-1]
        k = pl.program_id(2)

        @pl.when(k == 0)
        def _():
            o_ref[...] = jnp.zeros_like(o_ref)

        a = [r[...].astype(BF16) for r in a_refs]
        o_ref[...] += _dot_tn(a[0] if len(a) == 1 else jnp.concatenate(a, axis=-1), b_ref[...].astype(BF16))

        @pl.when(k == nk - 1)
        def _():
            if shard_cols is None:
                obf_ref[...] = o_ref[...].astype(BF16)
            else:
                for d in range(min(tn, n_valid) // shard_cols):
                    obf_ref[d] = o_ref[:, d * shard_cols:(d + 1) * shard_cols].astype(BF16)

    if shard_cols is None:
        bf_spec, bf_shape = pl.BlockSpec((tmm, tn), lambda i, j, k: (i, j)), (M, N)
    else:
        per_tile = min(tn, n_valid) // shard_cols
        bf_spec, bf_shape = pl.BlockSpec((per_tile, tmm, shard_cols), lambda i, j, k: (j, i, 0)), (N_DEV, M, shard_cols)
    a_specs = [pl.BlockSpec((tk, part_w), lambda i, j, k: (k, i)) for _ in a_parts]
    return pl.pallas_call(
        body, name=name, grid=(M // tmm, N // tn, nk),
        in_specs=[*a_specs, pl.BlockSpec((tk, tn), lambda i, j, k: (k, j))],
        out_specs=[pl.BlockSpec((tmm, tn), lambda i, j, k: (i, j)), bf_spec],
        out_shape=[jax.ShapeDtypeStruct((M, N), F32), jax.ShapeDtypeStruct(bf_shape, BF16)],
        compiler_params=_cp("arbitrary", "arbitrary", "arbitrary"),
    )(*a_parts, b)


class _Exchange:
    def __init__(self, gather, scatter, relay):
        self.n_g, self.n, self.relay = len(gather), len(gather) + len(scatter), relay
        self.operands = [*gather, *scatter]
        self.out_shape = [jax.ShapeDtypeStruct((N_DEV, *g.shape), g.dtype) for g in gather]
        self.out_shape += [jax.ShapeDtypeStruct(s.shape, s.dtype) for s in scatter]
        self.specs = [ANY] * self.n
        n = self.n
        self.scratch = [pltpu.SemaphoreType.DMA((7 * n,)), pltpu.SemaphoreType.DMA((7 * n,)),
                        pltpu.SemaphoreType.DMA((n,))] if n else []

    def _plan(self, ins, outs, sems):
        send_sems, recv_sems, local_sems = sems
        x, y, c = (lax.axis_index(ax) for ax in MESH_AXES)
        me = 4 * x + 2 * y + c
        sibling = (x, y, 1 - c)
        chips = [(1 - x, y), (x, 1 - y), (1 - x, 1 - y)]
        peers = [sibling] + [(*chip, c) for chip in chips] + [(*chip, 1 - c) for chip in chips]

        def index(dev):
            return 4 * dev[0] + 2 * dev[1] + dev[2]

        def remote(k, src, dst, to):
            return pltpu.make_async_remote_copy(src_ref=src, dst_ref=dst, send_sem=send_sems.at[k], recv_sem=recv_sems.at[k],
                                                device_id=to, device_id_type=pl.DeviceIdType.MESH)

        local, sends, relays, recvs = [], [], [], []
        for a in range(self.n):
            src, out, base = ins[a], outs[a], 7 * a
            if a >= self.n_g:
                local.append(pltpu.make_async_copy(src.at[me], out.at[me], local_sems.at[a]))
                sends += [remote(base + k, src.at[index(peer)], out.at[me], peer) for k, peer in enumerate(peers)]
            else:
                local.append(pltpu.make_async_copy(src, out.at[me], local_sems.at[a]))
                sends += [remote(base + k, src, out.at[me], peer) for k, peer in enumerate(peers[:4 if self.relay else 7])]
            for k, peer in enumerate(peers):
                slot = out.at[index(peer)]
                if a < self.n_g and self.relay and k >= 4:
                    continue
                recv = remote(base + k, slot, slot, peer)
                if a < self.n_g and self.relay and k >= 1:
                    relays.append((recv, remote(base + 3 + k, slot, slot, sibling)))
                else:
                    recvs.append(recv)
            if a < self.n_g and self.relay:
                for j, chip in enumerate(chips):
                    slot = out.at[index((*chip, 1 - c))]
                    recvs.append(remote(base + 4 + j, slot, slot, sibling))
        return local, sends, relays, recvs

    def start(self, ins, outs, sems):
        local, sends, _, _ = self._plan(ins, outs, sems)
        for cp in local + sends:
            cp.start()

    def finish(self, ins, outs, sems):
        local, sends, relays, recvs = self._plan(ins, outs, sems)
        for recv, fwd in relays:
            recv.wait_recv()
            fwd.start()
        for recv in recvs:
            recv.wait_recv()
        for cp in sends + [fwd for _, fwd in relays]:
            cp.wait_send()
        for cp in local:
            cp.wait()


def _exchange(gather, scatter, name):
    ex = _Exchange(gather, scatter, relay=True)
    n = ex.n

    def body(*refs):
        ins, outs, sems = refs[:n], refs[n:2 * n], refs[2 * n:]
        ex.start(ins, outs, sems)
        ex.finish(ins, outs, sems)

    return pl.pallas_call(body, name=name, in_specs=ex.specs, out_specs=ex.specs, out_shape=ex.out_shape,
                          scratch_shapes=ex.scratch)(*ex.operands)


def _adamw(w, g, m, v):
    m = ADAM_B1 * m + (1.0 - ADAM_B1) * g
    v = ADAM_B2 * v + (1.0 - ADAM_B2) * jnp.square(g)
    m_hat = m / (1.0 - ADAM_B1 ** ADAM_STEP)
    v_hat = v / (1.0 - ADAM_B2 ** ADAM_STEP)
    delta = -ADAM_LR * (m_hat / (jnp.sqrt(v_hat) + ADAM_EPS) + ADAM_WD * w)
    return delta, m, v


def _adamw_shard(w, m, v, recv, tr, name):
    _, R, C = w.shape

    def body(w_ref, m_ref, v_ref, recv_ref, g_ref, d_ref, nm_ref, nv_ref):
        g = recv_ref[0].astype(F32)
        for d in range(1, N_DEV):
            g = g + recv_ref[d].astype(F32)
        g_ref[...] = g
        d_ref[...], nm_ref[...], nv_ref[...] = _adamw(w_ref[...], g, m_ref[...], v_ref[...])

    blk = pl.BlockSpec((None, tr, C), lambda i: (0, i, 0))
    return pl.pallas_call(
        body, name=name, grid=(R // tr,),
        in_specs=[blk, blk, blk, pl.BlockSpec((N_DEV, tr, C), lambda i: (0, i, 0))],
        out_specs=[blk] * 4, out_shape=[jax.ShapeDtypeStruct((1, R, C), F32)] * 4,
        compiler_params=_cp("arbitrary"),
    )(w, m, v, recv)


def _adamw_small(params, gathered, loss_parts):
    n = len(params)

    def body(*refs):
        ins, gs, loss_ref, outs = refs[:3 * n], refs[3 * n:4 * n], refs[4 * n], refs[4 * n + 1:]
        for p in range(n):
            w_ref, m_ref, v_ref = ins[3 * p:3 * p + 3]
            g = gs[p][0]
            for d in range(1, N_DEV):
                g = g + gs[p][d]
            g = g[..., :w_ref.shape[-1]]
            g_ref, d_ref, nm_ref, nv_ref = outs[4 * p:4 * p + 4]
            g_ref[...] = g
            d_ref[...], nm_ref[...], nv_ref[...] = _adamw(w_ref[...], g, m_ref[...], v_ref[...])
        total = loss_ref[0]
        for d in range(1, N_DEV):
            total = total + loss_ref[d]
        outs[4 * n][...] = total

    out_shape = [jax.ShapeDtypeStruct(w.shape, F32) for w, _, _ in params for _ in range(4)]
    res = pl.pallas_call(body, name="adamw_small", out_shape=[*out_shape, jax.ShapeDtypeStruct((8, LANES), F32)],
                         compiler_params=pltpu.CompilerParams(vmem_limit_bytes=VMEM_LIMIT))(
        *[t for p in params for t in p], *gathered, loss_parts)
    return [res[4 * p:4 * p + 4] for p in range(n)], res[4 * n][0, 0]


def _col_shards(g):
    return jnp.transpose(g.reshape(g.shape[0], N_DEV, -1), (1, 0, 2))


def _row_shards(g):
    return g.reshape(N_DEV, -1, g.shape[1])


def _cols_whole(g):
    return jnp.transpose(g, (1, 0, 2)).reshape(g.shape[1], -1)


def kernel(x, norm_mix_g, w_in, f_bias, sg_ln_g, sg_w, sg_b, w_out, norm_ffn_g, w_up, w_conv, b_conv, w_down, norm_final_g, loss_target, m_norm_mix_g, m_w_in, m_f_bias, m_sg_ln_g, m_sg_w, m_sg_b, m_w_out, m_norm_ffn_g, m_w_up, m_w_conv, m_b_conv, m_w_down, m_norm_final_g, v_norm_mix_g, v_w_in, v_f_bias, v_sg_ln_g, v_sg_w, v_sg_b, v_w_out, v_norm_ffn_g, v_w_up, v_w_conv, v_b_conv, v_w_down, v_norm_final_g):
    xs, tgt = x[0], loss_target[0]
    g1, g2, g3 = norm_mix_g, norm_ffn_g, norm_final_g.reshape(1, D_MODEL)
    lg = sg_ln_g.reshape(1, D_A)
    fb = jnp.pad(f_bias, ((0, 0), (0, LANES - N_HEADS)))
    pos_chunk = jnp.arange(SG_BLOCK) // SG_CHUNK
    maskf = (pos_chunk[:, None] >= pos_chunk[None, :]).astype(F32)
    wm = (sg_w[0] * maskf[None]).astype(BF16)
    wmt = jnp.swapaxes(wm, 1, 2)
    bfull = jnp.repeat(sg_b[0].T, HEAD_DIM, axis=1)
    hsel = jnp.repeat(jnp.eye(N_HEADS, dtype=F32), HEAD_DIM, axis=1)
    avg = (jnp.repeat(hsel, HEAD_DIM, axis=0) * (1.0 / HEAD_DIM)).astype(BF16)

    (win_g,) = _exchange([w_in[0].astype(BF16)], [], "gather_w_in")
    w_in_p = jnp.pad(_cols_whole(win_g), ((0, 0), (0, D_IN_PAD - D_IN)))
    zuv, qa, ka, va, fl, h1 = _fwd_in(xs, g1, w_in_p, fb)
    oa = _gmlp_fwd(zuv, lg, wm, bfull, avg)
    rest = _Exchange([w_out[0].astype(BF16), w_up[0].astype(BF16), w_down[0].astype(BF16), w_conv[0]], [], relay=False)
    ob, lse, (wout_g, wup_g, wdown_g, wc_g) = _fox_fwd(qa, ka, va, rest)
    w_out_f, w_up_f = wout_g.reshape(D_MODEL, D_MODEL), _cols_whole(wup_g)
    w_down_f, wc_f = wdown_g.reshape(D_FF, D_MODEL), _cols_whole(wc_g)

    x2, h2, a = _fwd_mid(xs, oa, ob, w_out_f, g2, w_up_f)
    ac, yff, dx3, loss, dg3 = _fwd_ffn(a, x2, wc_f, b_conv, w_down_f, g3, tgt)
    da_g, da_v, dwc_g, dwc_v, dbc_g, dbc_v, dwdown_bf, dwup_bf = _bwd_ffn(dx3, a, ac, yff, h2, w_down_f, wc_f)
    dwc, dbc = jnp.concatenate([dwc_g, dwc_v], axis=1), jnp.concatenate([dbc_g, dbc_v], axis=1)
    dx2, doa, dob, dg2 = _bwd_mid(da_g, da_v, w_up_f, x2, g2, dx3, w_out_f, ob)
    dzuv, dwm, dsgb, dlg = _gmlp_bwd(zuv, doa, lg, wm, wmt, bfull, maskf, hsel, avg)
    _, dwout_bf = _matmul_tn([oa, ob], dx2, D_MODEL, D_MODEL, 1024, "dw_out")

    early = ("w_out", "w_up", "wc", "w_down")
    wire = [_row_shards(dwout_bf), dwup_bf, _col_shards(dwc).astype(BF16), dwdown_bf]
    small_early = dict(lg=dlg, sg_w=dwm, sg_b=dsgb, g2=dg2, bc=dbc, g3=dg3)
    grads = _Exchange([*small_early.values(), loss], wire, relay=False)
    dqa, dka, dva, got = _fox_bwd(qa, lse, dob, ka, va, grads)
    n_small = len(small_early)
    gathered, loss_parts = dict(zip(small_early, got[:n_small])), got[n_small]
    recv = dict(zip(early, got[n_small + 1:]))

    gx, dz, dg1, dfb = _bwd_in(dzuv, dqa, dka, dva, fl, xs, dx2, w_in_p, g1)
    _, dwin_bf = _matmul_tn([h1], dz, D_MODEL // 2, D_IN_PAD, 1024, "dw_in", shard_cols=D_IN // N_DEV, n_valid=D_IN)
    gathered["g1"], gathered["fb"], recv["w_in"] = _exchange([dg1, dfb], [dwin_bf], "exchange_w_in")

    weights = dict(w_in=(w_in, m_w_in, v_w_in, 256), w_out=(w_out, m_w_out, v_w_out, 128), w_up=(w_up, m_w_up, v_w_up, 256),
                   wc=(w_conv, m_w_conv, v_w_conv, 3), w_down=(w_down, m_w_down, v_w_down, 176))
    res = {n: _adamw_shard(w, m, v, recv[n], tr, "adamw_" + n) for n, (w, m, v, tr) in weights.items()}

    reps = dict(g1=((norm_mix_g, m_norm_mix_g, v_norm_mix_g), (1, D_MODEL)), fb=((f_bias, m_f_bias, v_f_bias), (1, N_HEADS)),
                lg=((sg_ln_g, m_sg_ln_g, v_sg_ln_g), (1, D_A)), sg_w=((sg_w, m_sg_w, v_sg_w), (N_HEADS, SG_BLOCK, SG_BLOCK)),
                sg_b=((sg_b, m_sg_b, v_sg_b), (N_HEADS, SG_BLOCK)), g2=((norm_ffn_g, m_norm_ffn_g, v_norm_ffn_g), (1, D_MODEL)),
                bc=((b_conv, m_b_conv, v_b_conv), (1, D_FF2)), g3=((norm_final_g, m_norm_final_g, v_norm_final_g), (1, D_MODEL)))
    outs, loss_sum = _adamw_small([tuple(t.reshape(shape) for t in wmv) for wmv, shape in reps.values()],
                                  [gathered[n] for n in reps], loss_parts)
    for (n, (wmv, _)), out in zip(reps.items(), outs):
        res[n] = [o.reshape(wmv[0].shape) for o in out]

    names = ("g1", "w_in", "fb", "lg", "sg_w", "sg_b", "w_out", "g2", "w_up", "wc", "bc", "w_down", "g3")
    return (loss_sum, gx[None], *[res[n][0] for n in names], *[res[n][1] for n in names],
            *[res[n][2] for n in names], *[res[n][3] for n in names])
```

```python
import functools
import math

import jax
import jax.numpy as jnp
from jax import lax
from jax.experimental import pallas as pl
from jax.experimental.pallas import tpu as pltpu

F32 = jnp.float32
BF16 = jnp.bfloat16

D_MODEL = 1024
HEAD_DIM = 64
N_HEADS = 8
D_A = 512
D_B = 512
D_IN = 2 * D_A + 3 * D_B + N_HEADS
D_IN_PAD = 2688
D_FF = 2816
D_FF2 = 2 * D_FF
SG_BLOCK = 128
SG_CHUNK = 64
EPS = 1e-6
N_DEV = 8
LANES = 128
NEG = -1e30
VMEM_LIMIT = 56 * 1024 * 1024

ADAM_LR = 0.001
ADAM_B1 = 0.9
ADAM_B2 = 0.999
ADAM_EPS = 1e-08
ADAM_WD = 0.01
ADAM_STEP = 10

TM = 256
TM_WIDE = 512
TQ = 512
FWD_UNROLL = 4
BWD_UNROLL = 2
CW = 256

MESH_AXES = ("x", "y", "c")
ANY = pl.BlockSpec(memory_space=pl.ANY)


def _cp(*sem):
    return pltpu.CompilerParams(dimension_semantics=sem, vmem_limit_bytes=VMEM_LIMIT)


def _dot(a, b):
    return jnp.dot(a, b, preferred_element_type=F32)


def _dot_nt(a, b):
    return lax.dot_general(a, b, (((1,), (1,)), ((), ())), preferred_element_type=F32)


def _dot_tn(a, b):
    return lax.dot_general(a, b, (((0,), (0,)), ((), ())), preferred_element_type=F32)


def _dot_f32(a, b):
    return jnp.dot(a, b, precision=lax.Precision.HIGHEST, preferred_element_type=F32)


def _gelu(z):
    return 0.5 * z * (1.0 + lax.erf(z * (1.0 / math.sqrt(2.0))))


def _gelu_grad(z):
    return 0.5 * (1.0 + lax.erf(z * (1.0 / math.sqrt(2.0)))) + z * jnp.exp(-0.5 * z * z) * (1.0 / math.sqrt(2.0 * math.pi))


def _log_sigmoid(x):
    return jnp.minimum(x, 0.0) - jnp.log1p(jnp.exp(-jnp.abs(x)))


def _rms(x):
    r = lax.rsqrt(jnp.mean(x * x, axis=-1, keepdims=True) + EPS)
    return r, x * r


def _rms_bwd(dy, n, r, g):
    dn = dy * g
    return r * (dn - n * jnp.mean(dn * n, axis=-1, keepdims=True))


def _full(shape, single=False):
    nd = len(shape)
    if single:
        return pl.BlockSpec(shape, lambda *_: (0,) * nd, pipeline_mode=pl.Buffered(1))
    return pl.BlockSpec(shape, lambda *_: (0,) * nd)


def _rows(tm, cols, rev_nt=None):
    if rev_nt is None:
        return pl.BlockSpec((tm, cols), lambda i: (i, 0))
    return pl.BlockSpec((tm, cols), lambda i: (rev_nt - 1 - i, 0))


def _head_sl(h):
    return slice(HEAD_DIM * h, HEAD_DIM * (h + 1))


L_ROW = HEAD_DIM
L_COL = HEAD_DIM + 3
L_LSE = HEAD_DIM + 6


def _split3(x):
    hi = x.astype(BF16).astype(F32)
    mid = (x - hi).astype(BF16).astype(F32)
    lo = (x - hi - mid).astype(BF16).astype(F32)
    return hi, mid, lo


def _lanes(rows, width, parts):
    lane = lax.broadcasted_iota(jnp.int32, (rows, width), 1)
    out = jnp.zeros((rows, width), F32)
    for at, val in parts.items():
        out = jnp.where(lane == at, val, out)
    return out


def _fwd_in(x, g1, w_in_p, fb):
    T = x.shape[0]
    tm = min(TM, T)

    def body(x_ref, g_ref, w_ref, fb_ref, place_ref, zuv_ref, qa_ref, ka_ref, va_ref, fl_ref, h1_ref, carry):
        @pl.when(pl.program_id(0) == 0)
        def _():
            carry[...] = jnp.zeros_like(carry)

        r, n = _rms(x_ref[...])
        h = (n * g_ref[...]).astype(BF16)
        h1_ref[...] = h
        z = _dot(h, w_ref[...])
        zuv_ref[...] = z[:, :2 * D_A]
        o = 2 * D_A
        fl = z[:, o + 3 * D_B:] + fb_ref[...]
        fl_ref[...] = fl
        tri = (lax.broadcasted_iota(jnp.int32, (tm, tm), 0) >= lax.broadcasted_iota(jnp.int32, (tm, tm), 1)).astype(F32)
        c = _dot_f32(tri, _log_sigmoid(fl)) + carry[...]
        carry[...] = c[tm - 1:tm, :]
        hi, mid, lo = _split3(c)
        parts = jnp.concatenate([hi.astype(BF16), mid.astype(BF16), lo.astype(BF16)], axis=-1)
        placed = _dot(parts, place_ref[...])
        lane = lax.broadcasted_iota(jnp.int32, (tm, LANES), 1)
        data = lane < HEAD_DIM
        ones_q = ((lane >= L_COL) & (lane < L_COL + 3)).astype(F32)
        ones_k = (((lane >= L_ROW) & (lane < L_ROW + 3)) | ((lane >= L_LSE) & (lane < L_LSE + 3))).astype(F32)
        ones_v = ((lane >= L_ROW) & (lane < L_ROW + 3)).astype(F32)
        for hd in range(N_HEADS):
            def rows_of(first_col):
                tile = z[:, first_col + LANES * (hd // 2):first_col + LANES * (hd // 2 + 1)]
                return tile if hd % 2 == 0 else pltpu.roll(tile, HEAD_DIM, axis=1)

            hs = slice(LANES * hd, LANES * (hd + 1))
            qa_ref[hd] = jnp.where(data, rows_of(o) * (HEAD_DIM ** -0.5), placed[:, hs] + ones_q).astype(BF16)
            ka_ref[hd] = jnp.where(data, rows_of(o + D_B), ones_k - placed[:, N_HEADS * LANES:][:, hs]).astype(BF16)
            va_ref[hd] = jnp.where(data, rows_of(o + 2 * D_B), ones_v).astype(BF16)

    heads = pl.BlockSpec((N_HEADS, tm, LANES), lambda i: (0, i, 0))
    aug = jax.ShapeDtypeStruct((N_HEADS, T, LANES), BF16)
    j, h = jnp.arange(3 * LANES) // LANES, jnp.arange(3 * LANES) % LANES
    cols = jnp.arange(2 * N_HEADS * LANES)
    place = ((h[:, None] < N_HEADS) & ((cols[None, :] == LANES * h[:, None] + L_ROW + j[:, None])
                                       | (cols[None, :] == N_HEADS * LANES + LANES * h[:, None] + L_COL + j[:, None]))).astype(BF16)
    return pl.pallas_call(
        body, name="fwd_in", grid=(T // tm,),
        in_specs=[_rows(tm, D_MODEL), _full((1, D_MODEL)), _full((D_MODEL, D_IN_PAD), True), _full((1, LANES)),
                  _full((3 * LANES, 2 * N_HEADS * LANES))],
        out_specs=[_rows(tm, 2 * D_A), heads, heads, heads, _rows(tm, LANES), _rows(tm, D_MODEL)],
        out_shape=[jax.ShapeDtypeStruct((T, 2 * D_A), F32), aug, aug, aug, jax.ShapeDtypeStruct((T, LANES), F32),
                   jax.ShapeDtypeStruct((T, D_MODEL), BF16)],
        scratch_shapes=[pltpu.VMEM((1, LANES), F32)],
        compiler_params=_cp("arbitrary"),
    )(x, g1, w_in_p, fb, place)


def _segmean(x, avg_ref, parts):
    out, rest = None, x
    for _ in range(parts):
        piece = rest.astype(BF16)
        term = _dot(piece, avg_ref[...])
        out = term if out is None else out + term
        rest = rest - piece.astype(F32)
    return out


def _layer_norm(v, lg, avg_ref):
    d = v - _segmean(v, avg_ref, 3)
    rstd = lax.rsqrt(_segmean(d * d, avg_ref, 2) + EPS)
    vhat = d * rstd
    return vhat, rstd, (vhat * lg).astype(BF16)


def _mix_heads(w_ref, x, row_slice):
    low = lax.broadcasted_iota(jnp.int32, (SG_BLOCK, LANES), 1) < HEAD_DIM
    tiles = []
    for p in range(N_HEADS // 2):
        xt = x[row_slice, LANES * p:LANES * (p + 1)]
        zero = jnp.zeros_like(xt)
        tiles.append(_dot(w_ref[2 * p], jnp.where(low, xt, zero)) + _dot(w_ref[2 * p + 1], jnp.where(low, zero, xt)))
    return jnp.concatenate(tiles, axis=-1)


def _gmlp_fwd(zuv, lg, wm, bfull, avg):
    T = zuv.shape[0]
    tg = min(TM, T)
    nb = tg // SG_BLOCK

    def body(zuv_ref, lg_ref, wm_ref, b_ref, avg_ref, oa_ref):
        u = _gelu(zuv_ref[:, :D_A])
        _, _, vn = _layer_norm(_gelu(zuv_ref[:, D_A:]), lg_ref[...], avg_ref)
        for n in range(nb):
            rs = slice(n * SG_BLOCK, (n + 1) * SG_BLOCK)
            oa_ref[rs, :] = (u[rs] * (_mix_heads(wm_ref, vn, rs) + b_ref[...])).astype(BF16)

    return pl.pallas_call(
        body, name="gmlp_fwd", grid=(T // tg,),
        in_specs=[_rows(tg, 2 * D_A), _full((1, D_A)), _full((N_HEADS, SG_BLOCK, SG_BLOCK)), _full((SG_BLOCK, D_A)),
                  _full((D_A, D_A))],
        out_specs=_rows(tg, D_A),
        out_shape=jax.ShapeDtypeStruct((T, D_A), BF16),
        compiler_params=_cp("arbitrary"),
    )(zuv, lg, wm, bfull, avg)


def _gmlp_bwd(zuv, doa, lg, wm, wmt, bfull, maskf, hsel, avg):
    T = zuv.shape[0]
    tg = min(TM, T)
    nb = tg // SG_BLOCK
    nt = T // tg

    def body(zuv_ref, doa_ref, lg_ref, wm_ref, wmt_ref, b_ref, mask_ref, hsel_ref, avg_ref,
             dzuv_ref, dwm_ref, dsgb_ref, dlg_ref, dbacc):
        i = pl.program_id(0)

        @pl.when(i == 0)
        def _():
            dwm_ref[...] = jnp.zeros_like(dwm_ref)
            dlg_ref[...] = jnp.zeros_like(dlg_ref)
            dbacc[...] = jnp.zeros_like(dbacc)

        zu, zv = zuv_ref[:, :D_A], zuv_ref[:, D_A:]
        lgv = lg_ref[...]
        vhat, rstd, vn = _layer_norm(_gelu(zv), lgv, avg_ref)
        dmixed = doa_ref[...] * _gelu(zu)
        dmb = dmixed.astype(BF16)
        low = lax.broadcasted_iota(jnp.int32, (SG_BLOCK, LANES), 1) < HEAD_DIM
        mixed, dvn = [], []
        for n in range(nb):
            rs = slice(n * SG_BLOCK, (n + 1) * SG_BLOCK)
            mixed.append(_mix_heads(wm_ref, vn, rs) + b_ref[...])
            dvn.append(_mix_heads(wmt_ref, dmb, rs))
            dbacc[...] += dmixed[rs]
            for p in range(N_HEADS // 2):
                ls = slice(LANES * p, LANES * (p + 1))
                dmt, vnt = dmb[rs, ls], vn[rs, ls]
                zero = jnp.zeros_like(dmt)
                dwm_ref[2 * p] += _dot_nt(jnp.where(low, dmt, zero), vnt) * mask_ref[...]
                dwm_ref[2 * p + 1] += _dot_nt(jnp.where(low, zero, dmt), vnt) * mask_ref[...]
        mixed = jnp.concatenate(mixed, axis=0) if nb > 1 else mixed[0]
        dvn = jnp.concatenate(dvn, axis=0) if nb > 1 else dvn[0]
        dlg_ref[...] += jnp.sum(dvn * vhat, axis=0, keepdims=True)
        dvhat = dvn * lgv
        dv = rstd * (dvhat - _segmean(dvhat, avg_ref, 2) - vhat * _segmean(dvhat * vhat, avg_ref, 2))
        dzuv_ref[:, :D_A] = (doa_ref[...] * mixed * _gelu_grad(zu)).astype(BF16)
        dzuv_ref[:, D_A:] = (dv * _gelu_grad(zv)).astype(BF16)

        @pl.when(i == nt - 1)
        def _():
            dsgb_ref[...] = lax.dot_general(hsel_ref[...], dbacc[...], (((1,), (1,)), ((), ())),
                                            precision=lax.Precision.HIGHEST, preferred_element_type=F32)

    return pl.pallas_call(
        body, name="gmlp_bwd", grid=(nt,),
        in_specs=[_rows(tg, 2 * D_A), _rows(tg, D_A), _full((1, D_A)), _full((N_HEADS, SG_BLOCK, SG_BLOCK)),
                  _full((N_HEADS, SG_BLOCK, SG_BLOCK)), _full((SG_BLOCK, D_A)), _full((SG_BLOCK, SG_BLOCK)),
                  _full((N_HEADS, D_A)), _full((D_A, D_A))],
        out_specs=[_rows(tg, 2 * D_A), _full((N_HEADS, SG_BLOCK, SG_BLOCK)), _full((N_HEADS, SG_BLOCK)), _full((1, D_A))],
        out_shape=[jax.ShapeDtypeStruct((T, 2 * D_A), BF16), jax.ShapeDtypeStruct((N_HEADS, SG_BLOCK, SG_BLOCK), F32),
                   jax.ShapeDtypeStruct((N_HEADS, SG_BLOCK), F32), jax.ShapeDtypeStruct((1, D_A), F32)],
        scratch_shapes=[pltpu.VMEM((SG_BLOCK, D_A), F32)],
        compiler_params=_cp("arbitrary"),
    )(zuv, doa, lg, wm, wmt, bfull, maskf, hsel, avg)


def _fox_fwd(qa, ka, va, side):
    T = qa.shape[1]
    tq = min(TQ, T)
    nq = T // tq
    n = side.n

    def body(qa_ref, ka_ref, va_ref, *rest):
        ins, (o_ref, lse_ref), outs = rest[:n], rest[n:n + 2], rest[n + 2:2 * n + 2]
        acc_ref, *sems = rest[2 * n + 2:]
        i = pl.program_id(1)
        if n:
            @pl.when((pl.program_id(0) == 0) & (i == 0))
            def _():
                side.start(ins, outs, sems)
        row = lax.broadcasted_iota(jnp.int32, (tq, tq), 0)
        col = lax.broadcasted_iota(jnp.int32, (tq, tq), 1)
        qs = [qa_ref[0], qa_ref[1]]

        def tiles(js, carry, diag):
            offs = [pl.multiple_of(j * tq, tq) for j in js]
            logits = [[_dot_nt(qs[hh], ka_ref[hh, pl.ds(off, tq), :]) for hh in range(2)] for off in offs]
            carry = list(carry)
            for off, per_head in zip(offs, logits):
                for hh, s in enumerate(per_head):
                    if diag:
                        s = jnp.where(col <= row, s, NEG)
                    m = carry[hh]
                    m_new = jnp.maximum(m, jnp.max(s, axis=-1, keepdims=True))
                    pr = jnp.exp(s - m_new)
                    acc_ref[hh] = jnp.exp(m - m_new) * acc_ref[hh] + _dot(pr.astype(BF16), va_ref[hh, pl.ds(off, tq), :])
                    carry[hh] = m_new
            return tuple(carry)

        acc_ref[...] = jnp.zeros_like(acc_ref)
        init = (jnp.full((tq, 1), NEG, F32),) * 2
        carry = lax.fori_loop(0, i // FWD_UNROLL, lambda t, cr: tiles([FWD_UNROLL * t + u for u in range(FWD_UNROLL)], cr, False), init)
        carry = lax.fori_loop(i - i % FWD_UNROLL, i, lambda j, cr: tiles([j], cr, False), carry)
        carry = tiles([i], carry, True)
        for hh in range(2):
            m, acc = carry[hh], acc_ref[hh]
            l = acc[:, L_ROW:L_ROW + 1]
            o_ref[:, _head_sl(hh)] = acc[:, :HEAD_DIM] / l
            hi, mid, lo = _split3(-(m + jnp.log(l)))
            lse_ref[hh] = _lanes(tq, LANES, {L_LSE: hi, L_LSE + 1: mid, L_LSE + 2: lo}).astype(BF16)
        if n:
            @pl.when((pl.program_id(0) == N_HEADS // 2 - 1) & (i == nq - 1))
            def _():
                side.finish(ins, outs, sems)

    tile = pl.BlockSpec((2, tq, LANES), lambda p, i: (p, i, 0))
    seq = pl.BlockSpec((2, T, LANES), lambda p, i: (p, 0, 0))
    res = pl.pallas_call(
        body, name="fox_fwd", grid=(N_HEADS // 2, nq),
        in_specs=[tile, seq, seq, *side.specs],
        out_specs=[pl.BlockSpec((tq, LANES), lambda p, i: (i, p)), tile, *side.specs],
        out_shape=[jax.ShapeDtypeStruct((T, D_B), F32), jax.ShapeDtypeStruct((N_HEADS, T, LANES), BF16), *side.out_shape],
        scratch_shapes=[pltpu.VMEM((2, tq, LANES), F32), *side.scratch],
        compiler_params=_cp("arbitrary", "arbitrary"),
    )(qa, ka, va, *side.operands)
    return res[0], res[1], res[2:]


def _fox_bwd(qa, lse, doa, ka, va, side):
    T = qa.shape[1]
    tq = min(TQ, T)
    nq = T // tq
    n = side.n

    def body(qa_ref, lse_ref, doa_ref, ka_ref, va_ref, *rest):
        ins, (dqa_ref, dka_ref, dva_ref), outs = rest[:n], rest[n:n + 3], rest[n + 3:2 * n + 3]
        dv_acc, *sems = rest[2 * n + 3:]
        j = pl.program_id(1)
        if n:
            @pl.when((pl.program_id(0) == 0) & (j == 0))
            def _():
                side.start(ins, outs, sems)

        @pl.when(j == 0)
        def _():
            dqa_ref[...] = jnp.zeros_like(dqa_ref)

        row = lax.broadcasted_iota(jnp.int32, (tq, tq), 0)
        col = lax.broadcasted_iota(jnp.int32, (tq, tq), 1)
        ks = [ka_ref[0], ka_ref[1]]
        vs = [va_ref[0], va_ref[1]]

        def tiles(ids, diag):
            work = []
            for i in ids:
                off = pl.multiple_of(i * tq, tq)
                for hh in range(2):
                    qi = qa_ref[hh, pl.ds(off, tq), :] + lse_ref[hh, pl.ds(off, tq), :]
                    doi = doa_ref[hh, pl.ds(off, tq), :]
                    work.append((off, hh, qi, doi, _dot_nt(ks[hh], qi), _dot_nt(vs[hh], doi)))
            for off, hh, qi, doi, st, dpt in work:
                if diag:
                    st = jnp.where(row <= col, st, NEG)
                pt = jnp.exp(st)
                dv_acc[hh] += _dot(pt.astype(BF16), doi)
                dsb = (pt * dpt).astype(BF16)
                dka_ref[hh] += _dot(dsb, qi)
                dqa_ref[hh, pl.ds(off, tq), :] += _dot_tn(dsb, ks[hh])

        dka_ref[...] = jnp.zeros_like(dka_ref)
        dv_acc[...] = jnp.zeros_like(dv_acc)
        tiles([j], True)
        todo = nq - 1 - j

        @pl.loop(0, todo // BWD_UNROLL)
        def _(t):
            tiles([j + 1 + BWD_UNROLL * t + u for u in range(BWD_UNROLL)], False)

        @pl.loop(nq - todo % BWD_UNROLL, nq)
        def _(i):
            tiles([i], False)

        dva_ref[...] = dv_acc[...].astype(BF16)
        if n:
            @pl.when((pl.program_id(0) == N_HEADS // 2 - 1) & (j == nq - 1))
            def _():
                side.finish(ins, outs, sems)

    tile = pl.BlockSpec((2, tq, LANES), lambda p, j: (p, j, 0))
    seq = pl.BlockSpec((2, T, LANES), lambda p, j: (p, 0, 0))
    res = pl.pallas_call(
        body, name="fox_bwd", grid=(N_HEADS // 2, nq),
        in_specs=[seq, seq, seq, tile, tile, *side.specs],
        out_specs=[seq, tile, tile, *side.specs],
        out_shape=[jax.ShapeDtypeStruct((N_HEADS, T, LANES), F32), jax.ShapeDtypeStruct((N_HEADS, T, LANES), F32),
                   jax.ShapeDtypeStruct((N_HEADS, T, LANES), BF16), *side.out_shape],
        scratch_shapes=[pltpu.VMEM((2, tq, LANES), F32), *side.scratch],
        compiler_params=_cp("arbitrary", "arbitrary"),
    )(qa, lse, doa, ka, va, *side.operands)
    return res[0], res[1], res[2], res[3:]


def _fwd_mid(x, oa, ob, w_out, g2, w_up):
    T = x.shape[0]
    tm = min(TM_WIDE, T)

    def body(x_ref, oa_ref, ob_ref, wo_ref, g_ref, wu_ref, x2_ref, h2_ref, a_ref):
        oab = jnp.concatenate([oa_ref[...], ob_ref[...].astype(BF16)], axis=-1)
        x2 = x_ref[...] + _dot(oab, wo_ref[...])
        x2_ref[...] = x2
        _, n = _rms(x2)
        h2 = (n * g_ref[...]).astype(BF16)
        h2_ref[...] = h2
        a_ref[...] = _dot(h2, wu_ref[...])

    return pl.pallas_call(
        body, name="fwd_mid", grid=(T // tm,),
        in_specs=[_rows(tm, D_MODEL), _rows(tm, D_A), _rows(tm, D_B), _full((D_MODEL, D_MODEL), True), _full((1, D_MODEL)),
                  _full((D_MODEL, D_FF2), True)],
        out_specs=[_rows(tm, D_MODEL), _rows(tm, D_MODEL), _rows(tm, D_FF2)],
        out_shape=[jax.ShapeDtypeStruct((T, D_MODEL), F32), jax.ShapeDtypeStruct((T, D_MODEL), BF16),
                   jax.ShapeDtypeStruct((T, D_FF2), F32)],
        compiler_params=_cp("arbitrary"),
    )(x, oa, ob, w_out, g2, w_up)


def _row_before(x, prev, k):
    rolled = pltpu.roll(x, k, axis=0)
    row = lax.broadcasted_iota(jnp.int32, (8, x.shape[1]), 0)
    head = rolled[0:8]
    for r in range(k):
        head = jnp.where(row == r, prev[8 - k + r:9 - k + r], head)
    return jnp.concatenate([head, rolled[8:]], axis=0)


def _row_after(x, nxt, k):
    tm = x.shape[0]
    rolled = pltpu.roll(x, tm - k, axis=0)
    row = lax.broadcasted_iota(jnp.int32, (8, x.shape[1]), 0)
    tail = rolled[tm - 8:tm]
    for r in range(k):
        tail = jnp.where(row == 8 - k + r, nxt[r:r + 1], tail)
    return jnp.concatenate([rolled[:tm - 8], tail], axis=0)


def _fwd_ffn(a, x2, wc, bc, w_down, g3, tgt):
    T = x2.shape[0]
    tm = min(TM, T)

    def body(a_ref, x2_ref, wc_ref, bc_ref, wd_ref, g_ref, tgt_ref, ac_ref, yff_ref, dx3_ref, loss_ref, dg3_ref, carry):
        @pl.when(pl.program_id(0) == 0)
        def _():
            carry[...] = jnp.zeros_like(carry)
            loss_ref[...] = jnp.zeros_like(loss_ref)
            dg3_ref[...] = jnp.zeros_like(dg3_ref)

        def conv(cs):
            a0 = a_ref[:, cs]
            prev = carry[:, cs]
            ac = (wc_ref[0:1, cs] * _row_before(a0, prev, 2) + wc_ref[1:2, cs] * _row_before(a0, prev, 1)
                  + wc_ref[2:3, cs] * a0 + bc_ref[:, cs])
            ac_ref[:, cs] = ac.astype(BF16)
            return ac

        x3 = x2_ref[...]
        for ci in range(D_FF // CW):
            gs = slice(ci * CW, (ci + 1) * CW)
            ag = conv(gs)
            av = conv(slice(D_FF + ci * CW, D_FF + (ci + 1) * CW))
            yb = (ag * jax.nn.sigmoid(ag) * av).astype(BF16)
            yff_ref[:, gs] = yb
            x3 = x3 + _dot(yb, wd_ref[gs, :])
        carry[...] = a_ref[tm - 8:tm, :]
        r, n = _rms(x3)
        g = g_ref[...]
        diff = n * g - tgt_ref[...]
        loss_ref[...] += (0.5 / D_MODEL) * jnp.sum(diff * diff)
        dout = diff * (1.0 / D_MODEL)
        dg3_ref[...] += jnp.sum(dout * n, axis=0, keepdims=True)
        dx3_ref[...] = _rms_bwd(dout, n, r, g)

    return pl.pallas_call(
        body, name="fwd_ffn", grid=(T // tm,),
        in_specs=[_rows(tm, D_FF2), _rows(tm, D_MODEL), _full((3, D_FF2)), _full((1, D_FF2)), _full((D_FF, D_MODEL), True),
                  _full((1, D_MODEL)), _rows(tm, D_MODEL)],
        out_specs=[_rows(tm, D_FF2), _rows(tm, D_FF), _rows(tm, D_MODEL), _full((8, LANES)), _full((1, D_MODEL))],
        out_shape=[jax.ShapeDtypeStruct((T, D_FF2), BF16), jax.ShapeDtypeStruct((T, D_FF), BF16),
                   jax.ShapeDtypeStruct((T, D_MODEL), F32), jax.ShapeDtypeStruct((8, LANES), F32),
                   jax.ShapeDtypeStruct((1, D_MODEL), F32)],
        scratch_shapes=[pltpu.VMEM((8, D_FF2), F32)],
        compiler_params=_cp("arbitrary"),
    )(a, x2, wc, bc, w_down, g3, tgt)


def _bwd_ffn(dx3, a, ac, yff, h2, w_down, wc):
    T = dx3.shape[0]
    tm = min(TM, T)
    nt = T // tm
    half = D_FF // 2
    shard_up, shard_down = D_FF2 // N_DEV, D_FF // N_DEV

    def body(dx3_ref, ag_ref, av_ref, acg_ref, acv_ref, yff_ref, h2_ref, wd_ref, wcg_ref, wcv_ref,
             dag_ref, dav_ref, dwcg_ref, dwcv_ref, dbcg_ref, dbcv_ref, dwd_ref, dwu_ref,
             nxt, shifted, acc_down, acc_g, acc_v, stage_up, stage_down, sem):
        c, r = pl.program_id(0), pl.program_id(1)

        @pl.when(r == 0)
        def _():
            for ref in (nxt, dwcg_ref, dwcv_ref, dbcg_ref, dbcv_ref, acc_down, acc_g, acc_v):
                ref[...] = jnp.zeros_like(ref)

        dxb = dx3_ref[...].astype(BF16)
        dy_all = _dot_nt(dxb, wd_ref[...])

        def back(a_ref, w_ref, da_ref, dwc_ref, dbc_ref, nx, cs, dac):
            a0 = a_ref[:, cs]
            shifted[0] = _row_after(dac, nxt[:, nx], 1)
            shifted[1] = _row_after(dac, nxt[:, nx], 2)
            dp1, dp2 = shifted[0], shifted[1]
            dbc_ref[:, cs] += jnp.sum(dac, axis=0, keepdims=True)
            dwc_ref[0:1, cs] += jnp.sum(dp2 * a0, axis=0, keepdims=True)
            dwc_ref[1:2, cs] += jnp.sum(dp1 * a0, axis=0, keepdims=True)
            dwc_ref[2:3, cs] += jnp.sum(dac * a0, axis=0, keepdims=True)
            da_ref[:, cs] = (w_ref[2:3, cs] * dac + w_ref[1:2, cs] * dp1 + w_ref[0:1, cs] * dp2).astype(BF16)
            nxt[:, nx] = dac[0:8]

        for ci in range(half // LANES):
            cs = slice(ci * LANES, (ci + 1) * LANES)
            dy = dy_all[:, cs]
            ag, av = acg_ref[:, cs].astype(F32), acv_ref[:, cs].astype(F32)
            sg = jax.nn.sigmoid(ag)
            back(av_ref, wcv_ref, dav_ref, dwcv_ref, dbcv_ref, slice(half + ci * LANES, half + (ci + 1) * LANES), cs,
                 dy * (ag * sg))
            back(ag_ref, wcg_ref, dag_ref, dwcg_ref, dbcg_ref, cs, cs, dy * av * (sg * (1.0 + ag * (1.0 - sg))))

        acc_down[...] += _dot_tn(yff_ref[...], dxb)
        h2 = h2_ref[...]
        acc_g[...] += _dot_tn(h2, dag_ref[...])
        acc_v[...] += _dot_tn(h2, dav_ref[...])

        @pl.when(r == nt - 1)
        def _():
            for s in range(half // shard_down):
                stage_down[...] = acc_down[s * shard_down:(s + 1) * shard_down, :].astype(BF16)
                out = pltpu.make_async_copy(stage_down, dwd_ref.at[(half // shard_down) * c + s], sem)
                out.start()
                out.wait()
            for acc, first in ((acc_g, 0), (acc_v, N_DEV // 2)):
                for s in range(half // shard_up):
                    stage_up[...] = acc[:, s * shard_up:(s + 1) * shard_up].astype(BF16)
                    out = pltpu.make_async_copy(stage_up, dwu_ref.at[first + (half // shard_up) * c + s], sem)
                    out.start()
                    out.wait()

    def cols(width, second_half):
        return pl.BlockSpec((tm, width), lambda c, r: (nt - 1 - r, c + (2 if second_half else 0)))

    def param(rows, second_half):
        return pl.BlockSpec((rows, half), lambda c, r: (0, c + (2 if second_half else 0)))

    tokens = pl.BlockSpec((tm, D_MODEL), lambda c, r: (nt - 1 - r, 0))
    return pl.pallas_call(
        body, name="bwd_ffn", grid=(2, nt),
        in_specs=[tokens, cols(half, False), cols(half, True), cols(half, False), cols(half, True), cols(half, False), tokens,
                  pl.BlockSpec((half, D_MODEL), lambda c, r: (c, 0), pipeline_mode=pl.Buffered(1)),
                  param(3, False), param(3, True)],
        out_specs=[cols(half, False), cols(half, False), param(3, False), param(3, False), param(1, False), param(1, False),
                   ANY, ANY],
        out_shape=[jax.ShapeDtypeStruct((T, D_FF), BF16), jax.ShapeDtypeStruct((T, D_FF), BF16),
                   jax.ShapeDtypeStruct((3, D_FF), F32), jax.ShapeDtypeStruct((3, D_FF), F32),
                   jax.ShapeDtypeStruct((1, D_FF), F32), jax.ShapeDtypeStruct((1, D_FF), F32),
                   jax.ShapeDtypeStruct((N_DEV, shard_down, D_MODEL), BF16), jax.ShapeDtypeStruct((N_DEV, D_MODEL, shard_up), BF16)],
        scratch_shapes=[pltpu.VMEM((8, D_FF), F32), pltpu.VMEM((2, tm, LANES), F32), pltpu.VMEM((half, D_MODEL), F32),
                        pltpu.VMEM((D_MODEL, half), F32), pltpu.VMEM((D_MODEL, half), F32),
                        pltpu.VMEM((D_MODEL, shard_up), BF16), pltpu.VMEM((shard_down, D_MODEL), BF16),
                        pltpu.SemaphoreType.DMA],
        compiler_params=_cp("arbitrary", "arbitrary"),
    )(dx3, a, a, ac, ac, yff, h2, w_down, wc, wc)


def _bwd_mid(da_g, da_v, w_up, x2, g2, dx3, w_out, ob):
    T = x2.shape[0]
    tm = min(TM, T)

    def body(dag_ref, dav_ref, wu_ref, x2_ref, g_ref, dx3_ref, wo_ref, ob_ref, hsum_ref, place_ref,
             dx2_ref, doa_ref, dob_ref, dg2_ref):
        @pl.when(pl.program_id(0) == 0)
        def _():
            dg2_ref[...] = jnp.zeros_like(dg2_ref)

        dh2 = _dot_nt(dag_ref[...], wu_ref[:, :D_FF]) + _dot_nt(dav_ref[...], wu_ref[:, D_FF:])
        r, n = _rms(x2_ref[...])
        dg2_ref[...] += jnp.sum(dh2 * n, axis=0, keepdims=True)
        dx2 = dx3_ref[...] + _rms_bwd(dh2, n, r, g_ref[...])
        dx2_ref[...] = dx2
        doab = _dot_nt(dx2.astype(BF16), wo_ref[...])
        doa_ref[...] = doab[:, :D_A]
        dob = doab[:, D_A:]
        rest, delta = dob.astype(BF16).astype(F32) * ob_ref[...], None
        for _ in range(3):
            piece = rest.astype(BF16)
            term = _dot(piece, hsum_ref[...])
            delta = term if delta is None else delta + term
            rest = rest - piece.astype(F32)
        hi, mid, lo = _split3(-delta)
        parts = jnp.concatenate([hi.astype(BF16), mid.astype(BF16), lo.astype(BF16)], axis=-1)
        placed = _dot(parts, place_ref[...])
        data = lax.broadcasted_iota(jnp.int32, (tm, LANES), 1) < HEAD_DIM
        for hd in range(N_HEADS):
            tile = dob[:, LANES * (hd // 2):LANES * (hd // 2 + 1)]
            rows = tile if hd % 2 == 0 else pltpu.roll(tile, HEAD_DIM, axis=1)
            dob_ref[hd] = jnp.where(data, rows, placed[:, LANES * hd:LANES * (hd + 1)]).astype(BF16)

    hsum = (jnp.arange(D_B)[:, None] // HEAD_DIM == jnp.arange(LANES)[None, :]).astype(BF16)
    j, h = jnp.arange(3 * LANES) // LANES, jnp.arange(3 * LANES) % LANES
    place = ((h[:, None] < N_HEADS)
             & (jnp.arange(N_HEADS * LANES)[None, :] == LANES * h[:, None] + L_ROW + j[:, None])).astype(BF16)
    return pl.pallas_call(
        body, name="bwd_mid", grid=(T // tm,),
        in_specs=[_rows(tm, D_FF), _rows(tm, D_FF), _full((D_MODEL, D_FF2), True), _rows(tm, D_MODEL), _full((1, D_MODEL)),
                  _rows(tm, D_MODEL), _full((D_MODEL, D_MODEL), True), _rows(tm, D_B), _full((D_B, LANES)),
                  _full((3 * LANES, N_HEADS * LANES))],
        out_specs=[_rows(tm, D_MODEL), _rows(tm, D_A), pl.BlockSpec((N_HEADS, tm, LANES), lambda i: (0, i, 0)),
                   _full((1, D_MODEL))],
        out_shape=[jax.ShapeDtypeStruct((T, D_MODEL), F32), jax.ShapeDtypeStruct((T, D_A), F32),
                   jax.ShapeDtypeStruct((N_HEADS, T, LANES), BF16), jax.ShapeDtypeStruct((1, D_MODEL), F32)],
        compiler_params=_cp("arbitrary"),
    )(da_g, da_v, w_up, x2, g2, dx3, w_out, ob, hsum, place)


def _bwd_in(dzuv, dqa, dka, dva, fl, x, dx2, w_in_p, g1):
    T = x.shape[0]
    tm = min(TM, T)
    nt = T // tm

    def body(dzuv_ref, dqa_ref, dka_ref, dva_ref, fl_ref, x_ref, dx2_ref, w_ref, g_ref,
             gx_ref, dz_ref, dg1_ref, dfb_ref, carry):
        @pl.when(pl.program_id(0) == 0)
        def _():
            carry[...] = jnp.zeros_like(carry)
            dg1_ref[...] = jnp.zeros_like(dg1_ref)
            dfb_ref[...] = jnp.zeros_like(dfb_ref)

        dc = _lanes(tm, LANES, {hd: dqa_ref[hd][:, L_ROW:L_ROW + 1] - dka_ref[hd][:, L_COL:L_COL + 1] for hd in range(N_HEADS)})
        later = (lax.broadcasted_iota(jnp.int32, (tm, tm), 1) >= lax.broadcasted_iota(jnp.int32, (tm, tm), 0)).astype(F32)
        dls = _dot_f32(later, dc) + carry[...]
        carry[...] = dls[0:1, :]
        dzf = dls * jax.nn.sigmoid(-fl_ref[...])
        dfb_ref[...] += jnp.sum(dzf, axis=0, keepdims=True)
        data = lax.broadcasted_iota(jnp.int32, (tm, LANES), 1) < HEAD_DIM

        def compact(ref):
            return [jnp.where(data, ref[2 * p], pltpu.roll(ref[2 * p + 1], HEAD_DIM, axis=1)) for p in range(N_HEADS // 2)]

        dz = jnp.concatenate([dzuv_ref[...], *[(t * (HEAD_DIM ** -0.5)).astype(BF16) for t in compact(dqa_ref)],
                              *[t.astype(BF16) for t in compact(dka_ref)], *compact(dva_ref), dzf.astype(BF16)], axis=-1)
        dz_ref[...] = dz
        dh1 = _dot_nt(dz, w_ref[...])
        r, n = _rms(x_ref[...])
        dg1_ref[...] += jnp.sum(dh1 * n, axis=0, keepdims=True)
        gx_ref[...] = dx2_ref[...] + _rms_bwd(dh1, n, r, g_ref[...])

    rv = functools.partial(_rows, tm, rev_nt=nt)
    heads = pl.BlockSpec((N_HEADS, tm, LANES), lambda i: (0, nt - 1 - i, 0))
    return pl.pallas_call(
        body, name="bwd_in", grid=(nt,),
        in_specs=[rv(2 * D_A), heads, heads, heads, rv(LANES), rv(D_MODEL), rv(D_MODEL),
                  _full((D_MODEL, D_IN_PAD), True), _full((1, D_MODEL))],
        out_specs=[rv(D_MODEL), rv(D_IN_PAD), _full((1, D_MODEL)), _full((1, LANES))],
        out_shape=[jax.ShapeDtypeStruct((T, D_MODEL), F32), jax.ShapeDtypeStruct((T, D_IN_PAD), BF16),
                   jax.ShapeDtypeStruct((1, D_MODEL), F32), jax.ShapeDtypeStruct((1, LANES), F32)],
        scratch_shapes=[pltpu.VMEM((1, LANES), F32)],
        compiler_params=_cp("arbitrary"),
    )(dzuv, dqa, dka, dva, fl, x, dx2, w_in_p, g1)


def _matmul_tn(a_parts, b, tmm, tn, tk, name, shard_cols=None, n_valid=None):
    T = b.shape[0]
    widths = [a.shape[1] for a in a_parts]
    M, N = sum(widths), b.shape[1]
    tk = min(tk, T)
    nk = T // tk
    part_w = tmm // len(a_parts)
    n_valid = N if n_valid is None else n_valid

    def body(*refs):
        a_refs, b_ref, o_ref, obf_ref = refs[:len(a_parts)], refs[-3], refs[-2], refs[-1]
        k = pl.program_id(2)

        @pl.when(k == 0)
        def _():
            o_ref[...] = jnp.zeros_like(o_ref)

        a = [r[...].astype(BF16) for r in a_refs]
        o_ref[...] += _dot_tn(a[0] if len(a) == 1 else jnp.concatenate(a, axis=-1), b_ref[...].astype(BF16))

        @pl.when(k == nk - 1)
        def _():
            if shard_cols is None:
                obf_ref[...] = o_ref[...].astype(BF16)
            else:
                for d in range(min(tn, n_valid) // shard_cols):
                    obf_ref[d] = o_ref[:, d * shard_cols:(d + 1) * shard_cols].astype(BF16)

    if shard_cols is None:
        bf_spec, bf_shape = pl.BlockSpec((tmm, tn), lambda i, j, k: (i, j)), (M, N)
    else:
        per_tile = min(tn, n_valid) // shard_cols
        bf_spec, bf_shape = pl.BlockSpec((per_tile, tmm, shard_cols), lambda i, j, k: (j, i, 0)), (N_DEV, M, shard_cols)
    a_specs = [pl.BlockSpec((tk, part_w), lambda i, j, k: (k, i)) for _ in a_parts]
    return pl.pallas_call(
        body, name=name, grid=(M // tmm, N // tn, nk),
        in_specs=[*a_specs, pl.BlockSpec((tk, tn), lambda i, j, k: (k, j))],
        out_specs=[pl.BlockSpec((tmm, tn), lambda i, j, k: (i, j)), bf_spec],
        out_shape=[jax.ShapeDtypeStruct((M, N), F32), jax.ShapeDtypeStruct(bf_shape, BF16)],
        compiler_params=_cp("arbitrary", "arbitrary", "arbitrary"),
    )(*a_parts, b)


class _Exchange:
    def __init__(self, gather, scatter, relay):
        self.n_g, self.n, self.relay = len(gather), len(gather) + len(scatter), relay
        self.operands = [*gather, *scatter]
        self.out_shape = [jax.ShapeDtypeStruct((N_DEV, *g.shape), g.dtype) for g in gather]
        self.out_shape += [jax.ShapeDtypeStruct(s.shape, s.dtype) for s in scatter]
        self.specs = [ANY] * self.n
        n = self.n
        self.scratch = [pltpu.SemaphoreType.DMA((7 * n,)), pltpu.SemaphoreType.DMA((7 * n,)),
                        pltpu.SemaphoreType.DMA((n,))] if n else []

    def _plan(self, ins, outs, sems):
        send_sems, recv_sems, local_sems = sems
        x, y, c = (lax.axis_index(ax) for ax in MESH_AXES)
        me = 4 * x + 2 * y + c
        sibling = (x, y, 1 - c)
        chips = [(1 - x, y), (x, 1 - y), (1 - x, 1 - y)]
        peers = [sibling] + [(*chip, c) for chip in chips] + [(*chip, 1 - c) for chip in chips]

        def index(dev):
            return 4 * dev[0] + 2 * dev[1] + dev[2]

        def remote(k, src, dst, to):
            return pltpu.make_async_remote_copy(src_ref=src, dst_ref=dst, send_sem=send_sems.at[k], recv_sem=recv_sems.at[k],
                                                device_id=to, device_id_type=pl.DeviceIdType.MESH)

        local, sends, relays, recvs = [], [], [], []
        for a in range(self.n):
            src, out, base = ins[a], outs[a], 7 * a
            if a >= self.n_g:
                local.append(pltpu.make_async_copy(src.at[me], out.at[me], local_sems.at[a]))
                sends += [remote(base + k, src.at[index(peer)], out.at[me], peer) for k, peer in enumerate(peers)]
            else:
                local.append(pltpu.make_async_copy(src, out.at[me], local_sems.at[a]))
                sends += [remote(base + k, src, out.at[me], peer) for k, peer in enumerate(peers[:4 if self.relay else 7])]
            for k, peer in enumerate(peers):
                slot = out.at[index(peer)]
                if a < self.n_g and self.relay and k >= 4:
                    continue
                recv = remote(base + k, slot, slot, peer)
                if a < self.n_g and self.relay and k >= 1:
                    relays.append((recv, remote(base + 3 + k, slot, slot, sibling)))
                else:
                    recvs.append(recv)
            if a < self.n_g and self.relay:
                for j, chip in enumerate(chips):
                    slot = out.at[index((*chip, 1 - c))]
                    recvs.append(remote(base + 4 + j, slot, slot, sibling))
        return local, sends, relays, recvs

    def start(self, ins, outs, sems):
        local, sends, _, _ = self._plan(ins, outs, sems)
        for cp in local + sends:
            cp.start()

    def finish(self, ins, outs, sems):
        local, sends, relays, recvs = self._plan(ins, outs, sems)
        for recv, fwd in relays:
            recv.wait_recv()
            fwd.start()
        for recv in recvs:
            recv.wait_recv()
        for cp in sends + [fwd for _, fwd in relays]:
            cp.wait_send()
        for cp in local:
            cp.wait()


def _exchange(gather, scatter, name):
    ex = _Exchange(gather, scatter, relay=True)
    n = ex.n

    def body(*refs):
        ins, outs, sems = refs[:n], refs[n:2 * n], refs[2 * n:]
        ex.start(ins, outs, sems)
        ex.finish(ins, outs, sems)

    return pl.pallas_call(body, name=name, in_specs=ex.specs, out_specs=ex.specs, out_shape=ex.out_shape,
                          scratch_shapes=ex.scratch)(*ex.operands)


def _adamw(w, g, m, v):
    m = ADAM_B1 * m + (1.0 - ADAM_B1) * g
    v = ADAM_B2 * v + (1.0 - ADAM_B2) * jnp.square(g)
    m_hat = m / (1.0 - ADAM_B1 ** ADAM_STEP)
    v_hat = v / (1.0 - ADAM_B2 ** ADAM_STEP)
    delta = -ADAM_LR * (m_hat / (jnp.sqrt(v_hat) + ADAM_EPS) + ADAM_WD * w)
    return delta, m, v


def _adamw_shard(w, m, v, recv, tr, name):
    _, R, C = w.shape

    def body(w_ref, m_ref, v_ref, recv_ref, g_ref, d_ref, nm_ref, nv_ref):
        g = recv_ref[0].astype(F32)
        for d in range(1, N_DEV):
            g = g + recv_ref[d].astype(F32)
        g_ref[...] = g
        d_ref[...], nm_ref[...], nv_ref[...] = _adamw(w_ref[...], g, m_ref[...], v_ref[...])

    blk = pl.BlockSpec((None, tr, C), lambda i: (0, i, 0))
    return pl.pallas_call(
        body, name=name, grid=(R // tr,),
        in_specs=[blk, blk, blk, pl.BlockSpec((N_DEV, tr, C), lambda i: (0, i, 0))],
        out_specs=[blk] * 4, out_shape=[jax.ShapeDtypeStruct((1, R, C), F32)] * 4,
        compiler_params=_cp("arbitrary"),
    )(w, m, v, recv)


def _adamw_small(params, gathered, loss_parts):
    n = len(params)

    def body(*refs):
        ins, gs, loss_ref, outs = refs[:3 * n], refs[3 * n:4 * n], refs[4 * n], refs[4 * n + 1:]
        for p in range(n):
            w_ref, m_ref, v_ref = ins[3 * p:3 * p + 3]
            g = gs[p][0]
            for d in range(1, N_DEV):
                g = g + gs[p][d]
            g = g[..., :w_ref.shape[-1]]
            g_ref, d_ref, nm_ref, nv_ref = outs[4 * p:4 * p + 4]
            g_ref[...] = g
            d_ref[...], nm_ref[...], nv_ref[...] = _adamw(w_ref[...], g, m_ref[...], v_ref[...])
        total = loss_ref[0]
        for d in range(1, N_DEV):
            total = total + loss_ref[d]
        outs[4 * n][...] = total

    out_shape = [jax.ShapeDtypeStruct(w.shape, F32) for w, _, _ in params for _ in range(4)]
    res = pl.pallas_call(body, name="adamw_small", out_shape=[*out_shape, jax.ShapeDtypeStruct((8, LANES), F32)],
                         compiler_params=pltpu.CompilerParams(vmem_limit_bytes=VMEM_LIMIT))(
        *[t for p in params for t in p], *gathered, loss_parts)
    return [res[4 * p:4 * p + 4] for p in range(n)], res[4 * n][0, 0]


def _col_shards(g):
    return jnp.transpose(g.reshape(g.shape[0], N_DEV, -1), (1, 0, 2))


def _row_shards(g):
    return g.reshape(N_DEV, -1, g.shape[1])


def _cols_whole(g):
    return jnp.transpose(g, (1, 0, 2)).reshape(g.shape[1], -1)


def kernel(x, norm_mix_g, w_in, f_bias, sg_ln_g, sg_w, sg_b, w_out, norm_ffn_g, w_up, w_conv, b_conv, w_down, norm_final_g, loss_target, m_norm_mix_g, m_w_in, m_f_bias, m_sg_ln_g, m_sg_w, m_sg_b, m_w_out, m_norm_ffn_g, m_w_up, m_w_conv, m_b_conv, m_w_down, m_norm_final_g, v_norm_mix_g, v_w_in, v_f_bias, v_sg_ln_g, v_sg_w, v_sg_b, v_w_out, v_norm_ffn_g, v_w_up, v_w_conv, v_b_conv, v_w_down, v_norm_final_g):
    xs, tgt = x[0], loss_target[0]
    g1, g2, g3 = norm_mix_g, norm_ffn_g, norm_final_g.reshape(1, D_MODEL)
    lg = sg_ln_g.reshape(1, D_A)
    fb = jnp.pad(f_bias, ((0, 0), (0, LANES - N_HEADS)))
    pos_chunk = jnp.arange(SG_BLOCK) // SG_CHUNK
    maskf = (pos_chunk[:, None] >= pos_chunk[None, :]).astype(F32)
    wm = (sg_w[0] * maskf[None]).astype(BF16)
    wmt = jnp.swapaxes(wm, 1, 2)
    bfull = jnp.repeat(sg_b[0].T, HEAD_DIM, axis=1)
    hsel = jnp.repeat(jnp.eye(N_HEADS, dtype=F32), HEAD_DIM, axis=1)
    avg = (jnp.repeat(hsel, HEAD_DIM, axis=0) * (1.0 / HEAD_DIM)).astype(BF16)

    (win_g,) = _exchange([w_in[0].astype(BF16)], [], "gather_w_in")
    w_in_p = jnp.pad(_cols_whole(win_g), ((0, 0), (0, D_IN_PAD - D_IN)))
    zuv, qa, ka, va, fl, h1 = _fwd_in(xs, g1, w_in_p, fb)
    oa = _gmlp_fwd(zuv, lg, wm, bfull, avg)
    rest = _Exchange([w_out[0].astype(BF16), w_up[0].astype(BF16), w_down[0].astype(BF16), w_conv[0]], [], relay=False)
    ob, lse, (wout_g, wup_g, wdown_g, wc_g) = _fox_fwd(qa, ka, va, rest)
    w_out_f, w_up_f = wout_g.reshape(D_MODEL, D_MODEL), _cols_whole(wup_g)
    w_down_f, wc_f = wdown_g.reshape(D_FF, D_MODEL), _cols_whole(wc_g)

    x2, h2, a = _fwd_mid(xs, oa, ob, w_out_f, g2, w_up_f)
    ac, yff, dx3, loss, dg3 = _fwd_ffn(a, x2, wc_f, b_conv, w_down_f, g3, tgt)
    da_g, da_v, dwc_g, dwc_v, dbc_g, dbc_v, dwdown_bf, dwup_bf = _bwd_ffn(dx3, a, ac, yff, h2, w_down_f, wc_f)
    dwc, dbc = jnp.concatenate([dwc_g, dwc_v], axis=1), jnp.concatenate([dbc_g, dbc_v], axis=1)
    dx2, doa, dob, dg2 = _bwd_mid(da_g, da_v, w_up_f, x2, g2, dx3, w_out_f, ob)
    dzuv, dwm, dsgb, dlg = _gmlp_bwd(zuv, doa, lg, wm, wmt, bfull, maskf, hsel, avg)
    _, dwout_bf = _matmul_tn([oa, ob], dx2, D_MODEL, D_MODEL, 1024, "dw_out")

    early = ("w_out", "w_up", "wc", "w_down")
    wire = [_row_shards(dwout_bf), dwup_bf, _col_shards(dwc).astype(BF16), dwdown_bf]
    small_early = dict(lg=dlg, sg_w=dwm, sg_b=dsgb, g2=dg2, bc=dbc, g3=dg3)
    grads = _Exchange([*small_early.values(), loss], wire, relay=False)
    dqa, dka, dva, got = _fox_bwd(qa, lse, dob, ka, va, grads)
    n_small = len(small_early)
    gathered, loss_parts = dict(zip(small_early, got[:n_small])), got[n_small]
    recv = dict(zip(early, got[n_small + 1:]))

    gx, dz, dg1, dfb = _bwd_in(dzuv, dqa, dka, dva, fl, xs, dx2, w_in_p, g1)
    _, dwin_bf = _matmul_tn([h1], dz, D_MODEL // 2, D_IN_PAD, 1024, "dw_in", shard_cols=D_IN // N_DEV, n_valid=D_IN)
    gathered["g1"], gathered["fb"], recv["w_in"] = _exchange([dg1, dfb], [dwin_bf], "exchange_w_in")

    weights = dict(w_in=(w_in, m_w_in, v_w_in, 256), w_out=(w_out, m_w_out, v_w_out, 128), w_up=(w_up, m_w_up, v_w_up, 256),
                   wc=(w_conv, m_w_conv, v_w_conv, 3), w_down=(w_down, m_w_down, v_w_down, 176))
    res = {n: _adamw_shard(w, m, v, recv[n], tr, "adamw_" + n) for n, (w, m, v, tr) in weights.items()}

    reps = dict(g1=((norm_mix_g, m_norm_mix_g, v_norm_mix_g), (1, D_MODEL)), fb=((f_bias, m_f_bias, v_f_bias), (1, N_HEADS)),
                lg=((sg_ln_g, m_sg_ln_g, v_sg_ln_g), (1, D_A)), sg_w=((sg_w, m_sg_w, v_sg_w), (N_HEADS, SG_BLOCK, SG_BLOCK)),
                sg_b=((sg_b, m_sg_b, v_sg_b), (N_HEADS, SG_BLOCK)), g2=((norm_ffn_g, m_norm_ffn_g, v_norm_ffn_g), (1, D_MODEL)),
                bc=((b_conv, m_b_conv, v_b_conv), (1, D_FF2)), g3=((norm_final_g, m_norm_final_g, v_norm_final_g), (1, D_MODEL)))
    outs, loss_sum = _adamw_small([tuple(t.reshape(shape) for t in wmv) for wmv, shape in reps.values()],
                                  [gathered[n] for n in reps], loss_parts)
    for (n, (wmv, _)), out in zip(reps.items(), outs):
        res[n] = [o.reshape(wmv[0].shape) for o in out]

    names = ("g1", "w_in", "fb", "lg", "sg_w", "sg_b", "w_out", "g2", "w_up", "wc", "bc", "w_down", "g3")
    return (loss_sum, gx[None], *[res[n][0] for n in names], *[res[n][1] for n in names],
            *[res[n][2] for n in names], *[res[n][3] for n in names])
```

```python
import functools
import math

import jax
import jax.numpy as jnp
from jax import lax
from jax.experimental import pallas as pl
from jax.experimental.pallas import tpu as pltpu

F32 = jnp.float32
BF16 = jnp.bfloat16

D_MODEL = 1024
HEAD_DIM = 64
N_HEADS = 8
D_A = 512
D_B = 512
D_IN = 2 * D_A + 3 * D_B + N_HEADS
D_IN_PAD = 2688
D_FF = 2816
D_FF2 = 2 * D_FF
SG_BLOCK = 128
SG_CHUNK = 64
EPS = 1e-6
N_DEV = 8
LANES = 128
NEG = -1e30
VMEM_LIMIT = 56 * 1024 * 1024

ADAM_LR = 0.001
ADAM_B1 = 0.9
ADAM_B2 = 0.999
ADAM_EPS = 1e-08
ADAM_WD = 0.01
ADAM_STEP = 10

TM = 256
TM_WIDE = 512
TQ = 512
FWD_UNROLL = 4
BWD_UNROLL = 2
CW = 256

MESH_AXES = ("x", "y", "c")
ANY = pl.BlockSpec(memory_space=pl.ANY)


def _cp(*sem):
    return pltpu.CompilerParams(dimension_semantics=sem, vmem_limit_bytes=VMEM_LIMIT)


def _dot(a, b):
    return jnp.dot(a, b, preferred_element_type=F32)


def _dot_nt(a, b):
    return lax.dot_general(a, b, (((1,), (1,)), ((), ())), preferred_element_type=F32)


def _dot_tn(a, b):
    return lax.dot_general(a, b, (((0,), (0,)), ((), ())), preferred_element_type=F32)


def _dot_f32(a, b):
    return jnp.dot(a, b, precision=lax.Precision.HIGHEST, preferred_element_type=F32)


def _gelu(z):
    return 0.5 * z * (1.0 + lax.erf(z * (1.0 / math.sqrt(2.0))))


def _gelu_grad(z):
    return 0.5 * (1.0 + lax.erf(z * (1.0 / math.sqrt(2.0)))) + z * jnp.exp(-0.5 * z * z) * (1.0 / math.sqrt(2.0 * math.pi))


def _log_sigmoid(x):
    return jnp.minimum(x, 0.0) - jnp.log1p(jnp.exp(-jnp.abs(x)))


def _rms(x):
    r = lax.rsqrt(jnp.mean(x * x, axis=-1, keepdims=True) + EPS)
    return r, x * r


def _rms_bwd(dy, n, r, g):
    dn = dy * g
    return r * (dn - n * jnp.mean(dn * n, axis=-1, keepdims=True))


def _full(shape, single=False):
    nd = len(shape)
    if single:
        return pl.BlockSpec(shape, lambda *_: (0,) * nd, pipeline_mode=pl.Buffered(1))
    return pl.BlockSpec(shape, lambda *_: (0,) * nd)


def _rows(tm, cols, rev_nt=None):
    if rev_nt is None:
        return pl.BlockSpec((tm, cols), lambda i: (i, 0))
    return pl.BlockSpec((tm, cols), lambda i: (rev_nt - 1 - i, 0))


def _head_sl(h):
    return slice(HEAD_DIM * h, HEAD_DIM * (h + 1))


L_ROW = HEAD_DIM
L_COL = HEAD_DIM + 3
L_LSE = HEAD_DIM + 6


def _split3(x):
    hi = x.astype(BF16).astype(F32)
    mid = (x - hi).astype(BF16).astype(F32)
    lo = (x - hi - mid).astype(BF16).astype(F32)
    return hi, mid, lo


def _lanes(rows, width, parts):
    lane = lax.broadcasted_iota(jnp.int32, (rows, width), 1)
    out = jnp.zeros((rows, width), F32)
    for at, val in parts.items():
        out = jnp.where(lane == at, val, out)
    return out


def _fwd_in(x, g1, w_in_p, fb):
    T = x.shape[0]
    tm = min(TM, T)

    def body(x_ref, g_ref, w_ref, fb_ref, place_ref, zuv_ref, qa_ref, ka_ref, va_ref, fl_ref, h1_ref, carry):
        @pl.when(pl.program_id(0) == 0)
        def _():
            carry[...] = jnp.zeros_like(carry)

        r, n = _rms(x_ref[...])
        h = (n * g_ref[...]).astype(BF16)
        h1_ref[...] = h
        z = _dot(h, w_ref[...])
        zuv_ref[...] = z[:, :2 * D_A]
        o = 2 * D_A
        fl = z[:, o + 3 * D_B:] + fb_ref[...]
        fl_ref[...] = fl
        tri = (lax.broadcasted_iota(jnp.int32, (tm, tm), 0) >= lax.broadcasted_iota(jnp.int32, (tm, tm), 1)).astype(F32)
        c = _dot_f32(tri, _log_sigmoid(fl)) + carry[...]
        carry[...] = c[tm - 1:tm, :]
        hi, mid, lo = _split3(c)
        parts = jnp.concatenate([hi.astype(BF16), mid.astype(BF16), lo.astype(BF16)], axis=-1)
        placed = _dot(parts, place_ref[...])
        lane = lax.broadcasted_iota(jnp.int32, (tm, LANES), 1)
        data = lane < HEAD_DIM
        ones_q = ((lane >= L_COL) & (lane < L_COL + 3)).astype(F32)
        ones_k = (((lane >= L_ROW) & (lane < L_ROW + 3)) | ((lane >= L_LSE) & (lane < L_LSE + 3))).astype(F32)
        ones_v = ((lane >= L_ROW) & (lane < L_ROW + 3)).astype(F32)
        for hd in range(N_HEADS):
            def rows_of(first_col):
                tile = z[:, first_col + LANES * (hd // 2):first_col + LANES * (hd // 2 + 1)]
                return tile if hd % 2 == 0 else pltpu.roll(tile, HEAD_DIM, axis=1)

            hs = slice(LANES * hd, LANES * (hd + 1))
            qa_ref[hd] = jnp.where(data, rows_of(o) * (HEAD_DIM ** -0.5), placed[:, hs] + ones_q).astype(BF16)
            key_side = pltpu.roll(placed[:, hs], L_COL - L_ROW, axis=1)
            ka_ref[hd] = jnp.where(data, rows_of(o + D_B), ones_k - key_side).astype(BF16)
            va_ref[hd] = jnp.where(data, rows_of(o + 2 * D_B), ones_v).astype(BF16)

    heads = pl.BlockSpec((N_HEADS, tm, LANES), lambda i: (0, i, 0))
    aug = jax.ShapeDtypeStruct((N_HEADS, T, LANES), BF16)
    j, h = jnp.arange(3 * LANES) // LANES, jnp.arange(3 * LANES) % LANES
    place = ((h[:, None] < N_HEADS)
             & (jnp.arange(N_HEADS * LANES)[None, :] == LANES * h[:, None] + L_ROW + j[:, None])).astype(BF16)
    return pl.pallas_call(
        body, name="fwd_in", grid=(T // tm,),
        in_specs=[_rows(tm, D_MODEL), _full((1, D_MODEL)), _full((D_MODEL, D_IN_PAD), True), _full((1, LANES)),
                  _full((3 * LANES, N_HEADS * LANES))],
        out_specs=[_rows(tm, 2 * D_A), heads, heads, heads, _rows(tm, LANES), _rows(tm, D_MODEL)],
        out_shape=[jax.ShapeDtypeStruct((T, 2 * D_A), F32), aug, aug, aug, jax.ShapeDtypeStruct((T, LANES), F32),
                   jax.ShapeDtypeStruct((T, D_MODEL), BF16)],
        scratch_shapes=[pltpu.VMEM((1, LANES), F32)],
        compiler_params=_cp("arbitrary"),
    )(x, g1, w_in_p, fb, place)


def _segmean(x, avg_ref, parts):
    out, rest = None, x
    for _ in range(parts):
        piece = rest.astype(BF16)
        term = _dot(piece, avg_ref[...])
        out = term if out is None else out + term
        rest = rest - piece.astype(F32)
    return out


def _layer_norm(v, lg, avg_ref):
    d = v - _segmean(v, avg_ref, 3)
    rstd = lax.rsqrt(_segmean(d * d, avg_ref, 2) + EPS)
    vhat = d * rstd
    return vhat, rstd, (vhat * lg).astype(BF16)


def _mix_heads(w_ref, x, row_slice):
    low = lax.broadcasted_iota(jnp.int32, (SG_BLOCK, LANES), 1) < HEAD_DIM
    tiles = []
    for p in range(N_HEADS // 2):
        xt = x[row_slice, LANES * p:LANES * (p + 1)]
        zero = jnp.zeros_like(xt)
        tiles.append(_dot(w_ref[2 * p], jnp.where(low, xt, zero)) + _dot(w_ref[2 * p + 1], jnp.where(low, zero, xt)))
    return jnp.concatenate(tiles, axis=-1)


def _gmlp_fwd(zuv, lg, wm, bfull, avg):
    T = zuv.shape[0]
    tg = min(TM, T)
    nb = tg // SG_BLOCK

    def body(zuv_ref, lg_ref, wm_ref, b_ref, avg_ref, oa_ref):
        u = _gelu(zuv_ref[:, :D_A])
        _, _, vn = _layer_norm(_gelu(zuv_ref[:, D_A:]), lg_ref[...], avg_ref)
        for n in range(nb):
            rs = slice(n * SG_BLOCK, (n + 1) * SG_BLOCK)
            oa_ref[rs, :] = (u[rs] * (_mix_heads(wm_ref, vn, rs) + b_ref[...])).astype(BF16)

    return pl.pallas_call(
        body, name="gmlp_fwd", grid=(T // tg,),
        in_specs=[_rows(tg, 2 * D_A), _full((1, D_A)), _full((N_HEADS, SG_BLOCK, SG_BLOCK)), _full((SG_BLOCK, D_A)),
                  _full((D_A, D_A))],
        out_specs=_rows(tg, D_A),
        out_shape=jax.ShapeDtypeStruct((T, D_A), BF16),
        compiler_params=_cp("arbitrary"),
    )(zuv, lg, wm, bfull, avg)


def _gmlp_bwd(zuv, doa, lg, wm, wmt, bfull, maskf, hsel, avg):
    T = zuv.shape[0]
    tg = min(TM, T)
    nb = tg // SG_BLOCK
    nt = T // tg

    def body(zuv_ref, doa_ref, lg_ref, wm_ref, wmt_ref, b_ref, mask_ref, hsel_ref, avg_ref,
             dzuv_ref, dwm_ref, dsgb_ref, dlg_ref, dbacc):
        i = pl.program_id(0)

        @pl.when(i == 0)
        def _():
            dwm_ref[...] = jnp.zeros_like(dwm_ref)
            dlg_ref[...] = jnp.zeros_like(dlg_ref)
            dbacc[...] = jnp.zeros_like(dbacc)

        zu, zv = zuv_ref[:, :D_A], zuv_ref[:, D_A:]
        lgv = lg_ref[...]
        vhat, rstd, vn = _layer_norm(_gelu(zv), lgv, avg_ref)
        dmixed = doa_ref[...] * _gelu(zu)
        dmb = dmixed.astype(BF16)
        low = lax.broadcasted_iota(jnp.int32, (SG_BLOCK, LANES), 1) < HEAD_DIM
        mixed, dvn = [], []
        for n in range(nb):
            rs = slice(n * SG_BLOCK, (n + 1) * SG_BLOCK)
            mixed.append(_mix_heads(wm_ref, vn, rs) + b_ref[...])
            dvn.append(_mix_heads(wmt_ref, dmb, rs))
            dbacc[...] += dmixed[rs]
            for p in range(N_HEADS // 2):
                ls = slice(LANES * p, LANES * (p + 1))
                dmt, vnt = dmb[rs, ls], vn[rs, ls]
                zero = jnp.zeros_like(dmt)
                dwm_ref[2 * p] += _dot_nt(jnp.where(low, dmt, zero), vnt) * mask_ref[...]
                dwm_ref[2 * p + 1] += _dot_nt(jnp.where(low, zero, dmt), vnt) * mask_ref[...]
        mixed = jnp.concatenate(mixed, axis=0) if nb > 1 else mixed[0]
        dvn = jnp.concatenate(dvn, axis=0) if nb > 1 else dvn[0]
        dlg_ref[...] += jnp.sum(dvn * vhat, axis=0, keepdims=True)
        dvhat = dvn * lgv
        dv = rstd * (dvhat - _segmean(dvhat, avg_ref, 2) - vhat * _segmean(dvhat * vhat, avg_ref, 2))
        dzuv_ref[:, :D_A] = (doa_ref[...] * mixed * _gelu_grad(zu)).astype(BF16)
        dzuv_ref[:, D_A:] = (dv * _gelu_grad(zv)).astype(BF16)

        @pl.when(i == nt - 1)
        def _():
            dsgb_ref[...] = lax.dot_general(hsel_ref[...], dbacc[...], (((1,), (1,)), ((), ())),
                                            precision=lax.Precision.HIGHEST, preferred_element_type=F32)

    return pl.pallas_call(
        body, name="gmlp_bwd", grid=(nt,),
        in_specs=[_rows(tg, 2 * D_A), _rows(tg, D_A), _full((1, D_A)), _full((N_HEADS, SG_BLOCK, SG_BLOCK)),
                  _full((N_HEADS, SG_BLOCK, SG_BLOCK)), _full((SG_BLOCK, D_A)), _full((SG_BLOCK, SG_BLOCK)),
                  _full((N_HEADS, D_A)), _full((D_A, D_A))],
        out_specs=[_rows(tg, 2 * D_A), _full((N_HEADS, SG_BLOCK, SG_BLOCK)), _full((N_HEADS, SG_BLOCK)), _full((1, D_A))],
        out_shape=[jax.ShapeDtypeStruct((T, 2 * D_A), BF16), jax.ShapeDtypeStruct((N_HEADS, SG_BLOCK, SG_BLOCK), F32),
                   jax.ShapeDtypeStruct((N_HEADS, SG_BLOCK), F32), jax.ShapeDtypeStruct((1, D_A), F32)],
        scratch_shapes=[pltpu.VMEM((SG_BLOCK, D_A), F32)],
        compiler_params=_cp("arbitrary"),
    )(zuv, doa, lg, wm, wmt, bfull, maskf, hsel, avg)


def _fox_fwd(qa, ka, va, side):
    T = qa.shape[1]
    tq = min(TQ, T)
    nq = T // tq
    n = side.n

    def body(qa_ref, ka_ref, va_ref, *rest):
        ins, (o_ref, lse_ref), outs = rest[:n], rest[n:n + 2], rest[n + 2:2 * n + 2]
        acc_ref, *sems = rest[2 * n + 2:]
        i = pl.program_id(1)
        if n:
            @pl.when((pl.program_id(0) == 0) & (i == 0))
            def _():
                side.start(ins, outs, sems)
        row = lax.broadcasted_iota(jnp.int32, (tq, tq), 0)
        col = lax.broadcasted_iota(jnp.int32, (tq, tq), 1)
        qs = [qa_ref[0], qa_ref[1]]

        def tiles(js, carry, diag):
            offs = [pl.multiple_of(j * tq, tq) for j in js]
            logits = [[_dot_nt(qs[hh], ka_ref[hh, pl.ds(off, tq), :]) for hh in range(2)] for off in offs]
            carry = list(carry)
            for off, per_head in zip(offs, logits):
                for hh, s in enumerate(per_head):
                    if diag:
                        s = jnp.where(col <= row, s, NEG)
                    m = carry[hh]
                    m_new = jnp.maximum(m, jnp.max(s, axis=-1, keepdims=True))
                    pr = jnp.exp(s - m_new)
                    acc_ref[hh] = jnp.exp(m - m_new) * acc_ref[hh] + _dot(pr.astype(BF16), va_ref[hh, pl.ds(off, tq), :])
                    carry[hh] = m_new
            return tuple(carry)

        acc_ref[...] = jnp.zeros_like(acc_ref)
        init = (jnp.full((tq, 1), NEG, F32),) * 2
        carry = lax.fori_loop(0, i // FWD_UNROLL, lambda t, cr: tiles([FWD_UNROLL * t + u for u in range(FWD_UNROLL)], cr, False), init)
        carry = lax.fori_loop(i - i % FWD_UNROLL, i, lambda j, cr: tiles([j], cr, False), carry)
        carry = tiles([i], carry, True)
        for hh in range(2):
            m, acc = carry[hh], acc_ref[hh]
            l = acc[:, L_ROW:L_ROW + 1]
            o_ref[:, _head_sl(hh)] = acc[:, :HEAD_DIM] / l
            hi, mid, lo = _split3(-(m + jnp.log(l)))
            lse_ref[hh] = _lanes(tq, LANES, {L_LSE: hi, L_LSE + 1: mid, L_LSE + 2: lo}).astype(BF16)
        if n:
            @pl.when((pl.program_id(0) == N_HEADS // 2 - 1) & (i == nq - 1))
            def _():
                side.finish(ins, outs, sems)

    tile = pl.BlockSpec((2, tq, LANES), lambda p, i: (p, i, 0))
    seq = pl.BlockSpec((2, T, LANES), lambda p, i: (p, 0, 0))
    res = pl.pallas_call(
        body, name="fox_fwd", grid=(N_HEADS // 2, nq),
        in_specs=[tile, seq, seq, *side.specs],
        out_specs=[pl.BlockSpec((tq, LANES), lambda p, i: (i, p)), tile, *side.specs],
        out_shape=[jax.ShapeDtypeStruct((T, D_B), F32), jax.ShapeDtypeStruct((N_HEADS, T, LANES), BF16), *side.out_shape],
        scratch_shapes=[pltpu.VMEM((2, tq, LANES), F32), *side.scratch],
        compiler_params=_cp("arbitrary", "arbitrary"),
    )(qa, ka, va, *side.operands)
    return res[0], res[1], res[2:]


def _fox_bwd(qa, lse, doa, ka, va, side):
    T = qa.shape[1]
    tq = min(TQ, T)
    nq = T // tq
    n = side.n

    def body(qa_ref, lse_ref, doa_ref, ka_ref, va_ref, *rest):
        ins, (dqa_ref, dka_ref, dva_ref), outs = rest[:n], rest[n:n + 3], rest[n + 3:2 * n + 3]
        dv_acc, *sems = rest[2 * n + 3:]
        j = pl.program_id(1)
        if n:
            @pl.when((pl.program_id(0) == 0) & (j == 0))
            def _():
                side.start(ins, outs, sems)

        @pl.when(j == 0)
        def _():
            dqa_ref[...] = jnp.zeros_like(dqa_ref)

        row = lax.broadcasted_iota(jnp.int32, (tq, tq), 0)
        col = lax.broadcasted_iota(jnp.int32, (tq, tq), 1)
        ks = [ka_ref[0], ka_ref[1]]
        vs = [va_ref[0], va_ref[1]]

        def tiles(ids, diag):
            work = []
            for i in ids:
                off = pl.multiple_of(i * tq, tq)
                for hh in range(2):
                    qi = qa_ref[hh, pl.ds(off, tq), :] + lse_ref[hh, pl.ds(off, tq), :]
                    doi = doa_ref[hh, pl.ds(off, tq), :]
                    work.append((off, hh, qi, doi, _dot_nt(ks[hh], qi), _dot_nt(vs[hh], doi)))
            for off, hh, qi, doi, st, dpt in work:
                if diag:
                    st = jnp.where(row <= col, st, NEG)
                pt = jnp.exp(st)
                dv_acc[hh] += _dot(pt.astype(BF16), doi)
                dsb = (pt * dpt).astype(BF16)
                dka_ref[hh] += _dot(dsb, qi)
                dqa_ref[hh, pl.ds(off, tq), :] += _dot_tn(dsb, ks[hh])

        dka_ref[...] = jnp.zeros_like(dka_ref)
        dv_acc[...] = jnp.zeros_like(dv_acc)
        tiles([j], True)
        todo = nq - 1 - j

        @pl.loop(0, todo // BWD_UNROLL)
        def _(t):
            tiles([j + 1 + BWD_UNROLL * t + u for u in range(BWD_UNROLL)], False)

        @pl.loop(nq - todo % BWD_UNROLL, nq)
        def _(i):
            tiles([i], False)

        dva_ref[...] = dv_acc[...].astype(BF16)
        if n:
            @pl.when((pl.program_id(0) == N_HEADS // 2 - 1) & (j == nq - 1))
            def _():
                side.finish(ins, outs, sems)

    tile = pl.BlockSpec((2, tq, LANES), lambda p, j: (p, j, 0))
    seq = pl.BlockSpec((2, T, LANES), lambda p, j: (p, 0, 0))
    res = pl.pallas_call(
        body, name="fox_bwd", grid=(N_HEADS // 2, nq),
        in_specs=[seq, seq, seq, tile, tile, *side.specs],
        out_specs=[seq, tile, tile, *side.specs],
        out_shape=[jax.ShapeDtypeStruct((N_HEADS, T, LANES), F32), jax.ShapeDtypeStruct((N_HEADS, T, LANES), F32),
                   jax.ShapeDtypeStruct((N_HEADS, T, LANES), BF16), *side.out_shape],
        scratch_shapes=[pltpu.VMEM((2, tq, LANES), F32), *side.scratch],
        compiler_params=_cp("arbitrary", "arbitrary"),
    )(qa, lse, doa, ka, va, *side.operands)
    return res[0], res[1], res[2], res[3:]


def _fwd_mid(x, oa, ob, w_out, g2, w_up):
    T = x.shape[0]
    tm = min(TM_WIDE, T)

    def body(x_ref, oa_ref, ob_ref, wo_ref, g_ref, wu_ref, x2_ref, h2_ref, a_ref):
        oab = jnp.concatenate([oa_ref[...], ob_ref[...].astype(BF16)], axis=-1)
        x2 = x_ref[...] + _dot(oab, wo_ref[...])
        x2_ref[...] = x2
        _, n = _rms(x2)
        h2 = (n * g_ref[...]).astype(BF16)
        h2_ref[...] = h2
        a_ref[...] = _dot(h2, wu_ref[...])

    return pl.pallas_call(
        body, name="fwd_mid", grid=(T // tm,),
        in_specs=[_rows(tm, D_MODEL), _rows(tm, D_A), _rows(tm, D_B), _full((D_MODEL, D_MODEL), True), _full((1, D_MODEL)),
                  _full((D_MODEL, D_FF2), True)],
        out_specs=[_rows(tm, D_MODEL), _rows(tm, D_MODEL), _rows(tm, D_FF2)],
        out_shape=[jax.ShapeDtypeStruct((T, D_MODEL), F32), jax.ShapeDtypeStruct((T, D_MODEL), BF16),
                   jax.ShapeDtypeStruct((T, D_FF2), F32)],
        compiler_params=_cp("arbitrary"),
    )(x, oa, ob, w_out, g2, w_up)


def _row_before(x, prev, k):
    rolled = pltpu.roll(x, k, axis=0)
    row = lax.broadcasted_iota(jnp.int32, (8, x.shape[1]), 0)
    head = rolled[0:8]
    for r in range(k):
        head = jnp.where(row == r, prev[8 - k + r:9 - k + r], head)
    return jnp.concatenate([head, rolled[8:]], axis=0)


def _row_after(x, nxt, k):
    tm = x.shape[0]
    rolled = pltpu.roll(x, tm - k, axis=0)
    row = lax.broadcasted_iota(jnp.int32, (8, x.shape[1]), 0)
    tail = rolled[tm - 8:tm]
    for r in range(k):
        tail = jnp.where(row == 8 - k + r, nxt[r:r + 1], tail)
    return jnp.concatenate([rolled[:tm - 8], tail], axis=0)


def _fwd_ffn(a, x2, wc, bc, w_down, g3, tgt):
    T = x2.shape[0]
    tm = min(TM, T)

    def body(a_ref, x2_ref, wc_ref, bc_ref, wd_ref, g_ref, tgt_ref, ac_ref, yff_ref, dx3_ref, loss_ref, dg3_ref, carry):
        @pl.when(pl.program_id(0) == 0)
        def _():
            carry[...] = jnp.zeros_like(carry)
            loss_ref[...] = jnp.zeros_like(loss_ref)
            dg3_ref[...] = jnp.zeros_like(dg3_ref)

        def conv(cs):
            a0 = a_ref[:, cs]
            prev = carry[:, cs]
            ac = (wc_ref[0:1, cs] * _row_before(a0, prev, 2) + wc_ref[1:2, cs] * _row_before(a0, prev, 1)
                  + wc_ref[2:3, cs] * a0 + bc_ref[:, cs])
            ac_ref[:, cs] = ac.astype(BF16)
            return ac

        x3 = x2_ref[...]
        for ci in range(D_FF // CW):
            gs = slice(ci * CW, (ci + 1) * CW)
            ag = conv(gs)
            av = conv(slice(D_FF + ci * CW, D_FF + (ci + 1) * CW))
            yb = (ag * jax.nn.sigmoid(ag) * av).astype(BF16)
            yff_ref[:, gs] = yb
            x3 = x3 + _dot(yb, wd_ref[gs, :])
        carry[...] = a_ref[tm - 8:tm, :]
        r, n = _rms(x3)
        g = g_ref[...]
        diff = n * g - tgt_ref[...]
        loss_ref[...] += (0.5 / D_MODEL) * jnp.sum(diff * diff)
        dout = diff * (1.0 / D_MODEL)
        dg3_ref[...] += jnp.sum(dout * n, axis=0, keepdims=True)
        dx3_ref[...] = _rms_bwd(dout, n, r, g)

    return pl.pallas_call(
        body, name="fwd_ffn", grid=(T // tm,),
        in_specs=[_rows(tm, D_FF2), _rows(tm, D_MODEL), _full((3, D_FF2)), _full((1, D_FF2)), _full((D_FF, D_MODEL), True),
                  _full((1, D_MODEL)), _rows(tm, D_MODEL)],
        out_specs=[_rows(tm, D_FF2), _rows(tm, D_FF), _rows(tm, D_MODEL), _full((8, LANES)), _full((1, D_MODEL))],
        out_shape=[jax.ShapeDtypeStruct((T, D_FF2), BF16), jax.ShapeDtypeStruct((T, D_FF), BF16),
                   jax.ShapeDtypeStruct((T, D_MODEL), F32), jax.ShapeDtypeStruct((8, LANES), F32),
                   jax.ShapeDtypeStruct((1, D_MODEL), F32)],
        scratch_shapes=[pltpu.VMEM((8, D_FF2), F32)],
        compiler_params=_cp("arbitrary"),
    )(a, x2, wc, bc, w_down, g3, tgt)


def _bwd_ffn(dx3, a, ac, yff, h2, w_down, wc):
    T = dx3.shape[0]
    tm = min(TM, T)
    nt = T // tm
    half = D_FF // 2
    shard_up, shard_down = D_FF2 // N_DEV, D_FF // N_DEV

    def body(dx3_ref, ag_ref, av_ref, acg_ref, acv_ref, yff_ref, h2_ref, wd_ref, wcg_ref, wcv_ref,
             dag_ref, dav_ref, dwcg_ref, dwcv_ref, dbcg_ref, dbcv_ref, dwd_ref, dwu_ref,
             nxt, shifted, acc_down, acc_g, acc_v, stage_up, stage_down, sem):
        c, r = pl.program_id(0), pl.program_id(1)

        @pl.when(r == 0)
        def _():
            for ref in (nxt, dwcg_ref, dwcv_ref, dbcg_ref, dbcv_ref, acc_down, acc_g, acc_v):
                ref[...] = jnp.zeros_like(ref)

        dxb = dx3_ref[...].astype(BF16)
        dy_all = _dot_nt(dxb, wd_ref[...])

        def back(a_ref, w_ref, da_ref, dwc_ref, dbc_ref, nx, cs, dac):
            a0 = a_ref[:, cs]
            shifted[0] = _row_after(dac, nxt[:, nx], 1)
            shifted[1] = _row_after(dac, nxt[:, nx], 2)
            dp1, dp2 = shifted[0], shifted[1]
            dbc_ref[:, cs] += jnp.sum(dac, axis=0, keepdims=True)
            dwc_ref[0:1, cs] += jnp.sum(dp2 * a0, axis=0, keepdims=True)
            dwc_ref[1:2, cs] += jnp.sum(dp1 * a0, axis=0, keepdims=True)
            dwc_ref[2:3, cs] += jnp.sum(dac * a0, axis=0, keepdims=True)
            da_ref[:, cs] = (w_ref[2:3, cs] * dac + w_ref[1:2, cs] * dp1 + w_ref[0:1, cs] * dp2).astype(BF16)
            nxt[:, nx] = dac[0:8]

        for ci in range(half // LANES):
            cs = slice(ci * LANES, (ci + 1) * LANES)
            dy = dy_all[:, cs]
            ag, av = acg_ref[:, cs].astype(F32), acv_ref[:, cs].astype(F32)
            sg = jax.nn.sigmoid(ag)
            back(av_ref, wcv_ref, dav_ref, dwcv_ref, dbcv_ref, slice(half + ci * LANES, half + (ci + 1) * LANES), cs,
                 dy * (ag * sg))
            back(ag_ref, wcg_ref, dag_ref, dwcg_ref, dbcg_ref, cs, cs, dy * av * (sg * (1.0 + ag * (1.0 - sg))))

        acc_down[...] += _dot_tn(yff_ref[...], dxb)
        h2 = h2_ref[...]
        acc_g[...] += _dot_tn(h2, dag_ref[...])
        acc_v[...] += _dot_tn(h2, dav_ref[...])

        @pl.when(r == nt - 1)
        def _():
            for s in range(half // shard_down):
                stage_down[...] = acc_down[s * shard_down:(s + 1) * shard_down, :].astype(BF16)
                out = pltpu.make_async_copy(stage_down, dwd_ref.at[(half // shard_down) * c + s], sem)
                out.start()
                out.wait()
            for acc, first in ((acc_g, 0), (acc_v, N_DEV // 2)):
                for s in range(half // shard_up):
                    stage_up[...] = acc[:, s * shard_up:(s + 1) * shard_up].astype(BF16)
                    out = pltpu.make_async_copy(stage_up, dwu_ref.at[first + (half // shard_up) * c + s], sem)
                    out.start()
                    out.wait()

    def cols(width, second_half):
        return pl.BlockSpec((tm, width), lambda c, r: (nt - 1 - r, c + (2 if second_half else 0)))

    def param(rows, second_half):
        return pl.BlockSpec((rows, half), lambda c, r: (0, c + (2 if second_half else 0)))

    tokens = pl.BlockSpec((tm, D_MODEL), lambda c, r: (nt - 1 - r, 0))
    return pl.pallas_call(
        body, name="bwd_ffn", grid=(2, nt),
        in_specs=[tokens, cols(half, False), cols(half, True), cols(half, False), cols(half, True), cols(half, False), tokens,
                  pl.BlockSpec((half, D_MODEL), lambda c, r: (c, 0), pipeline_mode=pl.Buffered(1)),
                  param(3, False), param(3, True)],
        out_specs=[cols(half, False), cols(half, False), param(3, False), param(3, False), param(1, False), param(1, False),
                   ANY, ANY],
        out_shape=[jax.ShapeDtypeStruct((T, D_FF), BF16), jax.ShapeDtypeStruct((T, D_FF), BF16),
                   jax.ShapeDtypeStruct((3, D_FF), F32), jax.ShapeDtypeStruct((3, D_FF), F32),
                   jax.ShapeDtypeStruct((1, D_FF), F32), jax.ShapeDtypeStruct((1, D_FF), F32),
                   jax.ShapeDtypeStruct((N_DEV, shard_down, D_MODEL), BF16), jax.ShapeDtypeStruct((N_DEV, D_MODEL, shard_up), BF16)],
        scratch_shapes=[pltpu.VMEM((8, D_FF), F32), pltpu.VMEM((2, tm, LANES), F32), pltpu.VMEM((half, D_MODEL), F32),
                        pltpu.VMEM((D_MODEL, half), F32), pltpu.VMEM((D_MODEL, half), F32),
                        pltpu.VMEM((D_MODEL, shard_up), BF16), pltpu.VMEM((shard_down, D_MODEL), BF16),
                        pltpu.SemaphoreType.DMA],
        compiler_params=_cp("arbitrary", "arbitrary"),
    )(dx3, a, a, ac, ac, yff, h2, w_down, wc, wc)


def _bwd_mid(da_g, da_v, w_up, x2, g2, dx3, w_out, ob):
    T = x2.shape[0]
    tm = min(TM, T)

    def body(dag_ref, dav_ref, wu_ref, x2_ref, g_ref, dx3_ref, wo_ref, ob_ref, hsum_ref, place_ref,
             dx2_ref, doa_ref, dob_ref, dg2_ref):
        @pl.when(pl.program_id(0) == 0)
        def _():
            dg2_ref[...] = jnp.zeros_like(dg2_ref)

        dh2 = _dot_nt(dag_ref[...], wu_ref[:, :D_FF]) + _dot_nt(dav_ref[...], wu_ref[:, D_FF:])
        r, n = _rms(x2_ref[...])
        dg2_ref[...] += jnp.sum(dh2 * n, axis=0, keepdims=True)
        dx2 = dx3_ref[...] + _rms_bwd(dh2, n, r, g_ref[...])
        dx2_ref[...] = dx2
        doab = _dot_nt(dx2.astype(BF16), wo_ref[...])
        doa_ref[...] = doab[:, :D_A]
        dob = doab[:, D_A:]
        rest, delta = dob.astype(BF16).astype(F32) * ob_ref[...], None
        for _ in range(3):
            piece = rest.astype(BF16)
            term = _dot(piece, hsum_ref[...])
            delta = term if delta is None else delta + term
            rest = rest - piece.astype(F32)
        hi, mid, lo = _split3(-delta)
        parts = jnp.concatenate([hi.astype(BF16), mid.astype(BF16), lo.astype(BF16)], axis=-1)
        placed = _dot(parts, place_ref[...])
        data = lax.broadcasted_iota(jnp.int32, (tm, LANES), 1) < HEAD_DIM
        for hd in range(N_HEADS):
            tile = dob[:, LANES * (hd // 2):LANES * (hd // 2 + 1)]
            rows = tile if hd % 2 == 0 else pltpu.roll(tile, HEAD_DIM, axis=1)
            dob_ref[hd] = jnp.where(data, rows, placed[:, LANES * hd:LANES * (hd + 1)]).astype(BF16)

    hsum = (jnp.arange(D_B)[:, None] // HEAD_DIM == jnp.arange(LANES)[None, :]).astype(BF16)
    j, h = jnp.arange(3 * LANES) // LANES, jnp.arange(3 * LANES) % LANES
    place = ((h[:, None] < N_HEADS)
             & (jnp.arange(N_HEADS * LANES)[None, :] == LANES * h[:, None] + L_ROW + j[:, None])).astype(BF16)
    return pl.pallas_call(
        body, name="bwd_mid", grid=(T // tm,),
        in_specs=[_rows(tm, D_FF), _rows(tm, D_FF), _full((D_MODEL, D_FF2), True), _rows(tm, D_MODEL), _full((1, D_MODEL)),
                  _rows(tm, D_MODEL), _full((D_MODEL, D_MODEL), True), _rows(tm, D_B), _full((D_B, LANES)),
                  _full((3 * LANES, N_HEADS * LANES))],
        out_specs=[_rows(tm, D_MODEL), _rows(tm, D_A), pl.BlockSpec((N_HEADS, tm, LANES), lambda i: (0, i, 0)),
                   _full((1, D_MODEL))],
        out_shape=[jax.ShapeDtypeStruct((T, D_MODEL), F32), jax.ShapeDtypeStruct((T, D_A), F32),
                   jax.ShapeDtypeStruct((N_HEADS, T, LANES), BF16), jax.ShapeDtypeStruct((1, D_MODEL), F32)],
        compiler_params=_cp("arbitrary"),
    )(da_g, da_v, w_up, x2, g2, dx3, w_out, ob, hsum, place)


def _bwd_in(dzuv, dqa, dka, dva, fl, x, dx2, w_in_p, g1):
    T = x.shape[0]
    tm = min(TM, T)
    nt = T // tm

    def body(dzuv_ref, dqa_ref, dka_ref, dva_ref, fl_ref, x_ref, dx2_ref, w_ref, g_ref,
             gx_ref, dz_ref, dg1_ref, dfb_ref, carry):
        @pl.when(pl.program_id(0) == 0)
        def _():
            carry[...] = jnp.zeros_like(carry)
            dg1_ref[...] = jnp.zeros_like(dg1_ref)
            dfb_ref[...] = jnp.zeros_like(dfb_ref)

        dc = _lanes(tm, LANES, {hd: dqa_ref[hd][:, L_ROW:L_ROW + 1] - dka_ref[hd][:, L_COL:L_COL + 1] for hd in range(N_HEADS)})
        later = (lax.broadcasted_iota(jnp.int32, (tm, tm), 1) >= lax.broadcasted_iota(jnp.int32, (tm, tm), 0)).astype(F32)
        dls = _dot_f32(later, dc) + carry[...]
        carry[...] = dls[0:1, :]
        dzf = dls * jax.nn.sigmoid(-fl_ref[...])
        dfb_ref[...] += jnp.sum(dzf, axis=0, keepdims=True)
        data = lax.broadcasted_iota(jnp.int32, (tm, LANES), 1) < HEAD_DIM

        def compact(ref, scale=None):
            def rows(hd):
                return (ref[hd] if scale is None else ref[hd] * scale).astype(BF16)

            return [jnp.where(data, rows(2 * p), pltpu.roll(rows(2 * p + 1), HEAD_DIM, axis=1)) for p in range(N_HEADS // 2)]

        dz = jnp.concatenate([dzuv_ref[...], *compact(dqa_ref, HEAD_DIM ** -0.5), *compact(dka_ref), *compact(dva_ref),
                              dzf.astype(BF16)], axis=-1)
        dz_ref[...] = dz
        dh1 = _dot_nt(dz, w_ref[...])
        r, n = _rms(x_ref[...])
        dg1_ref[...] += jnp.sum(dh1 * n, axis=0, keepdims=True)
        gx_ref[...] = dx2_ref[...] + _rms_bwd(dh1, n, r, g_ref[...])

    rv = functools.partial(_rows, tm, rev_nt=nt)
    heads = pl.BlockSpec((N_HEADS, tm, LANES), lambda i: (0, nt - 1 - i, 0))
    return pl.pallas_call(
        body, name="bwd_in", grid=(nt,),
        in_specs=[rv(2 * D_A), heads, heads, heads, rv(LANES), rv(D_MODEL), rv(D_MODEL),
                  _full((D_MODEL, D_IN_PAD), True), _full((1, D_MODEL))],
        out_specs=[rv(D_MODEL), rv(D_IN_PAD), _full((1, D_MODEL)), _full((1, LANES))],
        out_shape=[jax.ShapeDtypeStruct((T, D_MODEL), F32), jax.ShapeDtypeStruct((T, D_IN_PAD), BF16),
                   jax.ShapeDtypeStruct((1, D_MODEL), F32), jax.ShapeDtypeStruct((1, LANES), F32)],
        scratch_shapes=[pltpu.VMEM((1, LANES), F32)],
        compiler_params=_cp("arbitrary"),
    )(dzuv, dqa, dka, dva, fl, x, dx2, w_in_p, g1)


def _matmul_tn(a_parts, b, tmm, tn, tk, name, shard_cols=None, n_valid=None):
    T = b.shape[0]
    widths = [a.shape[1] for a in a_parts]
    M, N = sum(widths), b.shape[1]
    tk = min(tk, T)
    nk = T // tk
    part_w = tmm // len(a_parts)
    n_valid = N if n_valid is None else n_valid

    def body(*refs):
        a_refs, b_ref, o_ref, obf_ref = refs[:len(a_parts)], refs[-3], refs[-2], refs[-1]
        k = pl.program_id(2)

        @pl.when(k == 0)
        def _():
            o_ref[...] = jnp.zeros_like(o_ref)

        a = [r[...].astype(BF16) for r in a_refs]
        o_ref[...] += _dot_tn(a[0] if len(a) == 1 else jnp.concatenate(a, axis=-1), b_ref[...].astype(BF16))

        @pl.when(k == nk - 1)
        def _():
            if shard_cols is None:
                obf_ref[...] = o_ref[...].astype(BF16)
            else:
                for d in range(min(tn, n_valid) // shard_cols):
                    obf_ref[d] = o_ref[:, d * shard_cols:(d + 1) * shard_cols].astype(BF16)

    if shard_cols is None:
        bf_spec, bf_shape = pl.BlockSpec((tmm, tn), lambda i, j, k: (i, j)), (M, N)
    else:
        per_tile = min(tn, n_valid) // shard_cols
        bf_spec, bf_shape = pl.BlockSpec((per_tile, tmm, shard_cols), lambda i, j, k: (j, i, 0)), (N_DEV, M, shard_cols)
    a_specs = [pl.BlockSpec((tk, part_w), lambda i, j, k: (k, i)) for _ in a_parts]
    return pl.pallas_call(
        body, name=name, grid=(M // tmm, N // tn, nk),
        in_specs=[*a_specs, pl.BlockSpec((tk, tn), lambda i, j, k: (k, j))],
        out_specs=[pl.BlockSpec((tmm, tn), lambda i, j, k: (i, j)), bf_spec],
        out_shape=[jax.ShapeDtypeStruct((M, N), F32), jax.ShapeDtypeStruct(bf_shape, BF16)],
        compiler_params=_cp("arbitrary", "arbitrary", "arbitrary"),
    )(*a_parts, b)


class _Exchange:
    def __init__(self, gather, scatter, relay):
        self.n_g, self.n, self.relay = len(gather), len(gather) + len(scatter), relay
        self.operands = [*gather, *scatter]
        self.out_shape = [jax.ShapeDtypeStruct((N_DEV, *g.shape), g.dtype) for g in gather]
        self.out_shape += [jax.ShapeDtypeStruct(s.shape, s.dtype) for s in scatter]
        self.specs = [ANY] * self.n
        n = self.n
        self.scratch = [pltpu.SemaphoreType.DMA((7 * n,)), pltpu.SemaphoreType.DMA((7 * n,)),
                        pltpu.SemaphoreType.DMA((n,))] if n else []

    def _plan(self, ins, outs, sems):
        send_sems, recv_sems, local_sems = sems
        x, y, c = (lax.axis_index(ax) for ax in MESH_AXES)
        me = 4 * x + 2 * y + c
        sibling = (x, y, 1 - c)
        chips = [(1 - x, y), (x, 1 - y), (1 - x, 1 - y)]
        peers = [sibling] + [(*chip, c) for chip in chips] + [(*chip, 1 - c) for chip in chips]

        def index(dev):
            return 4 * dev[0] + 2 * dev[1] + dev[2]

        def remote(k, src, dst, to):
            return pltpu.make_async_remote_copy(src_ref=src, dst_ref=dst, send_sem=send_sems.at[k], recv_sem=recv_sems.at[k],
                                                device_id=to, device_id_type=pl.DeviceIdType.MESH)

        local, sends, relays, recvs = [], [], [], []
        for a in range(self.n):
            src, out, base = ins[a], outs[a], 7 * a
            if a >= self.n_g:
                local.append(pltpu.make_async_copy(src.at[me], out.at[me], local_sems.at[a]))
                sends += [remote(base + k, src.at[index(peer)], out.at[me], peer) for k, peer in enumerate(peers)]
            else:
                local.append(pltpu.make_async_copy(src, out.at[me], local_sems.at[a]))
                sends += [remote(base + k, src, out.at[me], peer) for k, peer in enumerate(peers[:4 if self.relay else 7])]
            for k, peer in enumerate(peers):
                slot = out.at[index(peer)]
                if a < self.n_g and self.relay and k >= 4:
                    continue
                recv = remote(base + k, slot, slot, peer)
                if a < self.n_g and self.relay and k >= 1:
                    relays.append((recv, remote(base + 3 + k, slot, slot, sibling)))
                else:
                    recvs.append(recv)
            if a < self.n_g and self.relay:
                for j, chip in enumerate(chips):
                    slot = out.at[index((*chip, 1 - c))]
                    recvs.append(remote(base + 4 + j, slot, slot, sibling))
        return local, sends, relays, recvs

    def start(self, ins, outs, sems):
        local, sends, _, _ = self._plan(ins, outs, sems)
        for cp in local + sends:
            cp.start()

    def finish(self, ins, outs, sems):
        local, sends, relays, recvs = self._plan(ins, outs, sems)
        for recv, fwd in relays:
            recv.wait_recv()
            fwd.start()
        for recv in recvs:
            recv.wait_recv()
        for cp in sends + [fwd for _, fwd in relays]:
            cp.wait_send()
        for cp in local:
            cp.wait()


def _exchange(gather, scatter, name):
    ex = _Exchange(gather, scatter, relay=True)
    n = ex.n

    def body(*refs):
        ins, outs, sems = refs[:n], refs[n:2 * n], refs[2 * n:]
        ex.start(ins, outs, sems)
        ex.finish(ins, outs, sems)

    return pl.pallas_call(body, name=name, in_specs=ex.specs, out_specs=ex.specs, out_shape=ex.out_shape,
                          scratch_shapes=ex.scratch)(*ex.operands)


def _adamw(w, g, m, v):
    m = ADAM_B1 * m + (1.0 - ADAM_B1) * g
    v = ADAM_B2 * v + (1.0 - ADAM_B2) * jnp.square(g)
    m_hat = m / (1.0 - ADAM_B1 ** ADAM_STEP)
    v_hat = v / (1.0 - ADAM_B2 ** ADAM_STEP)
    delta = -ADAM_LR * (m_hat / (jnp.sqrt(v_hat) + ADAM_EPS) + ADAM_WD * w)
    return delta, m, v


def _adamw_shard(w, m, v, recv, tr, name):
    _, R, C = w.shape

    def body(w_ref, m_ref, v_ref, recv_ref, g_ref, d_ref, nm_ref, nv_ref):
        g = recv_ref[0].astype(F32)
        for d in range(1, N_DEV):
            g = g + recv_ref[d].astype(F32)
        g_ref[...] = g
        d_ref[...], nm_ref[...], nv_ref[...] = _adamw(w_ref[...], g, m_ref[...], v_ref[...])

    blk = pl.BlockSpec((None, tr, C), lambda i: (0, i, 0))
    return pl.pallas_call(
        body, name=name, grid=(R // tr,),
        in_specs=[blk, blk, blk, pl.BlockSpec((N_DEV, tr, C), lambda i: (0, i, 0))],
        out_specs=[blk] * 4, out_shape=[jax.ShapeDtypeStruct((1, R, C), F32)] * 4,
        compiler_params=_cp("arbitrary"),
    )(w, m, v, recv)


def _adamw_small(params, gathered, loss_parts):
    n = len(params)

    def body(*refs):
        ins, gs, loss_ref, outs = refs[:3 * n], refs[3 * n:4 * n], refs[4 * n], refs[4 * n + 1:]
        for p in range(n):
            w_ref, m_ref, v_ref = ins[3 * p:3 * p + 3]
            g = gs[p][0]
            for d in range(1, N_DEV):
                g = g + gs[p][d]
            g = g[..., :w_ref.shape[-1]]
            g_ref, d_ref, nm_ref, nv_ref = outs[4 * p:4 * p + 4]
            g_ref[...] = g
            d_ref[...], nm_ref[...], nv_ref[...] = _adamw(w_ref[...], g, m_ref[...], v_ref[...])
        total = loss_ref[0]
        for d in range(1, N_DEV):
            total = total + loss_ref[d]
        outs[4 * n][...] = total

    out_shape = [jax.ShapeDtypeStruct(w.shape, F32) for w, _, _ in params for _ in range(4)]
    res = pl.pallas_call(body, name="adamw_small", out_shape=[*out_shape, jax.ShapeDtypeStruct((8, LANES), F32)],
                         compiler_params=pltpu.CompilerParams(vmem_limit_bytes=VMEM_LIMIT))(
        *[t for p in params for t in p], *gathered, loss_parts)
    return [res[4 * p:4 * p + 4] for p in range(n)], res[4 * n][0, 0]


def _col_shards(g):
    return jnp.transpose(g.reshape(g.shape[0], N_DEV, -1), (1, 0, 2))


def _row_shards(g):
    return g.reshape(N_DEV, -1, g.shape[1])


def _cols_whole(g):
    return jnp.transpose(g, (1, 0, 2)).reshape(g.shape[1], -1)


def kernel(x, norm_mix_g, w_in, f_bias, sg_ln_g, sg_w, sg_b, w_out, norm_ffn_g, w_up, w_conv, b_conv, w_down, norm_final_g, loss_target, m_norm_mix_g, m_w_in, m_f_bias, m_sg_ln_g, m_sg_w, m_sg_b, m_w_out, m_norm_ffn_g, m_w_up, m_w_conv, m_b_conv, m_w_down, m_norm_final_g, v_norm_mix_g, v_w_in, v_f_bias, v_sg_ln_g, v_sg_w, v_sg_b, v_w_out, v_norm_ffn_g, v_w_up, v_w_conv, v_b_conv, v_w_down, v_norm_final_g):
    xs, tgt = x[0], loss_target[0]
    g1, g2, g3 = norm_mix_g, norm_ffn_g, norm_final_g.reshape(1, D_MODEL)
    lg = sg_ln_g.reshape(1, D_A)
    fb = jnp.pad(f_bias, ((0, 0), (0, LANES - N_HEADS)))
    pos_chunk = jnp.arange(SG_BLOCK) // SG_CHUNK
    maskf = (pos_chunk[:, None] >= pos_chunk[None, :]).astype(F32)
    wm = (sg_w[0] * maskf[None]).astype(BF16)
    wmt = jnp.swapaxes(wm, 1, 2)
    bfull = jnp.repeat(sg_b[0].T, HEAD_DIM, axis=1)
    hsel = jnp.repeat(jnp.eye(N_HEADS, dtype=F32), HEAD_DIM, axis=1)
    avg = (jnp.repeat(hsel, HEAD_DIM, axis=0) * (1.0 / HEAD_DIM)).astype(BF16)

    (win_g,) = _exchange([w_in[0].astype(BF16)], [], "gather_w_in")
    w_in_p = jnp.pad(_cols_whole(win_g), ((0, 0), (0, D_IN_PAD - D_IN)))
    zuv, qa, ka, va, fl, h1 = _fwd_in(xs, g1, w_in_p, fb)
    oa = _gmlp_fwd(zuv, lg, wm, bfull, avg)
    rest = _Exchange([w_out[0].astype(BF16), w_up[0].astype(BF16), w_down[0].astype(BF16), w_conv[0]], [], relay=False)
    ob, lse, (wout_g, wup_g, wdown_g, wc_g) = _fox_fwd(qa, ka, va, rest)
    w_out_f, w_up_f = wout_g.reshape(D_MODEL, D_MODEL), _cols_whole(wup_g)
    w_down_f, wc_f = wdown_g.reshape(D_FF, D_MODEL), _cols_whole(wc_g)

    x2, h2, a = _fwd_mid(xs, oa, ob, w_out_f, g2, w_up_f)
    ac, yff, dx3, loss, dg3 = _fwd_ffn(a, x2, wc_f, b_conv, w_down_f, g3, tgt)
    da_g, da_v, dwc_g, dwc_v, dbc_g, dbc_v, dwdown_bf, dwup_bf = _bwd_ffn(dx3, a, ac, yff, h2, w_down_f, wc_f)
    dwc, dbc = jnp.concatenate([dwc_g, dwc_v], axis=1), jnp.concatenate([dbc_g, dbc_v], axis=1)
    dx2, doa, dob, dg2 = _bwd_mid(da_g, da_v, w_up_f, x2, g2, dx3, w_out_f, ob)
    dzuv, dwm, dsgb, dlg = _gmlp_bwd(zuv, doa, lg, wm, wmt, bfull, maskf, hsel, avg)
    _, dwout_bf = _matmul_tn([oa, ob], dx2, D_MODEL, D_MODEL, 1024, "dw_out")

    early = ("w_out", "w_up", "wc", "w_down")
    wire = [_row_shards(dwout_bf), dwup_bf, _col_shards(dwc).astype(BF16), dwdown_bf]
    small_early = dict(lg=dlg, sg_w=dwm, sg_b=dsgb, g2=dg2, bc=dbc, g3=dg3)
    grads = _Exchange([*small_early.values(), loss], wire, relay=False)
    dqa, dka, dva, got = _fox_bwd(qa, lse, dob, ka, va, grads)
    n_small = len(small_early)
    gathered, loss_parts = dict(zip(small_early, got[:n_small])), got[n_small]
    recv = dict(zip(early, got[n_small + 1:]))

    gx, dz, dg1, dfb = _bwd_in(dzuv, dqa, dka, dva, fl, xs, dx2, w_in_p, g1)
    _, dwin_bf = _matmul_tn([h1], dz, D_MODEL // 2, D_IN_PAD, 1024, "dw_in", shard_cols=D_IN // N_DEV, n_valid=D_IN)
    gathered["g1"], gathered["fb"], recv["w_in"] = _exchange([dg1, dfb], [dwin_bf], "exchange_w_in")

    weights = dict(w_in=(w_in, m_w_in, v_w_in, 256), w_out=(w_out, m_w_out, v_w_out, 128), w_up=(w_up, m_w_up, v_w_up, 256),
                   wc=(w_conv, m_w_conv, v_w_conv, 3), w_down=(w_down, m_w_down, v_w_down, 176))
    res = {n: _adamw_shard(w, m, v, recv[n], tr, "adamw_" + n) for n, (w, m, v, tr) in weights.items()}

    reps = dict(g1=((norm_mix_g, m_norm_mix_g, v_norm_mix_g), (1, D_MODEL)), fb=((f_bias, m_f_bias, v_f_bias), (1, N_HEADS)),
                lg=((sg_ln_g, m_sg_ln_g, v_sg_ln_g), (1, D_A)), sg_w=((sg_w, m_sg_w, v_sg_w), (N_HEADS, SG_BLOCK, SG_BLOCK)),
                sg_b=((sg_b, m_sg_b, v_sg_b), (N_HEADS, SG_BLOCK)), g2=((norm_ffn_g, m_norm_ffn_g, v_norm_ffn_g), (1, D_MODEL)),
                bc=((b_conv, m_b_conv, v_b_conv), (1, D_FF2)), g3=((norm_final_g, m_norm_final_g, v_norm_final_g), (1, D_MODEL)))
    outs, loss_sum = _adamw_small([tuple(t.reshape(shape) for t in wmv) for wmv, shape in reps.values()],
                                  [gathered[n] for n in reps], loss_parts)
    for (n, (wmv, _)), out in zip(reps.items(), outs):
        res[n] = [o.reshape(wmv[0].shape) for o in out]

    names = ("g1", "w_in", "fb", "lg", "sg_w", "sg_b", "w_out", "g2", "w_up", "wc", "bc", "w_down", "g3")
    return (loss_sum, gx[None], *[res[n][0] for n in names], *[res[n][1] for n in names],
            *[res[n][2] for n in names], *[res[n][3] for n in names])
```

```python
import functools
import math

import jax
import jax.numpy as jnp
from jax import lax
from jax.experimental import pallas as pl
from jax.experimental.pallas import tpu as pltpu

F32 = jnp.float32
BF16 = jnp.bfloat16

D_MODEL = 1024
HEAD_DIM = 64
N_HEADS = 8
D_A = 512
D_B = 512
D_IN = 2 * D_A + 3 * D_B + N_HEADS
D_IN_PAD = 2688
D_FF = 2816
D_FF2 = 2 * D_FF
SG_BLOCK = 128
SG_CHUNK = 64
EPS = 1e-6
N_DEV = 8
LANES = 128
NEG = -1e30
VMEM_LIMIT = 56 * 1024 * 1024

ADAM_LR = 0.001
ADAM_B1 = 0.9
ADAM_B2 = 0.999
ADAM_EPS = 1e-08
ADAM_WD = 0.01
ADAM_STEP = 10

TM = 256
TM_WIDE = 512
TQ = 512
FWD_UNROLL = 4
BWD_UNROLL = 2
CW = 256

MESH_AXES = ("x", "y", "c")
ANY = pl.BlockSpec(memory_space=pl.ANY)


def _cp(*sem):
    return pltpu.CompilerParams(dimension_semantics=sem, vmem_limit_bytes=VMEM_LIMIT)


def _dot(a, b):
    return jnp.dot(a, b, preferred_element_type=F32)


def _dot_nt(a, b):
    return lax.dot_general(a, b, (((1,), (1,)), ((), ())), preferred_element_type=F32)


def _dot_tn(a, b):
    return lax.dot_general(a, b, (((0,), (0,)), ((), ())), preferred_element_type=F32)


def _dot_f32(a, b):
    return jnp.dot(a, b, precision=lax.Precision.HIGHEST, preferred_element_type=F32)


def _gelu(z):
    return 0.5 * z * (1.0 + lax.erf(z * (1.0 / math.sqrt(2.0))))


def _gelu_grad(z):
    return 0.5 * (1.0 + lax.erf(z * (1.0 / math.sqrt(2.0)))) + z * jnp.exp(-0.5 * z * z) * (1.0 / math.sqrt(2.0 * math.pi))


def _log_sigmoid(x):
    return jnp.minimum(x, 0.0) - jnp.log1p(jnp.exp(-jnp.abs(x)))


def _rms(x):
    r = lax.rsqrt(jnp.mean(x * x, axis=-1, keepdims=True) + EPS)
    return r, x * r


def _rms_bwd(dy, n, r, g):
    dn = dy * g
    return r * (dn - n * jnp.mean(dn * n, axis=-1, keepdims=True))


def _full(shape, single=False):
    nd = len(shape)
    if single:
        return pl.BlockSpec(shape, lambda *_: (0,) * nd, pipeline_mode=pl.Buffered(1))
    return pl.BlockSpec(shape, lambda *_: (0,) * nd)


def _rows(tm, cols, rev_nt=None):
    if rev_nt is None:
        return pl.BlockSpec((tm, cols), lambda i: (i, 0))
    return pl.BlockSpec((tm, cols), lambda i: (rev_nt - 1 - i, 0))


def _head_sl(h):
    return slice(HEAD_DIM * h, HEAD_DIM * (h + 1))


L_ROW = HEAD_DIM
L_COL = HEAD_DIM + 3
L_LSE = HEAD_DIM + 6


def _split3(x):
    hi = x.astype(BF16).astype(F32)
    mid = (x - hi).astype(BF16).astype(F32)
    lo = (x - hi - mid).astype(BF16).astype(F32)
    return hi, mid, lo


def _lanes(rows, width, parts):
    lane = lax.broadcasted_iota(jnp.int32, (rows, width), 1)
    out = jnp.zeros((rows, width), F32)
    for at, val in parts.items():
        out = jnp.where(lane == at, val, out)
    return out


def _fwd_in(x, g1, w_in_p, fb):
    T = x.shape[0]
    tm = min(TM, T)

    def body(x_ref, g_ref, w_ref, fb_ref, place_ref, zuv_ref, qa_ref, ka_ref, va_ref, fl_ref, h1_ref, carry):
        @pl.when(pl.program_id(0) == 0)
        def _():
            carry[...] = jnp.zeros_like(carry)

        r, n = _rms(x_ref[...])
        h = (n * g_ref[...]).astype(BF16)
        h1_ref[...] = h
        z = _dot(h, w_ref[...])
        zuv_ref[...] = z[:, :2 * D_A]
        o = 2 * D_A
        fl = z[:, o + 3 * D_B:] + fb_ref[...]
        fl_ref[...] = fl
        tri = (lax.broadcasted_iota(jnp.int32, (tm, tm), 0) >= lax.broadcasted_iota(jnp.int32, (tm, tm), 1)).astype(F32)
        c = _dot_f32(tri, _log_sigmoid(fl)) + carry[...]
        carry[...] = c[tm - 1:tm, :]
        hi, mid, lo = _split3(c)
        parts = jnp.concatenate([hi.astype(BF16), mid.astype(BF16), lo.astype(BF16)], axis=-1)
        placed = _dot(parts, place_ref[...])
        lane = lax.broadcasted_iota(jnp.int32, (tm, LANES), 1)
        data = lane < HEAD_DIM
        ones_q = ((lane >= L_COL) & (lane < L_COL + 3)).astype(F32)
        ones_k = (((lane >= L_ROW) & (lane < L_ROW + 3)) | ((lane >= L_LSE) & (lane < L_LSE + 3))).astype(F32)
        ones_v = ((lane >= L_ROW) & (lane < L_ROW + 3)).astype(F32)
        for hd in range(N_HEADS):
            def rows_of(first_col):
                tile = z[:, first_col + LANES * (hd // 2):first_col + LANES * (hd // 2 + 1)]
                return tile if hd % 2 == 0 else pltpu.roll(tile, HEAD_DIM, axis=1)

            hs = slice(LANES * hd, LANES * (hd + 1))
            qa_ref[hd] = jnp.where(data, rows_of(o) * (HEAD_DIM ** -0.5), placed[:, hs] + ones_q).astype(BF16)
            key_side = pltpu.roll(placed[:, hs], L_COL - L_ROW, axis=1)
            ka_ref[hd] = jnp.where(data, rows_of(o + D_B), ones_k - key_side).astype(BF16)
            va_ref[hd] = jnp.where(data, rows_of(o + 2 * D_B), ones_v).astype(BF16)

    heads = pl.BlockSpec((N_HEADS, tm, LANES), lambda i: (0, i, 0))
    aug = jax.ShapeDtypeStruct((N_HEADS, T, LANES), BF16)
    j, h = jnp.arange(3 * LANES) // LANES, jnp.arange(3 * LANES) % LANES
    place = ((h[:, None] < N_HEADS)
             & (jnp.arange(N_HEADS * LANES)[None, :] == LANES * h[:, None] + L_ROW + j[:, None])).astype(BF16)
    return pl.pallas_call(
        body, name="fwd_in", grid=(T // tm,),
        in_specs=[_rows(tm, D_MODEL), _full((1, D_MODEL)), _full((D_MODEL, D_IN_PAD), True), _full((1, LANES)),
                  _full((3 * LANES, N_HEADS * LANES))],
        out_specs=[_rows(tm, 2 * D_A), heads, heads, heads, _rows(tm, LANES), _rows(tm, D_MODEL)],
        out_shape=[jax.ShapeDtypeStruct((T, 2 * D_A), F32), aug, aug, aug, jax.ShapeDtypeStruct((T, LANES), F32),
                   jax.ShapeDtypeStruct((T, D_MODEL), BF16)],
        scratch_shapes=[pltpu.VMEM((1, LANES), F32)],
        compiler_params=_cp("arbitrary"),
    )(x, g1, w_in_p, fb, place)


def _segmean(x, avg_ref, parts):
    out, rest = None, x
    for _ in range(parts):
        piece = rest.astype(BF16)
        term = _dot(piece, avg_ref[...])
        out = term if out is None else out + term
        rest = rest - piece.astype(F32)
    return out


def _layer_norm(v, lg, avg_ref):
    d = v - _segmean(v, avg_ref, 3)
    rstd = lax.rsqrt(_segmean(d * d, avg_ref, 2) + EPS)
    vhat = d * rstd
    return vhat, rstd, (vhat * lg).astype(BF16)


def _mix_heads(w_ref, x, row_slice):
    low = lax.broadcasted_iota(jnp.int32, (SG_BLOCK, LANES), 1) < HEAD_DIM
    tiles = []
    for p in range(N_HEADS // 2):
        xt = x[row_slice, LANES * p:LANES * (p + 1)]
        zero = jnp.zeros_like(xt)
        tiles.append(_dot(w_ref[2 * p], jnp.where(low, xt, zero)) + _dot(w_ref[2 * p + 1], jnp.where(low, zero, xt)))
    return jnp.concatenate(tiles, axis=-1)


def _gmlp_fwd(zuv, lg, wm, bfull, avg):
    T = zuv.shape[0]
    tg = min(TM_WIDE, T)
    nb = tg // SG_BLOCK

    def body(zuv_ref, lg_ref, wm_ref, b_ref, avg_ref, oa_ref):
        u = _gelu(zuv_ref[:, :D_A])
        _, _, vn = _layer_norm(_gelu(zuv_ref[:, D_A:]), lg_ref[...], avg_ref)
        for n in range(nb):
            rs = slice(n * SG_BLOCK, (n + 1) * SG_BLOCK)
            oa_ref[rs, :] = (u[rs] * (_mix_heads(wm_ref, vn, rs) + b_ref[...])).astype(BF16)

    return pl.pallas_call(
        body, name="gmlp_fwd", grid=(T // tg,),
        in_specs=[_rows(tg, 2 * D_A), _full((1, D_A)), _full((N_HEADS, SG_BLOCK, SG_BLOCK)), _full((SG_BLOCK, D_A)),
                  _full((D_A, D_A))],
        out_specs=_rows(tg, D_A),
        out_shape=jax.ShapeDtypeStruct((T, D_A), BF16),
        compiler_params=_cp("arbitrary"),
    )(zuv, lg, wm, bfull, avg)


def _gmlp_bwd(zuv, doa, lg, wm, wmt, bfull, maskf, hsel, avg):
    T = zuv.shape[0]
    tg = min(TM_WIDE, T)
    nb = tg // SG_BLOCK
    nt = T // tg

    def body(zuv_ref, doa_ref, lg_ref, wm_ref, wmt_ref, b_ref, mask_ref, hsel_ref, avg_ref,
             dzuv_ref, dwm_ref, dsgb_ref, dlg_ref, dbacc):
        i = pl.program_id(0)

        @pl.when(i == 0)
        def _():
            dwm_ref[...] = jnp.zeros_like(dwm_ref)
            dlg_ref[...] = jnp.zeros_like(dlg_ref)
            dbacc[...] = jnp.zeros_like(dbacc)

        zu, zv = zuv_ref[:, :D_A], zuv_ref[:, D_A:]
        lgv = lg_ref[...]
        vhat, rstd, vn = _layer_norm(_gelu(zv), lgv, avg_ref)
        dmixed = doa_ref[...] * _gelu(zu)
        dmb = dmixed.astype(BF16)
        low = lax.broadcasted_iota(jnp.int32, (SG_BLOCK, LANES), 1) < HEAD_DIM
        mixed, dvn = [], []
        for n in range(nb):
            rs = slice(n * SG_BLOCK, (n + 1) * SG_BLOCK)
            mixed.append(_mix_heads(wm_ref, vn, rs) + b_ref[...])
            dvn.append(_mix_heads(wmt_ref, dmb, rs))
            dbacc[...] += dmixed[rs]
            for p in range(N_HEADS // 2):
                ls = slice(LANES * p, LANES * (p + 1))
                dmt, vnt = dmb[rs, ls], vn[rs, ls]
                zero = jnp.zeros_like(dmt)
                dwm_ref[2 * p] += _dot_nt(jnp.where(low, dmt, zero), vnt) * mask_ref[...]
                dwm_ref[2 * p + 1] += _dot_nt(jnp.where(low, zero, dmt), vnt) * mask_ref[...]
        mixed = jnp.concatenate(mixed, axis=0) if nb > 1 else mixed[0]
        dvn = jnp.concatenate(dvn, axis=0) if nb > 1 else dvn[0]
        dlg_ref[...] += jnp.sum(dvn * vhat, axis=0, keepdims=True)
        dvhat = dvn * lgv
        dv = rstd * (dvhat - _segmean(dvhat, avg_ref, 2) - vhat * _segmean(dvhat * vhat, avg_ref, 2))
        dzuv_ref[:, :D_A] = (doa_ref[...] * mixed * _gelu_grad(zu)).astype(BF16)
        dzuv_ref[:, D_A:] = (dv * _gelu_grad(zv)).astype(BF16)

        @pl.when(i == nt - 1)
        def _():
            dsgb_ref[...] = lax.dot_general(hsel_ref[...], dbacc[...], (((1,), (1,)), ((), ())),
                                            precision=lax.Precision.HIGHEST, preferred_element_type=F32)

    return pl.pallas_call(
        body, name="gmlp_bwd", grid=(nt,),
        in_specs=[_rows(tg, 2 * D_A), _rows(tg, D_A), _full((1, D_A)), _full((N_HEADS, SG_BLOCK, SG_BLOCK)),
                  _full((N_HEADS, SG_BLOCK, SG_BLOCK)), _full((SG_BLOCK, D_A)), _full((SG_BLOCK, SG_BLOCK)),
                  _full((N_HEADS, D_A)), _full((D_A, D_A))],
        out_specs=[_rows(tg, 2 * D_A), _full((N_HEADS, SG_BLOCK, SG_BLOCK)), _full((N_HEADS, SG_BLOCK)), _full((1, D_A))],
        out_shape=[jax.ShapeDtypeStruct((T, 2 * D_A), BF16), jax.ShapeDtypeStruct((N_HEADS, SG_BLOCK, SG_BLOCK), F32),
                   jax.ShapeDtypeStruct((N_HEADS, SG_BLOCK), F32), jax.ShapeDtypeStruct((1, D_A), F32)],
        scratch_shapes=[pltpu.VMEM((SG_BLOCK, D_A), F32)],
        compiler_params=_cp("arbitrary"),
    )(zuv, doa, lg, wm, wmt, bfull, maskf, hsel, avg)


def _fox_fwd(qa, ka, va, side):
    T = qa.shape[1]
    tq = min(TQ, T)
    nq = T // tq
    n = side.n

    def body(qa_ref, ka_ref, va_ref, *rest):
        ins, (o_ref, lse_ref), outs = rest[:n], rest[n:n + 2], rest[n + 2:2 * n + 2]
        acc_ref, *sems = rest[2 * n + 2:]
        i = pl.program_id(1)
        if n:
            @pl.when((pl.program_id(0) == 0) & (i == 0))
            def _():
                side.start(ins, outs, sems)
        row = lax.broadcasted_iota(jnp.int32, (tq, tq), 0)
        col = lax.broadcasted_iota(jnp.int32, (tq, tq), 1)
        qs = [qa_ref[0], qa_ref[1]]

        def tiles(js, carry, diag):
            offs = [pl.multiple_of(j * tq, tq) for j in js]
            logits = [[_dot_nt(qs[hh], ka_ref[hh, pl.ds(off, tq), :]) for hh in range(2)] for off in offs]
            carry = list(carry)
            for off, per_head in zip(offs, logits):
                for hh, s in enumerate(per_head):
                    if diag:
                        s = jnp.where(col <= row, s, NEG)
                    m = carry[hh]
                    m_new = jnp.maximum(m, jnp.max(s, axis=-1, keepdims=True))
                    pr = jnp.exp(s - m_new)
                    acc_ref[hh] = jnp.exp(m - m_new) * acc_ref[hh] + _dot(pr.astype(BF16), va_ref[hh, pl.ds(off, tq), :])
                    carry[hh] = m_new
            return tuple(carry)

        acc_ref[...] = jnp.zeros_like(acc_ref)
        init = (jnp.full((tq, 1), NEG, F32),) * 2
        carry = lax.fori_loop(0, i // FWD_UNROLL, lambda t, cr: tiles([FWD_UNROLL * t + u for u in range(FWD_UNROLL)], cr, False), init)
        carry = lax.fori_loop(i - i % FWD_UNROLL, i, lambda j, cr: tiles([j], cr, False), carry)
        carry = tiles([i], carry, True)
        for hh in range(2):
            m, acc = carry[hh], acc_ref[hh]
            l = acc[:, L_ROW:L_ROW + 1]
            o_ref[:, _head_sl(hh)] = acc[:, :HEAD_DIM] / l
            hi, mid, lo = _split3(-(m + jnp.log(l)))
            lse_ref[hh] = _lanes(tq, LANES, {L_LSE: hi, L_LSE + 1: mid, L_LSE + 2: lo}).astype(BF16)
        if n:
            @pl.when((pl.program_id(0) == N_HEADS // 2 - 1) & (i == nq - 1))
            def _():
                side.finish(ins, outs, sems)

    tile = pl.BlockSpec((2, tq, LANES), lambda p, i: (p, i, 0))
    seq = pl.BlockSpec((2, T, LANES), lambda p, i: (p, 0, 0))
    res = pl.pallas_call(
        body, name="fox_fwd", grid=(N_HEADS // 2, nq),
        in_specs=[tile, seq, seq, *side.specs],
        out_specs=[pl.BlockSpec((tq, LANES), lambda p, i: (i, p)), tile, *side.specs],
        out_shape=[jax.ShapeDtypeStruct((T, D_B), F32), jax.ShapeDtypeStruct((N_HEADS, T, LANES), BF16), *side.out_shape],
        scratch_shapes=[pltpu.VMEM((2, tq, LANES), F32), *side.scratch],
        compiler_params=_cp("arbitrary", "arbitrary"),
    )(qa, ka, va, *side.operands)
    return res[0], res[1], res[2:]


def _fox_bwd(qa, lse, doa, ka, va, side):
    T = qa.shape[1]
    tq = min(TQ, T)
    nq = T // tq
    n = side.n

    def body(qa_ref, lse_ref, doa_ref, ka_ref, va_ref, *rest):
        ins, (dqa_ref, dka_ref, dva_ref), outs = rest[:n], rest[n:n + 3], rest[n + 3:2 * n + 3]
        dv_acc, *sems = rest[2 * n + 3:]
        j = pl.program_id(1)
        if n:
            @pl.when((pl.program_id(0) == 0) & (j == 0))
            def _():
                side.start(ins, outs, sems)

        @pl.when(j == 0)
        def _():
            dqa_ref[...] = jnp.zeros_like(dqa_ref)

        row = lax.broadcasted_iota(jnp.int32, (tq, tq), 0)
        col = lax.broadcasted_iota(jnp.int32, (tq, tq), 1)
        ks = [ka_ref[0], ka_ref[1]]
        vs = [va_ref[0], va_ref[1]]

        def tiles(ids, diag):
            work = []
            for i in ids:
                off = pl.multiple_of(i * tq, tq)
                for hh in range(2):
                    qi = qa_ref[hh, pl.ds(off, tq), :] + lse_ref[hh, pl.ds(off, tq), :]
                    doi = doa_ref[hh, pl.ds(off, tq), :]
                    work.append((off, hh, qi, doi, _dot_nt(ks[hh], qi), _dot_nt(vs[hh], doi)))
            for off, hh, qi, doi, st, dpt in work:
                if diag:
                    st = jnp.where(row <= col, st, NEG)
                pt = jnp.exp(st)
                dv_acc[hh] += _dot(pt.astype(BF16), doi)
                dsb = (pt * dpt).astype(BF16)
                dka_ref[hh] += _dot(dsb, qi)
                dqa_ref[hh, pl.ds(off, tq), :] += _dot_tn(dsb, ks[hh])

        dka_ref[...] = jnp.zeros_like(dka_ref)
        dv_acc[...] = jnp.zeros_like(dv_acc)
        tiles([j], True)
        todo = nq - 1 - j

        @pl.loop(0, todo // BWD_UNROLL)
        def _(t):
            tiles([j + 1 + BWD_UNROLL * t + u for u in range(BWD_UNROLL)], False)

        @pl.loop(nq - todo % BWD_UNROLL, nq)
        def _(i):
            tiles([i], False)

        dva_ref[...] = dv_acc[...].astype(BF16)
        if n:
            @pl.when((pl.program_id(0) == N_HEADS // 2 - 1) & (j == nq - 1))
            def _():
                side.finish(ins, outs, sems)

    tile = pl.BlockSpec((2, tq, LANES), lambda p, j: (p, j, 0))
    seq = pl.BlockSpec((2, T, LANES), lambda p, j: (p, 0, 0))
    res = pl.pallas_call(
        body, name="fox_bwd", grid=(N_HEADS // 2, nq),
        in_specs=[seq, seq, seq, tile, tile, *side.specs],
        out_specs=[seq, tile, tile, *side.specs],
        out_shape=[jax.ShapeDtypeStruct((N_HEADS, T, LANES), F32), jax.ShapeDtypeStruct((N_HEADS, T, LANES), F32),
                   jax.ShapeDtypeStruct((N_HEADS, T, LANES), BF16), *side.out_shape],
        scratch_shapes=[pltpu.VMEM((2, tq, LANES), F32), *side.scratch],
        compiler_params=_cp("arbitrary", "arbitrary"),
    )(qa, lse, doa, ka, va, *side.operands)
    return res[0], res[1], res[2], res[3:]


def _fwd_mid(x, oa, ob, w_out, g2, w_up):
    T = x.shape[0]
    tm = min(TM_WIDE, T)

    def body(x_ref, oa_ref, ob_ref, wo_ref, g_ref, wu_ref, x2_ref, h2_ref, a_ref):
        oab = jnp.concatenate([oa_ref[...], ob_ref[...].astype(BF16)], axis=-1)
        x2 = x_ref[...] + _dot(oab, wo_ref[...])
        x2_ref[...] = x2
        _, n = _rms(x2)
        h2 = (n * g_ref[...]).astype(BF16)
        h2_ref[...] = h2
        a_ref[...] = _dot(h2, wu_ref[...])

    return pl.pallas_call(
        body, name="fwd_mid", grid=(T // tm,),
        in_specs=[_rows(tm, D_MODEL), _rows(tm, D_A), _rows(tm, D_B), _full((D_MODEL, D_MODEL), True), _full((1, D_MODEL)),
                  _full((D_MODEL, D_FF2), True)],
        out_specs=[_rows(tm, D_MODEL), _rows(tm, D_MODEL), _rows(tm, D_FF2)],
        out_shape=[jax.ShapeDtypeStruct((T, D_MODEL), F32), jax.ShapeDtypeStruct((T, D_MODEL), BF16),
                   jax.ShapeDtypeStruct((T, D_FF2), F32)],
        compiler_params=_cp("arbitrary"),
    )(x, oa, ob, w_out, g2, w_up)


def _row_before(x, prev, k):
    rolled = pltpu.roll(x, k, axis=0)
    row = lax.broadcasted_iota(jnp.int32, (8, x.shape[1]), 0)
    head = rolled[0:8]
    for r in range(k):
        head = jnp.where(row == r, prev[8 - k + r:9 - k + r], head)
    return jnp.concatenate([head, rolled[8:]], axis=0)


def _row_after(x, nxt, k):
    tm = x.shape[0]
    rolled = pltpu.roll(x, tm - k, axis=0)
    row = lax.broadcasted_iota(jnp.int32, (8, x.shape[1]), 0)
    tail = rolled[tm - 8:tm]
    for r in range(k):
        tail = jnp.where(row == 8 - k + r, nxt[r:r + 1], tail)
    return jnp.concatenate([rolled[:tm - 8], tail], axis=0)


def _fwd_ffn(a, x2, wc, bc, w_down, g3, tgt):
    T = x2.shape[0]
    tm = min(TM, T)

    def body(a_ref, x2_ref, wc_ref, bc_ref, wd_ref, g_ref, tgt_ref, ac_ref, yff_ref, dx3_ref, loss_ref, dg3_ref, carry):
        @pl.when(pl.program_id(0) == 0)
        def _():
            carry[...] = jnp.zeros_like(carry)
            loss_ref[...] = jnp.zeros_like(loss_ref)
            dg3_ref[...] = jnp.zeros_like(dg3_ref)

        def conv(cs):
            a0 = a_ref[:, cs]
            prev = carry[:, cs]
            ac = (wc_ref[0:1, cs] * _row_before(a0, prev, 2) + wc_ref[1:2, cs] * _row_before(a0, prev, 1)
                  + wc_ref[2:3, cs] * a0 + bc_ref[:, cs])
            ac_ref[:, cs] = ac.astype(BF16)
            return ac

        x3 = x2_ref[...]
        for ci in range(D_FF // CW):
            gs = slice(ci * CW, (ci + 1) * CW)
            ag = conv(gs)
            av = conv(slice(D_FF + ci * CW, D_FF + (ci + 1) * CW))
            yb = (ag * jax.nn.sigmoid(ag) * av).astype(BF16)
            yff_ref[:, gs] = yb
            x3 = x3 + _dot(yb, wd_ref[gs, :])
        carry[...] = a_ref[tm - 8:tm, :]
        r, n = _rms(x3)
        g = g_ref[...]
        diff = n * g - tgt_ref[...]
        loss_ref[...] += (0.5 / D_MODEL) * jnp.sum(diff * diff)
        dout = diff * (1.0 / D_MODEL)
        dg3_ref[...] += jnp.sum(dout * n, axis=0, keepdims=True)
        dx3_ref[...] = _rms_bwd(dout, n, r, g)

    return pl.pallas_call(
        body, name="fwd_ffn", grid=(T // tm,),
        in_specs=[_rows(tm, D_FF2), _rows(tm, D_MODEL), _full((3, D_FF2)), _full((1, D_FF2)), _full((D_FF, D_MODEL), True),
                  _full((1, D_MODEL)), _rows(tm, D_MODEL)],
        out_specs=[_rows(tm, D_FF2), _rows(tm, D_FF), _rows(tm, D_MODEL), _full((8, LANES)), _full((1, D_MODEL))],
        out_shape=[jax.ShapeDtypeStruct((T, D_FF2), BF16), jax.ShapeDtypeStruct((T, D_FF), BF16),
                   jax.ShapeDtypeStruct((T, D_MODEL), F32), jax.ShapeDtypeStruct((8, LANES), F32),
                   jax.ShapeDtypeStruct((1, D_MODEL), F32)],
        scratch_shapes=[pltpu.VMEM((8, D_FF2), F32)],
        compiler_params=_cp("arbitrary"),
    )(a, x2, wc, bc, w_down, g3, tgt)


def _bwd_ffn(dx3, a, ac, yff, h2, w_down, wc):
    T = dx3.shape[0]
    tm = min(TM, T)
    nt = T // tm
    half = D_FF // 2
    shard_up, shard_down = D_FF2 // N_DEV, D_FF // N_DEV

    def body(dx3_ref, ag_ref, av_ref, acg_ref, acv_ref, yff_ref, h2_ref, wd_ref, wcg_ref, wcv_ref,
             dag_ref, dav_ref, dwcg_ref, dwcv_ref, dbcg_ref, dbcv_ref, dwd_ref, dwu_ref,
             nxt, shifted, acc_down, acc_g, acc_v, stage_up, stage_down, sem):
        c, r = pl.program_id(0), pl.program_id(1)

        @pl.when(r == 0)
        def _():
            for ref in (nxt, dwcg_ref, dwcv_ref, dbcg_ref, dbcv_ref, acc_down, acc_g, acc_v):
                ref[...] = jnp.zeros_like(ref)

        dxb = dx3_ref[...].astype(BF16)
        dy_all = _dot_nt(dxb, wd_ref[...])

        def back(a_ref, w_ref, da_ref, dwc_ref, dbc_ref, nx, cs, dac):
            a0 = a_ref[:, cs]
            shifted[0] = _row_after(dac, nxt[:, nx], 1)
            shifted[1] = _row_after(dac, nxt[:, nx], 2)
            dp1, dp2 = shifted[0], shifted[1]
            dbc_ref[:, cs] += jnp.sum(dac, axis=0, keepdims=True)
            dwc_ref[0:1, cs] += jnp.sum(dp2 * a0, axis=0, keepdims=True)
            dwc_ref[1:2, cs] += jnp.sum(dp1 * a0, axis=0, keepdims=True)
            dwc_ref[2:3, cs] += jnp.sum(dac * a0, axis=0, keepdims=True)
            da_ref[:, cs] = (w_ref[2:3, cs] * dac + w_ref[1:2, cs] * dp1 + w_ref[0:1, cs] * dp2).astype(BF16)
            nxt[:, nx] = dac[0:8]

        for ci in range(half // LANES):
            cs = slice(ci * LANES, (ci + 1) * LANES)
            dy = dy_all[:, cs]
            ag, av = acg_ref[:, cs].astype(F32), acv_ref[:, cs].astype(F32)
            sg = jax.nn.sigmoid(ag)
            back(av_ref, wcv_ref, dav_ref, dwcv_ref, dbcv_ref, slice(half + ci * LANES, half + (ci + 1) * LANES), cs,
                 dy * (ag * sg))
            back(ag_ref, wcg_ref, dag_ref, dwcg_ref, dbcg_ref, cs, cs, dy * av * (sg * (1.0 + ag * (1.0 - sg))))

        acc_down[...] += _dot_tn(yff_ref[...], dxb)
        h2 = h2_ref[...]
        acc_g[...] += _dot_tn(h2, dag_ref[...])
        acc_v[...] += _dot_tn(h2, dav_ref[...])

        @pl.when(r == nt - 1)
        def _():
            for s in range(half // shard_down):
                stage_down[...] = acc_down[s * shard_down:(s + 1) * shard_down, :].astype(BF16)
                out = pltpu.make_async_copy(stage_down, dwd_ref.at[(half // shard_down) * c + s], sem)
                out.start()
                out.wait()
            for acc, first in ((acc_g, 0), (acc_v, N_DEV // 2)):
                for s in range(half // shard_up):
                    stage_up[...] = acc[:, s * shard_up:(s + 1) * shard_up].astype(BF16)
                    out = pltpu.make_async_copy(stage_up, dwu_ref.at[first + (half // shard_up) * c + s], sem)
                    out.start()
                    out.wait()

    def cols(width, second_half):
        return pl.BlockSpec((tm, width), lambda c, r: (nt - 1 - r, c + (2 if second_half else 0)))

    def param(rows, second_half):
        return pl.BlockSpec((rows, half), lambda c, r: (0, c + (2 if second_half else 0)))

    tokens = pl.BlockSpec((tm, D_MODEL), lambda c, r: (nt - 1 - r, 0))
    return pl.pallas_call(
        body, name="bwd_ffn", grid=(2, nt),
        in_specs=[tokens, cols(half, False), cols(half, True), cols(half, False), cols(half, True), cols(half, False), tokens,
                  pl.BlockSpec((half, D_MODEL), lambda c, r: (c, 0), pipeline_mode=pl.Buffered(1)),
                  param(3, False), param(3, True)],
        out_specs=[cols(half, False), cols(half, False), param(3, False), param(3, False), param(1, False), param(1, False),
                   ANY, ANY],
        out_shape=[jax.ShapeDtypeStruct((T, D_FF), BF16), jax.ShapeDtypeStruct((T, D_FF), BF16),
                   jax.ShapeDtypeStruct((3, D_FF), F32), jax.ShapeDtypeStruct((3, D_FF), F32),
                   jax.ShapeDtypeStruct((1, D_FF), F32), jax.ShapeDtypeStruct((1, D_FF), F32),
                   jax.ShapeDtypeStruct((N_DEV, shard_down, D_MODEL), BF16), jax.ShapeDtypeStruct((N_DEV, D_MODEL, shard_up), BF16)],
        scratch_shapes=[pltpu.VMEM((8, D_FF), F32), pltpu.VMEM((2, tm, LANES), F32), pltpu.VMEM((half, D_MODEL), F32),
                        pltpu.VMEM((D_MODEL, half), F32), pltpu.VMEM((D_MODEL, half), F32),
                        pltpu.VMEM((D_MODEL, shard_up), BF16), pltpu.VMEM((shard_down, D_MODEL), BF16),
                        pltpu.SemaphoreType.DMA],
        compiler_params=_cp("arbitrary", "arbitrary"),
    )(dx3, a, a, ac, ac, yff, h2, w_down, wc, wc)


def _bwd_mid(da_g, da_v, w_up, x2, g2, dx3, w_out, ob):
    T = x2.shape[0]
    tm = min(TM, T)

    def body(dag_ref, dav_ref, wu_ref, x2_ref, g_ref, dx3_ref, wo_ref, ob_ref, hsum_ref, place_ref,
             dx2_ref, doa_ref, dob_ref, dg2_ref):
        @pl.when(pl.program_id(0) == 0)
        def _():
            dg2_ref[...] = jnp.zeros_like(dg2_ref)

        dh2 = _dot_nt(dag_ref[...], wu_ref[:, :D_FF]) + _dot_nt(dav_ref[...], wu_ref[:, D_FF:])
        r, n = _rms(x2_ref[...])
        dg2_ref[...] += jnp.sum(dh2 * n, axis=0, keepdims=True)
        dx2 = dx3_ref[...] + _rms_bwd(dh2, n, r, g_ref[...])
        dx2_ref[...] = dx2
        doab = _dot_nt(dx2.astype(BF16), wo_ref[...])
        doa_ref[...] = doab[:, :D_A]
        dob = doab[:, D_A:]
        rest, delta = dob.astype(BF16).astype(F32) * ob_ref[...], None
        for _ in range(3):
            piece = rest.astype(BF16)
            term = _dot(piece, hsum_ref[...])
            delta = term if delta is None else delta + term
            rest = rest - piece.astype(F32)
        hi, mid, lo = _split3(-delta)
        parts = jnp.concatenate([hi.astype(BF16), mid.astype(BF16), lo.astype(BF16)], axis=-1)
        placed = _dot(parts, place_ref[...])
        data = lax.broadcasted_iota(jnp.int32, (tm, LANES), 1) < HEAD_DIM
        for hd in range(N_HEADS):
            tile = dob[:, LANES * (hd // 2):LANES * (hd // 2 + 1)]
            rows = tile if hd % 2 == 0 else pltpu.roll(tile, HEAD_DIM, axis=1)
            dob_ref[hd] = jnp.where(data, rows, placed[:, LANES * hd:LANES * (hd + 1)]).astype(BF16)

    hsum = (jnp.arange(D_B)[:, None] // HEAD_DIM == jnp.arange(LANES)[None, :]).astype(BF16)
    j, h = jnp.arange(3 * LANES) // LANES, jnp.arange(3 * LANES) % LANES
    place = ((h[:, None] < N_HEADS)
             & (jnp.arange(N_HEADS * LANES)[None, :] == LANES * h[:, None] + L_ROW + j[:, None])).astype(BF16)
    return pl.pallas_call(
        body, name="bwd_mid", grid=(T // tm,),
        in_specs=[_rows(tm, D_FF), _rows(tm, D_FF), _full((D_MODEL, D_FF2), True), _rows(tm, D_MODEL), _full((1, D_MODEL)),
                  _rows(tm, D_MODEL), _full((D_MODEL, D_MODEL), True), _rows(tm, D_B), _full((D_B, LANES)),
                  _full((3 * LANES, N_HEADS * LANES))],
        out_specs=[_rows(tm, D_MODEL), _rows(tm, D_A), pl.BlockSpec((N_HEADS, tm, LANES), lambda i: (0, i, 0)),
                   _full((1, D_MODEL))],
        out_shape=[jax.ShapeDtypeStruct((T, D_MODEL), F32), jax.ShapeDtypeStruct((T, D_A), F32),
                   jax.ShapeDtypeStruct((N_HEADS, T, LANES), BF16), jax.ShapeDtypeStruct((1, D_MODEL), F32)],
        compiler_params=_cp("arbitrary"),
    )(da_g, da_v, w_up, x2, g2, dx3, w_out, ob, hsum, place)


def _bwd_in(dzuv, dqa, dka, dva, fl, x, dx2, w_in_p, g1):
    T = x.shape[0]
    tm = min(TM, T)
    nt = T // tm

    def body(dzuv_ref, dqa_ref, dka_ref, dva_ref, fl_ref, x_ref, dx2_ref, w_ref, g_ref,
             gx_ref, dz_ref, dg1_ref, dfb_ref, carry):
        @pl.when(pl.program_id(0) == 0)
        def _():
            carry[...] = jnp.zeros_like(carry)
            dg1_ref[...] = jnp.zeros_like(dg1_ref)
            dfb_ref[...] = jnp.zeros_like(dfb_ref)

        dc = _lanes(tm, LANES, {hd: dqa_ref[hd][:, L_ROW:L_ROW + 1] - dka_ref[hd][:, L_COL:L_COL + 1] for hd in range(N_HEADS)})
        later = (lax.broadcasted_iota(jnp.int32, (tm, tm), 1) >= lax.broadcasted_iota(jnp.int32, (tm, tm), 0)).astype(F32)
        dls = _dot_f32(later, dc) + carry[...]
        carry[...] = dls[0:1, :]
        dzf = dls * jax.nn.sigmoid(-fl_ref[...])
        dfb_ref[...] += jnp.sum(dzf, axis=0, keepdims=True)
        data = lax.broadcasted_iota(jnp.int32, (tm, LANES), 1) < HEAD_DIM

        def compact(ref, scale=None):
            def rows(hd):
                return (ref[hd] if scale is None else ref[hd] * scale).astype(BF16)

            return [jnp.where(data, rows(2 * p), pltpu.roll(rows(2 * p + 1), HEAD_DIM, axis=1)) for p in range(N_HEADS // 2)]

        dz = jnp.concatenate([dzuv_ref[...], *compact(dqa_ref, HEAD_DIM ** -0.5), *compact(dka_ref), *compact(dva_ref),
                              dzf.astype(BF16)], axis=-1)
        dz_ref[...] = dz
        dh1 = _dot_nt(dz, w_ref[...])
        r, n = _rms(x_ref[...])
        dg1_ref[...] += jnp.sum(dh1 * n, axis=0, keepdims=True)
        gx_ref[...] = dx2_ref[...] + _rms_bwd(dh1, n, r, g_ref[...])

    rv = functools.partial(_rows, tm, rev_nt=nt)
    heads = pl.BlockSpec((N_HEADS, tm, LANES), lambda i: (0, nt - 1 - i, 0))
    return pl.pallas_call(
        body, name="bwd_in", grid=(nt,),
        in_specs=[rv(2 * D_A), heads, heads, heads, rv(LANES), rv(D_MODEL), rv(D_MODEL),
                  _full((D_MODEL, D_IN_PAD), True), _full((1, D_MODEL))],
        out_specs=[rv(D_MODEL), rv(D_IN_PAD), _full((1, D_MODEL)), _full((1, LANES))],
        out_shape=[jax.ShapeDtypeStruct((T, D_MODEL), F32), jax.ShapeDtypeStruct((T, D_IN_PAD), BF16),
                   jax.ShapeDtypeStruct((1, D_MODEL), F32), jax.ShapeDtypeStruct((1, LANES), F32)],
        scratch_shapes=[pltpu.VMEM((1, LANES), F32)],
        compiler_params=_cp("arbitrary"),
    )(dzuv, dqa, dka, dva, fl, x, dx2, w_in_p, g1)


def _matmul_tn(a_parts, b, tmm, tn, tk, name, shard_cols=None, n_valid=None):
    T = b.shape[0]
    widths = [a.shape[1] for a in a_parts]
    M, N = sum(widths), b.shape[1]
    tk = min(tk, T)
    nk = T // tk
    part_w = tmm // len(a_parts)
    n_valid = N if n_valid is None else n_valid

    def body(*refs):
        a_refs, b_ref, o_ref, obf_ref = refs[:len(a_parts)], refs[-3], refs[-2], refs[-1]
        k = pl.program_id(2)

        @pl.when(k == 0)
        def _():
            o_ref[...] = jnp.zeros_like(o_ref)

        a = [r[...].astype(BF16) for r in a_refs]
        o_ref[...] += _dot_tn(a[0] if len(a) == 1 else jnp.concatenate(a, axis=-1), b_ref[...].astype(BF16))

        @pl.when(k == nk - 1)
        def _():
            if shard_cols is None:
                obf_ref[...] = o_ref[...].astype(BF16)
            else:
                for d in range(min(tn, n_valid) // shard_cols):
                    obf_ref[d] = o_ref[:, d * shard_cols:(d + 1) * shard_cols].astype(BF16)

    if shard_cols is None:
        bf_spec, bf_shape = pl.BlockSpec((tmm, tn), lambda i, j, k: (i, j)), (M, N)
    else:
        per_tile = min(tn, n_valid) // shard_cols
        bf_spec, bf_shape = pl.BlockSpec((per_tile, tmm, shard_cols), lambda i, j, k: (j, i, 0)), (N_DEV, M, shard_cols)
    a_specs = [pl.BlockSpec((tk, part_w), lambda i, j, k: (k, i)) for _ in a_parts]
    return pl.pallas_call(
        body, name=name, grid=(M // tmm, N // tn, nk),
        in_specs=[*a_specs, pl.BlockSpec((tk, tn), lambda i, j, k: (k, j))],
        out_specs=[pl.BlockSpec((tmm, tn), lambda i, j, k: (i, j)), bf_spec],
        out_shape=[jax.ShapeDtypeStruct((M, N), F32), jax.ShapeDtypeStruct(bf_shape, BF16)],
        compiler_params=_cp("arbitrary", "arbitrary", "arbitrary"),
    )(*a_parts, b)


class _Exchange:
    def __init__(self, gather, scatter, relay):
        self.n_g, self.n, self.relay = len(gather), len(gather) + len(scatter), relay
        self.operands = [*gather, *scatter]
        self.out_shape = [jax.ShapeDtypeStruct((N_DEV, *g.shape), g.dtype) for g in gather]
        self.out_shape += [jax.ShapeDtypeStruct(s.shape, s.dtype) for s in scatter]
        self.specs = [ANY] * self.n
        n = self.n
        self.scratch = [pltpu.SemaphoreType.DMA((7 * n,)), pltpu.SemaphoreType.DMA((7 * n,)),
                        pltpu.SemaphoreType.DMA((n,))] if n else []

    def _plan(self, ins, outs, sems):
        send_sems, recv_sems, local_sems = sems
        x, y, c = (lax.axis_index(ax) for ax in MESH_AXES)
        me = 4 * x + 2 * y + c
        sibling = (x, y, 1 - c)
        chips = [(1 - x, y), (x, 1 - y), (1 - x, 1 - y)]
        peers = [sibling] + [(*chip, c) for chip in chips] + [(*chip, 1 - c) for chip in chips]

        def index(dev):
            return 4 * dev[0] + 2 * dev[1] + dev[2]

        def remote(k, src, dst, to):
            return pltpu.make_async_remote_copy(src_ref=src, dst_ref=dst, send_sem=send_sems.at[k], recv_sem=recv_sems.at[k],
                                                device_id=to, device_id_type=pl.DeviceIdType.MESH)

        local, sends, relays, recvs = [], [], [], []
        for a in range(self.n):
            src, out, base = ins[a], outs[a], 7 * a
            if a >= self.n_g:
                local.append(pltpu.make_async_copy(src.at[me], out.at[me], local_sems.at[a]))
                sends += [remote(base + k, src.at[index(peer)], out.at[me], peer) for k, peer in enumerate(peers)]
            else:
                local.append(pltpu.make_async_copy(src, out.at[me], local_sems.at[a]))
                sends += [remote(base + k, src, out.at[me], peer) for k, peer in enumerate(peers[:4 if self.relay else 7])]
            for k, peer in enumerate(peers):
                slot = out.at[index(peer)]
                if a < self.n_g and self.relay and k >= 4:
                    continue
                recv = remote(base + k, slot, slot, peer)
                if a < self.n_g and self.relay and k >= 1:
                    relays.append((recv, remote(base + 3 + k, slot, slot, sibling)))
                else:
                    recvs.append(recv)
            if a < self.n_g and self.relay:
                for j, chip in enumerate(chips):
                    slot = out.at[index((*chip, 1 - c))]
                    recvs.append(remote(base + 4 + j, slot, slot, sibling))
        return local, sends, relays, recvs

    def start(self, ins, outs, sems):
        local, sends, _, _ = self._plan(ins, outs, sems)
        for cp in local + sends:
            cp.start()

    def finish(self, ins, outs, sems):
        local, sends, relays, recvs = self._plan(ins, outs, sems)
        for recv, fwd in relays:
            recv.wait_recv()
            fwd.start()
        for recv in recvs:
            recv.wait_recv()
        for cp in sends + [fwd for _, fwd in relays]:
            cp.wait_send()
        for cp in local:
            cp.wait()


def _exchange(gather, scatter, name):
    ex = _Exchange(gather, scatter, relay=True)
    n = ex.n

    def body(*refs):
        ins, outs, sems = refs[:n], refs[n:2 * n], refs[2 * n:]
        ex.start(ins, outs, sems)
        ex.finish(ins, outs, sems)

    return pl.pallas_call(body, name=name, in_specs=ex.specs, out_specs=ex.specs, out_shape=ex.out_shape,
                          scratch_shapes=ex.scratch)(*ex.operands)


def _adamw(w, g, m, v):
    m = ADAM_B1 * m + (1.0 - ADAM_B1) * g
    v = ADAM_B2 * v + (1.0 - ADAM_B2) * jnp.square(g)
    m_hat = m / (1.0 - ADAM_B1 ** ADAM_STEP)
    v_hat = v / (1.0 - ADAM_B2 ** ADAM_STEP)
    delta = -ADAM_LR * (m_hat / (jnp.sqrt(v_hat) + ADAM_EPS) + ADAM_WD * w)
    return delta, m, v


def _adamw_shard(w, m, v, recv, tr, name):
    _, R, C = w.shape

    def body(w_ref, m_ref, v_ref, recv_ref, g_ref, d_ref, nm_ref, nv_ref):
        g = recv_ref[0].astype(F32)
        for d in range(1, N_DEV):
            g = g + recv_ref[d].astype(F32)
        g_ref[...] = g
        d_ref[...], nm_ref[...], nv_ref[...] = _adamw(w_ref[...], g, m_ref[...], v_ref[...])

    blk = pl.BlockSpec((None, tr, C), lambda i: (0, i, 0))
    return pl.pallas_call(
        body, name=name, grid=(R // tr,),
        in_specs=[blk, blk, blk, pl.BlockSpec((N_DEV, tr, C), lambda i: (0, i, 0))],
        out_specs=[blk] * 4, out_shape=[jax.ShapeDtypeStruct((1, R, C), F32)] * 4,
        compiler_params=_cp("arbitrary"),
    )(w, m, v, recv)


def _adamw_small(params, gathered, loss_parts):
    n = len(params)

    def body(*refs):
        ins, gs, loss_ref, outs = refs[:3 * n], refs[3 * n:4 * n], refs[4 * n], refs[4 * n + 1:]
        for p in range(n):
            w_ref, m_ref, v_ref = ins[3 * p:3 * p + 3]
            g = gs[p][0]
            for d in range(1, N_DEV):
                g = g + gs[p][d]
            g = g[..., :w_ref.shape[-1]]
            g_ref, d_ref, nm_ref, nv_ref = outs[4 * p:4 * p + 4]
            g_ref[...] = g
            d_ref[...], nm_ref[...], nv_ref[...] = _adamw(w_ref[...], g, m_ref[...], v_ref[...])
        total = loss_ref[0]
        for d in range(1, N_DEV):
            total = total + loss_ref[d]
        outs[4 * n][...] = total

    out_shape = [jax.ShapeDtypeStruct(w.shape, F32) for w, _, _ in params for _ in range(4)]
    res = pl.pallas_call(body, name="adamw_small", out_shape=[*out_shape, jax.ShapeDtypeStruct((8, LANES), F32)],
                         compiler_params=pltpu.CompilerParams(vmem_limit_bytes=VMEM_LIMIT))(
        *[t for p in params for t in p], *gathered, loss_parts)
    return [res[4 * p:4 * p + 4] for p in range(n)], res[4 * n][0, 0]


def _col_shards(g):
    return jnp.transpose(g.reshape(g.shape[0], N_DEV, -1), (1, 0, 2))


def _row_shards(g):
    return g.reshape(N_DEV, -1, g.shape[1])


def _cols_whole(g):
    return jnp.transpose(g, (1, 0, 2)).reshape(g.shape[1], -1)


def kernel(x, norm_mix_g, w_in, f_bias, sg_ln_g, sg_w, sg_b, w_out, norm_ffn_g, w_up, w_conv, b_conv, w_down, norm_final_g, loss_target, m_norm_mix_g, m_w_in, m_f_bias, m_sg_ln_g, m_sg_w, m_sg_b, m_w_out, m_norm_ffn_g, m_w_up, m_w_conv, m_b_conv, m_w_down, m_norm_final_g, v_norm_mix_g, v_w_in, v_f_bias, v_sg_ln_g, v_sg_w, v_sg_b, v_w_out, v_norm_ffn_g, v_w_up, v_w_conv, v_b_conv, v_w_down, v_norm_final_g):
    xs, tgt = x[0], loss_target[0]
    g1, g2, g3 = norm_mix_g, norm_ffn_g, norm_final_g.reshape(1, D_MODEL)
    lg = sg_ln_g.reshape(1, D_A)
    fb = jnp.pad(f_bias, ((0, 0), (0, LANES - N_HEADS)))
    pos_chunk = jnp.arange(SG_BLOCK) // SG_CHUNK
    maskf = (pos_chunk[:, None] >= pos_chunk[None, :]).astype(F32)
    wm = (sg_w[0] * maskf[None]).astype(BF16)
    wmt = jnp.swapaxes(wm, 1, 2)
    bfull = jnp.repeat(sg_b[0].T, HEAD_DIM, axis=1)
    hsel = jnp.repeat(jnp.eye(N_HEADS, dtype=F32), HEAD_DIM, axis=1)
    avg = (jnp.repeat(hsel, HEAD_DIM, axis=0) * (1.0 / HEAD_DIM)).astype(BF16)

    (win_g,) = _exchange([w_in[0].astype(BF16)], [], "gather_w_in")
    w_in_p = jnp.pad(_cols_whole(win_g), ((0, 0), (0, D_IN_PAD - D_IN)))
    zuv, qa, ka, va, fl, h1 = _fwd_in(xs, g1, w_in_p, fb)
    oa = _gmlp_fwd(zuv, lg, wm, bfull, avg)
    rest = _Exchange([w_out[0].astype(BF16), w_up[0].astype(BF16), w_down[0].astype(BF16), w_conv[0]], [], relay=False)
    ob, lse, (wout_g, wup_g, wdown_g, wc_g) = _fox_fwd(qa, ka, va, rest)
    w_out_f, w_up_f = wout_g.reshape(D_MODEL, D_MODEL), _cols_whole(wup_g)
    w_down_f, wc_f = wdown_g.reshape(D_FF, D_MODEL), _cols_whole(wc_g)

    x2, h2, a = _fwd_mid(xs, oa, ob, w_out_f, g2, w_up_f)
    ac, yff, dx3, loss, dg3 = _fwd_ffn(a, x2, wc_f, b_conv, w_down_f, g3, tgt)
    da_g, da_v, dwc_g, dwc_v, dbc_g, dbc_v, dwdown_bf, dwup_bf = _bwd_ffn(dx3, a, ac, yff, h2, w_down_f, wc_f)
    dwc, dbc = jnp.concatenate([dwc_g, dwc_v], axis=1), jnp.concatenate([dbc_g, dbc_v], axis=1)
    dx2, doa, dob, dg2 = _bwd_mid(da_g, da_v, w_up_f, x2, g2, dx3, w_out_f, ob)
    dzuv, dwm, dsgb, dlg = _gmlp_bwd(zuv, doa, lg, wm, wmt, bfull, maskf, hsel, avg)
    _, dwout_bf = _matmul_tn([oa, ob], dx2, D_MODEL, D_MODEL, 1024, "dw_out")

    early = ("w_out", "w_up", "wc", "w_down")
    wire = [_row_shards(dwout_bf), dwup_bf, _col_shards(dwc).astype(BF16), dwdown_bf]
    small_early = dict(lg=dlg, sg_w=dwm, sg_b=dsgb, g2=dg2, bc=dbc, g3=dg3)
    grads = _Exchange([*small_early.values(), loss], wire, relay=False)
    dqa, dka, dva, got = _fox_bwd(qa, lse, dob, ka, va, grads)
    n_small = len(small_early)
    gathered, loss_parts = dict(zip(small_early, got[:n_small])), got[n_small]
    recv = dict(zip(early, got[n_small + 1:]))

    gx, dz, dg1, dfb = _bwd_in(dzuv, dqa, dka, dva, fl, xs, dx2, w_in_p, g1)
    _, dwin_bf = _matmul_tn([h1], dz, D_MODEL // 2, D_IN_PAD, 1024, "dw_in", shard_cols=D_IN // N_DEV, n_valid=D_IN)
    gathered["g1"], gathered["fb"], recv["w_in"] = _exchange([dg1, dfb], [dwin_bf], "exchange_w_in")

    weights = dict(w_in=(w_in, m_w_in, v_w_in, 256), w_out=(w_out, m_w_out, v_w_out, 128), w_up=(w_up, m_w_up, v_w_up, 256),
                   wc=(w_conv, m_w_conv, v_w_conv, 3), w_down=(w_down, m_w_down, v_w_down, 176))
    res = {n: _adamw_shard(w, m, v, recv[n], tr, "adamw_" + n) for n, (w, m, v, tr) in weights.items()}

    reps = dict(g1=((norm_mix_g, m_norm_mix_g, v_norm_mix_g), (1, D_MODEL)), fb=((f_bias, m_f_bias, v_f_bias), (1, N_HEADS)),
                lg=((sg_ln_g, m_sg_ln_g, v_sg_ln_g), (1, D_A)), sg_w=((sg_w, m_sg_w, v_sg_w), (N_HEADS, SG_BLOCK, SG_BLOCK)),
                sg_b=((sg_b, m_sg_b, v_sg_b), (N_HEADS, SG_BLOCK)), g2=((norm_ffn_g, m_norm_ffn_g, v_norm_ffn_g), (1, D_MODEL)),
                bc=((b_conv, m_b_conv, v_b_conv), (1, D_FF2)), g3=((norm_final_g, m_norm_final_g, v_norm_final_g), (1, D_MODEL)))
    outs, loss_sum = _adamw_small([tuple(t.reshape(shape) for t in wmv) for wmv, shape in reps.values()],
                                  [gathered[n] for n in reps], loss_parts)
    for (n, (wmv, _)), out in zip(reps.items(), outs):
        res[n] = [o.reshape(wmv[0].shape) for o in out]

    names = ("g1", "w_in", "fb", "lg", "sg_w", "sg_b", "w_out", "g2", "w_up", "wc", "bc", "w_down", "g3")
    return (loss_sum, gx[None], *[res[n][0] for n in names], *[res[n][1] for n in names],
            *[res[n][2] for n in names], *[res[n][3] for n in names])
```

```python
import functools
import math

import jax
import jax.numpy as jnp
from jax import lax
from jax.experimental import pallas as pl
from jax.experimental.pallas import tpu as pltpu

F32 = jnp.float32
BF16 = jnp.bfloat16

D_MODEL = 1024
HEAD_DIM = 64
N_HEADS = 8
D_A = 512
D_B = 512
D_IN = 2 * D_A + 3 * D_B + N_HEADS
D_IN_PAD = 2688
D_FF = 2816
D_FF2 = 2 * D_FF
SG_BLOCK = 128
SG_CHUNK = 64
EPS = 1e-6
N_DEV = 8
LANES = 128
NEG = -1e30
VMEM_LIMIT = 56 * 1024 * 1024

ADAM_LR = 0.001
ADAM_B1 = 0.9
ADAM_B2 = 0.999
ADAM_EPS = 1e-08
ADAM_WD = 0.01
ADAM_STEP = 10

TM = 256
TM_MID = 512
TM_GMLP = 1024
TK_DW = 2048
TQ = 512
FWD_UNROLL = 4
BWD_UNROLL = 2
CW = 256

MESH_AXES = ("x", "y", "c")
ANY = pl.BlockSpec(memory_space=pl.ANY)


def _cp(*sem):
    return pltpu.CompilerParams(dimension_semantics=sem, vmem_limit_bytes=VMEM_LIMIT)


def _dot(a, b):
    return jnp.dot(a, b, preferred_element_type=F32)


def _dot_nt(a, b):
    return lax.dot_general(a, b, (((1,), (1,)), ((), ())), preferred_element_type=F32)


def _dot_tn(a, b):
    return lax.dot_general(a, b, (((0,), (0,)), ((), ())), preferred_element_type=F32)


def _dot_f32(a, b):
    return jnp.dot(a, b, precision=lax.Precision.HIGHEST, preferred_element_type=F32)


def _gelu(z):
    return 0.5 * z * (1.0 + lax.erf(z * (1.0 / math.sqrt(2.0))))


def _gelu_grad(z):
    return 0.5 * (1.0 + lax.erf(z * (1.0 / math.sqrt(2.0)))) + z * jnp.exp(-0.5 * z * z) * (1.0 / math.sqrt(2.0 * math.pi))


def _log_sigmoid(x):
    return jnp.minimum(x, 0.0) - jnp.log1p(jnp.exp(-jnp.abs(x)))


def _rms(x):
    r = lax.rsqrt(jnp.mean(x * x, axis=-1, keepdims=True) + EPS)
    return r, x * r


def _rms_bwd(dy, n, r, g):
    dn = dy * g
    return r * (dn - n * jnp.mean(dn * n, axis=-1, keepdims=True))


def _full(shape, single=False):
    nd = len(shape)
    if single:
        return pl.BlockSpec(shape, lambda *_: (0,) * nd, pipeline_mode=pl.Buffered(1))
    return pl.BlockSpec(shape, lambda *_: (0,) * nd)


def _rows(tm, cols, rev_nt=None):
    if rev_nt is None:
        return pl.BlockSpec((tm, cols), lambda i: (i, 0))
    return pl.BlockSpec((tm, cols), lambda i: (rev_nt - 1 - i, 0))


def _head_sl(h):
    return slice(HEAD_DIM * h, HEAD_DIM * (h + 1))


L_ROW = HEAD_DIM
L_COL = HEAD_DIM + 3
L_LSE = HEAD_DIM + 6


def _split3(x):
    hi = x.astype(BF16).astype(F32)
    mid = (x - hi).astype(BF16).astype(F32)
    lo = (x - hi - mid).astype(BF16).astype(F32)
    return hi, mid, lo


def _lanes(rows, width, parts):
    lane = lax.broadcasted_iota(jnp.int32, (rows, width), 1)
    out = jnp.zeros((rows, width), F32)
    for at, val in parts.items():
        out = jnp.where(lane == at, val, out)
    return out


def _fwd_in(x, g1, w_in_p, fb):
    T = x.shape[0]
    tm = min(TM, T)

    def body(x_ref, g_ref, w_ref, fb_ref, place_ref, zuv_ref, qa_ref, ka_ref, va_ref, fl_ref, h1_ref, carry):
        @pl.when(pl.program_id(0) == 0)
        def _():
            carry[...] = jnp.zeros_like(carry)

        r, n = _rms(x_ref[...])
        h = (n * g_ref[...]).astype(BF16)
        h1_ref[...] = h
        z = _dot(h, w_ref[...])
        zuv_ref[...] = z[:, :2 * D_A]
        o = 2 * D_A
        fl = z[:, o + 3 * D_B:] + fb_ref[...]
        fl_ref[...] = fl
        tri = (lax.broadcasted_iota(jnp.int32, (tm, tm), 0) >= lax.broadcasted_iota(jnp.int32, (tm, tm), 1)).astype(F32)
        c = _dot_f32(tri, _log_sigmoid(fl)) + carry[...]
        carry[...] = c[tm - 1:tm, :]
        hi, mid, lo = _split3(c)
        parts = jnp.concatenate([hi.astype(BF16), mid.astype(BF16), lo.astype(BF16)], axis=-1)
        placed = _dot(parts, place_ref[...])
        lane = lax.broadcasted_iota(jnp.int32, (tm, LANES), 1)
        data = lane < HEAD_DIM
        ones_q = ((lane >= L_COL) & (lane < L_COL + 3)).astype(F32)
        ones_k = (((lane >= L_ROW) & (lane < L_ROW + 3)) | ((lane >= L_LSE) & (lane < L_LSE + 3))).astype(F32)
        ones_v = ((lane >= L_ROW) & (lane < L_ROW + 3)).astype(F32)
        for hd in range(N_HEADS):
            def rows_of(first_col):
                tile = z[:, first_col + LANES * (hd // 2):first_col + LANES * (hd // 2 + 1)]
                return tile if hd % 2 == 0 else pltpu.roll(tile, HEAD_DIM, axis=1)

            hs = slice(LANES * hd, LANES * (hd + 1))
            qa_ref[hd] = jnp.where(data, rows_of(o) * (HEAD_DIM ** -0.5), placed[:, hs] + ones_q).astype(BF16)
            key_side = pltpu.roll(placed[:, hs], L_COL - L_ROW, axis=1)
            ka_ref[hd] = jnp.where(data, rows_of(o + D_B), ones_k - key_side).astype(BF16)
            va_ref[hd] = jnp.where(data, rows_of(o + 2 * D_B), ones_v).astype(BF16)

    heads = pl.BlockSpec((N_HEADS, tm, LANES), lambda i: (0, i, 0))
    aug = jax.ShapeDtypeStruct((N_HEADS, T, LANES), BF16)
    j, h = jnp.arange(3 * LANES) // LANES, jnp.arange(3 * LANES) % LANES
    place = ((h[:, None] < N_HEADS)
             & (jnp.arange(N_HEADS * LANES)[None, :] == LANES * h[:, None] + L_ROW + j[:, None])).astype(BF16)
    return pl.pallas_call(
        body, name="fwd_in", grid=(T // tm,),
        in_specs=[_rows(tm, D_MODEL), _full((1, D_MODEL)), _full((D_MODEL, D_IN_PAD), True), _full((1, LANES)),
                  _full((3 * LANES, N_HEADS * LANES))],
        out_specs=[_rows(tm, 2 * D_A), heads, heads, heads, _rows(tm, LANES), _rows(tm, D_MODEL)],
        out_shape=[jax.ShapeDtypeStruct((T, 2 * D_A), F32), aug, aug, aug, jax.ShapeDtypeStruct((T, LANES), F32),
                   jax.ShapeDtypeStruct((T, D_MODEL), BF16)],
        scratch_shapes=[pltpu.VMEM((1, LANES), F32)],
        compiler_params=_cp("arbitrary"),
    )(x, g1, w_in_p, fb, place)


def _segmean(x, avg_ref, parts):
    out, rest = None, x
    for _ in range(parts):
        piece = rest.astype(BF16)
        term = _dot(piece, avg_ref[...])
        out = term if out is None else out + term
        rest = rest - piece.astype(F32)
    return out


def _layer_norm(v, lg, avg_ref):
    d = v - _segmean(v, avg_ref, 3)
    rstd = lax.rsqrt(_segmean(d * d, avg_ref, 2) + EPS)
    vhat = d * rstd
    return vhat, rstd, (vhat * lg).astype(BF16)


def _mix_heads(w_ref, x, row_slice):
    low = lax.broadcasted_iota(jnp.int32, (SG_BLOCK, LANES), 1) < HEAD_DIM
    tiles = []
    for p in range(N_HEADS // 2):
        xt = x[row_slice, LANES * p:LANES * (p + 1)]
        zero = jnp.zeros_like(xt)
        tiles.append(_dot(w_ref[2 * p], jnp.where(low, xt, zero)) + _dot(w_ref[2 * p + 1], jnp.where(low, zero, xt)))
    return jnp.concatenate(tiles, axis=-1)


def _gmlp_fwd(zuv, lg, wm, bfull, avg):
    T = zuv.shape[0]
    tg = min(TM_GMLP, T)
    nb = tg // SG_BLOCK

    def body(zuv_ref, lg_ref, wm_ref, b_ref, avg_ref, oa_ref):
        u = _gelu(zuv_ref[:, :D_A])
        _, _, vn = _layer_norm(_gelu(zuv_ref[:, D_A:]), lg_ref[...], avg_ref)
        for n in range(nb):
            rs = slice(n * SG_BLOCK, (n + 1) * SG_BLOCK)
            oa_ref[rs, :] = (u[rs] * (_mix_heads(wm_ref, vn, rs) + b_ref[...])).astype(BF16)

    return pl.pallas_call(
        body, name="gmlp_fwd", grid=(T // tg,),
        in_specs=[_rows(tg, 2 * D_A), _full((1, D_A)), _full((N_HEADS, SG_BLOCK, SG_BLOCK)), _full((SG_BLOCK, D_A)),
                  _full((D_A, D_A))],
        out_specs=_rows(tg, D_A),
        out_shape=jax.ShapeDtypeStruct((T, D_A), BF16),
        compiler_params=_cp("arbitrary"),
    )(zuv, lg, wm, bfull, avg)


def _gmlp_bwd(zuv, doa, lg, wm, wmt, bfull, maskf, hsel, avg):
    T = zuv.shape[0]
    tg = min(TM_GMLP, T)
    nb = tg // SG_BLOCK
    nt = T // tg

    def body(zuv_ref, doa_ref, lg_ref, wm_ref, wmt_ref, b_ref, mask_ref, hsel_ref, avg_ref,
             dzuv_ref, dwm_ref, dsgb_ref, dlg_ref, dbacc):
        i = pl.program_id(0)

        @pl.when(i == 0)
        def _():
            dwm_ref[...] = jnp.zeros_like(dwm_ref)
            dlg_ref[...] = jnp.zeros_like(dlg_ref)
            dbacc[...] = jnp.zeros_like(dbacc)

        zu, zv = zuv_ref[:, :D_A], zuv_ref[:, D_A:]
        lgv = lg_ref[...]
        vhat, rstd, vn = _layer_norm(_gelu(zv), lgv, avg_ref)
        dmixed = doa_ref[...] * _gelu(zu)
        dmb = dmixed.astype(BF16)
        low = lax.broadcasted_iota(jnp.int32, (SG_BLOCK, LANES), 1) < HEAD_DIM
        mixed, dvn = [], []
        for n in range(nb):
            rs = slice(n * SG_BLOCK, (n + 1) * SG_BLOCK)
            mixed.append(_mix_heads(wm_ref, vn, rs) + b_ref[...])
            dvn.append(_mix_heads(wmt_ref, dmb, rs))
            dbacc[...] += dmixed[rs]
            for p in range(N_HEADS // 2):
                ls = slice(LANES * p, LANES * (p + 1))
                dmt, vnt = dmb[rs, ls], vn[rs, ls]
                zero = jnp.zeros_like(dmt)
                dwm_ref[2 * p] += _dot_nt(jnp.where(low, dmt, zero), vnt) * mask_ref[...]
                dwm_ref[2 * p + 1] += _dot_nt(jnp.where(low, zero, dmt), vnt) * mask_ref[...]
        mixed = jnp.concatenate(mixed, axis=0) if nb > 1 else mixed[0]
        dvn = jnp.concatenate(dvn, axis=0) if nb > 1 else dvn[0]
        dlg_ref[...] += jnp.sum(dvn * vhat, axis=0, keepdims=True)
        dvhat = dvn * lgv
        dv = rstd * (dvhat - _segmean(dvhat, avg_ref, 2) - vhat * _segmean(dvhat * vhat, avg_ref, 2))
        dzuv_ref[:, :D_A] = (doa_ref[...] * mixed * _gelu_grad(zu)).astype(BF16)
        dzuv_ref[:, D_A:] = (dv * _gelu_grad(zv)).astype(BF16)

        @pl.when(i == nt - 1)
        def _():
            dsgb_ref[...] = lax.dot_general(hsel_ref[...], dbacc[...], (((1,), (1,)), ((), ())),
                                            precision=lax.Precision.HIGHEST, preferred_element_type=F32)

    return pl.pallas_call(
        body, name="gmlp_bwd", grid=(nt,),
        in_specs=[_rows(tg, 2 * D_A), _rows(tg, D_A), _full((1, D_A)), _full((N_HEADS, SG_BLOCK, SG_BLOCK)),
                  _full((N_HEADS, SG_BLOCK, SG_BLOCK)), _full((SG_BLOCK, D_A)), _full((SG_BLOCK, SG_BLOCK)),
                  _full((N_HEADS, D_A)), _full((D_A, D_A))],
        out_specs=[_rows(tg, 2 * D_A), _full((N_HEADS, SG_BLOCK, SG_BLOCK)), _full((N_HEADS, SG_BLOCK)), _full((1, D_A))],
        out_shape=[jax.ShapeDtypeStruct((T, 2 * D_A), BF16), jax.ShapeDtypeStruct((N_HEADS, SG_BLOCK, SG_BLOCK), F32),
                   jax.ShapeDtypeStruct((N_HEADS, SG_BLOCK), F32), jax.ShapeDtypeStruct((1, D_A), F32)],
        scratch_shapes=[pltpu.VMEM((SG_BLOCK, D_A), F32)],
        compiler_params=_cp("arbitrary"),
    )(zuv, doa, lg, wm, wmt, bfull, maskf, hsel, avg)


def _fox_fwd(qa, ka, va, side):
    T = qa.shape[1]
    tq = min(TQ, T)
    nq = T // tq
    n = side.n

    def body(qa_ref, ka_ref, va_ref, *rest):
        ins, (o_ref, lse_ref), outs = rest[:n], rest[n:n + 2], rest[n + 2:2 * n + 2]
        acc_ref, *sems = rest[2 * n + 2:]
        i = pl.program_id(1)
        if n:
            @pl.when((pl.program_id(0) == 0) & (i == 0))
            def _():
                side.start(ins, outs, sems)
        row = lax.broadcasted_iota(jnp.int32, (tq, tq), 0)
        col = lax.broadcasted_iota(jnp.int32, (tq, tq), 1)
        qs = [qa_ref[0], qa_ref[1]]

        def tiles(js, carry, diag):
            offs = [pl.multiple_of(j * tq, tq) for j in js]
            logits = [[_dot_nt(qs[hh], ka_ref[hh, pl.ds(off, tq), :]) for hh in range(2)] for off in offs]
            carry = list(carry)
            for off, per_head in zip(offs, logits):
                for hh, s in enumerate(per_head):
                    if diag:
                        s = jnp.where(col <= row, s, NEG)
                    m = carry[hh]
                    m_new = jnp.maximum(m, jnp.max(s, axis=-1, keepdims=True))
                    pr = jnp.exp(s - m_new)
                    acc_ref[hh] = jnp.exp(m - m_new) * acc_ref[hh] + _dot(pr.astype(BF16), va_ref[hh, pl.ds(off, tq), :])
                    carry[hh] = m_new
            return tuple(carry)

        acc_ref[...] = jnp.zeros_like(acc_ref)
        init = (jnp.full((tq, 1), NEG, F32),) * 2
        carry = lax.fori_loop(0, i // FWD_UNROLL, lambda t, cr: tiles([FWD_UNROLL * t + u for u in range(FWD_UNROLL)], cr, False), init)
        carry = lax.fori_loop(i - i % FWD_UNROLL, i, lambda j, cr: tiles([j], cr, False), carry)
        carry = tiles([i], carry, True)
        for hh in range(2):
            m, acc = carry[hh], acc_ref[hh]
            l = acc[:, L_ROW:L_ROW + 1]
            o_ref[:, _head_sl(hh)] = acc[:, :HEAD_DIM] / l
            hi, mid, lo = _split3(-(m + jnp.log(l)))
            lse_ref[hh] = _lanes(tq, LANES, {L_LSE: hi, L_LSE + 1: mid, L_LSE + 2: lo}).astype(BF16)
        if n:
            @pl.when((pl.program_id(0) == N_HEADS // 2 - 1) & (i == nq - 1))
            def _():
                side.finish(ins, outs, sems)

    tile = pl.BlockSpec((2, tq, LANES), lambda p, i: (p, i, 0))
    seq = pl.BlockSpec((2, T, LANES), lambda p, i: (p, 0, 0))
    res = pl.pallas_call(
        body, name="fox_fwd", grid=(N_HEADS // 2, nq),
        in_specs=[tile, seq, seq, *side.specs],
        out_specs=[pl.BlockSpec((tq, LANES), lambda p, i: (i, p)), tile, *side.specs],
        out_shape=[jax.ShapeDtypeStruct((T, D_B), F32), jax.ShapeDtypeStruct((N_HEADS, T, LANES), BF16), *side.out_shape],
        scratch_shapes=[pltpu.VMEM((2, tq, LANES), F32), *side.scratch],
        compiler_params=_cp("arbitrary", "arbitrary"),
    )(qa, ka, va, *side.operands)
    return res[0], res[1], res[2:]


def _fox_bwd(qa, lse, doa, ka, va, side):
    T = qa.shape[1]
    tq = min(TQ, T)
    nq = T // tq
    n = side.n

    def body(qa_ref, lse_ref, doa_ref, ka_ref, va_ref, *rest):
        ins, (dqa_ref, dka_ref, dva_ref), outs = rest[:n], rest[n:n + 3], rest[n + 3:2 * n + 3]
        dv_acc, *sems = rest[2 * n + 3:]
        j = pl.program_id(1)
        if n:
            @pl.when((pl.program_id(0) == 0) & (j == 0))
            def _():
                side.start(ins, outs, sems)

        @pl.when(j == 0)
        def _():
            dqa_ref[...] = jnp.zeros_like(dqa_ref)

        row = lax.broadcasted_iota(jnp.int32, (tq, tq), 0)
        col = lax.broadcasted_iota(jnp.int32, (tq, tq), 1)
        ks = [ka_ref[0], ka_ref[1]]
        vs = [va_ref[0], va_ref[1]]

        def tiles(ids, diag):
            work = []
            for i in ids:
                off = pl.multiple_of(i * tq, tq)
                for hh in range(2):
                    qi = qa_ref[hh, pl.ds(off, tq), :] + lse_ref[hh, pl.ds(off, tq), :]
                    doi = doa_ref[hh, pl.ds(off, tq), :]
                    work.append((off, hh, qi, doi, _dot_nt(ks[hh], qi), _dot_nt(vs[hh], doi)))
            for off, hh, qi, doi, st, dpt in work:
                if diag:
                    st = jnp.where(row <= col, st, NEG)
                pt = jnp.exp(st)
                dv_acc[hh] += _dot(pt.astype(BF16), doi)
                dsb = (pt * dpt).astype(BF16)
                dka_ref[hh] += _dot(dsb, qi)
                dqa_ref[hh, pl.ds(off, tq), :] += _dot_tn(dsb, ks[hh])

        dka_ref[...] = jnp.zeros_like(dka_ref)
        dv_acc[...] = jnp.zeros_like(dv_acc)
        tiles([j], True)
        todo = nq - 1 - j

        @pl.loop(0, todo // BWD_UNROLL)
        def _(t):
            tiles([j + 1 + BWD_UNROLL * t + u for u in range(BWD_UNROLL)], False)

        @pl.loop(nq - todo % BWD_UNROLL, nq)
        def _(i):
            tiles([i], False)

        dva_ref[...] = dv_acc[...].astype(BF16)
        if n:
            @pl.when((pl.program_id(0) == N_HEADS // 2 - 1) & (j == nq - 1))
            def _():
                side.finish(ins, outs, sems)

    tile = pl.BlockSpec((2, tq, LANES), lambda p, j: (p, j, 0))
    seq = pl.BlockSpec((2, T, LANES), lambda p, j: (p, 0, 0))
    res = pl.pallas_call(
        body, name="fox_bwd", grid=(N_HEADS // 2, nq),
        in_specs=[seq, seq, seq, tile, tile, *side.specs],
        out_specs=[seq, tile, tile, *side.specs],
        out_shape=[jax.ShapeDtypeStruct((N_HEADS, T, LANES), F32), jax.ShapeDtypeStruct((N_HEADS, T, LANES), F32),
                   jax.ShapeDtypeStruct((N_HEADS, T, LANES), BF16), *side.out_shape],
        scratch_shapes=[pltpu.VMEM((2, tq, LANES), F32), *side.scratch],
        compiler_params=_cp("arbitrary", "arbitrary"),
    )(qa, lse, doa, ka, va, *side.operands)
    return res[0], res[1], res[2], res[3:]


def _fwd_mid(x, oa, ob, w_out, g2, w_up):
    T = x.shape[0]
    tm = min(TM_MID, T)

    def body(x_ref, oa_ref, ob_ref, wo_ref, g_ref, wu_ref, x2_ref, h2_ref, a_ref):
        oab = jnp.concatenate([oa_ref[...], ob_ref[...].astype(BF16)], axis=-1)
        x2 = x_ref[...] + _dot(oab, wo_ref[...])
        x2_ref[...] = x2
        _, n = _rms(x2)
        h2 = (n * g_ref[...]).astype(BF16)
        h2_ref[...] = h2
        a_ref[...] = _dot(h2, wu_ref[...])

    return pl.pallas_call(
        body, name="fwd_mid", grid=(T // tm,),
        in_specs=[_rows(tm, D_MODEL), _rows(tm, D_A), _rows(tm, D_B), _full((D_MODEL, D_MODEL), True), _full((1, D_MODEL)),
                  _full((D_MODEL, D_FF2), True)],
        out_specs=[_rows(tm, D_MODEL), _rows(tm, D_MODEL), _rows(tm, D_FF2)],
        out_shape=[jax.ShapeDtypeStruct((T, D_MODEL), F32), jax.ShapeDtypeStruct((T, D_MODEL), BF16),
                   jax.ShapeDtypeStruct((T, D_FF2), F32)],
        compiler_params=_cp("arbitrary"),
    )(x, oa, ob, w_out, g2, w_up)


def _row_before(x, prev, k):
    rolled = pltpu.roll(x, k, axis=0)
    row = lax.broadcasted_iota(jnp.int32, (8, x.shape[1]), 0)
    head = rolled[0:8]
    for r in range(k):
        head = jnp.where(row == r, prev[8 - k + r:9 - k + r], head)
    return jnp.concatenate([head, rolled[8:]], axis=0)


def _row_after(x, nxt, k):
    tm = x.shape[0]
    rolled = pltpu.roll(x, tm - k, axis=0)
    row = lax.broadcasted_iota(jnp.int32, (8, x.shape[1]), 0)
    tail = rolled[tm - 8:tm]
    for r in range(k):
        tail = jnp.where(row == 8 - k + r, nxt[r:r + 1], tail)
    return jnp.concatenate([rolled[:tm - 8], tail], axis=0)


def _fwd_ffn(a, x2, wc, bc, w_down, g3, tgt):
    T = x2.shape[0]
    tm = min(TM, T)

    def body(a_ref, x2_ref, wc_ref, bc_ref, wd_ref, g_ref, tgt_ref, ac_ref, yff_ref, dx3_ref, loss_ref, dg3_ref, carry):
        @pl.when(pl.program_id(0) == 0)
        def _():
            carry[...] = jnp.zeros_like(carry)
            loss_ref[...] = jnp.zeros_like(loss_ref)
            dg3_ref[...] = jnp.zeros_like(dg3_ref)

        def conv(cs):
            a0 = a_ref[:, cs]
            prev = carry[:, cs]
            ac = (wc_ref[0:1, cs] * _row_before(a0, prev, 2) + wc_ref[1:2, cs] * _row_before(a0, prev, 1)
                  + wc_ref[2:3, cs] * a0 + bc_ref[:, cs])
            ac_ref[:, cs] = ac.astype(BF16)
            return ac

        x3 = x2_ref[...]
        for ci in range(D_FF // CW):
            gs = slice(ci * CW, (ci + 1) * CW)
            ag = conv(gs)
            av = conv(slice(D_FF + ci * CW, D_FF + (ci + 1) * CW))
            yb = (ag * jax.nn.sigmoid(ag) * av).astype(BF16)
            yff_ref[:, gs] = yb
            x3 = x3 + _dot(yb, wd_ref[gs, :])
        carry[...] = a_ref[tm - 8:tm, :]
        r, n = _rms(x3)
        g = g_ref[...]
        diff = n * g - tgt_ref[...]
        loss_ref[...] += (0.5 / D_MODEL) * jnp.sum(diff * diff)
        dout = diff * (1.0 / D_MODEL)
        dg3_ref[...] += jnp.sum(dout * n, axis=0, keepdims=True)
        dx3_ref[...] = _rms_bwd(dout, n, r, g)

    return pl.pallas_call(
        body, name="fwd_ffn", grid=(T // tm,),
        in_specs=[_rows(tm, D_FF2), _rows(tm, D_MODEL), _full((3, D_FF2)), _full((1, D_FF2)), _full((D_FF, D_MODEL), True),
                  _full((1, D_MODEL)), _rows(tm, D_MODEL)],
        out_specs=[_rows(tm, D_FF2), _rows(tm, D_FF), _rows(tm, D_MODEL), _full((8, LANES)), _full((1, D_MODEL))],
        out_shape=[jax.ShapeDtypeStruct((T, D_FF2), BF16), jax.ShapeDtypeStruct((T, D_FF), BF16),
                   jax.ShapeDtypeStruct((T, D_MODEL), F32), jax.ShapeDtypeStruct((8, LANES), F32),
                   jax.ShapeDtypeStruct((1, D_MODEL), F32)],
        scratch_shapes=[pltpu.VMEM((8, D_FF2), F32)],
        compiler_params=_cp("arbitrary"),
    )(a, x2, wc, bc, w_down, g3, tgt)


def _bwd_ffn(dx3, a, ac, yff, h2, w_down, wc):
    T = dx3.shape[0]
    tm = min(TM, T)
    nt = T // tm
    half = D_FF // 2
    shard_up, shard_down = D_FF2 // N_DEV, D_FF // N_DEV

    def body(dx3_ref, ag_ref, av_ref, acg_ref, acv_ref, yff_ref, h2_ref, wd_ref, wcg_ref, wcv_ref,
             dag_ref, dav_ref, dwcg_ref, dwcv_ref, dbcg_ref, dbcv_ref, dwd_ref, dwu_ref,
             nxt, shifted, acc_down, acc_g, acc_v, stage_up, stage_down, sem):
        c, r = pl.program_id(0), pl.program_id(1)

        @pl.when(r == 0)
        def _():
            for ref in (nxt, dwcg_ref, dwcv_ref, dbcg_ref, dbcv_ref, acc_down, acc_g, acc_v):
                ref[...] = jnp.zeros_like(ref)

        dxb = dx3_ref[...].astype(BF16)
        dy_all = _dot_nt(dxb, wd_ref[...])

        def back(a_ref, w_ref, da_ref, dwc_ref, dbc_ref, nx, cs, dac):
            a0 = a_ref[:, cs]
            shifted[0] = _row_after(dac, nxt[:, nx], 1)
            shifted[1] = _row_after(dac, nxt[:, nx], 2)
            dp1, dp2 = shifted[0], shifted[1]
            dbc_ref[:, cs] += jnp.sum(dac, axis=0, keepdims=True)
            dwc_ref[0:1, cs] += jnp.sum(dp2 * a0, axis=0, keepdims=True)
            dwc_ref[1:2, cs] += jnp.sum(dp1 * a0, axis=0, keepdims=True)
            dwc_ref[2:3, cs] += jnp.sum(dac * a0, axis=0, keepdims=True)
            da_ref[:, cs] = (w_ref[2:3, cs] * dac + w_ref[1:2, cs] * dp1 + w_ref[0:1, cs] * dp2).astype(BF16)
            nxt[:, nx] = dac[0:8]

        for ci in range(half // LANES):
            cs = slice(ci * LANES, (ci + 1) * LANES)
            dy = dy_all[:, cs]
            ag, av = acg_ref[:, cs].astype(F32), acv_ref[:, cs].astype(F32)
            sg = jax.nn.sigmoid(ag)
            back(av_ref, wcv_ref, dav_ref, dwcv_ref, dbcv_ref, slice(half + ci * LANES, half + (ci + 1) * LANES), cs,
                 dy * (ag * sg))
            back(ag_ref, wcg_ref, dag_ref, dwcg_ref, dbcg_ref, cs, cs, dy * av * (sg * (1.0 + ag * (1.0 - sg))))

        acc_down[...] += _dot_tn(yff_ref[...], dxb)
        h2 = h2_ref[...]
        acc_g[...] += _dot_tn(h2, dag_ref[...])
        acc_v[...] += _dot_tn(h2, dav_ref[...])

        @pl.when(r == nt - 1)
        def _():
            for s in range(half // shard_down):
                stage_down[...] = acc_down[s * shard_down:(s + 1) * shard_down, :].astype(BF16)
                out = pltpu.make_async_copy(stage_down, dwd_ref.at[(half // shard_down) * c + s], sem)
                out.start()
                out.wait()
            for acc, first in ((acc_g, 0), (acc_v, N_DEV // 2)):
                for s in range(half // shard_up):
                    stage_up[...] = acc[:, s * shard_up:(s + 1) * shard_up].astype(BF16)
                    out = pltpu.make_async_copy(stage_up, dwu_ref.at[first + (half // shard_up) * c + s], sem)
                    out.start()
                    out.wait()

    def cols(width, second_half):
        return pl.BlockSpec((tm, width), lambda c, r: (nt - 1 - r, c + (2 if second_half else 0)))

    def param(rows, second_half):
        return pl.BlockSpec((rows, half), lambda c, r: (0, c + (2 if second_half else 0)))

    tokens = pl.BlockSpec((tm, D_MODEL), lambda c, r: (nt - 1 - r, 0))
    return pl.pallas_call(
        body, name="bwd_ffn", grid=(2, nt),
        in_specs=[tokens, cols(half, False), cols(half, True), cols(half, False), cols(half, True), cols(half, False), tokens,
                  pl.BlockSpec((half, D_MODEL), lambda c, r: (c, 0), pipeline_mode=pl.Buffered(1)),
                  param(3, False), param(3, True)],
        out_specs=[cols(half, False), cols(half, False), param(3, False), param(3, False), param(1, False), param(1, False),
                   ANY, ANY],
        out_shape=[jax.ShapeDtypeStruct((T, D_FF), BF16), jax.ShapeDtypeStruct((T, D_FF), BF16),
                   jax.ShapeDtypeStruct((3, D_FF), F32), jax.ShapeDtypeStruct((3, D_FF), F32),
                   jax.ShapeDtypeStruct((1, D_FF), F32), jax.ShapeDtypeStruct((1, D_FF), F32),
                   jax.ShapeDtypeStruct((N_DEV, shard_down, D_MODEL), BF16), jax.ShapeDtypeStruct((N_DEV, D_MODEL, shard_up), BF16)],
        scratch_shapes=[pltpu.VMEM((8, D_FF), F32), pltpu.VMEM((2, tm, LANES), F32), pltpu.VMEM((half, D_MODEL), F32),
                        pltpu.VMEM((D_MODEL, half), F32), pltpu.VMEM((D_MODEL, half), F32),
                        pltpu.VMEM((D_MODEL, shard_up), BF16), pltpu.VMEM((shard_down, D_MODEL), BF16),
                        pltpu.SemaphoreType.DMA],
        compiler_params=_cp("arbitrary", "arbitrary"),
    )(dx3, a, a, ac, ac, yff, h2, w_down, wc, wc)


def _bwd_mid(da_g, da_v, w_up, x2, g2, dx3, w_out, ob):
    T = x2.shape[0]
    tm = min(TM, T)

    def body(dag_ref, dav_ref, wu_ref, x2_ref, g_ref, dx3_ref, wo_ref, ob_ref, hsum_ref, place_ref,
             dx2_ref, doa_ref, dob_ref, dg2_ref):
        @pl.when(pl.program_id(0) == 0)
        def _():
            dg2_ref[...] = jnp.zeros_like(dg2_ref)

        dh2 = _dot_nt(dag_ref[...], wu_ref[:, :D_FF]) + _dot_nt(dav_ref[...], wu_ref[:, D_FF:])
        r, n = _rms(x2_ref[...])
        dg2_ref[...] += jnp.sum(dh2 * n, axis=0, keepdims=True)
        dx2 = dx3_ref[...] + _rms_bwd(dh2, n, r, g_ref[...])
        dx2_ref[...] = dx2
        doab = _dot_nt(dx2.astype(BF16), wo_ref[...])
        doa_ref[...] = doab[:, :D_A]
        dob = doab[:, D_A:]
        rest, delta = dob.astype(BF16).astype(F32) * ob_ref[...], None
        for _ in range(3):
            piece = rest.astype(BF16)
            term = _dot(piece, hsum_ref[...])
            delta = term if delta is None else delta + term
            rest = rest - piece.astype(F32)
        hi, mid, lo = _split3(-delta)
        parts = jnp.concatenate([hi.astype(BF16), mid.astype(BF16), lo.astype(BF16)], axis=-1)
        placed = _dot(parts, place_ref[...])
        data = lax.broadcasted_iota(jnp.int32, (tm, LANES), 1) < HEAD_DIM
        for hd in range(N_HEADS):
            tile = dob[:, LANES * (hd // 2):LANES * (hd // 2 + 1)]
            rows = tile if hd % 2 == 0 else pltpu.roll(tile, HEAD_DIM, axis=1)
            dob_ref[hd] = jnp.where(data, rows, placed[:, LANES * hd:LANES * (hd + 1)]).astype(BF16)

    hsum = (jnp.arange(D_B)[:, None] // HEAD_DIM == jnp.arange(LANES)[None, :]).astype(BF16)
    j, h = jnp.arange(3 * LANES) // LANES, jnp.arange(3 * LANES) % LANES
    place = ((h[:, None] < N_HEADS)
             & (jnp.arange(N_HEADS * LANES)[None, :] == LANES * h[:, None] + L_ROW + j[:, None])).astype(BF16)
    return pl.pallas_call(
        body, name="bwd_mid", grid=(T // tm,),
        in_specs=[_rows(tm, D_FF), _rows(tm, D_FF), _full((D_MODEL, D_FF2), True), _rows(tm, D_MODEL), _full((1, D_MODEL)),
                  _rows(tm, D_MODEL), _full((D_MODEL, D_MODEL), True), _rows(tm, D_B), _full((D_B, LANES)),
                  _full((3 * LANES, N_HEADS * LANES))],
        out_specs=[_rows(tm, D_MODEL), _rows(tm, D_A), pl.BlockSpec((N_HEADS, tm, LANES), lambda i: (0, i, 0)),
                   _full((1, D_MODEL))],
        out_shape=[jax.ShapeDtypeStruct((T, D_MODEL), F32), jax.ShapeDtypeStruct((T, D_A), F32),
                   jax.ShapeDtypeStruct((N_HEADS, T, LANES), BF16), jax.ShapeDtypeStruct((1, D_MODEL), F32)],
        compiler_params=_cp("arbitrary"),
    )(da_g, da_v, w_up, x2, g2, dx3, w_out, ob, hsum, place)


def _bwd_in(dzuv, dqa, dka, dva, fl, x, dx2, w_in_p, g1):
    T = x.shape[0]
    tm = min(TM, T)
    nt = T // tm

    def body(dzuv_ref, dqa_ref, dka_ref, dva_ref, fl_ref, x_ref, dx2_ref, w_ref, g_ref,
             gx_ref, dz_ref, dg1_ref, dfb_ref, carry):
        @pl.when(pl.program_id(0) == 0)
        def _():
            carry[...] = jnp.zeros_like(carry)
            dg1_ref[...] = jnp.zeros_like(dg1_ref)
            dfb_ref[...] = jnp.zeros_like(dfb_ref)

        dc = _lanes(tm, LANES, {hd: dqa_ref[hd][:, L_ROW:L_ROW + 1] - dka_ref[hd][:, L_COL:L_COL + 1] for hd in range(N_HEADS)})
        later = (lax.broadcasted_iota(jnp.int32, (tm, tm), 1) >= lax.broadcasted_iota(jnp.int32, (tm, tm), 0)).astype(F32)
        dls = _dot_f32(later, dc) + carry[...]
        carry[...] = dls[0:1, :]
        dzf = dls * jax.nn.sigmoid(-fl_ref[...])
        dfb_ref[...] += jnp.sum(dzf, axis=0, keepdims=True)
        data = lax.broadcasted_iota(jnp.int32, (tm, LANES), 1) < HEAD_DIM

        def compact(ref, scale=None):
            def rows(hd):
                return (ref[hd] if scale is None else ref[hd] * scale).astype(BF16)

            return [jnp.where(data, rows(2 * p), pltpu.roll(rows(2 * p + 1), HEAD_DIM, axis=1)) for p in range(N_HEADS // 2)]

        dz = jnp.concatenate([dzuv_ref[...], *compact(dqa_ref, HEAD_DIM ** -0.5), *compact(dka_ref), *compact(dva_ref),
                              dzf.astype(BF16)], axis=-1)
        dz_ref[...] = dz
        dh1 = _dot_nt(dz, w_ref[...])
        r, n = _rms(x_ref[...])
        dg1_ref[...] += jnp.sum(dh1 * n, axis=0, keepdims=True)
        gx_ref[...] = dx2_ref[...] + _rms_bwd(dh1, n, r, g_ref[...])

    rv = functools.partial(_rows, tm, rev_nt=nt)
    heads = pl.BlockSpec((N_HEADS, tm, LANES), lambda i: (0, nt - 1 - i, 0))
    return pl.pallas_call(
        body, name="bwd_in", grid=(nt,),
        in_specs=[rv(2 * D_A), heads, heads, heads, rv(LANES), rv(D_MODEL), rv(D_MODEL),
                  _full((D_MODEL, D_IN_PAD), True), _full((1, D_MODEL))],
        out_specs=[rv(D_MODEL), rv(D_IN_PAD), _full((1, D_MODEL)), _full((1, LANES))],
        out_shape=[jax.ShapeDtypeStruct((T, D_MODEL), F32), jax.ShapeDtypeStruct((T, D_IN_PAD), BF16),
                   jax.ShapeDtypeStruct((1, D_MODEL), F32), jax.ShapeDtypeStruct((1, LANES), F32)],
        scratch_shapes=[pltpu.VMEM((1, LANES), F32)],
        compiler_params=_cp("arbitrary"),
    )(dzuv, dqa, dka, dva, fl, x, dx2, w_in_p, g1)


def _matmul_tn(a_parts, b, tmm, tn, tk, name, shard_cols=None, n_valid=None):
    T = b.shape[0]
    widths = [a.shape[1] for a in a_parts]
    M, N = sum(widths), b.shape[1]
    tk = min(tk, T)
    nk = T // tk
    part_w = tmm // len(a_parts)
    n_valid = N if n_valid is None else n_valid

    def body(*refs):
        a_refs, b_ref, o_ref, obf_ref = refs[:len(a_parts)], refs[-3], refs[-2], refs[-1]
        k = pl.program_id(2)

        @pl.when(k == 0)
        def _():
            o_ref[...] = jnp.zeros_like(o_ref)

        a = [r[...].astype(BF16) for r in a_refs]
        o_ref[...] += _dot_tn(a[0] if len(a) == 1 else jnp.concatenate(a, axis=-1), b_ref[...].astype(BF16))

        @pl.when(k == nk - 1)
        def _():
            if shard_cols is None:
                obf_ref[...] = o_ref[...].astype(BF16)
            else:
                for d in range(min(tn, n_valid) // shard_cols):
                    obf_ref[d] = o_ref[:, d * shard_cols:(d + 1) * shard_cols].astype(BF16)

    if shard_cols is None:
        bf_spec, bf_shape = pl.BlockSpec((tmm, tn), lambda i, j, k: (i, j)), (M, N)
    else:
        per_tile = min(tn, n_valid) // shard_cols
        bf_spec, bf_shape = pl.BlockSpec((per_tile, tmm, shard_cols), lambda i, j, k: (j, i, 0)), (N_DEV, M, shard_cols)
    a_specs = [pl.BlockSpec((tk, part_w), lambda i, j, k: (k, i)) for _ in a_parts]
    return pl.pallas_call(
        body, name=name, grid=(M // tmm, N // tn, nk),
        in_specs=[*a_specs, pl.BlockSpec((tk, tn), lambda i, j, k: (k, j))],
        out_specs=[pl.BlockSpec((tmm, tn), lambda i, j, k: (i, j)), bf_spec],
        out_shape=[jax.ShapeDtypeStruct((M, N), F32), jax.ShapeDtypeStruct(bf_shape, BF16)],
        compiler_params=_cp("arbitrary", "arbitrary", "arbitrary"),
    )(*a_parts, b)


class _Exchange:
    def __init__(self, gather, scatter, relay):
        self.n_g, self.n, self.relay = len(gather), len(gather) + len(scatter), relay
        self.operands = [*gather, *scatter]
        self.out_shape = [jax.ShapeDtypeStruct((N_DEV, *g.shape), g.dtype) for g in gather]
        self.out_shape += [jax.ShapeDtypeStruct(s.shape, s.dtype) for s in scatter]
        self.specs = [ANY] * self.n
        n = self.n
        self.scratch = [pltpu.SemaphoreType.DMA((7 * n,)), pltpu.SemaphoreType.DMA((7 * n,)),
                        pltpu.SemaphoreType.DMA((n,))] if n else []

    def _plan(self, ins, outs, sems):
        send_sems, recv_sems, local_sems = sems
        x, y, c = (lax.axis_index(ax) for ax in MESH_AXES)
        me = 4 * x + 2 * y + c
        sibling = (x, y, 1 - c)
        chips = [(1 - x, y), (x, 1 - y), (1 - x, 1 - y)]
        peers = [sibling] + [(*chip, c) for chip in chips] + [(*chip, 1 - c) for chip in chips]

        def index(dev):
            return 4 * dev[0] + 2 * dev[1] + dev[2]

        def remote(k, src, dst, to):
            return pltpu.make_async_remote_copy(src_ref=src, dst_ref=dst, send_sem=send_sems.at[k], recv_sem=recv_sems.at[k],
                                                device_id=to, device_id_type=pl.DeviceIdType.MESH)

        local, sends, relays, recvs = [], [], [], []
        for a in range(self.n):
            src, out, base = ins[a], outs[a], 7 * a
            if a >= self.n_g:
                local.append(pltpu.make_async_copy(src.at[me], out.at[me], local_sems.at[a]))
                sends += [remote(base + k, src.at[index(peer)], out.at[me], peer) for k, peer in enumerate(peers)]
            else:
                local.append(pltpu.make_async_copy(src, out.at[me], local_sems.at[a]))
                sends += [remote(base + k, src, out.at[me], peer) for k, peer in enumerate(peers[:4 if self.relay else 7])]
            for k, peer in enumerate(peers):
                slot = out.at[index(peer)]
                if a < self.n_g and self.relay and k >= 4:
                    continue
                recv = remote(base + k, slot, slot, peer)
                if a < self.n_g and self.relay and k >= 1:
                    relays.append((recv, remote(base + 3 + k, slot, slot, sibling)))
                else:
                    recvs.append(recv)
            if a < self.n_g and self.relay:
                for j, chip in enumerate(chips):
                    slot = out.at[index((*chip, 1 - c))]
                    recvs.append(remote(base + 4 + j, slot, slot, sibling))
        return local, sends, relays, recvs

    def start(self, ins, outs, sems):
        local, sends, _, _ = self._plan(ins, outs, sems)
        for cp in local + sends:
            cp.start()

    def finish(self, ins, outs, sems):
        local, sends, relays, recvs = self._plan(ins, outs, sems)
        for recv, fwd in relays:
            recv.wait_recv()
            fwd.start()
        for recv in recvs:
            recv.wait_recv()
        for cp in sends + [fwd for _, fwd in relays]:
            cp.wait_send()
        for cp in local:
            cp.wait()


def _exchange(gather, scatter, name):
    ex = _Exchange(gather, scatter, relay=True)
    n = ex.n

    def body(*refs):
        ins, outs, sems = refs[:n], refs[n:2 * n], refs[2 * n:]
        ex.start(ins, outs, sems)
        ex.finish(ins, outs, sems)

    return pl.pallas_call(body, name=name, in_specs=ex.specs, out_specs=ex.specs, out_shape=ex.out_shape,
                          scratch_shapes=ex.scratch)(*ex.operands)


def _adamw(w, g, m, v):
    m = ADAM_B1 * m + (1.0 - ADAM_B1) * g
    v = ADAM_B2 * v + (1.0 - ADAM_B2) * jnp.square(g)
    m_hat = m / (1.0 - ADAM_B1 ** ADAM_STEP)
    v_hat = v / (1.0 - ADAM_B2 ** ADAM_STEP)
    delta = -ADAM_LR * (m_hat / (jnp.sqrt(v_hat) + ADAM_EPS) + ADAM_WD * w)
    return delta, m, v


def _adamw_shard(w, m, v, recv, tr, name):
    _, R, C = w.shape

    def body(w_ref, m_ref, v_ref, recv_ref, g_ref, d_ref, nm_ref, nv_ref):
        g = recv_ref[0].astype(F32)
        for d in range(1, N_DEV):
            g = g + recv_ref[d].astype(F32)
        g_ref[...] = g
        d_ref[...], nm_ref[...], nv_ref[...] = _adamw(w_ref[...], g, m_ref[...], v_ref[...])

    blk = pl.BlockSpec((None, tr, C), lambda i: (0, i, 0))
    return pl.pallas_call(
        body, name=name, grid=(R // tr,),
        in_specs=[blk, blk, blk, pl.BlockSpec((N_DEV, tr, C), lambda i: (0, i, 0))],
        out_specs=[blk] * 4, out_shape=[jax.ShapeDtypeStruct((1, R, C), F32)] * 4,
        compiler_params=_cp("arbitrary"),
    )(w, m, v, recv)


def _adamw_small(params, gathered, loss_parts):
    n = len(params)

    def body(*refs):
        ins, gs, loss_ref, outs = refs[:3 * n], refs[3 * n:4 * n], refs[4 * n], refs[4 * n + 1:]
        for p in range(n):
            w_ref, m_ref, v_ref = ins[3 * p:3 * p + 3]
            g = gs[p][0]
            for d in range(1, N_DEV):
                g = g + gs[p][d]
            g = g[..., :w_ref.shape[-1]]
            g_ref, d_ref, nm_ref, nv_ref = outs[4 * p:4 * p + 4]
            g_ref[...] = g
            d_ref[...], nm_ref[...], nv_ref[...] = _adamw(w_ref[...], g, m_ref[...], v_ref[...])
        total = loss_ref[0]
        for d in range(1, N_DEV):
            total = total + loss_ref[d]
        outs[4 * n][...] = total

    out_shape = [jax.ShapeDtypeStruct(w.shape, F32) for w, _, _ in params for _ in range(4)]
    res = pl.pallas_call(body, name="adamw_small", out_shape=[*out_shape, jax.ShapeDtypeStruct((8, LANES), F32)],
                         compiler_params=pltpu.CompilerParams(vmem_limit_bytes=VMEM_LIMIT))(
        *[t for p in params for t in p], *gathered, loss_parts)
    return [res[4 * p:4 * p + 4] for p in range(n)], res[4 * n][0, 0]


def _col_shards(g):
    return jnp.transpose(g.reshape(g.shape[0], N_DEV, -1), (1, 0, 2))


def _row_shards(g):
    return g.reshape(N_DEV, -1, g.shape[1])


def _cols_whole(g):
    return jnp.transpose(g, (1, 0, 2)).reshape(g.shape[1], -1)


def kernel(x, norm_mix_g, w_in, f_bias, sg_ln_g, sg_w, sg_b, w_out, norm_ffn_g, w_up, w_conv, b_conv, w_down, norm_final_g, loss_target, m_norm_mix_g, m_w_in, m_f_bias, m_sg_ln_g, m_sg_w, m_sg_b, m_w_out, m_norm_ffn_g, m_w_up, m_w_conv, m_b_conv, m_w_down, m_norm_final_g, v_norm_mix_g, v_w_in, v_f_bias, v_sg_ln_g, v_sg_w, v_sg_b, v_w_out, v_norm_ffn_g, v_w_up, v_w_conv, v_b_conv, v_w_down, v_norm_final_g):
    xs, tgt = x[0], loss_target[0]
    g1, g2, g3 = norm_mix_g, norm_ffn_g, norm_final_g.reshape(1, D_MODEL)
    lg = sg_ln_g.reshape(1, D_A)
    fb = jnp.pad(f_bias, ((0, 0), (0, LANES - N_HEADS)))
    pos_chunk = jnp.arange(SG_BLOCK) // SG_CHUNK
    maskf = (pos_chunk[:, None] >= pos_chunk[None, :]).astype(F32)
    wm = (sg_w[0] * maskf[None]).astype(BF16)
    wmt = jnp.swapaxes(wm, 1, 2)
    bfull = jnp.repeat(sg_b[0].T, HEAD_DIM, axis=1)
    hsel = jnp.repeat(jnp.eye(N_HEADS, dtype=F32), HEAD_DIM, axis=1)
    avg = (jnp.repeat(hsel, HEAD_DIM, axis=0) * (1.0 / HEAD_DIM)).astype(BF16)

    (win_g,) = _exchange([w_in[0].astype(BF16)], [], "gather_w_in")
    w_in_p = jnp.pad(_cols_whole(win_g), ((0, 0), (0, D_IN_PAD - D_IN)))
    zuv, qa, ka, va, fl, h1 = _fwd_in(xs, g1, w_in_p, fb)
    oa = _gmlp_fwd(zuv, lg, wm, bfull, avg)
    rest = _Exchange([w_out[0].astype(BF16), w_up[0].astype(BF16), w_down[0].astype(BF16), w_conv[0]], [], relay=False)
    ob, lse, (wout_g, wup_g, wdown_g, wc_g) = _fox_fwd(qa, ka, va, rest)
    w_out_f, w_up_f = wout_g.reshape(D_MODEL, D_MODEL), _cols_whole(wup_g)
    w_down_f, wc_f = wdown_g.reshape(D_FF, D_MODEL), _cols_whole(wc_g)

    x2, h2, a = _fwd_mid(xs, oa, ob, w_out_f, g2, w_up_f)
    ac, yff, dx3, loss, dg3 = _fwd_ffn(a, x2, wc_f, b_conv, w_down_f, g3, tgt)
    da_g, da_v, dwc_g, dwc_v, dbc_g, dbc_v, dwdown_bf, dwup_bf = _bwd_ffn(dx3, a, ac, yff, h2, w_down_f, wc_f)
    dwc, dbc = jnp.concatenate([dwc_g, dwc_v], axis=1), jnp.concatenate([dbc_g, dbc_v], axis=1)
    dx2, doa, dob, dg2 = _bwd_mid(da_g, da_v, w_up_f, x2, g2, dx3, w_out_f, ob)
    dzuv, dwm, dsgb, dlg = _gmlp_bwd(zuv, doa, lg, wm, wmt, bfull, maskf, hsel, avg)
    _, dwout_bf = _matmul_tn([oa, ob], dx2, D_MODEL, D_MODEL, TK_DW, "dw_out")

    early = ("w_out", "w_up", "wc", "w_down")
    wire = [_row_shards(dwout_bf), dwup_bf, _col_shards(dwc).astype(BF16), dwdown_bf]
    small_early = dict(lg=dlg, sg_w=dwm, sg_b=dsgb, g2=dg2, bc=dbc, g3=dg3)
    grads = _Exchange([*small_early.values(), loss], wire, relay=False)
    dqa, dka, dva, got = _fox_bwd(qa, lse, dob, ka, va, grads)
    n_small = len(small_early)
    gathered, loss_parts = dict(zip(small_early, got[:n_small])), got[n_small]
    recv = dict(zip(early, got[n_small + 1:]))

    gx, dz, dg1, dfb = _bwd_in(dzuv, dqa, dka, dva, fl, xs, dx2, w_in_p, g1)
    _, dwin_bf = _matmul_tn([h1], dz, D_MODEL // 2, D_IN_PAD, TK_DW, "dw_in", shard_cols=D_IN // N_DEV, n_valid=D_IN)
    gathered["g1"], gathered["fb"], recv["w_in"] = _exchange([dg1, dfb], [dwin_bf], "exchange_w_in")

    weights = dict(w_in=(w_in, m_w_in, v_w_in, 256), w_out=(w_out, m_w_out, v_w_out, 128), w_up=(w_up, m_w_up, v_w_up, 256),
                   wc=(w_conv, m_w_conv, v_w_conv, 3), w_down=(w_down, m_w_down, v_w_down, 176))
    res = {n: _adamw_shard(w, m, v, recv[n], tr, "adamw_" + n) for n, (w, m, v, tr) in weights.items()}

    reps = dict(g1=((norm_mix_g, m_norm_mix_g, v_norm_mix_g), (1, D_MODEL)), fb=((f_bias, m_f_bias, v_f_bias), (1, N_HEADS)),
                lg=((sg_ln_g, m_sg_ln_g, v_sg_ln_g), (1, D_A)), sg_w=((sg_w, m_sg_w, v_sg_w), (N_HEADS, SG_BLOCK, SG_BLOCK)),
                sg_b=((sg_b, m_sg_b, v_sg_b), (N_HEADS, SG_BLOCK)), g2=((norm_ffn_g, m_norm_ffn_g, v_norm_ffn_g), (1, D_MODEL)),
                bc=((b_conv, m_b_conv, v_b_conv), (1, D_FF2)), g3=((norm_final_g, m_norm_final_g, v_norm_final_g), (1, D_MODEL)))
    outs, loss_sum = _adamw_small([tuple(t.reshape(shape) for t in wmv) for wmv, shape in reps.values()],
                                  [gathered[n] for n in reps], loss_parts)
    for (n, (wmv, _)), out in zip(reps.items(), outs):
        res[n] = [o.reshape(wmv[0].shape) for o in out]

    names = ("g1", "w_in", "fb", "lg", "sg_w", "sg_b", "w_out", "g2", "w_up", "wc", "bc", "w_down", "g3")
    return (loss_sum, gx[None], *[res[n][0] for n in names], *[res[n][1] for n in names],
            *[res[n][2] for n in names], *[res[n][3] for n in names])
```

```python
import functools
import math

import jax
import jax.numpy as jnp
from jax import lax
from jax.experimental import pallas as pl
from jax.experimental.pallas import tpu as pltpu

F32 = jnp.float32
BF16 = jnp.bfloat16

D_MODEL = 1024
HEAD_DIM = 64
N_HEADS = 8
D_A = 512
D_B = 512
D_IN = 2 * D_A + 3 * D_B + N_HEADS
D_IN_PAD = 2688
D_FF = 2816
D_FF2 = 2 * D_FF
SG_BLOCK = 128
SG_CHUNK = 64
EPS = 1e-6
N_DEV = 8
LANES = 128
NEG = -1e30
VMEM_LIMIT = 56 * 1024 * 1024

ADAM_LR = 0.001
ADAM_B1 = 0.9
ADAM_B2 = 0.999
ADAM_EPS = 1e-08
ADAM_WD = 0.01
ADAM_STEP = 10

TM = 256
TM_MID = 512
TM_GMLP = 1024
TK_DW = 2048
TQ = 512
FWD_UNROLL = 4
BWD_UNROLL = 2
CW = 256

MESH_AXES = ("x", "y", "c")
ANY = pl.BlockSpec(memory_space=pl.ANY)


def _cp(*sem):
    return pltpu.CompilerParams(dimension_semantics=sem, vmem_limit_bytes=VMEM_LIMIT)


def _dot(a, b):
    return jnp.dot(a, b, preferred_element_type=F32)


def _dot_nt(a, b):
    return lax.dot_general(a, b, (((1,), (1,)), ((), ())), preferred_element_type=F32)


def _dot_tn(a, b):
    return lax.dot_general(a, b, (((0,), (0,)), ((), ())), preferred_element_type=F32)


def _dot_f32(a, b):
    return jnp.dot(a, b, precision=lax.Precision.HIGHEST, preferred_element_type=F32)


def _gelu(z):
    return 0.5 * z * (1.0 + lax.erf(z * (1.0 / math.sqrt(2.0))))


def _gelu_grad(z):
    return 0.5 * (1.0 + lax.erf(z * (1.0 / math.sqrt(2.0)))) + z * jnp.exp(-0.5 * z * z) * (1.0 / math.sqrt(2.0 * math.pi))


def _log_sigmoid(x):
    return jnp.minimum(x, 0.0) - jnp.log1p(jnp.exp(-jnp.abs(x)))


def _rms(x):
    r = lax.rsqrt(jnp.mean(x * x, axis=-1, keepdims=True) + EPS)
    return r, x * r


def _rms_bwd(dy, n, r, g):
    dn = dy * g
    return r * (dn - n * jnp.mean(dn * n, axis=-1, keepdims=True))


def _full(shape, single=False):
    nd = len(shape)
    if single:
        return pl.BlockSpec(shape, lambda *_: (0,) * nd, pipeline_mode=pl.Buffered(1))
    return pl.BlockSpec(shape, lambda *_: (0,) * nd)


def _rows(tm, cols, rev_nt=None):
    if rev_nt is None:
        return pl.BlockSpec((tm, cols), lambda i: (i, 0))
    return pl.BlockSpec((tm, cols), lambda i: (rev_nt - 1 - i, 0))


def _head_sl(h):
    return slice(HEAD_DIM * h, HEAD_DIM * (h + 1))


L_ROW = HEAD_DIM
L_COL = HEAD_DIM + 3
L_LSE = HEAD_DIM + 6


def _split3(x):
    hi = x.astype(BF16).astype(F32)
    mid = (x - hi).astype(BF16).astype(F32)
    lo = (x - hi - mid).astype(BF16).astype(F32)
    return hi, mid, lo


def _lanes(rows, width, parts):
    lane = lax.broadcasted_iota(jnp.int32, (rows, width), 1)
    out = jnp.zeros((rows, width), F32)
    for at, val in parts.items():
        out = jnp.where(lane == at, val, out)
    return out


def _fwd_in(x, g1, w_in_p, fb):
    T = x.shape[0]
    tm = min(TM, T)

    def body(x_ref, g_ref, w_ref, fb_ref, place_ref, zuv_ref, qa_ref, ka_ref, va_ref, fl_ref, h1_ref, carry):
        @pl.when(pl.program_id(0) == 0)
        def _():
            carry[...] = jnp.zeros_like(carry)

        r, n = _rms(x_ref[...])
        h = (n * g_ref[...]).astype(BF16)
        h1_ref[...] = h
        z = _dot(h, w_ref[...])
        zuv_ref[...] = z[:, :2 * D_A]
        o = 2 * D_A
        fl = z[:, o + 3 * D_B:] + fb_ref[...]
        fl_ref[...] = fl
        tri = (lax.broadcasted_iota(jnp.int32, (tm, tm), 0) >= lax.broadcasted_iota(jnp.int32, (tm, tm), 1)).astype(F32)
        c = _dot_f32(tri, _log_sigmoid(fl)) + carry[...]
        carry[...] = c[tm - 1:tm, :]
        hi, mid, lo = _split3(c)
        parts = jnp.concatenate([hi.astype(BF16), mid.astype(BF16), lo.astype(BF16)], axis=-1)
        placed = _dot(parts, place_ref[...])
        lane = lax.broadcasted_iota(jnp.int32, (tm, LANES), 1)
        data = lane < HEAD_DIM
        ones_q = ((lane >= L_COL) & (lane < L_COL + 3)).astype(F32)
        ones_k = (((lane >= L_ROW) & (lane < L_ROW + 3)) | ((lane >= L_LSE) & (lane < L_LSE + 3))).astype(F32)
        ones_v = ((lane >= L_ROW) & (lane < L_ROW + 3)).astype(F32)
        for hd in range(N_HEADS):
            def rows_of(first_col):
                tile = z[:, first_col + LANES * (hd // 2):first_col + LANES * (hd // 2 + 1)]
                return tile if hd % 2 == 0 else pltpu.roll(tile, HEAD_DIM, axis=1)

            hs = slice(LANES * hd, LANES * (hd + 1))
            qa_ref[hd] = jnp.where(data, rows_of(o) * (HEAD_DIM ** -0.5), placed[:, hs] + ones_q).astype(BF16)
            key_side = pltpu.roll(placed[:, hs], L_COL - L_ROW, axis=1)
            ka_ref[hd] = jnp.where(data, rows_of(o + D_B), ones_k - key_side).astype(BF16)
            va_ref[hd] = jnp.where(data, rows_of(o + 2 * D_B), ones_v).astype(BF16)

    heads = pl.BlockSpec((N_HEADS, tm, LANES), lambda i: (0, i, 0))
    aug = jax.ShapeDtypeStruct((N_HEADS, T, LANES), BF16)
    j, h = jnp.arange(3 * LANES) // LANES, jnp.arange(3 * LANES) % LANES
    place = ((h[:, None] < N_HEADS)
             & (jnp.arange(N_HEADS * LANES)[None, :] == LANES * h[:, None] + L_ROW + j[:, None])).astype(BF16)
    return pl.pallas_call(
        body, name="fwd_in", grid=(T // tm,),
        in_specs=[_rows(tm, D_MODEL), _full((1, D_MODEL)), _full((D_MODEL, D_IN_PAD), True), _full((1, LANES)),
                  _full((3 * LANES, N_HEADS * LANES))],
        out_specs=[_rows(tm, 2 * D_A), heads, heads, heads, _rows(tm, LANES), _rows(tm, D_MODEL)],
        out_shape=[jax.ShapeDtypeStruct((T, 2 * D_A), F32), aug, aug, aug, jax.ShapeDtypeStruct((T, LANES), F32),
                   jax.ShapeDtypeStruct((T, D_MODEL), BF16)],
        scratch_shapes=[pltpu.VMEM((1, LANES), F32)],
        compiler_params=_cp("arbitrary"),
    )(x, g1, w_in_p, fb, place)


def _segmean(x, avg_ref, parts):
    out, rest = None, x
    for _ in range(parts):
        piece = rest.astype(BF16)
        term = _dot(piece, avg_ref[...])
        out = term if out is None else out + term
        rest = rest - piece.astype(F32)
    return out


def _layer_norm(v, lg, avg_ref):
    d = v - _segmean(v, avg_ref, 3)
    rstd = lax.rsqrt(_segmean(d * d, avg_ref, 2) + EPS)
    vhat = d * rstd
    return vhat, rstd, (vhat * lg).astype(BF16)


def _mix_heads(w_ref, x, row_slice):
    low = lax.broadcasted_iota(jnp.int32, (SG_BLOCK, LANES), 1) < HEAD_DIM
    tiles = []
    for p in range(N_HEADS // 2):
        xt = x[row_slice, LANES * p:LANES * (p + 1)]
        zero = jnp.zeros_like(xt)
        tiles.append(_dot(w_ref[2 * p], jnp.where(low, xt, zero)) + _dot(w_ref[2 * p + 1], jnp.where(low, zero, xt)))
    return jnp.concatenate(tiles, axis=-1)


def _gmlp_fwd(zuv, lg, wm, bfull, avg):
    T = zuv.shape[0]
    tg = min(TM_GMLP, T)
    nb = tg // SG_BLOCK

    def body(zuv_ref, lg_ref, wm_ref, b_ref, avg_ref, oa_ref):
        u = _gelu(zuv_ref[:, :D_A])
        _, _, vn = _layer_norm(_gelu(zuv_ref[:, D_A:]), lg_ref[...], avg_ref)
        for n in range(nb):
            rs = slice(n * SG_BLOCK, (n + 1) * SG_BLOCK)
            oa_ref[rs, :] = (u[rs] * (_mix_heads(wm_ref, vn, rs) + b_ref[...])).astype(BF16)

    return pl.pallas_call(
        body, name="gmlp_fwd", grid=(T // tg,),
        in_specs=[_rows(tg, 2 * D_A), _full((1, D_A)), _full((N_HEADS, SG_BLOCK, SG_BLOCK)), _full((SG_BLOCK, D_A)),
                  _full((D_A, D_A))],
        out_specs=_rows(tg, D_A),
        out_shape=jax.ShapeDtypeStruct((T, D_A), BF16),
        compiler_params=_cp("arbitrary"),
    )(zuv, lg, wm, bfull, avg)


def _gmlp_bwd(zuv, doa, lg, wm, wmt, bfull, maskf, hsel, avg):
    T = zuv.shape[0]
    tg = min(TM_GMLP, T)
    nb = tg // SG_BLOCK
    nt = T // tg

    def body(zuv_ref, doa_ref, lg_ref, wm_ref, wmt_ref, b_ref, mask_ref, hsel_ref, avg_ref,
             dzuv_ref, dwm_ref, dsgb_ref, dlg_ref, dbacc):
        i = pl.program_id(0)

        @pl.when(i == 0)
        def _():
            dwm_ref[...] = jnp.zeros_like(dwm_ref)
            dlg_ref[...] = jnp.zeros_like(dlg_ref)
            dbacc[...] = jnp.zeros_like(dbacc)

        zu, zv = zuv_ref[:, :D_A], zuv_ref[:, D_A:]
        lgv = lg_ref[...]
        vhat, rstd, vn = _layer_norm(_gelu(zv), lgv, avg_ref)
        dmixed = doa_ref[...] * _gelu(zu)
        dmb = dmixed.astype(BF16)
        low = lax.broadcasted_iota(jnp.int32, (SG_BLOCK, LANES), 1) < HEAD_DIM
        mixed, dvn = [], []
        for n in range(nb):
            rs = slice(n * SG_BLOCK, (n + 1) * SG_BLOCK)
            mixed.append(_mix_heads(wm_ref, vn, rs) + b_ref[...])
            dvn.append(_mix_heads(wmt_ref, dmb, rs))
            dbacc[...] += dmixed[rs]
            for p in range(N_HEADS // 2):
                ls = slice(LANES * p, LANES * (p + 1))
                dmt, vnt = dmb[rs, ls], vn[rs, ls]
                zero = jnp.zeros_like(dmt)
                dwm_ref[2 * p] += _dot_nt(jnp.where(low, dmt, zero), vnt) * mask_ref[...]
                dwm_ref[2 * p + 1] += _dot_nt(jnp.where(low, zero, dmt), vnt) * mask_ref[...]
        mixed = jnp.concatenate(mixed, axis=0) if nb > 1 else mixed[0]
        dvn = jnp.concatenate(dvn, axis=0) if nb > 1 else dvn[0]
        dlg_ref[...] += jnp.sum(dvn * vhat, axis=0, keepdims=True)
        dvhat = dvn * lgv
        dv = rstd * (dvhat - _segmean(dvhat, avg_ref, 2) - vhat * _segmean(dvhat * vhat, avg_ref, 2))
        dzuv_ref[:, :D_A] = (doa_ref[...] * mixed * _gelu_grad(zu)).astype(BF16)
        dzuv_ref[:, D_A:] = (dv * _gelu_grad(zv)).astype(BF16)

        @pl.when(i == nt - 1)
        def _():
            dsgb_ref[...] = lax.dot_general(hsel_ref[...], dbacc[...], (((1,), (1,)), ((), ())),
                                            precision=lax.Precision.HIGHEST, preferred_element_type=F32)

    return pl.pallas_call(
        body, name="gmlp_bwd", grid=(nt,),
        in_specs=[_rows(tg, 2 * D_A), _rows(tg, D_A), _full((1, D_A)), _full((N_HEADS, SG_BLOCK, SG_BLOCK)),
                  _full((N_HEADS, SG_BLOCK, SG_BLOCK)), _full((SG_BLOCK, D_A)), _full((SG_BLOCK, SG_BLOCK)),
                  _full((N_HEADS, D_A)), _full((D_A, D_A))],
        out_specs=[_rows(tg, 2 * D_A), _full((N_HEADS, SG_BLOCK, SG_BLOCK)), _full((N_HEADS, SG_BLOCK)), _full((1, D_A))],
        out_shape=[jax.ShapeDtypeStruct((T, 2 * D_A), BF16), jax.ShapeDtypeStruct((N_HEADS, SG_BLOCK, SG_BLOCK), F32),
                   jax.ShapeDtypeStruct((N_HEADS, SG_BLOCK), F32), jax.ShapeDtypeStruct((1, D_A), F32)],
        scratch_shapes=[pltpu.VMEM((SG_BLOCK, D_A), F32)],
        compiler_params=_cp("arbitrary"),
    )(zuv, doa, lg, wm, wmt, bfull, maskf, hsel, avg)


def _fox_fwd(qa, ka, va, side):
    T = qa.shape[1]
    tq = min(TQ, T)
    nq = T // tq
    n = side.n

    def body(qa_ref, ka_ref, va_ref, *rest):
        ins, (o_ref, lse_ref), outs = rest[:n], rest[n:n + 2], rest[n + 2:2 * n + 2]
        acc_ref, *sems = rest[2 * n + 2:]
        i = pl.program_id(1)
        if n:
            @pl.when((pl.program_id(0) == 0) & (i == 0))
            def _():
                side.start(ins, outs, sems)
        row = lax.broadcasted_iota(jnp.int32, (tq, tq), 0)
        col = lax.broadcasted_iota(jnp.int32, (tq, tq), 1)
        qs = [qa_ref[0], qa_ref[1]]

        def tiles(js, carry, diag):
            offs = [pl.multiple_of(j * tq, tq) for j in js]
            logits = [[_dot_nt(qs[hh], ka_ref[hh, pl.ds(off, tq), :]) for hh in range(2)] for off in offs]
            carry = list(carry)
            for off, per_head in zip(offs, logits):
                for hh, s in enumerate(per_head):
                    if diag:
                        s = jnp.where(col <= row, s, NEG)
                    m = carry[hh]
                    m_new = jnp.maximum(m, jnp.max(s, axis=-1, keepdims=True))
                    pr = jnp.exp(s - m_new)
                    acc_ref[hh] = jnp.exp(m - m_new) * acc_ref[hh] + _dot(pr.astype(BF16), va_ref[hh, pl.ds(off, tq), :])
                    carry[hh] = m_new
            return tuple(carry)

        acc_ref[...] = jnp.zeros_like(acc_ref)
        init = (jnp.full((tq, 1), NEG, F32),) * 2
        carry = lax.fori_loop(0, i // FWD_UNROLL, lambda t, cr: tiles([FWD_UNROLL * t + u for u in range(FWD_UNROLL)], cr, False), init)
        carry = lax.fori_loop(i - i % FWD_UNROLL, i, lambda j, cr: tiles([j], cr, False), carry)
        carry = tiles([i], carry, True)
        for hh in range(2):
            m, acc = carry[hh], acc_ref[hh]
            l = acc[:, L_ROW:L_ROW + 1]
            o_ref[:, _head_sl(hh)] = acc[:, :HEAD_DIM] / l
            hi, mid, lo = _split3(-(m + jnp.log(l)))
            lse_ref[hh] = _lanes(tq, LANES, {L_LSE: hi, L_LSE + 1: mid, L_LSE + 2: lo}).astype(BF16)
        if n:
            @pl.when((pl.program_id(0) == N_HEADS // 2 - 1) & (i == nq - 1))
            def _():
                side.finish(ins, outs, sems)

    tile = pl.BlockSpec((2, tq, LANES), lambda p, i: (p, i, 0))
    seq = pl.BlockSpec((2, T, LANES), lambda p, i: (p, 0, 0))
    res = pl.pallas_call(
        body, name="fox_fwd", grid=(N_HEADS // 2, nq),
        in_specs=[tile, seq, seq, *side.specs],
        out_specs=[pl.BlockSpec((tq, LANES), lambda p, i: (i, p)), tile, *side.specs],
        out_shape=[jax.ShapeDtypeStruct((T, D_B), F32), jax.ShapeDtypeStruct((N_HEADS, T, LANES), BF16), *side.out_shape],
        scratch_shapes=[pltpu.VMEM((2, tq, LANES), F32), *side.scratch],
        compiler_params=_cp("arbitrary", "arbitrary"),
    )(qa, ka, va, *side.operands)
    return res[0], res[1], res[2:]


def _fox_bwd(qa, lse, doa, ka, va, side):
    T = qa.shape[1]
    tq = min(TQ, T)
    nq = T // tq
    n = side.n

    def body(qa_ref, lse_ref, doa_ref, ka_ref, va_ref, *rest):
        ins, (dqa_ref, dka_ref, dva_ref), outs = rest[:n], rest[n:n + 3], rest[n + 3:2 * n + 3]
        dv_acc, *sems = rest[2 * n + 3:]
        j = pl.program_id(1)
        if n:
            @pl.when((pl.program_id(0) == 0) & (j == 0))
            def _():
                side.start(ins, outs, sems)

        @pl.when(j == 0)
        def _():
            dqa_ref[...] = jnp.zeros_like(dqa_ref)

        row = lax.broadcasted_iota(jnp.int32, (tq, tq), 0)
        col = lax.broadcasted_iota(jnp.int32, (tq, tq), 1)
        ks = [ka_ref[0], ka_ref[1]]
        vs = [va_ref[0], va_ref[1]]

        def tiles(ids, diag):
            work = []
            for i in ids:
                off = pl.multiple_of(i * tq, tq)
                for hh in range(2):
                    qi = qa_ref[hh, pl.ds(off, tq), :] + lse_ref[hh, pl.ds(off, tq), :]
                    doi = doa_ref[hh, pl.ds(off, tq), :]
                    work.append((off, hh, qi, doi, _dot_nt(ks[hh], qi), _dot_nt(vs[hh], doi)))
            for off, hh, qi, doi, st, dpt in work:
                if diag:
                    st = jnp.where(row <= col, st, NEG)
                pt = jnp.exp(st)
                dv_acc[hh] += _dot(pt.astype(BF16), doi)
                dsb = (pt * dpt).astype(BF16)
                dka_ref[hh] += _dot(dsb, qi)
                dqa_ref[hh, pl.ds(off, tq), :] += _dot_tn(dsb, ks[hh])

        dka_ref[...] = jnp.zeros_like(dka_ref)
        dv_acc[...] = jnp.zeros_like(dv_acc)
        tiles([j], True)
        todo = nq - 1 - j

        @pl.loop(0, todo // BWD_UNROLL)
        def _(t):
            tiles([j + 1 + BWD_UNROLL * t + u for u in range(BWD_UNROLL)], False)

        @pl.loop(nq - todo % BWD_UNROLL, nq)
        def _(i):
            tiles([i], False)

        dva_ref[...] = dv_acc[...].astype(BF16)
        if n:
            @pl.when((pl.program_id(0) == N_HEADS // 2 - 1) & (j == nq - 1))
            def _():
                side.finish(ins, outs, sems)

    tile = pl.BlockSpec((2, tq, LANES), lambda p, j: (p, j, 0))
    seq = pl.BlockSpec((2, T, LANES), lambda p, j: (p, 0, 0))
    res = pl.pallas_call(
        body, name="fox_bwd", grid=(N_HEADS // 2, nq),
        in_specs=[seq, seq, seq, tile, tile, *side.specs],
        out_specs=[seq, tile, tile, *side.specs],
        out_shape=[jax.ShapeDtypeStruct((N_HEADS, T, LANES), F32), jax.ShapeDtypeStruct((N_HEADS, T, LANES), F32),
                   jax.ShapeDtypeStruct((N_HEADS, T, LANES), BF16), *side.out_shape],
        scratch_shapes=[pltpu.VMEM((2, tq, LANES), F32), *side.scratch],
        compiler_params=_cp("arbitrary", "arbitrary"),
    )(qa, lse, doa, ka, va, *side.operands)
    return res[0], res[1], res[2], res[3:]


def _fwd_mid(x, oa, ob, w_out, g2, w_up):
    T = x.shape[0]
    tm = min(TM_MID, T)

    def body(x_ref, oa_ref, ob_ref, wo_ref, g_ref, wu_ref, x2_ref, h2_ref, a_ref):
        oab = jnp.concatenate([oa_ref[...], ob_ref[...].astype(BF16)], axis=-1)
        x2 = x_ref[...] + _dot(oab, wo_ref[...])
        x2_ref[...] = x2
        _, n = _rms(x2)
        h2 = (n * g_ref[...]).astype(BF16)
        h2_ref[...] = h2
        a_ref[...] = _dot(h2, wu_ref[...])

    return pl.pallas_call(
        body, name="fwd_mid", grid=(T // tm,),
        in_specs=[_rows(tm, D_MODEL), _rows(tm, D_A), _rows(tm, D_B), _full((D_MODEL, D_MODEL), True), _full((1, D_MODEL)),
                  _full((D_MODEL, D_FF2), True)],
        out_specs=[_rows(tm, D_MODEL), _rows(tm, D_MODEL), _rows(tm, D_FF2)],
        out_shape=[jax.ShapeDtypeStruct((T, D_MODEL), F32), jax.ShapeDtypeStruct((T, D_MODEL), BF16),
                   jax.ShapeDtypeStruct((T, D_FF2), F32)],
        compiler_params=_cp("arbitrary"),
    )(x, oa, ob, w_out, g2, w_up)


def _row_before(x, prev, k):
    rolled = pltpu.roll(x, k, axis=0)
    row = lax.broadcasted_iota(jnp.int32, (8, x.shape[1]), 0)
    head = rolled[0:8]
    for r in range(k):
        head = jnp.where(row == r, prev[8 - k + r:9 - k + r], head)
    return jnp.concatenate([head, rolled[8:]], axis=0)


def _row_after(x, nxt, k):
    tm = x.shape[0]
    rolled = pltpu.roll(x, tm - k, axis=0)
    row = lax.broadcasted_iota(jnp.int32, (8, x.shape[1]), 0)
    tail = rolled[tm - 8:tm]
    for r in range(k):
        tail = jnp.where(row == 8 - k + r, nxt[r:r + 1], tail)
    return jnp.concatenate([rolled[:tm - 8], tail], axis=0)


def _fwd_ffn(a, x2, wc, bc, w_down, g3, tgt):
    T = x2.shape[0]
    tm = min(TM, T)

    def body(a_ref, x2_ref, wc_ref, bc_ref, wd_ref, g_ref, tgt_ref, ac_ref, yff_ref, dx3_ref, loss_ref, dg3_ref, carry):
        @pl.when(pl.program_id(0) == 0)
        def _():
            carry[...] = jnp.zeros_like(carry)
            loss_ref[...] = jnp.zeros_like(loss_ref)
            dg3_ref[...] = jnp.zeros_like(dg3_ref)

        def conv(cs):
            a0 = a_ref[:, cs]
            prev = carry[:, cs]
            ac = (wc_ref[0:1, cs] * _row_before(a0, prev, 2) + wc_ref[1:2, cs] * _row_before(a0, prev, 1)
                  + wc_ref[2:3, cs] * a0 + bc_ref[:, cs])
            ac_ref[:, cs] = ac.astype(BF16)
            return ac

        x3 = x2_ref[...]
        for ci in range(D_FF // CW):
            gs = slice(ci * CW, (ci + 1) * CW)
            ag = conv(gs)
            av = conv(slice(D_FF + ci * CW, D_FF + (ci + 1) * CW))
            yb = (ag * jax.nn.sigmoid(ag) * av).astype(BF16)
            yff_ref[:, gs] = yb
            x3 = x3 + _dot(yb, wd_ref[gs, :])
        carry[...] = a_ref[tm - 8:tm, :]
        r, n = _rms(x3)
        g = g_ref[...]
        diff = n * g - tgt_ref[...]
        loss_ref[...] += (0.5 / D_MODEL) * jnp.sum(diff * diff)
        dout = diff * (1.0 / D_MODEL)
        dg3_ref[...] += jnp.sum(dout * n, axis=0, keepdims=True)
        dx3_ref[...] = _rms_bwd(dout, n, r, g)

    return pl.pallas_call(
        body, name="fwd_ffn", grid=(T // tm,),
        in_specs=[_rows(tm, D_FF2), _rows(tm, D_MODEL), _full((3, D_FF2)), _full((1, D_FF2)), _full((D_FF, D_MODEL), True),
                  _full((1, D_MODEL)), _rows(tm, D_MODEL)],
        out_specs=[_rows(tm, D_FF2), _rows(tm, D_FF), _rows(tm, D_MODEL), _full((8, LANES)), _full((1, D_MODEL))],
        out_shape=[jax.ShapeDtypeStruct((T, D_FF2), BF16), jax.ShapeDtypeStruct((T, D_FF), BF16),
                   jax.ShapeDtypeStruct((T, D_MODEL), F32), jax.ShapeDtypeStruct((8, LANES), F32),
                   jax.ShapeDtypeStruct((1, D_MODEL), F32)],
        scratch_shapes=[pltpu.VMEM((8, D_FF2), F32)],
        compiler_params=_cp("arbitrary"),
    )(a, x2, wc, bc, w_down, g3, tgt)


def _bwd_ffn(dx3, a, ac, yff, h2, w_down, wc):
    T = dx3.shape[0]
    tm = min(TM, T)
    nt = T // tm
    half = D_FF // 2
    shard_up, shard_down = D_FF2 // N_DEV, D_FF // N_DEV

    def body(dx3_ref, ag_ref, av_ref, acg_ref, acv_ref, yff_ref, h2_ref, wd_ref, wcg_ref, wcv_ref,
             dag_ref, dav_ref, dwcg_ref, dwcv_ref, dbcg_ref, dbcv_ref, dwd_ref, dwu_ref,
             nxt, shifted, acc_down, acc_g, acc_v, stage_up, stage_down, sem):
        c, r = pl.program_id(0), pl.program_id(1)

        @pl.when(r == 0)
        def _():
            for ref in (nxt, dwcg_ref, dwcv_ref, dbcg_ref, dbcv_ref, acc_down, acc_g, acc_v):
                ref[...] = jnp.zeros_like(ref)

        dxb = dx3_ref[...].astype(BF16)
        dy_all = _dot_nt(dxb, wd_ref[...])

        def back(a_ref, w_ref, da_ref, dwc_ref, dbc_ref, nx, cs, dac):
            a0 = a_ref[:, cs]
            shifted[0] = _row_after(dac, nxt[:, nx], 1)
            shifted[1] = _row_after(dac, nxt[:, nx], 2)
            dp1, dp2 = shifted[0], shifted[1]
            dbc_ref[:, cs] += jnp.sum(dac, axis=0, keepdims=True)
            dwc_ref[0:1, cs] += jnp.sum(dp2 * a0, axis=0, keepdims=True)
            dwc_ref[1:2, cs] += jnp.sum(dp1 * a0, axis=0, keepdims=True)
            dwc_ref[2:3, cs] += jnp.sum(dac * a0, axis=0, keepdims=True)
            da_ref[:, cs] = (w_ref[2:3, cs] * dac + w_ref[1:2, cs] * dp1 + w_ref[0:1, cs] * dp2).astype(BF16)
            nxt[:, nx] = dac[0:8]

        for ci in range(half // LANES):
            cs = slice(ci * LANES, (ci + 1) * LANES)
            dy = dy_all[:, cs]
            ag, av = acg_ref[:, cs].astype(F32), acv_ref[:, cs].astype(F32)
            sg = jax.nn.sigmoid(ag)
            back(av_ref, wcv_ref, dav_ref, dwcv_ref, dbcv_ref, slice(half + ci * LANES, half + (ci + 1) * LANES), cs,
                 dy * (ag * sg))
            back(ag_ref, wcg_ref, dag_ref, dwcg_ref, dbcg_ref, cs, cs, dy * av * (sg * (1.0 + ag * (1.0 - sg))))

        acc_down[...] += _dot_tn(yff_ref[...], dxb)
        h2 = h2_ref[...]
        acc_g[...] += _dot_tn(h2, dag_ref[...])
        acc_v[...] += _dot_tn(h2, dav_ref[...])

        @pl.when(r == nt - 1)
        def _():
            for s in range(half // shard_down):
                stage_down[...] = acc_down[s * shard_down:(s + 1) * shard_down, :].astype(BF16)
                out = pltpu.make_async_copy(stage_down, dwd_ref.at[(half // shard_down) * c + s], sem)
                out.start()
                out.wait()
            for acc, first in ((acc_g, 0), (acc_v, N_DEV // 2)):
                for s in range(half // shard_up):
                    stage_up[...] = acc[:, s * shard_up:(s + 1) * shard_up].astype(BF16)
                    out = pltpu.make_async_copy(stage_up, dwu_ref.at[first + (half // shard_up) * c + s], sem)
                    out.start()
                    out.wait()

    def cols(width, second_half):
        return pl.BlockSpec((tm, width), lambda c, r: (nt - 1 - r, c + (2 if second_half else 0)))

    def param(rows, second_half):
        return pl.BlockSpec((rows, half), lambda c, r: (0, c + (2 if second_half else 0)))

    tokens = pl.BlockSpec((tm, D_MODEL), lambda c, r: (nt - 1 - r, 0))
    return pl.pallas_call(
        body, name="bwd_ffn", grid=(2, nt),
        in_specs=[tokens, cols(half, False), cols(half, True), cols(half, False), cols(half, True), cols(half, False), tokens,
                  pl.BlockSpec((half, D_MODEL), lambda c, r: (c, 0), pipeline_mode=pl.Buffered(1)),
                  param(3, False), param(3, True)],
        out_specs=[cols(half, False), cols(half, False), param(3, False), param(3, False), param(1, False), param(1, False),
                   ANY, ANY],
        out_shape=[jax.ShapeDtypeStruct((T, D_FF), BF16), jax.ShapeDtypeStruct((T, D_FF), BF16),
                   jax.ShapeDtypeStruct((3, D_FF), F32), jax.ShapeDtypeStruct((3, D_FF), F32),
                   jax.ShapeDtypeStruct((1, D_FF), F32), jax.ShapeDtypeStruct((1, D_FF), F32),
                   jax.ShapeDtypeStruct((N_DEV, shard_down, D_MODEL), BF16), jax.ShapeDtypeStruct((N_DEV, D_MODEL, shard_up), BF16)],
        scratch_shapes=[pltpu.VMEM((8, D_FF), F32), pltpu.VMEM((2, tm, LANES), F32), pltpu.VMEM((half, D_MODEL), F32),
                        pltpu.VMEM((D_MODEL, half), F32), pltpu.VMEM((D_MODEL, half), F32),
                        pltpu.VMEM((D_MODEL, shard_up), BF16), pltpu.VMEM((shard_down, D_MODEL), BF16),
                        pltpu.SemaphoreType.DMA],
        compiler_params=_cp("arbitrary", "arbitrary"),
    )(dx3, a, a, ac, ac, yff, h2, w_down, wc, wc)


def _bwd_mid(da_g, da_v, w_up, x2, g2, dx3, w_out, ob):
    T = x2.shape[0]
    tm = min(TM, T)

    def body(dag_ref, dav_ref, wu_ref, x2_ref, g_ref, dx3_ref, wo_ref, ob_ref, hsum_ref, place_ref,
             dx2_ref, doa_ref, dob_ref, dg2_ref):
        @pl.when(pl.program_id(0) == 0)
        def _():
            dg2_ref[...] = jnp.zeros_like(dg2_ref)

        dh2 = _dot_nt(dag_ref[...], wu_ref[:, :D_FF]) + _dot_nt(dav_ref[...], wu_ref[:, D_FF:])
        r, n = _rms(x2_ref[...])
        dg2_ref[...] += jnp.sum(dh2 * n, axis=0, keepdims=True)
        dx2 = dx3_ref[...] + _rms_bwd(dh2, n, r, g_ref[...])
        dx2_ref[...] = dx2
        doab = _dot_nt(dx2.astype(BF16), wo_ref[...])
        doa_ref[...] = doab[:, :D_A]
        dob = doab[:, D_A:]
        rest, delta = dob.astype(BF16).astype(F32) * ob_ref[...], None
        for _ in range(3):
            piece = rest.astype(BF16)
            term = _dot(piece, hsum_ref[...])
            delta = term if delta is None else delta + term
            rest = rest - piece.astype(F32)
        hi, mid, lo = _split3(-delta)
        parts = jnp.concatenate([hi.astype(BF16), mid.astype(BF16), lo.astype(BF16)], axis=-1)
        placed = _dot(parts, place_ref[...])
        data = lax.broadcasted_iota(jnp.int32, (tm, LANES), 1) < HEAD_DIM
        for hd in range(N_HEADS):
            tile = dob[:, LANES * (hd // 2):LANES * (hd // 2 + 1)]
            rows = tile if hd % 2 == 0 else pltpu.roll(tile, HEAD_DIM, axis=1)
            dob_ref[hd] = jnp.where(data, rows, placed[:, LANES * hd:LANES * (hd + 1)]).astype(BF16)

    hsum = (jnp.arange(D_B)[:, None] // HEAD_DIM == jnp.arange(LANES)[None, :]).astype(BF16)
    j, h = jnp.arange(3 * LANES) // LANES, jnp.arange(3 * LANES) % LANES
    place = ((h[:, None] < N_HEADS)
             & (jnp.arange(N_HEADS * LANES)[None, :] == LANES * h[:, None] + L_ROW + j[:, None])).astype(BF16)
    return pl.pallas_call(
        body, name="bwd_mid", grid=(T // tm,),
        in_specs=[_rows(tm, D_FF), _rows(tm, D_FF), _full((D_MODEL, D_FF2), True), _rows(tm, D_MODEL), _full((1, D_MODEL)),
                  _rows(tm, D_MODEL), _full((D_MODEL, D_MODEL), True), _rows(tm, D_B), _full((D_B, LANES)),
                  _full((3 * LANES, N_HEADS * LANES))],
        out_specs=[_rows(tm, D_MODEL), _rows(tm, D_A), pl.BlockSpec((N_HEADS, tm, LANES), lambda i: (0, i, 0)),
                   _full((1, D_MODEL))],
        out_shape=[jax.ShapeDtypeStruct((T, D_MODEL), F32), jax.ShapeDtypeStruct((T, D_A), F32),
                   jax.ShapeDtypeStruct((N_HEADS, T, LANES), BF16), jax.ShapeDtypeStruct((1, D_MODEL), F32)],
        compiler_params=_cp("arbitrary"),
    )(da_g, da_v, w_up, x2, g2, dx3, w_out, ob, hsum, place)


def _bwd_in(dzuv, dqa, dka, dva, fl, x, dx2, w_in_p, g1):
    T = x.shape[0]
    tm = min(TM, T)
    nt = T // tm

    def body(dzuv_ref, dqa_ref, dka_ref, dva_ref, fl_ref, x_ref, dx2_ref, w_ref, g_ref,
             gx_ref, dz_ref, dg1_ref, dfb_ref, carry):
        @pl.when(pl.program_id(0) == 0)
        def _():
            carry[...] = jnp.zeros_like(carry)
            dg1_ref[...] = jnp.zeros_like(dg1_ref)
            dfb_ref[...] = jnp.zeros_like(dfb_ref)

        dc = _lanes(tm, LANES, {hd: dqa_ref[hd][:, L_ROW:L_ROW + 1] - dka_ref[hd][:, L_COL:L_COL + 1] for hd in range(N_HEADS)})
        later = (lax.broadcasted_iota(jnp.int32, (tm, tm), 1) >= lax.broadcasted_iota(jnp.int32, (tm, tm), 0)).astype(F32)
        dls = _dot_f32(later, dc) + carry[...]
        carry[...] = dls[0:1, :]
        dzf = dls * jax.nn.sigmoid(-fl_ref[...])
        dfb_ref[...] += jnp.sum(dzf, axis=0, keepdims=True)
        data = lax.broadcasted_iota(jnp.int32, (tm, LANES), 1) < HEAD_DIM

        def compact(ref, scale=None):
            def rows(hd):
                return (ref[hd] if scale is None else ref[hd] * scale).astype(BF16)

            return [jnp.where(data, rows(2 * p), pltpu.roll(rows(2 * p + 1), HEAD_DIM, axis=1)) for p in range(N_HEADS // 2)]

        dz = jnp.concatenate([dzuv_ref[...], *compact(dqa_ref, HEAD_DIM ** -0.5), *compact(dka_ref), *compact(dva_ref),
                              dzf.astype(BF16)], axis=-1)
        dz_ref[...] = dz
        dh1 = _dot_nt(dz, w_ref[...])
        r, n = _rms(x_ref[...])
        dg1_ref[...] += jnp.sum(dh1 * n, axis=0, keepdims=True)
        gx_ref[...] = dx2_ref[...] + _rms_bwd(dh1, n, r, g_ref[...])

    rv = functools.partial(_rows, tm, rev_nt=nt)
    heads = pl.BlockSpec((N_HEADS, tm, LANES), lambda i: (0, nt - 1 - i, 0))
    return pl.pallas_call(
        body, name="bwd_in", grid=(nt,),
        in_specs=[rv(2 * D_A), heads, heads, heads, rv(LANES), rv(D_MODEL), rv(D_MODEL),
                  _full((D_MODEL, D_IN_PAD), True), _full((1, D_MODEL))],
        out_specs=[rv(D_MODEL), rv(D_IN_PAD), _full((1, D_MODEL)), _full((1, LANES))],
        out_shape=[jax.ShapeDtypeStruct((T, D_MODEL), F32), jax.ShapeDtypeStruct((T, D_IN_PAD), BF16),
                   jax.ShapeDtypeStruct((1, D_MODEL), F32), jax.ShapeDtypeStruct((1, LANES), F32)],
        scratch_shapes=[pltpu.VMEM((1, LANES), F32)],
        compiler_params=_cp("arbitrary"),
    )(dzuv, dqa, dka, dva, fl, x, dx2, w_in_p, g1)


def _matmul_tn(a_parts, b, tmm, tn, tk, name, shard_cols=None, n_valid=None):
    T = b.shape[0]
    widths = [a.shape[1] for a in a_parts]
    M, N = sum(widths), b.shape[1]
    tk = min(tk, T)
    nk = T // tk
    part_w = tmm // len(a_parts)
    n_valid = N if n_valid is None else n_valid

    def body(*refs):
        a_refs, b_ref, o_ref, obf_ref = refs[:len(a_parts)], refs[-3], refs[-2], refs[-1]
        k = pl.program_id(2)

        @pl.when(k == 0)
        def _():
            o_ref[...] = jnp.zeros_like(o_ref)

        a = [r[...].astype(BF16) for r in a_refs]
        o_ref[...] += _dot_tn(a[0] if len(a) == 1 else jnp.concatenate(a, axis=-1), b_ref[...].astype(BF16))

        @pl.when(k == nk - 1)
        def _():
            if shard_cols is None:
                obf_ref[...] = o_ref[...].astype(BF16)
            else:
                for d in range(min(tn, n_valid) // shard_cols):
                    obf_ref[d] = o_ref[:, d * shard_cols:(d + 1) * shard_cols].astype(BF16)

    if shard_cols is None:
        bf_spec, bf_shape = pl.BlockSpec((tmm, tn), lambda i, j, k: (i, j)), (M, N)
    else:
        per_tile = min(tn, n_valid) // shard_cols
        bf_spec, bf_shape = pl.BlockSpec((per_tile, tmm, shard_cols), lambda i, j, k: (j, i, 0)), (N_DEV, M, shard_cols)
    a_specs = [pl.BlockSpec((tk, part_w), lambda i, j, k: (k, i)) for _ in a_parts]
    return pl.pallas_call(
        body, name=name, grid=(M // tmm, N // tn, nk),
        in_specs=[*a_specs, pl.BlockSpec((tk, tn), lambda i, j, k: (k, j))],
        out_specs=[pl.BlockSpec((tmm, tn), lambda i, j, k: (i, j)), bf_spec],
        out_shape=[jax.ShapeDtypeStruct((M, N), F32), jax.ShapeDtypeStruct(bf_shape, BF16)],
        compiler_params=_cp("arbitrary", "arbitrary", "arbitrary"),
    )(*a_parts, b)


class _Exchange:
    def __init__(self, gather, scatter, relay):
        self.n_g, self.n, self.relay = len(gather), len(gather) + len(scatter), relay
        self.operands = [*gather, *scatter]
        self.out_shape = [jax.ShapeDtypeStruct((N_DEV, *g.shape), g.dtype) for g in gather]
        self.out_shape += [jax.ShapeDtypeStruct(s.shape, s.dtype) for s in scatter]
        self.specs = [ANY] * self.n
        n = self.n
        self.scratch = [pltpu.SemaphoreType.DMA((7 * n,)), pltpu.SemaphoreType.DMA((7 * n,)),
                        pltpu.SemaphoreType.DMA((n,))] if n else []

    def _plan(self, ins, outs, sems):
        send_sems, recv_sems, local_sems = sems
        x, y, c = (lax.axis_index(ax) for ax in MESH_AXES)
        me = 4 * x + 2 * y + c
        sibling = (x, y, 1 - c)
        chips = [(1 - x, y), (x, 1 - y), (1 - x, 1 - y)]
        peers = [sibling] + [(*chip, c) for chip in chips] + [(*chip, 1 - c) for chip in chips]

        def index(dev):
            return 4 * dev[0] + 2 * dev[1] + dev[2]

        def remote(k, src, dst, to):
            return pltpu.make_async_remote_copy(src_ref=src, dst_ref=dst, send_sem=send_sems.at[k], recv_sem=recv_sems.at[k],
                                                device_id=to, device_id_type=pl.DeviceIdType.MESH)

        local, sends, relays, recvs = [], [], [], []
        for a in range(self.n):
            src, out, base = ins[a], outs[a], 7 * a
            if a >= self.n_g:
                local.append(pltpu.make_async_copy(src.at[me], out.at[me], local_sems.at[a]))
                sends += [remote(base + k, src.at[index(peer)], out.at[me], peer) for k, peer in enumerate(peers)]
            else:
                local.append(pltpu.make_async_copy(src, out.at[me], local_sems.at[a]))
                sends += [remote(base + k, src, out.at[me], peer) for k, peer in enumerate(peers[:4 if self.relay else 7])]
            for k, peer in enumerate(peers):
                slot = out.at[index(peer)]
                if a < self.n_g and self.relay and k >= 4:
                    continue
                recv = remote(base + k, slot, slot, peer)
                if a < self.n_g and self.relay and k >= 1:
                    relays.append((recv, remote(base + 3 + k, slot, slot, sibling)))
                else:
                    recvs.append(recv)
            if a < self.n_g and self.relay:
                for j, chip in enumerate(chips):
                    slot = out.at[index((*chip, 1 - c))]
                    recvs.append(remote(base + 4 + j, slot, slot, sibling))
        return local, sends, relays, recvs

    def start(self, ins, outs, sems):
        local, sends, _, _ = self._plan(ins, outs, sems)
        for cp in local + sends:
            cp.start()

    def finish(self, ins, outs, sems):
        local, sends, relays, recvs = self._plan(ins, outs, sems)
        for recv, fwd in relays:
            recv.wait_recv()
            fwd.start()
        for recv in recvs:
            recv.wait_recv()
        for cp in sends + [fwd for _, fwd in relays]:
            cp.wait_send()
        for cp in local:
            cp.wait()


def _exchange(gather, scatter, name):
    ex = _Exchange(gather, scatter, relay=True)
    n = ex.n

    def body(*refs):
        ins, outs, sems = refs[:n], refs[n:2 * n], refs[2 * n:]
        ex.start(ins, outs, sems)
        ex.finish(ins, outs, sems)

    return pl.pallas_call(body, name=name, in_specs=ex.specs, out_specs=ex.specs, out_shape=ex.out_shape,
                          scratch_shapes=ex.scratch)(*ex.operands)


def _adamw(w, g, m, v):
    m = ADAM_B1 * m + (1.0 - ADAM_B1) * g
    v = ADAM_B2 * v + (1.0 - ADAM_B2) * jnp.square(g)
    m_hat = m / (1.0 - ADAM_B1 ** ADAM_STEP)
    v_hat = v / (1.0 - ADAM_B2 ** ADAM_STEP)
    delta = -ADAM_LR * (m_hat / (jnp.sqrt(v_hat) + ADAM_EPS) + ADAM_WD * w)
    return delta, m, v


def _adamw_shard(w, m, v, recv, tr, name):
    _, R, C = w.shape

    def body(w_ref, m_ref, v_ref, recv_ref, g_ref, d_ref, nm_ref, nv_ref):
        g = recv_ref[0].astype(F32)
        for d in range(1, N_DEV):
            g = g + recv_ref[d].astype(F32)
        g_ref[...] = g
        d_ref[...], nm_ref[...], nv_ref[...] = _adamw(w_ref[...], g, m_ref[...], v_ref[...])

    blk = pl.BlockSpec((None, tr, C), lambda i: (0, i, 0))
    return pl.pallas_call(
        body, name=name, grid=(R // tr,),
        in_specs=[blk, blk, blk, pl.BlockSpec((N_DEV, tr, C), lambda i: (0, i, 0))],
        out_specs=[blk] * 4, out_shape=[jax.ShapeDtypeStruct((1, R, C), F32)] * 4,
        compiler_params=_cp("arbitrary"),
    )(w, m, v, recv)


def _adamw_small(params, gathered, loss_parts):
    n = len(params)

    def body(*refs):
        ins, gs, loss_ref, outs = refs[:3 * n], refs[3 * n:4 * n], refs[4 * n], refs[4 * n + 1:]
        for p in range(n):
            w_ref, m_ref, v_ref = ins[3 * p:3 * p + 3]
            g = gs[p][0]
            for d in range(1, N_DEV):
                g = g + gs[p][d]
            g = g[..., :w_ref.shape[-1]]
            g_ref, d_ref, nm_ref, nv_ref = outs[4 * p:4 * p + 4]
            g_ref[...] = g
            d_ref[...], nm_ref[...], nv_ref[...] = _adamw(w_ref[...], g, m_ref[...], v_ref[...])
        total = loss_ref[0]
        for d in range(1, N_DEV):
            total = total + loss_ref[d]
        outs[4 * n][...] = total

    out_shape = [jax.ShapeDtypeStruct(w.shape, F32) for w, _, _ in params for _ in range(4)]
    res = pl.pallas_call(body, name="adamw_small", out_shape=[*out_shape, jax.ShapeDtypeStruct((8, LANES), F32)],
                         compiler_params=pltpu.CompilerParams(vmem_limit_bytes=VMEM_LIMIT))(
        *[t for p in params for t in p], *gathered, loss_parts)
    return [res[4 * p:4 * p + 4] for p in range(n)], res[4 * n][0, 0]


def _col_shards(g):
    return jnp.transpose(g.reshape(g.shape[0], N_DEV, -1), (1, 0, 2))


def _row_shards(g):
    return g.reshape(N_DEV, -1, g.shape[1])


def _cols_whole(g):
    return jnp.transpose(g, (1, 0, 2)).reshape(g.shape[1], -1)


def _join_cols(g, width, name):
    n_shards, rows, c = g.shape
    tr = min(256, rows)

    def body(g_ref, o_ref):
        for d in range(n_shards):
            o_ref[:, d * c:(d + 1) * c] = g_ref[d]
        if width > n_shards * c:
            o_ref[:, n_shards * c:] = jnp.zeros((tr, width - n_shards * c), o_ref.dtype)

    return pl.pallas_call(
        body, name=name, grid=(rows // tr,),
        in_specs=[pl.BlockSpec((n_shards, tr, c), lambda i: (0, i, 0))],
        out_specs=pl.BlockSpec((tr, width), lambda i: (i, 0)),
        out_shape=jax.ShapeDtypeStruct((rows, width), g.dtype),
        compiler_params=_cp("arbitrary"),
    )(g)


def kernel(x, norm_mix_g, w_in, f_bias, sg_ln_g, sg_w, sg_b, w_out, norm_ffn_g, w_up, w_conv, b_conv, w_down, norm_final_g, loss_target, m_norm_mix_g, m_w_in, m_f_bias, m_sg_ln_g, m_sg_w, m_sg_b, m_w_out, m_norm_ffn_g, m_w_up, m_w_conv, m_b_conv, m_w_down, m_norm_final_g, v_norm_mix_g, v_w_in, v_f_bias, v_sg_ln_g, v_sg_w, v_sg_b, v_w_out, v_norm_ffn_g, v_w_up, v_w_conv, v_b_conv, v_w_down, v_norm_final_g):
    xs, tgt = x[0], loss_target[0]
    g1, g2, g3 = norm_mix_g, norm_ffn_g, norm_final_g.reshape(1, D_MODEL)
    lg = sg_ln_g.reshape(1, D_A)
    fb = jnp.pad(f_bias, ((0, 0), (0, LANES - N_HEADS)))
    pos_chunk = jnp.arange(SG_BLOCK) // SG_CHUNK
    maskf = (pos_chunk[:, None] >= pos_chunk[None, :]).astype(F32)
    wm = (sg_w[0] * maskf[None]).astype(BF16)
    wmt = jnp.swapaxes(wm, 1, 2)
    bfull = jnp.repeat(sg_b[0].T, HEAD_DIM, axis=1)
    hsel = jnp.repeat(jnp.eye(N_HEADS, dtype=F32), HEAD_DIM, axis=1)
    avg = (jnp.repeat(hsel, HEAD_DIM, axis=0) * (1.0 / HEAD_DIM)).astype(BF16)

    (win_g,) = _exchange([w_in[0].astype(BF16)], [], "gather_w_in")
    w_in_p = _join_cols(win_g, D_IN_PAD, "join_w_in")
    zuv, qa, ka, va, fl, h1 = _fwd_in(xs, g1, w_in_p, fb)
    oa = _gmlp_fwd(zuv, lg, wm, bfull, avg)
    rest = _Exchange([w_out[0].astype(BF16), w_up[0].astype(BF16), w_down[0].astype(BF16), w_conv[0]], [], relay=False)
    ob, lse, (wout_g, wup_g, wdown_g, wc_g) = _fox_fwd(qa, ka, va, rest)
    w_out_f, w_up_f = wout_g.reshape(D_MODEL, D_MODEL), _join_cols(wup_g, D_FF2, "join_w_up")
    w_down_f, wc_f = wdown_g.reshape(D_FF, D_MODEL), _cols_whole(wc_g)

    x2, h2, a = _fwd_mid(xs, oa, ob, w_out_f, g2, w_up_f)
    ac, yff, dx3, loss, dg3 = _fwd_ffn(a, x2, wc_f, b_conv, w_down_f, g3, tgt)
    da_g, da_v, dwc_g, dwc_v, dbc_g, dbc_v, dwdown_bf, dwup_bf = _bwd_ffn(dx3, a, ac, yff, h2, w_down_f, wc_f)
    dwc, dbc = jnp.concatenate([dwc_g, dwc_v], axis=1), jnp.concatenate([dbc_g, dbc_v], axis=1)
    dx2, doa, dob, dg2 = _bwd_mid(da_g, da_v, w_up_f, x2, g2, dx3, w_out_f, ob)
    dzuv, dwm, dsgb, dlg = _gmlp_bwd(zuv, doa, lg, wm, wmt, bfull, maskf, hsel, avg)
    _, dwout_bf = _matmul_tn([oa, ob], dx2, D_MODEL, D_MODEL, TK_DW, "dw_out")

    early = ("w_out", "w_up", "wc", "w_down")
    wire = [_row_shards(dwout_bf), dwup_bf, _col_shards(dwc).astype(BF16), dwdown_bf]
    small_early = dict(lg=dlg, sg_w=dwm, sg_b=dsgb, g2=dg2, bc=dbc, g3=dg3)
    grads = _Exchange([*small_early.values(), loss], wire, relay=False)
    dqa, dka, dva, got = _fox_bwd(qa, lse, dob, ka, va, grads)
    n_small = len(small_early)
    gathered, loss_parts = dict(zip(small_early, got[:n_small])), got[n_small]
    recv = dict(zip(early, got[n_small + 1:]))

    gx, dz, dg1, dfb = _bwd_in(dzuv, dqa, dka, dva, fl, xs, dx2, w_in_p, g1)
    _, dwin_bf = _matmul_tn([h1], dz, D_MODEL // 2, D_IN_PAD, TK_DW, "dw_in", shard_cols=D_IN // N_DEV, n_valid=D_IN)
    gathered["g1"], gathered["fb"], recv["w_in"] = _exchange([dg1, dfb], [dwin_bf], "exchange_w_in")

    weights = dict(w_in=(w_in, m_w_in, v_w_in, 256), w_out=(w_out, m_w_out, v_w_out, 128), w_up=(w_up, m_w_up, v_w_up, 256),
                   wc=(w_conv, m_w_conv, v_w_conv, 3), w_down=(w_down, m_w_down, v_w_down, 176))
    res = {n: _adamw_shard(w, m, v, recv[n], tr, "adamw_" + n) for n, (w, m, v, tr) in weights.items()}

    reps = dict(g1=((norm_mix_g, m_norm_mix_g, v_norm_mix_g), (1, D_MODEL)), fb=((f_bias, m_f_bias, v_f_bias), (1, N_HEADS)),
                lg=((sg_ln_g, m_sg_ln_g, v_sg_ln_g), (1, D_A)), sg_w=((sg_w, m_sg_w, v_sg_w), (N_HEADS, SG_BLOCK, SG_BLOCK)),
                sg_b=((sg_b, m_sg_b, v_sg_b), (N_HEADS, SG_BLOCK)), g2=((norm_ffn_g, m_norm_ffn_g, v_norm_ffn_g), (1, D_MODEL)),
                bc=((b_conv, m_b_conv, v_b_conv), (1, D_FF2)), g3=((norm_final_g, m_norm_final_g, v_norm_final_g), (1, D_MODEL)))
    outs, loss_sum = _adamw_small([tuple(t.reshape(shape) for t in wmv) for wmv, shape in reps.values()],
                                  [gathered[n] for n in reps], loss_parts)
    for (n, (wmv, _)), out in zip(reps.items(), outs):
        res[n] = [o.reshape(wmv[0].shape) for o in out]

    names = ("g1", "w_in", "fb", "lg", "sg_w", "sg_b", "w_out", "g2", "w_up", "wc", "bc", "w_down", "g3")
    return (loss_sum, gx[None], *[res[n][0] for n in names], *[res[n][1] for n in names],
            *[res[n][2] for n in names], *[res[n][3] for n in names])
```

```python
import functools
import math

import jax
import jax.numpy as jnp
import numpy as np
from jax import lax
from jax.experimental import pallas as pl
from jax.experimental.pallas import tpu as pltpu

F32 = jnp.float32
BF16 = jnp.bfloat16

D_MODEL = 1024
HEAD_DIM = 64
N_HEADS = 8
D_A = 512
D_B = 512
D_IN = 2 * D_A + 3 * D_B + N_HEADS
D_IN_PAD = 2688
D_FF = 2816
D_FF2 = 2 * D_FF
SG_BLOCK = 128
SG_CHUNK = 64
EPS = 1e-6
N_DEV = 8
LANES = 128
NEG = -1e30
VMEM_LIMIT = 56 * 1024 * 1024

ADAM_LR = 0.001
ADAM_B1 = 0.9
ADAM_B2 = 0.999
ADAM_EPS = 1e-08
ADAM_WD = 0.01
ADAM_STEP = 10

TM = 256
TM_MID = 512
TM_GMLP = 1024
TK_DW = 2048
TQ = 512
FWD_UNROLL = 4
BWD_UNROLL = 2
CW = 256

MESH_AXES = ("x", "y", "c")
ANY = pl.BlockSpec(memory_space=pl.ANY)


def _cp(*sem):
    return pltpu.CompilerParams(dimension_semantics=sem, vmem_limit_bytes=VMEM_LIMIT)


def _dot(a, b):
    return jnp.dot(a, b, preferred_element_type=F32)


def _dot_nt(a, b):
    return lax.dot_general(a, b, (((1,), (1,)), ((), ())), preferred_element_type=F32)


def _dot_tn(a, b):
    return lax.dot_general(a, b, (((0,), (0,)), ((), ())), preferred_element_type=F32)


def _dot_f32(a, b):
    return jnp.dot(a, b, precision=lax.Precision.HIGHEST, preferred_element_type=F32)


def _gelu(z):
    return 0.5 * z * (1.0 + lax.erf(z * (1.0 / math.sqrt(2.0))))


def _gelu_grad(z):
    return 0.5 * (1.0 + lax.erf(z * (1.0 / math.sqrt(2.0)))) + z * jnp.exp(-0.5 * z * z) * (1.0 / math.sqrt(2.0 * math.pi))


def _log_sigmoid(x):
    return jnp.minimum(x, 0.0) - jnp.log1p(jnp.exp(-jnp.abs(x)))


def _rms(x):
    r = lax.rsqrt(jnp.mean(x * x, axis=-1, keepdims=True) + EPS)
    return r, x * r


def _rms_bwd(dy, n, r, g):
    dn = dy * g
    return r * (dn - n * jnp.mean(dn * n, axis=-1, keepdims=True))


def _full(shape, single=False):
    nd = len(shape)
    if single:
        return pl.BlockSpec(shape, lambda *_: (0,) * nd, pipeline_mode=pl.Buffered(1))
    return pl.BlockSpec(shape, lambda *_: (0,) * nd)


def _rows(tm, cols, rev_nt=None):
    if rev_nt is None:
        return pl.BlockSpec((tm, cols), lambda i: (i, 0))
    return pl.BlockSpec((tm, cols), lambda i: (rev_nt - 1 - i, 0))


def _head_sl(h):
    return slice(HEAD_DIM * h, HEAD_DIM * (h + 1))


L_ROW = HEAD_DIM
L_COL = HEAD_DIM + 3
L_LSE = HEAD_DIM + 6


def _split3(x):
    hi = x.astype(BF16).astype(F32)
    mid = (x - hi).astype(BF16).astype(F32)
    lo = (x - hi - mid).astype(BF16).astype(F32)
    return hi, mid, lo


def _bias_lane_placement():
    j, h = np.arange(3 * LANES) // LANES, np.arange(3 * LANES) % LANES
    cols = np.arange(N_HEADS * LANES)[None, :]
    return jnp.asarray((h[:, None] < N_HEADS) & (cols == LANES * h[:, None] + L_ROW + j[:, None]), BF16)


def _lanes(rows, width, parts):
    lane = lax.broadcasted_iota(jnp.int32, (rows, width), 1)
    out = jnp.zeros((rows, width), F32)
    for at, val in parts.items():
        out = jnp.where(lane == at, val, out)
    return out


def _fwd_in(x, g1, w_in_p, fb):
    T = x.shape[0]
    tm = min(TM, T)

    def body(x_ref, g_ref, w_ref, fb_ref, place_ref, zuv_ref, qa_ref, ka_ref, va_ref, fl_ref, h1_ref, carry):
        @pl.when(pl.program_id(0) == 0)
        def _():
            carry[...] = jnp.zeros_like(carry)

        r, n = _rms(x_ref[...])
        h = (n * g_ref[...]).astype(BF16)
        h1_ref[...] = h
        z = _dot(h, w_ref[...])
        zuv_ref[...] = z[:, :2 * D_A]
        o = 2 * D_A
        fl = z[:, o + 3 * D_B:] + fb_ref[...]
        fl_ref[...] = fl
        tri = (lax.broadcasted_iota(jnp.int32, (tm, tm), 0) >= lax.broadcasted_iota(jnp.int32, (tm, tm), 1)).astype(F32)
        c = _dot_f32(tri, _log_sigmoid(fl)) + carry[...]
        carry[...] = c[tm - 1:tm, :]
        hi, mid, lo = _split3(c)
        parts = jnp.concatenate([hi.astype(BF16), mid.astype(BF16), lo.astype(BF16)], axis=-1)
        placed = _dot(parts, place_ref[...])
        lane = lax.broadcasted_iota(jnp.int32, (tm, LANES), 1)
        data = lane < HEAD_DIM
        ones_q = ((lane >= L_COL) & (lane < L_COL + 3)).astype(F32)
        ones_k = (((lane >= L_ROW) & (lane < L_ROW + 3)) | ((lane >= L_LSE) & (lane < L_LSE + 3))).astype(F32)
        ones_v = ((lane >= L_ROW) & (lane < L_ROW + 3)).astype(F32)
        for hd in range(N_HEADS):
            def rows_of(first_col):
                tile = z[:, first_col + LANES * (hd // 2):first_col + LANES * (hd // 2 + 1)]
                return tile if hd % 2 == 0 else pltpu.roll(tile, HEAD_DIM, axis=1)

            hs = slice(LANES * hd, LANES * (hd + 1))
            qa_ref[hd] = jnp.where(data, rows_of(o) * (HEAD_DIM ** -0.5), placed[:, hs] + ones_q).astype(BF16)
            key_side = pltpu.roll(placed[:, hs], L_COL - L_ROW, axis=1)
            ka_ref[hd] = jnp.where(data, rows_of(o + D_B), ones_k - key_side).astype(BF16)
            va_ref[hd] = jnp.where(data, rows_of(o + 2 * D_B), ones_v).astype(BF16)

    heads = pl.BlockSpec((N_HEADS, tm, LANES), lambda i: (0, i, 0))
    aug = jax.ShapeDtypeStruct((N_HEADS, T, LANES), BF16)
    place = _bias_lane_placement()
    return pl.pallas_call(
        body, name="fwd_in", grid=(T // tm,),
        in_specs=[_rows(tm, D_MODEL), _full((1, D_MODEL)), _full((D_MODEL, D_IN_PAD), True), _full((1, LANES)),
                  _full((3 * LANES, N_HEADS * LANES))],
        out_specs=[_rows(tm, 2 * D_A), heads, heads, heads, _rows(tm, LANES), _rows(tm, D_MODEL)],
        out_shape=[jax.ShapeDtypeStruct((T, 2 * D_A), F32), aug, aug, aug, jax.ShapeDtypeStruct((T, LANES), F32),
                   jax.ShapeDtypeStruct((T, D_MODEL), BF16)],
        scratch_shapes=[pltpu.VMEM((1, LANES), F32)],
        compiler_params=_cp("arbitrary"),
    )(x, g1, w_in_p, fb, place)


def _segmean(x, avg_ref, parts):
    out, rest = None, x
    for _ in range(parts):
        piece = rest.astype(BF16)
        term = _dot(piece, avg_ref[...])
        out = term if out is None else out + term
        rest = rest - piece.astype(F32)
    return out


def _layer_norm(v, lg, avg_ref):
    d = v - _segmean(v, avg_ref, 3)
    rstd = lax.rsqrt(_segmean(d * d, avg_ref, 2) + EPS)
    vhat = d * rstd
    return vhat, rstd, (vhat * lg).astype(BF16)


def _mix_heads(w_ref, x, row_slice):
    low = lax.broadcasted_iota(jnp.int32, (SG_BLOCK, LANES), 1) < HEAD_DIM
    tiles = []
    for p in range(N_HEADS // 2):
        xt = x[row_slice, LANES * p:LANES * (p + 1)]
        zero = jnp.zeros_like(xt)
        tiles.append(_dot(w_ref[2 * p], jnp.where(low, xt, zero)) + _dot(w_ref[2 * p + 1], jnp.where(low, zero, xt)))
    return jnp.concatenate(tiles, axis=-1)


def _gmlp_fwd(zuv, lg, wm, bfull, avg):
    T = zuv.shape[0]
    tg = min(TM_GMLP, T)
    nb = tg // SG_BLOCK

    def body(zuv_ref, lg_ref, wm_ref, b_ref, avg_ref, oa_ref):
        u = _gelu(zuv_ref[:, :D_A])
        _, _, vn = _layer_norm(_gelu(zuv_ref[:, D_A:]), lg_ref[...], avg_ref)
        for n in range(nb):
            rs = slice(n * SG_BLOCK, (n + 1) * SG_BLOCK)
            oa_ref[rs, :] = (u[rs] * (_mix_heads(wm_ref, vn, rs) + b_ref[...])).astype(BF16)

    return pl.pallas_call(
        body, name="gmlp_fwd", grid=(T // tg,),
        in_specs=[_rows(tg, 2 * D_A), _full((1, D_A)), _full((N_HEADS, SG_BLOCK, SG_BLOCK)), _full((SG_BLOCK, D_A)),
                  _full((D_A, D_A))],
        out_specs=_rows(tg, D_A),
        out_shape=jax.ShapeDtypeStruct((T, D_A), BF16),
        compiler_params=_cp("arbitrary"),
    )(zuv, lg, wm, bfull, avg)


def _gmlp_bwd(zuv, doa, lg, wm, wmt, bfull, maskf, hsel, avg):
    T = zuv.shape[0]
    tg = min(TM_GMLP, T)
    nb = tg // SG_BLOCK
    nt = T // tg

    def body(zuv_ref, doa_ref, lg_ref, wm_ref, wmt_ref, b_ref, mask_ref, hsel_ref, avg_ref,
             dzuv_ref, dwm_ref, dsgb_ref, dlg_ref, dbacc):
        i = pl.program_id(0)

        @pl.when(i == 0)
        def _():
            dwm_ref[...] = jnp.zeros_like(dwm_ref)
            dlg_ref[...] = jnp.zeros_like(dlg_ref)
            dbacc[...] = jnp.zeros_like(dbacc)

        zu, zv = zuv_ref[:, :D_A], zuv_ref[:, D_A:]
        lgv = lg_ref[...]
        vhat, rstd, vn = _layer_norm(_gelu(zv), lgv, avg_ref)
        dmixed = doa_ref[...] * _gelu(zu)
        dmb = dmixed.astype(BF16)
        low = lax.broadcasted_iota(jnp.int32, (SG_BLOCK, LANES), 1) < HEAD_DIM
        mixed, dvn = [], []
        for n in range(nb):
            rs = slice(n * SG_BLOCK, (n + 1) * SG_BLOCK)
            mixed.append(_mix_heads(wm_ref, vn, rs) + b_ref[...])
            dvn.append(_mix_heads(wmt_ref, dmb, rs))
            dbacc[...] += dmixed[rs]
            for p in range(N_HEADS // 2):
                ls = slice(LANES * p, LANES * (p + 1))
                dmt, vnt = dmb[rs, ls], vn[rs, ls]
                zero = jnp.zeros_like(dmt)
                dwm_ref[2 * p] += _dot_nt(jnp.where(low, dmt, zero), vnt) * mask_ref[...]
                dwm_ref[2 * p + 1] += _dot_nt(jnp.where(low, zero, dmt), vnt) * mask_ref[...]
        mixed = jnp.concatenate(mixed, axis=0) if nb > 1 else mixed[0]
        dvn = jnp.concatenate(dvn, axis=0) if nb > 1 else dvn[0]
        dlg_ref[...] += jnp.sum(dvn * vhat, axis=0, keepdims=True)
        dvhat = dvn * lgv
        dv = rstd * (dvhat - _segmean(dvhat, avg_ref, 2) - vhat * _segmean(dvhat * vhat, avg_ref, 2))
        dzuv_ref[:, :D_A] = (doa_ref[...] * mixed * _gelu_grad(zu)).astype(BF16)
        dzuv_ref[:, D_A:] = (dv * _gelu_grad(zv)).astype(BF16)

        @pl.when(i == nt - 1)
        def _():
            dsgb_ref[...] = lax.dot_general(hsel_ref[...], dbacc[...], (((1,), (1,)), ((), ())),
                                            precision=lax.Precision.HIGHEST, preferred_element_type=F32)

    return pl.pallas_call(
        body, name="gmlp_bwd", grid=(nt,),
        in_specs=[_rows(tg, 2 * D_A), _rows(tg, D_A), _full((1, D_A)), _full((N_HEADS, SG_BLOCK, SG_BLOCK)),
                  _full((N_HEADS, SG_BLOCK, SG_BLOCK)), _full((SG_BLOCK, D_A)), _full((SG_BLOCK, SG_BLOCK)),
                  _full((N_HEADS, D_A)), _full((D_A, D_A))],
        out_specs=[_rows(tg, 2 * D_A), _full((N_HEADS, SG_BLOCK, SG_BLOCK)), _full((N_HEADS, SG_BLOCK)), _full((1, D_A))],
        out_shape=[jax.ShapeDtypeStruct((T, 2 * D_A), BF16), jax.ShapeDtypeStruct((N_HEADS, SG_BLOCK, SG_BLOCK), F32),
                   jax.ShapeDtypeStruct((N_HEADS, SG_BLOCK), F32), jax.ShapeDtypeStruct((1, D_A), F32)],
        scratch_shapes=[pltpu.VMEM((SG_BLOCK, D_A), F32)],
        compiler_params=_cp("arbitrary"),
    )(zuv, doa, lg, wm, wmt, bfull, maskf, hsel, avg)


def _fox_fwd(qa, ka, va, side):
    T = qa.shape[1]
    tq = min(TQ, T)
    nq = T // tq
    n = side.n

    def body(qa_ref, ka_ref, va_ref, *rest):
        ins, (o_ref, lse_ref), outs = rest[:n], rest[n:n + 2], rest[n + 2:2 * n + 2]
        acc_ref, *sems = rest[2 * n + 2:]
        i = pl.program_id(1)
        if n:
            @pl.when((pl.program_id(0) == 0) & (i == 0))
            def _():
                side.start(ins, outs, sems)
        row = lax.broadcasted_iota(jnp.int32, (tq, tq), 0)
        col = lax.broadcasted_iota(jnp.int32, (tq, tq), 1)
        qs = [qa_ref[0], qa_ref[1]]

        def tiles(js, carry, diag):
            offs = [pl.multiple_of(j * tq, tq) for j in js]
            logits = [[_dot_nt(qs[hh], ka_ref[hh, pl.ds(off, tq), :]) for hh in range(2)] for off in offs]
            carry = list(carry)
            for off, per_head in zip(offs, logits):
                for hh, s in enumerate(per_head):
                    if diag:
                        s = jnp.where(col <= row, s, NEG)
                    m = carry[hh]
                    m_new = jnp.maximum(m, jnp.max(s, axis=-1, keepdims=True))
                    pr = jnp.exp(s - m_new)
                    acc_ref[hh] = jnp.exp(m - m_new) * acc_ref[hh] + _dot(pr.astype(BF16), va_ref[hh, pl.ds(off, tq), :])
                    carry[hh] = m_new
            return tuple(carry)

        acc_ref[...] = jnp.zeros_like(acc_ref)
        init = (jnp.full((tq, 1), NEG, F32),) * 2
        carry = lax.fori_loop(0, i // FWD_UNROLL, lambda t, cr: tiles([FWD_UNROLL * t + u for u in range(FWD_UNROLL)], cr, False), init)
        carry = lax.fori_loop(i - i % FWD_UNROLL, i, lambda j, cr: tiles([j], cr, False), carry)
        carry = tiles([i], carry, True)
        for hh in range(2):
            m, acc = carry[hh], acc_ref[hh]
            l = acc[:, L_ROW:L_ROW + 1]
            o_ref[:, _head_sl(hh)] = acc[:, :HEAD_DIM] / l
            hi, mid, lo = _split3(-(m + jnp.log(l)))
            lse_ref[hh] = _lanes(tq, LANES, {L_LSE: hi, L_LSE + 1: mid, L_LSE + 2: lo}).astype(BF16)
        if n:
            @pl.when((pl.program_id(0) == N_HEADS // 2 - 1) & (i == nq - 1))
            def _():
                side.finish(ins, outs, sems)

    tile = pl.BlockSpec((2, tq, LANES), lambda p, i: (p, i, 0))
    seq = pl.BlockSpec((2, T, LANES), lambda p, i: (p, 0, 0))
    res = pl.pallas_call(
        body, name="fox_fwd", grid=(N_HEADS // 2, nq),
        in_specs=[tile, seq, seq, *side.specs],
        out_specs=[pl.BlockSpec((tq, LANES), lambda p, i: (i, p)), tile, *side.specs],
        out_shape=[jax.ShapeDtypeStruct((T, D_B), F32), jax.ShapeDtypeStruct((N_HEADS, T, LANES), BF16), *side.out_shape],
        scratch_shapes=[pltpu.VMEM((2, tq, LANES), F32), *side.scratch],
        compiler_params=_cp("arbitrary", "arbitrary"),
    )(qa, ka, va, *side.operands)
    return res[0], res[1], res[2:]


def _fox_bwd(qa, lse, doa, ka, va, side):
    T = qa.shape[1]
    tq = min(TQ, T)
    nq = T // tq
    n = side.n

    def body(qa_ref, lse_ref, doa_ref, ka_ref, va_ref, *rest):
        ins, (dqa_ref, dka_ref, dva_ref), outs = rest[:n], rest[n:n + 3], rest[n + 3:2 * n + 3]
        dv_acc, *sems = rest[2 * n + 3:]
        j = pl.program_id(1)
        if n:
            @pl.when((pl.program_id(0) == 0) & (j == 0))
            def _():
                side.start(ins, outs, sems)

        @pl.when(j == 0)
        def _():
            dqa_ref[...] = jnp.zeros_like(dqa_ref)

        row = lax.broadcasted_iota(jnp.int32, (tq, tq), 0)
        col = lax.broadcasted_iota(jnp.int32, (tq, tq), 1)
        ks = [ka_ref[0], ka_ref[1]]
        vs = [va_ref[0], va_ref[1]]

        def tiles(ids, diag):
            work = []
            for i in ids:
                off = pl.multiple_of(i * tq, tq)
                for hh in range(2):
                    qi = qa_ref[hh, pl.ds(off, tq), :] + lse_ref[hh, pl.ds(off, tq), :]
                    doi = doa_ref[hh, pl.ds(off, tq), :]
                    work.append((off, hh, qi, doi, _dot_nt(ks[hh], qi), _dot_nt(vs[hh], doi)))
            for off, hh, qi, doi, st, dpt in work:
                if diag:
                    st = jnp.where(row <= col, st, NEG)
                pt = jnp.exp(st)
                dv_acc[hh] += _dot(pt.astype(BF16), doi)
                dsb = (pt * dpt).astype(BF16)
                dka_ref[hh] += _dot(dsb, qi)
                dqa_ref[hh, pl.ds(off, tq), :] += _dot_tn(dsb, ks[hh])

        dka_ref[...] = jnp.zeros_like(dka_ref)
        dv_acc[...] = jnp.zeros_like(dv_acc)
        tiles([j], True)
        todo = nq - 1 - j

        @pl.loop(0, todo // BWD_UNROLL)
        def _(t):
            tiles([j + 1 + BWD_UNROLL * t + u for u in range(BWD_UNROLL)], False)

        @pl.loop(nq - todo % BWD_UNROLL, nq)
        def _(i):
            tiles([i], False)

        dva_ref[...] = dv_acc[...].astype(BF16)
        if n:
            @pl.when((pl.program_id(0) == N_HEADS // 2 - 1) & (j == nq - 1))
            def _():
                side.finish(ins, outs, sems)

    tile = pl.BlockSpec((2, tq, LANES), lambda p, j: (p, j, 0))
    seq = pl.BlockSpec((2, T, LANES), lambda p, j: (p, 0, 0))
    res = pl.pallas_call(
        body, name="fox_bwd", grid=(N_HEADS // 2, nq),
        in_specs=[seq, seq, seq, tile, tile, *side.specs],
        out_specs=[seq, tile, tile, *side.specs],
        out_shape=[jax.ShapeDtypeStruct((N_HEADS, T, LANES), F32), jax.ShapeDtypeStruct((N_HEADS, T, LANES), F32),
                   jax.ShapeDtypeStruct((N_HEADS, T, LANES), BF16), *side.out_shape],
        scratch_shapes=[pltpu.VMEM((2, tq, LANES), F32), *side.scratch],
        compiler_params=_cp("arbitrary", "arbitrary"),
    )(qa, lse, doa, ka, va, *side.operands)
    return res[0], res[1], res[2], res[3:]


def _fwd_mid(x, oa, ob, w_out, g2, w_up):
    T = x.shape[0]
    tm = min(TM_MID, T)

    def body(x_ref, oa_ref, ob_ref, wo_ref, g_ref, wu_ref, x2_ref, h2_ref, a_ref):
        oab = jnp.concatenate([oa_ref[...], ob_ref[...].astype(BF16)], axis=-1)
        x2 = x_ref[...] + _dot(oab, wo_ref[...])
        x2_ref[...] = x2
        _, n = _rms(x2)
        h2 = (n * g_ref[...]).astype(BF16)
        h2_ref[...] = h2
        a_ref[...] = _dot(h2, wu_ref[...])

    return pl.pallas_call(
        body, name="fwd_mid", grid=(T // tm,),
        in_specs=[_rows(tm, D_MODEL), _rows(tm, D_A), _rows(tm, D_B), _full((D_MODEL, D_MODEL), True), _full((1, D_MODEL)),
                  _full((D_MODEL, D_FF2), True)],
        out_specs=[_rows(tm, D_MODEL), _rows(tm, D_MODEL), _rows(tm, D_FF2)],
        out_shape=[jax.ShapeDtypeStruct((T, D_MODEL), F32), jax.ShapeDtypeStruct((T, D_MODEL), BF16),
                   jax.ShapeDtypeStruct((T, D_FF2), F32)],
        compiler_params=_cp("arbitrary"),
    )(x, oa, ob, w_out, g2, w_up)


def _row_before(x, prev, k):
    rolled = pltpu.roll(x, k, axis=0)
    row = lax.broadcasted_iota(jnp.int32, (8, x.shape[1]), 0)
    head = rolled[0:8]
    for r in range(k):
        head = jnp.where(row == r, prev[8 - k + r:9 - k + r], head)
    return jnp.concatenate([head, rolled[8:]], axis=0)


def _row_after(x, nxt, k):
    tm = x.shape[0]
    rolled = pltpu.roll(x, tm - k, axis=0)
    row = lax.broadcasted_iota(jnp.int32, (8, x.shape[1]), 0)
    tail = rolled[tm - 8:tm]
    for r in range(k):
        tail = jnp.where(row == 8 - k + r, nxt[r:r + 1], tail)
    return jnp.concatenate([rolled[:tm - 8], tail], axis=0)


def _fwd_ffn(a, x2, wc, bc, w_down, g3, tgt):
    T = x2.shape[0]
    tm = min(TM, T)

    def body(a_ref, x2_ref, wc_ref, bc_ref, wd_ref, g_ref, tgt_ref, ac_ref, yff_ref, dx3_ref, loss_ref, dg3_ref, carry):
        @pl.when(pl.program_id(0) == 0)
        def _():
            carry[...] = jnp.zeros_like(carry)
            loss_ref[...] = jnp.zeros_like(loss_ref)
            dg3_ref[...] = jnp.zeros_like(dg3_ref)

        def conv(cs):
            a0 = a_ref[:, cs]
            prev = carry[:, cs]
            ac = (wc_ref[0:1, cs] * _row_before(a0, prev, 2) + wc_ref[1:2, cs] * _row_before(a0, prev, 1)
                  + wc_ref[2:3, cs] * a0 + bc_ref[:, cs])
            ac_ref[:, cs] = ac.astype(BF16)
            return ac

        x3 = x2_ref[...]
        for ci in range(D_FF // CW):
            gs = slice(ci * CW, (ci + 1) * CW)
            ag = conv(gs)
            av = conv(slice(D_FF + ci * CW, D_FF + (ci + 1) * CW))
            yb = (ag * jax.nn.sigmoid(ag) * av).astype(BF16)
            yff_ref[:, gs] = yb
            x3 = x3 + _dot(yb, wd_ref[gs, :])
        carry[...] = a_ref[tm - 8:tm, :]
        r, n = _rms(x3)
        g = g_ref[...]
        diff = n * g - tgt_ref[...]
        loss_ref[...] += (0.5 / D_MODEL) * jnp.sum(diff * diff)
        dout = diff * (1.0 / D_MODEL)
        dg3_ref[...] += jnp.sum(dout * n, axis=0, keepdims=True)
        dx3_ref[...] = _rms_bwd(dout, n, r, g)

    return pl.pallas_call(
        body, name="fwd_ffn", grid=(T // tm,),
        in_specs=[_rows(tm, D_FF2), _rows(tm, D_MODEL), _full((3, D_FF2)), _full((1, D_FF2)), _full((D_FF, D_MODEL), True),
                  _full((1, D_MODEL)), _rows(tm, D_MODEL)],
        out_specs=[_rows(tm, D_FF2), _rows(tm, D_FF), _rows(tm, D_MODEL), _full((8, LANES)), _full((1, D_MODEL))],
        out_shape=[jax.ShapeDtypeStruct((T, D_FF2), BF16), jax.ShapeDtypeStruct((T, D_FF), BF16),
                   jax.ShapeDtypeStruct((T, D_MODEL), F32), jax.ShapeDtypeStruct((8, LANES), F32),
                   jax.ShapeDtypeStruct((1, D_MODEL), F32)],
        scratch_shapes=[pltpu.VMEM((8, D_FF2), F32)],
        compiler_params=_cp("arbitrary"),
    )(a, x2, wc, bc, w_down, g3, tgt)


def _bwd_ffn(dx3, a, ac, yff, h2, w_down, wc):
    T = dx3.shape[0]
    tm = min(TM, T)
    nt = T // tm
    half = D_FF // 2
    shard_up, shard_down = D_FF2 // N_DEV, D_FF // N_DEV

    def body(dx3_ref, ag_ref, av_ref, acg_ref, acv_ref, yff_ref, h2_ref, wd_ref, wcg_ref, wcv_ref,
             dag_ref, dav_ref, dwcg_ref, dwcv_ref, dbcg_ref, dbcv_ref, dwd_ref, dwu_ref,
             nxt, shifted, acc_down, acc_g, acc_v, stage_up, stage_down, sem):
        c, r = pl.program_id(0), pl.program_id(1)

        @pl.when(r == 0)
        def _():
            for ref in (nxt, dwcg_ref, dwcv_ref, dbcg_ref, dbcv_ref, acc_down, acc_g, acc_v):
                ref[...] = jnp.zeros_like(ref)

        dxb = dx3_ref[...].astype(BF16)
        dy_all = _dot_nt(dxb, wd_ref[...])

        def back(a_ref, w_ref, da_ref, dwc_ref, dbc_ref, nx, cs, dac):
            a0 = a_ref[:, cs]
            shifted[0] = _row_after(dac, nxt[:, nx], 1)
            shifted[1] = _row_after(dac, nxt[:, nx], 2)
            dp1, dp2 = shifted[0], shifted[1]
            dbc_ref[:, cs] += jnp.sum(dac, axis=0, keepdims=True)
            dwc_ref[0:1, cs] += jnp.sum(dp2 * a0, axis=0, keepdims=True)
            dwc_ref[1:2, cs] += jnp.sum(dp1 * a0, axis=0, keepdims=True)
            dwc_ref[2:3, cs] += jnp.sum(dac * a0, axis=0, keepdims=True)
            da_ref[:, cs] = (w_ref[2:3, cs] * dac + w_ref[1:2, cs] * dp1 + w_ref[0:1, cs] * dp2).astype(BF16)
            nxt[:, nx] = dac[0:8]

        for ci in range(half // LANES):
            cs = slice(ci * LANES, (ci + 1) * LANES)
            dy = dy_all[:, cs]
            ag, av = acg_ref[:, cs].astype(F32), acv_ref[:, cs].astype(F32)
            sg = jax.nn.sigmoid(ag)
            back(av_ref, wcv_ref, dav_ref, dwcv_ref, dbcv_ref, slice(half + ci * LANES, half + (ci + 1) * LANES), cs,
                 dy * (ag * sg))
            back(ag_ref, wcg_ref, dag_ref, dwcg_ref, dbcg_ref, cs, cs, dy * av * (sg * (1.0 + ag * (1.0 - sg))))

        acc_down[...] += _dot_tn(yff_ref[...], dxb)
        h2 = h2_ref[...]
        acc_g[...] += _dot_tn(h2, dag_ref[...])
        acc_v[...] += _dot_tn(h2, dav_ref[...])

        @pl.when(r == nt - 1)
        def _():
            for s in range(half // shard_down):
                stage_down[...] = acc_down[s * shard_down:(s + 1) * shard_down, :].astype(BF16)
                out = pltpu.make_async_copy(stage_down, dwd_ref.at[(half // shard_down) * c + s], sem)
                out.start()
                out.wait()
            for acc, first in ((acc_g, 0), (acc_v, N_DEV // 2)):
                for s in range(half // shard_up):
                    stage_up[...] = acc[:, s * shard_up:(s + 1) * shard_up].astype(BF16)
                    out = pltpu.make_async_copy(stage_up, dwu_ref.at[first + (half // shard_up) * c + s], sem)
                    out.start()
                    out.wait()

    def cols(width, second_half):
        return pl.BlockSpec((tm, width), lambda c, r: (nt - 1 - r, c + (2 if second_half else 0)))

    def param(rows, second_half):
        return pl.BlockSpec((rows, half), lambda c, r: (0, c + (2 if second_half else 0)))

    tokens = pl.BlockSpec((tm, D_MODEL), lambda c, r: (nt - 1 - r, 0))
    return pl.pallas_call(
        body, name="bwd_ffn", grid=(2, nt),
        in_specs=[tokens, cols(half, False), cols(half, True), cols(half, False), cols(half, True), cols(half, False), tokens,
                  pl.BlockSpec((half, D_MODEL), lambda c, r: (c, 0), pipeline_mode=pl.Buffered(1)),
                  param(3, False), param(3, True)],
        out_specs=[cols(half, False), cols(half, False), param(3, False), param(3, False), param(1, False), param(1, False),
                   ANY, ANY],
        out_shape=[jax.ShapeDtypeStruct((T, D_FF), BF16), jax.ShapeDtypeStruct((T, D_FF), BF16),
                   jax.ShapeDtypeStruct((3, D_FF), F32), jax.ShapeDtypeStruct((3, D_FF), F32),
                   jax.ShapeDtypeStruct((1, D_FF), F32), jax.ShapeDtypeStruct((1, D_FF), F32),
                   jax.ShapeDtypeStruct((N_DEV, shard_down, D_MODEL), BF16), jax.ShapeDtypeStruct((N_DEV, D_MODEL, shard_up), BF16)],
        scratch_shapes=[pltpu.VMEM((8, D_FF), F32), pltpu.VMEM((2, tm, LANES), F32), pltpu.VMEM((half, D_MODEL), F32),
                        pltpu.VMEM((D_MODEL, half), F32), pltpu.VMEM((D_MODEL, half), F32),
                        pltpu.VMEM((D_MODEL, shard_up), BF16), pltpu.VMEM((shard_down, D_MODEL), BF16),
                        pltpu.SemaphoreType.DMA],
        compiler_params=_cp("arbitrary", "arbitrary"),
    )(dx3, a, a, ac, ac, yff, h2, w_down, wc, wc)


def _bwd_mid(da_g, da_v, w_up, x2, g2, dx3, w_out, ob):
    T = x2.shape[0]
    tm = min(TM, T)

    def body(dag_ref, dav_ref, wu_ref, x2_ref, g_ref, dx3_ref, wo_ref, ob_ref, hsum_ref, place_ref,
             dx2_ref, doa_ref, dob_ref, dg2_ref):
        @pl.when(pl.program_id(0) == 0)
        def _():
            dg2_ref[...] = jnp.zeros_like(dg2_ref)

        dh2 = _dot_nt(dag_ref[...], wu_ref[:, :D_FF]) + _dot_nt(dav_ref[...], wu_ref[:, D_FF:])
        r, n = _rms(x2_ref[...])
        dg2_ref[...] += jnp.sum(dh2 * n, axis=0, keepdims=True)
        dx2 = dx3_ref[...] + _rms_bwd(dh2, n, r, g_ref[...])
        dx2_ref[...] = dx2
        doab = _dot_nt(dx2.astype(BF16), wo_ref[...])
        doa_ref[...] = doab[:, :D_A]
        dob = doab[:, D_A:]
        rest, delta = dob.astype(BF16).astype(F32) * ob_ref[...], None
        for _ in range(3):
            piece = rest.astype(BF16)
            term = _dot(piece, hsum_ref[...])
            delta = term if delta is None else delta + term
            rest = rest - piece.astype(F32)
        hi, mid, lo = _split3(-delta)
        parts = jnp.concatenate([hi.astype(BF16), mid.astype(BF16), lo.astype(BF16)], axis=-1)
        placed = _dot(parts, place_ref[...])
        data = lax.broadcasted_iota(jnp.int32, (tm, LANES), 1) < HEAD_DIM
        for hd in range(N_HEADS):
            tile = dob[:, LANES * (hd // 2):LANES * (hd // 2 + 1)]
            rows = tile if hd % 2 == 0 else pltpu.roll(tile, HEAD_DIM, axis=1)
            dob_ref[hd] = jnp.where(data, rows, placed[:, LANES * hd:LANES * (hd + 1)]).astype(BF16)

    hsum = jnp.asarray(np.arange(D_B)[:, None] // HEAD_DIM == np.arange(LANES)[None, :], BF16)
    place = _bias_lane_placement()
    return pl.pallas_call(
        body, name="bwd_mid", grid=(T // tm,),
        in_specs=[_rows(tm, D_FF), _rows(tm, D_FF), _full((D_MODEL, D_FF2), True), _rows(tm, D_MODEL), _full((1, D_MODEL)),
                  _rows(tm, D_MODEL), _full((D_MODEL, D_MODEL), True), _rows(tm, D_B), _full((D_B, LANES)),
                  _full((3 * LANES, N_HEADS * LANES))],
        out_specs=[_rows(tm, D_MODEL), _rows(tm, D_A), pl.BlockSpec((N_HEADS, tm, LANES), lambda i: (0, i, 0)),
                   _full((1, D_MODEL))],
        out_shape=[jax.ShapeDtypeStruct((T, D_MODEL), F32), jax.ShapeDtypeStruct((T, D_A), F32),
                   jax.ShapeDtypeStruct((N_HEADS, T, LANES), BF16), jax.ShapeDtypeStruct((1, D_MODEL), F32)],
        compiler_params=_cp("arbitrary"),
    )(da_g, da_v, w_up, x2, g2, dx3, w_out, ob, hsum, place)


def _bwd_in(dzuv, dqa, dka, dva, fl, x, dx2, w_in_p, g1):
    T = x.shape[0]
    tm = min(TM, T)
    nt = T // tm

    def body(dzuv_ref, dqa_ref, dka_ref, dva_ref, fl_ref, x_ref, dx2_ref, w_ref, g_ref,
             gx_ref, dz_ref, dg1_ref, dfb_ref, carry):
        @pl.when(pl.program_id(0) == 0)
        def _():
            carry[...] = jnp.zeros_like(carry)
            dg1_ref[...] = jnp.zeros_like(dg1_ref)
            dfb_ref[...] = jnp.zeros_like(dfb_ref)

        dc = _lanes(tm, LANES, {hd: dqa_ref[hd][:, L_ROW:L_ROW + 1] - dka_ref[hd][:, L_COL:L_COL + 1] for hd in range(N_HEADS)})
        later = (lax.broadcasted_iota(jnp.int32, (tm, tm), 1) >= lax.broadcasted_iota(jnp.int32, (tm, tm), 0)).astype(F32)
        dls = _dot_f32(later, dc) + carry[...]
        carry[...] = dls[0:1, :]
        dzf = dls * jax.nn.sigmoid(-fl_ref[...])
        dfb_ref[...] += jnp.sum(dzf, axis=0, keepdims=True)
        data = lax.broadcasted_iota(jnp.int32, (tm, LANES), 1) < HEAD_DIM

        def compact(ref, scale=None):
            def rows(hd):
                return (ref[hd] if scale is None else ref[hd] * scale).astype(BF16)

            return [jnp.where(data, rows(2 * p), pltpu.roll(rows(2 * p + 1), HEAD_DIM, axis=1)) for p in range(N_HEADS // 2)]

        dz = jnp.concatenate([dzuv_ref[...], *compact(dqa_ref, HEAD_DIM ** -0.5), *compact(dka_ref), *compact(dva_ref),
                              dzf.astype(BF16)], axis=-1)
        dz_ref[...] = dz
        dh1 = _dot_nt(dz, w_ref[...])
        r, n = _rms(x_ref[...])
        dg1_ref[...] += jnp.sum(dh1 * n, axis=0, keepdims=True)
        gx_ref[...] = dx2_ref[...] + _rms_bwd(dh1, n, r, g_ref[...])

    rv = functools.partial(_rows, tm, rev_nt=nt)
    heads = pl.BlockSpec((N_HEADS, tm, LANES), lambda i: (0, nt - 1 - i, 0))
    return pl.pallas_call(
        body, name="bwd_in", grid=(nt,),
        in_specs=[rv(2 * D_A), heads, heads, heads, rv(LANES), rv(D_MODEL), rv(D_MODEL),
                  _full((D_MODEL, D_IN_PAD), True), _full((1, D_MODEL))],
        out_specs=[rv(D_MODEL), rv(D_IN_PAD), _full((1, D_MODEL)), _full((1, LANES))],
        out_shape=[jax.ShapeDtypeStruct((T, D_MODEL), F32), jax.ShapeDtypeStruct((T, D_IN_PAD), BF16),
                   jax.ShapeDtypeStruct((1, D_MODEL), F32), jax.ShapeDtypeStruct((1, LANES), F32)],
        scratch_shapes=[pltpu.VMEM((1, LANES), F32)],
        compiler_params=_cp("arbitrary"),
    )(dzuv, dqa, dka, dva, fl, x, dx2, w_in_p, g1)


def _matmul_tn(a_parts, b, tmm, tn, tk, name, shard_cols=None, n_valid=None):
    T = b.shape[0]
    widths = [a.shape[1] for a in a_parts]
    M, N = sum(widths), b.shape[1]
    tk = min(tk, T)
    nk = T // tk
    part_w = tmm // len(a_parts)
    n_valid = N if n_valid is None else n_valid

    def body(*refs):
        a_refs, b_ref, o_ref, obf_ref = refs[:len(a_parts)], refs[-3], refs[-2], refs[-1]
        k = pl.program_id(2)

        @pl.when(k == 0)
        def _():
            o_ref[...] = jnp.zeros_like(o_ref)

        a = [r[...].astype(BF16) for r in a_refs]
        o_ref[...] += _dot_tn(a[0] if len(a) == 1 else jnp.concatenate(a, axis=-1), b_ref[...].astype(BF16))

        @pl.when(k == nk - 1)
        def _():
            if shard_cols is None:
                obf_ref[...] = o_ref[...].astype(BF16)
            else:
                for d in range(min(tn, n_valid) // shard_cols):
                    obf_ref[d] = o_ref[:, d * shard_cols:(d + 1) * shard_cols].astype(BF16)

    if shard_cols is None:
        bf_spec, bf_shape = pl.BlockSpec((tmm, tn), lambda i, j, k: (i, j)), (M, N)
    else:
        per_tile = min(tn, n_valid) // shard_cols
        bf_spec, bf_shape = pl.BlockSpec((per_tile, tmm, shard_cols), lambda i, j, k: (j, i, 0)), (N_DEV, M, shard_cols)
    a_specs = [pl.BlockSpec((tk, part_w), lambda i, j, k: (k, i)) for _ in a_parts]
    return pl.pallas_call(
        body, name=name, grid=(M // tmm, N // tn, nk),
        in_specs=[*a_specs, pl.BlockSpec((tk, tn), lambda i, j, k: (k, j))],
        out_specs=[pl.BlockSpec((tmm, tn), lambda i, j, k: (i, j)), bf_spec],
        out_shape=[jax.ShapeDtypeStruct((M, N), F32), jax.ShapeDtypeStruct(bf_shape, BF16)],
        compiler_params=_cp("arbitrary", "arbitrary", "arbitrary"),
    )(*a_parts, b)


class _Exchange:
    def __init__(self, gather, scatter, relay):
        self.n_g, self.n, self.relay = len(gather), len(gather) + len(scatter), relay
        self.operands = [*gather, *scatter]
        self.out_shape = [jax.ShapeDtypeStruct((N_DEV, *g.shape), g.dtype) for g in gather]
        self.out_shape += [jax.ShapeDtypeStruct(s.shape, s.dtype) for s in scatter]
        self.specs = [ANY] * self.n
        n = self.n
        self.scratch = [pltpu.SemaphoreType.DMA((7 * n,)), pltpu.SemaphoreType.DMA((7 * n,)),
                        pltpu.SemaphoreType.DMA((n,))] if n else []

    def _plan(self, ins, outs, sems):
        send_sems, recv_sems, local_sems = sems
        x, y, c = (lax.axis_index(ax) for ax in MESH_AXES)
        me = 4 * x + 2 * y + c
        sibling = (x, y, 1 - c)
        chips = [(1 - x, y), (x, 1 - y), (1 - x, 1 - y)]
        peers = [sibling] + [(*chip, c) for chip in chips] + [(*chip, 1 - c) for chip in chips]

        def index(dev):
            return 4 * dev[0] + 2 * dev[1] + dev[2]

        def remote(k, src, dst, to):
            return pltpu.make_async_remote_copy(src_ref=src, dst_ref=dst, send_sem=send_sems.at[k], recv_sem=recv_sems.at[k],
                                                device_id=to, device_id_type=pl.DeviceIdType.MESH)

        local, sends, relays, recvs = [], [], [], []
        for a in range(self.n):
            src, out, base = ins[a], outs[a], 7 * a
            if a >= self.n_g:
                local.append(pltpu.make_async_copy(src.at[me], out.at[me], local_sems.at[a]))
                sends += [remote(base + k, src.at[index(peer)], out.at[me], peer) for k, peer in enumerate(peers)]
            else:
                local.append(pltpu.make_async_copy(src, out.at[me], local_sems.at[a]))
                sends += [remote(base + k, src, out.at[me], peer) for k, peer in enumerate(peers[:4 if self.relay else 7])]
            for k, peer in enumerate(peers):
                slot = out.at[index(peer)]
                if a < self.n_g and self.relay and k >= 4:
                    continue
                recv = remote(base + k, slot, slot, peer)
                if a < self.n_g and self.relay and k >= 1:
                    relays.append((recv, remote(base + 3 + k, slot, slot, sibling)))
                else:
                    recvs.append(recv)
            if a < self.n_g and self.relay:
                for j, chip in enumerate(chips):
                    slot = out.at[index((*chip, 1 - c))]
                    recvs.append(remote(base + 4 + j, slot, slot, sibling))
        return local, sends, relays, recvs

    def start(self, ins, outs, sems):
        local, sends, _, _ = self._plan(ins, outs, sems)
        for cp in local + sends:
            cp.start()

    def finish(self, ins, outs, sems):
        local, sends, relays, recvs = self._plan(ins, outs, sems)
        for recv, fwd in relays:
            recv.wait_recv()
            fwd.start()
        for recv in recvs:
            recv.wait_recv()
        for cp in sends + [fwd for _, fwd in relays]:
            cp.wait_send()
        for cp in local:
            cp.wait()


def _exchange(gather, scatter, name):
    ex = _Exchange(gather, scatter, relay=True)
    n = ex.n

    def body(*refs):
        ins, outs, sems = refs[:n], refs[n:2 * n], refs[2 * n:]
        ex.start(ins, outs, sems)
        ex.finish(ins, outs, sems)

    return pl.pallas_call(body, name=name, in_specs=ex.specs, out_specs=ex.specs, out_shape=ex.out_shape,
                          scratch_shapes=ex.scratch)(*ex.operands)


def _adamw(w, g, m, v):
    m = ADAM_B1 * m + (1.0 - ADAM_B1) * g
    v = ADAM_B2 * v + (1.0 - ADAM_B2) * jnp.square(g)
    m_hat = m / (1.0 - ADAM_B1 ** ADAM_STEP)
    v_hat = v / (1.0 - ADAM_B2 ** ADAM_STEP)
    delta = -ADAM_LR * (m_hat / (jnp.sqrt(v_hat) + ADAM_EPS) + ADAM_WD * w)
    return delta, m, v


def _adamw_shard(w, m, v, recv, tr, name):
    _, R, C = w.shape

    def body(w_ref, m_ref, v_ref, recv_ref, g_ref, d_ref, nm_ref, nv_ref):
        g = recv_ref[0].astype(F32)
        for d in range(1, N_DEV):
            g = g + recv_ref[d].astype(F32)
        g_ref[...] = g
        d_ref[...], nm_ref[...], nv_ref[...] = _adamw(w_ref[...], g, m_ref[...], v_ref[...])

    blk = pl.BlockSpec((None, tr, C), lambda i: (0, i, 0))
    return pl.pallas_call(
        body, name=name, grid=(R // tr,),
        in_specs=[blk, blk, blk, pl.BlockSpec((N_DEV, tr, C), lambda i: (0, i, 0))],
        out_specs=[blk] * 4, out_shape=[jax.ShapeDtypeStruct((1, R, C), F32)] * 4,
        compiler_params=_cp("arbitrary"),
    )(w, m, v, recv)


def _adamw_small(params, gathered, loss_parts):
    n = len(params)

    def body(*refs):
        ins, gs, loss_ref, outs = refs[:3 * n], refs[3 * n:4 * n], refs[4 * n], refs[4 * n + 1:]
        for p in range(n):
            w_ref, m_ref, v_ref = ins[3 * p:3 * p + 3]
            g = gs[p][0]
            for d in range(1, N_DEV):
                g = g + gs[p][d]
            g = g[..., :w_ref.shape[-1]]
            g_ref, d_ref, nm_ref, nv_ref = outs[4 * p:4 * p + 4]
            g_ref[...] = g
            d_ref[...], nm_ref[...], nv_ref[...] = _adamw(w_ref[...], g, m_ref[...], v_ref[...])
        total = loss_ref[0]
        for d in range(1, N_DEV):
            total = total + loss_ref[d]
        outs[4 * n][...] = total

    out_shape = [jax.ShapeDtypeStruct(w.shape, F32) for w, _, _ in params for _ in range(4)]
    res = pl.pallas_call(body, name="adamw_small", out_shape=[*out_shape, jax.ShapeDtypeStruct((8, LANES), F32)],
                         compiler_params=pltpu.CompilerParams(vmem_limit_bytes=VMEM_LIMIT))(
        *[t for p in params for t in p], *gathered, loss_parts)
    return [res[4 * p:4 * p + 4] for p in range(n)], res[4 * n][0, 0]


def _col_shards(g):
    return jnp.transpose(g.reshape(g.shape[0], N_DEV, -1), (1, 0, 2))


def _row_shards(g):
    return g.reshape(N_DEV, -1, g.shape[1])


def _cols_whole(g):
    return jnp.transpose(g, (1, 0, 2)).reshape(g.shape[1], -1)


def _join_cols(g, width, name):
    n_shards, rows, c = g.shape
    tr = min(256, rows)

    def body(g_ref, o_ref):
        for d in range(n_shards):
            o_ref[:, d * c:(d + 1) * c] = g_ref[d]
        if width > n_shards * c:
            o_ref[:, n_shards * c:] = jnp.zeros((tr, width - n_shards * c), o_ref.dtype)

    return pl.pallas_call(
        body, name=name, grid=(rows // tr,),
        in_specs=[pl.BlockSpec((n_shards, tr, c), lambda i: (0, i, 0))],
        out_specs=pl.BlockSpec((tr, width), lambda i: (i, 0)),
        out_shape=jax.ShapeDtypeStruct((rows, width), g.dtype),
        compiler_params=_cp("arbitrary"),
    )(g)


def kernel(x, norm_mix_g, w_in, f_bias, sg_ln_g, sg_w, sg_b, w_out, norm_ffn_g, w_up, w_conv, b_conv, w_down, norm_final_g, loss_target, m_norm_mix_g, m_w_in, m_f_bias, m_sg_ln_g, m_sg_w, m_sg_b, m_w_out, m_norm_ffn_g, m_w_up, m_w_conv, m_b_conv, m_w_down, m_norm_final_g, v_norm_mix_g, v_w_in, v_f_bias, v_sg_ln_g, v_sg_w, v_sg_b, v_w_out, v_norm_ffn_g, v_w_up, v_w_conv, v_b_conv, v_w_down, v_norm_final_g):
    xs, tgt = x[0], loss_target[0]
    g1, g2, g3 = norm_mix_g, norm_ffn_g, norm_final_g.reshape(1, D_MODEL)
    lg = sg_ln_g.reshape(1, D_A)
    fb = jnp.pad(f_bias, ((0, 0), (0, LANES - N_HEADS)))
    pos_chunk = np.arange(SG_BLOCK) // SG_CHUNK
    maskf = jnp.asarray(pos_chunk[:, None] >= pos_chunk[None, :], F32)
    wm = (sg_w[0] * maskf[None]).astype(BF16)
    wmt = jnp.swapaxes(wm, 1, 2)
    bfull = jnp.repeat(sg_b[0].T, HEAD_DIM, axis=1)
    same_head = np.arange(D_A)[:, None] // HEAD_DIM == np.arange(D_A)[None, :] // HEAD_DIM
    hsel = jnp.asarray(np.arange(N_HEADS)[:, None] == np.arange(D_A)[None, :] // HEAD_DIM, F32)
    avg = jnp.asarray(same_head * (1.0 / HEAD_DIM), BF16)

    (win_g,) = _exchange([w_in[0].astype(BF16)], [], "gather_w_in")
    w_in_p = _join_cols(win_g, D_IN_PAD, "join_w_in")
    zuv, qa, ka, va, fl, h1 = _fwd_in(xs, g1, w_in_p, fb)
    oa = _gmlp_fwd(zuv, lg, wm, bfull, avg)
    rest = _Exchange([w_out[0].astype(BF16), w_up[0].astype(BF16), w_down[0].astype(BF16), w_conv[0]], [], relay=False)
    ob, lse, (wout_g, wup_g, wdown_g, wc_g) = _fox_fwd(qa, ka, va, rest)
    w_out_f, w_up_f = wout_g.reshape(D_MODEL, D_MODEL), _join_cols(wup_g, D_FF2, "join_w_up")
    w_down_f, wc_f = wdown_g.reshape(D_FF, D_MODEL), _cols_whole(wc_g)

    x2, h2, a = _fwd_mid(xs, oa, ob, w_out_f, g2, w_up_f)
    ac, yff, dx3, loss, dg3 = _fwd_ffn(a, x2, wc_f, b_conv, w_down_f, g3, tgt)
    da_g, da_v, dwc_g, dwc_v, dbc_g, dbc_v, dwdown_bf, dwup_bf = _bwd_ffn(dx3, a, ac, yff, h2, w_down_f, wc_f)
    dwc, dbc = jnp.concatenate([dwc_g, dwc_v], axis=1), jnp.concatenate([dbc_g, dbc_v], axis=1)
    dx2, doa, dob, dg2 = _bwd_mid(da_g, da_v, w_up_f, x2, g2, dx3, w_out_f, ob)
    dzuv, dwm, dsgb, dlg = _gmlp_bwd(zuv, doa, lg, wm, wmt, bfull, maskf, hsel, avg)
    _, dwout_bf = _matmul_tn([oa, ob], dx2, D_MODEL, D_MODEL, TK_DW, "dw_out")

    early = ("w_out", "w_up", "wc", "w_down")
    wire = [_row_shards(dwout_bf), dwup_bf, _col_shards(dwc).astype(BF16), dwdown_bf]
    small_early = dict(lg=dlg, sg_w=dwm, sg_b=dsgb, g2=dg2, bc=dbc, g3=dg3)
    grads = _Exchange([*small_early.values(), loss], wire, relay=False)
    dqa, dka, dva, got = _fox_bwd(qa, lse, dob, ka, va, grads)
    n_small = len(small_early)
    gathered, loss_parts = dict(zip(small_early, got[:n_small])), got[n_small]
    recv = dict(zip(early, got[n_small + 1:]))

    gx, dz, dg1, dfb = _bwd_in(dzuv, dqa, dka, dva, fl, xs, dx2, w_in_p, g1)
    _, dwin_bf = _matmul_tn([h1], dz, D_MODEL // 2, D_IN_PAD, TK_DW, "dw_in", shard_cols=D_IN // N_DEV, n_valid=D_IN)
    gathered["g1"], gathered["fb"], recv["w_in"] = _exchange([dg1, dfb], [dwin_bf], "exchange_w_in")

    weights = dict(w_in=(w_in, m_w_in, v_w_in, 256), w_out=(w_out, m_w_out, v_w_out, 128), w_up=(w_up, m_w_up, v_w_up, 256),
                   wc=(w_conv, m_w_conv, v_w_conv, 3), w_down=(w_down, m_w_down, v_w_down, 176))
    res = {n: _adamw_shard(w, m, v, recv[n], tr, "adamw_" + n) for n, (w, m, v, tr) in weights.items()}

    reps = dict(g1=((norm_mix_g, m_norm_mix_g, v_norm_mix_g), (1, D_MODEL)), fb=((f_bias, m_f_bias, v_f_bias), (1, N_HEADS)),
                lg=((sg_ln_g, m_sg_ln_g, v_sg_ln_g), (1, D_A)), sg_w=((sg_w, m_sg_w, v_sg_w), (N_HEADS, SG_BLOCK, SG_BLOCK)),
                sg_b=((sg_b, m_sg_b, v_sg_b), (N_HEADS, SG_BLOCK)), g2=((norm_ffn_g, m_norm_ffn_g, v_norm_ffn_g), (1, D_MODEL)),
                bc=((b_conv, m_b_conv, v_b_conv), (1, D_FF2)), g3=((norm_final_g, m_norm_final_g, v_norm_final_g), (1, D_MODEL)))
    outs, loss_sum = _adamw_small([tuple(t.reshape(shape) for t in wmv) for wmv, shape in reps.values()],
                                  [gathered[n] for n in reps], loss_parts)
    for (n, (wmv, _)), out in zip(reps.items(), outs):
        res[n] = [o.reshape(wmv[0].shape) for o in out]

    names = ("g1", "w_in", "fb", "lg", "sg_w", "sg_b", "w_out", "g2", "w_up", "wc", "bc", "w_down", "g3")
    return (loss_sum, gx[None], *[res[n][0] for n in names], *[res[n][1] for n in names],
            *[res[n][2] for n in names], *[res[n][3] for n in names])
```

```python
import functools
import math

import jax
import jax.numpy as jnp
import numpy as np
from jax import lax
from jax.experimental import pallas as pl
from jax.experimental.pallas import tpu as pltpu

F32 = jnp.float32
BF16 = jnp.bfloat16

D_MODEL = 1024
HEAD_DIM = 64
N_HEADS = 8
D_A = 512
D_B = 512
D_IN = 2 * D_A + 3 * D_B + N_HEADS
D_IN_PAD = 2688
D_FF = 2816
D_FF2 = 2 * D_FF
SG_BLOCK = 128
SG_CHUNK = 64
EPS = 1e-6
N_DEV = 8
LANES = 128
NEG = -1e30
VMEM_LIMIT = 56 * 1024 * 1024

ADAM_LR = 0.001
ADAM_B1 = 0.9
ADAM_B2 = 0.999
ADAM_EPS = 1e-08
ADAM_WD = 0.01
ADAM_STEP = 10

TM = 256
TM_MID = 512
TM_GMLP = 1024
TK_DW = 2048
TQ = 512
FWD_UNROLL = 4
BWD_UNROLL = 2
CW = 256

MESH_AXES = ("x", "y", "c")
ANY = pl.BlockSpec(memory_space=pl.ANY)


def _cp(*sem):
    return pltpu.CompilerParams(dimension_semantics=sem, vmem_limit_bytes=VMEM_LIMIT)


def _dot(a, b):
    return jnp.dot(a, b, preferred_element_type=F32)


def _dot_nt(a, b):
    return lax.dot_general(a, b, (((1,), (1,)), ((), ())), preferred_element_type=F32)


def _dot_tn(a, b):
    return lax.dot_general(a, b, (((0,), (0,)), ((), ())), preferred_element_type=F32)


def _dot_f32(a, b):
    return jnp.dot(a, b, precision=lax.Precision.HIGHEST, preferred_element_type=F32)


def _gelu(z):
    return 0.5 * z * (1.0 + lax.erf(z * (1.0 / math.sqrt(2.0))))


def _gelu_grad(z):
    return 0.5 * (1.0 + lax.erf(z * (1.0 / math.sqrt(2.0)))) + z * jnp.exp(-0.5 * z * z) * (1.0 / math.sqrt(2.0 * math.pi))


def _log_sigmoid(x):
    return jnp.minimum(x, 0.0) - jnp.log1p(jnp.exp(-jnp.abs(x)))


def _rms(x):
    r = lax.rsqrt(jnp.mean(x * x, axis=-1, keepdims=True) + EPS)
    return r, x * r


def _rms_bwd(dy, n, r, g):
    dn = dy * g
    return r * (dn - n * jnp.mean(dn * n, axis=-1, keepdims=True))


def _full(shape, single=False):
    nd = len(shape)
    if single:
        return pl.BlockSpec(shape, lambda *_: (0,) * nd, pipeline_mode=pl.Buffered(1))
    return pl.BlockSpec(shape, lambda *_: (0,) * nd)


def _rows(tm, cols, rev_nt=None):
    if rev_nt is None:
        return pl.BlockSpec((tm, cols), lambda i: (i, 0))
    return pl.BlockSpec((tm, cols), lambda i: (rev_nt - 1 - i, 0))


def _head_sl(h):
    return slice(HEAD_DIM * h, HEAD_DIM * (h + 1))


L_ROW = HEAD_DIM
L_COL = HEAD_DIM + 3
L_LSE = HEAD_DIM + 6


def _split3(x):
    hi = x.astype(BF16).astype(F32)
    mid = (x - hi).astype(BF16).astype(F32)
    lo = (x - hi - mid).astype(BF16).astype(F32)
    return hi, mid, lo


def _bias_lane_placement():
    j, h = np.arange(3 * LANES) // LANES, np.arange(3 * LANES) % LANES
    cols = np.arange(N_HEADS * LANES)[None, :]
    return jnp.asarray((h[:, None] < N_HEADS) & (cols == LANES * h[:, None] + L_ROW + j[:, None]), BF16)


def _lanes(rows, width, parts):
    lane = lax.broadcasted_iota(jnp.int32, (rows, width), 1)
    out = jnp.zeros((rows, width), F32)
    for at, val in parts.items():
        out = jnp.where(lane == at, val, out)
    return out


def _fwd_in(x, g1, w_in_p, fb):
    T = x.shape[0]
    tm = min(TM, T)

    def body(x_ref, g_ref, w_ref, fb_ref, place_ref, zuv_ref, qa_ref, ka_ref, va_ref, fl_ref, h1_ref, carry):
        @pl.when(pl.program_id(0) == 0)
        def _():
            carry[...] = jnp.zeros_like(carry)

        r, n = _rms(x_ref[...])
        h = (n * g_ref[...]).astype(BF16)
        h1_ref[...] = h
        z = _dot(h, w_ref[...])
        zuv_ref[...] = z[:, :2 * D_A]
        o = 2 * D_A
        fl = z[:, o + 3 * D_B:] + fb_ref[...]
        fl_ref[...] = fl
        tri = (lax.broadcasted_iota(jnp.int32, (tm, tm), 0) >= lax.broadcasted_iota(jnp.int32, (tm, tm), 1)).astype(F32)
        c = _dot_f32(tri, _log_sigmoid(fl)) + carry[...]
        carry[...] = c[tm - 1:tm, :]
        hi, mid, lo = _split3(c)
        parts = jnp.concatenate([hi.astype(BF16), mid.astype(BF16), lo.astype(BF16)], axis=-1)
        placed = _dot(parts, place_ref[...])
        lane = lax.broadcasted_iota(jnp.int32, (tm, LANES), 1)
        data = lane < HEAD_DIM
        ones_q = ((lane >= L_COL) & (lane < L_COL + 3)).astype(F32)
        ones_k = (((lane >= L_ROW) & (lane < L_ROW + 3)) | ((lane >= L_LSE) & (lane < L_LSE + 3))).astype(F32)
        ones_v = ((lane >= L_ROW) & (lane < L_ROW + 3)).astype(F32)
        for hd in range(N_HEADS):
            def rows_of(first_col):
                tile = z[:, first_col + LANES * (hd // 2):first_col + LANES * (hd // 2 + 1)]
                return tile if hd % 2 == 0 else pltpu.roll(tile, HEAD_DIM, axis=1)

            hs = slice(LANES * hd, LANES * (hd + 1))
            qa_ref[hd] = jnp.where(data, rows_of(o) * (HEAD_DIM ** -0.5), placed[:, hs] + ones_q).astype(BF16)
            key_side = pltpu.roll(placed[:, hs], L_COL - L_ROW, axis=1)
            ka_ref[hd] = jnp.where(data, rows_of(o + D_B), ones_k - key_side).astype(BF16)
            va_ref[hd] = jnp.where(data, rows_of(o + 2 * D_B), ones_v).astype(BF16)

    heads = pl.BlockSpec((N_HEADS, tm, LANES), lambda i: (0, i, 0))
    aug = jax.ShapeDtypeStruct((N_HEADS, T, LANES), BF16)
    place = _bias_lane_placement()
    return pl.pallas_call(
        body, name="fwd_in", grid=(T // tm,),
        in_specs=[_rows(tm, D_MODEL), _full((1, D_MODEL)), _full((D_MODEL, D_IN_PAD), True), _full((1, LANES)),
                  _full((3 * LANES, N_HEADS * LANES))],
        out_specs=[_rows(tm, 2 * D_A), heads, heads, heads, _rows(tm, LANES), _rows(tm, D_MODEL)],
        out_shape=[jax.ShapeDtypeStruct((T, 2 * D_A), F32), aug, aug, aug, jax.ShapeDtypeStruct((T, LANES), F32),
                   jax.ShapeDtypeStruct((T, D_MODEL), BF16)],
        scratch_shapes=[pltpu.VMEM((1, LANES), F32)],
        compiler_params=_cp("arbitrary"),
    )(x, g1, w_in_p, fb, place)


def _segmean(x, avg_ref, parts):
    out, rest = None, x
    for _ in range(parts):
        piece = rest.astype(BF16)
        term = _dot(piece, avg_ref[...])
        out = term if out is None else out + term
        rest = rest - piece.astype(F32)
    return out


def _layer_norm(v, lg, avg_ref):
    d = v - _segmean(v, avg_ref, 3)
    rstd = lax.rsqrt(_segmean(d * d, avg_ref, 2) + EPS)
    vhat = d * rstd
    return vhat, rstd, (vhat * lg).astype(BF16)


def _mix_heads(w_ref, x, row_slice):
    low = lax.broadcasted_iota(jnp.int32, (SG_BLOCK, LANES), 1) < HEAD_DIM
    tiles = []
    for p in range(N_HEADS // 2):
        xt = x[row_slice, LANES * p:LANES * (p + 1)]
        zero = jnp.zeros_like(xt)
        tiles.append(_dot(w_ref[2 * p], jnp.where(low, xt, zero)) + _dot(w_ref[2 * p + 1], jnp.where(low, zero, xt)))
    return jnp.concatenate(tiles, axis=-1)


def _gmlp_fwd(zuv, lg, wm, bfull, avg):
    T = zuv.shape[0]
    tg = min(TM_GMLP, T)
    nb = tg // SG_BLOCK

    def body(zuv_ref, lg_ref, wm_ref, b_ref, avg_ref, oa_ref):
        u = _gelu(zuv_ref[:, :D_A])
        _, _, vn = _layer_norm(_gelu(zuv_ref[:, D_A:]), lg_ref[...], avg_ref)
        for n in range(nb):
            rs = slice(n * SG_BLOCK, (n + 1) * SG_BLOCK)
            oa_ref[rs, :] = (u[rs] * (_mix_heads(wm_ref, vn, rs) + b_ref[...])).astype(BF16)

    return pl.pallas_call(
        body, name="gmlp_fwd", grid=(T // tg,),
        in_specs=[_rows(tg, 2 * D_A), _full((1, D_A)), _full((N_HEADS, SG_BLOCK, SG_BLOCK)), _full((SG_BLOCK, D_A)),
                  _full((D_A, D_A))],
        out_specs=_rows(tg, D_A),
        out_shape=jax.ShapeDtypeStruct((T, D_A), BF16),
        compiler_params=_cp("arbitrary"),
    )(zuv, lg, wm, bfull, avg)


def _gmlp_bwd(zuv, doa, lg, wm, wmt, bfull, maskf, hsel, avg):
    T = zuv.shape[0]
    tg = min(TM_GMLP, T)
    nb = tg // SG_BLOCK
    nt = T // tg

    def body(zuv_ref, doa_ref, lg_ref, wm_ref, wmt_ref, b_ref, mask_ref, hsel_ref, avg_ref,
             dzuv_ref, dwm_ref, dsgb_ref, dlg_ref, dbacc):
        i = pl.program_id(0)

        @pl.when(i == 0)
        def _():
            dwm_ref[...] = jnp.zeros_like(dwm_ref)
            dlg_ref[...] = jnp.zeros_like(dlg_ref)
            dbacc[...] = jnp.zeros_like(dbacc)

        zu, zv = zuv_ref[:, :D_A], zuv_ref[:, D_A:]
        lgv = lg_ref[...]
        vhat, rstd, vn = _layer_norm(_gelu(zv), lgv, avg_ref)
        dmixed = doa_ref[...] * _gelu(zu)
        dmb = dmixed.astype(BF16)
        low = lax.broadcasted_iota(jnp.int32, (SG_BLOCK, LANES), 1) < HEAD_DIM
        mixed, dvn = [], []
        for n in range(nb):
            rs = slice(n * SG_BLOCK, (n + 1) * SG_BLOCK)
            mixed.append(_mix_heads(wm_ref, vn, rs) + b_ref[...])
            dvn.append(_mix_heads(wmt_ref, dmb, rs))
            dbacc[...] += dmixed[rs]
            for p in range(N_HEADS // 2):
                ls = slice(LANES * p, LANES * (p + 1))
                dmt, vnt = dmb[rs, ls], vn[rs, ls]
                zero = jnp.zeros_like(dmt)
                dwm_ref[2 * p] += _dot_nt(jnp.where(low, dmt, zero), vnt) * mask_ref[...]
                dwm_ref[2 * p + 1] += _dot_nt(jnp.where(low, zero, dmt), vnt) * mask_ref[...]
        mixed = jnp.concatenate(mixed, axis=0) if nb > 1 else mixed[0]
        dvn = jnp.concatenate(dvn, axis=0) if nb > 1 else dvn[0]
        dlg_ref[...] += jnp.sum(dvn * vhat, axis=0, keepdims=True)
        dvhat = dvn * lgv
        dv = rstd * (dvhat - _segmean(dvhat, avg_ref, 2) - vhat * _segmean(dvhat * vhat, avg_ref, 2))
        dzuv_ref[:, :D_A] = (doa_ref[...] * mixed * _gelu_grad(zu)).astype(BF16)
        dzuv_ref[:, D_A:] = (dv * _gelu_grad(zv)).astype(BF16)

        @pl.when(i == nt - 1)
        def _():
            dsgb_ref[...] = lax.dot_general(hsel_ref[...], dbacc[...], (((1,), (1,)), ((), ())),
                                            precision=lax.Precision.HIGHEST, preferred_element_type=F32)

    return pl.pallas_call(
        body, name="gmlp_bwd", grid=(nt,),
        in_specs=[_rows(tg, 2 * D_A), _rows(tg, D_A), _full((1, D_A)), _full((N_HEADS, SG_BLOCK, SG_BLOCK)),
                  _full((N_HEADS, SG_BLOCK, SG_BLOCK)), _full((SG_BLOCK, D_A)), _full((SG_BLOCK, SG_BLOCK)),
                  _full((N_HEADS, D_A)), _full((D_A, D_A))],
        out_specs=[_rows(tg, 2 * D_A), _full((N_HEADS, SG_BLOCK, SG_BLOCK)), _full((N_HEADS, SG_BLOCK)), _full((1, D_A))],
        out_shape=[jax.ShapeDtypeStruct((T, 2 * D_A), BF16), jax.ShapeDtypeStruct((N_HEADS, SG_BLOCK, SG_BLOCK), F32),
                   jax.ShapeDtypeStruct((N_HEADS, SG_BLOCK), F32), jax.ShapeDtypeStruct((1, D_A), F32)],
        scratch_shapes=[pltpu.VMEM((SG_BLOCK, D_A), F32)],
        compiler_params=_cp("arbitrary"),
    )(zuv, doa, lg, wm, wmt, bfull, maskf, hsel, avg)


def _fox_fwd(qa, ka, va, side):
    T = qa.shape[1]
    tq = min(TQ, T)
    nq = T // tq
    n = side.n

    def body(qa_ref, ka_ref, va_ref, *rest):
        ins, (o_ref, lse_ref), outs = rest[:n], rest[n:n + 2], rest[n + 2:2 * n + 2]
        acc_ref, *sems = rest[2 * n + 2:]
        i = pl.program_id(1)
        if n:
            @pl.when((pl.program_id(0) == 0) & (i == 0))
            def _():
                side.start(ins, outs, sems)
        row = lax.broadcasted_iota(jnp.int32, (tq, tq), 0)
        col = lax.broadcasted_iota(jnp.int32, (tq, tq), 1)
        qs = [qa_ref[0], qa_ref[1]]

        def tiles(js, carry, diag):
            offs = [pl.multiple_of(j * tq, tq) for j in js]
            logits = [[_dot_nt(qs[hh], ka_ref[hh, pl.ds(off, tq), :]) for hh in range(2)] for off in offs]
            carry = list(carry)
            for off, per_head in zip(offs, logits):
                for hh, s in enumerate(per_head):
                    if diag:
                        s = jnp.where(col <= row, s, NEG)
                    m = carry[hh]
                    m_new = jnp.maximum(m, jnp.max(s, axis=-1, keepdims=True))
                    pr = jnp.exp(s - m_new)
                    acc_ref[hh] = jnp.exp(m - m_new) * acc_ref[hh] + _dot(pr.astype(BF16), va_ref[hh, pl.ds(off, tq), :])
                    carry[hh] = m_new
            return tuple(carry)

        acc_ref[...] = jnp.zeros_like(acc_ref)
        init = (jnp.full((tq, 1), NEG, F32),) * 2
        carry = lax.fori_loop(0, i // FWD_UNROLL, lambda t, cr: tiles([FWD_UNROLL * t + u for u in range(FWD_UNROLL)], cr, False), init)
        carry = lax.fori_loop(i - i % FWD_UNROLL, i, lambda j, cr: tiles([j], cr, False), carry)
        carry = tiles([i], carry, True)
        for hh in range(2):
            m, acc = carry[hh], acc_ref[hh]
            l = acc[:, L_ROW:L_ROW + 1]
            o_ref[:, _head_sl(hh)] = acc[:, :HEAD_DIM] / l
            hi, mid, lo = _split3(-(m + jnp.log(l)))
            lse_ref[hh] = _lanes(tq, LANES, {L_LSE: hi, L_LSE + 1: mid, L_LSE + 2: lo}).astype(BF16)
        if n:
            @pl.when((pl.program_id(0) == N_HEADS // 2 - 1) & (i == nq - 1))
            def _():
                side.finish(ins, outs, sems)

    tile = pl.BlockSpec((2, tq, LANES), lambda p, i: (p, i, 0))
    seq = pl.BlockSpec((2, T, LANES), lambda p, i: (p, 0, 0))
    res = pl.pallas_call(
        body, name="fox_fwd", grid=(N_HEADS // 2, nq),
        in_specs=[tile, seq, seq, *side.specs],
        out_specs=[pl.BlockSpec((tq, LANES), lambda p, i: (i, p)), tile, *side.specs],
        out_shape=[jax.ShapeDtypeStruct((T, D_B), F32), jax.ShapeDtypeStruct((N_HEADS, T, LANES), BF16), *side.out_shape],
        scratch_shapes=[pltpu.VMEM((2, tq, LANES), F32), *side.scratch],
        compiler_params=_cp("arbitrary", "arbitrary"),
    )(qa, ka, va, *side.operands)
    return res[0], res[1], res[2:]


def _fox_bwd(qa, lse, doa, ka, va, side):
    T = qa.shape[1]
    tq = min(TQ, T)
    nq = T // tq
    n = side.n

    def body(qa_ref, lse_ref, doa_ref, ka_ref, va_ref, *rest):
        ins, (dqa_ref, dka_ref, dva_ref), outs = rest[:n], rest[n:n + 3], rest[n + 3:2 * n + 3]
        dv_acc, *sems = rest[2 * n + 3:]
        j = pl.program_id(1)
        if n:
            @pl.when((pl.program_id(0) == 0) & (j == 0))
            def _():
                side.start(ins, outs, sems)

        @pl.when(j == 0)
        def _():
            dqa_ref[...] = jnp.zeros_like(dqa_ref)

        row = lax.broadcasted_iota(jnp.int32, (tq, tq), 0)
        col = lax.broadcasted_iota(jnp.int32, (tq, tq), 1)
        ks = [ka_ref[0], ka_ref[1]]
        vs = [va_ref[0], va_ref[1]]

        def tiles(ids, diag):
            work = []
            for i in ids:
                off = pl.multiple_of(i * tq, tq)
                for hh in range(2):
                    qi = qa_ref[hh, pl.ds(off, tq), :] + lse_ref[hh, pl.ds(off, tq), :]
                    doi = doa_ref[hh, pl.ds(off, tq), :]
                    work.append((off, hh, qi, doi, _dot_nt(ks[hh], qi), _dot_nt(vs[hh], doi)))
            for off, hh, qi, doi, st, dpt in work:
                if diag:
                    st = jnp.where(row <= col, st, NEG)
                pt = jnp.exp(st)
                dv_acc[hh] += _dot(pt.astype(BF16), doi)
                dsb = (pt * dpt).astype(BF16)
                dka_ref[hh] += _dot(dsb, qi)
                dqa_ref[hh, pl.ds(off, tq), :] += _dot_tn(dsb, ks[hh])

        dka_ref[...] = jnp.zeros_like(dka_ref)
        dv_acc[...] = jnp.zeros_like(dv_acc)
        tiles([j], True)
        todo = nq - 1 - j

        @pl.loop(0, todo // BWD_UNROLL)
        def _(t):
            tiles([j + 1 + BWD_UNROLL * t + u for u in range(BWD_UNROLL)], False)

        @pl.loop(nq - todo % BWD_UNROLL, nq)
        def _(i):
            tiles([i], False)

        dva_ref[...] = dv_acc[...].astype(BF16)
        if n:
            @pl.when((pl.program_id(0) == N_HEADS // 2 - 1) & (j == nq - 1))
            def _():
                side.finish(ins, outs, sems)

    tile = pl.BlockSpec((2, tq, LANES), lambda p, j: (p, j, 0))
    seq = pl.BlockSpec((2, T, LANES), lambda p, j: (p, 0, 0))
    res = pl.pallas_call(
        body, name="fox_bwd", grid=(N_HEADS // 2, nq),
        in_specs=[seq, seq, seq, tile, tile, *side.specs],
        out_specs=[seq, tile, tile, *side.specs],
        out_shape=[jax.ShapeDtypeStruct((N_HEADS, T, LANES), F32), jax.ShapeDtypeStruct((N_HEADS, T, LANES), F32),
                   jax.ShapeDtypeStruct((N_HEADS, T, LANES), BF16), *side.out_shape],
        scratch_shapes=[pltpu.VMEM((2, tq, LANES), F32), *side.scratch],
        compiler_params=_cp("arbitrary", "arbitrary"),
    )(qa, lse, doa, ka, va, *side.operands)
    return res[0], res[1], res[2], res[3:]


def _fwd_mid(x, oa, ob, w_out, g2, w_up):
    T = x.shape[0]
    tm = min(TM_MID, T)

    def body(x_ref, oa_ref, ob_ref, wo_ref, g_ref, wu_ref, x2_ref, h2_ref, a_ref):
        oab = jnp.concatenate([oa_ref[...], ob_ref[...].astype(BF16)], axis=-1)
        x2 = x_ref[...] + _dot(oab, wo_ref[...])
        x2_ref[...] = x2
        _, n = _rms(x2)
        h2 = (n * g_ref[...]).astype(BF16)
        h2_ref[...] = h2
        a_ref[...] = _dot(h2, wu_ref[...])

    return pl.pallas_call(
        body, name="fwd_mid", grid=(T // tm,),
        in_specs=[_rows(tm, D_MODEL), _rows(tm, D_A), _rows(tm, D_B), _full((D_MODEL, D_MODEL), True), _full((1, D_MODEL)),
                  _full((D_MODEL, D_FF2), True)],
        out_specs=[_rows(tm, D_MODEL), _rows(tm, D_MODEL), _rows(tm, D_FF2)],
        out_shape=[jax.ShapeDtypeStruct((T, D_MODEL), F32), jax.ShapeDtypeStruct((T, D_MODEL), BF16),
                   jax.ShapeDtypeStruct((T, D_FF2), F32)],
        compiler_params=_cp("arbitrary"),
    )(x, oa, ob, w_out, g2, w_up)


def _row_before(x, prev, k):
    rolled = pltpu.roll(x, k, axis=0)
    row = lax.broadcasted_iota(jnp.int32, (8, x.shape[1]), 0)
    head = rolled[0:8]
    for r in range(k):
        head = jnp.where(row == r, prev[8 - k + r:9 - k + r], head)
    return jnp.concatenate([head, rolled[8:]], axis=0)


def _row_after(x, nxt, k):
    tm = x.shape[0]
    rolled = pltpu.roll(x, tm - k, axis=0)
    row = lax.broadcasted_iota(jnp.int32, (8, x.shape[1]), 0)
    tail = rolled[tm - 8:tm]
    for r in range(k):
        tail = jnp.where(row == 8 - k + r, nxt[r:r + 1], tail)
    return jnp.concatenate([rolled[:tm - 8], tail], axis=0)


def _fwd_ffn(a, x2, wc, bc, w_down, g3, tgt):
    T = x2.shape[0]
    tm = min(TM, T)

    def body(a_ref, x2_ref, wc_ref, bc_ref, wd_ref, g_ref, tgt_ref, ac_ref, yff_ref, dx3_ref, loss_ref, dg3_ref, carry):
        @pl.when(pl.program_id(0) == 0)
        def _():
            carry[...] = jnp.zeros_like(carry)
            loss_ref[...] = jnp.zeros_like(loss_ref)
            dg3_ref[...] = jnp.zeros_like(dg3_ref)

        def conv(cs):
            a0 = a_ref[:, cs]
            prev = carry[:, cs]
            ac = (wc_ref[0:1, cs] * _row_before(a0, prev, 2) + wc_ref[1:2, cs] * _row_before(a0, prev, 1)
                  + wc_ref[2:3, cs] * a0 + bc_ref[:, cs])
            ac_ref[:, cs] = ac.astype(BF16)
            return ac

        x3 = x2_ref[...]
        for ci in range(D_FF // CW):
            gs = slice(ci * CW, (ci + 1) * CW)
            ag = conv(gs)
            av = conv(slice(D_FF + ci * CW, D_FF + (ci + 1) * CW))
            yb = (ag * jax.nn.sigmoid(ag) * av).astype(BF16)
            yff_ref[:, gs] = yb
            x3 = x3 + _dot(yb, wd_ref[gs, :])
        carry[...] = a_ref[tm - 8:tm, :]
        r, n = _rms(x3)
        g = g_ref[...]
        diff = n * g - tgt_ref[...]
        loss_ref[...] += (0.5 / D_MODEL) * jnp.sum(diff * diff)
        dout = diff * (1.0 / D_MODEL)
        dg3_ref[...] += jnp.sum(dout * n, axis=0, keepdims=True)
        dx3_ref[...] = _rms_bwd(dout, n, r, g)

    return pl.pallas_call(
        body, name="fwd_ffn", grid=(T // tm,),
        in_specs=[_rows(tm, D_FF2), _rows(tm, D_MODEL), _full((3, D_FF2)), _full((1, D_FF2)), _full((D_FF, D_MODEL), True),
                  _full((1, D_MODEL)), _rows(tm, D_MODEL)],
        out_specs=[_rows(tm, D_FF2), _rows(tm, D_FF), _rows(tm, D_MODEL), _full((8, LANES)), _full((1, D_MODEL))],
        out_shape=[jax.ShapeDtypeStruct((T, D_FF2), BF16), jax.ShapeDtypeStruct((T, D_FF), BF16),
                   jax.ShapeDtypeStruct((T, D_MODEL), F32), jax.ShapeDtypeStruct((8, LANES), F32),
                   jax.ShapeDtypeStruct((1, D_MODEL), F32)],
        scratch_shapes=[pltpu.VMEM((8, D_FF2), F32)],
        compiler_params=_cp("arbitrary"),
    )(a, x2, wc, bc, w_down, g3, tgt)


def _bwd_ffn(dx3, a, ac, yff, h2, w_down, wc):
    T = dx3.shape[0]
    tm = min(TM, T)
    nt = T // tm
    half = D_FF // 2
    shard_up, shard_down = D_FF2 // N_DEV, D_FF // N_DEV

    def body(dx3_ref, ag_ref, av_ref, acg_ref, acv_ref, yff_ref, h2_ref, wd_ref, wcg_ref, wcv_ref,
             dag_ref, dav_ref, dwcg_ref, dwcv_ref, dbcg_ref, dbcv_ref, dwd_ref, dwu_ref,
             nxt, shifted, acc_down, acc_g, acc_v, stage_up, stage_down, sem):
        c, r = pl.program_id(0), pl.program_id(1)

        @pl.when(r == 0)
        def _():
            for ref in (nxt, dwcg_ref, dwcv_ref, dbcg_ref, dbcv_ref, acc_down, acc_g, acc_v):
                ref[...] = jnp.zeros_like(ref)

        dxb = dx3_ref[...].astype(BF16)
        dy_all = _dot_nt(dxb, wd_ref[...])

        def back(a_ref, w_ref, da_ref, dwc_ref, dbc_ref, nx, cs, dac):
            a0 = a_ref[:, cs]
            shifted[0] = _row_after(dac, nxt[:, nx], 1)
            shifted[1] = _row_after(dac, nxt[:, nx], 2)
            dp1, dp2 = shifted[0], shifted[1]
            dbc_ref[:, cs] += jnp.sum(dac, axis=0, keepdims=True)
            dwc_ref[0:1, cs] += jnp.sum(dp2 * a0, axis=0, keepdims=True)
            dwc_ref[1:2, cs] += jnp.sum(dp1 * a0, axis=0, keepdims=True)
            dwc_ref[2:3, cs] += jnp.sum(dac * a0, axis=0, keepdims=True)
            da_ref[:, cs] = (w_ref[2:3, cs] * dac + w_ref[1:2, cs] * dp1 + w_ref[0:1, cs] * dp2).astype(BF16)
            nxt[:, nx] = dac[0:8]

        for ci in range(half // LANES):
            cs = slice(ci * LANES, (ci + 1) * LANES)
            dy = dy_all[:, cs]
            ag, av = acg_ref[:, cs].astype(F32), acv_ref[:, cs].astype(F32)
            sg = jax.nn.sigmoid(ag)
            back(av_ref, wcv_ref, dav_ref, dwcv_ref, dbcv_ref, slice(half + ci * LANES, half + (ci + 1) * LANES), cs,
                 dy * (ag * sg))
            back(ag_ref, wcg_ref, dag_ref, dwcg_ref, dbcg_ref, cs, cs, dy * av * (sg * (1.0 + ag * (1.0 - sg))))

        acc_down[...] += _dot_tn(yff_ref[...], dxb)
        h2 = h2_ref[...]
        acc_g[...] += _dot_tn(h2, dag_ref[...])
        acc_v[...] += _dot_tn(h2, dav_ref[...])

        @pl.when(r == nt - 1)
        def _():
            for s in range(half // shard_down):
                stage_down[...] = acc_down[s * shard_down:(s + 1) * shard_down, :].astype(BF16)
                out = pltpu.make_async_copy(stage_down, dwd_ref.at[(half // shard_down) * c + s], sem)
                out.start()
                out.wait()
            for acc, first in ((acc_g, 0), (acc_v, N_DEV // 2)):
                for s in range(half // shard_up):
                    stage_up[...] = acc[:, s * shard_up:(s + 1) * shard_up].astype(BF16)
                    out = pltpu.make_async_copy(stage_up, dwu_ref.at[first + (half // shard_up) * c + s], sem)
                    out.start()
                    out.wait()

    def cols(width, second_half):
        return pl.BlockSpec((tm, width), lambda c, r: (nt - 1 - r, c + (2 if second_half else 0)))

    def param(rows, second_half):
        return pl.BlockSpec((rows, half), lambda c, r: (0, c + (2 if second_half else 0)))

    tokens = pl.BlockSpec((tm, D_MODEL), lambda c, r: (nt - 1 - r, 0))
    return pl.pallas_call(
        body, name="bwd_ffn", grid=(2, nt),
        in_specs=[tokens, cols(half, False), cols(half, True), cols(half, False), cols(half, True), cols(half, False), tokens,
                  pl.BlockSpec((half, D_MODEL), lambda c, r: (c, 0), pipeline_mode=pl.Buffered(1)),
                  param(3, False), param(3, True)],
        out_specs=[cols(half, False), cols(half, False), param(3, False), param(3, False), param(1, False), param(1, False),
                   ANY, ANY],
        out_shape=[jax.ShapeDtypeStruct((T, D_FF), BF16), jax.ShapeDtypeStruct((T, D_FF), BF16),
                   jax.ShapeDtypeStruct((3, D_FF), F32), jax.ShapeDtypeStruct((3, D_FF), F32),
                   jax.ShapeDtypeStruct((1, D_FF), F32), jax.ShapeDtypeStruct((1, D_FF), F32),
                   jax.ShapeDtypeStruct((N_DEV, shard_down, D_MODEL), BF16), jax.ShapeDtypeStruct((N_DEV, D_MODEL, shard_up), BF16)],
        scratch_shapes=[pltpu.VMEM((8, D_FF), F32), pltpu.VMEM((2, tm, LANES), F32), pltpu.VMEM((half, D_MODEL), F32),
                        pltpu.VMEM((D_MODEL, half), F32), pltpu.VMEM((D_MODEL, half), F32),
                        pltpu.VMEM((D_MODEL, shard_up), BF16), pltpu.VMEM((shard_down, D_MODEL), BF16),
                        pltpu.SemaphoreType.DMA],
        compiler_params=_cp("arbitrary", "arbitrary"),
    )(dx3, a, a, ac, ac, yff, h2, w_down, wc, wc)


def _bwd_mid(da_g, da_v, w_up, x2, g2, dx3, w_out, ob):
    T = x2.shape[0]
    tm = min(TM, T)

    def body(dag_ref, dav_ref, wu_ref, x2_ref, g_ref, dx3_ref, wo_ref, ob_ref, hsum_ref, place_ref,
             dx2_ref, doa_ref, dob_ref, dg2_ref):
        @pl.when(pl.program_id(0) == 0)
        def _():
            dg2_ref[...] = jnp.zeros_like(dg2_ref)

        dh2 = _dot_nt(dag_ref[...], wu_ref[:, :D_FF]) + _dot_nt(dav_ref[...], wu_ref[:, D_FF:])
        r, n = _rms(x2_ref[...])
        dg2_ref[...] += jnp.sum(dh2 * n, axis=0, keepdims=True)
        dx2 = dx3_ref[...] + _rms_bwd(dh2, n, r, g_ref[...])
        dx2_ref[...] = dx2
        doab = _dot_nt(dx2.astype(BF16), wo_ref[...])
        doa_ref[...] = doab[:, :D_A]
        dob = doab[:, D_A:]
        rest, delta = dob.astype(BF16).astype(F32) * ob_ref[...], None
        for _ in range(3):
            piece = rest.astype(BF16)
            term = _dot(piece, hsum_ref[...])
            delta = term if delta is None else delta + term
            rest = rest - piece.astype(F32)
        hi, mid, lo = _split3(-delta)
        parts = jnp.concatenate([hi.astype(BF16), mid.astype(BF16), lo.astype(BF16)], axis=-1)
        placed = _dot(parts, place_ref[...])
        data = lax.broadcasted_iota(jnp.int32, (tm, LANES), 1) < HEAD_DIM
        for hd in range(N_HEADS):
            tile = dob[:, LANES * (hd // 2):LANES * (hd // 2 + 1)]
            rows = tile if hd % 2 == 0 else pltpu.roll(tile, HEAD_DIM, axis=1)
            dob_ref[hd] = jnp.where(data, rows, placed[:, LANES * hd:LANES * (hd + 1)]).astype(BF16)

    hsum = jnp.asarray(np.arange(D_B)[:, None] // HEAD_DIM == np.arange(LANES)[None, :], BF16)
    place = _bias_lane_placement()
    return pl.pallas_call(
        body, name="bwd_mid", grid=(T // tm,),
        in_specs=[_rows(tm, D_FF), _rows(tm, D_FF), _full((D_MODEL, D_FF2), True), _rows(tm, D_MODEL), _full((1, D_MODEL)),
                  _rows(tm, D_MODEL), _full((D_MODEL, D_MODEL), True), _rows(tm, D_B), _full((D_B, LANES)),
                  _full((3 * LANES, N_HEADS * LANES))],
        out_specs=[_rows(tm, D_MODEL), _rows(tm, D_A), pl.BlockSpec((N_HEADS, tm, LANES), lambda i: (0, i, 0)),
                   _full((1, D_MODEL))],
        out_shape=[jax.ShapeDtypeStruct((T, D_MODEL), F32), jax.ShapeDtypeStruct((T, D_A), F32),
                   jax.ShapeDtypeStruct((N_HEADS, T, LANES), BF16), jax.ShapeDtypeStruct((1, D_MODEL), F32)],
        compiler_params=_cp("arbitrary"),
    )(da_g, da_v, w_up, x2, g2, dx3, w_out, ob, hsum, place)


def _bwd_in_dz(dzuv, dqa, dka, dva, fl):
    T = fl.shape[0]
    tm = min(TM, T)
    nt = T // tm

    def body(dzuv_ref, dqa_ref, dka_ref, dva_ref, fl_ref, dz_ref, dfb_ref, carry):
        @pl.when(pl.program_id(0) == 0)
        def _():
            carry[...] = jnp.zeros_like(carry)
            dfb_ref[...] = jnp.zeros_like(dfb_ref)

        dc = _lanes(tm, LANES, {hd: dqa_ref[hd][:, L_ROW:L_ROW + 1] - dka_ref[hd][:, L_COL:L_COL + 1] for hd in range(N_HEADS)})
        later = (lax.broadcasted_iota(jnp.int32, (tm, tm), 1) >= lax.broadcasted_iota(jnp.int32, (tm, tm), 0)).astype(F32)
        dls = _dot_f32(later, dc) + carry[...]
        carry[...] = dls[0:1, :]
        dzf = dls * jax.nn.sigmoid(-fl_ref[...])
        dfb_ref[...] += jnp.sum(dzf, axis=0, keepdims=True)
        data = lax.broadcasted_iota(jnp.int32, (tm, LANES), 1) < HEAD_DIM

        def compact(ref, scale=None):
            def rows(hd):
                return (ref[hd] if scale is None else ref[hd] * scale).astype(BF16)

            return [jnp.where(data, rows(2 * p), pltpu.roll(rows(2 * p + 1), HEAD_DIM, axis=1)) for p in range(N_HEADS // 2)]

        dz = jnp.concatenate([dzuv_ref[...], *compact(dqa_ref, HEAD_DIM ** -0.5), *compact(dka_ref), *compact(dva_ref),
                              dzf.astype(BF16)], axis=-1)
        dz_ref[...] = dz

    rv = functools.partial(_rows, tm, rev_nt=nt)
    heads = pl.BlockSpec((N_HEADS, tm, LANES), lambda i: (0, nt - 1 - i, 0))
    return pl.pallas_call(
        body, name="bwd_in_dz", grid=(nt,),
        in_specs=[rv(2 * D_A), heads, heads, heads, rv(LANES)],
        out_specs=[rv(D_IN_PAD), _full((1, LANES))],
        out_shape=[jax.ShapeDtypeStruct((T, D_IN_PAD), BF16), jax.ShapeDtypeStruct((1, LANES), F32)],
        scratch_shapes=[pltpu.VMEM((1, LANES), F32)],
        compiler_params=_cp("arbitrary"),
    )(dzuv, dqa, dka, dva, fl)


def _bwd_in_dx(dz, x, dx2, w_in_p, g1, side):
    T = x.shape[0]
    tm = min(TM, T)
    nt = T // tm
    n = side.n

    def body(dz_ref, x_ref, dx2_ref, w_ref, g_ref, *rest):
        ins, (gx_ref, dg1_ref), outs, sems = rest[:n], rest[n:n + 2], rest[n + 2:2 * n + 2], rest[2 * n + 2:]
        i = pl.program_id(0)

        @pl.when(i == 0)
        def _():
            dg1_ref[...] = jnp.zeros_like(dg1_ref)
            side.start(ins, outs, sems)

        dh1 = _dot_nt(dz_ref[...], w_ref[...])
        r, nrm = _rms(x_ref[...])
        dg1_ref[...] += jnp.sum(dh1 * nrm, axis=0, keepdims=True)
        gx_ref[...] = dx2_ref[...] + _rms_bwd(dh1, nrm, r, g_ref[...])

        @pl.when(i == nt - 1)
        def _():
            side.finish(ins, outs, sems)

    res = pl.pallas_call(
        body, name="bwd_in_dx", grid=(nt,),
        in_specs=[_rows(tm, D_IN_PAD), _rows(tm, D_MODEL), _rows(tm, D_MODEL), _full((D_MODEL, D_IN_PAD), True),
                  _full((1, D_MODEL)), *side.specs],
        out_specs=[_rows(tm, D_MODEL), _full((1, D_MODEL)), *side.specs],
        out_shape=[jax.ShapeDtypeStruct((T, D_MODEL), F32), jax.ShapeDtypeStruct((1, D_MODEL), F32), *side.out_shape],
        scratch_shapes=side.scratch,
        compiler_params=_cp("arbitrary"),
    )(dz, x, dx2, w_in_p, g1, *side.operands)
    return res[0], res[1], res[2:]


def _matmul_tn(a_parts, b, tmm, tn, tk, name, shard_cols=None, n_valid=None):
    T = b.shape[0]
    widths = [a.shape[1] for a in a_parts]
    M, N = sum(widths), b.shape[1]
    tk = min(tk, T)
    nk = T // tk
    part_w = tmm // len(a_parts)
    n_valid = N if n_valid is None else n_valid

    def body(*refs):
        a_refs, b_ref, o_ref, obf_ref = refs[:len(a_parts)], refs[-3], refs[-2], refs[-1]
        k = pl.program_id(2)

        @pl.when(k == 0)
        def _():
            o_ref[...] = jnp.zeros_like(o_ref)

        a = [r[...].astype(BF16) for r in a_refs]
        o_ref[...] += _dot_tn(a[0] if len(a) == 1 else jnp.concatenate(a, axis=-1), b_ref[...].astype(BF16))

        @pl.when(k == nk - 1)
        def _():
            if shard_cols is None:
                obf_ref[...] = o_ref[...].astype(BF16)
            else:
                for d in range(min(tn, n_valid) // shard_cols):
                    obf_ref[d] = o_ref[:, d * shard_cols:(d + 1) * shard_cols].astype(BF16)

    if shard_cols is None:
        bf_spec, bf_shape = pl.BlockSpec((tmm, tn), lambda i, j, k: (i, j)), (M, N)
    else:
        per_tile = min(tn, n_valid) // shard_cols
        bf_spec, bf_shape = pl.BlockSpec((per_tile, tmm, shard_cols), lambda i, j, k: (j, i, 0)), (N_DEV, M, shard_cols)
    a_specs = [pl.BlockSpec((tk, part_w), lambda i, j, k: (k, i)) for _ in a_parts]
    return pl.pallas_call(
        body, name=name, grid=(M // tmm, N // tn, nk),
        in_specs=[*a_specs, pl.BlockSpec((tk, tn), lambda i, j, k: (k, j))],
        out_specs=[pl.BlockSpec((tmm, tn), lambda i, j, k: (i, j)), bf_spec],
        out_shape=[jax.ShapeDtypeStruct((M, N), F32), jax.ShapeDtypeStruct(bf_shape, BF16)],
        compiler_params=_cp("arbitrary", "arbitrary", "arbitrary"),
    )(*a_parts, b)


class _Exchange:
    def __init__(self, gather, scatter, relay):
        self.n_g, self.n, self.relay = len(gather), len(gather) + len(scatter), relay
        self.operands = [*gather, *scatter]
        self.out_shape = [jax.ShapeDtypeStruct((N_DEV, *g.shape), g.dtype) for g in gather]
        self.out_shape += [jax.ShapeDtypeStruct(s.shape, s.dtype) for s in scatter]
        self.specs = [ANY] * self.n
        n = self.n
        self.scratch = [pltpu.SemaphoreType.DMA((7 * n,)), pltpu.SemaphoreType.DMA((7 * n,)),
                        pltpu.SemaphoreType.DMA((n,))] if n else []

    def _plan(self, ins, outs, sems):
        send_sems, recv_sems, local_sems = sems
        x, y, c = (lax.axis_index(ax) for ax in MESH_AXES)
        me = 4 * x + 2 * y + c
        sibling = (x, y, 1 - c)
        chips = [(1 - x, y), (x, 1 - y), (1 - x, 1 - y)]
        peers = [sibling] + [(*chip, c) for chip in chips] + [(*chip, 1 - c) for chip in chips]

        def index(dev):
            return 4 * dev[0] + 2 * dev[1] + dev[2]

        def remote(k, src, dst, to):
            return pltpu.make_async_remote_copy(src_ref=src, dst_ref=dst, send_sem=send_sems.at[k], recv_sem=recv_sems.at[k],
                                                device_id=to, device_id_type=pl.DeviceIdType.MESH)

        local, sends, relays, recvs = [], [], [], []
        for a in range(self.n):
            src, out, base = ins[a], outs[a], 7 * a
            if a >= self.n_g:
                local.append(pltpu.make_async_copy(src.at[me], out.at[me], local_sems.at[a]))
                sends += [remote(base + k, src.at[index(peer)], out.at[me], peer) for k, peer in enumerate(peers)]
            else:
                local.append(pltpu.make_async_copy(src, out.at[me], local_sems.at[a]))
                sends += [remote(base + k, src, out.at[me], peer) for k, peer in enumerate(peers[:4 if self.relay else 7])]
            for k, peer in enumerate(peers):
                slot = out.at[index(peer)]
                if a < self.n_g and self.relay and k >= 4:
                    continue
                recv = remote(base + k, slot, slot, peer)
                if a < self.n_g and self.relay and k >= 1:
                    relays.append((recv, remote(base + 3 + k, slot, slot, sibling)))
                else:
                    recvs.append(recv)
            if a < self.n_g and self.relay:
                for j, chip in enumerate(chips):
                    slot = out.at[index((*chip, 1 - c))]
                    recvs.append(remote(base + 4 + j, slot, slot, sibling))
        return local, sends, relays, recvs

    def start(self, ins, outs, sems):
        local, sends, _, _ = self._plan(ins, outs, sems)
        for cp in local + sends:
            cp.start()

    def finish(self, ins, outs, sems):
        local, sends, relays, recvs = self._plan(ins, outs, sems)
        for recv, fwd in relays:
            recv.wait_recv()
            fwd.start()
        for recv in recvs:
            recv.wait_recv()
        for cp in sends + [fwd for _, fwd in relays]:
            cp.wait_send()
        for cp in local:
            cp.wait()


def _exchange(gather, scatter, name):
    ex = _Exchange(gather, scatter, relay=True)
    n = ex.n

    def body(*refs):
        ins, outs, sems = refs[:n], refs[n:2 * n], refs[2 * n:]
        ex.start(ins, outs, sems)
        ex.finish(ins, outs, sems)

    return pl.pallas_call(body, name=name, in_specs=ex.specs, out_specs=ex.specs, out_shape=ex.out_shape,
                          scratch_shapes=ex.scratch)(*ex.operands)


def _adamw(w, g, m, v):
    m = ADAM_B1 * m + (1.0 - ADAM_B1) * g
    v = ADAM_B2 * v + (1.0 - ADAM_B2) * jnp.square(g)
    m_hat = m / (1.0 - ADAM_B1 ** ADAM_STEP)
    v_hat = v / (1.0 - ADAM_B2 ** ADAM_STEP)
    delta = -ADAM_LR * (m_hat / (jnp.sqrt(v_hat) + ADAM_EPS) + ADAM_WD * w)
    return delta, m, v


def _adamw_shard(w, m, v, recv, tr, name):
    _, R, C = w.shape

    def body(w_ref, m_ref, v_ref, recv_ref, g_ref, d_ref, nm_ref, nv_ref):
        g = recv_ref[0].astype(F32)
        for d in range(1, N_DEV):
            g = g + recv_ref[d].astype(F32)
        g_ref[...] = g
        d_ref[...], nm_ref[...], nv_ref[...] = _adamw(w_ref[...], g, m_ref[...], v_ref[...])

    blk = pl.BlockSpec((None, tr, C), lambda i: (0, i, 0))
    return pl.pallas_call(
        body, name=name, grid=(R // tr,),
        in_specs=[blk, blk, blk, pl.BlockSpec((N_DEV, tr, C), lambda i: (0, i, 0))],
        out_specs=[blk] * 4, out_shape=[jax.ShapeDtypeStruct((1, R, C), F32)] * 4,
        compiler_params=_cp("arbitrary"),
    )(w, m, v, recv)


def _adamw_small(params, gathered, loss_parts):
    n = len(params)

    def body(*refs):
        ins, gs, loss_ref, outs = refs[:3 * n], refs[3 * n:4 * n], refs[4 * n], refs[4 * n + 1:]
        for p in range(n):
            w_ref, m_ref, v_ref = ins[3 * p:3 * p + 3]
            g = gs[p][0]
            for d in range(1, N_DEV):
                g = g + gs[p][d]
            g = g[..., :w_ref.shape[-1]]
            g_ref, d_ref, nm_ref, nv_ref = outs[4 * p:4 * p + 4]
            g_ref[...] = g
            d_ref[...], nm_ref[...], nv_ref[...] = _adamw(w_ref[...], g, m_ref[...], v_ref[...])
        total = loss_ref[0]
        for d in range(1, N_DEV):
            total = total + loss_ref[d]
        outs[4 * n][...] = total

    out_shape = [jax.ShapeDtypeStruct(w.shape, F32) for w, _, _ in params for _ in range(4)]
    res = pl.pallas_call(body, name="adamw_small", out_shape=[*out_shape, jax.ShapeDtypeStruct((8, LANES), F32)],
                         compiler_params=pltpu.CompilerParams(vmem_limit_bytes=VMEM_LIMIT))(
        *[t for p in params for t in p], *gathered, loss_parts)
    return [res[4 * p:4 * p + 4] for p in range(n)], res[4 * n][0, 0]


def _col_shards(g):
    return jnp.transpose(g.reshape(g.shape[0], N_DEV, -1), (1, 0, 2))


def _row_shards(g):
    return g.reshape(N_DEV, -1, g.shape[1])


def _cols_whole(g):
    return jnp.transpose(g, (1, 0, 2)).reshape(g.shape[1], -1)


def _join_cols(g, width, name):
    n_shards, rows, c = g.shape
    tr = min(256, rows)

    def body(g_ref, o_ref):
        for d in range(n_shards):
            o_ref[:, d * c:(d + 1) * c] = g_ref[d]
        if width > n_shards * c:
            o_ref[:, n_shards * c:] = jnp.zeros((tr, width - n_shards * c), o_ref.dtype)

    return pl.pallas_call(
        body, name=name, grid=(rows // tr,),
        in_specs=[pl.BlockSpec((n_shards, tr, c), lambda i: (0, i, 0))],
        out_specs=pl.BlockSpec((tr, width), lambda i: (i, 0)),
        out_shape=jax.ShapeDtypeStruct((rows, width), g.dtype),
        compiler_params=_cp("arbitrary"),
    )(g)


def kernel(x, norm_mix_g, w_in, f_bias, sg_ln_g, sg_w, sg_b, w_out, norm_ffn_g, w_up, w_conv, b_conv, w_down, norm_final_g, loss_target, m_norm_mix_g, m_w_in, m_f_bias, m_sg_ln_g, m_sg_w, m_sg_b, m_w_out, m_norm_ffn_g, m_w_up, m_w_conv, m_b_conv, m_w_down, m_norm_final_g, v_norm_mix_g, v_w_in, v_f_bias, v_sg_ln_g, v_sg_w, v_sg_b, v_w_out, v_norm_ffn_g, v_w_up, v_w_conv, v_b_conv, v_w_down, v_norm_final_g):
    xs, tgt = x[0], loss_target[0]
    g1, g2, g3 = norm_mix_g, norm_ffn_g, norm_final_g.reshape(1, D_MODEL)
    lg = sg_ln_g.reshape(1, D_A)
    fb = jnp.pad(f_bias, ((0, 0), (0, LANES - N_HEADS)))
    pos_chunk = np.arange(SG_BLOCK) // SG_CHUNK
    maskf = jnp.asarray(pos_chunk[:, None] >= pos_chunk[None, :], F32)
    wm = (sg_w[0] * maskf[None]).astype(BF16)
    wmt = jnp.swapaxes(wm, 1, 2)
    bfull = jnp.repeat(sg_b[0].T, HEAD_DIM, axis=1)
    same_head = np.arange(D_A)[:, None] // HEAD_DIM == np.arange(D_A)[None, :] // HEAD_DIM
    hsel = jnp.asarray(np.arange(N_HEADS)[:, None] == np.arange(D_A)[None, :] // HEAD_DIM, F32)
    avg = jnp.asarray(same_head * (1.0 / HEAD_DIM), BF16)

    (win_g,) = _exchange([w_in[0].astype(BF16)], [], "gather_w_in")
    w_in_p = _join_cols(win_g, D_IN_PAD, "join_w_in")
    zuv, qa, ka, va, fl, h1 = _fwd_in(xs, g1, w_in_p, fb)
    oa = _gmlp_fwd(zuv, lg, wm, bfull, avg)
    rest = _Exchange([w_out[0].astype(BF16), w_up[0].astype(BF16), w_down[0].astype(BF16), w_conv[0]], [], relay=False)
    ob, lse, (wout_g, wup_g, wdown_g, wc_g) = _fox_fwd(qa, ka, va, rest)
    w_out_f, w_up_f = wout_g.reshape(D_MODEL, D_MODEL), _join_cols(wup_g, D_FF2, "join_w_up")
    w_down_f, wc_f = wdown_g.reshape(D_FF, D_MODEL), _cols_whole(wc_g)

    x2, h2, a = _fwd_mid(xs, oa, ob, w_out_f, g2, w_up_f)
    ac, yff, dx3, loss, dg3 = _fwd_ffn(a, x2, wc_f, b_conv, w_down_f, g3, tgt)
    da_g, da_v, dwc_g, dwc_v, dbc_g, dbc_v, dwdown_bf, dwup_bf = _bwd_ffn(dx3, a, ac, yff, h2, w_down_f, wc_f)
    dwc, dbc = jnp.concatenate([dwc_g, dwc_v], axis=1), jnp.concatenate([dbc_g, dbc_v], axis=1)
    dx2, doa, dob, dg2 = _bwd_mid(da_g, da_v, w_up_f, x2, g2, dx3, w_out_f, ob)
    dzuv, dwm, dsgb, dlg = _gmlp_bwd(zuv, doa, lg, wm, wmt, bfull, maskf, hsel, avg)
    _, dwout_bf = _matmul_tn([oa, ob], dx2, D_MODEL, D_MODEL, TK_DW, "dw_out")

    early = ("w_out", "w_up", "wc", "w_down")
    wire = [_row_shards(dwout_bf), dwup_bf, _col_shards(dwc).astype(BF16), dwdown_bf]
    small_early = dict(lg=dlg, sg_w=dwm, sg_b=dsgb, g2=dg2, bc=dbc, g3=dg3)
    grads = _Exchange([*small_early.values(), loss], wire, relay=False)
    dqa, dka, dva, got = _fox_bwd(qa, lse, dob, ka, va, grads)
    n_small = len(small_early)
    gathered, loss_parts = dict(zip(small_early, got[:n_small])), got[n_small]
    recv = dict(zip(early, got[n_small + 1:]))

    dz, dfb = _bwd_in_dz(dzuv, dqa, dka, dva, fl)
    _, dwin_bf = _matmul_tn([h1], dz, D_MODEL // 2, D_IN_PAD, TK_DW, "dw_in", shard_cols=D_IN // N_DEV, n_valid=D_IN)
    gx, dg1, (gathered["fb"], recv["w_in"]) = _bwd_in_dx(dz, xs, dx2, w_in_p, g1, _Exchange([dfb], [dwin_bf], relay=False))
    (gathered["g1"],) = _exchange([dg1], [], "gather_dg1")

    weights = dict(w_in=(w_in, m_w_in, v_w_in, 256), w_out=(w_out, m_w_out, v_w_out, 128), w_up=(w_up, m_w_up, v_w_up, 256),
                   wc=(w_conv, m_w_conv, v_w_conv, 3), w_down=(w_down, m_w_down, v_w_down, 176))
    res = {n: _adamw_shard(w, m, v, recv[n], tr, "adamw_" + n) for n, (w, m, v, tr) in weights.items()}

    reps = dict(g1=((norm_mix_g, m_norm_mix_g, v_norm_mix_g), (1, D_MODEL)), fb=((f_bias, m_f_bias, v_f_bias), (1, N_HEADS)),
                lg=((sg_ln_g, m_sg_ln_g, v_sg_ln_g), (1, D_A)), sg_w=((sg_w, m_sg_w, v_sg_w), (N_HEADS, SG_BLOCK, SG_BLOCK)),
                sg_b=((sg_b, m_sg_b, v_sg_b), (N_HEADS, SG_BLOCK)), g2=((norm_ffn_g, m_norm_ffn_g, v_norm_ffn_g), (1, D_MODEL)),
                bc=((b_conv, m_b_conv, v_b_conv), (1, D_FF2)), g3=((norm_final_g, m_norm_final_g, v_norm_final_g), (1, D_MODEL)))
    outs, loss_sum = _adamw_small([tuple(t.reshape(shape) for t in wmv) for wmv, shape in reps.values()],
                                  [gathered[n] for n in reps], loss_parts)
    for (n, (wmv, _)), out in zip(reps.items(), outs):
        res[n] = [o.reshape(wmv[0].shape) for o in out]

    names = ("g1", "w_in", "fb", "lg", "sg_w", "sg_b", "w_out", "g2", "w_up", "wc", "bc", "w_down", "g3")
    return (loss_sum, gx[None], *[res[n][0] for n in names], *[res[n][1] for n in names],
            *[res[n][2] for n in names], *[res[n][3] for n in names])
```

```python
import functools
import math

import jax
import jax.numpy as jnp
import numpy as np
from jax import lax
from jax.experimental import pallas as pl
from jax.experimental.pallas import tpu as pltpu

F32 = jnp.float32
BF16 = jnp.bfloat16

D_MODEL = 1024
HEAD_DIM = 64
N_HEADS = 8
D_A = 512
D_B = 512
D_IN = 2 * D_A + 3 * D_B + N_HEADS
D_IN_PAD = 2688
D_FF = 2816
D_FF2 = 2 * D_FF
SG_BLOCK = 128
SG_CHUNK = 64
EPS = 1e-6
N_DEV = 8
LANES = 128
NEG = -1e30
VMEM_LIMIT = 56 * 1024 * 1024

ADAM_LR = 0.001
ADAM_B1 = 0.9
ADAM_B2 = 0.999
ADAM_EPS = 1e-08
ADAM_WD = 0.01
ADAM_STEP = 10

TM = 256
TM_MID = 512
TM_GMLP = 1024
TK_DW = 2048
TQ = 512
FWD_UNROLL = 4
BWD_UNROLL = 2
CW = 256

MESH_AXES = ("x", "y", "c")
ANY = pl.BlockSpec(memory_space=pl.ANY)


def _cp(*sem):
    return pltpu.CompilerParams(dimension_semantics=sem, vmem_limit_bytes=VMEM_LIMIT)


def _dot(a, b):
    return jnp.dot(a, b, preferred_element_type=F32)


def _dot_nt(a, b):
    return lax.dot_general(a, b, (((1,), (1,)), ((), ())), preferred_element_type=F32)


def _dot_tn(a, b):
    return lax.dot_general(a, b, (((0,), (0,)), ((), ())), preferred_element_type=F32)


def _dot_f32(a, b):
    return jnp.dot(a, b, precision=lax.Precision.HIGHEST, preferred_element_type=F32)


def _gelu(z):
    return 0.5 * z * (1.0 + lax.erf(z * (1.0 / math.sqrt(2.0))))


def _gelu_grad(z):
    return 0.5 * (1.0 + lax.erf(z * (1.0 / math.sqrt(2.0)))) + z * jnp.exp(-0.5 * z * z) * (1.0 / math.sqrt(2.0 * math.pi))


def _log_sigmoid(x):
    return jnp.minimum(x, 0.0) - jnp.log1p(jnp.exp(-jnp.abs(x)))


def _rms(x):
    r = lax.rsqrt(jnp.mean(x * x, axis=-1, keepdims=True) + EPS)
    return r, x * r


def _rms_bwd(dy, n, r, g):
    dn = dy * g
    return r * (dn - n * jnp.mean(dn * n, axis=-1, keepdims=True))


def _full(shape, single=False):
    nd = len(shape)
    if single:
        return pl.BlockSpec(shape, lambda *_: (0,) * nd, pipeline_mode=pl.Buffered(1))
    return pl.BlockSpec(shape, lambda *_: (0,) * nd)


def _rows(tm, cols, rev_nt=None):
    if rev_nt is None:
        return pl.BlockSpec((tm, cols), lambda i: (i, 0))
    return pl.BlockSpec((tm, cols), lambda i: (rev_nt - 1 - i, 0))


def _head_sl(h):
    return slice(HEAD_DIM * h, HEAD_DIM * (h + 1))


L_ROW = HEAD_DIM
L_COL = HEAD_DIM + 3
L_LSE = HEAD_DIM + 6


def _split3(x):
    hi = x.astype(BF16).astype(F32)
    mid = (x - hi).astype(BF16).astype(F32)
    lo = (x - hi - mid).astype(BF16).astype(F32)
    return hi, mid, lo


def _bias_lane_placement():
    j, h = np.arange(3 * LANES) // LANES, np.arange(3 * LANES) % LANES
    cols = np.arange(N_HEADS * LANES)[None, :]
    return jnp.asarray((h[:, None] < N_HEADS) & (cols == LANES * h[:, None] + L_ROW + j[:, None]), BF16)


def _lanes(rows, width, parts):
    lane = lax.broadcasted_iota(jnp.int32, (rows, width), 1)
    out = jnp.zeros((rows, width), F32)
    for at, val in parts.items():
        out = jnp.where(lane == at, val, out)
    return out


def _fwd_in(x, g1, w_in_p, fb):
    T = x.shape[0]
    tm = min(TM, T)

    def body(x_ref, g_ref, w_ref, fb_ref, place_ref, zuv_ref, qa_ref, ka_ref, va_ref, fl_ref, h1_ref, carry):
        @pl.when(pl.program_id(0) == 0)
        def _():
            carry[...] = jnp.zeros_like(carry)

        r, n = _rms(x_ref[...])
        h = (n * g_ref[...]).astype(BF16)
        h1_ref[...] = h
        z = _dot(h, w_ref[...])
        zuv_ref[...] = z[:, :2 * D_A]
        o = 2 * D_A
        fl = z[:, o + 3 * D_B:] + fb_ref[...]
        fl_ref[...] = fl
        tri = (lax.broadcasted_iota(jnp.int32, (tm, tm), 0) >= lax.broadcasted_iota(jnp.int32, (tm, tm), 1)).astype(F32)
        c = _dot_f32(tri, _log_sigmoid(fl)) + carry[...]
        carry[...] = c[tm - 1:tm, :]
        hi, mid, lo = _split3(c)
        parts = jnp.concatenate([hi.astype(BF16), mid.astype(BF16), lo.astype(BF16)], axis=-1)
        placed = _dot(parts, place_ref[...])
        lane = lax.broadcasted_iota(jnp.int32, (tm, LANES), 1)
        data = lane < HEAD_DIM
        ones_q = ((lane >= L_COL) & (lane < L_COL + 3)).astype(F32)
        ones_k = (((lane >= L_ROW) & (lane < L_ROW + 3)) | ((lane >= L_LSE) & (lane < L_LSE + 3))).astype(F32)
        ones_v = ((lane >= L_ROW) & (lane < L_ROW + 3)).astype(F32)
        for hd in range(N_HEADS):
            def rows_of(first_col):
                tile = z[:, first_col + LANES * (hd // 2):first_col + LANES * (hd // 2 + 1)]
                return tile if hd % 2 == 0 else pltpu.roll(tile, HEAD_DIM, axis=1)

            hs = slice(LANES * hd, LANES * (hd + 1))
            qa_ref[hd] = jnp.where(data, rows_of(o) * (HEAD_DIM ** -0.5), placed[:, hs] + ones_q).astype(BF16)
            key_side = pltpu.roll(placed[:, hs], L_COL - L_ROW, axis=1)
            ka_ref[hd] = jnp.where(data, rows_of(o + D_B), ones_k - key_side).astype(BF16)
            va_ref[hd] = jnp.where(data, rows_of(o + 2 * D_B), ones_v).astype(BF16)

    heads = pl.BlockSpec((N_HEADS, tm, LANES), lambda i: (0, i, 0))
    aug = jax.ShapeDtypeStruct((N_HEADS, T, LANES), BF16)
    place = _bias_lane_placement()
    return pl.pallas_call(
        body, name="fwd_in", grid=(T // tm,),
        in_specs=[_rows(tm, D_MODEL), _full((1, D_MODEL)), _full((D_MODEL, D_IN_PAD), True), _full((1, LANES)),
                  _full((3 * LANES, N_HEADS * LANES))],
        out_specs=[_rows(tm, 2 * D_A), heads, heads, heads, _rows(tm, LANES), _rows(tm, D_MODEL)],
        out_shape=[jax.ShapeDtypeStruct((T, 2 * D_A), F32), aug, aug, aug, jax.ShapeDtypeStruct((T, LANES), F32),
                   jax.ShapeDtypeStruct((T, D_MODEL), BF16)],
        scratch_shapes=[pltpu.VMEM((1, LANES), F32)],
        compiler_params=_cp("arbitrary"),
    )(x, g1, w_in_p, fb, place)


def _segmean(x, avg_ref, parts):
    out, rest = None, x
    for _ in range(parts):
        piece = rest.astype(BF16)
        term = _dot(piece, avg_ref[...])
        out = term if out is None else out + term
        rest = rest - piece.astype(F32)
    return out


def _layer_norm(v, lg, avg_ref):
    d = v - _segmean(v, avg_ref, 3)
    rstd = lax.rsqrt(_segmean(d * d, avg_ref, 2) + EPS)
    vhat = d * rstd
    return vhat, rstd, (vhat * lg).astype(BF16)


def _mix_heads(w_ref, x, row_slice):
    low = lax.broadcasted_iota(jnp.int32, (SG_BLOCK, LANES), 1) < HEAD_DIM
    tiles = []
    for p in range(N_HEADS // 2):
        xt = x[row_slice, LANES * p:LANES * (p + 1)]
        zero = jnp.zeros_like(xt)
        tiles.append(_dot(w_ref[2 * p], jnp.where(low, xt, zero)) + _dot(w_ref[2 * p + 1], jnp.where(low, zero, xt)))
    return jnp.concatenate(tiles, axis=-1)


def _gmlp_fwd(zuv, lg, wm, bfull, avg):
    T = zuv.shape[0]
    tg = min(TM_GMLP, T)
    nb = tg // SG_BLOCK

    def body(zuv_ref, lg_ref, wm_ref, b_ref, avg_ref, oa_ref):
        u = _gelu(zuv_ref[:, :D_A])
        _, _, vn = _layer_norm(_gelu(zuv_ref[:, D_A:]), lg_ref[...], avg_ref)
        for n in range(nb):
            rs = slice(n * SG_BLOCK, (n + 1) * SG_BLOCK)
            oa_ref[rs, :] = (u[rs] * (_mix_heads(wm_ref, vn, rs) + b_ref[...])).astype(BF16)

    return pl.pallas_call(
        body, name="gmlp_fwd", grid=(T // tg,),
        in_specs=[_rows(tg, 2 * D_A), _full((1, D_A)), _full((N_HEADS, SG_BLOCK, SG_BLOCK)), _full((SG_BLOCK, D_A)),
                  _full((D_A, D_A))],
        out_specs=_rows(tg, D_A),
        out_shape=jax.ShapeDtypeStruct((T, D_A), BF16),
        compiler_params=_cp("arbitrary"),
    )(zuv, lg, wm, bfull, avg)


def _gmlp_bwd(zuv, doa, lg, wm, wmt, bfull, maskf, hsel, avg):
    T = zuv.shape[0]
    tg = min(TM_GMLP, T)
    nb = tg // SG_BLOCK
    nt = T // tg

    def body(zuv_ref, doa_ref, lg_ref, wm_ref, wmt_ref, b_ref, mask_ref, hsel_ref, avg_ref,
             dzuv_ref, dwm_ref, dsgb_ref, dlg_ref, dbacc):
        i = pl.program_id(0)

        @pl.when(i == 0)
        def _():
            dwm_ref[...] = jnp.zeros_like(dwm_ref)
            dlg_ref[...] = jnp.zeros_like(dlg_ref)
            dbacc[...] = jnp.zeros_like(dbacc)

        zu, zv = zuv_ref[:, :D_A], zuv_ref[:, D_A:]
        lgv = lg_ref[...]
        vhat, rstd, vn = _layer_norm(_gelu(zv), lgv, avg_ref)
        dmixed = doa_ref[...] * _gelu(zu)
        dmb = dmixed.astype(BF16)
        low = lax.broadcasted_iota(jnp.int32, (SG_BLOCK, LANES), 1) < HEAD_DIM
        mixed, dvn = [], []
        for n in range(nb):
            rs = slice(n * SG_BLOCK, (n + 1) * SG_BLOCK)
            mixed.append(_mix_heads(wm_ref, vn, rs) + b_ref[...])
            dvn.append(_mix_heads(wmt_ref, dmb, rs))
            dbacc[...] += dmixed[rs]
            for p in range(N_HEADS // 2):
                ls = slice(LANES * p, LANES * (p + 1))
                dmt, vnt = dmb[rs, ls], vn[rs, ls]
                zero = jnp.zeros_like(dmt)
                dwm_ref[2 * p] += _dot_nt(jnp.where(low, dmt, zero), vnt) * mask_ref[...]
                dwm_ref[2 * p + 1] += _dot_nt(jnp.where(low, zero, dmt), vnt) * mask_ref[...]
        mixed = jnp.concatenate(mixed, axis=0) if nb > 1 else mixed[0]
        dvn = jnp.concatenate(dvn, axis=0) if nb > 1 else dvn[0]
        dlg_ref[...] += jnp.sum(dvn * vhat, axis=0, keepdims=True)
        dvhat = dvn * lgv
        dv = rstd * (dvhat - _segmean(dvhat, avg_ref, 2) - vhat * _segmean(dvhat * vhat, avg_ref, 2))
        dzuv_ref[:, :D_A] = (doa_ref[...] * mixed * _gelu_grad(zu)).astype(BF16)
        dzuv_ref[:, D_A:] = (dv * _gelu_grad(zv)).astype(BF16)

        @pl.when(i == nt - 1)
        def _():
            dsgb_ref[...] = lax.dot_general(hsel_ref[...], dbacc[...], (((1,), (1,)), ((), ())),
                                            precision=lax.Precision.HIGHEST, preferred_element_type=F32)

    return pl.pallas_call(
        body, name="gmlp_bwd", grid=(nt,),
        in_specs=[_rows(tg, 2 * D_A), _rows(tg, D_A), _full((1, D_A)), _full((N_HEADS, SG_BLOCK, SG_BLOCK)),
                  _full((N_HEADS, SG_BLOCK, SG_BLOCK)), _full((SG_BLOCK, D_A)), _full((SG_BLOCK, SG_BLOCK)),
                  _full((N_HEADS, D_A)), _full((D_A, D_A))],
        out_specs=[_rows(tg, 2 * D_A), _full((N_HEADS, SG_BLOCK, SG_BLOCK)), _full((N_HEADS, SG_BLOCK)), _full((1, D_A))],
        out_shape=[jax.ShapeDtypeStruct((T, 2 * D_A), BF16), jax.ShapeDtypeStruct((N_HEADS, SG_BLOCK, SG_BLOCK), F32),
                   jax.ShapeDtypeStruct((N_HEADS, SG_BLOCK), F32), jax.ShapeDtypeStruct((1, D_A), F32)],
        scratch_shapes=[pltpu.VMEM((SG_BLOCK, D_A), F32)],
        compiler_params=_cp("arbitrary"),
    )(zuv, doa, lg, wm, wmt, bfull, maskf, hsel, avg)


def _fox_fwd(qa, ka, va, side):
    T = qa.shape[1]
    tq = min(TQ, T)
    nq = T // tq
    n = side.n

    def body(qa_ref, ka_ref, va_ref, *rest):
        ins, (o_ref, lse_ref), outs = rest[:n], rest[n:n + 2], rest[n + 2:2 * n + 2]
        acc_ref, *sems = rest[2 * n + 2:]
        i = pl.program_id(1)
        if n:
            @pl.when((pl.program_id(0) == 0) & (i == 0))
            def _():
                side.start(ins, outs, sems)
        row = lax.broadcasted_iota(jnp.int32, (tq, tq), 0)
        col = lax.broadcasted_iota(jnp.int32, (tq, tq), 1)
        qs = [qa_ref[0], qa_ref[1]]

        def tiles(js, carry, diag):
            offs = [pl.multiple_of(j * tq, tq) for j in js]
            logits = [[_dot_nt(qs[hh], ka_ref[hh, pl.ds(off, tq), :]) for hh in range(2)] for off in offs]
            carry = list(carry)
            for off, per_head in zip(offs, logits):
                for hh, s in enumerate(per_head):
                    if diag:
                        s = jnp.where(col <= row, s, NEG)
                    m = carry[hh]
                    m_new = jnp.maximum(m, jnp.max(s, axis=-1, keepdims=True))
                    pr = jnp.exp(s - m_new)
                    acc_ref[hh] = jnp.exp(m - m_new) * acc_ref[hh] + _dot(pr.astype(BF16), va_ref[hh, pl.ds(off, tq), :])
                    carry[hh] = m_new
            return tuple(carry)

        acc_ref[...] = jnp.zeros_like(acc_ref)
        init = (jnp.full((tq, 1), NEG, F32),) * 2
        carry = lax.fori_loop(0, i // FWD_UNROLL, lambda t, cr: tiles([FWD_UNROLL * t + u for u in range(FWD_UNROLL)], cr, False), init)
        carry = lax.fori_loop(i - i % FWD_UNROLL, i, lambda j, cr: tiles([j], cr, False), carry)
        carry = tiles([i], carry, True)
        for hh in range(2):
            m, acc = carry[hh], acc_ref[hh]
            l = acc[:, L_ROW:L_ROW + 1]
            o_ref[:, _head_sl(hh)] = acc[:, :HEAD_DIM] / l
            hi, mid, lo = _split3(-(m + jnp.log(l)))
            lse_ref[hh] = _lanes(tq, LANES, {L_LSE: hi, L_LSE + 1: mid, L_LSE + 2: lo}).astype(BF16)
        if n:
            @pl.when((pl.program_id(0) == N_HEADS // 2 - 1) & (i == nq - 1))
            def _():
                side.finish(ins, outs, sems)

    tile = pl.BlockSpec((2, tq, LANES), lambda p, i: (p, i, 0))
    seq = pl.BlockSpec((2, T, LANES), lambda p, i: (p, 0, 0))
    res = pl.pallas_call(
        body, name="fox_fwd", grid=(N_HEADS // 2, nq),
        in_specs=[tile, seq, seq, *side.specs],
        out_specs=[pl.BlockSpec((tq, LANES), lambda p, i: (i, p)), tile, *side.specs],
        out_shape=[jax.ShapeDtypeStruct((T, D_B), F32), jax.ShapeDtypeStruct((N_HEADS, T, LANES), BF16), *side.out_shape],
        scratch_shapes=[pltpu.VMEM((2, tq, LANES), F32), *side.scratch],
        compiler_params=_cp("arbitrary", "arbitrary"),
    )(qa, ka, va, *side.operands)
    return res[0], res[1], res[2:]


def _fox_bwd(qa, lse, doa, ka, va, side):
    T = qa.shape[1]
    tq = min(TQ, T)
    nq = T // tq
    n = side.n

    def body(qa_ref, lse_ref, doa_ref, ka_ref, va_ref, *rest):
        ins, (dqa_ref, dka_ref, dva_ref), outs = rest[:n], rest[n:n + 3], rest[n + 3:2 * n + 3]
        dv_acc, *sems = rest[2 * n + 3:]
        j = pl.program_id(1)
        if n:
            @pl.when((pl.program_id(0) == 0) & (j == 0))
            def _():
                side.start(ins, outs, sems)

        @pl.when(j == 0)
        def _():
            dqa_ref[...] = jnp.zeros_like(dqa_ref)

        row = lax.broadcasted_iota(jnp.int32, (tq, tq), 0)
        col = lax.broadcasted_iota(jnp.int32, (tq, tq), 1)
        ks = [ka_ref[0], ka_ref[1]]
        vs = [va_ref[0], va_ref[1]]

        def tiles(ids, diag):
            work = []
            for i in ids:
                off = pl.multiple_of(i * tq, tq)
                for hh in range(2):
                    qi = qa_ref[hh, pl.ds(off, tq), :] + lse_ref[hh, pl.ds(off, tq), :]
                    doi = doa_ref[hh, pl.ds(off, tq), :]
                    work.append((off, hh, qi, doi, _dot_nt(ks[hh], qi), _dot_nt(vs[hh], doi)))
            for off, hh, qi, doi, st, dpt in work:
                if diag:
                    st = jnp.where(row <= col, st, NEG)
                pt = jnp.exp(st)
                dv_acc[hh] += _dot(pt.astype(BF16), doi)
                dsb = (pt * dpt).astype(BF16)
                dka_ref[hh] += _dot(dsb, qi)
                dqa_ref[hh, pl.ds(off, tq), :] += _dot_tn(dsb, ks[hh])

        dka_ref[...] = jnp.zeros_like(dka_ref)
        dv_acc[...] = jnp.zeros_like(dv_acc)
        tiles([j], True)
        todo = nq - 1 - j

        @pl.loop(0, todo // BWD_UNROLL)
        def _(t):
            tiles([j + 1 + BWD_UNROLL * t + u for u in range(BWD_UNROLL)], False)

        @pl.loop(nq - todo % BWD_UNROLL, nq)
        def _(i):
            tiles([i], False)

        dva_ref[...] = dv_acc[...].astype(BF16)
        if n:
            @pl.when((pl.program_id(0) == N_HEADS // 2 - 1) & (j == nq - 1))
            def _():
                side.finish(ins, outs, sems)

    tile = pl.BlockSpec((2, tq, LANES), lambda p, j: (p, j, 0))
    seq = pl.BlockSpec((2, T, LANES), lambda p, j: (p, 0, 0))
    res = pl.pallas_call(
        body, name="fox_bwd", grid=(N_HEADS // 2, nq),
        in_specs=[seq, seq, seq, tile, tile, *side.specs],
        out_specs=[seq, tile, tile, *side.specs],
        out_shape=[jax.ShapeDtypeStruct((N_HEADS, T, LANES), F32), jax.ShapeDtypeStruct((N_HEADS, T, LANES), F32),
                   jax.ShapeDtypeStruct((N_HEADS, T, LANES), BF16), *side.out_shape],
        scratch_shapes=[pltpu.VMEM((2, tq, LANES), F32), *side.scratch],
        compiler_params=_cp("arbitrary", "arbitrary"),
    )(qa, lse, doa, ka, va, *side.operands)
    return res[0], res[1], res[2], res[3:]


def _fwd_mid(x, oa, ob, w_out, g2, w_up):
    T = x.shape[0]
    tm = min(TM_MID, T)

    def body(x_ref, oa_ref, ob_ref, wo_ref, g_ref, wu_ref, x2_ref, h2_ref, a_ref):
        oab = jnp.concatenate([oa_ref[...], ob_ref[...].astype(BF16)], axis=-1)
        x2 = x_ref[...] + _dot(oab, wo_ref[...])
        x2_ref[...] = x2
        _, n = _rms(x2)
        h2 = (n * g_ref[...]).astype(BF16)
        h2_ref[...] = h2
        a_ref[...] = _dot(h2, wu_ref[...])

    return pl.pallas_call(
        body, name="fwd_mid", grid=(T // tm,),
        in_specs=[_rows(tm, D_MODEL), _rows(tm, D_A), _rows(tm, D_B), _full((D_MODEL, D_MODEL), True), _full((1, D_MODEL)),
                  _full((D_MODEL, D_FF2), True)],
        out_specs=[_rows(tm, D_MODEL), _rows(tm, D_MODEL), _rows(tm, D_FF2)],
        out_shape=[jax.ShapeDtypeStruct((T, D_MODEL), F32), jax.ShapeDtypeStruct((T, D_MODEL), BF16),
                   jax.ShapeDtypeStruct((T, D_FF2), F32)],
        compiler_params=_cp("arbitrary"),
    )(x, oa, ob, w_out, g2, w_up)


def _row_before(x, prev, k):
    rolled = pltpu.roll(x, k, axis=0)
    row = lax.broadcasted_iota(jnp.int32, (8, x.shape[1]), 0)
    head = rolled[0:8]
    for r in range(k):
        head = jnp.where(row == r, prev[8 - k + r:9 - k + r], head)
    return jnp.concatenate([head, rolled[8:]], axis=0)


def _row_after(x, nxt, k):
    tm = x.shape[0]
    rolled = pltpu.roll(x, tm - k, axis=0)
    row = lax.broadcasted_iota(jnp.int32, (8, x.shape[1]), 0)
    tail = rolled[tm - 8:tm]
    for r in range(k):
        tail = jnp.where(row == 8 - k + r, nxt[r:r + 1], tail)
    return jnp.concatenate([rolled[:tm - 8], tail], axis=0)


def _fwd_ffn(a, x2, wc, bc, w_down, g3, tgt):
    T = x2.shape[0]
    tm = min(TM, T)

    def body(a_ref, x2_ref, wc_ref, bc_ref, wd_ref, g_ref, tgt_ref, ac_ref, yff_ref, dx3_ref, loss_ref, dg3_ref, carry):
        @pl.when(pl.program_id(0) == 0)
        def _():
            carry[...] = jnp.zeros_like(carry)
            loss_ref[...] = jnp.zeros_like(loss_ref)
            dg3_ref[...] = jnp.zeros_like(dg3_ref)

        def conv(cs):
            a0 = a_ref[:, cs]
            prev = carry[:, cs]
            ac = (wc_ref[0:1, cs] * _row_before(a0, prev, 2) + wc_ref[1:2, cs] * _row_before(a0, prev, 1)
                  + wc_ref[2:3, cs] * a0 + bc_ref[:, cs])
            ac_ref[:, cs] = ac.astype(BF16)
            return ac

        x3 = x2_ref[...]
        for ci in range(D_FF // CW):
            gs = slice(ci * CW, (ci + 1) * CW)
            ag = conv(gs)
            av = conv(slice(D_FF + ci * CW, D_FF + (ci + 1) * CW))
            yb = (ag * jax.nn.sigmoid(ag) * av).astype(BF16)
            yff_ref[:, gs] = yb
            x3 = x3 + _dot(yb, wd_ref[gs, :])
        carry[...] = a_ref[tm - 8:tm, :]
        r, n = _rms(x3)
        g = g_ref[...]
        diff = n * g - tgt_ref[...]
        loss_ref[...] += (0.5 / D_MODEL) * jnp.sum(diff * diff)
        dout = diff * (1.0 / D_MODEL)
        dg3_ref[...] += jnp.sum(dout * n, axis=0, keepdims=True)
        dx3_ref[...] = _rms_bwd(dout, n, r, g)

    return pl.pallas_call(
        body, name="fwd_ffn", grid=(T // tm,),
        in_specs=[_rows(tm, D_FF2), _rows(tm, D_MODEL), _full((3, D_FF2)), _full((1, D_FF2)), _full((D_FF, D_MODEL), True),
                  _full((1, D_MODEL)), _rows(tm, D_MODEL)],
        out_specs=[_rows(tm, D_FF2), _rows(tm, D_FF), _rows(tm, D_MODEL), _full((8, LANES)), _full((1, D_MODEL))],
        out_shape=[jax.ShapeDtypeStruct((T, D_FF2), BF16), jax.ShapeDtypeStruct((T, D_FF), BF16),
                   jax.ShapeDtypeStruct((T, D_MODEL), F32), jax.ShapeDtypeStruct((8, LANES), F32),
                   jax.ShapeDtypeStruct((1, D_MODEL), F32)],
        scratch_shapes=[pltpu.VMEM((8, D_FF2), F32)],
        compiler_params=_cp("arbitrary"),
    )(a, x2, wc, bc, w_down, g3, tgt)


def _bwd_ffn(dx3, a, ac, yff, h2, w_down, wc):
    T = dx3.shape[0]
    tm = min(TM, T)
    nt = T // tm
    half = D_FF // 2
    shard_up, shard_down = D_FF2 // N_DEV, D_FF // N_DEV

    def body(dx3_ref, ag_ref, av_ref, acg_ref, acv_ref, yff_ref, h2_ref, wd_ref, wcg_ref, wcv_ref,
             dag_ref, dav_ref, dwcg_ref, dwcv_ref, dbcg_ref, dbcv_ref, dwd_ref, dwu_ref,
             nxt, shifted, acc_down, acc_g, acc_v, stage_up, stage_down, sem):
        c, r = pl.program_id(0), pl.program_id(1)

        @pl.when(r == 0)
        def _():
            for ref in (nxt, dwcg_ref, dwcv_ref, dbcg_ref, dbcv_ref, acc_down, acc_g, acc_v):
                ref[...] = jnp.zeros_like(ref)

        dxb = dx3_ref[...].astype(BF16)
        dy_all = _dot_nt(dxb, wd_ref[...])

        def back(a_ref, w_ref, da_ref, dwc_ref, dbc_ref, nx, cs, dac):
            a0 = a_ref[:, cs]
            shifted[0] = _row_after(dac, nxt[:, nx], 1)
            shifted[1] = _row_after(dac, nxt[:, nx], 2)
            dp1, dp2 = shifted[0], shifted[1]
            dbc_ref[:, cs] += jnp.sum(dac, axis=0, keepdims=True)
            dwc_ref[0:1, cs] += jnp.sum(dp2 * a0, axis=0, keepdims=True)
            dwc_ref[1:2, cs] += jnp.sum(dp1 * a0, axis=0, keepdims=True)
            dwc_ref[2:3, cs] += jnp.sum(dac * a0, axis=0, keepdims=True)
            da_ref[:, cs] = (w_ref[2:3, cs] * dac + w_ref[1:2, cs] * dp1 + w_ref[0:1, cs] * dp2).astype(BF16)
            nxt[:, nx] = dac[0:8]

        for ci in range(half // LANES):
            cs = slice(ci * LANES, (ci + 1) * LANES)
            dy = dy_all[:, cs]
            ag, av = acg_ref[:, cs].astype(F32), acv_ref[:, cs].astype(F32)
            sg = jax.nn.sigmoid(ag)
            back(av_ref, wcv_ref, dav_ref, dwcv_ref, dbcv_ref, slice(half + ci * LANES, half + (ci + 1) * LANES), cs,
                 dy * (ag * sg))
            back(ag_ref, wcg_ref, dag_ref, dwcg_ref, dbcg_ref, cs, cs, dy * av * (sg * (1.0 + ag * (1.0 - sg))))

        acc_down[...] += _dot_tn(yff_ref[...], dxb)
        h2 = h2_ref[...]
        acc_g[...] += _dot_tn(h2, dag_ref[...])
        acc_v[...] += _dot_tn(h2, dav_ref[...])

        @pl.when(r == nt - 1)
        def _():
            for s in range(half // shard_down):
                stage_down[...] = acc_down[s * shard_down:(s + 1) * shard_down, :].astype(BF16)
                out = pltpu.make_async_copy(stage_down, dwd_ref.at[(half // shard_down) * c + s], sem)
                out.start()
                out.wait()
            for acc, first in ((acc_g, 0), (acc_v, N_DEV // 2)):
                for s in range(half // shard_up):
                    stage_up[...] = acc[:, s * shard_up:(s + 1) * shard_up].astype(BF16)
                    out = pltpu.make_async_copy(stage_up, dwu_ref.at[first + (half // shard_up) * c + s], sem)
                    out.start()
                    out.wait()

    def cols(width, second_half):
        return pl.BlockSpec((tm, width), lambda c, r: (nt - 1 - r, c + (2 if second_half else 0)))

    def param(rows, second_half):
        return pl.BlockSpec((rows, half), lambda c, r: (0, c + (2 if second_half else 0)))

    tokens = pl.BlockSpec((tm, D_MODEL), lambda c, r: (nt - 1 - r, 0))
    return pl.pallas_call(
        body, name="bwd_ffn", grid=(2, nt),
        in_specs=[tokens, cols(half, False), cols(half, True), cols(half, False), cols(half, True), cols(half, False), tokens,
                  pl.BlockSpec((half, D_MODEL), lambda c, r: (c, 0), pipeline_mode=pl.Buffered(1)),
                  param(3, False), param(3, True)],
        out_specs=[cols(half, False), cols(half, False), param(3, False), param(3, False), param(1, False), param(1, False),
                   ANY, ANY],
        out_shape=[jax.ShapeDtypeStruct((T, D_FF), BF16), jax.ShapeDtypeStruct((T, D_FF), BF16),
                   jax.ShapeDtypeStruct((3, D_FF), F32), jax.ShapeDtypeStruct((3, D_FF), F32),
                   jax.ShapeDtypeStruct((1, D_FF), F32), jax.ShapeDtypeStruct((1, D_FF), F32),
                   jax.ShapeDtypeStruct((N_DEV, shard_down, D_MODEL), BF16), jax.ShapeDtypeStruct((N_DEV, D_MODEL, shard_up), BF16)],
        scratch_shapes=[pltpu.VMEM((8, D_FF), F32), pltpu.VMEM((2, tm, LANES), F32), pltpu.VMEM((half, D_MODEL), F32),
                        pltpu.VMEM((D_MODEL, half), F32), pltpu.VMEM((D_MODEL, half), F32),
                        pltpu.VMEM((D_MODEL, shard_up), BF16), pltpu.VMEM((shard_down, D_MODEL), BF16),
                        pltpu.SemaphoreType.DMA],
        compiler_params=_cp("arbitrary", "arbitrary"),
    )(dx3, a, a, ac, ac, yff, h2, w_down, wc, wc)


def _bwd_mid(da_g, da_v, w_up, x2, g2, dx3, w_out, ob):
    T = x2.shape[0]
    tm = min(TM, T)

    def body(dag_ref, dav_ref, wu_ref, x2_ref, g_ref, dx3_ref, wo_ref, ob_ref, hsum_ref, place_ref,
             dx2_ref, doa_ref, dob_ref, dg2_ref):
        @pl.when(pl.program_id(0) == 0)
        def _():
            dg2_ref[...] = jnp.zeros_like(dg2_ref)

        dh2 = _dot_nt(dag_ref[...], wu_ref[:, :D_FF]) + _dot_nt(dav_ref[...], wu_ref[:, D_FF:])
        r, n = _rms(x2_ref[...])
        dg2_ref[...] += jnp.sum(dh2 * n, axis=0, keepdims=True)
        dx2 = dx3_ref[...] + _rms_bwd(dh2, n, r, g_ref[...])
        dx2_ref[...] = dx2
        doab = _dot_nt(dx2.astype(BF16), wo_ref[...])
        doa_ref[...] = doab[:, :D_A]
        dob = doab[:, D_A:]
        rest, delta = dob.astype(BF16).astype(F32) * ob_ref[...], None
        for _ in range(3):
            piece = rest.astype(BF16)
            term = _dot(piece, hsum_ref[...])
            delta = term if delta is None else delta + term
            rest = rest - piece.astype(F32)
        hi, mid, lo = _split3(-delta)
        parts = jnp.concatenate([hi.astype(BF16), mid.astype(BF16), lo.astype(BF16)], axis=-1)
        placed = _dot(parts, place_ref[...])
        data = lax.broadcasted_iota(jnp.int32, (tm, LANES), 1) < HEAD_DIM
        for hd in range(N_HEADS):
            tile = dob[:, LANES * (hd // 2):LANES * (hd // 2 + 1)]
            rows = tile if hd % 2 == 0 else pltpu.roll(tile, HEAD_DIM, axis=1)
            dob_ref[hd] = jnp.where(data, rows, placed[:, LANES * hd:LANES * (hd + 1)]).astype(BF16)

    hsum = jnp.asarray(np.arange(D_B)[:, None] // HEAD_DIM == np.arange(LANES)[None, :], BF16)
    place = _bias_lane_placement()
    return pl.pallas_call(
        body, name="bwd_mid", grid=(T // tm,),
        in_specs=[_rows(tm, D_FF), _rows(tm, D_FF), _full((D_MODEL, D_FF2), True), _rows(tm, D_MODEL), _full((1, D_MODEL)),
                  _rows(tm, D_MODEL), _full((D_MODEL, D_MODEL), True), _rows(tm, D_B), _full((D_B, LANES)),
                  _full((3 * LANES, N_HEADS * LANES))],
        out_specs=[_rows(tm, D_MODEL), _rows(tm, D_A), pl.BlockSpec((N_HEADS, tm, LANES), lambda i: (0, i, 0)),
                   _full((1, D_MODEL))],
        out_shape=[jax.ShapeDtypeStruct((T, D_MODEL), F32), jax.ShapeDtypeStruct((T, D_A), F32),
                   jax.ShapeDtypeStruct((N_HEADS, T, LANES), BF16), jax.ShapeDtypeStruct((1, D_MODEL), F32)],
        compiler_params=_cp("arbitrary"),
    )(da_g, da_v, w_up, x2, g2, dx3, w_out, ob, hsum, place)


def _bwd_in(dzuv, dqa, dka, dva, fl, x, dx2, w_in_p, g1):
    T = x.shape[0]
    tm = min(TM, T)
    nt = T // tm

    def body(dzuv_ref, dqa_ref, dka_ref, dva_ref, fl_ref, x_ref, dx2_ref, w_ref, g_ref,
             gx_ref, dz_ref, dg1_ref, dfb_ref, carry):
        @pl.when(pl.program_id(0) == 0)
        def _():
            carry[...] = jnp.zeros_like(carry)
            dg1_ref[...] = jnp.zeros_like(dg1_ref)
            dfb_ref[...] = jnp.zeros_like(dfb_ref)

        dc = _lanes(tm, LANES, {hd: dqa_ref[hd][:, L_ROW:L_ROW + 1] - dka_ref[hd][:, L_COL:L_COL + 1] for hd in range(N_HEADS)})
        later = (lax.broadcasted_iota(jnp.int32, (tm, tm), 1) >= lax.broadcasted_iota(jnp.int32, (tm, tm), 0)).astype(F32)
        dls = _dot_f32(later, dc) + carry[...]
        carry[...] = dls[0:1, :]
        dzf = dls * jax.nn.sigmoid(-fl_ref[...])
        dfb_ref[...] += jnp.sum(dzf, axis=0, keepdims=True)
        data = lax.broadcasted_iota(jnp.int32, (tm, LANES), 1) < HEAD_DIM

        def compact(ref, scale=None):
            def rows(hd):
                return (ref[hd] if scale is None else ref[hd] * scale).astype(BF16)

            return [jnp.where(data, rows(2 * p), pltpu.roll(rows(2 * p + 1), HEAD_DIM, axis=1)) for p in range(N_HEADS // 2)]

        dz = jnp.concatenate([dzuv_ref[...], *compact(dqa_ref, HEAD_DIM ** -0.5), *compact(dka_ref), *compact(dva_ref),
                              dzf.astype(BF16)], axis=-1)
        dz_ref[...] = dz
        dh1 = _dot_nt(dz, w_ref[...])
        r, n = _rms(x_ref[...])
        dg1_ref[...] += jnp.sum(dh1 * n, axis=0, keepdims=True)
        gx_ref[...] = dx2_ref[...] + _rms_bwd(dh1, n, r, g_ref[...])

    rv = functools.partial(_rows, tm, rev_nt=nt)
    heads = pl.BlockSpec((N_HEADS, tm, LANES), lambda i: (0, nt - 1 - i, 0))
    return pl.pallas_call(
        body, name="bwd_in", grid=(nt,),
        in_specs=[rv(2 * D_A), heads, heads, heads, rv(LANES), rv(D_MODEL), rv(D_MODEL),
                  _full((D_MODEL, D_IN_PAD), True), _full((1, D_MODEL))],
        out_specs=[rv(D_MODEL), rv(D_IN_PAD), _full((1, D_MODEL)), _full((1, LANES))],
        out_shape=[jax.ShapeDtypeStruct((T, D_MODEL), F32), jax.ShapeDtypeStruct((T, D_IN_PAD), BF16),
                   jax.ShapeDtypeStruct((1, D_MODEL), F32), jax.ShapeDtypeStruct((1, LANES), F32)],
        scratch_shapes=[pltpu.VMEM((1, LANES), F32)],
        compiler_params=_cp("arbitrary"),
    )(dzuv, dqa, dka, dva, fl, x, dx2, w_in_p, g1)


def _matmul_tn(a_parts, b, tmm, tn, tk, name, shard_cols=None, n_valid=None):
    T = b.shape[0]
    widths = [a.shape[1] for a in a_parts]
    M, N = sum(widths), b.shape[1]
    tk = min(tk, T)
    nk = T // tk
    part_w = tmm // len(a_parts)
    n_valid = N if n_valid is None else n_valid

    def body(*refs):
        a_refs, b_ref, o_ref, obf_ref = refs[:len(a_parts)], refs[-3], refs[-2], refs[-1]
        k = pl.program_id(2)

        @pl.when(k == 0)
        def _():
            o_ref[...] = jnp.zeros_like(o_ref)

        a = [r[...].astype(BF16) for r in a_refs]
        o_ref[...] += _dot_tn(a[0] if len(a) == 1 else jnp.concatenate(a, axis=-1), b_ref[...].astype(BF16))

        @pl.when(k == nk - 1)
        def _():
            if shard_cols is None:
                obf_ref[...] = o_ref[...].astype(BF16)
            else:
                for d in range(min(tn, n_valid) // shard_cols):
                    obf_ref[d] = o_ref[:, d * shard_cols:(d + 1) * shard_cols].astype(BF16)

    if shard_cols is None:
        bf_spec, bf_shape = pl.BlockSpec((tmm, tn), lambda i, j, k: (i, j)), (M, N)
    else:
        per_tile = min(tn, n_valid) // shard_cols
        bf_spec, bf_shape = pl.BlockSpec((per_tile, tmm, shard_cols), lambda i, j, k: (j, i, 0)), (N_DEV, M, shard_cols)
    a_specs = [pl.BlockSpec((tk, part_w), lambda i, j, k: (k, i)) for _ in a_parts]
    return pl.pallas_call(
        body, name=name, grid=(M // tmm, N // tn, nk),
        in_specs=[*a_specs, pl.BlockSpec((tk, tn), lambda i, j, k: (k, j))],
        out_specs=[pl.BlockSpec((tmm, tn), lambda i, j, k: (i, j)), bf_spec],
        out_shape=[jax.ShapeDtypeStruct((M, N), F32), jax.ShapeDtypeStruct(bf_shape, BF16)],
        compiler_params=_cp("arbitrary", "arbitrary", "arbitrary"),
    )(*a_parts, b)


def _dw_in_half(h1, dz, block, side, name):
    T = dz.shape[0]
    tk = min(TK_DW, T)
    nk = T // tk
    rows, shard, n = D_MODEL // 2, D_IN // N_DEV, side.n

    def body(a_ref, b_ref, *rest):
        ins, obf_ref, outs = rest[:n], rest[n], rest[n + 1:2 * n + 1]
        acc, *sems = rest[2 * n + 1:]
        k = pl.program_id(0)

        @pl.when(k == 0)
        def _():
            acc[...] = jnp.zeros_like(acc)
            if n:
                side.start(ins, outs, sems)

        acc[...] += _dot_tn(a_ref[...], b_ref[...])

        @pl.when(k == nk - 1)
        def _():
            for d in range(N_DEV):
                obf_ref[d] = acc[:, d * shard:(d + 1) * shard].astype(BF16)
            if n:
                side.finish(ins, outs, sems)

    res = pl.pallas_call(
        body, name=name, grid=(nk,),
        in_specs=[pl.BlockSpec((tk, rows), lambda k: (k, block)), pl.BlockSpec((tk, D_IN_PAD), lambda k: (k, 0)), *side.specs],
        out_specs=[pl.BlockSpec((N_DEV, rows, shard), lambda k: (0, 0, 0)), *side.specs],
        out_shape=[jax.ShapeDtypeStruct((N_DEV, rows, shard), BF16), *side.out_shape],
        scratch_shapes=[pltpu.VMEM((rows, D_IN_PAD), F32), *side.scratch],
        compiler_params=_cp("arbitrary"),
    )(h1, dz, *side.operands)
    return res[0], res[1:]


class _Exchange:
    def __init__(self, gather, scatter, relay):
        self.n_g, self.n, self.relay = len(gather), len(gather) + len(scatter), relay
        self.operands = [*gather, *scatter]
        self.out_shape = [jax.ShapeDtypeStruct((N_DEV, *g.shape), g.dtype) for g in gather]
        self.out_shape += [jax.ShapeDtypeStruct(s.shape, s.dtype) for s in scatter]
        self.specs = [ANY] * self.n
        n = self.n
        self.scratch = [pltpu.SemaphoreType.DMA((7 * n,)), pltpu.SemaphoreType.DMA((7 * n,)),
                        pltpu.SemaphoreType.DMA((n,))] if n else []

    def _plan(self, ins, outs, sems):
        send_sems, recv_sems, local_sems = sems
        x, y, c = (lax.axis_index(ax) for ax in MESH_AXES)
        me = 4 * x + 2 * y + c
        sibling = (x, y, 1 - c)
        chips = [(1 - x, y), (x, 1 - y), (1 - x, 1 - y)]
        peers = [sibling] + [(*chip, c) for chip in chips] + [(*chip, 1 - c) for chip in chips]

        def index(dev):
            return 4 * dev[0] + 2 * dev[1] + dev[2]

        def remote(k, src, dst, to):
            return pltpu.make_async_remote_copy(src_ref=src, dst_ref=dst, send_sem=send_sems.at[k], recv_sem=recv_sems.at[k],
                                                device_id=to, device_id_type=pl.DeviceIdType.MESH)

        local, sends, relays, recvs = [], [], [], []
        for a in range(self.n):
            src, out, base = ins[a], outs[a], 7 * a
            if a >= self.n_g:
                local.append(pltpu.make_async_copy(src.at[me], out.at[me], local_sems.at[a]))
                sends += [remote(base + k, src.at[index(peer)], out.at[me], peer) for k, peer in enumerate(peers)]
            else:
                local.append(pltpu.make_async_copy(src, out.at[me], local_sems.at[a]))
                sends += [remote(base + k, src, out.at[me], peer) for k, peer in enumerate(peers[:4 if self.relay else 7])]
            for k, peer in enumerate(peers):
                slot = out.at[index(peer)]
                if a < self.n_g and self.relay and k >= 4:
                    continue
                recv = remote(base + k, slot, slot, peer)
                if a < self.n_g and self.relay and k >= 1:
                    relays.append((recv, remote(base + 3 + k, slot, slot, sibling)))
                else:
                    recvs.append(recv)
            if a < self.n_g and self.relay:
                for j, chip in enumerate(chips):
                    slot = out.at[index((*chip, 1 - c))]
                    recvs.append(remote(base + 4 + j, slot, slot, sibling))
        return local, sends, relays, recvs

    def start(self, ins, outs, sems):
        local, sends, _, _ = self._plan(ins, outs, sems)
        for cp in local + sends:
            cp.start()

    def finish(self, ins, outs, sems):
        local, sends, relays, recvs = self._plan(ins, outs, sems)
        for recv, fwd in relays:
            recv.wait_recv()
            fwd.start()
        for recv in recvs:
            recv.wait_recv()
        for cp in sends + [fwd for _, fwd in relays]:
            cp.wait_send()
        for cp in local:
            cp.wait()


def _exchange(gather, scatter, name):
    ex = _Exchange(gather, scatter, relay=True)
    n = ex.n

    def body(*refs):
        ins, outs, sems = refs[:n], refs[n:2 * n], refs[2 * n:]
        ex.start(ins, outs, sems)
        ex.finish(ins, outs, sems)

    return pl.pallas_call(body, name=name, in_specs=ex.specs, out_specs=ex.specs, out_shape=ex.out_shape,
                          scratch_shapes=ex.scratch)(*ex.operands)


def _adamw(w, g, m, v):
    m = ADAM_B1 * m + (1.0 - ADAM_B1) * g
    v = ADAM_B2 * v + (1.0 - ADAM_B2) * jnp.square(g)
    m_hat = m / (1.0 - ADAM_B1 ** ADAM_STEP)
    v_hat = v / (1.0 - ADAM_B2 ** ADAM_STEP)
    delta = -ADAM_LR * (m_hat / (jnp.sqrt(v_hat) + ADAM_EPS) + ADAM_WD * w)
    return delta, m, v


def _adamw_shard(w, m, v, recv, tr, name):
    _, R, C = w.shape

    def body(w_ref, m_ref, v_ref, recv_ref, g_ref, d_ref, nm_ref, nv_ref):
        g = recv_ref[0].astype(F32)
        for d in range(1, N_DEV):
            g = g + recv_ref[d].astype(F32)
        g_ref[...] = g
        d_ref[...], nm_ref[...], nv_ref[...] = _adamw(w_ref[...], g, m_ref[...], v_ref[...])

    blk = pl.BlockSpec((None, tr, C), lambda i: (0, i, 0))
    return pl.pallas_call(
        body, name=name, grid=(R // tr,),
        in_specs=[blk, blk, blk, pl.BlockSpec((N_DEV, tr, C), lambda i: (0, i, 0))],
        out_specs=[blk] * 4, out_shape=[jax.ShapeDtypeStruct((1, R, C), F32)] * 4,
        compiler_params=_cp("arbitrary"),
    )(w, m, v, recv)


def _adamw_small(params, gathered, loss_parts):
    n = len(params)

    def body(*refs):
        ins, gs, loss_ref, outs = refs[:3 * n], refs[3 * n:4 * n], refs[4 * n], refs[4 * n + 1:]
        for p in range(n):
            w_ref, m_ref, v_ref = ins[3 * p:3 * p + 3]
            g = gs[p][0]
            for d in range(1, N_DEV):
                g = g + gs[p][d]
            g = g[..., :w_ref.shape[-1]]
            g_ref, d_ref, nm_ref, nv_ref = outs[4 * p:4 * p + 4]
            g_ref[...] = g
            d_ref[...], nm_ref[...], nv_ref[...] = _adamw(w_ref[...], g, m_ref[...], v_ref[...])
        total = loss_ref[0]
        for d in range(1, N_DEV):
            total = total + loss_ref[d]
        outs[4 * n][...] = total

    out_shape = [jax.ShapeDtypeStruct(w.shape, F32) for w, _, _ in params for _ in range(4)]
    res = pl.pallas_call(body, name="adamw_small", out_shape=[*out_shape, jax.ShapeDtypeStruct((8, LANES), F32)],
                         compiler_params=pltpu.CompilerParams(vmem_limit_bytes=VMEM_LIMIT))(
        *[t for p in params for t in p], *gathered, loss_parts)
    return [res[4 * p:4 * p + 4] for p in range(n)], res[4 * n][0, 0]


def _col_shards(g):
    return jnp.transpose(g.reshape(g.shape[0], N_DEV, -1), (1, 0, 2))


def _row_shards(g):
    return g.reshape(N_DEV, -1, g.shape[1])


def _cols_whole(g):
    return jnp.transpose(g, (1, 0, 2)).reshape(g.shape[1], -1)


def _join_cols(g, width, name):
    n_shards, rows, c = g.shape
    tr = min(256, rows)

    def body(g_ref, o_ref):
        for d in range(n_shards):
            o_ref[:, d * c:(d + 1) * c] = g_ref[d]
        if width > n_shards * c:
            o_ref[:, n_shards * c:] = jnp.zeros((tr, width - n_shards * c), o_ref.dtype)

    return pl.pallas_call(
        body, name=name, grid=(rows // tr,),
        in_specs=[pl.BlockSpec((n_shards, tr, c), lambda i: (0, i, 0))],
        out_specs=pl.BlockSpec((tr, width), lambda i: (i, 0)),
        out_shape=jax.ShapeDtypeStruct((rows, width), g.dtype),
        compiler_params=_cp("arbitrary"),
    )(g)


def kernel(x, norm_mix_g, w_in, f_bias, sg_ln_g, sg_w, sg_b, w_out, norm_ffn_g, w_up, w_conv, b_conv, w_down, norm_final_g, loss_target, m_norm_mix_g, m_w_in, m_f_bias, m_sg_ln_g, m_sg_w, m_sg_b, m_w_out, m_norm_ffn_g, m_w_up, m_w_conv, m_b_conv, m_w_down, m_norm_final_g, v_norm_mix_g, v_w_in, v_f_bias, v_sg_ln_g, v_sg_w, v_sg_b, v_w_out, v_norm_ffn_g, v_w_up, v_w_conv, v_b_conv, v_w_down, v_norm_final_g):
    xs, tgt = x[0], loss_target[0]
    g1, g2, g3 = norm_mix_g, norm_ffn_g, norm_final_g.reshape(1, D_MODEL)
    lg = sg_ln_g.reshape(1, D_A)
    fb = jnp.pad(f_bias, ((0, 0), (0, LANES - N_HEADS)))
    pos_chunk = np.arange(SG_BLOCK) // SG_CHUNK
    maskf = jnp.asarray(pos_chunk[:, None] >= pos_chunk[None, :], F32)
    wm = (sg_w[0] * maskf[None]).astype(BF16)
    wmt = jnp.swapaxes(wm, 1, 2)
    bfull = jnp.repeat(sg_b[0].T, HEAD_DIM, axis=1)
    same_head = np.arange(D_A)[:, None] // HEAD_DIM == np.arange(D_A)[None, :] // HEAD_DIM
    hsel = jnp.asarray(np.arange(N_HEADS)[:, None] == np.arange(D_A)[None, :] // HEAD_DIM, F32)
    avg = jnp.asarray(same_head * (1.0 / HEAD_DIM), BF16)

    (win_g,) = _exchange([w_in[0].astype(BF16)], [], "gather_w_in")
    w_in_p = _join_cols(win_g, D_IN_PAD, "join_w_in")
    zuv, qa, ka, va, fl, h1 = _fwd_in(xs, g1, w_in_p, fb)
    oa = _gmlp_fwd(zuv, lg, wm, bfull, avg)
    rest = _Exchange([w_out[0].astype(BF16), w_up[0].astype(BF16), w_down[0].astype(BF16), w_conv[0]], [], relay=False)
    ob, lse, (wout_g, wup_g, wdown_g, wc_g) = _fox_fwd(qa, ka, va, rest)
    w_out_f, w_up_f = wout_g.reshape(D_MODEL, D_MODEL), _join_cols(wup_g, D_FF2, "join_w_up")
    w_down_f, wc_f = wdown_g.reshape(D_FF, D_MODEL), _cols_whole(wc_g)

    x2, h2, a = _fwd_mid(xs, oa, ob, w_out_f, g2, w_up_f)
    ac, yff, dx3, loss, dg3 = _fwd_ffn(a, x2, wc_f, b_conv, w_down_f, g3, tgt)
    da_g, da_v, dwc_g, dwc_v, dbc_g, dbc_v, dwdown_bf, dwup_bf = _bwd_ffn(dx3, a, ac, yff, h2, w_down_f, wc_f)
    dwc, dbc = jnp.concatenate([dwc_g, dwc_v], axis=1), jnp.concatenate([dbc_g, dbc_v], axis=1)
    dx2, doa, dob, dg2 = _bwd_mid(da_g, da_v, w_up_f, x2, g2, dx3, w_out_f, ob)
    dzuv, dwm, dsgb, dlg = _gmlp_bwd(zuv, doa, lg, wm, wmt, bfull, maskf, hsel, avg)
    _, dwout_bf = _matmul_tn([oa, ob], dx2, D_MODEL, D_MODEL, TK_DW, "dw_out")

    early = ("w_out", "w_up", "wc", "w_down")
    wire = [_row_shards(dwout_bf), dwup_bf, _col_shards(dwc).astype(BF16), dwdown_bf]
    small_early = dict(lg=dlg, sg_w=dwm, sg_b=dsgb, g2=dg2, bc=dbc, g3=dg3)
    grads = _Exchange([*small_early.values(), loss], wire, relay=False)
    dqa, dka, dva, got = _fox_bwd(qa, lse, dob, ka, va, grads)
    n_small = len(small_early)
    gathered, loss_parts = dict(zip(small_early, got[:n_small])), got[n_small]
    recv = dict(zip(early, got[n_small + 1:]))

    gx, dz, dg1, dfb = _bwd_in(dzuv, dqa, dka, dva, fl, xs, dx2, w_in_p, g1)
    wire_a, _ = _dw_in_half(h1, dz, 0, _Exchange([], [], relay=False), "dw_in_a")
    wire_b, (recv_a,) = _dw_in_half(h1, dz, 1, _Exchange([], [wire_a], relay=False), "dw_in_b")
    gathered["g1"], gathered["fb"], recv_b = _exchange([dg1, dfb], [wire_b], "exchange_w_in")
    recv["w_in"] = jnp.concatenate([recv_a, recv_b], axis=1)

    weights = dict(w_in=(w_in, m_w_in, v_w_in, 256), w_out=(w_out, m_w_out, v_w_out, 128), w_up=(w_up, m_w_up, v_w_up, 256),
                   wc=(w_conv, m_w_conv, v_w_conv, 3), w_down=(w_down, m_w_down, v_w_down, 176))
    res = {n: _adamw_shard(w, m, v, recv[n], tr, "adamw_" + n) for n, (w, m, v, tr) in weights.items()}

    reps = dict(g1=((norm_mix_g, m_norm_mix_g, v_norm_mix_g), (1, D_MODEL)), fb=((f_bias, m_f_bias, v_f_bias), (1, N_HEADS)),
                lg=((sg_ln_g, m_sg_ln_g, v_sg_ln_g), (1, D_A)), sg_w=((sg_w, m_sg_w, v_sg_w), (N_HEADS, SG_BLOCK, SG_BLOCK)),
                sg_b=((sg_b, m_sg_b, v_sg_b), (N_HEADS, SG_BLOCK)), g2=((norm_ffn_g, m_norm_ffn_g, v_norm_ffn_g), (1, D_MODEL)),
                bc=((b_conv, m_b_conv, v_b_conv), (1, D_FF2)), g3=((norm_final_g, m_norm_final_g, v_norm_final_g), (1, D_MODEL)))
    outs, loss_sum = _adamw_small([tuple(t.reshape(shape) for t in wmv) for wmv, shape in reps.values()],
                                  [gathered[n] for n in reps], loss_parts)
    for (n, (wmv, _)), out in zip(reps.items(), outs):
        res[n] = [o.reshape(wmv[0].shape) for o in out]

    names = ("g1", "w_in", "fb", "lg", "sg_w", "sg_b", "w_out", "g2", "w_up", "wc", "bc", "w_down", "g3")
    return (loss_sum, gx[None], *[res[n][0] for n in names], *[res[n][1] for n in names],
            *[res[n][2] for n in names], *[res[n][3] for n in names])
```

```python
import functools
import math

import jax
import jax.numpy as jnp
import numpy as np
from jax import lax
from jax.experimental import pallas as pl
from jax.experimental.pallas import tpu as pltpu

F32 = jnp.float32
BF16 = jnp.bfloat16

D_MODEL = 1024
HEAD_DIM = 64
N_HEADS = 8
D_A = 512
D_B = 512
D_IN = 2 * D_A + 3 * D_B + N_HEADS
D_IN_PAD = 2688
D_FF = 2816
D_FF2 = 2 * D_FF
SG_BLOCK = 128
SG_CHUNK = 64
EPS = 1e-6
N_DEV = 8
LANES = 128
NEG = -1e30
VMEM_LIMIT = 56 * 1024 * 1024

ADAM_LR = 0.001
ADAM_B1 = 0.9
ADAM_B2 = 0.999
ADAM_EPS = 1e-08
ADAM_WD = 0.01
ADAM_STEP = 10

TM = 256
TM_MID = 512
TM_GMLP = 1024
TK_DW = 2048
TQ = 512
FWD_UNROLL = 4
BWD_UNROLL = 2
CW = 256

MESH_AXES = ("x", "y", "c")
ANY = pl.BlockSpec(memory_space=pl.ANY)


def _cp(*sem):
    return pltpu.CompilerParams(dimension_semantics=sem, vmem_limit_bytes=VMEM_LIMIT)


def _dot(a, b):
    return jnp.dot(a, b, preferred_element_type=F32)


def _dot_nt(a, b):
    return lax.dot_general(a, b, (((1,), (1,)), ((), ())), preferred_element_type=F32)


def _dot_tn(a, b):
    return lax.dot_general(a, b, (((0,), (0,)), ((), ())), preferred_element_type=F32)


def _dot_f32(a, b):
    return jnp.dot(a, b, precision=lax.Precision.HIGHEST, preferred_element_type=F32)


def _gelu(z):
    return 0.5 * z * (1.0 + lax.erf(z * (1.0 / math.sqrt(2.0))))


def _gelu_grad(z):
    return 0.5 * (1.0 + lax.erf(z * (1.0 / math.sqrt(2.0)))) + z * jnp.exp(-0.5 * z * z) * (1.0 / math.sqrt(2.0 * math.pi))


def _log_sigmoid(x):
    return jnp.minimum(x, 0.0) - jnp.log1p(jnp.exp(-jnp.abs(x)))


def _rms(x):
    r = lax.rsqrt(jnp.mean(x * x, axis=-1, keepdims=True) + EPS)
    return r, x * r


def _rms_bwd(dy, n, r, g):
    dn = dy * g
    return r * (dn - n * jnp.mean(dn * n, axis=-1, keepdims=True))


def _full(shape, single=False):
    nd = len(shape)
    if single:
        return pl.BlockSpec(shape, lambda *_: (0,) * nd, pipeline_mode=pl.Buffered(1))
    return pl.BlockSpec(shape, lambda *_: (0,) * nd)


def _rows(tm, cols, rev_nt=None):
    if rev_nt is None:
        return pl.BlockSpec((tm, cols), lambda i: (i, 0))
    return pl.BlockSpec((tm, cols), lambda i: (rev_nt - 1 - i, 0))


def _head_sl(h):
    return slice(HEAD_DIM * h, HEAD_DIM * (h + 1))


L_ROW = HEAD_DIM
L_COL = HEAD_DIM + 3
L_LSE = HEAD_DIM + 6


def _split3(x):
    hi = x.astype(BF16).astype(F32)
    mid = (x - hi).astype(BF16).astype(F32)
    lo = (x - hi - mid).astype(BF16).astype(F32)
    return hi, mid, lo


def _bias_lane_placement():
    j, h = np.arange(3 * LANES) // LANES, np.arange(3 * LANES) % LANES
    cols = np.arange(N_HEADS * LANES)[None, :]
    return jnp.asarray((h[:, None] < N_HEADS) & (cols == LANES * h[:, None] + L_ROW + j[:, None]), BF16)


def _lanes(rows, width, parts):
    lane = lax.broadcasted_iota(jnp.int32, (rows, width), 1)
    out = jnp.zeros((rows, width), F32)
    for at, val in parts.items():
        out = jnp.where(lane == at, val, out)
    return out


def _fwd_in(x, g1, w_in_p, fb):
    T = x.shape[0]
    tm = min(TM, T)

    def body(x_ref, g_ref, w_ref, fb_ref, place_ref, zuv_ref, qa_ref, ka_ref, va_ref, fl_ref, h1_ref, carry):
        @pl.when(pl.program_id(0) == 0)
        def _():
            carry[...] = jnp.zeros_like(carry)

        r, n = _rms(x_ref[...])
        h = (n * g_ref[...]).astype(BF16)
        h1_ref[...] = h
        z = _dot(h, w_ref[...])
        zuv_ref[...] = z[:, :2 * D_A]
        o = 2 * D_A
        fl = z[:, o + 3 * D_B:] + fb_ref[...]
        fl_ref[...] = fl
        tri = (lax.broadcasted_iota(jnp.int32, (tm, tm), 0) >= lax.broadcasted_iota(jnp.int32, (tm, tm), 1)).astype(F32)
        c = _dot_f32(tri, _log_sigmoid(fl)) + carry[...]
        carry[...] = c[tm - 1:tm, :]
        hi, mid, lo = _split3(c)
        parts = jnp.concatenate([hi.astype(BF16), mid.astype(BF16), lo.astype(BF16)], axis=-1)
        placed = _dot(parts, place_ref[...])
        lane = lax.broadcasted_iota(jnp.int32, (tm, LANES), 1)
        data = lane < HEAD_DIM
        ones_q = ((lane >= L_COL) & (lane < L_COL + 3)).astype(F32)
        ones_k = (((lane >= L_ROW) & (lane < L_ROW + 3)) | ((lane >= L_LSE) & (lane < L_LSE + 3))).astype(F32)
        ones_v = ((lane >= L_ROW) & (lane < L_ROW + 3)).astype(F32)
        for hd in range(N_HEADS):
            def rows_of(first_col):
                tile = z[:, first_col + LANES * (hd // 2):first_col + LANES * (hd // 2 + 1)]
                return tile if hd % 2 == 0 else pltpu.roll(tile, HEAD_DIM, axis=1)

            hs = slice(LANES * hd, LANES * (hd + 1))
            qa_ref[hd] = jnp.where(data, rows_of(o) * (HEAD_DIM ** -0.5), placed[:, hs] + ones_q).astype(BF16)
            key_side = pltpu.roll(placed[:, hs], L_COL - L_ROW, axis=1)
            ka_ref[hd] = jnp.where(data, rows_of(o + D_B), ones_k - key_side).astype(BF16)
            va_ref[hd] = jnp.where(data, rows_of(o + 2 * D_B), ones_v).astype(BF16)

    heads = pl.BlockSpec((N_HEADS, tm, LANES), lambda i: (0, i, 0))
    aug = jax.ShapeDtypeStruct((N_HEADS, T, LANES), BF16)
    place = _bias_lane_placement()
    return pl.pallas_call(
        body, name="fwd_in", grid=(T // tm,),
        in_specs=[_rows(tm, D_MODEL), _full((1, D_MODEL)), _full((D_MODEL, D_IN_PAD), True), _full((1, LANES)),
                  _full((3 * LANES, N_HEADS * LANES))],
        out_specs=[_rows(tm, 2 * D_A), heads, heads, heads, _rows(tm, LANES), _rows(tm, D_MODEL)],
        out_shape=[jax.ShapeDtypeStruct((T, 2 * D_A), F32), aug, aug, aug, jax.ShapeDtypeStruct((T, LANES), F32),
                   jax.ShapeDtypeStruct((T, D_MODEL), BF16)],
        scratch_shapes=[pltpu.VMEM((1, LANES), F32)],
        compiler_params=_cp("arbitrary"),
    )(x, g1, w_in_p, fb, place)


def _segmean(x, avg_ref, parts):
    out, rest = None, x
    for _ in range(parts):
        piece = rest.astype(BF16)
        term = _dot(piece, avg_ref[...])
        out = term if out is None else out + term
        rest = rest - piece.astype(F32)
    return out


def _layer_norm(v, lg, avg_ref):
    d = v - _segmean(v, avg_ref, 3)
    rstd = lax.rsqrt(_segmean(d * d, avg_ref, 2) + EPS)
    vhat = d * rstd
    return vhat, rstd, (vhat * lg).astype(BF16)


def _mix_heads(w_ref, x, row_slice):
    low = lax.broadcasted_iota(jnp.int32, (SG_BLOCK, LANES), 1) < HEAD_DIM
    tiles = []
    for p in range(N_HEADS // 2):
        xt = x[row_slice, LANES * p:LANES * (p + 1)]
        zero = jnp.zeros_like(xt)
        tiles.append(_dot(w_ref[2 * p], jnp.where(low, xt, zero)) + _dot(w_ref[2 * p + 1], jnp.where(low, zero, xt)))
    return jnp.concatenate(tiles, axis=-1)


def _gmlp_fwd(zuv, lg, wm, bfull, avg):
    T = zuv.shape[0]
    tg = min(TM_GMLP, T)
    nb = tg // SG_BLOCK

    def body(zuv_ref, lg_ref, wm_ref, b_ref, avg_ref, oa_ref):
        u = _gelu(zuv_ref[:, :D_A])
        _, _, vn = _layer_norm(_gelu(zuv_ref[:, D_A:]), lg_ref[...], avg_ref)
        for n in range(nb):
            rs = slice(n * SG_BLOCK, (n + 1) * SG_BLOCK)
            oa_ref[rs, :] = (u[rs] * (_mix_heads(wm_ref, vn, rs) + b_ref[...])).astype(BF16)

    return pl.pallas_call(
        body, name="gmlp_fwd", grid=(T // tg,),
        in_specs=[_rows(tg, 2 * D_A), _full((1, D_A)), _full((N_HEADS, SG_BLOCK, SG_BLOCK)), _full((SG_BLOCK, D_A)),
                  _full((D_A, D_A))],
        out_specs=_rows(tg, D_A),
        out_shape=jax.ShapeDtypeStruct((T, D_A), BF16),
        compiler_params=_cp("arbitrary"),
    )(zuv, lg, wm, bfull, avg)


def _gmlp_bwd(zuv, doa, lg, wm, wmt, bfull, maskf, hsel, avg):
    T = zuv.shape[0]
    tg = min(TM_GMLP, T)
    nb = tg // SG_BLOCK
    nt = T // tg

    def body(zuv_ref, doa_ref, lg_ref, wm_ref, wmt_ref, b_ref, mask_ref, hsel_ref, avg_ref,
             dzuv_ref, dwm_ref, dsgb_ref, dlg_ref, dbacc):
        i = pl.program_id(0)

        @pl.when(i == 0)
        def _():
            dwm_ref[...] = jnp.zeros_like(dwm_ref)
            dlg_ref[...] = jnp.zeros_like(dlg_ref)
            dbacc[...] = jnp.zeros_like(dbacc)

        zu, zv = zuv_ref[:, :D_A], zuv_ref[:, D_A:]
        lgv = lg_ref[...]
        vhat, rstd, vn = _layer_norm(_gelu(zv), lgv, avg_ref)
        dmixed = doa_ref[...] * _gelu(zu)
        dmb = dmixed.astype(BF16)
        low = lax.broadcasted_iota(jnp.int32, (SG_BLOCK, LANES), 1) < HEAD_DIM
        mixed, dvn = [], []
        for n in range(nb):
            rs = slice(n * SG_BLOCK, (n + 1) * SG_BLOCK)
            mixed.append(_mix_heads(wm_ref, vn, rs) + b_ref[...])
            dvn.append(_mix_heads(wmt_ref, dmb, rs))
            dbacc[...] += dmixed[rs]
            for p in range(N_HEADS // 2):
                ls = slice(LANES * p, LANES * (p + 1))
                dmt, vnt = dmb[rs, ls], vn[rs, ls]
                zero = jnp.zeros_like(dmt)
                dwm_ref[2 * p] += _dot_nt(jnp.where(low, dmt, zero), vnt) * mask_ref[...]
                dwm_ref[2 * p + 1] += _dot_nt(jnp.where(low, zero, dmt), vnt) * mask_ref[...]
        mixed = jnp.concatenate(mixed, axis=0) if nb > 1 else mixed[0]
        dvn = jnp.concatenate(dvn, axis=0) if nb > 1 else dvn[0]
        dlg_ref[...] += jnp.sum(dvn * vhat, axis=0, keepdims=True)
        dvhat = dvn * lgv
        dv = rstd * (dvhat - _segmean(dvhat, avg_ref, 2) - vhat * _segmean(dvhat * vhat, avg_ref, 2))
        dzuv_ref[:, :D_A] = (doa_ref[...] * mixed * _gelu_grad(zu)).astype(BF16)
        dzuv_ref[:, D_A:] = (dv * _gelu_grad(zv)).astype(BF16)

        @pl.when(i == nt - 1)
        def _():
            dsgb_ref[...] = lax.dot_general(hsel_ref[...], dbacc[...], (((1,), (1,)), ((), ())),
                                            precision=lax.Precision.HIGHEST, preferred_element_type=F32)

    return pl.pallas_call(
        body, name="gmlp_bwd", grid=(nt,),
        in_specs=[_rows(tg, 2 * D_A), _rows(tg, D_A), _full((1, D_A)), _full((N_HEADS, SG_BLOCK, SG_BLOCK)),
                  _full((N_HEADS, SG_BLOCK, SG_BLOCK)), _full((SG_BLOCK, D_A)), _full((SG_BLOCK, SG_BLOCK)),
                  _full((N_HEADS, D_A)), _full((D_A, D_A))],
        out_specs=[_rows(tg, 2 * D_A), _full((N_HEADS, SG_BLOCK, SG_BLOCK)), _full((N_HEADS, SG_BLOCK)), _full((1, D_A))],
        out_shape=[jax.ShapeDtypeStruct((T, 2 * D_A), BF16), jax.ShapeDtypeStruct((N_HEADS, SG_BLOCK, SG_BLOCK), F32),
                   jax.ShapeDtypeStruct((N_HEADS, SG_BLOCK), F32), jax.ShapeDtypeStruct((1, D_A), F32)],
        scratch_shapes=[pltpu.VMEM((SG_BLOCK, D_A), F32)],
        compiler_params=_cp("arbitrary"),
    )(zuv, doa, lg, wm, wmt, bfull, maskf, hsel, avg)


def _fox_fwd(qa, ka, va, side):
    T = qa.shape[1]
    tq = min(TQ, T)
    nq = T // tq
    n = side.n

    def body(qa_ref, ka_ref, va_ref, *rest):
        ins, (o_ref, lse_ref), outs = rest[:n], rest[n:n + 2], rest[n + 2:2 * n + 2]
        acc_ref, *sems = rest[2 * n + 2:]
        i = pl.program_id(1)
        if n:
            @pl.when((pl.program_id(0) == 0) & (i == 0))
            def _():
                side.start(ins, outs, sems)
        row = lax.broadcasted_iota(jnp.int32, (tq, tq), 0)
        col = lax.broadcasted_iota(jnp.int32, (tq, tq), 1)
        qs = [qa_ref[0], qa_ref[1]]

        def tiles(js, carry, diag):
            offs = [pl.multiple_of(j * tq, tq) for j in js]
            logits = [[_dot_nt(qs[hh], ka_ref[hh, pl.ds(off, tq), :]) for hh in range(2)] for off in offs]
            carry = list(carry)
            for off, per_head in zip(offs, logits):
                for hh, s in enumerate(per_head):
                    if diag:
                        s = jnp.where(col <= row, s, NEG)
                    m = carry[hh]
                    m_new = jnp.maximum(m, jnp.max(s, axis=-1, keepdims=True))
                    pr = jnp.exp(s - m_new)
                    acc_ref[hh] = jnp.exp(m - m_new) * acc_ref[hh] + _dot(pr.astype(BF16), va_ref[hh, pl.ds(off, tq), :])
                    carry[hh] = m_new
            return tuple(carry)

        acc_ref[...] = jnp.zeros_like(acc_ref)
        init = (jnp.full((tq, 1), NEG, F32),) * 2
        carry = lax.fori_loop(0, i // FWD_UNROLL, lambda t, cr: tiles([FWD_UNROLL * t + u for u in range(FWD_UNROLL)], cr, False), init)
        carry = lax.fori_loop(i - i % FWD_UNROLL, i, lambda j, cr: tiles([j], cr, False), carry)
        carry = tiles([i], carry, True)
        for hh in range(2):
            m, acc = carry[hh], acc_ref[hh]
            l = acc[:, L_ROW:L_ROW + 1]
            o_ref[:, _head_sl(hh)] = acc[:, :HEAD_DIM] / l
            hi, mid, lo = _split3(-(m + jnp.log(l)))
            lse_ref[hh] = _lanes(tq, LANES, {L_LSE: hi, L_LSE + 1: mid, L_LSE + 2: lo}).astype(BF16)
        if n:
            @pl.when((pl.program_id(0) == N_HEADS // 2 - 1) & (i == nq - 1))
            def _():
                side.finish(ins, outs, sems)

    tile = pl.BlockSpec((2, tq, LANES), lambda p, i: (p, i, 0))
    seq = pl.BlockSpec((2, T, LANES), lambda p, i: (p, 0, 0))
    res = pl.pallas_call(
        body, name="fox_fwd", grid=(N_HEADS // 2, nq),
        in_specs=[tile, seq, seq, *side.specs],
        out_specs=[pl.BlockSpec((tq, LANES), lambda p, i: (i, p)), tile, *side.specs],
        out_shape=[jax.ShapeDtypeStruct((T, D_B), F32), jax.ShapeDtypeStruct((N_HEADS, T, LANES), BF16), *side.out_shape],
        scratch_shapes=[pltpu.VMEM((2, tq, LANES), F32), *side.scratch],
        compiler_params=_cp("arbitrary", "arbitrary"),
    )(qa, ka, va, *side.operands)
    return res[0], res[1], res[2:]


def _fox_bwd(qa, lse, doa, ka, va, side):
    T = qa.shape[1]
    tq = min(TQ, T)
    nq = T // tq
    n = side.n

    def body(qa_ref, lse_ref, doa_ref, ka_ref, va_ref, *rest):
        ins, (dqa_ref, dka_ref, dva_ref), outs = rest[:n], rest[n:n + 3], rest[n + 3:2 * n + 3]
        dv_acc, *sems = rest[2 * n + 3:]
        j = pl.program_id(1)
        if n:
            @pl.when((pl.program_id(0) == 0) & (j == 0))
            def _():
                side.start(ins, outs, sems)

        @pl.when(j == 0)
        def _():
            dqa_ref[...] = jnp.zeros_like(dqa_ref)

        row = lax.broadcasted_iota(jnp.int32, (tq, tq), 0)
        col = lax.broadcasted_iota(jnp.int32, (tq, tq), 1)
        ks = [ka_ref[0], ka_ref[1]]
        vs = [va_ref[0], va_ref[1]]

        def tiles(ids, diag):
            work = []
            for i in ids:
                off = pl.multiple_of(i * tq, tq)
                for hh in range(2):
                    qi = qa_ref[hh, pl.ds(off, tq), :] + lse_ref[hh, pl.ds(off, tq), :]
                    doi = doa_ref[hh, pl.ds(off, tq), :]
                    work.append((off, hh, qi, doi, _dot_nt(ks[hh], qi), _dot_nt(vs[hh], doi)))
            for off, hh, qi, doi, st, dpt in work:
                if diag:
                    st = jnp.where(row <= col, st, NEG)
                pt = jnp.exp(st)
                dv_acc[hh] += _dot(pt.astype(BF16), doi)
                dsb = (pt * dpt).astype(BF16)
                dka_ref[hh] += _dot(dsb, qi)
                dqa_ref[hh, pl.ds(off, tq), :] += _dot_tn(dsb, ks[hh])

        dka_ref[...] = jnp.zeros_like(dka_ref)
        dv_acc[...] = jnp.zeros_like(dv_acc)
        tiles([j], True)
        todo = nq - 1 - j

        @pl.loop(0, todo // BWD_UNROLL)
        def _(t):
            tiles([j + 1 + BWD_UNROLL * t + u for u in range(BWD_UNROLL)], False)

        @pl.loop(nq - todo % BWD_UNROLL, nq)
        def _(i):
            tiles([i], False)

        dva_ref[...] = dv_acc[...].astype(BF16)
        if n:
            @pl.when((pl.program_id(0) == N_HEADS // 2 - 1) & (j == nq - 1))
            def _():
                side.finish(ins, outs, sems)

    tile = pl.BlockSpec((2, tq, LANES), lambda p, j: (p, j, 0))
    seq = pl.BlockSpec((2, T, LANES), lambda p, j: (p, 0, 0))
    res = pl.pallas_call(
        body, name="fox_bwd", grid=(N_HEADS // 2, nq),
        in_specs=[seq, seq, seq, tile, tile, *side.specs],
        out_specs=[seq, tile, tile, *side.specs],
        out_shape=[jax.ShapeDtypeStruct((N_HEADS, T, LANES), F32), jax.ShapeDtypeStruct((N_HEADS, T, LANES), F32),
                   jax.ShapeDtypeStruct((N_HEADS, T, LANES), BF16), *side.out_shape],
        scratch_shapes=[pltpu.VMEM((2, tq, LANES), F32), *side.scratch],
        compiler_params=_cp("arbitrary", "arbitrary"),
    )(qa, lse, doa, ka, va, *side.operands)
    return res[0], res[1], res[2], res[3:]


def _fwd_mid(x, oa, ob, w_out, g2, w_up):
    T = x.shape[0]
    tm = min(TM_MID, T)

    def body(x_ref, oa_ref, ob_ref, wo_ref, g_ref, wu_ref, x2_ref, h2_ref, a_ref):
        oab = jnp.concatenate([oa_ref[...], ob_ref[...].astype(BF16)], axis=-1)
        x2 = x_ref[...] + _dot(oab, wo_ref[...])
        x2_ref[...] = x2
        _, n = _rms(x2)
        h2 = (n * g_ref[...]).astype(BF16)
        h2_ref[...] = h2
        a_ref[...] = _dot(h2, wu_ref[...])

    return pl.pallas_call(
        body, name="fwd_mid", grid=(T // tm,),
        in_specs=[_rows(tm, D_MODEL), _rows(tm, D_A), _rows(tm, D_B), _full((D_MODEL, D_MODEL), True), _full((1, D_MODEL)),
                  _full((D_MODEL, D_FF2), True)],
        out_specs=[_rows(tm, D_MODEL), _rows(tm, D_MODEL), _rows(tm, D_FF2)],
        out_shape=[jax.ShapeDtypeStruct((T, D_MODEL), F32), jax.ShapeDtypeStruct((T, D_MODEL), BF16),
                   jax.ShapeDtypeStruct((T, D_FF2), F32)],
        compiler_params=_cp("arbitrary"),
    )(x, oa, ob, w_out, g2, w_up)


def _row_before(x, prev, k):
    rolled = pltpu.roll(x, k, axis=0)
    row = lax.broadcasted_iota(jnp.int32, (8, x.shape[1]), 0)
    head = rolled[0:8]
    for r in range(k):
        head = jnp.where(row == r, prev[8 - k + r:9 - k + r], head)
    return jnp.concatenate([head, rolled[8:]], axis=0)


def _row_after(x, nxt, k):
    tm = x.shape[0]
    rolled = pltpu.roll(x, tm - k, axis=0)
    row = lax.broadcasted_iota(jnp.int32, (8, x.shape[1]), 0)
    tail = rolled[tm - 8:tm]
    for r in range(k):
        tail = jnp.where(row == 8 - k + r, nxt[r:r + 1], tail)
    return jnp.concatenate([rolled[:tm - 8], tail], axis=0)


def _fwd_ffn(a, x2, wc, bc, w_down, g3, tgt):
    T = x2.shape[0]
    tm = min(TM, T)

    def body(a_ref, x2_ref, wc_ref, bc_ref, wd_ref, g_ref, tgt_ref, ac_ref, yff_ref, dx3_ref, loss_ref, dg3_ref, carry):
        @pl.when(pl.program_id(0) == 0)
        def _():
            carry[...] = jnp.zeros_like(carry)
            loss_ref[...] = jnp.zeros_like(loss_ref)
            dg3_ref[...] = jnp.zeros_like(dg3_ref)

        def conv(cs):
            a0 = a_ref[:, cs]
            prev = carry[:, cs]
            ac = (wc_ref[0:1, cs] * _row_before(a0, prev, 2) + wc_ref[1:2, cs] * _row_before(a0, prev, 1)
                  + wc_ref[2:3, cs] * a0 + bc_ref[:, cs])
            ac_ref[:, cs] = ac.astype(BF16)
            return ac

        x3 = x2_ref[...]
        for ci in range(D_FF // CW):
            gs = slice(ci * CW, (ci + 1) * CW)
            ag = conv(gs)
            av = conv(slice(D_FF + ci * CW, D_FF + (ci + 1) * CW))
            yb = (ag * jax.nn.sigmoid(ag) * av).astype(BF16)
            yff_ref[:, gs] = yb
            x3 = x3 + _dot(yb, wd_ref[gs, :])
        carry[...] = a_ref[tm - 8:tm, :]
        r, n = _rms(x3)
        g = g_ref[...]
        diff = n * g - tgt_ref[...]
        loss_ref[...] += (0.5 / D_MODEL) * jnp.sum(diff * diff)
        dout = diff * (1.0 / D_MODEL)
        dg3_ref[...] += jnp.sum(dout * n, axis=0, keepdims=True)
        dx3_ref[...] = _rms_bwd(dout, n, r, g)

    return pl.pallas_call(
        body, name="fwd_ffn", grid=(T // tm,),
        in_specs=[_rows(tm, D_FF2), _rows(tm, D_MODEL), _full((3, D_FF2)), _full((1, D_FF2)), _full((D_FF, D_MODEL), True),
                  _full((1, D_MODEL)), _rows(tm, D_MODEL)],
        out_specs=[_rows(tm, D_FF2), _rows(tm, D_FF), _rows(tm, D_MODEL), _full((8, LANES)), _full((1, D_MODEL))],
        out_shape=[jax.ShapeDtypeStruct((T, D_FF2), BF16), jax.ShapeDtypeStruct((T, D_FF), BF16),
                   jax.ShapeDtypeStruct((T, D_MODEL), F32), jax.ShapeDtypeStruct((8, LANES), F32),
                   jax.ShapeDtypeStruct((1, D_MODEL), F32)],
        scratch_shapes=[pltpu.VMEM((8, D_FF2), F32)],
        compiler_params=_cp("arbitrary"),
    )(a, x2, wc, bc, w_down, g3, tgt)


def _bwd_ffn(dx3, a, ac, yff, h2, w_down, wc):
    T = dx3.shape[0]
    tm = min(TM, T)
    nt = T // tm
    half = D_FF // 2
    shard_up, shard_down = D_FF2 // N_DEV, D_FF // N_DEV

    def body(dx3_ref, ag_ref, av_ref, acg_ref, acv_ref, yff_ref, h2_ref, wd_ref, wcg_ref, wcv_ref,
             dag_ref, dav_ref, dwcg_ref, dwcv_ref, dbcg_ref, dbcv_ref, dwd_ref, dwu_ref,
             nxt, shifted, acc_down, acc_g, acc_v, stage_up, stage_down, sem):
        c, r = pl.program_id(0), pl.program_id(1)

        @pl.when(r == 0)
        def _():
            for ref in (nxt, dwcg_ref, dwcv_ref, dbcg_ref, dbcv_ref, acc_down, acc_g, acc_v):
                ref[...] = jnp.zeros_like(ref)

        dxb = dx3_ref[...].astype(BF16)
        dy_all = _dot_nt(dxb, wd_ref[...])

        def back(a_ref, w_ref, da_ref, dwc_ref, dbc_ref, nx, cs, dac):
            a0 = a_ref[:, cs]
            shifted[0] = _row_after(dac, nxt[:, nx], 1)
            shifted[1] = _row_after(dac, nxt[:, nx], 2)
            dp1, dp2 = shifted[0], shifted[1]
            dbc_ref[:, cs] += jnp.sum(dac, axis=0, keepdims=True)
            dwc_ref[0:1, cs] += jnp.sum(dp2 * a0, axis=0, keepdims=True)
            dwc_ref[1:2, cs] += jnp.sum(dp1 * a0, axis=0, keepdims=True)
            dwc_ref[2:3, cs] += jnp.sum(dac * a0, axis=0, keepdims=True)
            da_ref[:, cs] = (w_ref[2:3, cs] * dac + w_ref[1:2, cs] * dp1 + w_ref[0:1, cs] * dp2).astype(BF16)
            nxt[:, nx] = dac[0:8]

        for ci in range(half // LANES):
            cs = slice(ci * LANES, (ci + 1) * LANES)
            dy = dy_all[:, cs]
            ag, av = acg_ref[:, cs].astype(F32), acv_ref[:, cs].astype(F32)
            sg = jax.nn.sigmoid(ag)
            back(av_ref, wcv_ref, dav_ref, dwcv_ref, dbcv_ref, slice(half + ci * LANES, half + (ci + 1) * LANES), cs,
                 dy * (ag * sg))
            back(ag_ref, wcg_ref, dag_ref, dwcg_ref, dbcg_ref, cs, cs, dy * av * (sg * (1.0 + ag * (1.0 - sg))))

        acc_down[...] += _dot_tn(yff_ref[...], dxb)
        h2 = h2_ref[...]
        acc_g[...] += _dot_tn(h2, dag_ref[...])
        acc_v[...] += _dot_tn(h2, dav_ref[...])

        @pl.when(r == nt - 1)
        def _():
            for s in range(half // shard_down):
                stage_down[...] = acc_down[s * shard_down:(s + 1) * shard_down, :].astype(BF16)
                out = pltpu.make_async_copy(stage_down, dwd_ref.at[(half // shard_down) * c + s], sem)
                out.start()
                out.wait()
            for acc, first in ((acc_g, 0), (acc_v, N_DEV // 2)):
                for s in range(half // shard_up):
                    stage_up[...] = acc[:, s * shard_up:(s + 1) * shard_up].astype(BF16)
                    out = pltpu.make_async_copy(stage_up, dwu_ref.at[first + (half // shard_up) * c + s], sem)
                    out.start()
                    out.wait()

    def cols(width, second_half):
        return pl.BlockSpec((tm, width), lambda c, r: (nt - 1 - r, c + (2 if second_half else 0)))

    def param(rows, second_half):
        return pl.BlockSpec((rows, half), lambda c, r: (0, c + (2 if second_half else 0)))

    tokens = pl.BlockSpec((tm, D_MODEL), lambda c, r: (nt - 1 - r, 0))
    return pl.pallas_call(
        body, name="bwd_ffn", grid=(2, nt),
        in_specs=[tokens, cols(half, False), cols(half, True), cols(half, False), cols(half, True), cols(half, False), tokens,
                  pl.BlockSpec((half, D_MODEL), lambda c, r: (c, 0), pipeline_mode=pl.Buffered(1)),
                  param(3, False), param(3, True)],
        out_specs=[cols(half, False), cols(half, False), param(3, False), param(3, False), param(1, False), param(1, False),
                   ANY, ANY],
        out_shape=[jax.ShapeDtypeStruct((T, D_FF), BF16), jax.ShapeDtypeStruct((T, D_FF), BF16),
                   jax.ShapeDtypeStruct((3, D_FF), F32), jax.ShapeDtypeStruct((3, D_FF), F32),
                   jax.ShapeDtypeStruct((1, D_FF), F32), jax.ShapeDtypeStruct((1, D_FF), F32),
                   jax.ShapeDtypeStruct((N_DEV, shard_down, D_MODEL), BF16), jax.ShapeDtypeStruct((N_DEV, D_MODEL, shard_up), BF16)],
        scratch_shapes=[pltpu.VMEM((8, D_FF), F32), pltpu.VMEM((2, tm, LANES), F32), pltpu.VMEM((half, D_MODEL), F32),
                        pltpu.VMEM((D_MODEL, half), F32), pltpu.VMEM((D_MODEL, half), F32),
                        pltpu.VMEM((D_MODEL, shard_up), BF16), pltpu.VMEM((shard_down, D_MODEL), BF16),
                        pltpu.SemaphoreType.DMA],
        compiler_params=_cp("arbitrary", "arbitrary"),
    )(dx3, a, a, ac, ac, yff, h2, w_down, wc, wc)


def _bwd_mid(da_g, da_v, w_up, x2, g2, dx3, w_out, ob):
    T = x2.shape[0]
    tm = min(TM, T)

    def body(dag_ref, dav_ref, wu_ref, x2_ref, g_ref, dx3_ref, wo_ref, ob_ref, hsum_ref, place_ref,
             dx2_ref, doa_ref, dob_ref, dg2_ref):
        @pl.when(pl.program_id(0) == 0)
        def _():
            dg2_ref[...] = jnp.zeros_like(dg2_ref)

        dh2 = _dot_nt(dag_ref[...], wu_ref[:, :D_FF]) + _dot_nt(dav_ref[...], wu_ref[:, D_FF:])
        r, n = _rms(x2_ref[...])
        dg2_ref[...] += jnp.sum(dh2 * n, axis=0, keepdims=True)
        dx2 = dx3_ref[...] + _rms_bwd(dh2, n, r, g_ref[...])
        dx2_ref[...] = dx2
        doab = _dot_nt(dx2.astype(BF16), wo_ref[...])
        doa_ref[...] = doab[:, :D_A]
        dob = doab[:, D_A:]
        rest, delta = dob.astype(BF16).astype(F32) * ob_ref[...], None
        for _ in range(3):
            piece = rest.astype(BF16)
            term = _dot(piece, hsum_ref[...])
            delta = term if delta is None else delta + term
            rest = rest - piece.astype(F32)
        hi, mid, lo = _split3(-delta)
        parts = jnp.concatenate([hi.astype(BF16), mid.astype(BF16), lo.astype(BF16)], axis=-1)
        placed = _dot(parts, place_ref[...])
        data = lax.broadcasted_iota(jnp.int32, (tm, LANES), 1) < HEAD_DIM
        for hd in range(N_HEADS):
            tile = dob[:, LANES * (hd // 2):LANES * (hd // 2 + 1)]
            rows = tile if hd % 2 == 0 else pltpu.roll(tile, HEAD_DIM, axis=1)
            dob_ref[hd] = jnp.where(data, rows, placed[:, LANES * hd:LANES * (hd + 1)]).astype(BF16)

    hsum = jnp.asarray(np.arange(D_B)[:, None] // HEAD_DIM == np.arange(LANES)[None, :], BF16)
    place = _bias_lane_placement()
    return pl.pallas_call(
        body, name="bwd_mid", grid=(T // tm,),
        in_specs=[_rows(tm, D_FF), _rows(tm, D_FF), _full((D_MODEL, D_FF2), True), _rows(tm, D_MODEL), _full((1, D_MODEL)),
                  _rows(tm, D_MODEL), _full((D_MODEL, D_MODEL), True), _rows(tm, D_B), _full((D_B, LANES)),
                  _full((3 * LANES, N_HEADS * LANES))],
        out_specs=[_rows(tm, D_MODEL), _rows(tm, D_A), pl.BlockSpec((N_HEADS, tm, LANES), lambda i: (0, i, 0)),
                   _full((1, D_MODEL))],
        out_shape=[jax.ShapeDtypeStruct((T, D_MODEL), F32), jax.ShapeDtypeStruct((T, D_A), F32),
                   jax.ShapeDtypeStruct((N_HEADS, T, LANES), BF16), jax.ShapeDtypeStruct((1, D_MODEL), F32)],
        compiler_params=_cp("arbitrary"),
    )(da_g, da_v, w_up, x2, g2, dx3, w_out, ob, hsum, place)


def _bwd_in(dzuv, dqa, dka, dva, fl, x, dx2, w_in_p, g1):
    T = x.shape[0]
    tm = min(TM, T)
    nt = T // tm

    def body(dzuv_ref, dqa_ref, dka_ref, dva_ref, fl_ref, x_ref, dx2_ref, w_ref, g_ref,
             gx_ref, dz_ref, dg1_ref, dfb_ref, carry):
        @pl.when(pl.program_id(0) == 0)
        def _():
            carry[...] = jnp.zeros_like(carry)
            dg1_ref[...] = jnp.zeros_like(dg1_ref)
            dfb_ref[...] = jnp.zeros_like(dfb_ref)

        dc = _lanes(tm, LANES, {hd: dqa_ref[hd][:, L_ROW:L_ROW + 1] - dka_ref[hd][:, L_COL:L_COL + 1] for hd in range(N_HEADS)})
        later = (lax.broadcasted_iota(jnp.int32, (tm, tm), 1) >= lax.broadcasted_iota(jnp.int32, (tm, tm), 0)).astype(F32)
        dls = _dot_f32(later, dc) + carry[...]
        carry[...] = dls[0:1, :]
        dzf = dls * jax.nn.sigmoid(-fl_ref[...])
        dfb_ref[...] += jnp.sum(dzf, axis=0, keepdims=True)
        data = lax.broadcasted_iota(jnp.int32, (tm, LANES), 1) < HEAD_DIM

        def compact(ref, scale=None):
            def rows(hd):
                return (ref[hd] if scale is None else ref[hd] * scale).astype(BF16)

            return [jnp.where(data, rows(2 * p), pltpu.roll(rows(2 * p + 1), HEAD_DIM, axis=1)) for p in range(N_HEADS // 2)]

        dz = jnp.concatenate([dzuv_ref[...], *compact(dqa_ref, HEAD_DIM ** -0.5), *compact(dka_ref), *compact(dva_ref),
                              dzf.astype(BF16)], axis=-1)
        dz_ref[...] = dz
        dh1 = _dot_nt(dz, w_ref[...])
        r, n = _rms(x_ref[...])
        dg1_ref[...] += jnp.sum(dh1 * n, axis=0, keepdims=True)
        gx_ref[...] = dx2_ref[...] + _rms_bwd(dh1, n, r, g_ref[...])

    rv = functools.partial(_rows, tm, rev_nt=nt)
    heads = pl.BlockSpec((N_HEADS, tm, LANES), lambda i: (0, nt - 1 - i, 0))
    return pl.pallas_call(
        body, name="bwd_in", grid=(nt,),
        in_specs=[rv(2 * D_A), heads, heads, heads, rv(LANES), rv(D_MODEL), rv(D_MODEL),
                  _full((D_MODEL, D_IN_PAD), True), _full((1, D_MODEL))],
        out_specs=[rv(D_MODEL), rv(D_IN_PAD), _full((1, D_MODEL)), _full((1, LANES))],
        out_shape=[jax.ShapeDtypeStruct((T, D_MODEL), F32), jax.ShapeDtypeStruct((T, D_IN_PAD), BF16),
                   jax.ShapeDtypeStruct((1, D_MODEL), F32), jax.ShapeDtypeStruct((1, LANES), F32)],
        scratch_shapes=[pltpu.VMEM((1, LANES), F32)],
        compiler_params=_cp("arbitrary"),
    )(dzuv, dqa, dka, dva, fl, x, dx2, w_in_p, g1)


def _matmul_tn(a_parts, b, tmm, tn, tk, name, shard_cols=None, n_valid=None):
    T = b.shape[0]
    widths = [a.shape[1] for a in a_parts]
    M, N = sum(widths), b.shape[1]
    tk = min(tk, T)
    nk = T // tk
    part_w = tmm // len(a_parts)
    n_valid = N if n_valid is None else n_valid

    def body(*refs):
        a_refs, b_ref, o_ref, obf_ref = refs[:len(a_parts)], refs[-3], refs[-2], refs[-1]
        k = pl.program_id(2)

        @pl.when(k == 0)
        def _():
            o_ref[...] = jnp.zeros_like(o_ref)

        a = [r[...].astype(BF16) for r in a_refs]
        o_ref[...] += _dot_tn(a[0] if len(a) == 1 else jnp.concatenate(a, axis=-1), b_ref[...].astype(BF16))

        @pl.when(k == nk - 1)
        def _():
            if shard_cols is None:
                obf_ref[...] = o_ref[...].astype(BF16)
            else:
                for d in range(min(tn, n_valid) // shard_cols):
                    obf_ref[d] = o_ref[:, d * shard_cols:(d + 1) * shard_cols].astype(BF16)

    if shard_cols is None:
        bf_spec, bf_shape = pl.BlockSpec((tmm, tn), lambda i, j, k: (i, j)), (M, N)
    else:
        per_tile = min(tn, n_valid) // shard_cols
        bf_spec, bf_shape = pl.BlockSpec((per_tile, tmm, shard_cols), lambda i, j, k: (j, i, 0)), (N_DEV, M, shard_cols)
    a_specs = [pl.BlockSpec((tk, part_w), lambda i, j, k: (k, i)) for _ in a_parts]
    return pl.pallas_call(
        body, name=name, grid=(M // tmm, N // tn, nk),
        in_specs=[*a_specs, pl.BlockSpec((tk, tn), lambda i, j, k: (k, j))],
        out_specs=[pl.BlockSpec((tmm, tn), lambda i, j, k: (i, j)), bf_spec],
        out_shape=[jax.ShapeDtypeStruct((M, N), F32), jax.ShapeDtypeStruct(bf_shape, BF16)],
        compiler_params=_cp("arbitrary", "arbitrary", "arbitrary"),
    )(*a_parts, b)


def _dw_in_half(h1, dz, block, side, name):
    T = dz.shape[0]
    tk = min(TK_DW, T)
    nk = T // tk
    rows, shard, n = D_MODEL // 2, D_IN // N_DEV, side.n

    def body(a_ref, b_ref, *rest):
        ins, obf_ref, outs = rest[:n], rest[n], rest[n + 1:2 * n + 1]
        acc, *sems = rest[2 * n + 1:]
        k = pl.program_id(0)

        @pl.when(k == 0)
        def _():
            acc[...] = jnp.zeros_like(acc)
            if n:
                side.start(ins, outs, sems)

        acc[...] += _dot_tn(a_ref[...], b_ref[...])

        @pl.when(k == nk - 1)
        def _():
            for d in range(N_DEV):
                obf_ref[d] = acc[:, d * shard:(d + 1) * shard].astype(BF16)
            if n:
                side.finish(ins, outs, sems)

    res = pl.pallas_call(
        body, name=name, grid=(nk,),
        in_specs=[pl.BlockSpec((tk, rows), lambda k: (k, block)), pl.BlockSpec((tk, D_IN_PAD), lambda k: (k, 0)), *side.specs],
        out_specs=[pl.BlockSpec((N_DEV, rows, shard), lambda k: (0, 0, 0)), *side.specs],
        out_shape=[jax.ShapeDtypeStruct((N_DEV, rows, shard), BF16), *side.out_shape],
        scratch_shapes=[pltpu.VMEM((rows, D_IN_PAD), F32), *side.scratch],
        compiler_params=_cp("arbitrary"),
    )(h1, dz, *side.operands)
    return res[0], res[1:]


class _Exchange:
    def __init__(self, gather, scatter, relay):
        self.n_g, self.n, self.relay = len(gather), len(gather) + len(scatter), relay
        self.operands = [*gather, *scatter]
        self.out_shape = [jax.ShapeDtypeStruct((N_DEV, *g.shape), g.dtype) for g in gather]
        self.out_shape += [jax.ShapeDtypeStruct(s.shape, s.dtype) for s in scatter]
        self.specs = [ANY] * self.n
        n = self.n
        self.scratch = [pltpu.SemaphoreType.DMA((7 * n,)), pltpu.SemaphoreType.DMA((7 * n,)),
                        pltpu.SemaphoreType.DMA((n,))] if n else []

    def _plan(self, ins, outs, sems):
        send_sems, recv_sems, local_sems = sems
        x, y, c = (lax.axis_index(ax) for ax in MESH_AXES)
        me = 4 * x + 2 * y + c
        sibling = (x, y, 1 - c)
        chips = [(1 - x, y), (x, 1 - y), (1 - x, 1 - y)]
        peers = [sibling] + [(*chip, c) for chip in chips] + [(*chip, 1 - c) for chip in chips]

        def index(dev):
            return 4 * dev[0] + 2 * dev[1] + dev[2]

        def remote(k, src, dst, to):
            return pltpu.make_async_remote_copy(src_ref=src, dst_ref=dst, send_sem=send_sems.at[k], recv_sem=recv_sems.at[k],
                                                device_id=to, device_id_type=pl.DeviceIdType.MESH)

        local, sends, relays, recvs = [], [], [], []
        for a in range(self.n):
            src, out, base = ins[a], outs[a], 7 * a
            if a >= self.n_g:
                local.append(pltpu.make_async_copy(src.at[me], out.at[me], local_sems.at[a]))
                sends += [remote(base + k, src.at[index(peer)], out.at[me], peer) for k, peer in enumerate(peers)]
            else:
                local.append(pltpu.make_async_copy(src, out.at[me], local_sems.at[a]))
                sends += [remote(base + k, src, out.at[me], peer) for k, peer in enumerate(peers[:4 if self.relay else 7])]
            for k, peer in enumerate(peers):
                slot = out.at[index(peer)]
                if a < self.n_g and self.relay and k >= 4:
                    continue
                recv = remote(base + k, slot, slot, peer)
                if a < self.n_g and self.relay and k >= 1:
                    relays.append((recv, remote(base + 3 + k, slot, slot, sibling)))
                else:
                    recvs.append(recv)
            if a < self.n_g and self.relay:
                for j, chip in enumerate(chips):
                    slot = out.at[index((*chip, 1 - c))]
                    recvs.append(remote(base + 4 + j, slot, slot, sibling))
        return local, sends, relays, recvs

    def start(self, ins, outs, sems):
        local, sends, _, _ = self._plan(ins, outs, sems)
        for cp in local + sends:
            cp.start()

    def finish(self, ins, outs, sems):
        local, sends, relays, recvs = self._plan(ins, outs, sems)
        for recv, fwd in relays:
            recv.wait_recv()
            fwd.start()
        for recv in recvs:
            recv.wait_recv()
        for cp in sends + [fwd for _, fwd in relays]:
            cp.wait_send()
        for cp in local:
            cp.wait()


def _exchange(gather, scatter, name, relay):
    ex = _Exchange(gather, scatter, relay=relay)
    n = ex.n

    def body(*refs):
        ins, outs, sems = refs[:n], refs[n:2 * n], refs[2 * n:]
        ex.start(ins, outs, sems)
        ex.finish(ins, outs, sems)

    return pl.pallas_call(body, name=name, in_specs=ex.specs, out_specs=ex.specs, out_shape=ex.out_shape,
                          scratch_shapes=ex.scratch)(*ex.operands)


def _adamw(w, g, m, v):
    m = ADAM_B1 * m + (1.0 - ADAM_B1) * g
    v = ADAM_B2 * v + (1.0 - ADAM_B2) * jnp.square(g)
    m_hat = m / (1.0 - ADAM_B1 ** ADAM_STEP)
    v_hat = v / (1.0 - ADAM_B2 ** ADAM_STEP)
    delta = -ADAM_LR * (m_hat / (jnp.sqrt(v_hat) + ADAM_EPS) + ADAM_WD * w)
    return delta, m, v


def _adamw_shard(w, m, v, recv, tr, name):
    _, R, C = w.shape

    def body(w_ref, m_ref, v_ref, recv_ref, g_ref, d_ref, nm_ref, nv_ref):
        g = recv_ref[0].astype(F32)
        for d in range(1, N_DEV):
            g = g + recv_ref[d].astype(F32)
        g_ref[...] = g
        d_ref[...], nm_ref[...], nv_ref[...] = _adamw(w_ref[...], g, m_ref[...], v_ref[...])

    blk = pl.BlockSpec((None, tr, C), lambda i: (0, i, 0))
    return pl.pallas_call(
        body, name=name, grid=(R // tr,),
        in_specs=[blk, blk, blk, pl.BlockSpec((N_DEV, tr, C), lambda i: (0, i, 0))],
        out_specs=[blk] * 4, out_shape=[jax.ShapeDtypeStruct((1, R, C), F32)] * 4,
        compiler_params=_cp("arbitrary"),
    )(w, m, v, recv)


def _adamw_small(params, gathered, loss_parts):
    n = len(params)

    def body(*refs):
        ins, gs, loss_ref, outs = refs[:3 * n], refs[3 * n:4 * n], refs[4 * n], refs[4 * n + 1:]
        for p in range(n):
            w_ref, m_ref, v_ref = ins[3 * p:3 * p + 3]
            g = gs[p][0]
            for d in range(1, N_DEV):
                g = g + gs[p][d]
            g = g[..., :w_ref.shape[-1]]
            g_ref, d_ref, nm_ref, nv_ref = outs[4 * p:4 * p + 4]
            g_ref[...] = g
            d_ref[...], nm_ref[...], nv_ref[...] = _adamw(w_ref[...], g, m_ref[...], v_ref[...])
        total = loss_ref[0]
        for d in range(1, N_DEV):
            total = total + loss_ref[d]
        outs[4 * n][...] = total

    out_shape = [jax.ShapeDtypeStruct(w.shape, F32) for w, _, _ in params for _ in range(4)]
    res = pl.pallas_call(body, name="adamw_small", out_shape=[*out_shape, jax.ShapeDtypeStruct((8, LANES), F32)],
                         compiler_params=pltpu.CompilerParams(vmem_limit_bytes=VMEM_LIMIT))(
        *[t for p in params for t in p], *gathered, loss_parts)
    return [res[4 * p:4 * p + 4] for p in range(n)], res[4 * n][0, 0]


def _col_shards(g):
    return jnp.transpose(g.reshape(g.shape[0], N_DEV, -1), (1, 0, 2))


def _row_shards(g):
    return g.reshape(N_DEV, -1, g.shape[1])


def _cols_whole(g):
    return jnp.transpose(g, (1, 0, 2)).reshape(g.shape[1], -1)


def _join_cols(g, width, name):
    n_shards, rows, c = g.shape
    tr = min(256, rows)

    def body(g_ref, o_ref):
        for d in range(n_shards):
            o_ref[:, d * c:(d + 1) * c] = g_ref[d]
        if width > n_shards * c:
            o_ref[:, n_shards * c:] = jnp.zeros((tr, width - n_shards * c), o_ref.dtype)

    return pl.pallas_call(
        body, name=name, grid=(rows // tr,),
        in_specs=[pl.BlockSpec((n_shards, tr, c), lambda i: (0, i, 0))],
        out_specs=pl.BlockSpec((tr, width), lambda i: (i, 0)),
        out_shape=jax.ShapeDtypeStruct((rows, width), g.dtype),
        compiler_params=_cp("arbitrary"),
    )(g)


def kernel(x, norm_mix_g, w_in, f_bias, sg_ln_g, sg_w, sg_b, w_out, norm_ffn_g, w_up, w_conv, b_conv, w_down, norm_final_g, loss_target, m_norm_mix_g, m_w_in, m_f_bias, m_sg_ln_g, m_sg_w, m_sg_b, m_w_out, m_norm_ffn_g, m_w_up, m_w_conv, m_b_conv, m_w_down, m_norm_final_g, v_norm_mix_g, v_w_in, v_f_bias, v_sg_ln_g, v_sg_w, v_sg_b, v_w_out, v_norm_ffn_g, v_w_up, v_w_conv, v_b_conv, v_w_down, v_norm_final_g):
    xs, tgt = x[0], loss_target[0]
    g1, g2, g3 = norm_mix_g, norm_ffn_g, norm_final_g.reshape(1, D_MODEL)
    lg = sg_ln_g.reshape(1, D_A)
    fb = jnp.pad(f_bias, ((0, 0), (0, LANES - N_HEADS)))
    pos_chunk = np.arange(SG_BLOCK) // SG_CHUNK
    maskf = jnp.asarray(pos_chunk[:, None] >= pos_chunk[None, :], F32)
    wm = (sg_w[0] * maskf[None]).astype(BF16)
    wmt = jnp.swapaxes(wm, 1, 2)
    bfull = jnp.repeat(sg_b[0].T, HEAD_DIM, axis=1)
    same_head = np.arange(D_A)[:, None] // HEAD_DIM == np.arange(D_A)[None, :] // HEAD_DIM
    hsel = jnp.asarray(np.arange(N_HEADS)[:, None] == np.arange(D_A)[None, :] // HEAD_DIM, F32)
    avg = jnp.asarray(same_head * (1.0 / HEAD_DIM), BF16)

    (win_g,) = _exchange([w_in[0].astype(BF16)], [], "gather_w_in", relay=True)
    w_in_p = _join_cols(win_g, D_IN_PAD, "join_w_in")
    zuv, qa, ka, va, fl, h1 = _fwd_in(xs, g1, w_in_p, fb)
    oa = _gmlp_fwd(zuv, lg, wm, bfull, avg)
    rest = _Exchange([w_out[0].astype(BF16), w_up[0].astype(BF16), w_down[0].astype(BF16), w_conv[0]], [], relay=False)
    ob, lse, (wout_g, wup_g, wdown_g, wc_g) = _fox_fwd(qa, ka, va, rest)
    w_out_f, w_up_f = wout_g.reshape(D_MODEL, D_MODEL), _join_cols(wup_g, D_FF2, "join_w_up")
    w_down_f, wc_f = wdown_g.reshape(D_FF, D_MODEL), _cols_whole(wc_g)

    x2, h2, a = _fwd_mid(xs, oa, ob, w_out_f, g2, w_up_f)
    ac, yff, dx3, loss, dg3 = _fwd_ffn(a, x2, wc_f, b_conv, w_down_f, g3, tgt)
    da_g, da_v, dwc_g, dwc_v, dbc_g, dbc_v, dwdown_bf, dwup_bf = _bwd_ffn(dx3, a, ac, yff, h2, w_down_f, wc_f)
    dwc, dbc = jnp.concatenate([dwc_g, dwc_v], axis=1), jnp.concatenate([dbc_g, dbc_v], axis=1)
    dx2, doa, dob, dg2 = _bwd_mid(da_g, da_v, w_up_f, x2, g2, dx3, w_out_f, ob)
    dzuv, dwm, dsgb, dlg = _gmlp_bwd(zuv, doa, lg, wm, wmt, bfull, maskf, hsel, avg)
    _, dwout_bf = _matmul_tn([oa, ob], dx2, D_MODEL, D_MODEL, TK_DW, "dw_out")

    early = ("w_out", "w_up", "wc", "w_down")
    wire = [_row_shards(dwout_bf), dwup_bf, _col_shards(dwc).astype(BF16), dwdown_bf]
    small_early = dict(lg=dlg, sg_w=dwm, sg_b=dsgb, g2=dg2, bc=dbc, g3=dg3)
    grads = _Exchange([*small_early.values(), loss], wire, relay=False)
    dqa, dka, dva, got = _fox_bwd(qa, lse, dob, ka, va, grads)
    n_small = len(small_early)
    gathered, loss_parts = dict(zip(small_early, got[:n_small])), got[n_small]
    recv = dict(zip(early, got[n_small + 1:]))

    gx, dz, dg1, dfb = _bwd_in(dzuv, dqa, dka, dva, fl, xs, dx2, w_in_p, g1)
    wire_a, _ = _dw_in_half(h1, dz, 0, _Exchange([], [], relay=False), "dw_in_a")
    wire_b, (recv_a,) = _dw_in_half(h1, dz, 1, _Exchange([], [wire_a], relay=False), "dw_in_b")
    gathered["g1"], gathered["fb"], recv_b = _exchange([dg1, dfb], [wire_b], "exchange_w_in", relay=False)
    recv["w_in"] = jnp.concatenate([recv_a, recv_b], axis=1)

    weights = dict(w_in=(w_in, m_w_in, v_w_in, 256), w_out=(w_out, m_w_out, v_w_out, 128), w_up=(w_up, m_w_up, v_w_up, 256),
                   wc=(w_conv, m_w_conv, v_w_conv, 3), w_down=(w_down, m_w_down, v_w_down, 176))
    res = {n: _adamw_shard(w, m, v, recv[n], tr, "adamw_" + n) for n, (w, m, v, tr) in weights.items()}

    reps = dict(g1=((norm_mix_g, m_norm_mix_g, v_norm_mix_g), (1, D_MODEL)), fb=((f_bias, m_f_bias, v_f_bias), (1, N_HEADS)),
                lg=((sg_ln_g, m_sg_ln_g, v_sg_ln_g), (1, D_A)), sg_w=((sg_w, m_sg_w, v_sg_w), (N_HEADS, SG_BLOCK, SG_BLOCK)),
                sg_b=((sg_b, m_sg_b, v_sg_b), (N_HEADS, SG_BLOCK)), g2=((norm_ffn_g, m_norm_ffn_g, v_norm_ffn_g), (1, D_MODEL)),
                bc=((b_conv, m_b_conv, v_b_conv), (1, D_FF2)), g3=((norm_final_g, m_norm_final_g, v_norm_final_g), (1, D_MODEL)))
    outs, loss_sum = _adamw_small([tuple(t.reshape(shape) for t in wmv) for wmv, shape in reps.values()],
                                  [gathered[n] for n in reps], loss_parts)
    for (n, (wmv, _)), out in zip(reps.items(), outs):
        res[n] = [o.reshape(wmv[0].shape) for o in out]

    names = ("g1", "w_in", "fb", "lg", "sg_w", "sg_b", "w_out", "g2", "w_up", "wc", "bc", "w_down", "g3")
    return (loss_sum, gx[None], *[res[n][0] for n in names], *[res[n][1] for n in names],
            *[res[n][2] for n in names], *[res[n][3] for n in names])
```

```python
import functools
import math

import jax
import jax.numpy as jnp
import numpy as np
from jax import lax
from jax.experimental import pallas as pl
from jax.experimental.pallas import tpu as pltpu

F32 = jnp.float32
BF16 = jnp.bfloat16

D_MODEL = 1024
HEAD_DIM = 64
N_HEADS = 8
D_A = 512
D_B = 512
D_IN = 2 * D_A + 3 * D_B + N_HEADS
D_IN_PAD = 2688
D_FF = 2816
D_FF2 = 2 * D_FF
SG_BLOCK = 128
SG_CHUNK = 64
EPS = 1e-6
N_DEV = 8
LANES = 128
NEG = -1e30
VMEM_LIMIT = 56 * 1024 * 1024

ADAM_LR = 0.001
ADAM_B1 = 0.9
ADAM_B2 = 0.999
ADAM_EPS = 1e-08
ADAM_WD = 0.01
ADAM_STEP = 10

TM = 256
TM_MID = 512
TM_GMLP = 1024
TK_DW = 2048
TQ = 512
FWD_UNROLL = 4
BWD_UNROLL = 2
CW = 256

MESH_AXES = ("x", "y", "c")
ANY = pl.BlockSpec(memory_space=pl.ANY)


def _cp(*sem):
    return pltpu.CompilerParams(dimension_semantics=sem, vmem_limit_bytes=VMEM_LIMIT)


def _dot(a, b):
    return jnp.dot(a, b, preferred_element_type=F32)


def _dot_nt(a, b):
    return lax.dot_general(a, b, (((1,), (1,)), ((), ())), preferred_element_type=F32)


def _dot_tn(a, b):
    return lax.dot_general(a, b, (((0,), (0,)), ((), ())), preferred_element_type=F32)


def _dot_f32(a, b):
    return jnp.dot(a, b, precision=lax.Precision.HIGHEST, preferred_element_type=F32)


def _gelu(z):
    return 0.5 * z * (1.0 + lax.erf(z * (1.0 / math.sqrt(2.0))))


def _gelu_grad(z):
    return 0.5 * (1.0 + lax.erf(z * (1.0 / math.sqrt(2.0)))) + z * jnp.exp(-0.5 * z * z) * (1.0 / math.sqrt(2.0 * math.pi))


def _log_sigmoid(x):
    return jnp.minimum(x, 0.0) - jnp.log1p(jnp.exp(-jnp.abs(x)))


def _rms(x):
    r = lax.rsqrt(jnp.mean(x * x, axis=-1, keepdims=True) + EPS)
    return r, x * r


def _rms_bwd(dy, n, r, g):
    dn = dy * g
    return r * (dn - n * jnp.mean(dn * n, axis=-1, keepdims=True))


def _full(shape, single=False):
    nd = len(shape)
    if single:
        return pl.BlockSpec(shape, lambda *_: (0,) * nd, pipeline_mode=pl.Buffered(1))
    return pl.BlockSpec(shape, lambda *_: (0,) * nd)


def _rows(tm, cols, rev_nt=None):
    if rev_nt is None:
        return pl.BlockSpec((tm, cols), lambda i: (i, 0))
    return pl.BlockSpec((tm, cols), lambda i: (rev_nt - 1 - i, 0))


def _head_sl(h):
    return slice(HEAD_DIM * h, HEAD_DIM * (h + 1))


L_ROW = HEAD_DIM
L_COL = HEAD_DIM + 3
L_LSE = HEAD_DIM + 6


def _split3(x):
    hi = x.astype(BF16).astype(F32)
    mid = (x - hi).astype(BF16).astype(F32)
    lo = (x - hi - mid).astype(BF16).astype(F32)
    return hi, mid, lo


def _bias_lane_placement():
    j, h = np.arange(3 * LANES) // LANES, np.arange(3 * LANES) % LANES
    cols = np.arange(N_HEADS * LANES)[None, :]
    return jnp.asarray((h[:, None] < N_HEADS) & (cols == LANES * h[:, None] + L_ROW + j[:, None]), BF16)


def _lanes(rows, width, parts):
    lane = lax.broadcasted_iota(jnp.int32, (rows, width), 1)
    out = jnp.zeros((rows, width), F32)
    for at, val in parts.items():
        out = jnp.where(lane == at, val, out)
    return out


def _fwd_in(x, g1, w_in_p, fb):
    T = x.shape[0]
    tm = min(TM, T)

    def body(x_ref, g_ref, w_ref, fb_ref, place_ref, zuv_ref, qa_ref, ka_ref, va_ref, fl_ref, h1_ref, carry):
        @pl.when(pl.program_id(0) == 0)
        def _():
            carry[...] = jnp.zeros_like(carry)

        r, n = _rms(x_ref[...])
        h = (n * g_ref[...]).astype(BF16)
        h1_ref[...] = h
        z = _dot(h, w_ref[...])
        zuv_ref[...] = z[:, :2 * D_A]
        o = 2 * D_A
        fl = z[:, o + 3 * D_B:] + fb_ref[...]
        fl_ref[...] = fl
        tri = (lax.broadcasted_iota(jnp.int32, (tm, tm), 0) >= lax.broadcasted_iota(jnp.int32, (tm, tm), 1)).astype(F32)
        c = _dot_f32(tri, _log_sigmoid(fl)) + carry[...]
        carry[...] = c[tm - 1:tm, :]
        hi, mid, lo = _split3(c)
        parts = jnp.concatenate([hi.astype(BF16), mid.astype(BF16), lo.astype(BF16)], axis=-1)
        placed = _dot(parts, place_ref[...])
        lane = lax.broadcasted_iota(jnp.int32, (tm, LANES), 1)
        data = lane < HEAD_DIM
        ones_q = ((lane >= L_COL) & (lane < L_COL + 3)).astype(F32)
        ones_k = (((lane >= L_ROW) & (lane < L_ROW + 3)) | ((lane >= L_LSE) & (lane < L_LSE + 3))).astype(F32)
        ones_v = ((lane >= L_ROW) & (lane < L_ROW + 3)).astype(F32)
        for hd in range(N_HEADS):
            def rows_of(first_col):
                tile = z[:, first_col + LANES * (hd // 2):first_col + LANES * (hd // 2 + 1)]
                return tile if hd % 2 == 0 else pltpu.roll(tile, HEAD_DIM, axis=1)

            hs = slice(LANES * hd, LANES * (hd + 1))
            qa_ref[hd] = jnp.where(data, rows_of(o) * (HEAD_DIM ** -0.5), placed[:, hs] + ones_q).astype(BF16)
            key_side = pltpu.roll(placed[:, hs], L_COL - L_ROW, axis=1)
            ka_ref[hd] = jnp.where(data, rows_of(o + D_B), ones_k - key_side).astype(BF16)
            va_ref[hd] = jnp.where(data, rows_of(o + 2 * D_B), ones_v).astype(BF16)

    heads = pl.BlockSpec((N_HEADS, tm, LANES), lambda i: (0, i, 0))
    aug = jax.ShapeDtypeStruct((N_HEADS, T, LANES), BF16)
    place = _bias_lane_placement()
    return pl.pallas_call(
        body, name="fwd_in", grid=(T // tm,),
        in_specs=[_rows(tm, D_MODEL), _full((1, D_MODEL)), _full((D_MODEL, D_IN_PAD), True), _full((1, LANES)),
                  _full((3 * LANES, N_HEADS * LANES))],
        out_specs=[_rows(tm, 2 * D_A), heads, heads, heads, _rows(tm, LANES), _rows(tm, D_MODEL)],
        out_shape=[jax.ShapeDtypeStruct((T, 2 * D_A), F32), aug, aug, aug, jax.ShapeDtypeStruct((T, LANES), F32),
                   jax.ShapeDtypeStruct((T, D_MODEL), BF16)],
        scratch_shapes=[pltpu.VMEM((1, LANES), F32)],
        compiler_params=_cp("arbitrary"),
    )(x, g1, w_in_p, fb, place)


def _segmean(x, avg_ref, parts):
    out, rest = None, x
    for _ in range(parts):
        piece = rest.astype(BF16)
        term = _dot(piece, avg_ref[...])
        out = term if out is None else out + term
        rest = rest - piece.astype(F32)
    return out


def _layer_norm(v, lg, avg_ref):
    d = v - _segmean(v, avg_ref, 3)
    rstd = lax.rsqrt(_segmean(d * d, avg_ref, 2) + EPS)
    vhat = d * rstd
    return vhat, rstd, (vhat * lg).astype(BF16)


def _mix_heads(w_ref, x, row_slice):
    low = lax.broadcasted_iota(jnp.int32, (SG_BLOCK, LANES), 1) < HEAD_DIM
    tiles = []
    for p in range(N_HEADS // 2):
        xt = x[row_slice, LANES * p:LANES * (p + 1)]
        zero = jnp.zeros_like(xt)
        tiles.append(_dot(w_ref[2 * p], jnp.where(low, xt, zero)) + _dot(w_ref[2 * p + 1], jnp.where(low, zero, xt)))
    return jnp.concatenate(tiles, axis=-1)


def _gmlp_fwd(zuv, lg, wm, bfull, avg):
    T = zuv.shape[0]
    tg = min(TM_GMLP, T)
    nb = tg // SG_BLOCK

    def body(zuv_ref, lg_ref, wm_ref, b_ref, avg_ref, oa_ref):
        u = _gelu(zuv_ref[:, :D_A])
        _, _, vn = _layer_norm(_gelu(zuv_ref[:, D_A:]), lg_ref[...], avg_ref)
        for n in range(nb):
            rs = slice(n * SG_BLOCK, (n + 1) * SG_BLOCK)
            oa_ref[rs, :] = (u[rs] * (_mix_heads(wm_ref, vn, rs) + b_ref[...])).astype(BF16)

    return pl.pallas_call(
        body, name="gmlp_fwd", grid=(T // tg,),
        in_specs=[_rows(tg, 2 * D_A), _full((1, D_A)), _full((N_HEADS, SG_BLOCK, SG_BLOCK)), _full((SG_BLOCK, D_A)),
                  _full((D_A, D_A))],
        out_specs=_rows(tg, D_A),
        out_shape=jax.ShapeDtypeStruct((T, D_A), BF16),
        compiler_params=_cp("arbitrary"),
    )(zuv, lg, wm, bfull, avg)


def _gmlp_bwd(zuv, doa, lg, wm, wmt, bfull, maskf, hsel, avg):
    T = zuv.shape[0]
    tg = min(TM_GMLP, T)
    nb = tg // SG_BLOCK
    nt = T // tg

    def body(zuv_ref, doa_ref, lg_ref, wm_ref, wmt_ref, b_ref, mask_ref, hsel_ref, avg_ref,
             dzuv_ref, dwm_ref, dsgb_ref, dlg_ref, dbacc):
        i = pl.program_id(0)

        @pl.when(i == 0)
        def _():
            dwm_ref[...] = jnp.zeros_like(dwm_ref)
            dlg_ref[...] = jnp.zeros_like(dlg_ref)
            dbacc[...] = jnp.zeros_like(dbacc)

        zu, zv = zuv_ref[:, :D_A], zuv_ref[:, D_A:]
        lgv = lg_ref[...]
        vhat, rstd, vn = _layer_norm(_gelu(zv), lgv, avg_ref)
        dmixed = doa_ref[...] * _gelu(zu)
        dmb = dmixed.astype(BF16)
        low = lax.broadcasted_iota(jnp.int32, (SG_BLOCK, LANES), 1) < HEAD_DIM
        mixed, dvn = [], []
        for n in range(nb):
            rs = slice(n * SG_BLOCK, (n + 1) * SG_BLOCK)
            mixed.append(_mix_heads(wm_ref, vn, rs) + b_ref[...])
            dvn.append(_mix_heads(wmt_ref, dmb, rs))
            dbacc[...] += dmixed[rs]
            for p in range(N_HEADS // 2):
                ls = slice(LANES * p, LANES * (p + 1))
                dmt, vnt = dmb[rs, ls], vn[rs, ls]
                zero = jnp.zeros_like(dmt)
                dwm_ref[2 * p] += _dot_nt(jnp.where(low, dmt, zero), vnt) * mask_ref[...]
                dwm_ref[2 * p + 1] += _dot_nt(jnp.where(low, zero, dmt), vnt) * mask_ref[...]
        mixed = jnp.concatenate(mixed, axis=0) if nb > 1 else mixed[0]
        dvn = jnp.concatenate(dvn, axis=0) if nb > 1 else dvn[0]
        dlg_ref[...] += jnp.sum(dvn * vhat, axis=0, keepdims=True)
        dvhat = dvn * lgv
        dv = rstd * (dvhat - _segmean(dvhat, avg_ref, 2) - vhat * _segmean(dvhat * vhat, avg_ref, 2))
        dzuv_ref[:, :D_A] = (doa_ref[...] * mixed * _gelu_grad(zu)).astype(BF16)
        dzuv_ref[:, D_A:] = (dv * _gelu_grad(zv)).astype(BF16)

        @pl.when(i == nt - 1)
        def _():
            dsgb_ref[...] = lax.dot_general(hsel_ref[...], dbacc[...], (((1,), (1,)), ((), ())),
                                            precision=lax.Precision.HIGHEST, preferred_element_type=F32)

    return pl.pallas_call(
        body, name="gmlp_bwd", grid=(nt,),
        in_specs=[_rows(tg, 2 * D_A), _rows(tg, D_A), _full((1, D_A)), _full((N_HEADS, SG_BLOCK, SG_BLOCK)),
                  _full((N_HEADS, SG_BLOCK, SG_BLOCK)), _full((SG_BLOCK, D_A)), _full((SG_BLOCK, SG_BLOCK)),
                  _full((N_HEADS, D_A)), _full((D_A, D_A))],
        out_specs=[_rows(tg, 2 * D_A), _full((N_HEADS, SG_BLOCK, SG_BLOCK)), _full((N_HEADS, SG_BLOCK)), _full((1, D_A))],
        out_shape=[jax.ShapeDtypeStruct((T, 2 * D_A), BF16), jax.ShapeDtypeStruct((N_HEADS, SG_BLOCK, SG_BLOCK), F32),
                   jax.ShapeDtypeStruct((N_HEADS, SG_BLOCK), F32), jax.ShapeDtypeStruct((1, D_A), F32)],
        scratch_shapes=[pltpu.VMEM((SG_BLOCK, D_A), F32)],
        compiler_params=_cp("arbitrary"),
    )(zuv, doa, lg, wm, wmt, bfull, maskf, hsel, avg)


def _fox_fwd(qa, ka, va, side):
    T = qa.shape[1]
    tq = min(TQ, T)
    nq = T // tq
    n = side.n

    def body(qa_ref, ka_ref, va_ref, *rest):
        ins, (o_ref, lse_ref), outs = rest[:n], rest[n:n + 2], rest[n + 2:2 * n + 2]
        acc_ref, *sems = rest[2 * n + 2:]
        i = pl.program_id(1)
        if n:
            @pl.when((pl.program_id(0) == 0) & (i == 0))
            def _():
                side.start(ins, outs, sems)
        row = lax.broadcasted_iota(jnp.int32, (tq, tq), 0)
        col = lax.broadcasted_iota(jnp.int32, (tq, tq), 1)
        qs = [qa_ref[0], qa_ref[1]]

        def tiles(js, carry, diag):
            offs = [pl.multiple_of(j * tq, tq) for j in js]
            logits = [[_dot_nt(qs[hh], ka_ref[hh, pl.ds(off, tq), :]) for hh in range(2)] for off in offs]
            carry = list(carry)
            for off, per_head in zip(offs, logits):
                for hh, s in enumerate(per_head):
                    if diag:
                        s = jnp.where(col <= row, s, NEG)
                    m = carry[hh]
                    m_new = jnp.maximum(m, jnp.max(s, axis=-1, keepdims=True))
                    pr = jnp.exp(s - m_new)
                    acc_ref[hh] = jnp.exp(m - m_new) * acc_ref[hh] + _dot(pr.astype(BF16), va_ref[hh, pl.ds(off, tq), :])
                    carry[hh] = m_new
            return tuple(carry)

        acc_ref[...] = jnp.zeros_like(acc_ref)
        init = (jnp.full((tq, 1), NEG, F32),) * 2
        carry = lax.fori_loop(0, i // FWD_UNROLL, lambda t, cr: tiles([FWD_UNROLL * t + u for u in range(FWD_UNROLL)], cr, False), init)
        carry = lax.fori_loop(i - i % FWD_UNROLL, i, lambda j, cr: tiles([j], cr, False), carry)
        carry = tiles([i], carry, True)
        for hh in range(2):
            m, acc = carry[hh], acc_ref[hh]
            l = acc[:, L_ROW:L_ROW + 1]
            o_ref[:, _head_sl(hh)] = acc[:, :HEAD_DIM] / l
            hi, mid, lo = _split3(-(m + jnp.log(l)))
            lse_ref[hh] = _lanes(tq, LANES, {L_LSE: hi, L_LSE + 1: mid, L_LSE + 2: lo}).astype(BF16)
        if n:
            @pl.when((pl.program_id(0) == N_HEADS // 2 - 1) & (i == nq - 1))
            def _():
                side.finish(ins, outs, sems)

    tile = pl.BlockSpec((2, tq, LANES), lambda p, i: (p, i, 0))
    seq = pl.BlockSpec((2, T, LANES), lambda p, i: (p, 0, 0))
    res = pl.pallas_call(
        body, name="fox_fwd", grid=(N_HEADS // 2, nq),
        in_specs=[tile, seq, seq, *side.specs],
        out_specs=[pl.BlockSpec((tq, LANES), lambda p, i: (i, p)), tile, *side.specs],
        out_shape=[jax.ShapeDtypeStruct((T, D_B), F32), jax.ShapeDtypeStruct((N_HEADS, T, LANES), BF16), *side.out_shape],
        scratch_shapes=[pltpu.VMEM((2, tq, LANES), F32), *side.scratch],
        compiler_params=_cp("arbitrary", "arbitrary"),
    )(qa, ka, va, *side.operands)
    return res[0], res[1], res[2:]


def _fox_bwd(qa, lse, doa, ka, va, side):
    T = qa.shape[1]
    tq = min(TQ, T)
    nq = T // tq
    n = side.n

    def body(qa_ref, lse_ref, doa_ref, ka_ref, va_ref, *rest):
        ins, (dqa_ref, dka_ref, dva_ref), outs = rest[:n], rest[n:n + 3], rest[n + 3:2 * n + 3]
        dv_acc, *sems = rest[2 * n + 3:]
        j = pl.program_id(1)
        if n:
            @pl.when((pl.program_id(0) == 0) & (j == 0))
            def _():
                side.start(ins, outs, sems)

        @pl.when(j == 0)
        def _():
            dqa_ref[...] = jnp.zeros_like(dqa_ref)

        row = lax.broadcasted_iota(jnp.int32, (tq, tq), 0)
        col = lax.broadcasted_iota(jnp.int32, (tq, tq), 1)
        ks = [ka_ref[0], ka_ref[1]]
        vs = [va_ref[0], va_ref[1]]

        def tiles(ids, diag):
            work = []
            for i in ids:
                off = pl.multiple_of(i * tq, tq)
                for hh in range(2):
                    qi = qa_ref[hh, pl.ds(off, tq), :] + lse_ref[hh, pl.ds(off, tq), :]
                    doi = doa_ref[hh, pl.ds(off, tq), :]
                    work.append((off, hh, qi, doi, _dot_nt(ks[hh], qi), _dot_nt(vs[hh], doi)))
            for off, hh, qi, doi, st, dpt in work:
                if diag:
                    st = jnp.where(row <= col, st, NEG)
                pt = jnp.exp(st)
                dv_acc[hh] += _dot(pt.astype(BF16), doi)
                dsb = (pt * dpt).astype(BF16)
                dka_ref[hh] += _dot(dsb, qi)
                dqa_ref[hh, pl.ds(off, tq), :] += _dot_tn(dsb, ks[hh])

        dka_ref[...] = jnp.zeros_like(dka_ref)
        dv_acc[...] = jnp.zeros_like(dv_acc)
        tiles([j], True)
        todo = nq - 1 - j

        @pl.loop(0, todo // BWD_UNROLL)
        def _(t):
            tiles([j + 1 + BWD_UNROLL * t + u for u in range(BWD_UNROLL)], False)

        @pl.loop(nq - todo % BWD_UNROLL, nq)
        def _(i):
            tiles([i], False)

        dva_ref[...] = dv_acc[...].astype(BF16)
        if n:
            @pl.when((pl.program_id(0) == N_HEADS // 2 - 1) & (j == nq - 1))
            def _():
                side.finish(ins, outs, sems)

    tile = pl.BlockSpec((2, tq, LANES), lambda p, j: (p, j, 0))
    seq = pl.BlockSpec((2, T, LANES), lambda p, j: (p, 0, 0))
    res = pl.pallas_call(
        body, name="fox_bwd", grid=(N_HEADS // 2, nq),
        in_specs=[seq, seq, seq, tile, tile, *side.specs],
        out_specs=[seq, tile, tile, *side.specs],
        out_shape=[jax.ShapeDtypeStruct((N_HEADS, T, LANES), F32), jax.ShapeDtypeStruct((N_HEADS, T, LANES), F32),
                   jax.ShapeDtypeStruct((N_HEADS, T, LANES), BF16), *side.out_shape],
        scratch_shapes=[pltpu.VMEM((2, tq, LANES), F32), *side.scratch],
        compiler_params=_cp("arbitrary", "arbitrary"),
    )(qa, lse, doa, ka, va, *side.operands)
    return res[0], res[1], res[2], res[3:]


def _fwd_mid(x, oa, ob, w_out, g2, w_up):
    T = x.shape[0]
    tm = min(TM_MID, T)

    def body(x_ref, oa_ref, ob_ref, wo_ref, g_ref, wu_ref, x2_ref, h2_ref, a_ref):
        oab = jnp.concatenate([oa_ref[...], ob_ref[...].astype(BF16)], axis=-1)
        x2 = x_ref[...] + _dot(oab, wo_ref[...])
        x2_ref[...] = x2
        _, n = _rms(x2)
        h2 = (n * g_ref[...]).astype(BF16)
        h2_ref[...] = h2
        a_ref[...] = _dot(h2, wu_ref[...])

    return pl.pallas_call(
        body, name="fwd_mid", grid=(T // tm,),
        in_specs=[_rows(tm, D_MODEL), _rows(tm, D_A), _rows(tm, D_B), _full((D_MODEL, D_MODEL), True), _full((1, D_MODEL)),
                  _full((D_MODEL, D_FF2), True)],
        out_specs=[_rows(tm, D_MODEL), _rows(tm, D_MODEL), _rows(tm, D_FF2)],
        out_shape=[jax.ShapeDtypeStruct((T, D_MODEL), F32), jax.ShapeDtypeStruct((T, D_MODEL), BF16),
                   jax.ShapeDtypeStruct((T, D_FF2), F32)],
        compiler_params=_cp("arbitrary"),
    )(x, oa, ob, w_out, g2, w_up)


def _row_before(x, prev, k):
    rolled = pltpu.roll(x, k, axis=0)
    row = lax.broadcasted_iota(jnp.int32, (8, x.shape[1]), 0)
    head = rolled[0:8]
    for r in range(k):
        head = jnp.where(row == r, prev[8 - k + r:9 - k + r], head)
    return jnp.concatenate([head, rolled[8:]], axis=0)


def _row_after(x, nxt, k):
    tm = x.shape[0]
    rolled = pltpu.roll(x, tm - k, axis=0)
    row = lax.broadcasted_iota(jnp.int32, (8, x.shape[1]), 0)
    tail = rolled[tm - 8:tm]
    for r in range(k):
        tail = jnp.where(row == 8 - k + r, nxt[r:r + 1], tail)
    return jnp.concatenate([rolled[:tm - 8], tail], axis=0)


def _fwd_ffn(a, x2, wc, bc, w_down, g3, tgt):
    T = x2.shape[0]
    tm = min(TM, T)

    def body(a_ref, x2_ref, wc_ref, bc_ref, wd_ref, g_ref, tgt_ref, ac_ref, yff_ref, dx3_ref, loss_ref, dg3_ref, carry):
        @pl.when(pl.program_id(0) == 0)
        def _():
            carry[...] = jnp.zeros_like(carry)
            loss_ref[...] = jnp.zeros_like(loss_ref)
            dg3_ref[...] = jnp.zeros_like(dg3_ref)

        def conv(cs):
            a0 = a_ref[:, cs]
            prev = carry[:, cs]
            ac = (wc_ref[0:1, cs] * _row_before(a0, prev, 2) + wc_ref[1:2, cs] * _row_before(a0, prev, 1)
                  + wc_ref[2:3, cs] * a0 + bc_ref[:, cs])
            ac_ref[:, cs] = ac.astype(BF16)
            return ac

        x3 = x2_ref[...]
        for ci in range(D_FF // CW):
            gs = slice(ci * CW, (ci + 1) * CW)
            ag = conv(gs)
            av = conv(slice(D_FF + ci * CW, D_FF + (ci + 1) * CW))
            yb = (ag * jax.nn.sigmoid(ag) * av).astype(BF16)
            yff_ref[:, gs] = yb
            x3 = x3 + _dot(yb, wd_ref[gs, :])
        carry[...] = a_ref[tm - 8:tm, :]
        r, n = _rms(x3)
        g = g_ref[...]
        diff = n * g - tgt_ref[...]
        loss_ref[...] += (0.5 / D_MODEL) * jnp.sum(diff * diff)
        dout = diff * (1.0 / D_MODEL)
        dg3_ref[...] += jnp.sum(dout * n, axis=0, keepdims=True)
        dx3_ref[...] = _rms_bwd(dout, n, r, g)

    return pl.pallas_call(
        body, name="fwd_ffn", grid=(T // tm,),
        in_specs=[_rows(tm, D_FF2), _rows(tm, D_MODEL), _full((3, D_FF2)), _full((1, D_FF2)), _full((D_FF, D_MODEL), True),
                  _full((1, D_MODEL)), _rows(tm, D_MODEL)],
        out_specs=[_rows(tm, D_FF2), _rows(tm, D_FF), _rows(tm, D_MODEL), _full((8, LANES)), _full((1, D_MODEL))],
        out_shape=[jax.ShapeDtypeStruct((T, D_FF2), BF16), jax.ShapeDtypeStruct((T, D_FF), BF16),
                   jax.ShapeDtypeStruct((T, D_MODEL), F32), jax.ShapeDtypeStruct((8, LANES), F32),
                   jax.ShapeDtypeStruct((1, D_MODEL), F32)],
        scratch_shapes=[pltpu.VMEM((8, D_FF2), F32)],
        compiler_params=_cp("arbitrary"),
    )(a, x2, wc, bc, w_down, g3, tgt)


def _bwd_ffn(dx3, a, ac, yff, h2, w_down, wc):
    T = dx3.shape[0]
    tm = min(TM, T)
    nt = T // tm
    half = D_FF // 2
    shard_up, shard_down = D_FF2 // N_DEV, D_FF // N_DEV

    def body(dx3_ref, ag_ref, av_ref, acg_ref, acv_ref, yff_ref, h2_ref, wd_ref, wcg_ref, wcv_ref,
             dag_ref, dav_ref, dwcg_ref, dwcv_ref, dbcg_ref, dbcv_ref, dwd_ref, dwu_ref,
             nxt, shifted, acc_down, acc_g, acc_v, stage_up, stage_down, sem):
        c, r = pl.program_id(0), pl.program_id(1)

        @pl.when(r == 0)
        def _():
            for ref in (nxt, dwcg_ref, dwcv_ref, dbcg_ref, dbcv_ref, acc_down, acc_g, acc_v):
                ref[...] = jnp.zeros_like(ref)

        dxb = dx3_ref[...].astype(BF16)
        dy_all = _dot_nt(dxb, wd_ref[...])

        def back(a_ref, w_ref, da_ref, dwc_ref, dbc_ref, nx, cs, dac):
            a0 = a_ref[:, cs]
            shifted[0] = _row_after(dac, nxt[:, nx], 1)
            shifted[1] = _row_after(dac, nxt[:, nx], 2)
            dp1, dp2 = shifted[0], shifted[1]
            dbc_ref[:, cs] += jnp.sum(dac, axis=0, keepdims=True)
            dwc_ref[0:1, cs] += jnp.sum(dp2 * a0, axis=0, keepdims=True)
            dwc_ref[1:2, cs] += jnp.sum(dp1 * a0, axis=0, keepdims=True)
            dwc_ref[2:3, cs] += jnp.sum(dac * a0, axis=0, keepdims=True)
            da_ref[:, cs] = (w_ref[2:3, cs] * dac + w_ref[1:2, cs] * dp1 + w_ref[0:1, cs] * dp2).astype(BF16)
            nxt[:, nx] = dac[0:8]

        h2t = h2_ref[...].T
        n_chunks = half // LANES
        groups = {min(2 * g + 1, n_chunks - 1) if g < n_chunks // 2 - 1 else n_chunks - 1: 2 * g for g in range(n_chunks // 2)}
        edge = None
        for ci in range(n_chunks):
            cs = slice(ci * LANES, (ci + 1) * LANES)
            dy = dy_all[:, cs]
            if edge is not None:
                bits = lax.bitcast_convert_type(dy, jnp.uint32).reshape(tm // 8, 8, LANES) + edge[None]
                dy = lax.bitcast_convert_type(bits.reshape(tm, LANES), F32)
                edge = None
            ag, av = acg_ref[:, cs].astype(F32), acv_ref[:, cs].astype(F32)
            sg = jax.nn.sigmoid(ag)
            back(av_ref, wcv_ref, dav_ref, dwcv_ref, dbcv_ref, slice(half + ci * LANES, half + (ci + 1) * LANES), cs,
                 dy * (ag * sg))
            back(ag_ref, wcg_ref, dag_ref, dwcg_ref, dbcg_ref, cs, cs, dy * av * (sg * (1.0 + ag * (1.0 - sg))))
            if ci in groups:
                gs = slice(groups[ci] * LANES, (ci + 1) * LANES)
                up_g, up_v = _dot(h2t, dag_ref[:, gs]), _dot(h2t, dav_ref[:, gs])
                up_d = _dot_tn(yff_ref[:, gs], dxb)
                acc_g[:, gs] += up_g
                acc_v[:, gs] += up_v
                acc_down[gs, :] += up_d
                last = lax.bitcast_convert_type(up_d[-8:, -LANES:], jnp.uint32)
                edge = lax.shift_right_logical(lax.shift_right_logical(last, jnp.uint32(16)), jnp.uint32(16))

        @pl.when(r == nt - 1)
        def _():
            for s in range(half // shard_down):
                stage_down[...] = acc_down[s * shard_down:(s + 1) * shard_down, :].astype(BF16)
                out = pltpu.make_async_copy(stage_down, dwd_ref.at[(half // shard_down) * c + s], sem)
                out.start()
                out.wait()
            for acc, first in ((acc_g, 0), (acc_v, N_DEV // 2)):
                for s in range(half // shard_up):
                    stage_up[...] = acc[:, s * shard_up:(s + 1) * shard_up].astype(BF16)
                    out = pltpu.make_async_copy(stage_up, dwu_ref.at[first + (half // shard_up) * c + s], sem)
                    out.start()
                    out.wait()

    def cols(width, second_half):
        return pl.BlockSpec((tm, width), lambda c, r: (nt - 1 - r, c + (2 if second_half else 0)))

    def param(rows, second_half):
        return pl.BlockSpec((rows, half), lambda c, r: (0, c + (2 if second_half else 0)))

    tokens = pl.BlockSpec((tm, D_MODEL), lambda c, r: (nt - 1 - r, 0))
    return pl.pallas_call(
        body, name="bwd_ffn", grid=(2, nt),
        in_specs=[tokens, cols(half, False), cols(half, True), cols(half, False), cols(half, True), cols(half, False), tokens,
                  pl.BlockSpec((half, D_MODEL), lambda c, r: (c, 0), pipeline_mode=pl.Buffered(1)),
                  param(3, False), param(3, True)],
        out_specs=[cols(half, False), cols(half, False), param(3, False), param(3, False), param(1, False), param(1, False),
                   ANY, ANY],
        out_shape=[jax.ShapeDtypeStruct((T, D_FF), BF16), jax.ShapeDtypeStruct((T, D_FF), BF16),
                   jax.ShapeDtypeStruct((3, D_FF), F32), jax.ShapeDtypeStruct((3, D_FF), F32),
                   jax.ShapeDtypeStruct((1, D_FF), F32), jax.ShapeDtypeStruct((1, D_FF), F32),
                   jax.ShapeDtypeStruct((N_DEV, shard_down, D_MODEL), BF16), jax.ShapeDtypeStruct((N_DEV, D_MODEL, shard_up), BF16)],
        scratch_shapes=[pltpu.VMEM((8, D_FF), F32), pltpu.VMEM((2, tm, LANES), F32), pltpu.VMEM((half, D_MODEL), F32),
                        pltpu.VMEM((D_MODEL, half), F32), pltpu.VMEM((D_MODEL, half), F32),
                        pltpu.VMEM((D_MODEL, shard_up), BF16), pltpu.VMEM((shard_down, D_MODEL), BF16),
                        pltpu.SemaphoreType.DMA],
        compiler_params=_cp("arbitrary", "arbitrary"),
    )(dx3, a, a, ac, ac, yff, h2, w_down, wc, wc)


def _bwd_mid(da_g, da_v, w_up, x2, g2, dx3, w_out, ob):
    T = x2.shape[0]
    tm = min(TM, T)

    def body(dag_ref, dav_ref, wu_ref, x2_ref, g_ref, dx3_ref, wo_ref, ob_ref, hsum_ref, place_ref,
             dx2_ref, doa_ref, dob_ref, dg2_ref):
        @pl.when(pl.program_id(0) == 0)
        def _():
            dg2_ref[...] = jnp.zeros_like(dg2_ref)

        dh2 = _dot_nt(dag_ref[...], wu_ref[:, :D_FF]) + _dot_nt(dav_ref[...], wu_ref[:, D_FF:])
        r, n = _rms(x2_ref[...])
        dg2_ref[...] += jnp.sum(dh2 * n, axis=0, keepdims=True)
        dx2 = dx3_ref[...] + _rms_bwd(dh2, n, r, g_ref[...])
        dx2_ref[...] = dx2
        doab = _dot_nt(dx2.astype(BF16), wo_ref[...])
        doa_ref[...] = doab[:, :D_A]
        dob = doab[:, D_A:]
        rest, delta = dob.astype(BF16).astype(F32) * ob_ref[...], None
        for _ in range(3):
            piece = rest.astype(BF16)
            term = _dot(piece, hsum_ref[...])
            delta = term if delta is None else delta + term
            rest = rest - piece.astype(F32)
        hi, mid, lo = _split3(-delta)
        parts = jnp.concatenate([hi.astype(BF16), mid.astype(BF16), lo.astype(BF16)], axis=-1)
        placed = _dot(parts, place_ref[...])
        data = lax.broadcasted_iota(jnp.int32, (tm, LANES), 1) < HEAD_DIM
        for hd in range(N_HEADS):
            tile = dob[:, LANES * (hd // 2):LANES * (hd // 2 + 1)]
            rows = tile if hd % 2 == 0 else pltpu.roll(tile, HEAD_DIM, axis=1)
            dob_ref[hd] = jnp.where(data, rows, placed[:, LANES * hd:LANES * (hd + 1)]).astype(BF16)

    hsum = jnp.asarray(np.arange(D_B)[:, None] // HEAD_DIM == np.arange(LANES)[None, :], BF16)
    place = _bias_lane_placement()
    return pl.pallas_call(
        body, name="bwd_mid", grid=(T // tm,),
        in_specs=[_rows(tm, D_FF), _rows(tm, D_FF), _full((D_MODEL, D_FF2), True), _rows(tm, D_MODEL), _full((1, D_MODEL)),
                  _rows(tm, D_MODEL), _full((D_MODEL, D_MODEL), True), _rows(tm, D_B), _full((D_B, LANES)),
                  _full((3 * LANES, N_HEADS * LANES))],
        out_specs=[_rows(tm, D_MODEL), _rows(tm, D_A), pl.BlockSpec((N_HEADS, tm, LANES), lambda i: (0, i, 0)),
                   _full((1, D_MODEL))],
        out_shape=[jax.ShapeDtypeStruct((T, D_MODEL), F32), jax.ShapeDtypeStruct((T, D_A), F32),
                   jax.ShapeDtypeStruct((N_HEADS, T, LANES), BF16), jax.ShapeDtypeStruct((1, D_MODEL), F32)],
        compiler_params=_cp("arbitrary"),
    )(da_g, da_v, w_up, x2, g2, dx3, w_out, ob, hsum, place)


def _bwd_in(dzuv, dqa, dka, dva, fl, x, dx2, w_in_p, g1):
    T = x.shape[0]
    tm = min(TM, T)
    nt = T // tm

    def body(dzuv_ref, dqa_ref, dka_ref, dva_ref, fl_ref, x_ref, dx2_ref, w_ref, g_ref,
             gx_ref, dz_ref, dg1_ref, dfb_ref, carry):
        @pl.when(pl.program_id(0) == 0)
        def _():
            carry[...] = jnp.zeros_like(carry)
            dg1_ref[...] = jnp.zeros_like(dg1_ref)
            dfb_ref[...] = jnp.zeros_like(dfb_ref)

        dc = _lanes(tm, LANES, {hd: dqa_ref[hd][:, L_ROW:L_ROW + 1] - dka_ref[hd][:, L_COL:L_COL + 1] for hd in range(N_HEADS)})
        later = (lax.broadcasted_iota(jnp.int32, (tm, tm), 1) >= lax.broadcasted_iota(jnp.int32, (tm, tm), 0)).astype(F32)
        dls = _dot_f32(later, dc) + carry[...]
        carry[...] = dls[0:1, :]
        dzf = dls * jax.nn.sigmoid(-fl_ref[...])
        dfb_ref[...] += jnp.sum(dzf, axis=0, keepdims=True)
        data = lax.broadcasted_iota(jnp.int32, (tm, LANES), 1) < HEAD_DIM

        def compact(ref, scale=None):
            def rows(hd):
                return (ref[hd] if scale is None else ref[hd] * scale).astype(BF16)

            return [jnp.where(data, rows(2 * p), pltpu.roll(rows(2 * p + 1), HEAD_DIM, axis=1)) for p in range(N_HEADS // 2)]

        dz = jnp.concatenate([dzuv_ref[...], *compact(dqa_ref, HEAD_DIM ** -0.5), *compact(dka_ref), *compact(dva_ref),
                              dzf.astype(BF16)], axis=-1)
        dz_ref[...] = dz
        dh1 = _dot_nt(dz, w_ref[...])
        r, n = _rms(x_ref[...])
        dg1_ref[...] += jnp.sum(dh1 * n, axis=0, keepdims=True)
        gx_ref[...] = dx2_ref[...] + _rms_bwd(dh1, n, r, g_ref[...])

    rv = functools.partial(_rows, tm, rev_nt=nt)
    heads = pl.BlockSpec((N_HEADS, tm, LANES), lambda i: (0, nt - 1 - i, 0))
    return pl.pallas_call(
        body, name="bwd_in", grid=(nt,),
        in_specs=[rv(2 * D_A), heads, heads, heads, rv(LANES), rv(D_MODEL), rv(D_MODEL),
                  _full((D_MODEL, D_IN_PAD), True), _full((1, D_MODEL))],
        out_specs=[rv(D_MODEL), rv(D_IN_PAD), _full((1, D_MODEL)), _full((1, LANES))],
        out_shape=[jax.ShapeDtypeStruct((T, D_MODEL), F32), jax.ShapeDtypeStruct((T, D_IN_PAD), BF16),
                   jax.ShapeDtypeStruct((1, D_MODEL), F32), jax.ShapeDtypeStruct((1, LANES), F32)],
        scratch_shapes=[pltpu.VMEM((1, LANES), F32)],
        compiler_params=_cp("arbitrary"),
    )(dzuv, dqa, dka, dva, fl, x, dx2, w_in_p, g1)


def _matmul_tn(a_parts, b, tmm, tn, tk, name, shard_cols=None, n_valid=None):
    T = b.shape[0]
    widths = [a.shape[1] for a in a_parts]
    M, N = sum(widths), b.shape[1]
    tk = min(tk, T)
    nk = T // tk
    part_w = tmm // len(a_parts)
    n_valid = N if n_valid is None else n_valid

    def body(*refs):
        a_refs, b_ref, o_ref, obf_ref = refs[:len(a_parts)], refs[-3], refs[-2], refs[-1]
        k = pl.program_id(2)

        @pl.when(k == 0)
        def _():
            o_ref[...] = jnp.zeros_like(o_ref)

        a = [r[...].astype(BF16) for r in a_refs]
        o_ref[...] += _dot_tn(a[0] if len(a) == 1 else jnp.concatenate(a, axis=-1), b_ref[...].astype(BF16))

        @pl.when(k == nk - 1)
        def _():
            if shard_cols is None:
                obf_ref[...] = o_ref[...].astype(BF16)
            else:
                for d in range(min(tn, n_valid) // shard_cols):
                    obf_ref[d] = o_ref[:, d * shard_cols:(d + 1) * shard_cols].astype(BF16)

    if shard_cols is None:
        bf_spec, bf_shape = pl.BlockSpec((tmm, tn), lambda i, j, k: (i, j)), (M, N)
    else:
        per_tile = min(tn, n_valid) // shard_cols
        bf_spec, bf_shape = pl.BlockSpec((per_tile, tmm, shard_cols), lambda i, j, k: (j, i, 0)), (N_DEV, M, shard_cols)
    a_specs = [pl.BlockSpec((tk, part_w), lambda i, j, k: (k, i)) for _ in a_parts]
    return pl.pallas_call(
        body, name=name, grid=(M // tmm, N // tn, nk),
        in_specs=[*a_specs, pl.BlockSpec((tk, tn), lambda i, j, k: (k, j))],
        out_specs=[pl.BlockSpec((tmm, tn), lambda i, j, k: (i, j)), bf_spec],
        out_shape=[jax.ShapeDtypeStruct((M, N), F32), jax.ShapeDtypeStruct(bf_shape, BF16)],
        compiler_params=_cp("arbitrary", "arbitrary", "arbitrary"),
    )(*a_parts, b)


def _dw_in_half(h1, dz, block, side, name):
    T = dz.shape[0]
    tk = min(TK_DW, T)
    nk = T // tk
    rows, shard, n = D_MODEL // 2, D_IN // N_DEV, side.n

    def body(a_ref, b_ref, *rest):
        ins, obf_ref, outs = rest[:n], rest[n], rest[n + 1:2 * n + 1]
        acc, *sems = rest[2 * n + 1:]
        k = pl.program_id(0)

        @pl.when(k == 0)
        def _():
            acc[...] = jnp.zeros_like(acc)
            if n:
                side.start(ins, outs, sems)

        acc[...] += _dot_tn(a_ref[...], b_ref[...])

        @pl.when(k == nk - 1)
        def _():
            for d in range(N_DEV):
                obf_ref[d] = acc[:, d * shard:(d + 1) * shard].astype(BF16)
            if n:
                side.finish(ins, outs, sems)

    res = pl.pallas_call(
        body, name=name, grid=(nk,),
        in_specs=[pl.BlockSpec((tk, rows), lambda k: (k, block)), pl.BlockSpec((tk, D_IN_PAD), lambda k: (k, 0)), *side.specs],
        out_specs=[pl.BlockSpec((N_DEV, rows, shard), lambda k: (0, 0, 0)), *side.specs],
        out_shape=[jax.ShapeDtypeStruct((N_DEV, rows, shard), BF16), *side.out_shape],
        scratch_shapes=[pltpu.VMEM((rows, D_IN_PAD), F32), *side.scratch],
        compiler_params=_cp("arbitrary"),
    )(h1, dz, *side.operands)
    return res[0], res[1:]


class _Exchange:
    def __init__(self, gather, scatter, relay):
        self.n_g, self.n, self.relay = len(gather), len(gather) + len(scatter), relay
        self.operands = [*gather, *scatter]
        self.out_shape = [jax.ShapeDtypeStruct((N_DEV, *g.shape), g.dtype) for g in gather]
        self.out_shape += [jax.ShapeDtypeStruct(s.shape, s.dtype) for s in scatter]
        self.specs = [ANY] * self.n
        n = self.n
        self.scratch = [pltpu.SemaphoreType.DMA((7 * n,)), pltpu.SemaphoreType.DMA((7 * n,)),
                        pltpu.SemaphoreType.DMA((n,))] if n else []

    def _plan(self, ins, outs, sems):
        send_sems, recv_sems, local_sems = sems
        x, y, c = (lax.axis_index(ax) for ax in MESH_AXES)
        me = 4 * x + 2 * y + c
        sibling = (x, y, 1 - c)
        chips = [(1 - x, y), (x, 1 - y), (1 - x, 1 - y)]
        peers = [sibling] + [(*chip, c) for chip in chips] + [(*chip, 1 - c) for chip in chips]

        def index(dev):
            return 4 * dev[0] + 2 * dev[1] + dev[2]

        def remote(k, src, dst, to):
            return pltpu.make_async_remote_copy(src_ref=src, dst_ref=dst, send_sem=send_sems.at[k], recv_sem=recv_sems.at[k],
                                                device_id=to, device_id_type=pl.DeviceIdType.MESH)

        local, sends, relays, recvs = [], [], [], []
        for a in range(self.n):
            src, out, base = ins[a], outs[a], 7 * a
            if a >= self.n_g:
                local.append(pltpu.make_async_copy(src.at[me], out.at[me], local_sems.at[a]))
                sends += [remote(base + k, src.at[index(peer)], out.at[me], peer) for k, peer in enumerate(peers)]
            else:
                local.append(pltpu.make_async_copy(src, out.at[me], local_sems.at[a]))
                sends += [remote(base + k, src, out.at[me], peer) for k, peer in enumerate(peers[:4 if self.relay else 7])]
            for k, peer in enumerate(peers):
                slot = out.at[index(peer)]
                if a < self.n_g and self.relay and k >= 4:
                    continue
                recv = remote(base + k, slot, slot, peer)
                if a < self.n_g and self.relay and k >= 1:
                    relays.append((recv, remote(base + 3 + k, slot, slot, sibling)))
                else:
                    recvs.append(recv)
            if a < self.n_g and self.relay:
                for j, chip in enumerate(chips):
                    slot = out.at[index((*chip, 1 - c))]
                    recvs.append(remote(base + 4 + j, slot, slot, sibling))
        return local, sends, relays, recvs

    def start(self, ins, outs, sems):
        local, sends, _, _ = self._plan(ins, outs, sems)
        for cp in local + sends:
            cp.start()

    def finish(self, ins, outs, sems):
        local, sends, relays, recvs = self._plan(ins, outs, sems)
        for recv, fwd in relays:
            recv.wait_recv()
            fwd.start()
        for recv in recvs:
            recv.wait_recv()
        for cp in sends + [fwd for _, fwd in relays]:
            cp.wait_send()
        for cp in local:
            cp.wait()


def _exchange(gather, scatter, name):
    ex = _Exchange(gather, scatter, relay=True)
    n = ex.n

    def body(*refs):
        ins, outs, sems = refs[:n], refs[n:2 * n], refs[2 * n:]
        ex.start(ins, outs, sems)
        ex.finish(ins, outs, sems)

    return pl.pallas_call(body, name=name, in_specs=ex.specs, out_specs=ex.specs, out_shape=ex.out_shape,
                          scratch_shapes=ex.scratch)(*ex.operands)


def _adamw(w, g, m, v):
    m = ADAM_B1 * m + (1.0 - ADAM_B1) * g
    v = ADAM_B2 * v + (1.0 - ADAM_B2) * jnp.square(g)
    m_hat = m / (1.0 - ADAM_B1 ** ADAM_STEP)
    v_hat = v / (1.0 - ADAM_B2 ** ADAM_STEP)
    delta = -ADAM_LR * (m_hat / (jnp.sqrt(v_hat) + ADAM_EPS) + ADAM_WD * w)
    return delta, m, v


def _adamw_shard(w, m, v, recv, tr, name):
    _, R, C = w.shape

    def body(w_ref, m_ref, v_ref, recv_ref, g_ref, d_ref, nm_ref, nv_ref):
        g = recv_ref[0].astype(F32)
        for d in range(1, N_DEV):
            g = g + recv_ref[d].astype(F32)
        g_ref[...] = g
        d_ref[...], nm_ref[...], nv_ref[...] = _adamw(w_ref[...], g, m_ref[...], v_ref[...])

    blk = pl.BlockSpec((None, tr, C), lambda i: (0, i, 0))
    return pl.pallas_call(
        body, name=name, grid=(R // tr,),
        in_specs=[blk, blk, blk, pl.BlockSpec((N_DEV, tr, C), lambda i: (0, i, 0))],
        out_specs=[blk] * 4, out_shape=[jax.ShapeDtypeStruct((1, R, C), F32)] * 4,
        compiler_params=_cp("arbitrary"),
    )(w, m, v, recv)


def _adamw_small(params, gathered, loss_parts):
    n = len(params)

    def body(*refs):
        ins, gs, loss_ref, outs = refs[:3 * n], refs[3 * n:4 * n], refs[4 * n], refs[4 * n + 1:]
        for p in range(n):
            w_ref, m_ref, v_ref = ins[3 * p:3 * p + 3]
            g = gs[p][0]
            for d in range(1, N_DEV):
                g = g + gs[p][d]
            g = g[..., :w_ref.shape[-1]]
            g_ref, d_ref, nm_ref, nv_ref = outs[4 * p:4 * p + 4]
            g_ref[...] = g
            d_ref[...], nm_ref[...], nv_ref[...] = _adamw(w_ref[...], g, m_ref[...], v_ref[...])
        total = loss_ref[0]
        for d in range(1, N_DEV):
            total = total + loss_ref[d]
        outs[4 * n][...] = total

    out_shape = [jax.ShapeDtypeStruct(w.shape, F32) for w, _, _ in params for _ in range(4)]
    res = pl.pallas_call(body, name="adamw_small", out_shape=[*out_shape, jax.ShapeDtypeStruct((8, LANES), F32)],
                         compiler_params=pltpu.CompilerParams(vmem_limit_bytes=VMEM_LIMIT))(
        *[t for p in params for t in p], *gathered, loss_parts)
    return [res[4 * p:4 * p + 4] for p in range(n)], res[4 * n][0, 0]


def _col_shards(g):
    return jnp.transpose(g.reshape(g.shape[0], N_DEV, -1), (1, 0, 2))


def _row_shards(g):
    return g.reshape(N_DEV, -1, g.shape[1])


def _cols_whole(g):
    return jnp.transpose(g, (1, 0, 2)).reshape(g.shape[1], -1)


def _join_cols(g, width, name):
    n_shards, rows, c = g.shape
    tr = min(256, rows)

    def body(g_ref, o_ref):
        for d in range(n_shards):
            o_ref[:, d * c:(d + 1) * c] = g_ref[d]
        if width > n_shards * c:
            o_ref[:, n_shards * c:] = jnp.zeros((tr, width - n_shards * c), o_ref.dtype)

    return pl.pallas_call(
        body, name=name, grid=(rows // tr,),
        in_specs=[pl.BlockSpec((n_shards, tr, c), lambda i: (0, i, 0))],
        out_specs=pl.BlockSpec((tr, width), lambda i: (i, 0)),
        out_shape=jax.ShapeDtypeStruct((rows, width), g.dtype),
        compiler_params=_cp("arbitrary"),
    )(g)


def kernel(x, norm_mix_g, w_in, f_bias, sg_ln_g, sg_w, sg_b, w_out, norm_ffn_g, w_up, w_conv, b_conv, w_down, norm_final_g, loss_target, m_norm_mix_g, m_w_in, m_f_bias, m_sg_ln_g, m_sg_w, m_sg_b, m_w_out, m_norm_ffn_g, m_w_up, m_w_conv, m_b_conv, m_w_down, m_norm_final_g, v_norm_mix_g, v_w_in, v_f_bias, v_sg_ln_g, v_sg_w, v_sg_b, v_w_out, v_norm_ffn_g, v_w_up, v_w_conv, v_b_conv, v_w_down, v_norm_final_g):
    xs, tgt = x[0], loss_target[0]
    g1, g2, g3 = norm_mix_g, norm_ffn_g, norm_final_g.reshape(1, D_MODEL)
    lg = sg_ln_g.reshape(1, D_A)
    fb = jnp.pad(f_bias, ((0, 0), (0, LANES - N_HEADS)))
    pos_chunk = np.arange(SG_BLOCK) // SG_CHUNK
    maskf = jnp.asarray(pos_chunk[:, None] >= pos_chunk[None, :], F32)
    wm = (sg_w[0] * maskf[None]).astype(BF16)
    wmt = jnp.swapaxes(wm, 1, 2)
    bfull = jnp.repeat(sg_b[0].T, HEAD_DIM, axis=1)
    same_head = np.arange(D_A)[:, None] // HEAD_DIM == np.arange(D_A)[None, :] // HEAD_DIM
    hsel = jnp.asarray(np.arange(N_HEADS)[:, None] == np.arange(D_A)[None, :] // HEAD_DIM, F32)
    avg = jnp.asarray(same_head * (1.0 / HEAD_DIM), BF16)

    (win_g,) = _exchange([w_in[0].astype(BF16)], [], "gather_w_in")
    w_in_p = _join_cols(win_g, D_IN_PAD, "join_w_in")
    zuv, qa, ka, va, fl, h1 = _fwd_in(xs, g1, w_in_p, fb)
    oa = _gmlp_fwd(zuv, lg, wm, bfull, avg)
    rest = _Exchange([w_out[0].astype(BF16), w_up[0].astype(BF16), w_down[0].astype(BF16), w_conv[0]], [], relay=False)
    ob, lse, (wout_g, wup_g, wdown_g, wc_g) = _fox_fwd(qa, ka, va, rest)
    w_out_f, w_up_f = wout_g.reshape(D_MODEL, D_MODEL), _join_cols(wup_g, D_FF2, "join_w_up")
    w_down_f, wc_f = wdown_g.reshape(D_FF, D_MODEL), _cols_whole(wc_g)

    x2, h2, a = _fwd_mid(xs, oa, ob, w_out_f, g2, w_up_f)
    ac, yff, dx3, loss, dg3 = _fwd_ffn(a, x2, wc_f, b_conv, w_down_f, g3, tgt)
    da_g, da_v, dwc_g, dwc_v, dbc_g, dbc_v, dwdown_bf, dwup_bf = _bwd_ffn(dx3, a, ac, yff, h2, w_down_f, wc_f)
    dwc, dbc = jnp.concatenate([dwc_g, dwc_v], axis=1), jnp.concatenate([dbc_g, dbc_v], axis=1)
    dx2, doa, dob, dg2 = _bwd_mid(da_g, da_v, w_up_f, x2, g2, dx3, w_out_f, ob)
    dzuv, dwm, dsgb, dlg = _gmlp_bwd(zuv, doa, lg, wm, wmt, bfull, maskf, hsel, avg)
    _, dwout_bf = _matmul_tn([oa, ob], dx2, D_MODEL, D_MODEL, TK_DW, "dw_out")

    early = ("w_out", "w_up", "wc", "w_down")
    wire = [_row_shards(dwout_bf), dwup_bf, _col_shards(dwc).astype(BF16), dwdown_bf]
    small_early = dict(lg=dlg, sg_w=dwm, sg_b=dsgb, g2=dg2, bc=dbc, g3=dg3)
    grads = _Exchange([*small_early.values(), loss], wire, relay=False)
    dqa, dka, dva, got = _fox_bwd(qa, lse, dob, ka, va, grads)
    n_small = len(small_early)
    gathered, loss_parts = dict(zip(small_early, got[:n_small])), got[n_small]
    recv = dict(zip(early, got[n_small + 1:]))

    gx, dz, dg1, dfb = _bwd_in(dzuv, dqa, dka, dva, fl, xs, dx2, w_in_p, g1)
    wire_a, _ = _dw_in_half(h1, dz, 0, _Exchange([], [], relay=False), "dw_in_a")
    wire_b, (recv_a,) = _dw_in_half(h1, dz, 1, _Exchange([], [wire_a], relay=False), "dw_in_b")
    gathered["g1"], gathered["fb"], recv_b = _exchange([dg1, dfb], [wire_b], "exchange_w_in")
    recv["w_in"] = jnp.concatenate([recv_a, recv_b], axis=1)

    weights = dict(w_in=(w_in, m_w_in, v_w_in, 256), w_out=(w_out, m_w_out, v_w_out, 128), w_up=(w_up, m_w_up, v_w_up, 256),
                   wc=(w_conv, m_w_conv, v_w_conv, 3), w_down=(w_down, m_w_down, v_w_down, 176))
    res = {n: _adamw_shard(w, m, v, recv[n], tr, "adamw_" + n) for n, (w, m, v, tr) in weights.items()}

    reps = dict(g1=((norm_mix_g, m_norm_mix_g, v_norm_mix_g), (1, D_MODEL)), fb=((f_bias, m_f_bias, v_f_bias), (1, N_HEADS)),
                lg=((sg_ln_g, m_sg_ln_g, v_sg_ln_g), (1, D_A)), sg_w=((sg_w, m_sg_w, v_sg_w), (N_HEADS, SG_BLOCK, SG_BLOCK)),
                sg_b=((sg_b, m_sg_b, v_sg_b), (N_HEADS, SG_BLOCK)), g2=((norm_ffn_g, m_norm_ffn_g, v_norm_ffn_g), (1, D_MODEL)),
                bc=((b_conv, m_b_conv, v_b_conv), (1, D_FF2)), g3=((norm_final_g, m_norm_final_g, v_norm_final_g), (1, D_MODEL)))
    outs, loss_sum = _adamw_small([tuple(t.reshape(shape) for t in wmv) for wmv, shape in reps.values()],
                                  [gathered[n] for n in reps], loss_parts)
    for (n, (wmv, _)), out in zip(reps.items(), outs):
        res[n] = [o.reshape(wmv[0].shape) for o in out]

    names = ("g1", "w_in", "fb", "lg", "sg_w", "sg_b", "w_out", "g2", "w_up", "wc", "bc", "w_down", "g3")
    return (loss_sum, gx[None], *[res[n][0] for n in names], *[res[n][1] for n in names],
            *[res[n][2] for n in names], *[res[n][3] for n in names])
```
